```python
import jax, jax.numpy as jnp
from jax import lax
import numpy as np

D_MODEL = 1024
BATCH = 32
SEQ = 2048
DEPTH = 4

HEAD_DIM = 64
D_MIX = D_MODEL
D_ATTN = D_MIX // 2
D_POOL = D_MIX // 4
D_CONV = D_MIX // 4
N_ATTN_HEADS = D_ATTN // HEAD_DIM
POOL_WINDOWS = (2, 4, 8, 16)
N_POOL_GROUPS = len(POOL_WINDOWS)
POOL_GROUP_DIM = D_POOL // N_POOL_GROUPS
N_CONV_HEADS = D_CONV // HEAD_DIM
CONV_WIDTH = 3
D_FF = 2816
Q_BLOCK = 128
RMS_EPS = 1e-6
IN_SPLITS = (D_ATTN, D_ATTN, D_ATTN, N_ATTN_HEADS, D_POOL, D_CONV, D_CONV, D_CONV)
D_IN = sum(IN_SPLITS)

kernel_name = "hymba_style_fox_pool_shortconv_macaron"


def _rmsnorm(x, g):
    x32 = x.astype(jnp.float32)
    y = x32 * lax.rsqrt(jnp.mean(x32 * x32, axis=-1, keepdims=True) + RMS_EPS)
    return (y * g.astype(jnp.float32)).astype(x.dtype)


def _swiglu(h, w_in, w_out):
    gate, up = jnp.split(h @ w_in, 2, axis=-1)
    return (jax.nn.silu(gate) * up) @ w_out


def _fox_attention(q, k, v, f_logit, b_forget):
    b, s, _ = q.shape
    qh = q.reshape(b, s, N_ATTN_HEADS, HEAD_DIM).transpose(0, 2, 1, 3)
    kh = k.reshape(b, s, N_ATTN_HEADS, HEAD_DIM).transpose(0, 2, 1, 3)
    vh = v.reshape(b, s, N_ATTN_HEADS, HEAD_DIM).transpose(0, 2, 1, 3)
    log_f = jax.nn.log_sigmoid(f_logit.astype(jnp.float32) + b_forget.astype(jnp.float32))
    dcum = lax.cumsum(log_f, axis=1).transpose(0, 2, 1)
    scale = HEAD_DIM ** -0.5
    outs = []
    for i in range(s // Q_BLOCK):
        q0 = i * Q_BLOCK
        end = q0 + Q_BLOCK
        sc = jnp.einsum('bhqd,bhkd->bhqk', qh[:, :, q0:end], kh[:, :, :end],
                        preferred_element_type=jnp.float32) * scale
        sc = sc + dcum[:, :, q0:end, None] - dcum[:, :, None, :end]
        mask = jnp.arange(end)[None, :] <= (q0 + jnp.arange(Q_BLOCK))[:, None]
        sc = jnp.where(mask, sc, -jnp.inf)
        p = jax.nn.softmax(sc, axis=-1)
        outs.append(jnp.einsum('bhqk,bhkd->bhqd', p.astype(vh.dtype), vh[:, :, :end]))
    o = jnp.concatenate(outs, axis=2)
    return o.transpose(0, 2, 1, 3).reshape(b, s, D_ATTN)


def _multiscale_pool(u, w_pool, pool_scale):
    b, s, _ = u.shape
    groups = jnp.split(u, N_POOL_GROUPS, axis=-1)
    pos = jnp.arange(s)
    outs = []
    for gi, w in enumerate(POOL_WINDOWS):
        ug = groups[gi]
        cs = lax.cumsum(ug.astype(jnp.float32), axis=1)
        cs_pad = jnp.pad(cs, ((0, 0), (w, 0), (0, 0)))
        win_sum = cs - cs_pad[:, :s]
        count = jnp.minimum(pos + 1, w).astype(jnp.float32)[None, :, None]
        pooled = (win_sum / count).astype(ug.dtype) - ug
        outs.append(jnp.einsum('bsc,cd->bsd', pooled, w_pool[gi]))
    return jnp.concatenate(outs, axis=-1) * pool_scale


def _short_conv(b_gate, c_gate, h, conv_w):
    u = c_gate * h
    rhs = conv_w.reshape(CONV_WIDTH, 1, D_CONV).astype(u.dtype)
    y = lax.conv_general_dilated(u, rhs, window_strides=(1,), padding=((CONV_WIDTH - 1, 0),),
                                 dimension_numbers=('NWC', 'WIO', 'NWC'),
                                 feature_group_count=D_CONV)
    return b_gate * y


def _hybrid_mixer(xn, w_in, b_forget, w_pool, pool_scale, conv_w, w_out):
    proj = xn @ w_in
    idx = list(np.cumsum(IN_SPLITS)[:-1])
    q, k, v, f_logit, pool_in, cb, cc, ch = jnp.split(proj, idx, axis=-1)
    y_attn = _fox_attention(q, k, v, f_logit, b_forget)
    y_pool = _multiscale_pool(pool_in, w_pool, pool_scale)
    y_conv = _short_conv(cb, cc, ch, conv_w)
    return jnp.concatenate([y_attn, y_pool, y_conv], axis=-1) @ w_out


def _fwd_setup_inputs(seed: int = 0) -> dict:
    key = jax.random.key(seed)
    ks = jax.random.split(key, 16)
    f32 = jnp.float32

    def nrm(k, shape, fan_in):
        return jax.random.normal(k, shape, f32) * (fan_in ** -0.5)

    def gain(k, shape):
        return 1.0 + 0.1 * jax.random.normal(k, shape, f32)

    return {
        "x": jax.random.normal(ks[0], (BATCH, SEQ, D_MODEL), f32),
        "norm_ffn1": gain(ks[1], (DEPTH, D_MODEL)),
        "w_ffn1_in": nrm(ks[2], (DEPTH, D_MODEL, 2 * D_FF), D_MODEL),
        "w_ffn1_out": nrm(ks[3], (DEPTH, D_FF, D_MODEL), D_FF),
        "norm_mix": gain(ks[4], (DEPTH, D_MODEL)),
        "w_mix_in": nrm(ks[5], (DEPTH, D_MODEL, D_IN), D_MODEL),
        "b_forget": 3.0 + 0.5 * jax.random.normal(ks[6], (DEPTH, N_ATTN_HEADS), f32),
        "w_pool": nrm(ks[7], (DEPTH, N_POOL_GROUPS, POOL_GROUP_DIM, POOL_GROUP_DIM), POOL_GROUP_DIM),
        "pool_scale": gain(ks[8], (DEPTH, D_POOL)),
        "conv_w": nrm(ks[9], (DEPTH, CONV_WIDTH, D_CONV), CONV_WIDTH),
        "w_mix_out": nrm(ks[10], (DEPTH, D_MIX, D_MODEL), D_MIX),
        "norm_ffn2": gain(ks[11], (DEPTH, D_MODEL)),
        "w_ffn2_in": nrm(ks[12], (DEPTH, D_MODEL, 2 * D_FF), D_MODEL),
        "w_ffn2_out": nrm(ks[13], (DEPTH, D_FF, D_MODEL), D_FF),
        "norm_final": gain(ks[14], (D_MODEL,)),
    }


def _fwd_reference(x, norm_ffn1, w_ffn1_in, w_ffn1_out, norm_mix, w_mix_in, b_forget, w_pool,
              pool_scale, conv_w, w_mix_out, norm_ffn2, w_ffn2_in, w_ffn2_out, norm_final):
    for l in range(DEPTH):
        x = x + 0.5 * _swiglu(_rmsnorm(x, norm_ffn1[l]), w_ffn1_in[l], w_ffn1_out[l])
        x = x + _hybrid_mixer(_rmsnorm(x, norm_mix[l]), w_mix_in[l], b_forget[l], w_pool[l],
                              pool_scale[l], conv_w[l], w_mix_out[l])
        x = x + 0.5 * _swiglu(_rmsnorm(x, norm_ffn2[l]), w_ffn2_in[l], w_ffn2_out[l])
    return _rmsnorm(x, norm_final)


import jax as _jax
import jax.numpy as _jnp

TWIN_FORMAT = 'train_step'
FWD_PARAMS = ['x', 'norm_ffn1', 'w_ffn1_in', 'w_ffn1_out', 'norm_mix', 'w_mix_in', 'b_forget', 'w_pool', 'pool_scale', 'conv_w', 'w_mix_out', 'norm_ffn2', 'w_ffn2_in', 'w_ffn2_out', 'norm_final']
TWIN_WEIGHTS = ['norm_ffn1', 'w_ffn1_in', 'w_ffn1_out', 'norm_mix', 'w_mix_in', 'b_forget', 'w_pool', 'pool_scale', 'conv_w', 'w_mix_out', 'norm_ffn2', 'w_ffn2_in', 'w_ffn2_out', 'norm_final']
TWIN_DIFF_INPUT = 'x'
TWIN_INPUTS = ['x', 'norm_ffn1', 'w_ffn1_in', 'w_ffn1_out', 'norm_mix', 'w_mix_in', 'b_forget', 'w_pool', 'pool_scale', 'conv_w', 'w_mix_out', 'norm_ffn2', 'w_ffn2_in', 'w_ffn2_out', 'norm_final', 'loss_target', 'm_norm_ffn1', 'm_w_ffn1_in', 'm_w_ffn1_out', 'm_norm_mix', 'm_w_mix_in', 'm_b_forget', 'm_w_pool', 'm_pool_scale', 'm_conv_w', 'm_w_mix_out', 'm_norm_ffn2', 'm_w_ffn2_in', 'm_w_ffn2_out', 'm_norm_final', 'v_norm_ffn1', 'v_w_ffn1_in', 'v_w_ffn1_out', 'v_norm_mix', 'v_w_mix_in', 'v_b_forget', 'v_w_pool', 'v_pool_scale', 'v_conv_w', 'v_w_mix_out', 'v_norm_ffn2', 'v_w_ffn2_in', 'v_w_ffn2_out', 'v_norm_final']
TWIN_OUTPUTS = ['loss', 'grad_x', 'grad_norm_ffn1', 'grad_w_ffn1_in', 'grad_w_ffn1_out', 'grad_norm_mix', 'grad_w_mix_in', 'grad_b_forget', 'grad_w_pool', 'grad_pool_scale', 'grad_conv_w', 'grad_w_mix_out', 'grad_norm_ffn2', 'grad_w_ffn2_in', 'grad_w_ffn2_out', 'grad_norm_final', 'delta_norm_ffn1', 'delta_w_ffn1_in', 'delta_w_ffn1_out', 'delta_norm_mix', 'delta_w_mix_in', 'delta_b_forget', 'delta_w_pool', 'delta_pool_scale', 'delta_conv_w', 'delta_w_mix_out', 'delta_norm_ffn2', 'delta_w_ffn2_in', 'delta_w_ffn2_out', 'delta_norm_final', 'new_m_norm_ffn1', 'new_m_w_ffn1_in', 'new_m_w_ffn1_out', 'new_m_norm_mix', 'new_m_w_mix_in', 'new_m_b_forget', 'new_m_w_pool', 'new_m_pool_scale', 'new_m_conv_w', 'new_m_w_mix_out', 'new_m_norm_ffn2', 'new_m_w_ffn2_in', 'new_m_w_ffn2_out', 'new_m_norm_final', 'new_v_norm_ffn1', 'new_v_w_ffn1_in', 'new_v_w_ffn1_out', 'new_v_norm_mix', 'new_v_w_mix_in', 'new_v_b_forget', 'new_v_w_pool', 'new_v_pool_scale', 'new_v_conv_w', 'new_v_w_mix_out', 'new_v_norm_ffn2', 'new_v_w_ffn2_in', 'new_v_w_ffn2_out', 'new_v_norm_final']
TWIN_LEAF_KINDS = {'loss': 'loss', 'grad_x': 'grad_x', 'grad_norm_ffn1': 'grad_w', 'grad_w_ffn1_in': 'grad_w', 'grad_w_ffn1_out': 'grad_w', 'grad_norm_mix': 'grad_w', 'grad_w_mix_in': 'grad_w', 'grad_b_forget': 'grad_w', 'grad_w_pool': 'grad_w', 'grad_pool_scale': 'grad_w', 'grad_conv_w': 'grad_w', 'grad_w_mix_out': 'grad_w', 'grad_norm_ffn2': 'grad_w', 'grad_w_ffn2_in': 'grad_w', 'grad_w_ffn2_out': 'grad_w', 'grad_norm_final': 'grad_w', 'delta_norm_ffn1': 'delta_w', 'delta_w_ffn1_in': 'delta_w', 'delta_w_ffn1_out': 'delta_w', 'delta_norm_mix': 'delta_w', 'delta_w_mix_in': 'delta_w', 'delta_b_forget': 'delta_w', 'delta_w_pool': 'delta_w', 'delta_pool_scale': 'delta_w', 'delta_conv_w': 'delta_w', 'delta_w_mix_out': 'delta_w', 'delta_norm_ffn2': 'delta_w', 'delta_w_ffn2_in': 'delta_w', 'delta_w_ffn2_out': 'delta_w', 'delta_norm_final': 'delta_w', 'new_m_norm_ffn1': 'new_m', 'new_m_w_ffn1_in': 'new_m', 'new_m_w_ffn1_out': 'new_m', 'new_m_norm_mix': 'new_m', 'new_m_w_mix_in': 'new_m', 'new_m_b_forget': 'new_m', 'new_m_w_pool': 'new_m', 'new_m_pool_scale': 'new_m', 'new_m_conv_w': 'new_m', 'new_m_w_mix_out': 'new_m', 'new_m_norm_ffn2': 'new_m', 'new_m_w_ffn2_in': 'new_m', 'new_m_w_ffn2_out': 'new_m', 'new_m_norm_final': 'new_m', 'new_v_norm_ffn1': 'new_v', 'new_v_w_ffn1_in': 'new_v', 'new_v_w_ffn1_out': 'new_v', 'new_v_norm_mix': 'new_v', 'new_v_w_mix_in': 'new_v', 'new_v_b_forget': 'new_v', 'new_v_w_pool': 'new_v', 'new_v_pool_scale': 'new_v', 'new_v_conv_w': 'new_v', 'new_v_w_mix_out': 'new_v', 'new_v_norm_ffn2': 'new_v', 'new_v_w_ffn2_in': 'new_v', 'new_v_w_ffn2_out': 'new_v', 'new_v_norm_final': 'new_v'}


def _forward(args):
    return _fwd_reference(*[args[k] for k in FWD_PARAMS])


def _output_shape():
    out = _jax.eval_shape(lambda: _forward(_fwd_setup_inputs(0)))
    return out.shape, out.dtype

N_MICROBATCH = 1
ADAM_LR = 0.001
ADAM_B1 = 0.9
ADAM_B2 = 0.999
ADAM_EPS = 1e-08
ADAM_WD = 0.01
ADAM_STEP = 10
PER_EXAMPLE_BATCH_AXIS = {'x': 0, 'loss_target': 0}
SHARED_INPUTS = []
_WEIGHT_DTYPES = {'norm_ffn1': _jnp.float32, 'w_ffn1_in': _jnp.float32, 'w_ffn1_out': _jnp.float32, 'norm_mix': _jnp.float32, 'w_mix_in': _jnp.float32, 'b_forget': _jnp.float32, 'w_pool': _jnp.float32, 'pool_scale': _jnp.float32, 'conv_w': _jnp.float32, 'w_mix_out': _jnp.float32, 'norm_ffn2': _jnp.float32, 'w_ffn2_in': _jnp.float32, 'w_ffn2_out': _jnp.float32, 'norm_final': _jnp.float32}
MOMENT_SCALE = {'norm_ffn1': 1.262196e-01, 'w_ffn1_in': 5.378319e-02, 'w_ffn1_out': 8.805975e-02, 'norm_mix': 2.329958e-01, 'w_mix_in': 1.442972e-01, 'b_forget': 2.781671e-01, 'w_pool': 2.145941e-01, 'pool_scale': 2.246565e-01, 'conv_w': 2.343003e-01, 'w_mix_out': 1.623037e-01, 'norm_ffn2': 9.567511e-02, 'w_ffn2_in': 3.944739e-02, 'w_ffn2_out': 6.465829e-02, 'norm_final': 6.439864e+01}


def _to_microbatches(a, axis):
    t = _jnp.moveaxis(a, axis, 0)
    t = t.reshape((N_MICROBATCH, t.shape[0] // N_MICROBATCH) + t.shape[1:])
    return _jnp.moveaxis(t, 1, axis + 1)


def setup_inputs(seed: int = 0) -> dict:
    inp = _fwd_setup_inputs(seed)
    key = _jax.random.fold_in(_jax.random.key(seed), 7919)
    shape, _ = _output_shape()
    out = dict(inp)
    out["loss_target"] = _jax.random.normal(_jax.random.fold_in(key, 0), shape, _jnp.float32)
    for i, name in enumerate(TWIN_WEIGHTS):
        w = inp[name].astype(_jnp.float32)
        if MOMENT_SCALE is None:
            s = _jnp.sqrt(_jnp.mean(_jnp.square(w)) + 1e-30)
        else:
            s = MOMENT_SCALE[name]
        km, kv = _jax.random.split(_jax.random.fold_in(key, i + 1))
        out[name] = w
        out["m_" + name] = s * _jax.random.normal(km, w.shape, _jnp.float32)
        out["v_" + name] = (s * s) * _jax.random.uniform(kv, w.shape, _jnp.float32, 0.5, 1.5)
    if N_MICROBATCH > 1:
        for name, axis in PER_EXAMPLE_BATCH_AXIS.items():
            out[name] = _to_microbatches(out[name], axis)
    return {'x': out['x'], 'norm_ffn1': out['norm_ffn1'], 'w_ffn1_in': out['w_ffn1_in'], 'w_ffn1_out': out['w_ffn1_out'], 'norm_mix': out['norm_mix'], 'w_mix_in': out['w_mix_in'], 'b_forget': out['b_forget'], 'w_pool': out['w_pool'], 'pool_scale': out['pool_scale'], 'conv_w': out['conv_w'], 'w_mix_out': out['w_mix_out'], 'norm_ffn2': out['norm_ffn2'], 'w_ffn2_in': out['w_ffn2_in'], 'w_ffn2_out': out['w_ffn2_out'], 'norm_final': out['norm_final'], 'loss_target': out['loss_target'], 'm_norm_ffn1': out['m_norm_ffn1'], 'm_w_ffn1_in': out['m_w_ffn1_in'], 'm_w_ffn1_out': out['m_w_ffn1_out'], 'm_norm_mix': out['m_norm_mix'], 'm_w_mix_in': out['m_w_mix_in'], 'm_b_forget': out['m_b_forget'], 'm_w_pool': out['m_w_pool'], 'm_pool_scale': out['m_pool_scale'], 'm_conv_w': out['m_conv_w'], 'm_w_mix_out': out['m_w_mix_out'], 'm_norm_ffn2': out['m_norm_ffn2'], 'm_w_ffn2_in': out['m_w_ffn2_in'], 'm_w_ffn2_out': out['m_w_ffn2_out'], 'm_norm_final': out['m_norm_final'], 'v_norm_ffn1': out['v_norm_ffn1'], 'v_w_ffn1_in': out['v_w_ffn1_in'], 'v_w_ffn1_out': out['v_w_ffn1_out'], 'v_norm_mix': out['v_norm_mix'], 'v_w_mix_in': out['v_w_mix_in'], 'v_b_forget': out['v_b_forget'], 'v_w_pool': out['v_w_pool'], 'v_pool_scale': out['v_pool_scale'], 'v_conv_w': out['v_conv_w'], 'v_w_mix_out': out['v_w_mix_out'], 'v_norm_ffn2': out['v_norm_ffn2'], 'v_w_ffn2_in': out['v_w_ffn2_in'], 'v_w_ffn2_out': out['v_w_ffn2_out'], 'v_norm_final': out['v_norm_final']}


def _loss(weights, diff, rest, loss_target):
    with _jax.named_scope("forward"):
        args = {**rest, TWIN_DIFF_INPUT: diff, **{k: w.astype(_WEIGHT_DTYPES[k]) for k, w in weights.items()}}
        y = _forward(args)
    with _jax.named_scope("loss_head"):
        err = _jnp.square(y.astype(_jnp.float32) - loss_target)
        return 0.5 * _jnp.sum(_jnp.mean(err, axis=-1)) if err.ndim else 0.5 * err


def _adamw(w, g, m, v):
    m = ADAM_B1 * m + (1.0 - ADAM_B1) * g
    v = ADAM_B2 * v + (1.0 - ADAM_B2) * _jnp.square(g)
    m_hat = m / (1.0 - ADAM_B1 ** ADAM_STEP)
    v_hat = v / (1.0 - ADAM_B2 ** ADAM_STEP)
    delta = -ADAM_LR * (m_hat / (_jnp.sqrt(v_hat) + ADAM_EPS) + ADAM_WD * w)
    return delta, m, v


def reference(x, norm_ffn1, w_ffn1_in, w_ffn1_out, norm_mix, w_mix_in, b_forget, w_pool, pool_scale, conv_w, w_mix_out, norm_ffn2, w_ffn2_in, w_ffn2_out, norm_final, loss_target, m_norm_ffn1, m_w_ffn1_in, m_w_ffn1_out, m_norm_mix, m_w_mix_in, m_b_forget, m_w_pool, m_pool_scale, m_conv_w, m_w_mix_out, m_norm_ffn2, m_w_ffn2_in, m_w_ffn2_out, m_norm_final, v_norm_ffn1, v_w_ffn1_in, v_w_ffn1_out, v_norm_mix, v_w_mix_in, v_b_forget, v_w_pool, v_pool_scale, v_conv_w, v_w_mix_out, v_norm_ffn2, v_w_ffn2_in, v_w_ffn2_out, v_norm_final):
    given = dict(x=x, norm_ffn1=norm_ffn1, w_ffn1_in=w_ffn1_in, w_ffn1_out=w_ffn1_out, norm_mix=norm_mix, w_mix_in=w_mix_in, b_forget=b_forget, w_pool=w_pool, pool_scale=pool_scale, conv_w=conv_w, w_mix_out=w_mix_out, norm_ffn2=norm_ffn2, w_ffn2_in=w_ffn2_in, w_ffn2_out=w_ffn2_out, norm_final=norm_final, loss_target=loss_target, m_norm_ffn1=m_norm_ffn1, m_w_ffn1_in=m_w_ffn1_in, m_w_ffn1_out=m_w_ffn1_out, m_norm_mix=m_norm_mix, m_w_mix_in=m_w_mix_in, m_b_forget=m_b_forget, m_w_pool=m_w_pool, m_pool_scale=m_pool_scale, m_conv_w=m_conv_w, m_w_mix_out=m_w_mix_out, m_norm_ffn2=m_norm_ffn2, m_w_ffn2_in=m_w_ffn2_in, m_w_ffn2_out=m_w_ffn2_out, m_norm_final=m_norm_final, v_norm_ffn1=v_norm_ffn1, v_w_ffn1_in=v_w_ffn1_in, v_w_ffn1_out=v_w_ffn1_out, v_norm_mix=v_norm_mix, v_w_mix_in=v_w_mix_in, v_b_forget=v_b_forget, v_w_pool=v_w_pool, v_pool_scale=v_pool_scale, v_conv_w=v_conv_w, v_w_mix_out=v_w_mix_out, v_norm_ffn2=v_norm_ffn2, v_w_ffn2_in=v_w_ffn2_in, v_w_ffn2_out=v_w_ffn2_out, v_norm_final=v_norm_final)
    weights = {n: given[n] for n in TWIN_WEIGHTS}
    shared = {n: given[n] for n in SHARED_INPUTS}
    per_example = {n: given[n] for n in ['x']}
    grad_fn = _jax.value_and_grad(_loss, argnums=(0, 1))

    def one_microbatch(ex, loss_target):
        ex = dict(ex)
        diff = ex.pop(TWIN_DIFF_INPUT)
        return grad_fn(weights, diff, {**shared, **ex}, loss_target)

    if N_MICROBATCH == 1:
        loss, (grad_w, grad_x) = one_microbatch(per_example, given["loss_target"])
    else:
        def body(carry, xs):
            loss_sum, grad_sum = carry
            l_k, (gw_k, gx_k) = one_microbatch(xs[0], xs[1])
            with _jax.named_scope("update"):
                return (loss_sum + l_k, _jax.tree.map(_jnp.add, grad_sum, gw_k)), gx_k

        init = (_jnp.zeros((), _jnp.float32), _jax.tree.map(_jnp.zeros_like, weights))
        (loss, grad_w), grad_x = _jax.lax.scan(body, init, (per_example, given["loss_target"]))
    with _jax.named_scope("update"):
        delta_w, new_m, new_v = {}, {}, {}
        for n in TWIN_WEIGHTS:
            delta_w[n], new_m[n], new_v[n] = _adamw(weights[n], grad_w[n], given["m_" + n], given["v_" + n])
    return (loss, grad_x, *[grad_w[n] for n in TWIN_WEIGHTS], *[delta_w[n] for n in TWIN_WEIGHTS],
            *[new_m[n] for n in TWIN_WEIGHTS], *[new_v[n] for n in TWIN_WEIGHTS])
```

```python
import functools

import jax
import jax.numpy as jnp
from jax import lax
from jax.experimental import pallas as pl
from jax.experimental.pallas import tpu as pltpu

F32 = jnp.float32
BF16 = jnp.bfloat16

D_MODEL = 1024
D_FF = 2816
HEAD_DIM = 64
N_HEADS = 8
D_ATTN = 512
D_POOL = 256
D_CONV = 256
POOL_WINDOWS = (2, 4, 8, 16)
POOL_GROUP = 64
D_IN = 2568
RMS_EPS = 1e-6
ADAM_LR, ADAM_B1, ADAM_B2, ADAM_EPS, ADAM_WD, ADAM_STEP = 0.001, 0.9, 0.999, 1e-08, 0.01, 10

N_DEV = 8
LANES = 128
VMEM_BYTES_V7X = 64 * 1024 * 1024
VMEM_LIMIT_MAX = VMEM_BYTES_V7X - 8 * 1024 * 1024

F_HALF = D_FF // 2
D_QKV = 3 * D_ATTN
D_REST = D_POOL + 3 * D_CONV
D_INP = D_QKV + D_REST + LANES
MIX_ROWS = 321
MIX_ROWS_PAD = 336
FFN_ROWS = 704
OUT_ROWS = 352
MO_ROWS = 128
LAYER_ROWS = 2 * (FFN_ROWS + OUT_ROWS) + MIX_ROWS_PAD + MO_ROWS
IL_PERM = (0, 1, 4, 5, 2, 3, 6, 7)
NEG_BIG = -1e30


def _cparams(sem, vmem_bytes):
    limit = int(min(max(vmem_bytes, 16 * 1024 * 1024), VMEM_LIMIT_MAX))
    return pltpu.CompilerParams(dimension_semantics=sem, vmem_limit_bytes=limit)


def _nbytes(shape, dtype):
    n = 1
    for s in shape:
        n *= s
    return n * jnp.dtype(dtype).itemsize


def _pick(n, prefs):
    for p in prefs:
        if n % p == 0:
            return p
    return n


def _rmsnorm_fwd(x, g, name):
    t, d = x.shape
    tm = _pick(t, (512, 256, 128))

    def body(x_ref, g_ref, o_ref):
        xv = x_ref[...]
        r = lax.rsqrt(jnp.mean(xv * xv, axis=-1, keepdims=True) + RMS_EPS)
        o_ref[...] = ((xv * r) * g_ref[...]).astype(o_ref.dtype)

    return pl.pallas_call(
        body, grid=(t // tm,),
        in_specs=[pl.BlockSpec((tm, d), lambda i: (i, 0)), pl.BlockSpec((1, d), lambda i: (0, 0))],
        out_specs=pl.BlockSpec((tm, d), lambda i: (i, 0)),
        out_shape=jax.ShapeDtypeStruct((t, d), BF16), name=name,
        compiler_params=_cparams(("parallel",), 6 * tm * d * 4),
    )(x, g)


def _mm_nn(a, b, *, out_dtype, name, res=None, alpha=1.0, tn=None):
    m, k = a.shape
    n = b.shape[1]
    tn = n if tn is None else tn
    tm = _pick(m, (512, 256, 128))
    with_res = res is not None

    def body(*refs):
        if with_res:
            a_ref, b_ref, r_ref, o_ref = refs
        else:
            a_ref, b_ref, o_ref = refs
        acc = jnp.dot(a_ref[...], b_ref[...], preferred_element_type=F32)
        if with_res:
            acc = r_ref[...] + alpha * acc
        o_ref[...] = acc.astype(o_ref.dtype)

    in_specs = [pl.BlockSpec((tm, k), lambda j, i: (i, 0)), pl.BlockSpec((k, tn), lambda j, i: (0, j))]
    args = [a, b]
    if with_res:
        in_specs.append(pl.BlockSpec((tm, tn), lambda j, i: (i, j)))
        args.append(res)
    vmem = 2 * (_nbytes((tm, k), BF16) + _nbytes((k, tn), BF16) + 3 * _nbytes((tm, tn), F32))
    return pl.pallas_call(
        body, grid=(n // tn, m // tm), in_specs=in_specs,
        out_specs=pl.BlockSpec((tm, tn), lambda j, i: (i, j)),
        out_shape=jax.ShapeDtypeStruct((m, n), out_dtype), name=name,
        compiler_params=_cparams(("parallel", "parallel"), vmem),
    )(*args)


def _mm_tn(a, b, *, name, alpha=1.0, tm=None):
    t, m = a.shape
    n = b.shape[1]
    tm = m if tm is None else tm
    tk = _pick(t, (1024, 512, 256, 128))
    nk = t // tk

    def body(a_ref, b_ref, o_ref):
        kk = pl.program_id(1)
        p = lax.dot_general(a_ref[...], b_ref[...], (((0,), (0,)), ((), ())), preferred_element_type=F32)
        if alpha != 1.0:
            p = alpha * p

        @pl.when(kk == 0)
        def _():
            o_ref[...] = p

        @pl.when(kk > 0)
        def _():
            o_ref[...] += p

    vmem = 2 * (_nbytes((tk, tm), BF16) + _nbytes((tk, n), BF16) + 2 * _nbytes((tm, n), F32))
    return pl.pallas_call(
        body, grid=(m // tm, nk),
        in_specs=[pl.BlockSpec((tk, tm), lambda i, kk: (kk, i)), pl.BlockSpec((tk, n), lambda i, kk: (kk, 0))],
        out_specs=pl.BlockSpec((tm, n), lambda i, kk: (i, 0)),
        out_shape=jax.ShapeDtypeStruct((m, n), F32), name=name,
        compiler_params=_cparams(("parallel", "arbitrary"), vmem),
    )(a, b)


def _sigmoid(v):
    return 1.0 / (1.0 + jnp.exp(-v))


def _ffn_in(xn, w_il, name):
    t, d = xn.shape
    tm = _pick(t, (512, 256, 128))

    def body(x_ref, w_ref, h_ref, z_ref):
        z = jnp.dot(x_ref[...], w_ref[...], preferred_element_type=F32)
        g = z[:, :F_HALF]
        u = z[:, F_HALF:]
        h_ref[...] = ((g * _sigmoid(g)) * u).astype(h_ref.dtype)
        z_ref[...] = z.astype(z_ref.dtype)

    vmem = 2 * (_nbytes((tm, d), BF16) + _nbytes((d, D_FF), BF16) + 4 * _nbytes((tm, D_FF), F32))
    return pl.pallas_call(
        body, grid=(2, t // tm),
        in_specs=[pl.BlockSpec((tm, d), lambda j, i: (i, 0)), pl.BlockSpec((d, D_FF), lambda j, i: (0, j))],
        out_specs=[pl.BlockSpec((tm, F_HALF), lambda j, i: (i, j)), pl.BlockSpec((tm, D_FF), lambda j, i: (i, j))],
        out_shape=[jax.ShapeDtypeStruct((t, D_FF), BF16), jax.ShapeDtypeStruct((t, 2 * D_FF), BF16)], name=name,
        compiler_params=_cparams(("parallel", "parallel"), vmem),
    )(xn, w_il)


def _ffn_bwd_mid(dxo, w_out_t, z, name):
    t, d = dxo.shape
    tm = _pick(t, (512, 256, 128))

    def body(d_ref, w_ref, z_ref, dz_ref):
        dh = 0.5 * jnp.dot(d_ref[...], w_ref[...], preferred_element_type=F32)
        zz = z_ref[...].astype(F32)
        g = zz[:, :F_HALF]
        u = zz[:, F_HALF:]
        s = _sigmoid(g)
        dz_ref[:, :F_HALF] = (dh * u * (s * (1.0 + g * (1.0 - s)))).astype(dz_ref.dtype)
        dz_ref[:, F_HALF:] = (dh * (g * s)).astype(dz_ref.dtype)

    vmem = 2 * (_nbytes((tm, d), BF16) + _nbytes((d, F_HALF), BF16) + 5 * _nbytes((tm, D_FF), F32))
    return pl.pallas_call(
        body, grid=(2, t // tm),
        in_specs=[pl.BlockSpec((tm, d), lambda j, i: (i, 0)), pl.BlockSpec((d, F_HALF), lambda j, i: (0, j)),
                  pl.BlockSpec((tm, D_FF), lambda j, i: (i, j))],
        out_specs=pl.BlockSpec((tm, D_FF), lambda j, i: (i, j)),
        out_shape=jax.ShapeDtypeStruct((t, 2 * D_FF), BF16), name=name,
        compiler_params=_cparams(("parallel", "parallel"), vmem),
    )(dxo, w_out_t, z)


def _rmsnorm_bwd(x, g, dxn, dxo, name):
    t, d = x.shape
    tm = _pick(t, (512, 256, 128))

    def body(x_ref, g_ref, dn_ref, do_ref, dx_ref, dxb_ref, dg_ref):
        i = pl.program_id(0)
        xv = x_ref[...]
        r = lax.rsqrt(jnp.mean(xv * xv, axis=-1, keepdims=True) + RMS_EPS)
        xh = xv * r
        dn = dn_ref[...]
        dgp = jnp.sum(dn * xh, axis=0, keepdims=True)
        dh = dn * g_ref[...]
        dx = do_ref[...] + r * (dh - xh * jnp.mean(dh * xh, axis=-1, keepdims=True))
        dx_ref[...] = dx
        dxb_ref[...] = dx.astype(dxb_ref.dtype)

        @pl.when(i == 0)
        def _():
            dg_ref[...] = dgp

        @pl.when(i > 0)
        def _():
            dg_ref[...] += dgp

    blk = pl.BlockSpec((tm, d), lambda i: (i, 0))
    row = pl.BlockSpec((1, d), lambda i: (0, 0))
    return pl.pallas_call(
        body, grid=(t // tm,), in_specs=[blk, row, blk, blk], out_specs=[blk, blk, row],
        out_shape=[jax.ShapeDtypeStruct((t, d), F32), jax.ShapeDtypeStruct((t, d), BF16),
                   jax.ShapeDtypeStruct((1, d), F32)], name=name,
        compiler_params=_cparams(("arbitrary",), 16 * tm * d * 4),
    )(x, g, dxn, dxo)


def _final_loss_bwd(x, g, tgt):
    t, d = x.shape
    tm = _pick(t, (512, 256, 128))

    def body(x_ref, g_ref, t_ref, dx_ref, dxb_ref, dg_ref, loss_ref):
        i = pl.program_id(0)
        xv = x_ref[...]
        r = lax.rsqrt(jnp.mean(xv * xv, axis=-1, keepdims=True) + RMS_EPS)
        xh = xv * r
        gv = g_ref[...]
        err = xh * gv - t_ref[...]
        lp = 0.5 * jnp.sum(jnp.mean(err * err, axis=-1, keepdims=True), axis=0, keepdims=True)
        dy = err * (1.0 / d)
        dgp = jnp.sum(dy * xh, axis=0, keepdims=True)
        dh = dy * gv
        dx = r * (dh - xh * jnp.mean(dh * xh, axis=-1, keepdims=True))
        dx_ref[...] = dx
        dxb_ref[...] = dx.astype(dxb_ref.dtype)
        lpb = jnp.broadcast_to(lp, (1, LANES))

        @pl.when(i == 0)
        def _():
            dg_ref[...] = dgp
            loss_ref[...] = lpb

        @pl.when(i > 0)
        def _():
            dg_ref[...] += dgp
            loss_ref[...] += lpb

    blk = pl.BlockSpec((tm, d), lambda i: (i, 0))
    row = pl.BlockSpec((1, d), lambda i: (0, 0))
    return pl.pallas_call(
        body, grid=(t // tm,), in_specs=[blk, row, blk],
        out_specs=[blk, blk, row, pl.BlockSpec((1, LANES), lambda i: (0, 0))],
        out_shape=[jax.ShapeDtypeStruct((t, d), F32), jax.ShapeDtypeStruct((t, d), BF16),
                   jax.ShapeDtypeStruct((1, d), F32), jax.ShapeDtypeStruct((1, LANES), F32)], name="final_loss_bwd",
        compiler_params=_cparams(("arbitrary",), 16 * tm * d * 4),
    )(x, g, tgt)


def _seq_scan(v, seq, reverse):
    row = lax.broadcasted_iota(jnp.int32, v.shape, 0)
    k = 1
    while k < seq:
        if reverse:
            v = v + jnp.where(row < seq - k, pltpu.roll(v, seq - k, 0), 0.0)
        else:
            v = v + jnp.where(row >= k, pltpu.roll(v, k, 0), 0.0)
        k *= 2
    return v


def _log_sigmoid(v):
    return jnp.minimum(v, 0.0) - jnp.log(1.0 + jnp.exp(-jnp.abs(v)))


def _fox_prep(fl, bf, seq):
    t = fl.shape[0]

    def body(f_ref, b_ref, o_ref):
        o_ref[...] = _seq_scan(_log_sigmoid(f_ref[...] + b_ref[...]), seq, False)

    blk = pl.BlockSpec((seq, LANES), lambda b: (b, 0))
    return pl.pallas_call(
        body, grid=(t // seq,), in_specs=[blk, pl.BlockSpec((1, LANES), lambda b: (0, 0))], out_specs=blk,
        out_shape=jax.ShapeDtypeStruct((t, LANES), F32), name="fox_prep",
        compiler_params=_cparams(("parallel",), 24 * seq * LANES * 4),
    )(fl, bf)


def _fox_prep_bwd(dd, fl, bf, seq):
    t = fl.shape[0]

    def body(d_ref, f_ref, b_ref, o_ref, db_ref):
        i = pl.program_id(0)
        dlog = _seq_scan(d_ref[...], seq, True)
        dfl = dlog * _sigmoid(-(f_ref[...] + b_ref[...]))
        o_ref[...] = dfl.astype(o_ref.dtype)
        dbp = jnp.sum(dfl, axis=0, keepdims=True)

        @pl.when(i == 0)
        def _():
            db_ref[...] = dbp

        @pl.when(i > 0)
        def _():
            db_ref[...] += dbp

    blk = pl.BlockSpec((seq, LANES), lambda b: (b, 0))
    row = pl.BlockSpec((1, LANES), lambda b: (0, 0))
    return pl.pallas_call(
        body, grid=(t // seq,), in_specs=[blk, blk, row], out_specs=[blk, row],
        out_shape=[jax.ShapeDtypeStruct((t, LANES), BF16), jax.ShapeDtypeStruct((1, LANES), F32)], name="fox_prep_bwd",
        compiler_params=_cparams(("arbitrary",), 24 * seq * LANES * 4),
    )(dd, fl, bf)


def _head_mask(shape, h):
    lane = lax.broadcasted_iota(jnp.int32, shape, 1)
    return (lane >= HEAD_DIM * h) & (lane < HEAD_DIM * (h + 1))


def _fox_fwd(qkv, dcol, drow, nb, seq, tq):
    t = qkv.shape[0]
    nq = seq // tq
    scale = HEAD_DIM ** -0.5
    npair = N_HEADS // 2

    def body(q_ref, k_ref, v_ref, dc_ref, dr_ref, o_ref, lse_ref):
        i = pl.program_id(2)
        q = q_ref[...]
        rowi = i * tq + lax.broadcasted_iota(jnp.int32, (tq, tq), 0)
        coli = lax.broadcasted_iota(jnp.int32, (tq, tq), 1)
        outs = []
        for h in range(2):
            qh = jnp.where(_head_mask(q.shape, h), q, jnp.zeros_like(q))
            dq_col = dc_ref[0, h]

            def kv_step(j, carry, qh=qh, dq_col=dq_col, h=h):
                m, l, acc = carry
                ks = k_ref[pl.ds(pl.multiple_of(j * tq, tq), tq), :]
                vs = v_ref[pl.ds(pl.multiple_of(j * tq, tq), tq), :]
                s = lax.dot_general(qh, ks, (((1,), (1,)), ((), ())), preferred_element_type=F32) * scale
                s = s + dq_col - dr_ref[0, h, j]
                s = jnp.where(j * tq + coli <= rowi, s, NEG_BIG)
                m_new = jnp.maximum(m, jnp.max(s, axis=-1, keepdims=True))
                p = jnp.exp(s - m_new)
                corr = jnp.exp(m - m_new)
                l = corr * l + jnp.sum(p, axis=-1, keepdims=True)
                acc = corr * acc + jnp.dot(p.astype(BF16), vs, preferred_element_type=F32)
                return m_new, l, acc

            init = (jnp.full((tq, 1), NEG_BIG, F32), jnp.zeros((tq, 1), F32), jnp.zeros((tq, LANES), F32))
            m, l, acc = lax.fori_loop(0, i + 1, kv_step, init)
            outs.append(acc / l)
            lse_ref[0, h] = m + jnp.log(l)
        o_ref[...] = jnp.where(_head_mask(outs[0].shape, 0), outs[0], outs[1]).astype(o_ref.dtype)

    vmem = 4 * _nbytes((seq, LANES), BF16) + 16 * tq * tq * 4 + 4 * 1024 * 1024
    return pl.pallas_call(
        body, grid=(nb, npair, nq),
        in_specs=[pl.BlockSpec((tq, LANES), lambda b, p, i: (b * nq + i, p)),
                  pl.BlockSpec((seq, LANES), lambda b, p, i: (b, npair + p)),
                  pl.BlockSpec((seq, LANES), lambda b, p, i: (b, 2 * npair + p)),
                  pl.BlockSpec((1, 2, tq, 1), lambda b, p, i: (b, p, i, 0)),
                  pl.BlockSpec((1, 2, nq, 1, tq), lambda b, p, i: (b, p, 0, 0, 0))],
        out_specs=[pl.BlockSpec((tq, LANES), lambda b, p, i: (b * nq + i, p)),
                   pl.BlockSpec((1, 2, tq, 1), lambda b, p, i: (b, p, i, 0))],
        out_shape=[jax.ShapeDtypeStruct((t, D_ATTN), BF16), jax.ShapeDtypeStruct((nb, N_HEADS, seq, 1), F32)],
        name="fox_fwd", compiler_params=_cparams(("parallel", "parallel", "parallel"), vmem),
    )(qkv, qkv, qkv, dcol, drow)


def _fox_bwd(qkv, y, dy, lse, dcol, drow, nb, seq, tq):
    t = qkv.shape[0]
    nq = seq // tq
    scale = HEAD_DIM ** -0.5
    npair = N_HEADS // 2

    def body(q_ref, k_ref, v_ref, o_ref, do_ref, lse_ref, dc_ref, dr_ref, dq_ref, dk_ref, dv_ref, rs_ref, cs_ref):
        i = pl.program_id(2)

        @pl.when(i == 0)
        def _():
            dk_ref[...] = jnp.zeros_like(dk_ref)
            dv_ref[...] = jnp.zeros_like(dv_ref)
            cs_ref[...] = jnp.zeros_like(cs_ref)

        q = q_ref[...]
        do = do_ref[...]
        dob = do.astype(BF16)
        doo = do * o_ref[...].astype(F32)
        rowi = i * tq + lax.broadcasted_iota(jnp.int32, (tq, tq), 0)
        coli = lax.broadcasted_iota(jnp.int32, (tq, tq), 1)
        dqs = []
        for h in range(2):
            inh = _head_mask(q.shape, h)
            qh = jnp.where(inh, q, jnp.zeros_like(q))
            doh = jnp.where(inh, dob, jnp.zeros_like(dob))
            delta = jnp.sum(jnp.where(inh, doo, 0.0), axis=-1, keepdims=True)
            lse_h = lse_ref[0, h]
            dq_col = dc_ref[0, h]

            def kv_step(j, carry, qh=qh, doh=doh, delta=delta, lse_h=lse_h, dq_col=dq_col, h=h):
                dq_acc, rs_acc = carry
                rows = pl.ds(pl.multiple_of(j * tq, tq), tq)
                ks = k_ref[rows, :]
                vs = v_ref[rows, :]
                s = lax.dot_general(qh, ks, (((1,), (1,)), ((), ())), preferred_element_type=F32) * scale
                s = s + dq_col - dr_ref[0, h, j]
                s = jnp.where(j * tq + coli <= rowi, s, NEG_BIG)
                p = jnp.exp(s - lse_h)
                dp = lax.dot_general(doh, vs, (((1,), (1,)), ((), ())), preferred_element_type=F32)
                ds32 = p * (dp - delta)
                ds = ds32.astype(BF16)
                r = lax.dot_general(ds, qh, (((0,), (0,)), ((), ())), preferred_element_type=F32)
                dvp = lax.dot_general(p.astype(BF16), doh, (((0,), (0,)), ((), ())), preferred_element_type=F32)
                dk_ref[rows, :] += r * scale
                dv_ref[rows, :] += dvp
                cs_ref[0, h, j] += jnp.sum(ds32, axis=0, keepdims=True)
                return (dq_acc + jnp.dot(ds, ks, preferred_element_type=F32),
                        rs_acc + jnp.sum(ds32, axis=-1, keepdims=True))

            init = (jnp.zeros((tq, LANES), F32), jnp.zeros((tq, 1), F32))
            dq_h, rs_h = lax.fori_loop(0, i + 1, kv_step, init)
            dqs.append(dq_h * scale)
            rs_ref[0, h] = rs_h
        dq_ref[...] = jnp.where(_head_mask(dqs[0].shape, 0), dqs[0], dqs[1]).astype(dq_ref.dtype)

    vmem = 4 * _nbytes((seq, LANES), BF16) + 4 * _nbytes((seq, LANES), F32) + 20 * tq * tq * 4 + 4 * 1024 * 1024
    qblk = lambda b, p, i: (b * nq + i, p)
    acc_blk = pl.BlockSpec((seq, LANES), lambda b, p, i: (b, p))
    col_blk = pl.BlockSpec((1, 2, tq, 1), lambda b, p, i: (b, p, i, 0))
    row_blk = pl.BlockSpec((1, 2, nq, 1, tq), lambda b, p, i: (b, p, 0, 0, 0))
    return pl.pallas_call(
        body, grid=(nb, npair, nq),
        in_specs=[pl.BlockSpec((tq, LANES), qblk),
                  pl.BlockSpec((seq, LANES), lambda b, p, i: (b, npair + p)),
                  pl.BlockSpec((seq, LANES), lambda b, p, i: (b, 2 * npair + p)),
                  pl.BlockSpec((tq, LANES), qblk), pl.BlockSpec((tq, LANES), qblk), col_blk, col_blk, row_blk],
        out_specs=[pl.BlockSpec((tq, LANES), qblk), acc_blk, acc_blk, col_blk, row_blk],
        out_shape=[jax.ShapeDtypeStruct((t, D_ATTN), BF16), jax.ShapeDtypeStruct((t, D_ATTN), F32),
                   jax.ShapeDtypeStruct((t, D_ATTN), F32), jax.ShapeDtypeStruct((nb, N_HEADS, seq, 1), F32),
                   jax.ShapeDtypeStruct((nb, N_HEADS, nq, 1, tq), F32)],
        name="fox_bwd", compiler_params=_cparams(("parallel", "parallel", "arbitrary"), vmem),
    )(qkv, qkv, qkv, y, dy, lse, dcol, drow)


def _shift_down(a, k):
    row = lax.broadcasted_iota(jnp.int32, a.shape, 0)
    return jnp.where(row >= k, pltpu.roll(a, k, 0), 0.0)


def _shift_up(a, k):
    n = a.shape[0]
    row = lax.broadcasted_iota(jnp.int32, a.shape, 0)
    return jnp.where(row < n - k, pltpu.roll(a, n - k, 0), 0.0)


def _by_group(vals, shape):
    lane = lax.broadcasted_iota(jnp.int32, shape, 1)
    out = vals[-1]
    for gi in range(len(vals) - 2, -1, -1):
        out = jnp.where(lane < POOL_GROUP * (gi + 1), vals[gi], out)
    return out


def _pooled(u):
    s2 = u + _shift_down(u, 1)
    s4 = s2 + _shift_down(s2, 2)
    s8 = s4 + _shift_down(s4, 4)
    s16 = s8 + _shift_down(s8, 8)
    win = _by_group([s2, s4, s8, s16], u.shape)
    row = lax.broadcasted_iota(jnp.int32, u.shape, 0)
    wsize = _by_group([jnp.full(u.shape, w, jnp.int32) for w in POOL_WINDOWS], u.shape)
    inv = 1.0 / jnp.minimum(row + 1, wsize).astype(F32)
    return win * inv - u, inv


def _pool_fwd(rest, wbd, scale, seq):
    t = rest.shape[0]

    def body(u_ref, w_ref, s_ref, o_ref):
        pooled, _ = _pooled(u_ref[...])
        pw = jnp.dot(pooled.astype(BF16), w_ref[...], preferred_element_type=F32)
        o_ref[...] = (pw * s_ref[...]).astype(o_ref.dtype)

    blk = pl.BlockSpec((seq, D_POOL), lambda b: (b, 0))
    return pl.pallas_call(
        body, grid=(t // seq,),
        in_specs=[blk, pl.BlockSpec((D_POOL, D_POOL), lambda b: (0, 0)), pl.BlockSpec((1, D_POOL), lambda b: (0, 0))],
        out_specs=blk, out_shape=jax.ShapeDtypeStruct((t, D_POOL), BF16), name="pool_fwd",
        compiler_params=_cparams(("parallel",), 24 * seq * D_POOL * 4),
    )(rest, wbd, scale)


def _pool_bwd(rest, dy, wbd, wbd_t, scale, seq):
    t = rest.shape[0]

    def body(u_ref, dy_ref, w_ref, wt_ref, s_ref, du_ref, dw_ref, dsc_ref):
        i = pl.program_id(0)
        pooled, inv = _pooled(u_ref[...])
        pb = pooled.astype(BF16)
        pw = jnp.dot(pb, w_ref[...], preferred_element_type=F32)
        dyp = dy_ref[...]
        dsp = jnp.sum(dyp * pw, axis=0, keepdims=True)
        dpw = (dyp * s_ref[...]).astype(BF16)
        dwp = lax.dot_general(pb, dpw, (((0,), (0,)), ((), ())), preferred_element_type=F32)
        dpooled = jnp.dot(dpw, wt_ref[...], preferred_element_type=F32)
        dwin = dpooled * inv
        t2 = dwin + _shift_up(dwin, 1)
        t4 = t2 + _shift_up(t2, 2)
        t8 = t4 + _shift_up(t4, 4)
        t16 = t8 + _shift_up(t8, 8)
        du_ref[...] = (_by_group([t2, t4, t8, t16], dwin.shape) - dpooled).astype(du_ref.dtype)

        @pl.when(i == 0)
        def _():
            dw_ref[...] = dwp
            dsc_ref[...] = dsp

        @pl.when(i > 0)
        def _():
            dw_ref[...] += dwp
            dsc_ref[...] += dsp

    blk = pl.BlockSpec((seq, D_POOL), lambda b: (b, 0))
    sq = pl.BlockSpec((D_POOL, D_POOL), lambda b: (0, 0))
    row = pl.BlockSpec((1, D_POOL), lambda b: (0, 0))
    return pl.pallas_call(
        body, grid=(t // seq,),
        in_specs=[blk, pl.BlockSpec((seq, D_POOL), lambda b: (b, 2)), sq, sq, row],
        out_specs=[blk, sq, row],
        out_shape=[jax.ShapeDtypeStruct((t, D_POOL), BF16), jax.ShapeDtypeStruct((D_POOL, D_POOL), F32),
                   jax.ShapeDtypeStruct((1, D_POOL), F32)], name="pool_bwd",
        compiler_params=_cparams(("arbitrary",), 40 * seq * D_POOL * 4),
    )(rest, dy, wbd, wbd_t, scale)


def _conv_fwd(rest, cw, seq):
    t = rest.shape[0]

    def body(cb_ref, cc_ref, ch_ref, w_ref, o_ref):
        u = cc_ref[...] * ch_ref[...]
        y = w_ref[0:1, :] * _shift_down(u, 2) + w_ref[1:2, :] * _shift_down(u, 1) + w_ref[2:3, :] * u
        o_ref[...] = (cb_ref[...] * y).astype(o_ref.dtype)

    def col(c):
        return pl.BlockSpec((seq, D_CONV), lambda b, c=c: (b, c))

    return pl.pallas_call(
        body, grid=(t // seq,), in_specs=[col(1), col(2), col(3), pl.BlockSpec((8, D_CONV), lambda b: (0, 0))],
        out_specs=pl.BlockSpec((seq, D_CONV), lambda b: (b, 0)),
        out_shape=jax.ShapeDtypeStruct((t, D_CONV), BF16), name="conv_fwd",
        compiler_params=_cparams(("parallel",), 24 * seq * D_CONV * 4),
    )(rest, rest, rest, cw)


def _conv_bwd(rest, dy, cw, seq):
    t = rest.shape[0]

    def body(cb_ref, cc_ref, ch_ref, dy_ref, w_ref, o_ref, dw_ref):
        i = pl.program_id(0)
        cc = cc_ref[...]
        ch = ch_ref[...]
        u = cc * ch
        u1 = _shift_down(u, 1)
        u2 = _shift_down(u, 2)
        y = w_ref[0:1, :] * u2 + w_ref[1:2, :] * u1 + w_ref[2:3, :] * u
        dyc = dy_ref[...]
        d2 = dyc * cb_ref[...]
        du = w_ref[0:1, :] * _shift_up(d2, 2) + w_ref[1:2, :] * _shift_up(d2, 1) + w_ref[2:3, :] * d2
        o_ref[:, 0:D_CONV] = (dyc * y).astype(o_ref.dtype)
        o_ref[:, D_CONV:2 * D_CONV] = (du * ch).astype(o_ref.dtype)
        o_ref[:, 2 * D_CONV:3 * D_CONV] = (du * cc).astype(o_ref.dtype)
        tap = lax.broadcasted_iota(jnp.int32, (8, D_CONV), 0)
        dwp = jnp.where(tap == 0, jnp.sum(d2 * u2, axis=0, keepdims=True),
                        jnp.where(tap == 1, jnp.sum(d2 * u1, axis=0, keepdims=True),
                                  jnp.where(tap == 2, jnp.sum(d2 * u, axis=0, keepdims=True), 0.0)))

        @pl.when(i == 0)
        def _():
            dw_ref[...] = dwp

        @pl.when(i > 0)
        def _():
            dw_ref[...] += dwp

    def col(c):
        return pl.BlockSpec((seq, D_CONV), lambda b, c=c: (b, c))

    taps = pl.BlockSpec((8, D_CONV), lambda b: (0, 0))
    return pl.pallas_call(
        body, grid=(t // seq,), in_specs=[col(1), col(2), col(3), col(3), taps],
        out_specs=[pl.BlockSpec((seq, 3 * D_CONV), lambda b: (b, 0)), taps],
        out_shape=[jax.ShapeDtypeStruct((t, 3 * D_CONV), BF16), jax.ShapeDtypeStruct((8, D_CONV), F32)],
        name="conv_bwd", compiler_params=_cparams(("arbitrary",), 48 * seq * D_CONV * 4),
    )(rest, rest, rest, dy, cw)


def _adamw(w, g, m, v, name):
    r, c = w.shape
    tr = _pick(r, (512, 352, 256, 128)) if r > 512 else r

    def body(w_ref, g_ref, m_ref, v_ref, d_ref, mo_ref, vo_ref):
        gv = g_ref[...]
        mn = ADAM_B1 * m_ref[...] + (1.0 - ADAM_B1) * gv
        vn = ADAM_B2 * v_ref[...] + (1.0 - ADAM_B2) * (gv * gv)
        m_hat = mn / (1.0 - ADAM_B1 ** ADAM_STEP)
        v_hat = vn / (1.0 - ADAM_B2 ** ADAM_STEP)
        d_ref[...] = -ADAM_LR * (m_hat / (jnp.sqrt(v_hat) + ADAM_EPS) + ADAM_WD * w_ref[...])
        mo_ref[...] = mn
        vo_ref[...] = vn

    blk = pl.BlockSpec((tr, c), lambda i: (i, 0))
    sds = jax.ShapeDtypeStruct((r, c), F32)
    return pl.pallas_call(
        body, grid=(r // tr,), in_specs=[blk] * 4, out_specs=[blk] * 3, out_shape=[sds] * 3, name=name,
        compiler_params=_cparams(("parallel",), 20 * tr * max(c, LANES) * 4),
    )(w, g, m, v)


def _sum_slots(a, name):
    _, r, c = a.shape
    tr = _pick(r, (368, 256, 184, 136, 128, 88, 8))

    def body(a_ref, o_ref):
        acc = a_ref[0]
        for s in range(1, N_DEV):
            acc = acc + a_ref[s]
        o_ref[...] = acc

    return pl.pallas_call(
        body, grid=(r // tr,), in_specs=[pl.BlockSpec((N_DEV, tr, c), lambda i: (0, i, 0))],
        out_specs=pl.BlockSpec((tr, c), lambda i: (i, 0)), out_shape=jax.ShapeDtypeStruct((r, c), F32), name=name,
        compiler_params=_cparams(("parallel",), 24 * tr * c * 4),
    )(a)


def _mesh_pos():
    return lax.axis_index("x"), lax.axis_index("y"), lax.axis_index("c")


def _all_gather(x, name):
    r, c = x.shape

    def body(x_ref, out_ref, send_sems, recv_sems, local_sem):
        mx, my, mc = _mesh_pos()
        me, sibling = (mx, my, mc), (mx, my, 1 - mc)
        chips = [(1 - mx, my), (mx, 1 - my), (1 - mx, 1 - my)]

        def slot(px, py, pc):
            return out_ref.at[4 * px + 2 * py + pc]

        def copy(k, block, to, src=None):
            return pltpu.make_async_remote_copy(
                src_ref=slot(*block) if src is None else src, dst_ref=slot(*block),
                send_sem=send_sems.at[k], recv_sem=recv_sems.at[k],
                device_id=to, device_id_type=pl.DeviceIdType.MESH)

        mine = pltpu.make_async_copy(x_ref, slot(*me), local_sem)
        mine.start()
        first = [copy(0, me, sibling, src=x_ref)]
        first += [copy(1 + j, me, (*chip, mc), src=x_ref) for j, chip in enumerate(chips)]
        for cp in first:
            cp.start()
        passed = [copy(4 + j, (*chip, mc), sibling) for j, chip in enumerate(chips)]
        for j, chip in enumerate(chips):
            copy(1 + j, (*chip, mc), me).wait_recv()
            passed[j].start()
        copy(0, sibling, me).wait_recv()
        for j, chip in enumerate(chips):
            copy(4 + j, (*chip, 1 - mc), me).wait_recv()
        for cp in first + passed:
            cp.wait_send()
        mine.wait()

    return pl.pallas_call(
        body, out_shape=jax.ShapeDtypeStruct((N_DEV, r, c), x.dtype),
        in_specs=[pl.BlockSpec(memory_space=pl.ANY)], out_specs=pl.BlockSpec(memory_space=pl.ANY),
        scratch_shapes=[pltpu.SemaphoreType.DMA((7,)), pltpu.SemaphoreType.DMA((7,)), pltpu.SemaphoreType.DMA],
        name=name,
    )(x)


def _exchange(g, name):
    _, r, c = g.shape

    def body(g_ref, out_ref, send_sems, recv_sems, local_sem):
        mx, my, mc = _mesh_pos()
        me_id = 4 * mx + 2 * my + mc
        mine = pltpu.make_async_copy(g_ref.at[me_id], out_ref.at[me_id], local_sem)
        mine.start()
        copies = []
        for k in range(1, N_DEV):
            px = 1 - mx if k & 4 else mx
            py = 1 - my if k & 2 else my
            pc = 1 - mc if k & 1 else mc
            peer_id = 4 * px + 2 * py + pc
            copies.append((
                pltpu.make_async_remote_copy(
                    src_ref=g_ref.at[peer_id], dst_ref=out_ref.at[me_id],
                    send_sem=send_sems.at[k - 1], recv_sem=recv_sems.at[k - 1],
                    device_id=(px, py, pc), device_id_type=pl.DeviceIdType.MESH),
                pltpu.make_async_remote_copy(
                    src_ref=g_ref.at[peer_id], dst_ref=out_ref.at[peer_id],
                    send_sem=send_sems.at[k - 1], recv_sem=recv_sems.at[k - 1],
                    device_id=(px, py, pc), device_id_type=pl.DeviceIdType.MESH)))
        for send, _ in copies:
            send.start()
        for _, landed in copies:
            landed.wait_recv()
        for send, _ in copies:
            send.wait_send()
        mine.wait()

    return pl.pallas_call(
        body, out_shape=jax.ShapeDtypeStruct(g.shape, g.dtype),
        in_specs=[pl.BlockSpec(memory_space=pl.ANY)], out_specs=pl.BlockSpec(memory_space=pl.ANY),
        scratch_shapes=[pltpu.SemaphoreType.DMA((7,)), pltpu.SemaphoreType.DMA((7,)), pltpu.SemaphoreType.DMA],
        name=name,
    )(g)


def _perm_mix_cols(wm):
    f0 = D_QKV
    f1 = f0 + N_HEADS
    pad = jnp.zeros((wm.shape[0], LANES - N_HEADS), wm.dtype)
    return jnp.concatenate([wm[:, :f0], wm[:, f1:], wm[:, f0:f1], pad], axis=1)


def _unperm_mix_rows(gt):
    f0 = D_QKV
    return jnp.concatenate([gt[:f0], gt[f0 + D_REST:f0 + D_REST + N_HEADS], gt[f0:f0 + D_REST]], axis=0)


def _pack_shards(parts, depth, dtype):
    w1i, w1o, wmi, wmo, w2i, w2o = parts
    rows = []
    for l in range(depth):
        rows += [w1i[l].T, w1o[l],
                 jnp.pad(wmi[l].T, ((0, MIX_ROWS_PAD - MIX_ROWS), (0, 0))), wmo[l], w2i[l].T, w2o[l]]
    return jnp.concatenate(rows, axis=0).astype(dtype)


def _layer_weights(wg, l):
    base = l * LAYER_ROWS
    offs = {}
    o = base
    for nm, n in (("f1i", FFN_ROWS), ("f1o", OUT_ROWS), ("mi", MIX_ROWS_PAD), ("mo", MO_ROWS), ("f2i", FFN_ROWS),
                  ("f2o", OUT_ROWS)):
        offs[nm] = (o, n)
        o += n

    def piece(nm, n_used=None):
        o, n = offs[nm]
        return wg[:, o:o + (n if n_used is None else n_used)]

    out = {}
    for tag in ("f1", "f2"):
        wi_t = piece(tag + "i")[jnp.array(IL_PERM)].reshape(2 * D_FF, D_MODEL)
        wo = piece(tag + "o").reshape(D_FF, D_MODEL)
        out[tag] = dict(wi=wi_t.T, wi_t=wi_t, wo=wo, wo_t=wo.T)
    wm = _perm_mix_cols(piece("mi", MIX_ROWS).reshape(D_IN, D_MODEL).T)
    wm_t = wm.T
    wo = piece("mo").reshape(D_MODEL, D_MODEL)
    out["mix"] = dict(w_qkv=wm[:, :D_QKV], w_rest=wm[:, D_QKV:D_QKV + D_REST], w_f=wm[:, D_QKV + D_REST:],
                      wm_t=wm_t, wo=wo, wo_t=wo.T)
    return out


def _layer_grad_rows(gr):
    def il(gt):
        return gt.reshape(N_DEV, FFN_ROWS, D_MODEL)[jnp.array(IL_PERM)]

    gmi = _unperm_mix_rows(gr["mix_in_t"]).reshape(N_DEV, MIX_ROWS, D_MODEL)
    gmi = jnp.pad(gmi, ((0, 0), (0, MIX_ROWS_PAD - MIX_ROWS), (0, 0)))
    return jnp.concatenate(
        [il(gr["f1_in_t"]), gr["f1_out"].reshape(N_DEV, OUT_ROWS, D_MODEL), gmi,
         gr["mix_out"].reshape(N_DEV, MO_ROWS, D_MODEL), il(gr["f2_in_t"]),
         gr["f2_out"].reshape(N_DEV, OUT_ROWS, D_MODEL)], axis=1)


def _ffn_forward(x, gain, w):
    xn = _rmsnorm_fwd(x, gain, name="ffn_norm")
    h, z = _ffn_in(xn, w["wi"], name="ffn_in")
    x_new = _mm_nn(h, w["wo"], out_dtype=F32, res=x, alpha=0.5, name="ffn_out")
    return x_new, dict(x=x, xn=xn, h=h, z=z)


def _ffn_backward(dxo, dxo_b, gain, w, saved):
    dz = _ffn_bwd_mid(dxo_b, w["wo_t"], saved["z"], name="ffn_bwd_mid")
    g_out = _mm_tn(saved["h"], dxo_b, alpha=0.5, tm=F_HALF, name="ffn_gw_out")
    g_in_t = _mm_tn(dz, saved["xn"], tm=F_HALF, name="ffn_gw_in")
    dxn = _mm_nn(dz, w["wi_t"], out_dtype=F32, name="ffn_dxn")
    dx, dx_b, dg = _rmsnorm_bwd(saved["x"], gain, dxn, dxo, name="ffn_norm_bwd")
    return dx, dx_b, dg, g_in_t, g_out


def _d_layouts(dmat, nb, seq, tq):
    dt = dmat[:, :N_HEADS].reshape(nb, seq, N_HEADS).transpose(0, 2, 1)
    return dt.reshape(nb, N_HEADS, seq, 1), dt.reshape(nb, N_HEADS, seq // tq, 1, tq)


def _mixer_forward(x, p, w, nb, seq, tq):
    xn = _rmsnorm_fwd(x, p["norm"], name="mix_norm")
    qkv = _mm_nn(xn, w["w_qkv"], out_dtype=BF16, name="mix_qkv")
    rest = _mm_nn(xn, w["w_rest"], out_dtype=F32, name="mix_rest")
    fl = _mm_nn(xn, w["w_f"], out_dtype=F32, name="mix_f")
    dmat = _fox_prep(fl, p["bf"], seq)
    dcol, drow = _d_layouts(dmat, nb, seq, tq)
    y_attn, lse = _fox_fwd(qkv, dcol, drow, nb, seq, tq)
    y_pool = _pool_fwd(rest, p["wbd"], p["scale"], seq)
    y_conv = _conv_fwd(rest, p["cw"], seq)
    y = jnp.concatenate([y_attn, y_pool, y_conv], axis=1)
    x_new = _mm_nn(y, w["wo"], out_dtype=F32, res=x, alpha=1.0, name="mix_out")
    return x_new, dict(x=x, xn=xn, qkv=qkv, rest=rest, fl=fl, dcol=dcol, drow=drow, lse=lse, y=y)


def _mixer_backward(dxo, dxo_b, p, w, sv, nb, seq, tq):
    t = dxo.shape[0]
    dy = _mm_nn(dxo_b, w["wo_t"], out_dtype=F32, name="mix_dy")
    g_out = _mm_tn(sv["y"], dxo_b, name="mix_gw_out")
    dq, dk, dv, d_rows, d_cols = _fox_bwd(sv["qkv"], sv["y"], dy, sv["lse"], sv["dcol"], sv["drow"], nb, seq, tq)
    ddh = (d_rows.reshape(nb, N_HEADS, seq) - d_cols.reshape(nb, N_HEADS, seq)).transpose(0, 2, 1).reshape(t, N_HEADS)
    dfl, dbf = _fox_prep_bwd(jnp.pad(ddh, ((0, 0), (0, LANES - N_HEADS))), sv["fl"], p["bf"], seq)
    dpool, dwbd, dscale = _pool_bwd(sv["rest"], dy, p["wbd"], p["wbd_t"], p["scale"], seq)
    dconv, dcw = _conv_bwd(sv["rest"], dy, p["cw"], seq)
    dproj = jnp.concatenate([dq, dk.astype(BF16), dv.astype(BF16), dpool, dconv, dfl], axis=1)
    g_in_t = _mm_tn(dproj, sv["xn"], tm=D_INP // 3, name="mix_gw_in")
    dxn = _mm_nn(dproj, w["wm_t"], out_dtype=F32, name="mix_dxn")
    dx, dx_b, dg = _rmsnorm_bwd(sv["x"], p["norm"], dxn, dxo, name="mix_norm_bwd")
    return dx, dx_b, dict(norm=dg, bf=dbf, wbd=dwbd, scale=dscale, cw=dcw, mix_in_t=g_in_t, mix_out=g_out)


def _block_diag(wp):
    z = jnp.zeros((POOL_GROUP, POOL_GROUP), wp.dtype)
    return jnp.concatenate(
        [jnp.concatenate([wp[g] if g == r else z for g in range(4)], axis=1) for r in range(4)], axis=0)


def _row_pad(a, rows):
    a = a.reshape(-1, a.shape[-1])
    return jnp.pad(a, ((0, rows - a.shape[0]), (0, 0)))


def kernel(x, norm_ffn1, w_ffn1_in, w_ffn1_out, norm_mix, w_mix_in, b_forget, w_pool, pool_scale, conv_w, w_mix_out, norm_ffn2, w_ffn2_in, w_ffn2_out, norm_final, loss_target, m_norm_ffn1, m_w_ffn1_in, m_w_ffn1_out, m_norm_mix, m_w_mix_in, m_b_forget, m_w_pool, m_pool_scale, m_conv_w, m_w_mix_out, m_norm_ffn2, m_w_ffn2_in, m_w_ffn2_out, m_norm_final, v_norm_ffn1, v_w_ffn1_in, v_w_ffn1_out, v_norm_mix, v_w_mix_in, v_b_forget, v_w_pool, v_pool_scale, v_conv_w, v_w_mix_out, v_norm_ffn2, v_w_ffn2_in, v_w_ffn2_out, v_norm_final):
    nb, seq, d = x.shape
    depth = norm_ffn1.shape[0]
    t = nb * seq
    tq = _pick(seq, (256, 128))
    my_id = 4 * lax.axis_index("x") + 2 * lax.axis_index("y") + lax.axis_index("c")
    cshard = conv_w.shape[-1]

    wg = _all_gather(_pack_shards((w_ffn1_in, w_ffn1_out, w_mix_in, w_mix_out, w_ffn2_in, w_ffn2_out), depth, BF16),
                     name="gather_weights")
    cw_g = _all_gather(_row_pad(conv_w.reshape(depth * 3, cshard), 16).reshape(4, LANES), name="gather_conv_taps")
    cw_all = cw_g.reshape(N_DEV, 16, cshard)[:, :depth * 3].reshape(N_DEV, depth, 3, cshard)
    cw_all = cw_all.transpose(1, 2, 0, 3).reshape(depth, 3, D_CONV)

    xs = x.reshape(t, d)
    saved = []
    for l in range(depth):
        w = _layer_weights(wg, l)
        wbd = _block_diag(w_pool[l])
        p = dict(norm=norm_mix[l][None], bf=jnp.pad(b_forget[l], (0, LANES - N_HEADS))[None],
                 wbd=wbd.astype(BF16), wbd_t=wbd.T.astype(BF16), scale=pool_scale[l][None],
                 cw=_row_pad(cw_all[l], 8))
        xs, s1 = _ffn_forward(xs, norm_ffn1[l][None], w["f1"])
        xs, sm = _mixer_forward(xs, p, w["mix"], nb, seq, tq)
        xs, s2 = _ffn_forward(xs, norm_ffn2[l][None], w["f2"])
        saved.append((w, p, s1, sm, s2))

    dx, dx_b, g_norm_final, loss_part = _final_loss_bwd(xs, norm_final[None], loss_target.reshape(t, d))
    layer_rows = [None] * depth
    small = [None] * depth
    for l in reversed(range(depth)):
        w, p, s1, sm, s2 = saved[l]
        dx, dx_b, dg2, g2_in_t, g2_out = _ffn_backward(dx, dx_b, norm_ffn2[l][None], w["f2"], s2)
        dx, dx_b, gm = _mixer_backward(dx, dx_b, p, w["mix"], sm, nb, seq, tq)
        dx, dx_b, dg1, g1_in_t, g1_out = _ffn_backward(dx, dx_b, norm_ffn1[l][None], w["f1"], s1)
        layer_rows[l] = _layer_grad_rows(dict(f1_in_t=g1_in_t, f1_out=g1_out, mix_in_t=gm["mix_in_t"],
                                              mix_out=gm["mix_out"], f2_in_t=g2_in_t, f2_out=g2_out))
        small[l] = dict(n1=dg1, nm=gm["norm"], n2=dg2, bf=gm["bf"], wbd=gm["wbd"], scale=gm["scale"], cw=gm["cw"])
    grad_x = dx.reshape(nb, seq, d)

    g_rows = _sum_slots(_exchange(jnp.concatenate(layer_rows, axis=1), name="exchange_grads"), name="sum_grads")
    g_rows = g_rows.reshape(depth, LAYER_ROWS, D_MODEL)
    o = 0
    pieces = {}
    for nm, n in (("f1i", FFN_ROWS), ("f1o", OUT_ROWS), ("mi", MIX_ROWS_PAD), ("mo", MO_ROWS), ("f2i", FFN_ROWS),
                  ("f2o", OUT_ROWS)):
        pieces[nm] = g_rows[:, o:o + n]
        o += n
    g_sharded = dict(
        w_ffn1_in=pieces["f1i"].transpose(0, 2, 1), w_ffn1_out=pieces["f1o"],
        w_mix_in=pieces["mi"][:, :MIX_ROWS].transpose(0, 2, 1), w_mix_out=pieces["mo"],
        w_ffn2_in=pieces["f2i"].transpose(0, 2, 1), w_ffn2_out=pieces["f2o"])

    def tile8(a):
        return jnp.pad(a, ((0, 8 - a.shape[0]), (0, D_MODEL - a.shape[1])))

    rows = []
    for l in range(depth):
        s = small[l]
        wp_rows = jnp.stack([s["wbd"][POOL_GROUP * g:POOL_GROUP * (g + 1), POOL_GROUP * g:POOL_GROUP * (g + 1)]
                             for g in range(4)]).reshape(16, D_MODEL)
        rows += [tile8(s["n1"]), tile8(s["nm"]), tile8(s["n2"]), tile8(s["bf"]), tile8(s["scale"]), tile8(s["cw"]),
                 wp_rows]
    rows += [tile8(g_norm_final), tile8(loss_part)]
    per_layer = 6 * 8 + 16
    small_sum = _sum_slots(_all_gather(jnp.concatenate(rows, axis=0), name="gather_small_grads"),
                           name="sum_small_grads")
    lay = small_sum[:depth * per_layer].reshape(depth, per_layer, D_MODEL)
    g_small = dict(
        norm_ffn1=lay[:, 0], norm_mix=lay[:, 8], norm_ffn2=lay[:, 16], b_forget=lay[:, 24, :N_HEADS],
        pool_scale=lay[:, 32, :D_POOL],
        conv_w=lax.dynamic_slice_in_dim(lay[:, 40:43, :D_CONV], my_id * cshard, cshard, axis=2),
        w_pool=lay[:, 48:64].reshape(depth, 4, POOL_GROUP, POOL_GROUP),
        norm_final=small_sum[depth * per_layer])
    loss = small_sum[depth * per_layer + 8, 0]

    given = dict(norm_ffn1=(norm_ffn1, m_norm_ffn1, v_norm_ffn1), w_ffn1_in=(w_ffn1_in, m_w_ffn1_in, v_w_ffn1_in),
                 w_ffn1_out=(w_ffn1_out, m_w_ffn1_out, v_w_ffn1_out), norm_mix=(norm_mix, m_norm_mix, v_norm_mix),
                 w_mix_in=(w_mix_in, m_w_mix_in, v_w_mix_in), b_forget=(b_forget, m_b_forget, v_b_forget),
                 w_pool=(w_pool, m_w_pool, v_w_pool), pool_scale=(pool_scale, m_pool_scale, v_pool_scale),
                 conv_w=(conv_w, m_conv_w, v_conv_w), w_mix_out=(w_mix_out, m_w_mix_out, v_w_mix_out),
                 norm_ffn2=(norm_ffn2, m_norm_ffn2, v_norm_ffn2), w_ffn2_in=(w_ffn2_in, m_w_ffn2_in, v_w_ffn2_in),
                 w_ffn2_out=(w_ffn2_out, m_w_ffn2_out, v_w_ffn2_out), norm_final=(norm_final, m_norm_final, v_norm_final))
    names = list(given)
    grads, deltas, new_m, new_v = {}, {}, {}, {}
    for nm in names:
        wv, mv, vv = given[nm]
        gv = (g_sharded[nm] if nm in g_sharded else g_small[nm]).reshape(wv.shape)
        shape2 = (-1, wv.shape[-1]) if wv.ndim > 1 else (1, wv.shape[0])
        dl, mn, vn = _adamw(wv.reshape(shape2), gv.reshape(shape2), mv.reshape(shape2), vv.reshape(shape2),
                            name="adamw_" + nm)
        grads[nm], deltas[nm], new_m[nm], new_v[nm] = gv, dl.reshape(wv.shape), mn.reshape(wv.shape), vn.reshape(wv.shape)
    return (loss, grad_x, *[grads[n] for n in names], *[deltas[n] for n in names],
            *[new_m[n] for n in names], *[new_v[n] for n in names])
```

```python
import functools

import jax
import jax.numpy as jnp
from jax import lax
from jax.experimental import pallas as pl
from jax.experimental.pallas import tpu as pltpu

F32 = jnp.float32
BF16 = jnp.bfloat16

D_MODEL = 1024
D_FF = 2816
HEAD_DIM = 64
N_HEADS = 8
N_PAIRS = N_HEADS // 2
D_ATTN = 512
D_POOL = 256
D_CONV = 256
POOL_WINDOWS = (2, 4, 8, 16)
POOL_GROUP = 64
D_IN = 2568
RMS_EPS = 1e-6
ADAM_LR, ADAM_B1, ADAM_B2, ADAM_EPS, ADAM_WD, ADAM_STEP = 0.001, 0.9, 0.999, 1e-08, 0.01, 10

N_DEV = 8
N_CHIPS = 4
LANES = 128
VMEM_BYTES_V7X = 64 * 1024 * 1024
VMEM_LIMIT_MAX = VMEM_BYTES_V7X - 8 * 1024 * 1024

F_HALF = D_FF // 2
D_QKV = 3 * D_ATTN
D_REST = D_POOL + 3 * D_CONV
D_INP = D_QKV + D_REST + LANES
MIX_ROWS = 321
MIX_ROWS_PAD = 336
FFN_ROWS = 704
OUT_ROWS = 352
MO_ROWS = 128
LAYER_ROWS = 2 * (FFN_ROWS + OUT_ROWS) + MIX_ROWS_PAD + MO_ROWS
IL_PERM = (0, 1, 4, 5, 2, 3, 6, 7)
NEG_BIG = -1e30
ATT_SCALE = HEAD_DIM ** -0.5
ATT_K = 2 * LANES


def _cparams(sem, vmem_bytes):
    limit = int(min(max(vmem_bytes, 16 * 1024 * 1024), VMEM_LIMIT_MAX))
    return pltpu.CompilerParams(dimension_semantics=sem, vmem_limit_bytes=limit)


def _nbytes(shape, dtype):
    n = 1
    for s in shape:
        n *= s
    return n * jnp.dtype(dtype).itemsize


def _pick(n, prefs):
    for p in prefs:
        if n % p == 0:
            return p
    return n


def _rmsnorm_fwd(x, g, name):
    t, d = x.shape
    tm = _pick(t, (512, 256, 128))

    def body(x_ref, g_ref, o_ref):
        xv = x_ref[...]
        r = lax.rsqrt(jnp.mean(xv * xv, axis=-1, keepdims=True) + RMS_EPS)
        o_ref[...] = ((xv * r) * g_ref[...]).astype(o_ref.dtype)

    return pl.pallas_call(
        body, grid=(t // tm,),
        in_specs=[pl.BlockSpec((tm, d), lambda i: (i, 0)), pl.BlockSpec((1, d), lambda i: (0, 0))],
        out_specs=pl.BlockSpec((tm, d), lambda i: (i, 0)),
        out_shape=jax.ShapeDtypeStruct((t, d), BF16), name=name,
        compiler_params=_cparams(("parallel",), 6 * tm * d * 4),
    )(x, g)


def _mm_nn(a, b, *, out_dtype, name, res=None, alpha=1.0, tn=None):
    m, k = a.shape
    n = b.shape[1]
    tn = n if tn is None else tn
    tm = _pick(m, (512, 256, 128))
    with_res = res is not None

    def body(*refs):
        if with_res:
            a_ref, b_ref, r_ref, o_ref = refs
        else:
            a_ref, b_ref, o_ref = refs
        acc = jnp.dot(a_ref[...], b_ref[...], preferred_element_type=F32)
        if with_res:
            acc = r_ref[...] + alpha * acc
        o_ref[...] = acc.astype(o_ref.dtype)

    in_specs = [pl.BlockSpec((tm, k), lambda j, i: (i, 0)), pl.BlockSpec((k, tn), lambda j, i: (0, j))]
    args = [a, b]
    if with_res:
        in_specs.append(pl.BlockSpec((tm, tn), lambda j, i: (i, j)))
        args.append(res)
    vmem = 2 * (_nbytes((tm, k), BF16) + _nbytes((k, tn), BF16) + 3 * _nbytes((tm, tn), F32))
    return pl.pallas_call(
        body, grid=(n // tn, m // tm), in_specs=in_specs,
        out_specs=pl.BlockSpec((tm, tn), lambda j, i: (i, j)),
        out_shape=jax.ShapeDtypeStruct((m, n), out_dtype), name=name,
        compiler_params=_cparams(("parallel", "parallel"), vmem),
    )(*args)


def _mm_tn(a, b, *, name, alpha=1.0, tm=None):
    t, m = a.shape
    n = b.shape[1]
    tm = m if tm is None else tm
    tk = _pick(t, (1024, 512, 256, 128))
    nk = t // tk

    def body(a_ref, b_ref, o_ref):
        kk = pl.program_id(1)
        p = lax.dot_general(a_ref[...], b_ref[...], (((0,), (0,)), ((), ())), preferred_element_type=F32)
        if alpha != 1.0:
            p = alpha * p

        @pl.when(kk == 0)
        def _():
            o_ref[...] = p

        @pl.when(kk > 0)
        def _():
            o_ref[...] += p

    vmem = 2 * (_nbytes((tk, tm), BF16) + _nbytes((tk, n), BF16) + 2 * _nbytes((tm, n), F32))
    return pl.pallas_call(
        body, grid=(m // tm, nk),
        in_specs=[pl.BlockSpec((tk, tm), lambda i, kk: (kk, i)), pl.BlockSpec((tk, n), lambda i, kk: (kk, 0))],
        out_specs=pl.BlockSpec((tm, n), lambda i, kk: (i, 0)),
        out_shape=jax.ShapeDtypeStruct((m, n), F32), name=name,
        compiler_params=_cparams(("parallel", "arbitrary"), vmem),
    )(a, b)


def _sigmoid(v):
    return 1.0 / (1.0 + jnp.exp(-v))


def _ffn_in(xn, w_il, name):
    t, d = xn.shape
    tm = _pick(t, (512, 256, 128))

    def body(x_ref, w_ref, h_ref, z_ref):
        z = jnp.dot(x_ref[...], w_ref[...], preferred_element_type=F32)
        g = z[:, :F_HALF]
        u = z[:, F_HALF:]
        h_ref[...] = ((g * _sigmoid(g)) * u).astype(h_ref.dtype)
        z_ref[...] = z.astype(z_ref.dtype)

    vmem = 2 * (_nbytes((tm, d), BF16) + _nbytes((d, D_FF), BF16) + 4 * _nbytes((tm, D_FF), F32))
    return pl.pallas_call(
        body, grid=(2, t // tm),
        in_specs=[pl.BlockSpec((tm, d), lambda j, i: (i, 0)), pl.BlockSpec((d, D_FF), lambda j, i: (0, j))],
        out_specs=[pl.BlockSpec((tm, F_HALF), lambda j, i: (i, j)), pl.BlockSpec((tm, D_FF), lambda j, i: (i, j))],
        out_shape=[jax.ShapeDtypeStruct((t, D_FF), BF16), jax.ShapeDtypeStruct((t, 2 * D_FF), BF16)], name=name,
        compiler_params=_cparams(("parallel", "parallel"), vmem),
    )(xn, w_il)


def _ffn_bwd_mid(dxo, w_out_t, z, name):
    t, d = dxo.shape
    tm = _pick(t, (512, 256, 128))

    def body(d_ref, w_ref, z_ref, dz_ref):
        dh = 0.5 * jnp.dot(d_ref[...], w_ref[...], preferred_element_type=F32)
        zz = z_ref[...].astype(F32)
        g = zz[:, :F_HALF]
        u = zz[:, F_HALF:]
        s = _sigmoid(g)
        dz_ref[:, :F_HALF] = (dh * u * (s * (1.0 + g * (1.0 - s)))).astype(dz_ref.dtype)
        dz_ref[:, F_HALF:] = (dh * (g * s)).astype(dz_ref.dtype)

    vmem = 2 * (_nbytes((tm, d), BF16) + _nbytes((d, F_HALF), BF16) + 5 * _nbytes((tm, D_FF), F32))
    return pl.pallas_call(
        body, grid=(2, t // tm),
        in_specs=[pl.BlockSpec((tm, d), lambda j, i: (i, 0)), pl.BlockSpec((d, F_HALF), lambda j, i: (0, j)),
                  pl.BlockSpec((tm, D_FF), lambda j, i: (i, j))],
        out_specs=pl.BlockSpec((tm, D_FF), lambda j, i: (i, j)),
        out_shape=jax.ShapeDtypeStruct((t, 2 * D_FF), BF16), name=name,
        compiler_params=_cparams(("parallel", "parallel"), vmem),
    )(dxo, w_out_t, z)


def _rmsnorm_bwd(x, g, dxn, dxo, name):
    t, d = x.shape
    tm = _pick(t, (512, 256, 128))

    def body(x_ref, g_ref, dn_ref, do_ref, dx_ref, dxb_ref, dg_ref):
        i = pl.program_id(0)
        xv = x_ref[...]
        r = lax.rsqrt(jnp.mean(xv * xv, axis=-1, keepdims=True) + RMS_EPS)
        xh = xv * r
        dn = dn_ref[...]
        dgp = jnp.sum(dn * xh, axis=0, keepdims=True)
        dh = dn * g_ref[...]
        dx = do_ref[...] + r * (dh - xh * jnp.mean(dh * xh, axis=-1, keepdims=True))
        dx_ref[...] = dx
        dxb_ref[...] = dx.astype(dxb_ref.dtype)

        @pl.when(i == 0)
        def _():
            dg_ref[...] = dgp

        @pl.when(i > 0)
        def _():
            dg_ref[...] += dgp

    blk = pl.BlockSpec((tm, d), lambda i: (i, 0))
    row = pl.BlockSpec((1, d), lambda i: (0, 0))
    return pl.pallas_call(
        body, grid=(t // tm,), in_specs=[blk, row, blk, blk], out_specs=[blk, blk, row],
        out_shape=[jax.ShapeDtypeStruct((t, d), F32), jax.ShapeDtypeStruct((t, d), BF16),
                   jax.ShapeDtypeStruct((1, d), F32)], name=name,
        compiler_params=_cparams(("arbitrary",), 16 * tm * d * 4),
    )(x, g, dxn, dxo)


def _final_loss_bwd(x, g, tgt):
    t, d = x.shape
    tm = _pick(t, (512, 256, 128))

    def body(x_ref, g_ref, t_ref, dx_ref, dxb_ref, dg_ref, loss_ref):
        i = pl.program_id(0)
        xv = x_ref[...]
        r = lax.rsqrt(jnp.mean(xv * xv, axis=-1, keepdims=True) + RMS_EPS)
        xh = xv * r
        gv = g_ref[...]
        err = xh * gv - t_ref[...]
        lp = 0.5 * jnp.sum(jnp.mean(err * err, axis=-1, keepdims=True), axis=0, keepdims=True)
        dy = err * (1.0 / d)
        dgp = jnp.sum(dy * xh, axis=0, keepdims=True)
        dh = dy * gv
        dx = r * (dh - xh * jnp.mean(dh * xh, axis=-1, keepdims=True))
        dx_ref[...] = dx
        dxb_ref[...] = dx.astype(dxb_ref.dtype)
        lpb = jnp.broadcast_to(lp, (1, LANES))

        @pl.when(i == 0)
        def _():
            dg_ref[...] = dgp
            loss_ref[...] = lpb

        @pl.when(i > 0)
        def _():
            dg_ref[...] += dgp
            loss_ref[...] += lpb

    blk = pl.BlockSpec((tm, d), lambda i: (i, 0))
    row = pl.BlockSpec((1, d), lambda i: (0, 0))
    return pl.pallas_call(
        body, grid=(t // tm,), in_specs=[blk, row, blk],
        out_specs=[blk, blk, row, pl.BlockSpec((1, LANES), lambda i: (0, 0))],
        out_shape=[jax.ShapeDtypeStruct((t, d), F32), jax.ShapeDtypeStruct((t, d), BF16),
                   jax.ShapeDtypeStruct((1, d), F32), jax.ShapeDtypeStruct((1, LANES), F32)], name="final_loss_bwd",
        compiler_params=_cparams(("arbitrary",), 16 * tm * d * 4),
    )(x, g, tgt)


def _seq_scan(v, seq, reverse):
    row = lax.broadcasted_iota(jnp.int32, v.shape, 0)
    k = 1
    while k < seq:
        if reverse:
            v = v + jnp.where(row < seq - k, pltpu.roll(v, seq - k, 0), 0.0)
        else:
            v = v + jnp.where(row >= k, pltpu.roll(v, k, 0), 0.0)
        k *= 2
    return v


def _log_sigmoid(v):
    return jnp.minimum(v, 0.0) - jnp.log(1.0 + jnp.exp(-jnp.abs(v)))


def _fox_prep(fl, bf, seq):
    t = fl.shape[0]

    def body(f_ref, b_ref, o_ref):
        o_ref[...] = _seq_scan(_log_sigmoid(f_ref[...] + b_ref[...]), seq, False)

    blk = pl.BlockSpec((seq, LANES), lambda b: (b, 0))
    return pl.pallas_call(
        body, grid=(t // seq,), in_specs=[blk, pl.BlockSpec((1, LANES), lambda b: (0, 0))], out_specs=blk,
        out_shape=jax.ShapeDtypeStruct((t, LANES), F32), name="fox_prep",
        compiler_params=_cparams(("parallel",), 24 * seq * LANES * 4),
    )(fl, bf)


def _fox_prep_bwd(dd, fl, bf, seq):
    t = fl.shape[0]

    def body(d_ref, f_ref, b_ref, o_ref, db_ref):
        i = pl.program_id(0)
        dlog = _seq_scan(d_ref[...], seq, True)
        dfl = dlog * _sigmoid(-(f_ref[...] + b_ref[...]))
        o_ref[...] = dfl.astype(o_ref.dtype)
        dbp = jnp.sum(dfl, axis=0, keepdims=True)

        @pl.when(i == 0)
        def _():
            db_ref[...] = dbp

        @pl.when(i > 0)
        def _():
            db_ref[...] += dbp

    blk = pl.BlockSpec((seq, LANES), lambda b: (b, 0))
    row = pl.BlockSpec((1, LANES), lambda b: (0, 0))
    return pl.pallas_call(
        body, grid=(t // seq,), in_specs=[blk, blk, row], out_specs=[blk, row],
        out_shape=[jax.ShapeDtypeStruct((t, LANES), BF16), jax.ShapeDtypeStruct((1, LANES), F32)], name="fox_prep_bwd",
        compiler_params=_cparams(("arbitrary",), 24 * seq * LANES * 4),
    )(dd, fl, bf)


def _att_operands(qkv, dmat, nb, seq):
    q = qkv[:, :D_ATTN].reshape(nb, seq, N_PAIRS, 2, HEAD_DIM)
    k = qkv[:, D_ATTN:2 * D_ATTN].reshape(nb, seq, N_PAIRS, LANES)
    v = qkv[:, 2 * D_ATTN:].reshape(nb, seq, N_PAIRS, 2, HEAD_DIM)
    d = dmat[:, :N_HEADS].reshape(nb, seq, N_PAIRS, 2, 1)
    d1 = lax.reduce_precision(d, 8, 7)
    r1 = d - d1
    d2 = lax.reduce_precision(r1, 8, 7)
    d3 = lax.reduce_precision(r1 - d2, 8, 7)
    one = jnp.ones_like(d)
    fill = jnp.zeros(d.shape[:-1] + (HEAD_DIM - 6,), F32)
    aux_q = jnp.concatenate([d1, d2, d3, one, one, one, fill], axis=-1).astype(BF16)
    aux_k = jnp.concatenate([one, one, one, -d1, -d2, -d3, fill], axis=-1).astype(BF16)
    zero = jnp.zeros_like(q)
    qs = q * ATT_SCALE
    qa = jnp.stack([jnp.concatenate([qs[..., 0, :], zero[..., 0, :], aux_q[..., 0, :], zero[..., 0, :]], axis=-1),
                    jnp.concatenate([zero[..., 0, :], qs[..., 1, :], zero[..., 0, :], aux_q[..., 1, :]], axis=-1)],
                   axis=3)
    ka = jnp.concatenate([k, aux_k[..., 0, :], aux_k[..., 1, :]], axis=-1)
    vm = jnp.stack([jnp.concatenate([v[..., 0, :], zero[..., 0, :]], axis=-1),
                    jnp.concatenate([zero[..., 0, :], v[..., 1, :]], axis=-1)], axis=3)
    return qa.transpose(0, 2, 3, 1, 4), ka.transpose(0, 2, 1, 3), vm.transpose(0, 2, 3, 1, 4)


def _pair_rows(a, ta):
    lane = lax.broadcasted_iota(jnp.int32, (ta, LANES), 1)
    return jnp.where(lane < HEAD_DIM, a[:ta], a[ta:])


def _diag_mask(ta):
    r = lax.broadcasted_iota(jnp.int32, (2 * ta, ta), 0)
    c = lax.broadcasted_iota(jnp.int32, (2 * ta, ta), 1)
    return c <= jnp.where(r >= ta, r - ta, r)


def _nt(a, b):
    return lax.dot_general(a, b, (((1,), (1,)), ((), ())), preferred_element_type=F32)


def _tn(a, b):
    return lax.dot_general(a, b, (((0,), (0,)), ((), ())), preferred_element_type=F32)


def _fox_fwd(qa, ka, vm, nb, seq, ta):
    nq = seq // ta

    def body(q_ref, k_ref, v_ref, o_ref, lse_ref):
        i = pl.program_id(2)
        q2 = q_ref[0, 0].reshape(2 * ta, ATT_K)

        def step(j, carry, masked):
            m, l, acc = carry
            rows = pl.ds(pl.multiple_of(j * ta, ta), ta)
            s = _nt(q2, k_ref[0, 0, rows, :])
            if masked:
                s = jnp.where(_diag_mask(ta), s, NEG_BIG)
            m_new = jnp.maximum(m, jnp.max(s, axis=-1, keepdims=True))
            p = jnp.exp(s - m_new)
            corr = jnp.exp(m - m_new)
            l = corr * l + jnp.sum(p, axis=-1, keepdims=True)
            pb = p.astype(BF16)
            pv = (jnp.dot(pb[:ta], v_ref[0, 0, 0, rows, :], preferred_element_type=F32)
                  + jnp.dot(pb[ta:], v_ref[0, 0, 1, rows, :], preferred_element_type=F32))
            return m_new, l, _pair_rows(corr, ta) * acc + pv

        init = (jnp.full((2 * ta, 1), NEG_BIG, F32), jnp.zeros((2 * ta, 1), F32), jnp.zeros((ta, LANES), F32))
        carry = lax.fori_loop(0, i, functools.partial(step, masked=False), init)
        m, l, acc = step(i, carry, True)
        o_ref[...] = (acc * _pair_rows(1.0 / l, ta)).astype(o_ref.dtype)
        lse = m + jnp.log(l)
        lse_ref[0, 0, 0] = lse[:ta]
        lse_ref[0, 0, 1] = lse[ta:]

    vmem = 2 * (_nbytes((seq, ATT_K), BF16) + 2 * _nbytes((seq, LANES), BF16)) + 24 * ta * ta * 4 + 8 * 1024 * 1024
    return pl.pallas_call(
        body, grid=(nb, N_PAIRS, nq),
        in_specs=[pl.BlockSpec((1, 1, 2, ta, ATT_K), lambda b, p, i: (b, p, 0, i, 0)),
                  pl.BlockSpec((1, 1, seq, ATT_K), lambda b, p, i: (b, p, 0, 0)),
                  pl.BlockSpec((1, 1, 2, seq, LANES), lambda b, p, i: (b, p, 0, 0, 0))],
        out_specs=[pl.BlockSpec((ta, LANES), lambda b, p, i: (b * nq + i, p)),
                   pl.BlockSpec((1, 1, 2, ta, 1), lambda b, p, i: (b, p, 0, i, 0))],
        out_shape=[jax.ShapeDtypeStruct((nb * seq, D_ATTN), BF16),
                   jax.ShapeDtypeStruct((nb, N_PAIRS, 2, seq, 1), F32)],
        name="fox_fwd", compiler_params=_cparams(("parallel", "parallel", "parallel"), vmem),
    )(qa, ka, vm)


def _fox_bwd(qa, ka, vm, y, dy, lse, nb, seq, ta):
    nq = seq // ta

    def body(q_ref, k_ref, v_ref, o_ref, do_ref, lse_ref, dq_ref, dk_ref, dv_ref, rs_ref, cs_ref):
        i = pl.program_id(2)

        @pl.when(i == 0)
        def _():
            dk_ref[...] = jnp.zeros_like(dk_ref)
            dv_ref[...] = jnp.zeros_like(dv_ref)
            cs_ref[...] = jnp.zeros_like(cs_ref)

        q2 = q_ref[0, 0].reshape(2 * ta, ATT_K)
        qd = q2[:, :LANES]
        do = do_ref[...]
        first = lax.broadcasted_iota(jnp.int32, do.shape, 1) < HEAD_DIM
        doo = do * o_ref[...].astype(F32)
        do2 = jnp.concatenate([jnp.where(first, do, 0.0), jnp.where(first, 0.0, do)], axis=0).astype(BF16)
        delta = jnp.concatenate([jnp.sum(jnp.where(first, doo, 0.0), axis=-1, keepdims=True),
                                 jnp.sum(jnp.where(first, 0.0, doo), axis=-1, keepdims=True)], axis=0)
        lse2 = jnp.concatenate([lse_ref[0, 0, 0], lse_ref[0, 0, 1]], axis=0)

        def step(j, carry, masked):
            dq_acc, rs_acc = carry
            rows = pl.ds(pl.multiple_of(j * ta, ta), ta)
            ks = k_ref[0, 0, rows, :]
            s = _nt(q2, ks)
            if masked:
                s = jnp.where(_diag_mask(ta), s, NEG_BIG)
            p = jnp.exp(s - lse2)
            dp = _nt(do2, v_ref[0, 0, 0, rows, :] + v_ref[0, 0, 1, rows, :])
            ds32 = p * (dp - delta)
            ds = ds32.astype(BF16)
            dk_ref[rows, :] += _tn(ds, qd)
            dv_ref[rows, :] += _tn(p.astype(BF16), do2)
            cs_ref[0, 0, 0, j] += jnp.sum(ds32[:ta], axis=0, keepdims=True)
            cs_ref[0, 0, 1, j] += jnp.sum(ds32[ta:], axis=0, keepdims=True)
            return (dq_acc + jnp.dot(ds, ks[:, :LANES], preferred_element_type=F32),
                    rs_acc + jnp.sum(ds32, axis=-1, keepdims=True))

        init = (jnp.zeros((2 * ta, LANES), F32), jnp.zeros((2 * ta, 1), F32))
        carry = lax.fori_loop(0, i, functools.partial(step, masked=False), init)
        dq_acc, rs_acc = step(i, carry, True)
        lane = lax.broadcasted_iota(jnp.int32, (ta, LANES), 1)
        dq_ref[...] = (jnp.where(lane < HEAD_DIM, dq_acc[:ta], dq_acc[ta:]) * ATT_SCALE).astype(dq_ref.dtype)
        rs_ref[0, 0, 0] = rs_acc[:ta]
        rs_ref[0, 0, 1] = rs_acc[ta:]

    vmem = (2 * (_nbytes((seq, ATT_K), BF16) + 2 * _nbytes((seq, LANES), BF16) + 2 * _nbytes((seq, LANES), F32))
            + 32 * ta * ta * 4 + 8 * 1024 * 1024)
    qblk = lambda b, p, i: (b * nq + i, p)
    acc_blk = pl.BlockSpec((seq, LANES), lambda b, p, i: (b, p))
    col_blk = pl.BlockSpec((1, 1, 2, ta, 1), lambda b, p, i: (b, p, 0, i, 0))
    return pl.pallas_call(
        body, grid=(nb, N_PAIRS, nq),
        in_specs=[pl.BlockSpec((1, 1, 2, ta, ATT_K), lambda b, p, i: (b, p, 0, i, 0)),
                  pl.BlockSpec((1, 1, seq, ATT_K), lambda b, p, i: (b, p, 0, 0)),
                  pl.BlockSpec((1, 1, 2, seq, LANES), lambda b, p, i: (b, p, 0, 0, 0)),
                  pl.BlockSpec((ta, LANES), qblk), pl.BlockSpec((ta, LANES), qblk), col_blk],
        out_specs=[pl.BlockSpec((ta, LANES), qblk), acc_blk, acc_blk, col_blk,
                   pl.BlockSpec((1, 1, 2, nq, 1, ta), lambda b, p, i: (b, p, 0, 0, 0, 0))],
        out_shape=[jax.ShapeDtypeStruct((nb * seq, D_ATTN), BF16), jax.ShapeDtypeStruct((nb * seq, D_ATTN), F32),
                   jax.ShapeDtypeStruct((nb * seq, D_ATTN), F32), jax.ShapeDtypeStruct((nb, N_PAIRS, 2, seq, 1), F32),
                   jax.ShapeDtypeStruct((nb, N_PAIRS, 2, nq, 1, ta), F32)],
        name="fox_bwd", compiler_params=_cparams(("parallel", "parallel", "arbitrary"), vmem),
    )(qa, ka, vm, y, dy, lse)


def _shift_down(a, k):
    row = lax.broadcasted_iota(jnp.int32, a.shape, 0)
    return jnp.where(row >= k, pltpu.roll(a, k, 0), 0.0)


def _shift_up(a, k):
    n = a.shape[0]
    row = lax.broadcasted_iota(jnp.int32, a.shape, 0)
    return jnp.where(row < n - k, pltpu.roll(a, n - k, 0), 0.0)


def _by_group(vals, shape):
    lane = lax.broadcasted_iota(jnp.int32, shape, 1)
    out = vals[-1]
    for gi in range(len(vals) - 2, -1, -1):
        out = jnp.where(lane < POOL_GROUP * (gi + 1), vals[gi], out)
    return out


def _pooled(u):
    s2 = u + _shift_down(u, 1)
    s4 = s2 + _shift_down(s2, 2)
    s8 = s4 + _shift_down(s4, 4)
    s16 = s8 + _shift_down(s8, 8)
    win = _by_group([s2, s4, s8, s16], u.shape)
    row = lax.broadcasted_iota(jnp.int32, u.shape, 0)
    wsize = _by_group([jnp.full(u.shape, w, jnp.int32) for w in POOL_WINDOWS], u.shape)
    inv = 1.0 / jnp.minimum(row + 1, wsize).astype(F32)
    return win * inv - u, inv


def _pool_fwd(rest, wbd, scale, seq):
    t = rest.shape[0]

    def body(u_ref, w_ref, s_ref, o_ref):
        pooled, _ = _pooled(u_ref[...])
        pw = jnp.dot(pooled.astype(BF16), w_ref[...], preferred_element_type=F32)
        o_ref[...] = (pw * s_ref[...]).astype(o_ref.dtype)

    blk = pl.BlockSpec((seq, D_POOL), lambda b: (b, 0))
    return pl.pallas_call(
        body, grid=(t // seq,),
        in_specs=[blk, pl.BlockSpec((D_POOL, D_POOL), lambda b: (0, 0)), pl.BlockSpec((1, D_POOL), lambda b: (0, 0))],
        out_specs=blk, out_shape=jax.ShapeDtypeStruct((t, D_POOL), BF16), name="pool_fwd",
        compiler_params=_cparams(("parallel",), 24 * seq * D_POOL * 4),
    )(rest, wbd, scale)


def _pool_bwd(rest, dy, wbd, wbd_t, scale, seq):
    t = rest.shape[0]

    def body(u_ref, dy_ref, w_ref, wt_ref, s_ref, du_ref, dw_ref, dsc_ref):
        i = pl.program_id(0)
        pooled, inv = _pooled(u_ref[...])
        pb = pooled.astype(BF16)
        pw = jnp.dot(pb, w_ref[...], preferred_element_type=F32)
        dyp = dy_ref[...]
        dsp = jnp.sum(dyp * pw, axis=0, keepdims=True)
        dpw = (dyp * s_ref[...]).astype(BF16)
        dwp = _tn(pb, dpw)
        dpooled = jnp.dot(dpw, wt_ref[...], preferred_element_type=F32)
        dwin = dpooled * inv
        t2 = dwin + _shift_up(dwin, 1)
        t4 = t2 + _shift_up(t2, 2)
        t8 = t4 + _shift_up(t4, 4)
        t16 = t8 + _shift_up(t8, 8)
        du_ref[...] = (_by_group([t2, t4, t8, t16], dwin.shape) - dpooled).astype(du_ref.dtype)

        @pl.when(i == 0)
        def _():
            dw_ref[...] = dwp
            dsc_ref[...] = dsp

        @pl.when(i > 0)
        def _():
            dw_ref[...] += dwp
            dsc_ref[...] += dsp

    blk = pl.BlockSpec((seq, D_POOL), lambda b: (b, 0))
    sq = pl.BlockSpec((D_POOL, D_POOL), lambda b: (0, 0))
    row = pl.BlockSpec((1, D_POOL), lambda b: (0, 0))
    return pl.pallas_call(
        body, grid=(t // seq,),
        in_specs=[blk, pl.BlockSpec((seq, D_POOL), lambda b: (b, 2)), sq, sq, row],
        out_specs=[blk, sq, row],
        out_shape=[jax.ShapeDtypeStruct((t, D_POOL), BF16), jax.ShapeDtypeStruct((D_POOL, D_POOL), F32),
                   jax.ShapeDtypeStruct((1, D_POOL), F32)], name="pool_bwd",
        compiler_params=_cparams(("arbitrary",), 40 * seq * D_POOL * 4),
    )(rest, dy, wbd, wbd_t, scale)


def _conv_fwd(rest, cw, seq):
    t = rest.shape[0]

    def body(cb_ref, cc_ref, ch_ref, w_ref, o_ref):
        u = cc_ref[...] * ch_ref[...]
        y = w_ref[0:1, :] * _shift_down(u, 2) + w_ref[1:2, :] * _shift_down(u, 1) + w_ref[2:3, :] * u
        o_ref[...] = (cb_ref[...] * y).astype(o_ref.dtype)

    def col(c):
        return pl.BlockSpec((seq, D_CONV), lambda b, c=c: (b, c))

    return pl.pallas_call(
        body, grid=(t // seq,), in_specs=[col(1), col(2), col(3), pl.BlockSpec((8, D_CONV), lambda b: (0, 0))],
        out_specs=pl.BlockSpec((seq, D_CONV), lambda b: (b, 0)),
        out_shape=jax.ShapeDtypeStruct((t, D_CONV), BF16), name="conv_fwd",
        compiler_params=_cparams(("parallel",), 24 * seq * D_CONV * 4),
    )(rest, rest, rest, cw)


def _conv_bwd(rest, dy, cw, seq):
    t = rest.shape[0]

    def body(cb_ref, cc_ref, ch_ref, dy_ref, w_ref, o_ref, dw_ref):
        i = pl.program_id(0)
        cc = cc_ref[...]
        ch = ch_ref[...]
        u = cc * ch
        u1 = _shift_down(u, 1)
        u2 = _shift_down(u, 2)
        y = w_ref[0:1, :] * u2 + w_ref[1:2, :] * u1 + w_ref[2:3, :] * u
        dyc = dy_ref[...]
        d2 = dyc * cb_ref[...]
        du = w_ref[0:1, :] * _shift_up(d2, 2) + w_ref[1:2, :] * _shift_up(d2, 1) + w_ref[2:3, :] * d2
        o_ref[:, 0:D_CONV] = (dyc * y).astype(o_ref.dtype)
        o_ref[:, D_CONV:2 * D_CONV] = (du * ch).astype(o_ref.dtype)
        o_ref[:, 2 * D_CONV:3 * D_CONV] = (du * cc).astype(o_ref.dtype)
        tap = lax.broadcasted_iota(jnp.int32, (8, D_CONV), 0)
        dwp = jnp.where(tap == 0, jnp.sum(d2 * u2, axis=0, keepdims=True),
                        jnp.where(tap == 1, jnp.sum(d2 * u1, axis=0, keepdims=True),
                                  jnp.where(tap == 2, jnp.sum(d2 * u, axis=0, keepdims=True), 0.0)))

        @pl.when(i == 0)
        def _():
            dw_ref[...] = dwp

        @pl.when(i > 0)
        def _():
            dw_ref[...] += dwp

    def col(c):
        return pl.BlockSpec((seq, D_CONV), lambda b, c=c: (b, c))

    taps = pl.BlockSpec((8, D_CONV), lambda b: (0, 0))
    return pl.pallas_call(
        body, grid=(t // seq,), in_specs=[col(1), col(2), col(3), col(3), taps],
        out_specs=[pl.BlockSpec((seq, 3 * D_CONV), lambda b: (b, 0)), taps],
        out_shape=[jax.ShapeDtypeStruct((t, 3 * D_CONV), BF16), jax.ShapeDtypeStruct((8, D_CONV), F32)],
        name="conv_bwd", compiler_params=_cparams(("arbitrary",), 48 * seq * D_CONV * 4),
    )(rest, rest, rest, dy, cw)


def _adamw(w, g, m, v, name):
    r, c = w.shape
    tr = _pick(r, (512, 352, 256, 128)) if r > 512 else r

    def body(w_ref, g_ref, m_ref, v_ref, d_ref, mo_ref, vo_ref):
        gv = g_ref[...]
        mn = ADAM_B1 * m_ref[...] + (1.0 - ADAM_B1) * gv
        vn = ADAM_B2 * v_ref[...] + (1.0 - ADAM_B2) * (gv * gv)
        m_hat = mn / (1.0 - ADAM_B1 ** ADAM_STEP)
        v_hat = vn / (1.0 - ADAM_B2 ** ADAM_STEP)
        d_ref[...] = -ADAM_LR * (m_hat / (jnp.sqrt(v_hat) + ADAM_EPS) + ADAM_WD * w_ref[...])
        mo_ref[...] = mn
        vo_ref[...] = vn

    blk = pl.BlockSpec((tr, c), lambda i: (i, 0))
    sds = jax.ShapeDtypeStruct((r, c), F32)
    return pl.pallas_call(
        body, grid=(r // tr,), in_specs=[blk] * 4, out_specs=[blk] * 3, out_shape=[sds] * 3, name=name,
        compiler_params=_cparams(("parallel",), 20 * tr * max(c, LANES) * 4),
    )(w, g, m, v)


def _sum_slots(a, name):
    ns, r, c = a.shape
    tr = _pick(r, (368, 256, 184, 136, 128, 88, 8))

    def body(a_ref, o_ref):
        acc = a_ref[0].astype(F32)
        for s in range(1, ns):
            acc = acc + a_ref[s].astype(F32)
        o_ref[...] = acc

    return pl.pallas_call(
        body, grid=(r // tr,), in_specs=[pl.BlockSpec((ns, tr, c), lambda i: (0, i, 0))],
        out_specs=pl.BlockSpec((tr, c), lambda i: (i, 0)), out_shape=jax.ShapeDtypeStruct((r, c), F32), name=name,
        compiler_params=_cparams(("parallel",), 4 * (ns + 2) * tr * c * 4),
    )(a)


def _add_pairs(a, b, out_dtype, name):
    ns, r, c = a.shape
    tr = _pick(r, (368, 256, 184, 136, 128, 88, 8))

    def body(a_ref, b_ref, o_ref):
        o_ref[...] = (a_ref[...] + b_ref[...]).astype(o_ref.dtype)

    blk = pl.BlockSpec((1, tr, c), lambda s, i: (s, i, 0))
    return pl.pallas_call(
        body, grid=(ns, r // tr), in_specs=[blk, blk], out_specs=blk,
        out_shape=jax.ShapeDtypeStruct(a.shape, out_dtype), name=name,
        compiler_params=_cparams(("parallel", "parallel"), 10 * tr * c * 4),
    )(a, b)


def _mesh_pos():
    return lax.axis_index("x"), lax.axis_index("y"), lax.axis_index("c")


def _all_gather(x, name):
    r, c = x.shape

    def body(x_ref, out_ref, send_sems, recv_sems, local_sem):
        mx, my, mc = _mesh_pos()
        me, sibling = (mx, my, mc), (mx, my, 1 - mc)
        chips = [(1 - mx, my), (mx, 1 - my), (1 - mx, 1 - my)]

        def slot(px, py, pc):
            return out_ref.at[4 * px + 2 * py + pc]

        def copy(k, block, to, src=None):
            return pltpu.make_async_remote_copy(
                src_ref=slot(*block) if src is None else src, dst_ref=slot(*block),
                send_sem=send_sems.at[k], recv_sem=recv_sems.at[k],
                device_id=to, device_id_type=pl.DeviceIdType.MESH)

        mine = pltpu.make_async_copy(x_ref, slot(*me), local_sem)
        mine.start()
        first = [copy(0, me, sibling, src=x_ref)]
        first += [copy(1 + j, me, (*chip, mc), src=x_ref) for j, chip in enumerate(chips)]
        for cp in first:
            cp.start()
        passed = [copy(4 + j, (*chip, mc), sibling) for j, chip in enumerate(chips)]
        for j, chip in enumerate(chips):
            copy(1 + j, (*chip, mc), me).wait_recv()
            passed[j].start()
        copy(0, sibling, me).wait_recv()
        for j, chip in enumerate(chips):
            copy(4 + j, (*chip, 1 - mc), me).wait_recv()
        for cp in first + passed:
            cp.wait_send()
        mine.wait()

    return pl.pallas_call(
        body, out_shape=jax.ShapeDtypeStruct((N_DEV, r, c), x.dtype),
        in_specs=[pl.BlockSpec(memory_space=pl.ANY)], out_specs=pl.BlockSpec(memory_space=pl.ANY),
        scratch_shapes=[pltpu.SemaphoreType.DMA((7,)), pltpu.SemaphoreType.DMA((7,)), pltpu.SemaphoreType.DMA],
        name=name,
    )(x)


def _exchange_sibling(g4, name):
    nchip, _, r, c = g4.shape

    def body(g_ref, mine_ref, theirs_ref, send_sems, recv_sems, local_sems):
        mx, my, mc = _mesh_pos()
        sends, keeps = [], []
        for chip in range(nchip):
            sends.append(pltpu.make_async_remote_copy(
                src_ref=g_ref.at[chip, 1 - mc], dst_ref=theirs_ref.at[chip],
                send_sem=send_sems.at[chip], recv_sem=recv_sems.at[chip],
                device_id=(mx, my, 1 - mc), device_id_type=pl.DeviceIdType.MESH))
            keeps.append(pltpu.make_async_copy(g_ref.at[chip, mc], mine_ref.at[chip], local_sems.at[chip]))
        for cp in sends + keeps:
            cp.start()
        for cp in sends:
            cp.wait_recv()
        for cp in sends:
            cp.wait_send()
        for cp in keeps:
            cp.wait()

    sds = jax.ShapeDtypeStruct((nchip, r, c), g4.dtype)
    any_spec = pl.BlockSpec(memory_space=pl.ANY)
    return pl.pallas_call(
        body, out_shape=[sds, sds], in_specs=[any_spec], out_specs=[any_spec, any_spec],
        scratch_shapes=[pltpu.SemaphoreType.DMA((nchip,)), pltpu.SemaphoreType.DMA((nchip,)),
                        pltpu.SemaphoreType.DMA((nchip,))],
        name=name,
    )(g4)


def _exchange_chips(ts, name):
    nchip, r, c = ts.shape

    def body(t_ref, out_ref, send_sems, recv_sems, local_sem):
        mx, my, mc = _mesh_pos()
        my_chip = 2 * mx + my
        keep = pltpu.make_async_copy(t_ref.at[my_chip], out_ref.at[my_chip], local_sem)
        keep.start()
        copies = []
        for k in range(1, nchip):
            px = 1 - mx if k & 2 else mx
            py = 1 - my if k & 1 else my
            peer_chip = 2 * px + py

            def rdma(dst_slot, px=px, py=py, peer_chip=peer_chip, k=k):
                return pltpu.make_async_remote_copy(
                    src_ref=t_ref.at[peer_chip], dst_ref=out_ref.at[dst_slot],
                    send_sem=send_sems.at[k - 1], recv_sem=recv_sems.at[k - 1],
                    device_id=(px, py, mc), device_id_type=pl.DeviceIdType.MESH)

            copies.append((rdma(my_chip), rdma(peer_chip)))
        for send, _ in copies:
            send.start()
        for _, landed in copies:
            landed.wait_recv()
        for send, _ in copies:
            send.wait_send()
        keep.wait()

    any_spec = pl.BlockSpec(memory_space=pl.ANY)
    return pl.pallas_call(
        body, out_shape=jax.ShapeDtypeStruct(ts.shape, ts.dtype), in_specs=[any_spec], out_specs=any_spec,
        scratch_shapes=[pltpu.SemaphoreType.DMA((nchip - 1,)), pltpu.SemaphoreType.DMA((nchip - 1,)),
                        pltpu.SemaphoreType.DMA],
        name=name,
    )(ts)


def _reduce_scatter(g):
    _, r, c = g.shape
    mine, theirs = _exchange_sibling(g.reshape(N_CHIPS, 2, r, c), name="exchange_grads_sibling")
    chip_sums = _add_pairs(mine, theirs, BF16, name="add_sibling_grads")
    return _sum_slots(_exchange_chips(chip_sums, name="exchange_grads_chips"), name="sum_grads")


def _perm_mix_cols(wm):
    f0 = D_QKV
    f1 = f0 + N_HEADS
    pad = jnp.zeros((wm.shape[0], LANES - N_HEADS), wm.dtype)
    return jnp.concatenate([wm[:, :f0], wm[:, f1:], wm[:, f0:f1], pad], axis=1)


def _unperm_mix_rows(gt):
    f0 = D_QKV
    return jnp.concatenate([gt[:f0], gt[f0 + D_REST:f0 + D_REST + N_HEADS], gt[f0:f0 + D_REST]], axis=0)


def _pack_shards(parts, depth, dtype):
    w1i, w1o, wmi, wmo, w2i, w2o = parts
    rows = []
    for l in range(depth):
        rows += [w1i[l].T, w1o[l],
                 jnp.pad(wmi[l].T, ((0, MIX_ROWS_PAD - MIX_ROWS), (0, 0))), wmo[l], w2i[l].T, w2o[l]]
    return jnp.concatenate(rows, axis=0).astype(dtype)


def _layer_weights(wg, l):
    base = l * LAYER_ROWS
    offs = {}
    o = base
    for nm, n in (("f1i", FFN_ROWS), ("f1o", OUT_ROWS), ("mi", MIX_ROWS_PAD), ("mo", MO_ROWS), ("f2i", FFN_ROWS),
                  ("f2o", OUT_ROWS)):
        offs[nm] = (o, n)
        o += n

    def piece(nm, n_used=None):
        o, n = offs[nm]
        return wg[:, o:o + (n if n_used is None else n_used)]

    out = {}
    for tag in ("f1", "f2"):
        wi = piece(tag + "i")
        wi_t = jnp.concatenate([wi[p] for p in IL_PERM], axis=0)
        wo = piece(tag + "o").reshape(D_FF, D_MODEL)
        out[tag] = dict(wi=wi_t.T, wi_t=wi_t, wo=wo, wo_t=wo.T)
    wm = _perm_mix_cols(piece("mi", MIX_ROWS).reshape(D_IN, D_MODEL).T)
    wm_t = wm.T
    wo = piece("mo").reshape(D_MODEL, D_MODEL)
    out["mix"] = dict(w_qkv=wm[:, :D_QKV], w_rest=wm[:, D_QKV:D_QKV + D_REST], w_f=wm[:, D_QKV + D_REST:],
                      wm_t=wm_t, wo=wo, wo_t=wo.T)
    return out


def _layer_grad_rows(gr):
    def il(gt):
        return jnp.stack([gt[FFN_ROWS * p:FFN_ROWS * (p + 1)] for p in IL_PERM])

    gmi = _unperm_mix_rows(gr["mix_in_t"]).reshape(N_DEV, MIX_ROWS, D_MODEL)
    gmi = jnp.pad(gmi, ((0, 0), (0, MIX_ROWS_PAD - MIX_ROWS), (0, 0)))
    return jnp.concatenate(
        [il(gr["f1_in_t"]), gr["f1_out"].reshape(N_DEV, OUT_ROWS, D_MODEL), gmi,
         gr["mix_out"].reshape(N_DEV, MO_ROWS, D_MODEL), il(gr["f2_in_t"]),
         gr["f2_out"].reshape(N_DEV, OUT_ROWS, D_MODEL)], axis=1)


def _ffn_forward(x, gain, w):
    xn = _rmsnorm_fwd(x, gain, name="ffn_norm")
    h, z = _ffn_in(xn, w["wi"], name="ffn_in")
    x_new = _mm_nn(h, w["wo"], out_dtype=F32, res=x, alpha=0.5, name="ffn_out")
    return x_new, dict(x=x, xn=xn, h=h, z=z)


def _ffn_backward(dxo, dxo_b, gain, w, saved):
    dz = _ffn_bwd_mid(dxo_b, w["wo_t"], saved["z"], name="ffn_bwd_mid")
    g_out = _mm_tn(saved["h"], dxo_b, alpha=0.5, tm=F_HALF, name="ffn_gw_out")
    g_in_t = _mm_tn(dz, saved["xn"], tm=F_HALF, name="ffn_gw_in")
    dxn = _mm_nn(dz, w["wi_t"], out_dtype=F32, name="ffn_dxn")
    dx, dx_b, dg = _rmsnorm_bwd(saved["x"], gain, dxn, dxo, name="ffn_norm_bwd")
    return dx, dx_b, dg, g_in_t, g_out


def _mixer_forward(x, p, w, nb, seq, ta):
    xn = _rmsnorm_fwd(x, p["norm"], name="mix_norm")
    qkv = _mm_nn(xn, w["w_qkv"], out_dtype=BF16, name="mix_qkv")
    rest = _mm_nn(xn, w["w_rest"], out_dtype=F32, name="mix_rest")
    fl = _mm_nn(xn, w["w_f"], out_dtype=F32, name="mix_f")
    qa, ka, vm = _att_operands(qkv, _fox_prep(fl, p["bf"], seq), nb, seq)
    y_attn, lse = _fox_fwd(qa, ka, vm, nb, seq, ta)
    y_pool = _pool_fwd(rest, p["wbd"], p["scale"], seq)
    y_conv = _conv_fwd(rest, p["cw"], seq)
    y = jnp.concatenate([y_attn, y_pool, y_conv], axis=1)
    x_new = _mm_nn(y, w["wo"], out_dtype=F32, res=x, alpha=1.0, name="mix_out")
    return x_new, dict(x=x, xn=xn, qa=qa, ka=ka, vm=vm, rest=rest, fl=fl, lse=lse, y=y)


def _mixer_backward(dxo, dxo_b, p, w, sv, nb, seq, ta):
    t = dxo.shape[0]
    dy = _mm_nn(dxo_b, w["wo_t"], out_dtype=F32, name="mix_dy")
    g_out = _mm_tn(sv["y"], dxo_b, name="mix_gw_out")
    dq, dk, dv, d_rows, d_cols = _fox_bwd(sv["qa"], sv["ka"], sv["vm"], sv["y"], dy, sv["lse"], nb, seq, ta)
    ddh = (d_rows.reshape(nb, N_HEADS, seq) - d_cols.reshape(nb, N_HEADS, seq)).transpose(0, 2, 1).reshape(t, N_HEADS)
    dfl, dbf = _fox_prep_bwd(jnp.pad(ddh, ((0, 0), (0, LANES - N_HEADS))), sv["fl"], p["bf"], seq)
    dpool, dwbd, dscale = _pool_bwd(sv["rest"], dy, p["wbd"], p["wbd_t"], p["scale"], seq)
    dconv, dcw = _conv_bwd(sv["rest"], dy, p["cw"], seq)
    dproj = jnp.concatenate([dq, dk.astype(BF16), dv.astype(BF16), dpool, dconv, dfl], axis=1)
    g_in_t = _mm_tn(dproj, sv["xn"], tm=D_INP // 3, name="mix_gw_in")
    dxn = _mm_nn(dproj, w["wm_t"], out_dtype=F32, name="mix_dxn")
    dx, dx_b, dg = _rmsnorm_bwd(sv["x"], p["norm"], dxn, dxo, name="mix_norm_bwd")
    return dx, dx_b, dict(norm=dg, bf=dbf, wbd=dwbd, scale=dscale, cw=dcw, mix_in_t=g_in_t, mix_out=g_out)


def _block_diag(wp):
    z = jnp.zeros((POOL_GROUP, POOL_GROUP), wp.dtype)
    return jnp.concatenate(
        [jnp.concatenate([wp[g] if g == r else z for g in range(4)], axis=1) for r in range(4)], axis=0)


def _row_pad(a, rows):
    a = a.reshape(-1, a.shape[-1])
    return jnp.pad(a, ((0, rows - a.shape[0]), (0, 0)))


def kernel(x, norm_ffn1, w_ffn1_in, w_ffn1_out, norm_mix, w_mix_in, b_forget, w_pool, pool_scale, conv_w, w_mix_out, norm_ffn2, w_ffn2_in, w_ffn2_out, norm_final, loss_target, m_norm_ffn1, m_w_ffn1_in, m_w_ffn1_out, m_norm_mix, m_w_mix_in, m_b_forget, m_w_pool, m_pool_scale, m_conv_w, m_w_mix_out, m_norm_ffn2, m_w_ffn2_in, m_w_ffn2_out, m_norm_final, v_norm_ffn1, v_w_ffn1_in, v_w_ffn1_out, v_norm_mix, v_w_mix_in, v_b_forget, v_w_pool, v_pool_scale, v_conv_w, v_w_mix_out, v_norm_ffn2, v_w_ffn2_in, v_w_ffn2_out, v_norm_final):
    nb, seq, d = x.shape
    depth = norm_ffn1.shape[0]
    t = nb * seq
    ta = _pick(seq, (128,))
    my_id = 4 * lax.axis_index("x") + 2 * lax.axis_index("y") + lax.axis_index("c")
    cshard = conv_w.shape[-1]

    wg = _all_gather(_pack_shards((w_ffn1_in, w_ffn1_out, w_mix_in, w_mix_out, w_ffn2_in, w_ffn2_out), depth, BF16),
                     name="gather_weights")
    cw_g = _all_gather(_row_pad(conv_w.reshape(depth * 3, cshard), 16).reshape(4, LANES), name="gather_conv_taps")
    cw_all = cw_g.reshape(N_DEV, 16, cshard)[:, :depth * 3].reshape(N_DEV, depth, 3, cshard)
    cw_all = cw_all.transpose(1, 2, 0, 3).reshape(depth, 3, D_CONV)

    xs = x.reshape(t, d)
    saved = []
    for l in range(depth):
        w = _layer_weights(wg, l)
        wbd = _block_diag(w_pool[l])
        p = dict(norm=norm_mix[l][None], bf=jnp.pad(b_forget[l], (0, LANES - N_HEADS))[None],
                 wbd=wbd.astype(BF16), wbd_t=wbd.T.astype(BF16), scale=pool_scale[l][None],
                 cw=_row_pad(cw_all[l], 8))
        xs, s1 = _ffn_forward(xs, norm_ffn1[l][None], w["f1"])
        xs, sm = _mixer_forward(xs, p, w["mix"], nb, seq, ta)
        xs, s2 = _ffn_forward(xs, norm_ffn2[l][None], w["f2"])
        saved.append((w, p, s1, sm, s2))

    dx, dx_b, g_norm_final, loss_part = _final_loss_bwd(xs, norm_final[None], loss_target.reshape(t, d))
    layer_rows = [None] * depth
    small = [None] * depth
    for l in reversed(range(depth)):
        w, p, s1, sm, s2 = saved[l]
        dx, dx_b, dg2, g2_in_t, g2_out = _ffn_backward(dx, dx_b, norm_ffn2[l][None], w["f2"], s2)
        dx, dx_b, gm = _mixer_backward(dx, dx_b, p, w["mix"], sm, nb, seq, ta)
        dx, dx_b, dg1, g1_in_t, g1_out = _ffn_backward(dx, dx_b, norm_ffn1[l][None], w["f1"], s1)
        layer_rows[l] = _layer_grad_rows(dict(f1_in_t=g1_in_t, f1_out=g1_out, mix_in_t=gm["mix_in_t"],
                                              mix_out=gm["mix_out"], f2_in_t=g2_in_t, f2_out=g2_out))
        small[l] = dict(n1=dg1, nm=gm["norm"], n2=dg2, bf=gm["bf"], wbd=gm["wbd"], scale=gm["scale"], cw=gm["cw"])
    grad_x = dx.reshape(nb, seq, d)

    g_rows = _reduce_scatter(jnp.concatenate(layer_rows, axis=1)).reshape(depth, LAYER_ROWS, D_MODEL)
    o = 0
    pieces = {}
    for nm, n in (("f1i", FFN_ROWS), ("f1o", OUT_ROWS), ("mi", MIX_ROWS_PAD), ("mo", MO_ROWS), ("f2i", FFN_ROWS),
                  ("f2o", OUT_ROWS)):
        pieces[nm] = g_rows[:, o:o + n]
        o += n
    g_sharded = dict(
        w_ffn1_in=pieces["f1i"].transpose(0, 2, 1), w_ffn1_out=pieces["f1o"],
        w_mix_in=pieces["mi"][:, :MIX_ROWS].transpose(0, 2, 1), w_mix_out=pieces["mo"],
        w_ffn2_in=pieces["f2i"].transpose(0, 2, 1), w_ffn2_out=pieces["f2o"])

    def tile8(a):
        return jnp.pad(a, ((0, 8 - a.shape[0]), (0, D_MODEL - a.shape[1])))

    rows = []
    for l in range(depth):
        s = small[l]
        wp_rows = jnp.stack([s["wbd"][POOL_GROUP * g:POOL_GROUP * (g + 1), POOL_GROUP * g:POOL_GROUP * (g + 1)]
                             for g in range(4)]).reshape(16, D_MODEL)
        rows += [tile8(s["n1"]), tile8(s["nm"]), tile8(s["n2"]), tile8(s["bf"]), tile8(s["scale"]), tile8(s["cw"]),
                 wp_rows]
    rows += [tile8(g_norm_final), tile8(loss_part)]
    per_layer = 6 * 8 + 16
    small_sum = _sum_slots(_all_gather(jnp.concatenate(rows, axis=0), name="gather_small_grads"),
                           name="sum_small_grads")
    lay = small_sum[:depth * per_layer].reshape(depth, per_layer, D_MODEL)
    g_small = dict(
        norm_ffn1=lay[:, 0], norm_mix=lay[:, 8], norm_ffn2=lay[:, 16], b_forget=lay[:, 24, :N_HEADS],
        pool_scale=lay[:, 32, :D_POOL],
        conv_w=lax.dynamic_slice_in_dim(lay[:, 40:43, :D_CONV], my_id * cshard, cshard, axis=2),
        w_pool=lay[:, 48:64].reshape(depth, 4, POOL_GROUP, POOL_GROUP),
        norm_final=small_sum[depth * per_layer])
    loss = small_sum[depth * per_layer + 8, 0]

    given = dict(norm_ffn1=(norm_ffn1, m_norm_ffn1, v_norm_ffn1), w_ffn1_in=(w_ffn1_in, m_w_ffn1_in, v_w_ffn1_in),
                 w_ffn1_out=(w_ffn1_out, m_w_ffn1_out, v_w_ffn1_out), norm_mix=(norm_mix, m_norm_mix, v_norm_mix),
                 w_mix_in=(w_mix_in, m_w_mix_in, v_w_mix_in), b_forget=(b_forget, m_b_forget, v_b_forget),
                 w_pool=(w_pool, m_w_pool, v_w_pool), pool_scale=(pool_scale, m_pool_scale, v_pool_scale),
                 conv_w=(conv_w, m_conv_w, v_conv_w), w_mix_out=(w_mix_out, m_w_mix_out, v_w_mix_out),
                 norm_ffn2=(norm_ffn2, m_norm_ffn2, v_norm_ffn2), w_ffn2_in=(w_ffn2_in, m_w_ffn2_in, v_w_ffn2_in),
                 w_ffn2_out=(w_ffn2_out, m_w_ffn2_out, v_w_ffn2_out), norm_final=(norm_final, m_norm_final, v_norm_final))
    names = list(given)
    grads, deltas, new_m, new_v = {}, {}, {}, {}
    for nm in names:
        wv, mv, vv = given[nm]
        gv = (g_sharded[nm] if nm in g_sharded else g_small[nm]).reshape(wv.shape)
        shape2 = (-1, wv.shape[-1]) if wv.ndim > 1 else (1, wv.shape[0])
        dl, mn, vn = _adamw(wv.reshape(shape2), gv.reshape(shape2), mv.reshape(shape2), vv.reshape(shape2),
                            name="adamw_" + nm)
        grads[nm], deltas[nm], new_m[nm], new_v[nm] = gv, dl.reshape(wv.shape), mn.reshape(wv.shape), vn.reshape(wv.shape)
    return (loss, grad_x, *[grads[n] for n in names], *[deltas[n] for n in names],
            *[new_m[n] for n in names], *[new_v[n] for n in names])
```

```python
import functools

import jax
import jax.numpy as jnp
from jax import lax
from jax.experimental import pallas as pl
from jax.experimental.pallas import tpu as pltpu

F32 = jnp.float32
BF16 = jnp.bfloat16

D_MODEL = 1024
D_FF = 2816
HEAD_DIM = 64
N_HEADS = 8
N_PAIRS = N_HEADS // 2
D_ATTN = 512
D_POOL = 256
D_CONV = 256
POOL_WINDOWS = (2, 4, 8, 16)
POOL_GROUP = 64
D_IN = 2568
RMS_EPS = 1e-6
ADAM_LR, ADAM_B1, ADAM_B2, ADAM_EPS, ADAM_WD, ADAM_STEP = 0.001, 0.9, 0.999, 1e-08, 0.01, 10

N_DEV = 8
N_CHIPS = 4
LANES = 128
VMEM_BYTES_V7X = 64 * 1024 * 1024
VMEM_LIMIT_MAX = VMEM_BYTES_V7X - 8 * 1024 * 1024

F_HALF = D_FF // 2
D_QKV = 3 * D_ATTN
D_REST = D_POOL + 3 * D_CONV
D_INP = D_QKV + D_REST + LANES
MIX_ROWS = 321
MIX_ROWS_PAD = 336
FFN_ROWS = 704
OUT_ROWS = 352
MO_ROWS = 128
LAYER_ROWS = 2 * (FFN_ROWS + OUT_ROWS) + MIX_ROWS_PAD + MO_ROWS
NEG_BIG = -1e30
ATT_SCALE = HEAD_DIM ** -0.5
ATT_K = 2 * LANES
ATT_TILE = 256
ATT_PAIRS_FWD = 4
ATT_PAIRS_BWD = 2


def _cparams(sem, vmem_bytes):
    limit = int(min(max(vmem_bytes, 16 * 1024 * 1024), VMEM_LIMIT_MAX))
    return pltpu.CompilerParams(dimension_semantics=sem, vmem_limit_bytes=limit)


def _nbytes(shape, dtype):
    n = 1
    for s in shape:
        n *= s
    return n * jnp.dtype(dtype).itemsize


def _pick(n, prefs):
    for p in prefs:
        if n % p == 0:
            return p
    return n


def _rmsnorm_fwd(x, g, name):
    t, d = x.shape
    tm = _pick(t, (512, 256, 128))

    def body(x_ref, g_ref, o_ref):
        xv = x_ref[...]
        r = lax.rsqrt(jnp.mean(xv * xv, axis=-1, keepdims=True) + RMS_EPS)
        o_ref[...] = ((xv * r) * g_ref[...]).astype(o_ref.dtype)

    return pl.pallas_call(
        body, grid=(t // tm,),
        in_specs=[pl.BlockSpec((tm, d), lambda i: (i, 0)), pl.BlockSpec((1, d), lambda i: (0, 0))],
        out_specs=pl.BlockSpec((tm, d), lambda i: (i, 0)),
        out_shape=jax.ShapeDtypeStruct((t, d), BF16), name=name,
        compiler_params=_cparams(("parallel",), 6 * tm * d * 4),
    )(x, g)


def _mm_nn(a, b, *, out_dtype, name, res=None, alpha=1.0, tn=None):
    m, k = a.shape
    n = b.shape[1]
    tn = n if tn is None else tn
    tm = _pick(m, (512, 256, 128))
    with_res = res is not None

    def body(*refs):
        if with_res:
            a_ref, b_ref, r_ref, o_ref = refs
        else:
            a_ref, b_ref, o_ref = refs
        acc = jnp.dot(a_ref[...], b_ref[...], preferred_element_type=F32)
        if with_res:
            acc = r_ref[...] + alpha * acc
        o_ref[...] = acc.astype(o_ref.dtype)

    in_specs = [pl.BlockSpec((tm, k), lambda j, i: (i, 0)), pl.BlockSpec((k, tn), lambda j, i: (0, j))]
    args = [a, b]
    if with_res:
        in_specs.append(pl.BlockSpec((tm, tn), lambda j, i: (i, j)))
        args.append(res)
    vmem = 2 * (_nbytes((tm, k), BF16) + _nbytes((k, tn), BF16) + 3 * _nbytes((tm, tn), F32))
    return pl.pallas_call(
        body, grid=(n // tn, m // tm), in_specs=in_specs,
        out_specs=pl.BlockSpec((tm, tn), lambda j, i: (i, j)),
        out_shape=jax.ShapeDtypeStruct((m, n), out_dtype), name=name,
        compiler_params=_cparams(("parallel", "parallel"), vmem),
    )(*args)


def _mm_tn(a, b, *, name, alpha=1.0, tm=None):
    t, m = a.shape
    n = b.shape[1]
    tm = m if tm is None else tm
    tk = _pick(t, (1024, 512, 256, 128))
    nk = t // tk

    def body(a_ref, b_ref, o_ref):
        kk = pl.program_id(1)
        p = lax.dot_general(a_ref[...], b_ref[...], (((0,), (0,)), ((), ())), preferred_element_type=F32)
        if alpha != 1.0:
            p = alpha * p

        @pl.when(kk == 0)
        def _():
            o_ref[...] = p

        @pl.when(kk > 0)
        def _():
            o_ref[...] += p

    vmem = 2 * (_nbytes((tk, tm), BF16) + _nbytes((tk, n), BF16) + 2 * _nbytes((tm, n), F32))
    return pl.pallas_call(
        body, grid=(m // tm, nk),
        in_specs=[pl.BlockSpec((tk, tm), lambda i, kk: (kk, i)), pl.BlockSpec((tk, n), lambda i, kk: (kk, 0))],
        out_specs=pl.BlockSpec((tm, n), lambda i, kk: (i, 0)),
        out_shape=jax.ShapeDtypeStruct((m, n), F32), name=name,
        compiler_params=_cparams(("parallel", "arbitrary"), vmem),
    )(a, b)


def _sigmoid(v):
    return 1.0 / (1.0 + jnp.exp(-v))


def _ffn_in(xn, w, name):
    t, d = xn.shape
    tm = _pick(t, (512, 256, 128))

    def body(x_ref, wg_ref, wu_ref, h_ref, zg_ref, zu_ref):
        xv = x_ref[...]
        g = jnp.dot(xv, wg_ref[...], preferred_element_type=F32)
        u = jnp.dot(xv, wu_ref[...], preferred_element_type=F32)
        h_ref[...] = ((g * _sigmoid(g)) * u).astype(h_ref.dtype)
        zg_ref[...] = g.astype(zg_ref.dtype)
        zu_ref[...] = u.astype(zu_ref.dtype)

    vmem = 2 * (_nbytes((tm, d), BF16) + 2 * _nbytes((d, F_HALF), BF16) + 4 * _nbytes((tm, D_FF), F32))
    out_blk = pl.BlockSpec((tm, F_HALF), lambda j, i: (i, j))
    sds = jax.ShapeDtypeStruct((t, D_FF), BF16)
    return pl.pallas_call(
        body, grid=(2, t // tm),
        in_specs=[pl.BlockSpec((tm, d), lambda j, i: (i, 0)), pl.BlockSpec((d, F_HALF), lambda j, i: (0, j)),
                  pl.BlockSpec((d, F_HALF), lambda j, i: (0, 2 + j))],
        out_specs=[out_blk, out_blk, out_blk], out_shape=[sds, sds, sds], name=name,
        compiler_params=_cparams(("parallel", "parallel"), vmem),
    )(xn, w, w)


def _ffn_bwd_mid(dxo, w_out_t, zg, zu, name):
    t, d = dxo.shape
    tm = _pick(t, (512, 256, 128))

    def body(d_ref, w_ref, zg_ref, zu_ref, dg_ref, du_ref):
        dh = 0.5 * jnp.dot(d_ref[...], w_ref[...], preferred_element_type=F32)
        g = zg_ref[...].astype(F32)
        u = zu_ref[...].astype(F32)
        s = _sigmoid(g)
        dg_ref[...] = (dh * u * (s * (1.0 + g * (1.0 - s)))).astype(dg_ref.dtype)
        du_ref[...] = (dh * (g * s)).astype(du_ref.dtype)

    vmem = 2 * (_nbytes((tm, d), BF16) + _nbytes((d, F_HALF), BF16) + 5 * _nbytes((tm, D_FF), F32))
    blk = pl.BlockSpec((tm, F_HALF), lambda j, i: (i, j))
    sds = jax.ShapeDtypeStruct((t, D_FF), BF16)
    return pl.pallas_call(
        body, grid=(2, t // tm),
        in_specs=[pl.BlockSpec((tm, d), lambda j, i: (i, 0)), pl.BlockSpec((d, F_HALF), lambda j, i: (0, j)), blk, blk],
        out_specs=[blk, blk], out_shape=[sds, sds], name=name,
        compiler_params=_cparams(("parallel", "parallel"), vmem),
    )(dxo, w_out_t, zg, zu)


def _mm_nn2(a1, a2, b, name):
    m, k = a1.shape
    n = b.shape[1]
    tm = _pick(m, (512, 256, 128))

    def body(a1_ref, a2_ref, b1_ref, b2_ref, o_ref):
        o_ref[...] = (jnp.dot(a1_ref[...], b1_ref[...], preferred_element_type=F32)
                      + jnp.dot(a2_ref[...], b2_ref[...], preferred_element_type=F32))

    a_blk = pl.BlockSpec((tm, k), lambda i: (i, 0))
    vmem = 2 * (2 * _nbytes((tm, k), BF16) + 2 * _nbytes((k, n), BF16) + 3 * _nbytes((tm, n), F32))
    return pl.pallas_call(
        body, grid=(m // tm,),
        in_specs=[a_blk, a_blk, pl.BlockSpec((k, n), lambda i: (0, 0)), pl.BlockSpec((k, n), lambda i: (1, 0))],
        out_specs=pl.BlockSpec((tm, n), lambda i: (i, 0)), out_shape=jax.ShapeDtypeStruct((m, n), F32), name=name,
        compiler_params=_cparams(("parallel",), vmem),
    )(a1, a2, b, b)


def _rmsnorm_bwd(x, g, dxn, dxo, name):
    t, d = x.shape
    tm = _pick(t, (512, 256, 128))

    def body(x_ref, g_ref, dn_ref, do_ref, dx_ref, dxb_ref, dg_ref):
        i = pl.program_id(0)
        xv = x_ref[...]
        r = lax.rsqrt(jnp.mean(xv * xv, axis=-1, keepdims=True) + RMS_EPS)
        xh = xv * r
        dn = dn_ref[...]
        dgp = jnp.sum(dn * xh, axis=0, keepdims=True)
        dh = dn * g_ref[...]
        dx = do_ref[...] + r * (dh - xh * jnp.mean(dh * xh, axis=-1, keepdims=True))
        dx_ref[...] = dx
        dxb_ref[...] = dx.astype(dxb_ref.dtype)

        @pl.when(i == 0)
        def _():
            dg_ref[...] = dgp

        @pl.when(i > 0)
        def _():
            dg_ref[...] += dgp

    blk = pl.BlockSpec((tm, d), lambda i: (i, 0))
    row = pl.BlockSpec((1, d), lambda i: (0, 0))
    return pl.pallas_call(
        body, grid=(t // tm,), in_specs=[blk, row, blk, blk], out_specs=[blk, blk, row],
        out_shape=[jax.ShapeDtypeStruct((t, d), F32), jax.ShapeDtypeStruct((t, d), BF16),
                   jax.ShapeDtypeStruct((1, d), F32)], name=name,
        compiler_params=_cparams(("arbitrary",), 16 * tm * d * 4),
    )(x, g, dxn, dxo)


def _final_loss_bwd(x, g, tgt):
    t, d = x.shape
    tm = _pick(t, (512, 256, 128))

    def body(x_ref, g_ref, t_ref, dx_ref, dxb_ref, dg_ref, loss_ref):
        i = pl.program_id(0)
        xv = x_ref[...]
        r = lax.rsqrt(jnp.mean(xv * xv, axis=-1, keepdims=True) + RMS_EPS)
        xh = xv * r
        gv = g_ref[...]
        err = xh * gv - t_ref[...]
        lp = 0.5 * jnp.sum(jnp.mean(err * err, axis=-1, keepdims=True), axis=0, keepdims=True)
        dy = err * (1.0 / d)
        dgp = jnp.sum(dy * xh, axis=0, keepdims=True)
        dh = dy * gv
        dx = r * (dh - xh * jnp.mean(dh * xh, axis=-1, keepdims=True))
        dx_ref[...] = dx
        dxb_ref[...] = dx.astype(dxb_ref.dtype)
        lpb = jnp.broadcast_to(lp, (1, LANES))

        @pl.when(i == 0)
        def _():
            dg_ref[...] = dgp
            loss_ref[...] = lpb

        @pl.when(i > 0)
        def _():
            dg_ref[...] += dgp
            loss_ref[...] += lpb

    blk = pl.BlockSpec((tm, d), lambda i: (i, 0))
    row = pl.BlockSpec((1, d), lambda i: (0, 0))
    return pl.pallas_call(
        body, grid=(t // tm,), in_specs=[blk, row, blk],
        out_specs=[blk, blk, row, pl.BlockSpec((1, LANES), lambda i: (0, 0))],
        out_shape=[jax.ShapeDtypeStruct((t, d), F32), jax.ShapeDtypeStruct((t, d), BF16),
                   jax.ShapeDtypeStruct((1, d), F32), jax.ShapeDtypeStruct((1, LANES), F32)], name="final_loss_bwd",
        compiler_params=_cparams(("arbitrary",), 16 * tm * d * 4),
    )(x, g, tgt)


def _seq_scan(v, seq, reverse):
    row = lax.broadcasted_iota(jnp.int32, v.shape, 0)
    k = 1
    while k < seq:
        if reverse:
            v = v + jnp.where(row < seq - k, pltpu.roll(v, seq - k, 0), 0.0)
        else:
            v = v + jnp.where(row >= k, pltpu.roll(v, k, 0), 0.0)
        k *= 2
    return v


def _log_sigmoid(v):
    return jnp.minimum(v, 0.0) - jnp.log(1.0 + jnp.exp(-jnp.abs(v)))


def _fox_prep(fl, bf, qkv, nb, seq):
    def body(f_ref, b_ref, q_ref, k_ref, v_ref, qa_ref, ka_ref, vm_ref):
        dsum = _seq_scan(_log_sigmoid(f_ref[...] + b_ref[...]), seq, False)
        d1 = dsum.astype(BF16).astype(F32)
        r1 = dsum - d1
        d2 = r1.astype(BF16).astype(F32)
        d3 = (r1 - d2).astype(BF16).astype(F32)
        lane = lax.broadcasted_iota(jnp.int32, (seq, LANES), 1)
        first = lane < HEAD_DIM
        l64 = jnp.where(first, lane, lane - HEAD_DIM)
        for p in range(N_PAIRS):
            def head_cols(a, p=p):
                return jnp.where(first, a[:, 2 * p:2 * p + 1], a[:, 2 * p + 1:2 * p + 2])

            e1, e2, e3 = head_cols(d1), head_cols(d2), head_cols(d3)
            aux_q = jnp.where(l64 == 0, e1, jnp.where(l64 == 1, e2, jnp.where(l64 == 2, e3,
                              jnp.where(l64 < 6, 1.0, 0.0)))).astype(BF16)
            aux_k = jnp.where(l64 < 3, 1.0, jnp.where(l64 == 3, -e1, jnp.where(l64 == 4, -e2,
                              jnp.where(l64 == 5, -e3, 0.0)))).astype(BF16)
            cols = slice(LANES * p, LANES * (p + 1))
            qs = q_ref[:, cols] * ATT_SCALE
            vp = v_ref[:, cols]
            zero = jnp.zeros_like(qs)
            qa_ref[0, p, 0, :, :LANES] = jnp.where(first, qs, zero)
            qa_ref[0, p, 0, :, LANES:] = jnp.where(first, aux_q, zero)
            qa_ref[0, p, 1, :, :LANES] = jnp.where(first, zero, qs)
            qa_ref[0, p, 1, :, LANES:] = jnp.where(first, zero, aux_q)
            ka_ref[0, p, :, :LANES] = k_ref[:, cols]
            ka_ref[0, p, :, LANES:] = aux_k
            vm_ref[0, p, 0] = jnp.where(first, vp, zero)
            vm_ref[0, p, 1] = jnp.where(first, zero, vp)

    def part(c):
        return pl.BlockSpec((seq, D_ATTN), lambda b, c=c: (b, c))

    return pl.pallas_call(
        body, grid=(nb,),
        in_specs=[pl.BlockSpec((seq, LANES), lambda b: (b, 0)), pl.BlockSpec((1, LANES), lambda b: (0, 0)),
                  part(0), part(1), part(2)],
        out_specs=[pl.BlockSpec((1, N_PAIRS, 2, seq, ATT_K), lambda b: (b, 0, 0, 0, 0)),
                   pl.BlockSpec((1, N_PAIRS, seq, ATT_K), lambda b: (b, 0, 0, 0)),
                   pl.BlockSpec((1, N_PAIRS, 2, seq, LANES), lambda b: (b, 0, 0, 0, 0))],
        out_shape=[jax.ShapeDtypeStruct((nb, N_PAIRS, 2, seq, ATT_K), BF16),
                   jax.ShapeDtypeStruct((nb, N_PAIRS, seq, ATT_K), BF16),
                   jax.ShapeDtypeStruct((nb, N_PAIRS, 2, seq, LANES), BF16)],
        name="fox_prep", compiler_params=_cparams(("parallel",), 48 * 1024 * 1024),
    )(fl, bf, qkv, qkv, qkv)


def _fox_prep_bwd(dd, fl, bf, seq):
    t = fl.shape[0]

    def body(d_ref, f_ref, b_ref, o_ref, db_ref):
        i = pl.program_id(0)
        dlog = _seq_scan(d_ref[...], seq, True)
        dfl = dlog * _sigmoid(-(f_ref[...] + b_ref[...]))
        o_ref[...] = dfl.astype(o_ref.dtype)
        dbp = jnp.sum(dfl, axis=0, keepdims=True)

        @pl.when(i == 0)
        def _():
            db_ref[...] = dbp

        @pl.when(i > 0)
        def _():
            db_ref[...] += dbp

    blk = pl.BlockSpec((seq, LANES), lambda b: (b, 0))
    row = pl.BlockSpec((1, LANES), lambda b: (0, 0))
    return pl.pallas_call(
        body, grid=(t // seq,), in_specs=[blk, blk, row], out_specs=[blk, row],
        out_shape=[jax.ShapeDtypeStruct((t, LANES), BF16), jax.ShapeDtypeStruct((1, LANES), F32)], name="fox_prep_bwd",
        compiler_params=_cparams(("arbitrary",), 24 * seq * LANES * 4),
    )(dd, fl, bf)


def _pair_rows(a, ta):
    lane = lax.broadcasted_iota(jnp.int32, (ta, LANES), 1)
    return jnp.where(lane < HEAD_DIM, a[:ta], a[ta:])


def _diag_mask(ta):
    r = lax.broadcasted_iota(jnp.int32, (2 * ta, ta), 0)
    c = lax.broadcasted_iota(jnp.int32, (2 * ta, ta), 1)
    return c <= jnp.where(r >= ta, r - ta, r)


def _nt(a, b):
    return lax.dot_general(a, b, (((1,), (1,)), ((), ())), preferred_element_type=F32)


def _tn(a, b):
    return lax.dot_general(a, b, (((0,), (0,)), ((), ())), preferred_element_type=F32)


def _fox_fwd(qa, ka, vm, nb, seq, ta):
    nq = seq // ta
    npp = ATT_PAIRS_FWD

    def body(q_ref, k_ref, v_ref, o_ref, lse_ref):
        i = pl.program_id(2)
        q2s = [q_ref[0, pp].reshape(2 * ta, ATT_K) for pp in range(npp)]

        def step(j, carry, masked):
            rows = pl.ds(pl.multiple_of(j * ta, ta), ta)
            out = []
            for pp in range(npp):
                m, l, acc = carry[pp]
                s = _nt(q2s[pp], k_ref[0, pp, rows, :])
                if masked:
                    s = jnp.where(_diag_mask(ta), s, NEG_BIG)
                m_new = jnp.maximum(m, jnp.max(s, axis=-1, keepdims=True))
                p = jnp.exp(s - m_new)
                corr = jnp.exp(m - m_new)
                l = corr * l + jnp.sum(p, axis=-1, keepdims=True)
                pb = p.astype(BF16)
                pv = (jnp.dot(pb[:ta], v_ref[0, pp, 0, rows, :], preferred_element_type=F32)
                      + jnp.dot(pb[ta:], v_ref[0, pp, 1, rows, :], preferred_element_type=F32))
                out.append((m_new, l, _pair_rows(corr, ta) * acc + pv))
            return tuple(out)

        init = tuple((jnp.full((2 * ta, 1), NEG_BIG, F32), jnp.zeros((2 * ta, 1), F32),
                      jnp.zeros((ta, LANES), F32)) for _ in range(npp))
        carry = lax.fori_loop(0, i, functools.partial(step, masked=False), init)
        for pp, (m, l, acc) in enumerate(step(i, carry, True)):
            o_ref[:, LANES * pp:LANES * (pp + 1)] = (acc * _pair_rows(1.0 / l, ta)).astype(o_ref.dtype)
            lse = m + jnp.log(l)
            lse_ref[0, pp, 0] = lse[:ta]
            lse_ref[0, pp, 1] = lse[ta:]

    vmem = (2 * npp * (_nbytes((seq, ATT_K), BF16) + 2 * _nbytes((seq, LANES), BF16)) + 24 * npp * ta * ta * 4
            + 8 * 1024 * 1024)
    return pl.pallas_call(
        body, grid=(nb, N_PAIRS // npp, nq),
        in_specs=[pl.BlockSpec((1, npp, 2, ta, ATT_K), lambda b, g, i: (b, g, 0, i, 0)),
                  pl.BlockSpec((1, npp, seq, ATT_K), lambda b, g, i: (b, g, 0, 0)),
                  pl.BlockSpec((1, npp, 2, seq, LANES), lambda b, g, i: (b, g, 0, 0, 0))],
        out_specs=[pl.BlockSpec((ta, LANES * npp), lambda b, g, i: (b * nq + i, g)),
                   pl.BlockSpec((1, npp, 2, ta, 1), lambda b, g, i: (b, g, 0, i, 0))],
        out_shape=[jax.ShapeDtypeStruct((nb * seq, D_ATTN), BF16),
                   jax.ShapeDtypeStruct((nb, N_PAIRS, 2, seq, 1), F32)],
        name="fox_fwd", compiler_params=_cparams(("parallel", "parallel", "parallel"), vmem),
    )(qa, ka, vm)


def _fox_bwd(qa, ka, vm, y, dy, lse, nb, seq, ta):
    nq = seq // ta
    npp = ATT_PAIRS_BWD

    def body(q_ref, k_ref, v_ref, o_ref, do_ref, lse_ref, dq_ref, dk_ref, dv_ref, rs_ref, cs_ref):
        i = pl.program_id(2)

        @pl.when(i == 0)
        def _():
            dk_ref[...] = jnp.zeros_like(dk_ref)
            dv_ref[...] = jnp.zeros_like(dv_ref)
            cs_ref[...] = jnp.zeros_like(cs_ref)

        first = lax.broadcasted_iota(jnp.int32, (ta, LANES), 1) < HEAD_DIM
        q2s, do2s, deltas, lses = [], [], [], []
        for pp in range(npp):
            cols = slice(LANES * pp, LANES * (pp + 1))
            q2s.append(q_ref[0, pp].reshape(2 * ta, ATT_K))
            do = do_ref[:, cols]
            doo = do * o_ref[:, cols].astype(F32)
            do2s.append(jnp.concatenate([jnp.where(first, do, 0.0), jnp.where(first, 0.0, do)], axis=0).astype(BF16))
            deltas.append(jnp.concatenate([jnp.sum(jnp.where(first, doo, 0.0), axis=-1, keepdims=True),
                                           jnp.sum(jnp.where(first, 0.0, doo), axis=-1, keepdims=True)], axis=0))
            lses.append(jnp.concatenate([lse_ref[0, pp, 0], lse_ref[0, pp, 1]], axis=0))

        def step(j, carry, masked):
            rows = pl.ds(pl.multiple_of(j * ta, ta), ta)
            out = []
            for pp in range(npp):
                dq_acc, rs_acc = carry[pp]
                cols = slice(LANES * pp, LANES * (pp + 1))
                ks = k_ref[0, pp, rows, :]
                s = _nt(q2s[pp], ks)
                if masked:
                    s = jnp.where(_diag_mask(ta), s, NEG_BIG)
                p = jnp.exp(s - lses[pp])
                dp = _nt(do2s[pp], v_ref[0, pp, 0, rows, :] + v_ref[0, pp, 1, rows, :])
                ds32 = p * (dp - deltas[pp])
                ds = ds32.astype(BF16)
                dk_ref[rows, cols] += _tn(ds, q2s[pp][:, :LANES])
                dv_ref[rows, cols] += _tn(p.astype(BF16), do2s[pp])
                cs_ref[0, pp, 0, j] += jnp.sum(ds32[:ta], axis=0, keepdims=True)
                cs_ref[0, pp, 1, j] += jnp.sum(ds32[ta:], axis=0, keepdims=True)
                out.append((dq_acc + jnp.dot(ds, ks[:, :LANES], preferred_element_type=F32),
                            rs_acc + jnp.sum(ds32, axis=-1, keepdims=True)))
            return tuple(out)

        init = tuple((jnp.zeros((2 * ta, LANES), F32), jnp.zeros((2 * ta, 1), F32)) for _ in range(npp))
        carry = lax.fori_loop(0, i, functools.partial(step, masked=False), init)
        for pp, (dq_acc, rs_acc) in enumerate(step(i, carry, True)):
            dq = jnp.where(first, dq_acc[:ta], dq_acc[ta:]) * ATT_SCALE
            dq_ref[:, LANES * pp:LANES * (pp + 1)] = dq.astype(dq_ref.dtype)
            rs_ref[0, pp, 0] = rs_acc[:ta]
            rs_ref[0, pp, 1] = rs_acc[ta:]

    vmem = (2 * npp * (_nbytes((seq, ATT_K), BF16) + 2 * _nbytes((seq, LANES), BF16) + 2 * _nbytes((seq, LANES), F32))
            + 32 * npp * ta * ta * 4 + 8 * 1024 * 1024)
    qblk = lambda b, g, i: (b * nq + i, g)
    acc_blk = pl.BlockSpec((seq, LANES * npp), lambda b, g, i: (b, g))
    col_blk = pl.BlockSpec((1, npp, 2, ta, 1), lambda b, g, i: (b, g, 0, i, 0))
    return pl.pallas_call(
        body, grid=(nb, N_PAIRS // npp, nq),
        in_specs=[pl.BlockSpec((1, npp, 2, ta, ATT_K), lambda b, g, i: (b, g, 0, i, 0)),
                  pl.BlockSpec((1, npp, seq, ATT_K), lambda b, g, i: (b, g, 0, 0)),
                  pl.BlockSpec((1, npp, 2, seq, LANES), lambda b, g, i: (b, g, 0, 0, 0)),
                  pl.BlockSpec((ta, LANES * npp), qblk), pl.BlockSpec((ta, LANES * npp), qblk), col_blk],
        out_specs=[pl.BlockSpec((ta, LANES * npp), qblk), acc_blk, acc_blk, col_blk,
                   pl.BlockSpec((1, npp, 2, nq, 1, ta), lambda b, g, i: (b, g, 0, 0, 0, 0))],
        out_shape=[jax.ShapeDtypeStruct((nb * seq, D_ATTN), BF16), jax.ShapeDtypeStruct((nb * seq, D_ATTN), F32),
                   jax.ShapeDtypeStruct((nb * seq, D_ATTN), F32), jax.ShapeDtypeStruct((nb, N_PAIRS, 2, seq, 1), F32),
                   jax.ShapeDtypeStruct((nb, N_PAIRS, 2, nq, 1, ta), F32)],
        name="fox_bwd", compiler_params=_cparams(("parallel", "parallel", "arbitrary"), vmem),
    )(qa, ka, vm, y, dy, lse)


def _shift_down(a, k):
    row = lax.broadcasted_iota(jnp.int32, a.shape, 0)
    return jnp.where(row >= k, pltpu.roll(a, k, 0), 0.0)


def _shift_up(a, k):
    n = a.shape[0]
    row = lax.broadcasted_iota(jnp.int32, a.shape, 0)
    return jnp.where(row < n - k, pltpu.roll(a, n - k, 0), 0.0)


def _by_group(vals, shape):
    lane = lax.broadcasted_iota(jnp.int32, shape, 1)
    out = vals[-1]
    for gi in range(len(vals) - 2, -1, -1):
        out = jnp.where(lane < POOL_GROUP * (gi + 1), vals[gi], out)
    return out


def _pooled(u):
    s2 = u + _shift_down(u, 1)
    s4 = s2 + _shift_down(s2, 2)
    s8 = s4 + _shift_down(s4, 4)
    s16 = s8 + _shift_down(s8, 8)
    win = _by_group([s2, s4, s8, s16], u.shape)
    row = lax.broadcasted_iota(jnp.int32, u.shape, 0)
    wsize = _by_group([jnp.full(u.shape, w, jnp.int32) for w in POOL_WINDOWS], u.shape)
    inv = 1.0 / jnp.minimum(row + 1, wsize).astype(F32)
    return win * inv - u, inv


def _pool_fwd(rest, wbd, scale, seq):
    t = rest.shape[0]

    def body(u_ref, w_ref, s_ref, o_ref):
        pooled, _ = _pooled(u_ref[...])
        pw = jnp.dot(pooled.astype(BF16), w_ref[...], preferred_element_type=F32)
        o_ref[...] = (pw * s_ref[...]).astype(o_ref.dtype)

    blk = pl.BlockSpec((seq, D_POOL), lambda b: (b, 0))
    return pl.pallas_call(
        body, grid=(t // seq,),
        in_specs=[blk, pl.BlockSpec((D_POOL, D_POOL), lambda b: (0, 0)), pl.BlockSpec((1, D_POOL), lambda b: (0, 0))],
        out_specs=blk, out_shape=jax.ShapeDtypeStruct((t, D_POOL), BF16), name="pool_fwd",
        compiler_params=_cparams(("parallel",), 24 * seq * D_POOL * 4),
    )(rest, wbd, scale)


def _pool_bwd(rest, dy, wbd, wbd_t, scale, seq):
    t = rest.shape[0]

    def body(u_ref, dy_ref, w_ref, wt_ref, s_ref, du_ref, dw_ref, dsc_ref):
        i = pl.program_id(0)
        pooled, inv = _pooled(u_ref[...])
        pb = pooled.astype(BF16)
        pw = jnp.dot(pb, w_ref[...], preferred_element_type=F32)
        dyp = dy_ref[...]
        dsp = jnp.sum(dyp * pw, axis=0, keepdims=True)
        dpw = (dyp * s_ref[...]).astype(BF16)
        dwp = _tn(pb, dpw)
        dpooled = jnp.dot(dpw, wt_ref[...], preferred_element_type=F32)
        dwin = dpooled * inv
        t2 = dwin + _shift_up(dwin, 1)
        t4 = t2 + _shift_up(t2, 2)
        t8 = t4 + _shift_up(t4, 4)
        t16 = t8 + _shift_up(t8, 8)
        du_ref[...] = (_by_group([t2, t4, t8, t16], dwin.shape) - dpooled).astype(du_ref.dtype)

        @pl.when(i == 0)
        def _():
            dw_ref[...] = dwp
            dsc_ref[...] = dsp

        @pl.when(i > 0)
        def _():
            dw_ref[...] += dwp
            dsc_ref[...] += dsp

    blk = pl.BlockSpec((seq, D_POOL), lambda b: (b, 0))
    sq = pl.BlockSpec((D_POOL, D_POOL), lambda b: (0, 0))
    row = pl.BlockSpec((1, D_POOL), lambda b: (0, 0))
    return pl.pallas_call(
        body, grid=(t // seq,),
        in_specs=[blk, pl.BlockSpec((seq, D_POOL), lambda b: (b, 2)), sq, sq, row],
        out_specs=[blk, sq, row],
        out_shape=[jax.ShapeDtypeStruct((t, D_POOL), BF16), jax.ShapeDtypeStruct((D_POOL, D_POOL), F32),
                   jax.ShapeDtypeStruct((1, D_POOL), F32)], name="pool_bwd",
        compiler_params=_cparams(("arbitrary",), 40 * seq * D_POOL * 4),
    )(rest, dy, wbd, wbd_t, scale)


def _conv_fwd(rest, cw, seq):
    t = rest.shape[0]

    def body(cb_ref, cc_ref, ch_ref, w_ref, o_ref):
        u = cc_ref[...] * ch_ref[...]
        y = w_ref[0:1, :] * _shift_down(u, 2) + w_ref[1:2, :] * _shift_down(u, 1) + w_ref[2:3, :] * u
        o_ref[...] = (cb_ref[...] * y).astype(o_ref.dtype)

    def col(c):
        return pl.BlockSpec((seq, D_CONV), lambda b, c=c: (b, c))

    return pl.pallas_call(
        body, grid=(t // seq,), in_specs=[col(1), col(2), col(3), pl.BlockSpec((8, D_CONV), lambda b: (0, 0))],
        out_specs=pl.BlockSpec((seq, D_CONV), lambda b: (b, 0)),
        out_shape=jax.ShapeDtypeStruct((t, D_CONV), BF16), name="conv_fwd",
        compiler_params=_cparams(("parallel",), 24 * seq * D_CONV * 4),
    )(rest, rest, rest, cw)


def _conv_bwd(rest, dy, cw, seq):
    t = rest.shape[0]

    def body(cb_ref, cc_ref, ch_ref, dy_ref, w_ref, o_ref, dw_ref):
        i = pl.program_id(0)
        cc = cc_ref[...]
        ch = ch_ref[...]
        u = cc * ch
        u1 = _shift_down(u, 1)
        u2 = _shift_down(u, 2)
        y = w_ref[0:1, :] * u2 + w_ref[1:2, :] * u1 + w_ref[2:3, :] * u
        dyc = dy_ref[...]
        d2 = dyc * cb_ref[...]
        du = w_ref[0:1, :] * _shift_up(d2, 2) + w_ref[1:2, :] * _shift_up(d2, 1) + w_ref[2:3, :] * d2
        o_ref[:, 0:D_CONV] = (dyc * y).astype(o_ref.dtype)
        o_ref[:, D_CONV:2 * D_CONV] = (du * ch).astype(o_ref.dtype)
        o_ref[:, 2 * D_CONV:3 * D_CONV] = (du * cc).astype(o_ref.dtype)
        tap = lax.broadcasted_iota(jnp.int32, (8, D_CONV), 0)
        dwp = jnp.where(tap == 0, jnp.sum(d2 * u2, axis=0, keepdims=True),
                        jnp.where(tap == 1, jnp.sum(d2 * u1, axis=0, keepdims=True),
                                  jnp.where(tap == 2, jnp.sum(d2 * u, axis=0, keepdims=True), 0.0)))

        @pl.when(i == 0)
        def _():
            dw_ref[...] = dwp

        @pl.when(i > 0)
        def _():
            dw_ref[...] += dwp

    def col(c):
        return pl.BlockSpec((seq, D_CONV), lambda b, c=c: (b, c))

    taps = pl.BlockSpec((8, D_CONV), lambda b: (0, 0))
    return pl.pallas_call(
        body, grid=(t // seq,), in_specs=[col(1), col(2), col(3), col(3), taps],
        out_specs=[pl.BlockSpec((seq, 3 * D_CONV), lambda b: (b, 0)), taps],
        out_shape=[jax.ShapeDtypeStruct((t, 3 * D_CONV), BF16), jax.ShapeDtypeStruct((8, D_CONV), F32)],
        name="conv_bwd", compiler_params=_cparams(("arbitrary",), 48 * seq * D_CONV * 4),
    )(rest, rest, rest, dy, cw)


def _adamw(w, g, m, v, name):
    r, c = w.shape
    tr = _pick(r, (512, 352, 256, 128)) if r > 512 else r

    def body(w_ref, g_ref, m_ref, v_ref, d_ref, mo_ref, vo_ref):
        gv = g_ref[...]
        mn = ADAM_B1 * m_ref[...] + (1.0 - ADAM_B1) * gv
        vn = ADAM_B2 * v_ref[...] + (1.0 - ADAM_B2) * (gv * gv)
        m_hat = mn / (1.0 - ADAM_B1 ** ADAM_STEP)
        v_hat = vn / (1.0 - ADAM_B2 ** ADAM_STEP)
        d_ref[...] = -ADAM_LR * (m_hat / (jnp.sqrt(v_hat) + ADAM_EPS) + ADAM_WD * w_ref[...])
        mo_ref[...] = mn
        vo_ref[...] = vn

    blk = pl.BlockSpec((tr, c), lambda i: (i, 0))
    sds = jax.ShapeDtypeStruct((r, c), F32)
    return pl.pallas_call(
        body, grid=(r // tr,), in_specs=[blk] * 4, out_specs=[blk] * 3, out_shape=[sds] * 3, name=name,
        compiler_params=_cparams(("parallel",), 20 * tr * max(c, LANES) * 4),
    )(w, g, m, v)


def _sum_slots(a, name):
    ns, r, c = a.shape
    tr = _pick(r, (368, 256, 184, 136, 128, 88, 8))

    def body(a_ref, o_ref):
        acc = a_ref[0].astype(F32)
        for s in range(1, ns):
            acc = acc + a_ref[s].astype(F32)
        o_ref[...] = acc

    return pl.pallas_call(
        body, grid=(r // tr,), in_specs=[pl.BlockSpec((ns, tr, c), lambda i: (0, i, 0))],
        out_specs=pl.BlockSpec((tr, c), lambda i: (i, 0)), out_shape=jax.ShapeDtypeStruct((r, c), F32), name=name,
        compiler_params=_cparams(("parallel",), 4 * (ns + 2) * tr * c * 4),
    )(a)


def _add_core_half(core, g4, theirs, out_dtype, name):
    ns, _, r, c = g4.shape
    tr = _pick(r, (368, 256, 184, 136, 128, 88, 8))

    def body(core_ref, a_ref, b_ref, o_ref):
        o_ref[...] = (a_ref[0] + b_ref[...]).astype(o_ref.dtype)

    blk = pl.BlockSpec((1, tr, c), lambda s, i, core_ref: (s, i, 0))
    return pl.pallas_call(
        body,
        grid_spec=pltpu.PrefetchScalarGridSpec(
            num_scalar_prefetch=1, grid=(ns, r // tr),
            in_specs=[pl.BlockSpec((1, 1, tr, c), lambda s, i, core_ref: (s, core_ref[0], i, 0)), blk],
            out_specs=blk),
        out_shape=jax.ShapeDtypeStruct((ns, r, c), out_dtype), name=name,
        compiler_params=_cparams(("parallel", "parallel"), 10 * tr * c * 4),
    )(core, g4, theirs)


def _sum_chips(order, own, landed, name):
    ns, r, c = own.shape
    tr = _pick(r, (368, 256, 184, 136, 128, 88, 8))

    def body(order_ref, a_ref, b1_ref, b2_ref, b3_ref, o_ref):
        o_ref[...] = ((a_ref[0].astype(F32) + b1_ref[0].astype(F32)) + b2_ref[0].astype(F32)) + b3_ref[0].astype(F32)

    def slot(k):
        return pl.BlockSpec((1, tr, c), lambda i, order_ref, k=k: (order_ref[k], i, 0))

    return pl.pallas_call(
        body,
        grid_spec=pltpu.PrefetchScalarGridSpec(
            num_scalar_prefetch=1, grid=(r // tr,), in_specs=[slot(0), slot(1), slot(2), slot(3)],
            out_specs=pl.BlockSpec((tr, c), lambda i, order_ref: (i, 0))),
        out_shape=jax.ShapeDtypeStruct((r, c), F32), name=name,
        compiler_params=_cparams(("parallel",), 16 * tr * c * 4),
    )(order, own, landed, landed, landed)


def _mesh_pos():
    return lax.axis_index("x"), lax.axis_index("y"), lax.axis_index("c")


def _all_gather(x, name):
    r, c = x.shape

    def body(x_ref, out_ref, send_sems, recv_sems):
        mx, my, mc = _mesh_pos()
        me, sibling = (mx, my, mc), (mx, my, 1 - mc)
        chips = [(1 - mx, my), (mx, 1 - my), (1 - mx, 1 - my)]

        def slot(px, py, pc):
            return out_ref.at[4 * px + 2 * py + pc]

        def copy(k, block, to, src=None):
            return pltpu.make_async_remote_copy(
                src_ref=slot(*block) if src is None else src, dst_ref=slot(*block),
                send_sem=send_sems.at[k], recv_sem=recv_sems.at[k],
                device_id=to, device_id_type=pl.DeviceIdType.MESH)

        first = [copy(0, me, sibling, src=x_ref)]
        first += [copy(1 + j, me, (*chip, mc), src=x_ref) for j, chip in enumerate(chips)]
        for cp in first:
            cp.start()
        passed = [copy(4 + j, (*chip, mc), sibling) for j, chip in enumerate(chips)]
        for j, chip in enumerate(chips):
            copy(1 + j, (*chip, mc), me).wait_recv()
            passed[j].start()
        copy(0, sibling, me).wait_recv()
        for j, chip in enumerate(chips):
            copy(4 + j, (*chip, 1 - mc), me).wait_recv()
        for cp in first + passed:
            cp.wait_send()

    gathered = pl.pallas_call(
        body, out_shape=jax.ShapeDtypeStruct((N_DEV, r, c), x.dtype),
        in_specs=[pl.BlockSpec(memory_space=pl.ANY)], out_specs=pl.BlockSpec(memory_space=pl.ANY),
        scratch_shapes=[pltpu.SemaphoreType.DMA((7,)), pltpu.SemaphoreType.DMA((7,))],
        name=name,
    )(x)
    mx, my, mc = _mesh_pos()
    return lax.dynamic_update_slice_in_dim(gathered, x[None], 4 * mx + 2 * my + mc, axis=0)


def _exchange_sibling(g4, name):
    nchip, _, r, c = g4.shape

    def body(g_ref, theirs_ref, send_sems, recv_sems):
        mx, my, mc = _mesh_pos()
        sends = [pltpu.make_async_remote_copy(
            src_ref=g_ref.at[chip, 1 - mc], dst_ref=theirs_ref.at[chip],
            send_sem=send_sems.at[chip], recv_sem=recv_sems.at[chip],
            device_id=(mx, my, 1 - mc), device_id_type=pl.DeviceIdType.MESH) for chip in range(nchip)]
        for cp in sends:
            cp.start()
        for cp in sends:
            cp.wait_recv()
        for cp in sends:
            cp.wait_send()

    any_spec = pl.BlockSpec(memory_space=pl.ANY)
    return pl.pallas_call(
        body, out_shape=jax.ShapeDtypeStruct((nchip, r, c), g4.dtype), in_specs=[any_spec], out_specs=any_spec,
        scratch_shapes=[pltpu.SemaphoreType.DMA((nchip,)), pltpu.SemaphoreType.DMA((nchip,))],
        name=name,
    )(g4)


def _exchange_chips(ts, name):
    nchip, r, c = ts.shape

    def body(t_ref, out_ref, send_sems, recv_sems):
        mx, my, mc = _mesh_pos()
        my_chip = 2 * mx + my
        copies = []
        for k in range(1, nchip):
            px = 1 - mx if k & 2 else mx
            py = 1 - my if k & 1 else my
            peer_chip = 2 * px + py

            def rdma(dst_slot, px=px, py=py, peer_chip=peer_chip, k=k):
                return pltpu.make_async_remote_copy(
                    src_ref=t_ref.at[peer_chip], dst_ref=out_ref.at[dst_slot],
                    send_sem=send_sems.at[k - 1], recv_sem=recv_sems.at[k - 1],
                    device_id=(px, py, mc), device_id_type=pl.DeviceIdType.MESH)

            copies.append((rdma(my_chip), rdma(peer_chip)))
        for send, _ in copies:
            send.start()
        for _, landed in copies:
            landed.wait_recv()
        for send, _ in copies:
            send.wait_send()

    any_spec = pl.BlockSpec(memory_space=pl.ANY)
    return pl.pallas_call(
        body, out_shape=jax.ShapeDtypeStruct(ts.shape, ts.dtype), in_specs=[any_spec], out_specs=any_spec,
        scratch_shapes=[pltpu.SemaphoreType.DMA((nchip - 1,)), pltpu.SemaphoreType.DMA((nchip - 1,))],
        name=name,
    )(ts)


def _reduce_scatter(g):
    _, r, c = g.shape
    mx, my, mc = _mesh_pos()
    g4 = g.reshape(N_CHIPS, 2, r, c)
    theirs = _exchange_sibling(g4, name="exchange_grads_sibling")
    chip_sums = _add_core_half(jnp.reshape(mc, (1,)).astype(jnp.int32), g4, theirs, BF16, name="add_sibling_grads")
    landed = _exchange_chips(chip_sums, name="exchange_grads_chips")
    order = jnp.stack([2 * mx + my, 2 * (1 - mx) + my, 2 * mx + (1 - my), 2 * (1 - mx) + (1 - my)]).astype(jnp.int32)
    return _sum_chips(order, chip_sums, landed, name="sum_grads")


def _perm_mix_cols(wm):
    f0 = D_QKV
    f1 = f0 + N_HEADS
    pad = jnp.zeros((wm.shape[0], LANES - N_HEADS), wm.dtype)
    return jnp.concatenate([wm[:, :f0], wm[:, f1:], wm[:, f0:f1], pad], axis=1)


def _unperm_mix_rows(gt):
    f0 = D_QKV
    return jnp.concatenate([gt[:f0], gt[f0 + D_REST:f0 + D_REST + N_HEADS], gt[f0:f0 + D_REST]], axis=0)


def _pack_shards(parts, depth, dtype):
    w1i, w1o, wmi, wmo, w2i, w2o = parts
    rows = []
    for l in range(depth):
        rows += [w1i[l].T, w1o[l],
                 jnp.pad(wmi[l].T, ((0, MIX_ROWS_PAD - MIX_ROWS), (0, 0))), wmo[l], w2i[l].T, w2o[l]]
    return jnp.concatenate(rows, axis=0).astype(dtype)


def _layer_weights(wg, l):
    base = l * LAYER_ROWS
    offs = {}
    o = base
    for nm, n in (("f1i", FFN_ROWS), ("f1o", OUT_ROWS), ("mi", MIX_ROWS_PAD), ("mo", MO_ROWS), ("f2i", FFN_ROWS),
                  ("f2o", OUT_ROWS)):
        offs[nm] = (o, n)
        o += n

    def piece(nm, n_used=None):
        o, n = offs[nm]
        return wg[:, o:o + (n if n_used is None else n_used)]

    out = {}
    for tag in ("f1", "f2"):
        wi_t = piece(tag + "i").reshape(2 * D_FF, D_MODEL)
        wo = piece(tag + "o").reshape(D_FF, D_MODEL)
        out[tag] = dict(wi=wi_t.T, wi_t=wi_t, wo=wo, wo_t=wo.T)
    wm = _perm_mix_cols(piece("mi", MIX_ROWS).reshape(D_IN, D_MODEL).T)
    wm_t = wm.T
    wo = piece("mo").reshape(D_MODEL, D_MODEL)
    out["mix"] = dict(w_qkv=wm[:, :D_QKV], w_rest=wm[:, D_QKV:D_QKV + D_REST], w_f=wm[:, D_QKV + D_REST:],
                      wm_t=wm_t, wo=wo, wo_t=wo.T)
    return out


def _layer_grad_rows(gr):
    def halves(gate_t, up_t):
        return jnp.concatenate([gate_t.reshape(N_DEV // 2, FFN_ROWS, D_MODEL),
                                up_t.reshape(N_DEV // 2, FFN_ROWS, D_MODEL)], axis=0)

    gmi = _unperm_mix_rows(gr["mix_in_t"]).reshape(N_DEV, MIX_ROWS, D_MODEL)
    gmi = jnp.pad(gmi, ((0, 0), (0, MIX_ROWS_PAD - MIX_ROWS), (0, 0)))
    return jnp.concatenate(
        [halves(*gr["f1_in_t"]), gr["f1_out"].reshape(N_DEV, OUT_ROWS, D_MODEL), gmi,
         gr["mix_out"].reshape(N_DEV, MO_ROWS, D_MODEL), halves(*gr["f2_in_t"]),
         gr["f2_out"].reshape(N_DEV, OUT_ROWS, D_MODEL)], axis=1)


def _ffn_forward(x, gain, w):
    xn = _rmsnorm_fwd(x, gain, name="ffn_norm")
    h, zg, zu = _ffn_in(xn, w["wi"], name="ffn_in")
    x_new = _mm_nn(h, w["wo"], out_dtype=F32, res=x, alpha=0.5, name="ffn_out")
    return x_new, dict(x=x, xn=xn, h=h, zg=zg, zu=zu)


def _ffn_backward(dxo, dxo_b, gain, w, saved):
    dzg, dzu = _ffn_bwd_mid(dxo_b, w["wo_t"], saved["zg"], saved["zu"], name="ffn_bwd_mid")
    g_out = _mm_tn(saved["h"], dxo_b, alpha=0.5, tm=F_HALF, name="ffn_gw_out")
    g_in_t = (_mm_tn(dzg, saved["xn"], tm=F_HALF, name="ffn_gw_in"), _mm_tn(dzu, saved["xn"], tm=F_HALF, name="ffn_gw_in"))
    dxn = _mm_nn2(dzg, dzu, w["wi_t"], name="ffn_dxn")
    dx, dx_b, dg = _rmsnorm_bwd(saved["x"], gain, dxn, dxo, name="ffn_norm_bwd")
    return dx, dx_b, dg, g_in_t, g_out


def _mixer_forward(x, p, w, nb, seq, ta):
    xn = _rmsnorm_fwd(x, p["norm"], name="mix_norm")
    qkv = _mm_nn(xn, w["w_qkv"], out_dtype=BF16, name="mix_qkv")
    rest = _mm_nn(xn, w["w_rest"], out_dtype=F32, name="mix_rest")
    fl = _mm_nn(xn, w["w_f"], out_dtype=F32, name="mix_f")
    qa, ka, vm = _fox_prep(fl, p["bf"], qkv, nb, seq)
    y_attn, lse = _fox_fwd(qa, ka, vm, nb, seq, ta)
    y_pool = _pool_fwd(rest, p["wbd"], p["scale"], seq)
    y_conv = _conv_fwd(rest, p["cw"], seq)
    y = jnp.concatenate([y_attn, y_pool, y_conv], axis=1)
    x_new = _mm_nn(y, w["wo"], out_dtype=F32, res=x, alpha=1.0, name="mix_out")
    return x_new, dict(x=x, xn=xn, qa=qa, ka=ka, vm=vm, rest=rest, fl=fl, lse=lse, y=y)


def _mixer_backward(dxo, dxo_b, p, w, sv, nb, seq, ta):
    t = dxo.shape[0]
    dy = _mm_nn(dxo_b, w["wo_t"], out_dtype=F32, name="mix_dy")
    g_out = _mm_tn(sv["y"], dxo_b, name="mix_gw_out")
    dq, dk, dv, d_rows, d_cols = _fox_bwd(sv["qa"], sv["ka"], sv["vm"], sv["y"], dy, sv["lse"], nb, seq, ta)
    ddh = (d_rows.reshape(nb, N_HEADS, seq) - d_cols.reshape(nb, N_HEADS, seq)).transpose(0, 2, 1).reshape(t, N_HEADS)
    dfl, dbf = _fox_prep_bwd(jnp.pad(ddh, ((0, 0), (0, LANES - N_HEADS))), sv["fl"], p["bf"], seq)
    dpool, dwbd, dscale = _pool_bwd(sv["rest"], dy, p["wbd"], p["wbd_t"], p["scale"], seq)
    dconv, dcw = _conv_bwd(sv["rest"], dy, p["cw"], seq)
    dproj = jnp.concatenate([dq, dk.astype(BF16), dv.astype(BF16), dpool, dconv, dfl], axis=1)
    g_in_t = _mm_tn(dproj, sv["xn"], tm=D_INP // 3, name="mix_gw_in")
    dxn = _mm_nn(dproj, w["wm_t"], out_dtype=F32, name="mix_dxn")
    dx, dx_b, dg = _rmsnorm_bwd(sv["x"], p["norm"], dxn, dxo, name="mix_norm_bwd")
    return dx, dx_b, dict(norm=dg, bf=dbf, wbd=dwbd, scale=dscale, cw=dcw, mix_in_t=g_in_t, mix_out=g_out)


def _block_diag(wp):
    z = jnp.zeros((POOL_GROUP, POOL_GROUP), wp.dtype)
    return jnp.concatenate(
        [jnp.concatenate([wp[g] if g == r else z for g in range(4)], axis=1) for r in range(4)], axis=0)


def _row_pad(a, rows):
    a = a.reshape(-1, a.shape[-1])
    return jnp.pad(a, ((0, rows - a.shape[0]), (0, 0)))


def kernel(x, norm_ffn1, w_ffn1_in, w_ffn1_out, norm_mix, w_mix_in, b_forget, w_pool, pool_scale, conv_w, w_mix_out, norm_ffn2, w_ffn2_in, w_ffn2_out, norm_final, loss_target, m_norm_ffn1, m_w_ffn1_in, m_w_ffn1_out, m_norm_mix, m_w_mix_in, m_b_forget, m_w_pool, m_pool_scale, m_conv_w, m_w_mix_out, m_norm_ffn2, m_w_ffn2_in, m_w_ffn2_out, m_norm_final, v_norm_ffn1, v_w_ffn1_in, v_w_ffn1_out, v_norm_mix, v_w_mix_in, v_b_forget, v_w_pool, v_pool_scale, v_conv_w, v_w_mix_out, v_norm_ffn2, v_w_ffn2_in, v_w_ffn2_out, v_norm_final):
    nb, seq, d = x.shape
    depth = norm_ffn1.shape[0]
    t = nb * seq
    ta = _pick(seq, (ATT_TILE, 128))
    my_id = 4 * lax.axis_index("x") + 2 * lax.axis_index("y") + lax.axis_index("c")
    cshard = conv_w.shape[-1]

    wg = _all_gather(_pack_shards((w_ffn1_in, w_ffn1_out, w_mix_in, w_mix_out, w_ffn2_in, w_ffn2_out), depth, BF16),
                     name="gather_weights")
    cw_g = _all_gather(_row_pad(conv_w.reshape(depth * 3, cshard), 16).reshape(4, LANES), name="gather_conv_taps")
    cw_all = cw_g.reshape(N_DEV, 16, cshard)[:, :depth * 3].reshape(N_DEV, depth, 3, cshard)
    cw_all = cw_all.transpose(1, 2, 0, 3).reshape(depth, 3, D_CONV)

    xs = x.reshape(t, d)
    saved = []
    for l in range(depth):
        w = _layer_weights(wg, l)
        wbd = _block_diag(w_pool[l])
        p = dict(norm=norm_mix[l][None], bf=jnp.pad(b_forget[l], (0, LANES - N_HEADS))[None],
                 wbd=wbd.astype(BF16), wbd_t=wbd.T.astype(BF16), scale=pool_scale[l][None],
                 cw=_row_pad(cw_all[l], 8))
        xs, s1 = _ffn_forward(xs, norm_ffn1[l][None], w["f1"])
        xs, sm = _mixer_forward(xs, p, w["mix"], nb, seq, ta)
        xs, s2 = _ffn_forward(xs, norm_ffn2[l][None], w["f2"])
        saved.append((w, p, s1, sm, s2))

    dx, dx_b, g_norm_final, loss_part = _final_loss_bwd(xs, norm_final[None], loss_target.reshape(t, d))
    layer_rows = [None] * depth
    small = [None] * depth
    for l in reversed(range(depth)):
        w, p, s1, sm, s2 = saved[l]
        dx, dx_b, dg2, g2_in_t, g2_out = _ffn_backward(dx, dx_b, norm_ffn2[l][None], w["f2"], s2)
        dx, dx_b, gm = _mixer_backward(dx, dx_b, p, w["mix"], sm, nb, seq, ta)
        dx, dx_b, dg1, g1_in_t, g1_out = _ffn_backward(dx, dx_b, norm_ffn1[l][None], w["f1"], s1)
        layer_rows[l] = _layer_grad_rows(dict(f1_in_t=g1_in_t, f1_out=g1_out, mix_in_t=gm["mix_in_t"],
                                              mix_out=gm["mix_out"], f2_in_t=g2_in_t, f2_out=g2_out))
        small[l] = dict(n1=dg1, nm=gm["norm"], n2=dg2, bf=gm["bf"], wbd=gm["wbd"], scale=gm["scale"], cw=gm["cw"])
    grad_x = dx.reshape(nb, seq, d)

    g_rows = _reduce_scatter(jnp.concatenate(layer_rows, axis=1)).reshape(depth, LAYER_ROWS, D_MODEL)
    o = 0
    pieces = {}
    for nm, n in (("f1i", FFN_ROWS), ("f1o", OUT_ROWS), ("mi", MIX_ROWS_PAD), ("mo", MO_ROWS), ("f2i", FFN_ROWS),
                  ("f2o", OUT_ROWS)):
        pieces[nm] = g_rows[:, o:o + n]
        o += n
    g_sharded = dict(
        w_ffn1_in=pieces["f1i"].transpose(0, 2, 1), w_ffn1_out=pieces["f1o"],
        w_mix_in=pieces["mi"][:, :MIX_ROWS].transpose(0, 2, 1), w_mix_out=pieces["mo"],
        w_ffn2_in=pieces["f2i"].transpose(0, 2, 1), w_ffn2_out=pieces["f2o"])

    def tile8(a):
        return jnp.pad(a, ((0, 8 - a.shape[0]), (0, D_MODEL - a.shape[1])))

    rows = []
    for l in range(depth):
        s = small[l]
        wp_rows = jnp.stack([s["wbd"][POOL_GROUP * g:POOL_GROUP * (g + 1), POOL_GROUP * g:POOL_GROUP * (g + 1)]
                             for g in range(4)]).reshape(16, D_MODEL)
        rows += [tile8(s["n1"]), tile8(s["nm"]), tile8(s["n2"]), tile8(s["bf"]), tile8(s["scale"]), tile8(s["cw"]),
                 wp_rows]
    rows += [tile8(g_norm_final), tile8(loss_part)]
    per_layer = 6 * 8 + 16
    small_sum = _sum_slots(_all_gather(jnp.concatenate(rows, axis=0), name="gather_small_grads"),
                           name="sum_small_grads")
    lay = small_sum[:depth * per_layer].reshape(depth, per_layer, D_MODEL)
    g_small = dict(
        norm_ffn1=lay[:, 0], norm_mix=lay[:, 8], norm_ffn2=lay[:, 16], b_forget=lay[:, 24, :N_HEADS],
        pool_scale=lay[:, 32, :D_POOL],
        conv_w=lax.dynamic_slice_in_dim(lay[:, 40:43, :D_CONV], my_id * cshard, cshard, axis=2),
        w_pool=lay[:, 48:64].reshape(depth, 4, POOL_GROUP, POOL_GROUP),
        norm_final=small_sum[depth * per_layer])
    loss = small_sum[depth * per_layer + 8, 0]

    given = dict(norm_ffn1=(norm_ffn1, m_norm_ffn1, v_norm_ffn1), w_ffn1_in=(w_ffn1_in, m_w_ffn1_in, v_w_ffn1_in),
                 w_ffn1_out=(w_ffn1_out, m_w_ffn1_out, v_w_ffn1_out), norm_mix=(norm_mix, m_norm_mix, v_norm_mix),
                 w_mix_in=(w_mix_in, m_w_mix_in, v_w_mix_in), b_forget=(b_forget, m_b_forget, v_b_forget),
                 w_pool=(w_pool, m_w_pool, v_w_pool), pool_scale=(pool_scale, m_pool_scale, v_pool_scale),
                 conv_w=(conv_w, m_conv_w, v_conv_w), w_mix_out=(w_mix_out, m_w_mix_out, v_w_mix_out),
                 norm_ffn2=(norm_ffn2, m_norm_ffn2, v_norm_ffn2), w_ffn2_in=(w_ffn2_in, m_w_ffn2_in, v_w_ffn2_in),
                 w_ffn2_out=(w_ffn2_out, m_w_ffn2_out, v_w_ffn2_out), norm_final=(norm_final, m_norm_final, v_norm_final))
    names = list(given)
    grads, deltas, new_m, new_v = {}, {}, {}, {}
    for nm in names:
        wv, mv, vv = given[nm]
        gv = (g_sharded[nm] if nm in g_sharded else g_small[nm]).reshape(wv.shape)
        shape2 = (-1, wv.shape[-1]) if wv.ndim > 1 else (1, wv.shape[0])
        dl, mn, vn = _adamw(wv.reshape(shape2), gv.reshape(shape2), mv.reshape(shape2), vv.reshape(shape2),
                            name="adamw_" + nm)
        grads[nm], deltas[nm], new_m[nm], new_v[nm] = gv, dl.reshape(wv.shape), mn.reshape(wv.shape), vn.reshape(wv.shape)
    return (loss, grad_x, *[grads[n] for n in names], *[deltas[n] for n in names],
            *[new_m[n] for n in names], *[new_v[n] for n in names])
```

```python
import functools

import jax
import jax.numpy as jnp
from jax import lax
from jax.experimental import pallas as pl
from jax.experimental.pallas import tpu as pltpu

F32 = jnp.float32
BF16 = jnp.bfloat16

D_MODEL = 1024
D_FF = 2816
HEAD_DIM = 64
N_HEADS = 8
N_PAIRS = N_HEADS // 2
D_ATTN = 512
D_POOL = 256
D_CONV = 256
POOL_WINDOWS = (2, 4, 8, 16)
POOL_GROUP = 64
D_IN = 2568
RMS_EPS = 1e-6
ADAM_LR, ADAM_B1, ADAM_B2, ADAM_EPS, ADAM_WD, ADAM_STEP = 0.001, 0.9, 0.999, 1e-08, 0.01, 10

N_DEV = 8
N_CHIPS = 4
LANES = 128
VMEM_BYTES_V7X = 64 * 1024 * 1024
VMEM_LIMIT_MAX = VMEM_BYTES_V7X - 8 * 1024 * 1024

F_HALF = D_FF // 2
D_QKV = 3 * D_ATTN
D_REST = D_POOL + 3 * D_CONV
D_INP = D_QKV + D_REST + LANES
MIX_ROWS = 321
MIX_ROWS_PAD = 336
FFN_ROWS = 704
OUT_ROWS = 352
MO_ROWS = 128
LAYER_ROWS = 2 * (FFN_ROWS + OUT_ROWS) + MIX_ROWS_PAD + MO_ROWS
NEG_BIG = -1e30
ATT_SCALE = HEAD_DIM ** -0.5
ATT_K = 2 * LANES
ATT_TILE = 256
ATT_PAIRS_FWD = 4
ATT_PAIRS_BWD = 2


def _cparams(sem, vmem_bytes):
    limit = int(min(max(vmem_bytes, 16 * 1024 * 1024), VMEM_LIMIT_MAX))
    return pltpu.CompilerParams(dimension_semantics=sem, vmem_limit_bytes=limit)


def _nbytes(shape, dtype):
    n = 1
    for s in shape:
        n *= s
    return n * jnp.dtype(dtype).itemsize


def _pick(n, prefs):
    for p in prefs:
        if n % p == 0:
            return p
    return n


def _rmsnorm_fwd(x, g, name):
    t, d = x.shape
    tm = _pick(t, (512, 256, 128))

    def body(x_ref, g_ref, o_ref):
        xv = x_ref[...]
        r = lax.rsqrt(jnp.mean(xv * xv, axis=-1, keepdims=True) + RMS_EPS)
        o_ref[...] = ((xv * r) * g_ref[...]).astype(o_ref.dtype)

    return pl.pallas_call(
        body, grid=(t // tm,),
        in_specs=[pl.BlockSpec((tm, d), lambda i: (i, 0)), pl.BlockSpec((1, d), lambda i: (0, 0))],
        out_specs=pl.BlockSpec((tm, d), lambda i: (i, 0)),
        out_shape=jax.ShapeDtypeStruct((t, d), BF16), name=name,
        compiler_params=_cparams(("parallel",), 6 * tm * d * 4),
    )(x, g)


def _mm_nn(a, b, *, out_dtype, name, res=None, alpha=1.0, tn=None):
    m, k = a.shape
    n = b.shape[1]
    tn = n if tn is None else tn
    tm = _pick(m, (512, 256, 128))
    with_res = res is not None

    def body(*refs):
        if with_res:
            a_ref, b_ref, r_ref, o_ref = refs
        else:
            a_ref, b_ref, o_ref = refs
        acc = jnp.dot(a_ref[...], b_ref[...], preferred_element_type=F32)
        if with_res:
            acc = r_ref[...] + alpha * acc
        o_ref[...] = acc.astype(o_ref.dtype)

    in_specs = [pl.BlockSpec((tm, k), lambda j, i: (i, 0)), pl.BlockSpec((k, tn), lambda j, i: (0, j))]
    args = [a, b]
    if with_res:
        in_specs.append(pl.BlockSpec((tm, tn), lambda j, i: (i, j)))
        args.append(res)
    vmem = 2 * (_nbytes((tm, k), BF16) + _nbytes((k, tn), BF16) + 3 * _nbytes((tm, tn), F32))
    return pl.pallas_call(
        body, grid=(n // tn, m // tm), in_specs=in_specs,
        out_specs=pl.BlockSpec((tm, tn), lambda j, i: (i, j)),
        out_shape=jax.ShapeDtypeStruct((m, n), out_dtype), name=name,
        compiler_params=_cparams(("parallel", "parallel"), vmem),
    )(*args)


def _mm_tn(a, b, *, name, alpha=1.0, tm=None):
    t, m = a.shape
    n = b.shape[1]
    tm = m if tm is None else tm
    tk = _pick(t, (1024, 512, 256, 128))
    nk = t // tk

    def body(a_ref, b_ref, o_ref):
        kk = pl.program_id(1)
        p = lax.dot_general(a_ref[...], b_ref[...], (((0,), (0,)), ((), ())), preferred_element_type=F32)
        if alpha != 1.0:
            p = alpha * p

        @pl.when(kk == 0)
        def _():
            o_ref[...] = p

        @pl.when(kk > 0)
        def _():
            o_ref[...] += p

    vmem = 2 * (_nbytes((tk, tm), BF16) + _nbytes((tk, n), BF16) + 2 * _nbytes((tm, n), F32))
    return pl.pallas_call(
        body, grid=(m // tm, nk),
        in_specs=[pl.BlockSpec((tk, tm), lambda i, kk: (kk, i)), pl.BlockSpec((tk, n), lambda i, kk: (kk, 0))],
        out_specs=pl.BlockSpec((tm, n), lambda i, kk: (i, 0)),
        out_shape=jax.ShapeDtypeStruct((m, n), F32), name=name,
        compiler_params=_cparams(("parallel", "arbitrary"), vmem),
    )(a, b)


def _sigmoid(v):
    return 1.0 / (1.0 + jnp.exp(-v))


def _ffn_in(xn, w, name):
    t, d = xn.shape
    tm = _pick(t, (512, 256, 128))

    def body(x_ref, wg_ref, wu_ref, h_ref, zg_ref, zu_ref):
        xv = x_ref[...]
        g = jnp.dot(xv, wg_ref[...], preferred_element_type=F32)
        u = jnp.dot(xv, wu_ref[...], preferred_element_type=F32)
        h_ref[...] = ((g * _sigmoid(g)) * u).astype(h_ref.dtype)
        zg_ref[...] = g.astype(zg_ref.dtype)
        zu_ref[...] = u.astype(zu_ref.dtype)

    vmem = 2 * (_nbytes((tm, d), BF16) + 2 * _nbytes((d, F_HALF), BF16) + 4 * _nbytes((tm, D_FF), F32))
    out_blk = pl.BlockSpec((tm, F_HALF), lambda j, i: (i, j))
    sds = jax.ShapeDtypeStruct((t, D_FF), BF16)
    return pl.pallas_call(
        body, grid=(2, t // tm),
        in_specs=[pl.BlockSpec((tm, d), lambda j, i: (i, 0)), pl.BlockSpec((d, F_HALF), lambda j, i: (0, j)),
                  pl.BlockSpec((d, F_HALF), lambda j, i: (0, 2 + j))],
        out_specs=[out_blk, out_blk, out_blk], out_shape=[sds, sds, sds], name=name,
        compiler_params=_cparams(("parallel", "parallel"), vmem),
    )(xn, w, w)


def _ffn_bwd_mid(dxo, w_out_t, zg, zu, name):
    t, d = dxo.shape
    tm = _pick(t, (512, 256, 128))

    def body(d_ref, w_ref, zg_ref, zu_ref, dg_ref, du_ref):
        dh = 0.5 * jnp.dot(d_ref[...], w_ref[...], preferred_element_type=F32)
        g = zg_ref[...].astype(F32)
        u = zu_ref[...].astype(F32)
        s = _sigmoid(g)
        dg_ref[...] = (dh * u * (s * (1.0 + g * (1.0 - s)))).astype(dg_ref.dtype)
        du_ref[...] = (dh * (g * s)).astype(du_ref.dtype)

    vmem = 2 * (_nbytes((tm, d), BF16) + _nbytes((d, F_HALF), BF16) + 5 * _nbytes((tm, D_FF), F32))
    blk = pl.BlockSpec((tm, F_HALF), lambda j, i: (i, j))
    sds = jax.ShapeDtypeStruct((t, D_FF), BF16)
    return pl.pallas_call(
        body, grid=(2, t // tm),
        in_specs=[pl.BlockSpec((tm, d), lambda j, i: (i, 0)), pl.BlockSpec((d, F_HALF), lambda j, i: (0, j)), blk, blk],
        out_specs=[blk, blk], out_shape=[sds, sds], name=name,
        compiler_params=_cparams(("parallel", "parallel"), vmem),
    )(dxo, w_out_t, zg, zu)


def _mm_nn2(a1, a2, b, name):
    m, k = a1.shape
    n = b.shape[1]
    tm = _pick(m, (512, 256, 128))

    def body(a1_ref, a2_ref, b1_ref, b2_ref, o_ref):
        o_ref[...] = (jnp.dot(a1_ref[...], b1_ref[...], preferred_element_type=F32)
                      + jnp.dot(a2_ref[...], b2_ref[...], preferred_element_type=F32))

    a_blk = pl.BlockSpec((tm, k), lambda i: (i, 0))
    vmem = 2 * (2 * _nbytes((tm, k), BF16) + 2 * _nbytes((k, n), BF16) + 3 * _nbytes((tm, n), F32))
    return pl.pallas_call(
        body, grid=(m // tm,),
        in_specs=[a_blk, a_blk, pl.BlockSpec((k, n), lambda i: (0, 0)), pl.BlockSpec((k, n), lambda i: (1, 0))],
        out_specs=pl.BlockSpec((tm, n), lambda i: (i, 0)), out_shape=jax.ShapeDtypeStruct((m, n), F32), name=name,
        compiler_params=_cparams(("parallel",), vmem),
    )(a1, a2, b, b)


def _rmsnorm_bwd(x, g, dxn, dxo, name):
    t, d = x.shape
    tm = _pick(t, (512, 256, 128))

    def body(x_ref, g_ref, dn_ref, do_ref, dx_ref, dxb_ref, dg_ref):
        i = pl.program_id(0)
        xv = x_ref[...]
        r = lax.rsqrt(jnp.mean(xv * xv, axis=-1, keepdims=True) + RMS_EPS)
        xh = xv * r
        dn = dn_ref[...]
        dgp = jnp.sum(dn * xh, axis=0, keepdims=True)
        dh = dn * g_ref[...]
        dx = do_ref[...] + r * (dh - xh * jnp.mean(dh * xh, axis=-1, keepdims=True))
        dx_ref[...] = dx
        dxb_ref[...] = dx.astype(dxb_ref.dtype)

        @pl.when(i == 0)
        def _():
            dg_ref[...] = dgp

        @pl.when(i > 0)
        def _():
            dg_ref[...] += dgp

    blk = pl.BlockSpec((tm, d), lambda i: (i, 0))
    row = pl.BlockSpec((1, d), lambda i: (0, 0))
    return pl.pallas_call(
        body, grid=(t // tm,), in_specs=[blk, row, blk, blk], out_specs=[blk, blk, row],
        out_shape=[jax.ShapeDtypeStruct((t, d), F32), jax.ShapeDtypeStruct((t, d), BF16),
                   jax.ShapeDtypeStruct((1, d), F32)], name=name,
        compiler_params=_cparams(("arbitrary",), 16 * tm * d * 4),
    )(x, g, dxn, dxo)


def _final_loss_bwd(x, g, tgt):
    t, d = x.shape
    tm = _pick(t, (512, 256, 128))

    def body(x_ref, g_ref, t_ref, dx_ref, dxb_ref, dg_ref, loss_ref):
        i = pl.program_id(0)
        xv = x_ref[...]
        r = lax.rsqrt(jnp.mean(xv * xv, axis=-1, keepdims=True) + RMS_EPS)
        xh = xv * r
        gv = g_ref[...]
        err = xh * gv - t_ref[...]
        lp = 0.5 * jnp.sum(jnp.mean(err * err, axis=-1, keepdims=True), axis=0, keepdims=True)
        dy = err * (1.0 / d)
        dgp = jnp.sum(dy * xh, axis=0, keepdims=True)
        dh = dy * gv
        dx = r * (dh - xh * jnp.mean(dh * xh, axis=-1, keepdims=True))
        dx_ref[...] = dx
        dxb_ref[...] = dx.astype(dxb_ref.dtype)
        lpb = jnp.broadcast_to(lp, (1, LANES))

        @pl.when(i == 0)
        def _():
            dg_ref[...] = dgp
            loss_ref[...] = lpb

        @pl.when(i > 0)
        def _():
            dg_ref[...] += dgp
            loss_ref[...] += lpb

    blk = pl.BlockSpec((tm, d), lambda i: (i, 0))
    row = pl.BlockSpec((1, d), lambda i: (0, 0))
    return pl.pallas_call(
        body, grid=(t // tm,), in_specs=[blk, row, blk],
        out_specs=[blk, blk, row, pl.BlockSpec((1, LANES), lambda i: (0, 0))],
        out_shape=[jax.ShapeDtypeStruct((t, d), F32), jax.ShapeDtypeStruct((t, d), BF16),
                   jax.ShapeDtypeStruct((1, d), F32), jax.ShapeDtypeStruct((1, LANES), F32)], name="final_loss_bwd",
        compiler_params=_cparams(("arbitrary",), 16 * tm * d * 4),
    )(x, g, tgt)


def _seq_scan(v, seq, reverse):
    row = lax.broadcasted_iota(jnp.int32, v.shape, 0)
    k = 1
    while k < seq:
        if reverse:
            v = v + jnp.where(row < seq - k, pltpu.roll(v, seq - k, 0), 0.0)
        else:
            v = v + jnp.where(row >= k, pltpu.roll(v, k, 0), 0.0)
        k *= 2
    return v


def _log_sigmoid(v):
    return jnp.minimum(v, 0.0) - jnp.log(1.0 + jnp.exp(-jnp.abs(v)))


def _fox_prep(fl, bf, qkv, nb, seq):
    def body(f_ref, b_ref, q_ref, k_ref, v_ref, qa_ref, ka_ref, vm_ref):
        dsum = _seq_scan(_log_sigmoid(f_ref[...] + b_ref[...]), seq, False)
        d1 = dsum.astype(BF16).astype(F32)
        r1 = dsum - d1
        d2 = r1.astype(BF16).astype(F32)
        d3 = (r1 - d2).astype(BF16).astype(F32)
        lane = lax.broadcasted_iota(jnp.int32, (seq, LANES), 1)
        first = lane < HEAD_DIM
        l64 = jnp.where(first, lane, lane - HEAD_DIM)
        for p in range(N_PAIRS):
            def head_cols(a, p=p):
                return jnp.where(first, a[:, 2 * p:2 * p + 1], a[:, 2 * p + 1:2 * p + 2])

            e1, e2, e3 = head_cols(d1), head_cols(d2), head_cols(d3)
            aux_q = jnp.where(l64 == 0, e1, jnp.where(l64 == 1, e2, jnp.where(l64 == 2, e3,
                              jnp.where(l64 < 6, 1.0, 0.0)))).astype(BF16)
            aux_k = jnp.where(l64 < 3, 1.0, jnp.where(l64 == 3, -e1, jnp.where(l64 == 4, -e2,
                              jnp.where(l64 == 5, -e3, 0.0)))).astype(BF16)
            cols = slice(LANES * p, LANES * (p + 1))
            qs = q_ref[:, cols] * ATT_SCALE
            vp = v_ref[:, cols]
            zero = jnp.zeros_like(qs)
            qa_ref[0, p, 0, :, :LANES] = jnp.where(first, qs, zero)
            qa_ref[0, p, 0, :, LANES:] = jnp.where(first, aux_q, zero)
            qa_ref[0, p, 1, :, :LANES] = jnp.where(first, zero, qs)
            qa_ref[0, p, 1, :, LANES:] = jnp.where(first, zero, aux_q)
            ka_ref[0, p, :, :LANES] = k_ref[:, cols]
            ka_ref[0, p, :, LANES:] = aux_k
            vm_ref[0, p, 0] = jnp.where(first, vp, zero)
            vm_ref[0, p, 1] = jnp.where(first, zero, vp)

    def part(c):
        return pl.BlockSpec((seq, D_ATTN), lambda b, c=c: (b, c))

    return pl.pallas_call(
        body, grid=(nb,),
        in_specs=[pl.BlockSpec((seq, LANES), lambda b: (b, 0)), pl.BlockSpec((1, LANES), lambda b: (0, 0)),
                  part(0), part(1), part(2)],
        out_specs=[pl.BlockSpec((1, N_PAIRS, 2, seq, ATT_K), lambda b: (b, 0, 0, 0, 0)),
                   pl.BlockSpec((1, N_PAIRS, seq, ATT_K), lambda b: (b, 0, 0, 0)),
                   pl.BlockSpec((1, N_PAIRS, 2, seq, LANES), lambda b: (b, 0, 0, 0, 0))],
        out_shape=[jax.ShapeDtypeStruct((nb, N_PAIRS, 2, seq, ATT_K), BF16),
                   jax.ShapeDtypeStruct((nb, N_PAIRS, seq, ATT_K), BF16),
                   jax.ShapeDtypeStruct((nb, N_PAIRS, 2, seq, LANES), BF16)],
        name="fox_prep", compiler_params=_cparams(("parallel",), 48 * 1024 * 1024),
    )(fl, bf, qkv, qkv, qkv)


def _fox_prep_bwd(dd, fl, bf, seq):
    t = fl.shape[0]

    def body(d_ref, f_ref, b_ref, o_ref, db_ref):
        i = pl.program_id(0)
        dlog = _seq_scan(d_ref[...], seq, True)
        dfl = dlog * _sigmoid(-(f_ref[...] + b_ref[...]))
        o_ref[...] = dfl.astype(o_ref.dtype)
        dbp = jnp.sum(dfl, axis=0, keepdims=True)

        @pl.when(i == 0)
        def _():
            db_ref[...] = dbp

        @pl.when(i > 0)
        def _():
            db_ref[...] += dbp

    blk = pl.BlockSpec((seq, LANES), lambda b: (b, 0))
    row = pl.BlockSpec((1, LANES), lambda b: (0, 0))
    return pl.pallas_call(
        body, grid=(t // seq,), in_specs=[blk, blk, row], out_specs=[blk, row],
        out_shape=[jax.ShapeDtypeStruct((t, LANES), BF16), jax.ShapeDtypeStruct((1, LANES), F32)], name="fox_prep_bwd",
        compiler_params=_cparams(("arbitrary",), 24 * seq * LANES * 4),
    )(dd, fl, bf)


def _pair_rows(a, ta):
    lane = lax.broadcasted_iota(jnp.int32, (ta, LANES), 1)
    return jnp.where(lane < HEAD_DIM, a[:ta], a[ta:])


def _diag_mask(ta):
    r = lax.broadcasted_iota(jnp.int32, (2 * ta, ta), 0)
    c = lax.broadcasted_iota(jnp.int32, (2 * ta, ta), 1)
    return c <= jnp.where(r >= ta, r - ta, r)


def _nt(a, b):
    return lax.dot_general(a, b, (((1,), (1,)), ((), ())), preferred_element_type=F32)


def _tn(a, b):
    return lax.dot_general(a, b, (((0,), (0,)), ((), ())), preferred_element_type=F32)


def _grid_ends(ids, sizes):
    first = functools.reduce(jnp.logical_and, [i == 0 for i in ids])
    last = functools.reduce(jnp.logical_and, [i == n - 1 for i, n in zip(ids, sizes)])
    return first, last


def _fox_fwd(qa, ka, vm, nb, seq, ta, carry_gather=None):
    nq = seq // ta
    npp = ATT_PAIRS_FWD
    grid = (nb, N_PAIRS // npp, nq)

    def body(q_ref, k_ref, v_ref, *rest):
        if carry_gather is None:
            o_ref, lse_ref = rest
        else:
            x_ref, o_ref, lse_ref, gathered_ref, send_sems, recv_sems = rest
            first_step, last_step = _grid_ends([pl.program_id(a) for a in range(3)], grid)

            @pl.when(first_step)
            def _():
                _gather_start(x_ref, gathered_ref, send_sems, recv_sems)

        i = pl.program_id(2)
        q2s = [q_ref[0, pp].reshape(2 * ta, ATT_K) for pp in range(npp)]

        def step(j, carry, masked):
            rows = pl.ds(pl.multiple_of(j * ta, ta), ta)
            out = []
            for pp in range(npp):
                m, l, acc = carry[pp]
                s = _nt(q2s[pp], k_ref[0, pp, rows, :])
                if masked:
                    s = jnp.where(_diag_mask(ta), s, NEG_BIG)
                m_new = jnp.maximum(m, jnp.max(s, axis=-1, keepdims=True))
                p = jnp.exp(s - m_new)
                corr = jnp.exp(m - m_new)
                l = corr * l + jnp.sum(p, axis=-1, keepdims=True)
                pb = p.astype(BF16)
                pv = (jnp.dot(pb[:ta], v_ref[0, pp, 0, rows, :], preferred_element_type=F32)
                      + jnp.dot(pb[ta:], v_ref[0, pp, 1, rows, :], preferred_element_type=F32))
                out.append((m_new, l, _pair_rows(corr, ta) * acc + pv))
            return tuple(out)

        init = tuple((jnp.full((2 * ta, 1), NEG_BIG, F32), jnp.zeros((2 * ta, 1), F32),
                      jnp.zeros((ta, LANES), F32)) for _ in range(npp))
        carry = lax.fori_loop(0, i, functools.partial(step, masked=False), init)
        for pp, (m, l, acc) in enumerate(step(i, carry, True)):
            o_ref[:, LANES * pp:LANES * (pp + 1)] = (acc * _pair_rows(1.0 / l, ta)).astype(o_ref.dtype)
            lse = m + jnp.log(l)
            lse_ref[0, pp, 0] = lse[:ta]
            lse_ref[0, pp, 1] = lse[ta:]

        if carry_gather is not None:
            @pl.when(last_step)
            def _():
                _gather_finish(x_ref, gathered_ref, send_sems, recv_sems)

    vmem = (2 * npp * (_nbytes((seq, ATT_K), BF16) + 2 * _nbytes((seq, LANES), BF16)) + 24 * npp * ta * ta * 4
            + 8 * 1024 * 1024)
    in_specs = [pl.BlockSpec((1, npp, 2, ta, ATT_K), lambda b, g, i: (b, g, 0, i, 0)),
                pl.BlockSpec((1, npp, seq, ATT_K), lambda b, g, i: (b, g, 0, 0)),
                pl.BlockSpec((1, npp, 2, seq, LANES), lambda b, g, i: (b, g, 0, 0, 0))]
    out_specs = [pl.BlockSpec((ta, LANES * npp), lambda b, g, i: (b * nq + i, g)),
                 pl.BlockSpec((1, npp, 2, ta, 1), lambda b, g, i: (b, g, 0, i, 0))]
    out_shape = [jax.ShapeDtypeStruct((nb * seq, D_ATTN), BF16), jax.ShapeDtypeStruct((nb, N_PAIRS, 2, seq, 1), F32)]
    if carry_gather is None:
        return pl.pallas_call(
            body, grid=grid, in_specs=in_specs, out_specs=out_specs, out_shape=out_shape, name="fox_fwd",
            compiler_params=_cparams(("parallel", "parallel", "parallel"), vmem),
        )(qa, ka, vm)
    any_spec = pl.BlockSpec(memory_space=pl.ANY)
    return pl.pallas_call(
        body, grid=grid, in_specs=in_specs + [any_spec], out_specs=out_specs + [any_spec],
        out_shape=out_shape + [jax.ShapeDtypeStruct((N_DEV,) + carry_gather.shape, carry_gather.dtype)],
        scratch_shapes=list(GATHER_SEMS), name="fox_fwd_gather",
        compiler_params=_cparams(("arbitrary", "arbitrary", "arbitrary"), vmem),
    )(qa, ka, vm, carry_gather)


def _fox_bwd(qa, ka, vm, y, dy, lse, nb, seq, ta, carry_exchange=None):
    nq = seq // ta
    npp = ATT_PAIRS_BWD
    grid = (nb, N_PAIRS // npp, nq)

    def body(q_ref, k_ref, v_ref, o_ref, do_ref, lse_ref, *rest):
        if carry_exchange is None:
            dq_ref, dk_ref, dv_ref, rs_ref, cs_ref = rest
        else:
            t_ref, dq_ref, dk_ref, dv_ref, rs_ref, cs_ref, landed_ref, send_sems, recv_sems = rest
            first_step, last_step = _grid_ends([pl.program_id(a) for a in range(3)], grid)

            @pl.when(first_step)
            def _():
                _chips_start(t_ref, landed_ref, send_sems, recv_sems)

        i = pl.program_id(2)

        @pl.when(i == 0)
        def _():
            dk_ref[...] = jnp.zeros_like(dk_ref)
            dv_ref[...] = jnp.zeros_like(dv_ref)
            cs_ref[...] = jnp.zeros_like(cs_ref)

        first = lax.broadcasted_iota(jnp.int32, (ta, LANES), 1) < HEAD_DIM
        q2s, do2s, deltas, lses = [], [], [], []
        for pp in range(npp):
            cols = slice(LANES * pp, LANES * (pp + 1))
            q2s.append(q_ref[0, pp].reshape(2 * ta, ATT_K))
            do = do_ref[:, cols]
            doo = do * o_ref[:, cols].astype(F32)
            do2s.append(jnp.concatenate([jnp.where(first, do, 0.0), jnp.where(first, 0.0, do)], axis=0).astype(BF16))
            deltas.append(jnp.concatenate([jnp.sum(jnp.where(first, doo, 0.0), axis=-1, keepdims=True),
                                           jnp.sum(jnp.where(first, 0.0, doo), axis=-1, keepdims=True)], axis=0))
            lses.append(jnp.concatenate([lse_ref[0, pp, 0], lse_ref[0, pp, 1]], axis=0))

        def step(j, carry, masked):
            rows = pl.ds(pl.multiple_of(j * ta, ta), ta)
            out = []
            for pp in range(npp):
                dq_acc, rs_acc = carry[pp]
                cols = slice(LANES * pp, LANES * (pp + 1))
                ks = k_ref[0, pp, rows, :]
                s = _nt(q2s[pp], ks)
                if masked:
                    s = jnp.where(_diag_mask(ta), s, NEG_BIG)
                p = jnp.exp(s - lses[pp])
                dp = _nt(do2s[pp], v_ref[0, pp, 0, rows, :] + v_ref[0, pp, 1, rows, :])
                ds32 = p * (dp - deltas[pp])
                ds = ds32.astype(BF16)
                dk_ref[rows, cols] += _tn(ds, q2s[pp][:, :LANES])
                dv_ref[rows, cols] += _tn(p.astype(BF16), do2s[pp])
                cs_ref[0, pp, 0, j] += jnp.sum(ds32[:ta], axis=0, keepdims=True)
                cs_ref[0, pp, 1, j] += jnp.sum(ds32[ta:], axis=0, keepdims=True)
                out.append((dq_acc + jnp.dot(ds, ks[:, :LANES], preferred_element_type=F32),
                            rs_acc + jnp.sum(ds32, axis=-1, keepdims=True)))
            return tuple(out)

        init = tuple((jnp.zeros((2 * ta, LANES), F32), jnp.zeros((2 * ta, 1), F32)) for _ in range(npp))
        carry = lax.fori_loop(0, i, functools.partial(step, masked=False), init)
        for pp, (dq_acc, rs_acc) in enumerate(step(i, carry, True)):
            dq = jnp.where(first, dq_acc[:ta], dq_acc[ta:]) * ATT_SCALE
            dq_ref[:, LANES * pp:LANES * (pp + 1)] = dq.astype(dq_ref.dtype)
            rs_ref[0, pp, 0] = jnp.broadcast_to(rs_acc[:ta], (ta, LANES))
            rs_ref[0, pp, 1] = jnp.broadcast_to(rs_acc[ta:], (ta, LANES))

        if carry_exchange is not None:
            @pl.when(last_step)
            def _():
                _chips_finish(t_ref, landed_ref, send_sems, recv_sems)

    vmem = (2 * npp * (_nbytes((seq, ATT_K), BF16) + 2 * _nbytes((seq, LANES), BF16) + 2 * _nbytes((seq, LANES), F32))
            + 32 * npp * ta * ta * 4 + 8 * 1024 * 1024)
    qblk = lambda b, g, i: (b * nq + i, g)
    acc_blk = pl.BlockSpec((seq, LANES * npp), lambda b, g, i: (b, g))
    in_specs = [pl.BlockSpec((1, npp, 2, ta, ATT_K), lambda b, g, i: (b, g, 0, i, 0)),
                pl.BlockSpec((1, npp, seq, ATT_K), lambda b, g, i: (b, g, 0, 0)),
                pl.BlockSpec((1, npp, 2, seq, LANES), lambda b, g, i: (b, g, 0, 0, 0)),
                pl.BlockSpec((ta, LANES * npp), qblk), pl.BlockSpec((ta, LANES * npp), qblk),
                pl.BlockSpec((1, npp, 2, ta, 1), lambda b, g, i: (b, g, 0, i, 0))]
    out_specs = [pl.BlockSpec((ta, LANES * npp), qblk), acc_blk, acc_blk,
                 pl.BlockSpec((1, npp, 2, ta, LANES), lambda b, g, i: (b, g, 0, i, 0)),
                 pl.BlockSpec((1, npp, 2, nq, 1, ta), lambda b, g, i: (b, g, 0, 0, 0, 0))]
    out_shape = [jax.ShapeDtypeStruct((nb * seq, D_ATTN), BF16), jax.ShapeDtypeStruct((nb * seq, D_ATTN), F32),
                 jax.ShapeDtypeStruct((nb * seq, D_ATTN), F32), jax.ShapeDtypeStruct((nb, N_PAIRS, 2, seq, LANES), F32),
                 jax.ShapeDtypeStruct((nb, N_PAIRS, 2, nq, 1, ta), F32)]
    if carry_exchange is None:
        return pl.pallas_call(
            body, grid=grid, in_specs=in_specs, out_specs=out_specs, out_shape=out_shape, name="fox_bwd",
            compiler_params=_cparams(("parallel", "parallel", "arbitrary"), vmem),
        )(qa, ka, vm, y, dy, lse)
    any_spec = pl.BlockSpec(memory_space=pl.ANY)
    return pl.pallas_call(
        body, grid=grid, in_specs=in_specs + [any_spec], out_specs=out_specs + [any_spec],
        out_shape=out_shape + [jax.ShapeDtypeStruct(carry_exchange.shape, carry_exchange.dtype)],
        scratch_shapes=list(CHIPS_SEMS), name="fox_bwd_exchange",
        compiler_params=_cparams(("arbitrary", "arbitrary", "arbitrary"), vmem),
    )(qa, ka, vm, y, dy, lse, carry_exchange)


def _shift_down(a, k):
    row = lax.broadcasted_iota(jnp.int32, a.shape, 0)
    return jnp.where(row >= k, pltpu.roll(a, k, 0), 0.0)


def _shift_up(a, k):
    n = a.shape[0]
    row = lax.broadcasted_iota(jnp.int32, a.shape, 0)
    return jnp.where(row < n - k, pltpu.roll(a, n - k, 0), 0.0)


def _by_group(vals, shape):
    lane = lax.broadcasted_iota(jnp.int32, shape, 1)
    out = vals[-1]
    for gi in range(len(vals) - 2, -1, -1):
        out = jnp.where(lane < POOL_GROUP * (gi + 1), vals[gi], out)
    return out


def _pooled(u):
    s2 = u + _shift_down(u, 1)
    s4 = s2 + _shift_down(s2, 2)
    s8 = s4 + _shift_down(s4, 4)
    s16 = s8 + _shift_down(s8, 8)
    win = _by_group([s2, s4, s8, s16], u.shape)
    row = lax.broadcasted_iota(jnp.int32, u.shape, 0)
    wsize = _by_group([jnp.full(u.shape, w, jnp.int32) for w in POOL_WINDOWS], u.shape)
    inv = 1.0 / jnp.minimum(row + 1, wsize).astype(F32)
    return win * inv - u, inv


def _pool_fwd(rest, wbd, scale, seq):
    t = rest.shape[0]

    def body(u_ref, w_ref, s_ref, o_ref):
        pooled, _ = _pooled(u_ref[...])
        pw = jnp.dot(pooled.astype(BF16), w_ref[...], preferred_element_type=F32)
        o_ref[...] = (pw * s_ref[...]).astype(o_ref.dtype)

    blk = pl.BlockSpec((seq, D_POOL), lambda b: (b, 0))
    return pl.pallas_call(
        body, grid=(t // seq,),
        in_specs=[blk, pl.BlockSpec((D_POOL, D_POOL), lambda b: (0, 0)), pl.BlockSpec((1, D_POOL), lambda b: (0, 0))],
        out_specs=blk, out_shape=jax.ShapeDtypeStruct((t, D_POOL), BF16), name="pool_fwd",
        compiler_params=_cparams(("parallel",), 24 * seq * D_POOL * 4),
    )(rest, wbd, scale)


def _pool_bwd(rest, dy, wbd, wbd_t, scale, seq):
    t = rest.shape[0]

    def body(u_ref, dy_ref, w_ref, wt_ref, s_ref, du_ref, dw_ref, dsc_ref):
        i = pl.program_id(0)
        pooled, inv = _pooled(u_ref[...])
        pb = pooled.astype(BF16)
        pw = jnp.dot(pb, w_ref[...], preferred_element_type=F32)
        dyp = dy_ref[...]
        dsp = jnp.sum(dyp * pw, axis=0, keepdims=True)
        dpw = (dyp * s_ref[...]).astype(BF16)
        dwp = _tn(pb, dpw)
        dpooled = jnp.dot(dpw, wt_ref[...], preferred_element_type=F32)
        dwin = dpooled * inv
        t2 = dwin + _shift_up(dwin, 1)
        t4 = t2 + _shift_up(t2, 2)
        t8 = t4 + _shift_up(t4, 4)
        t16 = t8 + _shift_up(t8, 8)
        du_ref[...] = (_by_group([t2, t4, t8, t16], dwin.shape) - dpooled).astype(du_ref.dtype)

        @pl.when(i == 0)
        def _():
            dw_ref[...] = dwp
            dsc_ref[...] = dsp

        @pl.when(i > 0)
        def _():
            dw_ref[...] += dwp
            dsc_ref[...] += dsp

    blk = pl.BlockSpec((seq, D_POOL), lambda b: (b, 0))
    sq = pl.BlockSpec((D_POOL, D_POOL), lambda b: (0, 0))
    row = pl.BlockSpec((1, D_POOL), lambda b: (0, 0))
    return pl.pallas_call(
        body, grid=(t // seq,),
        in_specs=[blk, pl.BlockSpec((seq, D_POOL), lambda b: (b, 2)), sq, sq, row],
        out_specs=[blk, sq, row],
        out_shape=[jax.ShapeDtypeStruct((t, D_POOL), BF16), jax.ShapeDtypeStruct((D_POOL, D_POOL), F32),
                   jax.ShapeDtypeStruct((1, D_POOL), F32)], name="pool_bwd",
        compiler_params=_cparams(("arbitrary",), 40 * seq * D_POOL * 4),
    )(rest, dy, wbd, wbd_t, scale)


def _conv_fwd(rest, cw, seq):
    t = rest.shape[0]

    def body(cb_ref, cc_ref, ch_ref, w_ref, o_ref):
        u = cc_ref[...] * ch_ref[...]
        y = w_ref[0:1, :] * _shift_down(u, 2) + w_ref[1:2, :] * _shift_down(u, 1) + w_ref[2:3, :] * u
        o_ref[...] = (cb_ref[...] * y).astype(o_ref.dtype)

    def col(c):
        return pl.BlockSpec((seq, D_CONV), lambda b, c=c: (b, c))

    return pl.pallas_call(
        body, grid=(t // seq,), in_specs=[col(1), col(2), col(3), pl.BlockSpec((8, D_CONV), lambda b: (0, 0))],
        out_specs=pl.BlockSpec((seq, D_CONV), lambda b: (b, 0)),
        out_shape=jax.ShapeDtypeStruct((t, D_CONV), BF16), name="conv_fwd",
        compiler_params=_cparams(("parallel",), 24 * seq * D_CONV * 4),
    )(rest, rest, rest, cw)


def _conv_bwd(rest, dy, cw, seq):
    t = rest.shape[0]

    def body(cb_ref, cc_ref, ch_ref, dy_ref, w_ref, o_ref, dw_ref):
        i = pl.program_id(0)
        cc = cc_ref[...]
        ch = ch_ref[...]
        u = cc * ch
        u1 = _shift_down(u, 1)
        u2 = _shift_down(u, 2)
        y = w_ref[0:1, :] * u2 + w_ref[1:2, :] * u1 + w_ref[2:3, :] * u
        dyc = dy_ref[...]
        d2 = dyc * cb_ref[...]
        du = w_ref[0:1, :] * _shift_up(d2, 2) + w_ref[1:2, :] * _shift_up(d2, 1) + w_ref[2:3, :] * d2
        o_ref[:, 0:D_CONV] = (dyc * y).astype(o_ref.dtype)
        o_ref[:, D_CONV:2 * D_CONV] = (du * ch).astype(o_ref.dtype)
        o_ref[:, 2 * D_CONV:3 * D_CONV] = (du * cc).astype(o_ref.dtype)
        tap = lax.broadcasted_iota(jnp.int32, (8, D_CONV), 0)
        dwp = jnp.where(tap == 0, jnp.sum(d2 * u2, axis=0, keepdims=True),
                        jnp.where(tap == 1, jnp.sum(d2 * u1, axis=0, keepdims=True),
                                  jnp.where(tap == 2, jnp.sum(d2 * u, axis=0, keepdims=True), 0.0)))

        @pl.when(i == 0)
        def _():
            dw_ref[...] = dwp

        @pl.when(i > 0)
        def _():
            dw_ref[...] += dwp

    def col(c):
        return pl.BlockSpec((seq, D_CONV), lambda b, c=c: (b, c))

    taps = pl.BlockSpec((8, D_CONV), lambda b: (0, 0))
    return pl.pallas_call(
        body, grid=(t // seq,), in_specs=[col(1), col(2), col(3), col(3), taps],
        out_specs=[pl.BlockSpec((seq, 3 * D_CONV), lambda b: (b, 0)), taps],
        out_shape=[jax.ShapeDtypeStruct((t, 3 * D_CONV), BF16), jax.ShapeDtypeStruct((8, D_CONV), F32)],
        name="conv_bwd", compiler_params=_cparams(("arbitrary",), 48 * seq * D_CONV * 4),
    )(rest, rest, rest, dy, cw)


def _adamw(w, g, m, v, name):
    r, c = w.shape
    tr = _pick(r, (512, 352, 256, 128)) if r > 512 else r

    def body(w_ref, g_ref, m_ref, v_ref, d_ref, mo_ref, vo_ref):
        gv = g_ref[...]
        mn = ADAM_B1 * m_ref[...] + (1.0 - ADAM_B1) * gv
        vn = ADAM_B2 * v_ref[...] + (1.0 - ADAM_B2) * (gv * gv)
        m_hat = mn / (1.0 - ADAM_B1 ** ADAM_STEP)
        v_hat = vn / (1.0 - ADAM_B2 ** ADAM_STEP)
        d_ref[...] = -ADAM_LR * (m_hat / (jnp.sqrt(v_hat) + ADAM_EPS) + ADAM_WD * w_ref[...])
        mo_ref[...] = mn
        vo_ref[...] = vn

    blk = pl.BlockSpec((tr, c), lambda i: (i, 0))
    sds = jax.ShapeDtypeStruct((r, c), F32)
    return pl.pallas_call(
        body, grid=(r // tr,), in_specs=[blk] * 4, out_specs=[blk] * 3, out_shape=[sds] * 3, name=name,
        compiler_params=_cparams(("parallel",), 20 * tr * max(c, LANES) * 4),
    )(w, g, m, v)


def _sum_slots(a, name):
    ns, r, c = a.shape
    tr = _pick(r, (368, 256, 184, 136, 128, 88, 8))

    def body(a_ref, o_ref):
        acc = a_ref[0].astype(F32)
        for s in range(1, ns):
            acc = acc + a_ref[s].astype(F32)
        o_ref[...] = acc

    return pl.pallas_call(
        body, grid=(r // tr,), in_specs=[pl.BlockSpec((ns, tr, c), lambda i: (0, i, 0))],
        out_specs=pl.BlockSpec((tr, c), lambda i: (i, 0)), out_shape=jax.ShapeDtypeStruct((r, c), F32), name=name,
        compiler_params=_cparams(("parallel",), 4 * (ns + 2) * tr * c * 4),
    )(a)


def _add_core_half(core, g4, theirs, out_dtype, name):
    ns, _, r, c = g4.shape
    tr = _pick(r, (368, 256, 184, 136, 128, 88, 8))

    def body(core_ref, a_ref, b_ref, o_ref):
        o_ref[...] = (a_ref[0] + b_ref[...]).astype(o_ref.dtype)

    blk = pl.BlockSpec((1, tr, c), lambda s, i, core_ref: (s, i, 0))
    return pl.pallas_call(
        body,
        grid_spec=pltpu.PrefetchScalarGridSpec(
            num_scalar_prefetch=1, grid=(ns, r // tr),
            in_specs=[pl.BlockSpec((1, 1, tr, c), lambda s, i, core_ref: (s, core_ref[0], i, 0)), blk],
            out_specs=blk),
        out_shape=jax.ShapeDtypeStruct((ns, r, c), out_dtype), name=name,
        compiler_params=_cparams(("parallel", "parallel"), 10 * tr * c * 4),
    )(core, g4, theirs)


def _sum_chips(order, own, landed, name):
    ns, r, c = own.shape
    tr = _pick(r, (368, 256, 184, 136, 128, 88, 8))

    def body(order_ref, a_ref, b1_ref, b2_ref, b3_ref, o_ref):
        o_ref[...] = ((a_ref[0].astype(F32) + b1_ref[0].astype(F32)) + b2_ref[0].astype(F32)) + b3_ref[0].astype(F32)

    def slot(k):
        return pl.BlockSpec((1, tr, c), lambda i, order_ref, k=k: (order_ref[k], i, 0))

    return pl.pallas_call(
        body,
        grid_spec=pltpu.PrefetchScalarGridSpec(
            num_scalar_prefetch=1, grid=(r // tr,), in_specs=[slot(0), slot(1), slot(2), slot(3)],
            out_specs=pl.BlockSpec((tr, c), lambda i, order_ref: (i, 0))),
        out_shape=jax.ShapeDtypeStruct((r, c), F32), name=name,
        compiler_params=_cparams(("parallel",), 16 * tr * c * 4),
    )(order, own, landed, landed, landed)


def _mesh_pos():
    return lax.axis_index("x"), lax.axis_index("y"), lax.axis_index("c")


def _all_gather(x, name):
    r, c = x.shape

    def body(x_ref, out_ref, send_sems, recv_sems):
        _gather_start(x_ref, out_ref, send_sems, recv_sems)
        _gather_finish(x_ref, out_ref, send_sems, recv_sems)

    gathered = pl.pallas_call(
        body, out_shape=jax.ShapeDtypeStruct((N_DEV, r, c), x.dtype),
        in_specs=[pl.BlockSpec(memory_space=pl.ANY)], out_specs=pl.BlockSpec(memory_space=pl.ANY),
        scratch_shapes=list(GATHER_SEMS), name=name,
    )(x)
    return _fill_own_slot(gathered, x)


GATHER_SEMS = (pltpu.SemaphoreType.DMA((7,)), pltpu.SemaphoreType.DMA((7,)))


def _fill_own_slot(gathered, x):
    mx, my, mc = _mesh_pos()
    return lax.dynamic_update_slice_in_dim(gathered, x[None], 4 * mx + 2 * my + mc, axis=0)


def _gather_copies(x_ref, out_ref, send_sems, recv_sems):
    mx, my, mc = _mesh_pos()
    me, sibling = (mx, my, mc), (mx, my, 1 - mc)
    chips = [(1 - mx, my), (mx, 1 - my), (1 - mx, 1 - my)]

    def slot(px, py, pc):
        return out_ref.at[4 * px + 2 * py + pc]

    def copy(k, block, to, src=None):
        return pltpu.make_async_remote_copy(
            src_ref=slot(*block) if src is None else src, dst_ref=slot(*block),
            send_sem=send_sems.at[k], recv_sem=recv_sems.at[k],
            device_id=to, device_id_type=pl.DeviceIdType.MESH)

    first = [copy(0, me, sibling, src=x_ref)]
    first += [copy(1 + j, me, (*chip, mc), src=x_ref) for j, chip in enumerate(chips)]
    passed = [copy(4 + j, (*chip, mc), sibling) for j, chip in enumerate(chips)]
    over_ici = [copy(1 + j, (*chip, mc), me) for j, chip in enumerate(chips)]
    over_d2d = [copy(0, sibling, me)] + [copy(4 + j, (*chip, 1 - mc), me) for j, chip in enumerate(chips)]
    return first, passed, over_ici, over_d2d


def _gather_start(x_ref, out_ref, send_sems, recv_sems):
    for cp in _gather_copies(x_ref, out_ref, send_sems, recv_sems)[0]:
        cp.start()


def _gather_finish(x_ref, out_ref, send_sems, recv_sems):
    first, passed, over_ici, over_d2d = _gather_copies(x_ref, out_ref, send_sems, recv_sems)
    for landed, relay in zip(over_ici, passed):
        landed.wait_recv()
        relay.start()
    for landed in over_d2d:
        landed.wait_recv()
    for cp in first + passed:
        cp.wait_send()


def _exchange_sibling(g4, name):
    nchip, _, r, c = g4.shape

    def body(g_ref, theirs_ref, send_sems, recv_sems):
        mx, my, mc = _mesh_pos()
        sends = [pltpu.make_async_remote_copy(
            src_ref=g_ref.at[chip, 1 - mc], dst_ref=theirs_ref.at[chip],
            send_sem=send_sems.at[chip], recv_sem=recv_sems.at[chip],
            device_id=(mx, my, 1 - mc), device_id_type=pl.DeviceIdType.MESH) for chip in range(nchip)]
        for cp in sends:
            cp.start()
        for cp in sends:
            cp.wait_recv()
        for cp in sends:
            cp.wait_send()

    any_spec = pl.BlockSpec(memory_space=pl.ANY)
    return pl.pallas_call(
        body, out_shape=jax.ShapeDtypeStruct((nchip, r, c), g4.dtype), in_specs=[any_spec], out_specs=any_spec,
        scratch_shapes=[pltpu.SemaphoreType.DMA((nchip,)), pltpu.SemaphoreType.DMA((nchip,))],
        name=name,
    )(g4)


def _exchange_chips(ts, name):
    def body(t_ref, out_ref, send_sems, recv_sems):
        _chips_start(t_ref, out_ref, send_sems, recv_sems)
        _chips_finish(t_ref, out_ref, send_sems, recv_sems)

    any_spec = pl.BlockSpec(memory_space=pl.ANY)
    return pl.pallas_call(
        body, out_shape=jax.ShapeDtypeStruct(ts.shape, ts.dtype), in_specs=[any_spec], out_specs=any_spec,
        scratch_shapes=list(CHIPS_SEMS), name=name,
    )(ts)


CHIPS_SEMS = (pltpu.SemaphoreType.DMA((N_CHIPS - 1,)), pltpu.SemaphoreType.DMA((N_CHIPS - 1,)))


def _chips_copies(t_ref, out_ref, send_sems, recv_sems):
    mx, my, mc = _mesh_pos()
    my_chip = 2 * mx + my
    copies = []
    for k in range(1, N_CHIPS):
        px = 1 - mx if k & 2 else mx
        py = 1 - my if k & 1 else my
        peer_chip = 2 * px + py

        def rdma(dst_slot, px=px, py=py, peer_chip=peer_chip, k=k):
            return pltpu.make_async_remote_copy(
                src_ref=t_ref.at[peer_chip], dst_ref=out_ref.at[dst_slot],
                send_sem=send_sems.at[k - 1], recv_sem=recv_sems.at[k - 1],
                device_id=(px, py, mc), device_id_type=pl.DeviceIdType.MESH)

        copies.append((rdma(my_chip), rdma(peer_chip)))
    return copies


def _chips_start(t_ref, out_ref, send_sems, recv_sems):
    for send, _ in _chips_copies(t_ref, out_ref, send_sems, recv_sems):
        send.start()


def _chips_finish(t_ref, out_ref, send_sems, recv_sems):
    copies = _chips_copies(t_ref, out_ref, send_sems, recv_sems)
    for _, landed in copies:
        landed.wait_recv()
    for send, _ in copies:
        send.wait_send()


def _chip_sums(g):
    _, r, c = g.shape
    mc = lax.axis_index("c")
    g4 = g.reshape(N_CHIPS, 2, r, c)
    theirs = _exchange_sibling(g4, name="exchange_grads_sibling")
    return _add_core_half(jnp.reshape(mc, (1,)).astype(jnp.int32), g4, theirs, BF16, name="add_sibling_grads")


def _sum_landed(chip_sums, landed):
    mx, my, _ = _mesh_pos()
    order = jnp.stack([2 * mx + my, 2 * (1 - mx) + my, 2 * mx + (1 - my), 2 * (1 - mx) + (1 - my)]).astype(jnp.int32)
    return _sum_chips(order, chip_sums, landed, name="sum_grads")


def _perm_mix_cols(wm):
    f0 = D_QKV
    f1 = f0 + N_HEADS
    pad = jnp.zeros((wm.shape[0], LANES - N_HEADS), wm.dtype)
    return jnp.concatenate([wm[:, :f0], wm[:, f1:], wm[:, f0:f1], pad], axis=1)


def _unperm_mix_rows(gt):
    f0 = D_QKV
    return jnp.concatenate([gt[:f0], gt[f0 + D_REST:f0 + D_REST + N_HEADS], gt[f0:f0 + D_REST]], axis=0)


def _pack_shards(parts, l, dtype):
    w1i, w1o, wmi, wmo, w2i, w2o = parts
    rows = [w1i[l].T, w1o[l], jnp.pad(wmi[l].T, ((0, MIX_ROWS_PAD - MIX_ROWS), (0, 0))), wmo[l], w2i[l].T, w2o[l]]
    return jnp.concatenate(rows, axis=0).astype(dtype)


def _layer_weights(wg):
    offs = {}
    o = 0
    for nm, n in (("f1i", FFN_ROWS), ("f1o", OUT_ROWS), ("mi", MIX_ROWS_PAD), ("mo", MO_ROWS), ("f2i", FFN_ROWS),
                  ("f2o", OUT_ROWS)):
        offs[nm] = (o, n)
        o += n

    def piece(nm, n_used=None):
        o, n = offs[nm]
        return wg[:, o:o + (n if n_used is None else n_used)]

    out = {}
    for tag in ("f1", "f2"):
        wi_t = piece(tag + "i").reshape(2 * D_FF, D_MODEL)
        wo = piece(tag + "o").reshape(D_FF, D_MODEL)
        out[tag] = dict(wi=wi_t.T, wi_t=wi_t, wo=wo, wo_t=wo.T)
    wm = _perm_mix_cols(piece("mi", MIX_ROWS).reshape(D_IN, D_MODEL).T)
    wm_t = wm.T
    wo = piece("mo").reshape(D_MODEL, D_MODEL)
    out["mix"] = dict(w_qkv=wm[:, :D_QKV], w_rest=wm[:, D_QKV:D_QKV + D_REST], w_f=wm[:, D_QKV + D_REST:],
                      wm_t=wm_t, wo=wo, wo_t=wo.T)
    return out


def _layer_grad_rows(gr):
    def halves(gate_t, up_t):
        return jnp.concatenate([gate_t.reshape(N_DEV // 2, FFN_ROWS, D_MODEL),
                                up_t.reshape(N_DEV // 2, FFN_ROWS, D_MODEL)], axis=0)

    gmi = _unperm_mix_rows(gr["mix_in_t"]).reshape(N_DEV, MIX_ROWS, D_MODEL)
    gmi = jnp.pad(gmi, ((0, 0), (0, MIX_ROWS_PAD - MIX_ROWS), (0, 0)))
    return jnp.concatenate(
        [halves(*gr["f1_in_t"]), gr["f1_out"].reshape(N_DEV, OUT_ROWS, D_MODEL), gmi,
         gr["mix_out"].reshape(N_DEV, MO_ROWS, D_MODEL), halves(*gr["f2_in_t"]),
         gr["f2_out"].reshape(N_DEV, OUT_ROWS, D_MODEL)], axis=1)


def _ffn_forward(x, gain, w):
    xn = _rmsnorm_fwd(x, gain, name="ffn_norm")
    h, zg, zu = _ffn_in(xn, w["wi"], name="ffn_in")
    x_new = _mm_nn(h, w["wo"], out_dtype=F32, res=x, alpha=0.5, name="ffn_out")
    return x_new, dict(x=x, xn=xn, h=h, zg=zg, zu=zu)


def _ffn_backward(dxo, dxo_b, gain, w, saved):
    dzg, dzu = _ffn_bwd_mid(dxo_b, w["wo_t"], saved["zg"], saved["zu"], name="ffn_bwd_mid")
    g_out = _mm_tn(saved["h"], dxo_b, alpha=0.5, tm=F_HALF, name="ffn_gw_out")
    g_in_t = (_mm_tn(dzg, saved["xn"], tm=F_HALF, name="ffn_gw_in"), _mm_tn(dzu, saved["xn"], tm=F_HALF, name="ffn_gw_in"))
    dxn = _mm_nn2(dzg, dzu, w["wi_t"], name="ffn_dxn")
    dx, dx_b, dg = _rmsnorm_bwd(saved["x"], gain, dxn, dxo, name="ffn_norm_bwd")
    return dx, dx_b, dg, g_in_t, g_out


def _mixer_forward(x, p, w, nb, seq, ta, next_pack=None):
    xn = _rmsnorm_fwd(x, p["norm"], name="mix_norm")
    qkv = _mm_nn(xn, w["w_qkv"], out_dtype=BF16, name="mix_qkv")
    rest = _mm_nn(xn, w["w_rest"], out_dtype=F32, name="mix_rest")
    fl = _mm_nn(xn, w["w_f"], out_dtype=F32, name="mix_f")
    qa, ka, vm = _fox_prep(fl, p["bf"], qkv, nb, seq)
    if next_pack is None:
        (y_attn, lse), next_gathered = _fox_fwd(qa, ka, vm, nb, seq, ta), None
    else:
        y_attn, lse, next_gathered = _fox_fwd(qa, ka, vm, nb, seq, ta, carry_gather=next_pack)
        next_gathered = _fill_own_slot(next_gathered, next_pack)
    y_pool = _pool_fwd(rest, p["wbd"], p["scale"], seq)
    y_conv = _conv_fwd(rest, p["cw"], seq)
    y = jnp.concatenate([y_attn, y_pool, y_conv], axis=1)
    x_new = _mm_nn(y, w["wo"], out_dtype=F32, res=x, alpha=1.0, name="mix_out")
    return x_new, dict(x=x, xn=xn, qa=qa, ka=ka, vm=vm, rest=rest, fl=fl, lse=lse, y=y), next_gathered


def _mixer_backward(dxo, dxo_b, p, w, sv, nb, seq, ta, pending=None):
    t = dxo.shape[0]
    dy = _mm_nn(dxo_b, w["wo_t"], out_dtype=F32, name="mix_dy")
    g_out = _mm_tn(sv["y"], dxo_b, name="mix_gw_out")
    res = _fox_bwd(sv["qa"], sv["ka"], sv["vm"], sv["y"], dy, sv["lse"], nb, seq, ta, carry_exchange=pending)
    dq, dk, dv, d_rows, d_cols = res[:5]
    landed = None if pending is None else res[5]
    ddh = (d_rows[..., 0].reshape(nb, N_HEADS, seq) - d_cols.reshape(nb, N_HEADS, seq)).transpose(0, 2, 1)
    ddh = ddh.reshape(t, N_HEADS)
    dfl, dbf = _fox_prep_bwd(jnp.pad(ddh, ((0, 0), (0, LANES - N_HEADS))), sv["fl"], p["bf"], seq)
    dpool, dwbd, dscale = _pool_bwd(sv["rest"], dy, p["wbd"], p["wbd_t"], p["scale"], seq)
    dconv, dcw = _conv_bwd(sv["rest"], dy, p["cw"], seq)
    dproj = jnp.concatenate([dq, dk.astype(BF16), dv.astype(BF16), dpool, dconv, dfl], axis=1)
    g_in_t = _mm_tn(dproj, sv["xn"], tm=D_INP // 3, name="mix_gw_in")
    dxn = _mm_nn(dproj, w["wm_t"], out_dtype=F32, name="mix_dxn")
    dx, dx_b, dg = _rmsnorm_bwd(sv["x"], p["norm"], dxn, dxo, name="mix_norm_bwd")
    return dx, dx_b, dict(norm=dg, bf=dbf, wbd=dwbd, scale=dscale, cw=dcw, mix_in_t=g_in_t, mix_out=g_out), landed


def _block_diag(wp):
    z = jnp.zeros((POOL_GROUP, POOL_GROUP), wp.dtype)
    return jnp.concatenate(
        [jnp.concatenate([wp[g] if g == r else z for g in range(4)], axis=1) for r in range(4)], axis=0)


def _row_pad(a, rows):
    a = a.reshape(-1, a.shape[-1])
    return jnp.pad(a, ((0, rows - a.shape[0]), (0, 0)))


def kernel(x, norm_ffn1, w_ffn1_in, w_ffn1_out, norm_mix, w_mix_in, b_forget, w_pool, pool_scale, conv_w, w_mix_out, norm_ffn2, w_ffn2_in, w_ffn2_out, norm_final, loss_target, m_norm_ffn1, m_w_ffn1_in, m_w_ffn1_out, m_norm_mix, m_w_mix_in, m_b_forget, m_w_pool, m_pool_scale, m_conv_w, m_w_mix_out, m_norm_ffn2, m_w_ffn2_in, m_w_ffn2_out, m_norm_final, v_norm_ffn1, v_w_ffn1_in, v_w_ffn1_out, v_norm_mix, v_w_mix_in, v_b_forget, v_w_pool, v_pool_scale, v_conv_w, v_w_mix_out, v_norm_ffn2, v_w_ffn2_in, v_w_ffn2_out, v_norm_final):
    nb, seq, d = x.shape
    depth = norm_ffn1.shape[0]
    t = nb * seq
    ta = _pick(seq, (ATT_TILE, 128))
    my_id = 4 * lax.axis_index("x") + 2 * lax.axis_index("y") + lax.axis_index("c")
    cshard = conv_w.shape[-1]

    shards = (w_ffn1_in, w_ffn1_out, w_mix_in, w_mix_out, w_ffn2_in, w_ffn2_out)
    wg = _all_gather(_pack_shards(shards, 0, BF16), name="gather_weights")
    cw_g = _all_gather(_row_pad(conv_w.reshape(depth * 3, cshard), 16).reshape(4, LANES), name="gather_conv_taps")
    cw_all = cw_g.reshape(N_DEV, 16, cshard)[:, :depth * 3].reshape(N_DEV, depth, 3, cshard)
    cw_all = cw_all.transpose(1, 2, 0, 3).reshape(depth, 3, D_CONV)

    xs = x.reshape(t, d)
    saved = []
    for l in range(depth):
        w = _layer_weights(wg)
        wbd = _block_diag(w_pool[l])
        p = dict(norm=norm_mix[l][None], bf=jnp.pad(b_forget[l], (0, LANES - N_HEADS))[None],
                 wbd=wbd.astype(BF16), wbd_t=wbd.T.astype(BF16), scale=pool_scale[l][None],
                 cw=_row_pad(cw_all[l], 8))
        xs, s1 = _ffn_forward(xs, norm_ffn1[l][None], w["f1"])
        next_pack = _pack_shards(shards, l + 1, BF16) if l + 1 < depth else None
        xs, sm, wg = _mixer_forward(xs, p, w["mix"], nb, seq, ta, next_pack)
        xs, s2 = _ffn_forward(xs, norm_ffn2[l][None], w["f2"])
        saved.append((w, p, s1, sm, s2))

    dx, dx_b, g_norm_final, loss_part = _final_loss_bwd(xs, norm_final[None], loss_target.reshape(t, d))
    layer_g = [None] * depth
    small = [None] * depth
    pending = None
    for l in reversed(range(depth)):
        w, p, s1, sm, s2 = saved[l]
        dx, dx_b, dg2, g2_in_t, g2_out = _ffn_backward(dx, dx_b, norm_ffn2[l][None], w["f2"], s2)
        dx, dx_b, gm, landed = _mixer_backward(dx, dx_b, p, w["mix"], sm, nb, seq, ta,
                                               None if pending is None else pending[1])
        if pending is not None:
            layer_g[pending[0]] = _sum_landed(pending[1], landed)
        dx, dx_b, dg1, g1_in_t, g1_out = _ffn_backward(dx, dx_b, norm_ffn1[l][None], w["f1"], s1)
        rows_l = _layer_grad_rows(dict(f1_in_t=g1_in_t, f1_out=g1_out, mix_in_t=gm["mix_in_t"],
                                       mix_out=gm["mix_out"], f2_in_t=g2_in_t, f2_out=g2_out))
        pending = (l, _chip_sums(rows_l))
        small[l] = dict(n1=dg1, nm=gm["norm"], n2=dg2, bf=gm["bf"], wbd=gm["wbd"], scale=gm["scale"], cw=gm["cw"])
    grad_x = dx.reshape(nb, seq, d)
    layer_g[pending[0]] = _sum_landed(pending[1], _exchange_chips(pending[1], name="exchange_grads_chips"))

    g_rows = jnp.stack(layer_g)
    o = 0
    pieces = {}
    for nm, n in (("f1i", FFN_ROWS), ("f1o", OUT_ROWS), ("mi", MIX_ROWS_PAD), ("mo", MO_ROWS), ("f2i", FFN_ROWS),
                  ("f2o", OUT_ROWS)):
        pieces[nm] = g_rows[:, o:o + n]
        o += n
    g_sharded = dict(
        w_ffn1_in=pieces["f1i"].transpose(0, 2, 1), w_ffn1_out=pieces["f1o"],
        w_mix_in=pieces["mi"][:, :MIX_ROWS].transpose(0, 2, 1), w_mix_out=pieces["mo"],
        w_ffn2_in=pieces["f2i"].transpose(0, 2, 1), w_ffn2_out=pieces["f2o"])

    def tile8(a):
        return jnp.pad(a, ((0, 8 - a.shape[0]), (0, D_MODEL - a.shape[1])))

    rows = []
    for l in range(depth):
        s = small[l]
        wp_rows = jnp.stack([s["wbd"][POOL_GROUP * g:POOL_GROUP * (g + 1), POOL_GROUP * g:POOL_GROUP * (g + 1)]
                             for g in range(4)]).reshape(16, D_MODEL)
        rows += [tile8(s["n1"]), tile8(s["nm"]), tile8(s["n2"]), tile8(s["bf"]), tile8(s["scale"]), tile8(s["cw"]),
                 wp_rows]
    rows += [tile8(g_norm_final), tile8(loss_part)]
    per_layer = 6 * 8 + 16
    small_sum = _sum_slots(_all_gather(jnp.concatenate(rows, axis=0), name="gather_small_grads"),
                           name="sum_small_grads")
    lay = small_sum[:depth * per_layer].reshape(depth, per_layer, D_MODEL)
    g_small = dict(
        norm_ffn1=lay[:, 0], norm_mix=lay[:, 8], norm_ffn2=lay[:, 16], b_forget=lay[:, 24, :N_HEADS],
        pool_scale=lay[:, 32, :D_POOL],
        conv_w=lax.dynamic_slice_in_dim(lay[:, 40:43, :D_CONV], my_id * cshard, cshard, axis=2),
        w_pool=lay[:, 48:64].reshape(depth, 4, POOL_GROUP, POOL_GROUP),
        norm_final=small_sum[depth * per_layer])
    loss = small_sum[depth * per_layer + 8, 0]

    given = dict(norm_ffn1=(norm_ffn1, m_norm_ffn1, v_norm_ffn1), w_ffn1_in=(w_ffn1_in, m_w_ffn1_in, v_w_ffn1_in),
                 w_ffn1_out=(w_ffn1_out, m_w_ffn1_out, v_w_ffn1_out), norm_mix=(norm_mix, m_norm_mix, v_norm_mix),
                 w_mix_in=(w_mix_in, m_w_mix_in, v_w_mix_in), b_forget=(b_forget, m_b_forget, v_b_forget),
                 w_pool=(w_pool, m_w_pool, v_w_pool), pool_scale=(pool_scale, m_pool_scale, v_pool_scale),
                 conv_w=(conv_w, m_conv_w, v_conv_w), w_mix_out=(w_mix_out, m_w_mix_out, v_w_mix_out),
                 norm_ffn2=(norm_ffn2, m_norm_ffn2, v_norm_ffn2), w_ffn2_in=(w_ffn2_in, m_w_ffn2_in, v_w_ffn2_in),
                 w_ffn2_out=(w_ffn2_out, m_w_ffn2_out, v_w_ffn2_out), norm_final=(norm_final, m_norm_final, v_norm_final))
    names = list(given)
    grads, deltas, new_m, new_v = {}, {}, {}, {}
    for nm in names:
        wv, mv, vv = given[nm]
        gv = (g_sharded[nm] if nm in g_sharded else g_small[nm]).reshape(wv.shape)
        shape2 = (-1, wv.shape[-1]) if wv.ndim > 1 else (1, wv.shape[0])
        dl, mn, vn = _adamw(wv.reshape(shape2), gv.reshape(shape2), mv.reshape(shape2), vv.reshape(shape2),
                            name="adamw_" + nm)
        grads[nm], deltas[nm], new_m[nm], new_v[nm] = gv, dl.reshape(wv.shape), mn.reshape(wv.shape), vn.reshape(wv.shape)
    return (loss, grad_x, *[grads[n] for n in names], *[deltas[n] for n in names],
            *[new_m[n] for n in names], *[new_v[n] for n in names])
```

```python
import functools

import jax
import jax.numpy as jnp
from jax import lax
from jax.experimental import pallas as pl
from jax.experimental.pallas import tpu as pltpu

F32 = jnp.float32
BF16 = jnp.bfloat16

D_MODEL = 1024
D_FF = 2816
HEAD_DIM = 64
N_HEADS = 8
N_PAIRS = N_HEADS // 2
D_ATTN = 512
D_POOL = 256
D_CONV = 256
POOL_WINDOWS = (2, 4, 8, 16)
POOL_GROUP = 64
D_IN = 2568
RMS_EPS = 1e-6
ADAM_LR, ADAM_B1, ADAM_B2, ADAM_EPS, ADAM_WD, ADAM_STEP = 0.001, 0.9, 0.999, 1e-08, 0.01, 10

N_DEV = 8
N_CHIPS = 4
LANES = 128
VMEM_BYTES_V7X = 64 * 1024 * 1024
VMEM_LIMIT_MAX = VMEM_BYTES_V7X - 8 * 1024 * 1024

F_HALF = D_FF // 2
D_QKV = 3 * D_ATTN
D_REST = D_POOL + 3 * D_CONV
D_INP = D_QKV + D_REST + LANES
MIX_ROWS = 321
MIX_ROWS_PAD = 336
FFN_ROWS = 704
OUT_ROWS = 352
MO_ROWS = 128
LAYER_ROWS = 2 * (FFN_ROWS + OUT_ROWS) + MIX_ROWS_PAD + MO_ROWS
NEG_BIG = -1e30
ATT_SCALE = HEAD_DIM ** -0.5
ATT_K = 2 * LANES
ATT_TILE = 256
ATT_PAIRS_FWD = 4
ATT_PAIRS_BWD = 4
MXU_COLS = 256


def _cparams(sem, vmem_bytes):
    limit = int(min(max(vmem_bytes, 16 * 1024 * 1024), VMEM_LIMIT_MAX))
    return pltpu.CompilerParams(dimension_semantics=sem, vmem_limit_bytes=limit)


def _nbytes(shape, dtype):
    n = 1
    for s in shape:
        n *= s
    return n * jnp.dtype(dtype).itemsize


def _pick(n, prefs):
    for p in prefs:
        if n % p == 0:
            return p
    return n


def _rmsnorm_fwd(x, g, name):
    t, d = x.shape
    tm = _pick(t, (512, 256, 128))

    def body(x_ref, g_ref, o_ref):
        xv = x_ref[...]
        r = lax.rsqrt(jnp.mean(xv * xv, axis=-1, keepdims=True) + RMS_EPS)
        o_ref[...] = ((xv * r) * g_ref[...]).astype(o_ref.dtype)

    return pl.pallas_call(
        body, grid=(t // tm,),
        in_specs=[pl.BlockSpec((tm, d), lambda i: (i, 0)), pl.BlockSpec((1, d), lambda i: (0, 0))],
        out_specs=pl.BlockSpec((tm, d), lambda i: (i, 0)),
        out_shape=jax.ShapeDtypeStruct((t, d), BF16), name=name,
        compiler_params=_cparams(("parallel",), 6 * tm * d * 4),
    )(x, g)


def _mm_nn(a, b, *, out_dtype, name, res=None, alpha=1.0, tn=None):
    m, k = a.shape
    n = b.shape[1]
    tn = n if tn is None else tn
    tm = _pick(m, (512, 256, 128))
    with_res = res is not None

    def body(*refs):
        if with_res:
            a_ref, b_ref, r_ref, o_ref = refs
        else:
            a_ref, b_ref, o_ref = refs
        acc = jnp.dot(a_ref[...], b_ref[...], preferred_element_type=F32)
        if with_res:
            acc = r_ref[...] + alpha * acc
        o_ref[...] = acc.astype(o_ref.dtype)

    in_specs = [pl.BlockSpec((tm, k), lambda j, i: (i, 0)), pl.BlockSpec((k, tn), lambda j, i: (0, j))]
    args = [a, b]
    if with_res:
        in_specs.append(pl.BlockSpec((tm, tn), lambda j, i: (i, j)))
        args.append(res)
    vmem = 2 * (_nbytes((tm, k), BF16) + _nbytes((k, tn), BF16) + 3 * _nbytes((tm, tn), F32))
    return pl.pallas_call(
        body, grid=(n // tn, m // tm), in_specs=in_specs,
        out_specs=pl.BlockSpec((tm, tn), lambda j, i: (i, j)),
        out_shape=jax.ShapeDtypeStruct((m, n), out_dtype), name=name,
        compiler_params=_cparams(("parallel", "parallel"), vmem),
    )(*args)


def _mm_tn(a, b, *, name, alpha=1.0, tm=None):
    t, m = a.shape
    n = b.shape[1]
    tm = m if tm is None else tm
    tk = _pick(t, (1024, 512, 256, 128))
    nk = t // tk

    def body(a_ref, b_ref, o_ref):
        kk = pl.program_id(1)
        p = lax.dot_general(a_ref[...], b_ref[...], (((0,), (0,)), ((), ())), preferred_element_type=F32)
        if alpha != 1.0:
            p = alpha * p

        @pl.when(kk == 0)
        def _():
            o_ref[...] = p

        @pl.when(kk > 0)
        def _():
            o_ref[...] += p

    vmem = 2 * (_nbytes((tk, tm), BF16) + _nbytes((tk, n), BF16) + 2 * _nbytes((tm, n), F32))
    return pl.pallas_call(
        body, grid=(m // tm, nk),
        in_specs=[pl.BlockSpec((tk, tm), lambda i, kk: (kk, i)), pl.BlockSpec((tk, n), lambda i, kk: (kk, 0))],
        out_specs=pl.BlockSpec((tm, n), lambda i, kk: (i, 0)),
        out_shape=jax.ShapeDtypeStruct((m, n), F32), name=name,
        compiler_params=_cparams(("parallel", "arbitrary"), vmem),
    )(a, b)


def _sigmoid(v):
    return 1.0 / (1.0 + jnp.exp(-v))


def _col_chunks(n, width):
    return [(c, min(width, n - c)) for c in range(0, n, width)]


def _ffn_in(xn, w, name):
    t, d = xn.shape
    tm = _pick(t, (512, 256, 128))

    def body(x_ref, wg_ref, wu_ref, h_ref, pg_ref, pu_ref):
        xv = x_ref[...]
        for c0, cw in _col_chunks(F_HALF, MXU_COLS):
            cols = slice(c0, c0 + cw)
            g = jnp.dot(xv, wg_ref[:, cols], preferred_element_type=F32)
            u = jnp.dot(xv, wu_ref[:, cols], preferred_element_type=F32)
            s = _sigmoid(g)
            silu = g * s
            h_ref[:, cols] = (silu * u).astype(h_ref.dtype)
            pg_ref[:, cols] = (u * (s * (1.0 + g * (1.0 - s)))).astype(pg_ref.dtype)
            pu_ref[:, cols] = silu.astype(pu_ref.dtype)

    vmem = 2 * (_nbytes((tm, d), BF16) + 2 * _nbytes((d, F_HALF), BF16) + 4 * _nbytes((tm, D_FF), F32))
    out_blk = pl.BlockSpec((tm, F_HALF), lambda j, i: (i, j))
    sds = jax.ShapeDtypeStruct((t, D_FF), BF16)
    return pl.pallas_call(
        body, grid=(2, t // tm),
        in_specs=[pl.BlockSpec((tm, d), lambda j, i: (i, 0)), pl.BlockSpec((d, F_HALF), lambda j, i: (0, j)),
                  pl.BlockSpec((d, F_HALF), lambda j, i: (0, 2 + j))],
        out_specs=[out_blk, out_blk, out_blk], out_shape=[sds, sds, sds], name=name,
        compiler_params=_cparams(("parallel", "parallel"), vmem),
    )(xn, w, w)


def _ffn_bwd_mid(dxo, w_out_t, pg, pu, name, carry_sibling=None):
    t, d = dxo.shape
    tm = _pick(t, (512, 256, 128))
    grid = (2, t // tm)

    def body(d_ref, w_ref, pg_ref, pu_ref, *rest):
        if carry_sibling is None:
            dg_ref, du_ref = rest
        else:
            g4_ref, dg_ref, du_ref, theirs_ref, send_sems, recv_sems = rest
            first_step, last_step = _grid_ends([pl.program_id(a) for a in range(2)], grid)

            @pl.when(first_step)
            def _():
                _sibling_start(g4_ref, theirs_ref, send_sems, recv_sems)

        dv = d_ref[...]
        for c0, cw in _col_chunks(F_HALF, MXU_COLS):
            cols = slice(c0, c0 + cw)
            dh = 0.5 * jnp.dot(dv, w_ref[:, cols], preferred_element_type=F32)
            dg_ref[:, cols] = (dh * pg_ref[:, cols].astype(F32)).astype(dg_ref.dtype)
            du_ref[:, cols] = (dh * pu_ref[:, cols].astype(F32)).astype(du_ref.dtype)

        if carry_sibling is not None:
            @pl.when(last_step)
            def _():
                _sibling_finish(g4_ref, theirs_ref, send_sems, recv_sems)

    vmem = 2 * (_nbytes((tm, d), BF16) + _nbytes((d, F_HALF), BF16) + 5 * _nbytes((tm, D_FF), F32))
    blk = pl.BlockSpec((tm, F_HALF), lambda j, i: (i, j))
    sds = jax.ShapeDtypeStruct((t, D_FF), BF16)
    in_specs = [pl.BlockSpec((tm, d), lambda j, i: (i, 0)), pl.BlockSpec((d, F_HALF), lambda j, i: (0, j)), blk, blk]
    if carry_sibling is None:
        return pl.pallas_call(
            body, grid=grid, in_specs=in_specs, out_specs=[blk, blk], out_shape=[sds, sds], name=name,
            compiler_params=_cparams(("parallel", "parallel"), vmem),
        )(dxo, w_out_t, pg, pu)
    nchip, _, r, c = carry_sibling.shape
    any_spec = pl.BlockSpec(memory_space=pl.ANY)
    return pl.pallas_call(
        body, grid=grid, in_specs=in_specs + [any_spec], out_specs=[blk, blk, any_spec],
        out_shape=[sds, sds, jax.ShapeDtypeStruct((nchip, r, c), carry_sibling.dtype)],
        scratch_shapes=list(SIBLING_SEMS), name=name + "_exchange",
        compiler_params=_cparams(("arbitrary", "arbitrary"), vmem),
    )(dxo, w_out_t, pg, pu, carry_sibling)


def _mm_nn2(a1, a2, b, name):
    m, k = a1.shape
    n = b.shape[1]
    tm = _pick(m, (512, 256, 128))

    def body(a1_ref, a2_ref, b1_ref, b2_ref, o_ref):
        o_ref[...] = (jnp.dot(a1_ref[...], b1_ref[...], preferred_element_type=F32)
                      + jnp.dot(a2_ref[...], b2_ref[...], preferred_element_type=F32))

    a_blk = pl.BlockSpec((tm, k), lambda i: (i, 0))
    vmem = 2 * (2 * _nbytes((tm, k), BF16) + 2 * _nbytes((k, n), BF16) + 3 * _nbytes((tm, n), F32))
    return pl.pallas_call(
        body, grid=(m // tm,),
        in_specs=[a_blk, a_blk, pl.BlockSpec((k, n), lambda i: (0, 0)), pl.BlockSpec((k, n), lambda i: (1, 0))],
        out_specs=pl.BlockSpec((tm, n), lambda i: (i, 0)), out_shape=jax.ShapeDtypeStruct((m, n), F32), name=name,
        compiler_params=_cparams(("parallel",), vmem),
    )(a1, a2, b, b)


def _rmsnorm_bwd(x, g, dxn, dxo, name):
    t, d = x.shape
    tm = _pick(t, (512, 256, 128))

    def body(x_ref, g_ref, dn_ref, do_ref, dx_ref, dxb_ref, dg_ref):
        i = pl.program_id(0)
        xv = x_ref[...]
        r = lax.rsqrt(jnp.mean(xv * xv, axis=-1, keepdims=True) + RMS_EPS)
        xh = xv * r
        dn = dn_ref[...]
        dgp = jnp.sum(dn * xh, axis=0, keepdims=True)
        dh = dn * g_ref[...]
        dx = do_ref[...] + r * (dh - xh * jnp.mean(dh * xh, axis=-1, keepdims=True))
        dx_ref[...] = dx
        dxb_ref[...] = dx.astype(dxb_ref.dtype)

        @pl.when(i == 0)
        def _():
            dg_ref[...] = dgp

        @pl.when(i > 0)
        def _():
            dg_ref[...] += dgp

    blk = pl.BlockSpec((tm, d), lambda i: (i, 0))
    row = pl.BlockSpec((1, d), lambda i: (0, 0))
    return pl.pallas_call(
        body, grid=(t // tm,), in_specs=[blk, row, blk, blk], out_specs=[blk, blk, row],
        out_shape=[jax.ShapeDtypeStruct((t, d), F32), jax.ShapeDtypeStruct((t, d), BF16),
                   jax.ShapeDtypeStruct((1, d), F32)], name=name,
        compiler_params=_cparams(("arbitrary",), 16 * tm * d * 4),
    )(x, g, dxn, dxo)


def _final_loss_bwd(x, g, tgt):
    t, d = x.shape
    tm = _pick(t, (512, 256, 128))

    def body(x_ref, g_ref, t_ref, dx_ref, dxb_ref, dg_ref, loss_ref):
        i = pl.program_id(0)
        xv = x_ref[...]
        r = lax.rsqrt(jnp.mean(xv * xv, axis=-1, keepdims=True) + RMS_EPS)
        xh = xv * r
        gv = g_ref[...]
        err = xh * gv - t_ref[...]
        lp = 0.5 * jnp.sum(jnp.mean(err * err, axis=-1, keepdims=True), axis=0, keepdims=True)
        dy = err * (1.0 / d)
        dgp = jnp.sum(dy * xh, axis=0, keepdims=True)
        dh = dy * gv
        dx = r * (dh - xh * jnp.mean(dh * xh, axis=-1, keepdims=True))
        dx_ref[...] = dx
        dxb_ref[...] = dx.astype(dxb_ref.dtype)
        lpb = jnp.broadcast_to(lp, (1, LANES))

        @pl.when(i == 0)
        def _():
            dg_ref[...] = dgp
            loss_ref[...] = lpb

        @pl.when(i > 0)
        def _():
            dg_ref[...] += dgp
            loss_ref[...] += lpb

    blk = pl.BlockSpec((tm, d), lambda i: (i, 0))
    row = pl.BlockSpec((1, d), lambda i: (0, 0))
    return pl.pallas_call(
        body, grid=(t // tm,), in_specs=[blk, row, blk],
        out_specs=[blk, blk, row, pl.BlockSpec((1, LANES), lambda i: (0, 0))],
        out_shape=[jax.ShapeDtypeStruct((t, d), F32), jax.ShapeDtypeStruct((t, d), BF16),
                   jax.ShapeDtypeStruct((1, d), F32), jax.ShapeDtypeStruct((1, LANES), F32)], name="final_loss_bwd",
        compiler_params=_cparams(("arbitrary",), 16 * tm * d * 4),
    )(x, g, tgt)


def _seq_scan(v, seq, reverse):
    row = lax.broadcasted_iota(jnp.int32, v.shape, 0)
    k = 1
    while k < seq:
        if reverse:
            v = v + jnp.where(row < seq - k, pltpu.roll(v, seq - k, 0), 0.0)
        else:
            v = v + jnp.where(row >= k, pltpu.roll(v, k, 0), 0.0)
        k *= 2
    return v


def _log_sigmoid(v):
    return jnp.minimum(v, 0.0) - jnp.log(1.0 + jnp.exp(-jnp.abs(v)))


def _fox_prep(fl, bf, qkv, nb, seq):
    def body(f_ref, b_ref, q_ref, k_ref, v_ref, qa_ref, ka_ref, vm_ref):
        dsum = _seq_scan(_log_sigmoid(f_ref[...] + b_ref[...]), seq, False)
        d1 = dsum.astype(BF16).astype(F32)
        r1 = dsum - d1
        d2 = r1.astype(BF16).astype(F32)
        d3 = (r1 - d2).astype(BF16).astype(F32)
        lane = lax.broadcasted_iota(jnp.int32, (seq, LANES), 1)
        first = lane < HEAD_DIM
        l64 = jnp.where(first, lane, lane - HEAD_DIM)
        for p in range(N_PAIRS):
            def head_cols(a, p=p):
                return jnp.where(first, a[:, 2 * p:2 * p + 1], a[:, 2 * p + 1:2 * p + 2])

            e1, e2, e3 = head_cols(d1), head_cols(d2), head_cols(d3)
            aux_q = jnp.where(l64 == 0, e1, jnp.where(l64 == 1, e2, jnp.where(l64 == 2, e3,
                              jnp.where(l64 < 6, 1.0, 0.0)))).astype(BF16)
            aux_k = jnp.where(l64 < 3, 1.0, jnp.where(l64 == 3, -e1, jnp.where(l64 == 4, -e2,
                              jnp.where(l64 == 5, -e3, 0.0)))).astype(BF16)
            cols = slice(LANES * p, LANES * (p + 1))
            qs = q_ref[:, cols] * ATT_SCALE
            vp = v_ref[:, cols]
            zero = jnp.zeros_like(qs)
            qa_ref[0, p, 0, :, :LANES] = jnp.where(first, qs, zero)
            qa_ref[0, p, 0, :, LANES:] = jnp.where(first, aux_q, zero)
            qa_ref[0, p, 1, :, :LANES] = jnp.where(first, zero, qs)
            qa_ref[0, p, 1, :, LANES:] = jnp.where(first, zero, aux_q)
            ka_ref[0, p, :, :LANES] = k_ref[:, cols]
            ka_ref[0, p, :, LANES:] = aux_k
            vm_ref[0, p, 0] = jnp.where(first, vp, zero)
            vm_ref[0, p, 1] = jnp.where(first, zero, vp)

    def part(c):
        return pl.BlockSpec((seq, D_ATTN), lambda b, c=c: (b, c))

    return pl.pallas_call(
        body, grid=(nb,),
        in_specs=[pl.BlockSpec((seq, LANES), lambda b: (b, 0)), pl.BlockSpec((1, LANES), lambda b: (0, 0)),
                  part(0), part(1), part(2)],
        out_specs=[pl.BlockSpec((1, N_PAIRS, 2, seq, ATT_K), lambda b: (b, 0, 0, 0, 0)),
                   pl.BlockSpec((1, N_PAIRS, seq, ATT_K), lambda b: (b, 0, 0, 0)),
                   pl.BlockSpec((1, N_PAIRS, 2, seq, LANES), lambda b: (b, 0, 0, 0, 0))],
        out_shape=[jax.ShapeDtypeStruct((nb, N_PAIRS, 2, seq, ATT_K), BF16),
                   jax.ShapeDtypeStruct((nb, N_PAIRS, seq, ATT_K), BF16),
                   jax.ShapeDtypeStruct((nb, N_PAIRS, 2, seq, LANES), BF16)],
        name="fox_prep", compiler_params=_cparams(("parallel",), 48 * 1024 * 1024),
    )(fl, bf, qkv, qkv, qkv)


def _fox_prep_bwd(dd, fl, bf, seq):
    t = fl.shape[0]

    def body(d_ref, f_ref, b_ref, o_ref, db_ref):
        i = pl.program_id(0)
        dlog = _seq_scan(d_ref[...], seq, True)
        dfl = dlog * _sigmoid(-(f_ref[...] + b_ref[...]))
        o_ref[...] = dfl.astype(o_ref.dtype)
        dbp = jnp.sum(dfl, axis=0, keepdims=True)

        @pl.when(i == 0)
        def _():
            db_ref[...] = dbp

        @pl.when(i > 0)
        def _():
            db_ref[...] += dbp

    blk = pl.BlockSpec((seq, LANES), lambda b: (b, 0))
    row = pl.BlockSpec((1, LANES), lambda b: (0, 0))
    return pl.pallas_call(
        body, grid=(t // seq,), in_specs=[blk, blk, row], out_specs=[blk, row],
        out_shape=[jax.ShapeDtypeStruct((t, LANES), BF16), jax.ShapeDtypeStruct((1, LANES), F32)], name="fox_prep_bwd",
        compiler_params=_cparams(("arbitrary",), 24 * seq * LANES * 4),
    )(dd, fl, bf)


def _pair_rows(a, ta):
    lane = lax.broadcasted_iota(jnp.int32, (ta, LANES), 1)
    return jnp.where(lane < HEAD_DIM, a[:ta], a[ta:])


def _diag_mask(ta):
    r = lax.broadcasted_iota(jnp.int32, (2 * ta, ta), 0)
    c = lax.broadcasted_iota(jnp.int32, (2 * ta, ta), 1)
    return c <= jnp.where(r >= ta, r - ta, r)


def _nt(a, b):
    return lax.dot_general(a, b, (((1,), (1,)), ((), ())), preferred_element_type=F32)


def _tn(a, b):
    return lax.dot_general(a, b, (((0,), (0,)), ((), ())), preferred_element_type=F32)


def _grid_ends(ids, sizes):
    first = functools.reduce(jnp.logical_and, [i == 0 for i in ids])
    last = functools.reduce(jnp.logical_and, [i == n - 1 for i, n in zip(ids, sizes)])
    return first, last


def _fox_fwd(qa, ka, vm, nb, seq, ta, carry_gather=None):
    nq = seq // ta
    npp = ATT_PAIRS_FWD
    grid = (nb, N_PAIRS // npp, nq)

    def body(q_ref, k_ref, v_ref, *rest):
        if carry_gather is None:
            o_ref, lse_ref = rest
        else:
            x_ref, o_ref, lse_ref, gathered_ref, send_sems, recv_sems = rest
            first_step, last_step = _grid_ends([pl.program_id(a) for a in range(3)], grid)

            @pl.when(first_step)
            def _():
                _gather_start(x_ref, gathered_ref, send_sems, recv_sems)

        i = pl.program_id(2)
        q2s = [q_ref[0, pp].reshape(2 * ta, ATT_K) for pp in range(npp)]

        def step(j, carry, masked):
            rows = pl.ds(pl.multiple_of(j * ta, ta), ta)
            out = []
            for pp in range(npp):
                m, l, acc = carry[pp]
                s = _nt(q2s[pp], k_ref[0, pp, rows, :])
                if masked:
                    s = jnp.where(_diag_mask(ta), s, NEG_BIG)
                m_new = jnp.maximum(m, jnp.max(s, axis=-1, keepdims=True))
                p = jnp.exp(s - m_new)
                corr = jnp.exp(m - m_new)
                l = corr * l + jnp.sum(p, axis=-1, keepdims=True)
                pb = p.astype(BF16)
                pv = (jnp.dot(pb[:ta], v_ref[0, pp, 0, rows, :], preferred_element_type=F32)
                      + jnp.dot(pb[ta:], v_ref[0, pp, 1, rows, :], preferred_element_type=F32))
                out.append((m_new, l, _pair_rows(corr, ta) * acc + pv))
            return tuple(out)

        init = tuple((jnp.full((2 * ta, 1), NEG_BIG, F32), jnp.zeros((2 * ta, 1), F32),
                      jnp.zeros((ta, LANES), F32)) for _ in range(npp))
        carry = lax.fori_loop(0, i, functools.partial(step, masked=False), init)
        for pp, (m, l, acc) in enumerate(step(i, carry, True)):
            o_ref[:, LANES * pp:LANES * (pp + 1)] = (acc * _pair_rows(1.0 / l, ta)).astype(o_ref.dtype)
            lse = m + jnp.log(l)
            lse_ref[0, pp, 0] = lse[:ta]
            lse_ref[0, pp, 1] = lse[ta:]

        if carry_gather is not None:
            @pl.when(last_step)
            def _():
                _gather_finish(x_ref, gathered_ref, send_sems, recv_sems)

    vmem = (2 * npp * (_nbytes((seq, ATT_K), BF16) + 2 * _nbytes((seq, LANES), BF16)) + 24 * npp * ta * ta * 4
            + 8 * 1024 * 1024)
    in_specs = [pl.BlockSpec((1, npp, 2, ta, ATT_K), lambda b, g, i: (b, g, 0, i, 0)),
                pl.BlockSpec((1, npp, seq, ATT_K), lambda b, g, i: (b, g, 0, 0)),
                pl.BlockSpec((1, npp, 2, seq, LANES), lambda b, g, i: (b, g, 0, 0, 0))]
    out_specs = [pl.BlockSpec((ta, LANES * npp), lambda b, g, i: (b * nq + i, g)),
                 pl.BlockSpec((1, npp, 2, ta, 1), lambda b, g, i: (b, g, 0, i, 0))]
    out_shape = [jax.ShapeDtypeStruct((nb * seq, D_ATTN), BF16), jax.ShapeDtypeStruct((nb, N_PAIRS, 2, seq, 1), F32)]
    if carry_gather is None:
        return pl.pallas_call(
            body, grid=grid, in_specs=in_specs, out_specs=out_specs, out_shape=out_shape, name="fox_fwd",
            compiler_params=_cparams(("parallel", "parallel", "parallel"), vmem),
        )(qa, ka, vm)
    any_spec = pl.BlockSpec(memory_space=pl.ANY)
    return pl.pallas_call(
        body, grid=grid, in_specs=in_specs + [any_spec], out_specs=out_specs + [any_spec],
        out_shape=out_shape + [jax.ShapeDtypeStruct((N_DEV,) + carry_gather.shape, carry_gather.dtype)],
        scratch_shapes=list(GATHER_SEMS), name="fox_fwd_gather",
        compiler_params=_cparams(("arbitrary", "arbitrary", "arbitrary"), vmem),
    )(qa, ka, vm, carry_gather)


def _fox_bwd(qa, ka, vm, y, dy, lse, nb, seq, ta, carry_exchange=None):
    nq = seq // ta
    npp = ATT_PAIRS_BWD
    grid = (nb, N_PAIRS // npp, nq)

    def body(q_ref, k_ref, v_ref, o_ref, do_ref, lse_ref, *rest):
        if carry_exchange is None:
            dq_ref, dk_ref, dv_ref, rs_ref, cs_ref = rest
        else:
            t_ref, dq_ref, dk_ref, dv_ref, rs_ref, cs_ref, landed_ref, send_sems, recv_sems = rest
            first_step, last_step = _grid_ends([pl.program_id(a) for a in range(3)], grid)

            @pl.when(first_step)
            def _():
                _chips_start(t_ref, landed_ref, send_sems, recv_sems)

        i = pl.program_id(2)

        @pl.when(i == 0)
        def _():
            dk_ref[...] = jnp.zeros_like(dk_ref)
            dv_ref[...] = jnp.zeros_like(dv_ref)
            cs_ref[...] = jnp.zeros_like(cs_ref)

        first = lax.broadcasted_iota(jnp.int32, (ta, LANES), 1) < HEAD_DIM
        q2s, do2s, deltas, lses = [], [], [], []
        for pp in range(npp):
            cols = slice(LANES * pp, LANES * (pp + 1))
            q2s.append(q_ref[0, pp].reshape(2 * ta, ATT_K))
            do = do_ref[:, cols]
            doo = do * o_ref[:, cols].astype(F32)
            do2s.append(jnp.concatenate([jnp.where(first, do, 0.0), jnp.where(first, 0.0, do)], axis=0).astype(BF16))
            deltas.append(jnp.concatenate([jnp.sum(jnp.where(first, doo, 0.0), axis=-1, keepdims=True),
                                           jnp.sum(jnp.where(first, 0.0, doo), axis=-1, keepdims=True)], axis=0))
            lses.append(jnp.concatenate([lse_ref[0, pp, 0], lse_ref[0, pp, 1]], axis=0))

        def step(j, carry, masked):
            rows = pl.ds(pl.multiple_of(j * ta, ta), ta)
            out = []
            for pp in range(npp):
                dq_acc, rs_acc = carry[pp]
                cols = slice(LANES * pp, LANES * (pp + 1))
                ks = k_ref[0, pp, rows, :]
                s = _nt(q2s[pp], ks)
                if masked:
                    s = jnp.where(_diag_mask(ta), s, NEG_BIG)
                p = jnp.exp(s - lses[pp])
                dp = _nt(do2s[pp], v_ref[0, pp, 0, rows, :] + v_ref[0, pp, 1, rows, :])
                ds32 = p * (dp - deltas[pp])
                ds = ds32.astype(BF16)
                dk_ref[rows, cols] += _tn(ds, q2s[pp][:, :LANES])
                dv_ref[rows, cols] += _tn(p.astype(BF16), do2s[pp])
                cs_ref[0, pp, 0, j] += jnp.sum(ds32[:ta], axis=0, keepdims=True)
                cs_ref[0, pp, 1, j] += jnp.sum(ds32[ta:], axis=0, keepdims=True)
                out.append((dq_acc + jnp.dot(ds, ks[:, :LANES], preferred_element_type=F32),
                            rs_acc + jnp.sum(ds32, axis=-1, keepdims=True)))
            return tuple(out)

        init = tuple((jnp.zeros((2 * ta, LANES), F32), jnp.zeros((2 * ta, 1), F32)) for _ in range(npp))
        carry = lax.fori_loop(0, i, functools.partial(step, masked=False), init)
        for pp, (dq_acc, rs_acc) in enumerate(step(i, carry, True)):
            dq = jnp.where(first, dq_acc[:ta], dq_acc[ta:]) * ATT_SCALE
            dq_ref[:, LANES * pp:LANES * (pp + 1)] = dq.astype(dq_ref.dtype)
            rs_row = jnp.transpose(jnp.broadcast_to(rs_acc, (2 * ta, LANES)))[0:1]
            rs_ref[0, pp, 0, 0] = rs_row[:, :ta]
            rs_ref[0, pp, 1, 0] = rs_row[:, ta:]

        if carry_exchange is not None:
            @pl.when(last_step)
            def _():
                _chips_finish(t_ref, landed_ref, send_sems, recv_sems)

    vmem = (2 * npp * (_nbytes((seq, ATT_K), BF16) + 2 * _nbytes((seq, LANES), BF16) + 2 * _nbytes((seq, LANES), F32))
            + 32 * npp * ta * ta * 4 + 8 * 1024 * 1024)
    qblk = lambda b, g, i: (b * nq + i, g)
    acc_blk = pl.BlockSpec((seq, LANES * npp), lambda b, g, i: (b, g))
    in_specs = [pl.BlockSpec((1, npp, 2, ta, ATT_K), lambda b, g, i: (b, g, 0, i, 0)),
                pl.BlockSpec((1, npp, seq, ATT_K), lambda b, g, i: (b, g, 0, 0)),
                pl.BlockSpec((1, npp, 2, seq, LANES), lambda b, g, i: (b, g, 0, 0, 0)),
                pl.BlockSpec((ta, LANES * npp), qblk), pl.BlockSpec((ta, LANES * npp), qblk),
                pl.BlockSpec((1, npp, 2, ta, 1), lambda b, g, i: (b, g, 0, i, 0))]
    out_specs = [pl.BlockSpec((ta, LANES * npp), qblk), acc_blk, acc_blk,
                 pl.BlockSpec((1, npp, 2, 1, 1, ta), lambda b, g, i: (b, g, 0, i, 0, 0)),
                 pl.BlockSpec((1, npp, 2, nq, 1, ta), lambda b, g, i: (b, g, 0, 0, 0, 0))]
    sums = jax.ShapeDtypeStruct((nb, N_PAIRS, 2, nq, 1, ta), F32)
    out_shape = [jax.ShapeDtypeStruct((nb * seq, D_ATTN), BF16), jax.ShapeDtypeStruct((nb * seq, D_ATTN), F32),
                 jax.ShapeDtypeStruct((nb * seq, D_ATTN), F32), sums, sums]
    if carry_exchange is None:
        return pl.pallas_call(
            body, grid=grid, in_specs=in_specs, out_specs=out_specs, out_shape=out_shape, name="fox_bwd",
            compiler_params=_cparams(("parallel", "parallel", "arbitrary"), vmem),
        )(qa, ka, vm, y, dy, lse)
    any_spec = pl.BlockSpec(memory_space=pl.ANY)
    return pl.pallas_call(
        body, grid=grid, in_specs=in_specs + [any_spec], out_specs=out_specs + [any_spec],
        out_shape=out_shape + [jax.ShapeDtypeStruct(carry_exchange.shape, carry_exchange.dtype)],
        scratch_shapes=list(CHIPS_SEMS), name="fox_bwd_exchange",
        compiler_params=_cparams(("arbitrary", "arbitrary", "arbitrary"), vmem),
    )(qa, ka, vm, y, dy, lse, carry_exchange)


def _shift_down(a, k):
    row = lax.broadcasted_iota(jnp.int32, a.shape, 0)
    return jnp.where(row >= k, pltpu.roll(a, k, 0), 0.0)


def _shift_up(a, k):
    n = a.shape[0]
    row = lax.broadcasted_iota(jnp.int32, a.shape, 0)
    return jnp.where(row < n - k, pltpu.roll(a, n - k, 0), 0.0)


def _by_group(vals, shape):
    lane = lax.broadcasted_iota(jnp.int32, shape, 1)
    out = vals[-1]
    for gi in range(len(vals) - 2, -1, -1):
        out = jnp.where(lane < POOL_GROUP * (gi + 1), vals[gi], out)
    return out


def _pooled(u):
    s2 = u + _shift_down(u, 1)
    s4 = s2 + _shift_down(s2, 2)
    s8 = s4 + _shift_down(s4, 4)
    s16 = s8 + _shift_down(s8, 8)
    win = _by_group([s2, s4, s8, s16], u.shape)
    row = lax.broadcasted_iota(jnp.int32, u.shape, 0)
    wsize = _by_group([jnp.full(u.shape, w, jnp.int32) for w in POOL_WINDOWS], u.shape)
    inv = 1.0 / jnp.minimum(row + 1, wsize).astype(F32)
    return win * inv - u, inv


def _pool_fwd(rest, wbd, scale, seq):
    t = rest.shape[0]

    def body(u_ref, w_ref, s_ref, o_ref):
        pooled, _ = _pooled(u_ref[...])
        pw = jnp.dot(pooled.astype(BF16), w_ref[...], preferred_element_type=F32)
        o_ref[...] = (pw * s_ref[...]).astype(o_ref.dtype)

    blk = pl.BlockSpec((seq, D_POOL), lambda b: (b, 0))
    return pl.pallas_call(
        body, grid=(t // seq,),
        in_specs=[blk, pl.BlockSpec((D_POOL, D_POOL), lambda b: (0, 0)), pl.BlockSpec((1, D_POOL), lambda b: (0, 0))],
        out_specs=blk, out_shape=jax.ShapeDtypeStruct((t, D_POOL), BF16), name="pool_fwd",
        compiler_params=_cparams(("parallel",), 24 * seq * D_POOL * 4),
    )(rest, wbd, scale)


def _pool_bwd(rest, dy, wbd, wbd_t, scale, seq):
    t = rest.shape[0]

    def body(u_ref, dy_ref, w_ref, wt_ref, s_ref, du_ref, dw_ref, dsc_ref):
        i = pl.program_id(0)
        pooled, inv = _pooled(u_ref[...])
        pb = pooled.astype(BF16)
        pw = jnp.dot(pb, w_ref[...], preferred_element_type=F32)
        dyp = dy_ref[...]
        dsp = jnp.sum(dyp * pw, axis=0, keepdims=True)
        dpw = (dyp * s_ref[...]).astype(BF16)
        dwp = _tn(pb, dpw)
        dpooled = jnp.dot(dpw, wt_ref[...], preferred_element_type=F32)
        dwin = dpooled * inv
        t2 = dwin + _shift_up(dwin, 1)
        t4 = t2 + _shift_up(t2, 2)
        t8 = t4 + _shift_up(t4, 4)
        t16 = t8 + _shift_up(t8, 8)
        du_ref[...] = (_by_group([t2, t4, t8, t16], dwin.shape) - dpooled).astype(du_ref.dtype)

        @pl.when(i == 0)
        def _():
            dw_ref[...] = dwp
            dsc_ref[...] = dsp

        @pl.when(i > 0)
        def _():
            dw_ref[...] += dwp
            dsc_ref[...] += dsp

    blk = pl.BlockSpec((seq, D_POOL), lambda b: (b, 0))
    sq = pl.BlockSpec((D_POOL, D_POOL), lambda b: (0, 0))
    row = pl.BlockSpec((1, D_POOL), lambda b: (0, 0))
    return pl.pallas_call(
        body, grid=(t // seq,),
        in_specs=[blk, pl.BlockSpec((seq, D_POOL), lambda b: (b, 2)), sq, sq, row],
        out_specs=[blk, sq, row],
        out_shape=[jax.ShapeDtypeStruct((t, D_POOL), BF16), jax.ShapeDtypeStruct((D_POOL, D_POOL), F32),
                   jax.ShapeDtypeStruct((1, D_POOL), F32)], name="pool_bwd",
        compiler_params=_cparams(("arbitrary",), 40 * seq * D_POOL * 4),
    )(rest, dy, wbd, wbd_t, scale)


def _conv_fwd(rest, cw, seq):
    t = rest.shape[0]

    def body(cb_ref, cc_ref, ch_ref, w_ref, o_ref):
        u = cc_ref[...] * ch_ref[...]
        y = w_ref[0:1, :] * _shift_down(u, 2) + w_ref[1:2, :] * _shift_down(u, 1) + w_ref[2:3, :] * u
        o_ref[...] = (cb_ref[...] * y).astype(o_ref.dtype)

    def col(c):
        return pl.BlockSpec((seq, D_CONV), lambda b, c=c: (b, c))

    return pl.pallas_call(
        body, grid=(t // seq,), in_specs=[col(1), col(2), col(3), pl.BlockSpec((8, D_CONV), lambda b: (0, 0))],
        out_specs=pl.BlockSpec((seq, D_CONV), lambda b: (b, 0)),
        out_shape=jax.ShapeDtypeStruct((t, D_CONV), BF16), name="conv_fwd",
        compiler_params=_cparams(("parallel",), 24 * seq * D_CONV * 4),
    )(rest, rest, rest, cw)


def _conv_bwd(rest, dy, cw, seq):
    t = rest.shape[0]

    def body(cb_ref, cc_ref, ch_ref, dy_ref, w_ref, o_ref, dw_ref):
        i = pl.program_id(0)
        cc = cc_ref[...]
        ch = ch_ref[...]
        u = cc * ch
        u1 = _shift_down(u, 1)
        u2 = _shift_down(u, 2)
        y = w_ref[0:1, :] * u2 + w_ref[1:2, :] * u1 + w_ref[2:3, :] * u
        dyc = dy_ref[...]
        d2 = dyc * cb_ref[...]
        du = w_ref[0:1, :] * _shift_up(d2, 2) + w_ref[1:2, :] * _shift_up(d2, 1) + w_ref[2:3, :] * d2
        o_ref[:, 0:D_CONV] = (dyc * y).astype(o_ref.dtype)
        o_ref[:, D_CONV:2 * D_CONV] = (du * ch).astype(o_ref.dtype)
        o_ref[:, 2 * D_CONV:3 * D_CONV] = (du * cc).astype(o_ref.dtype)
        tap = lax.broadcasted_iota(jnp.int32, (8, D_CONV), 0)
        dwp = jnp.where(tap == 0, jnp.sum(d2 * u2, axis=0, keepdims=True),
                        jnp.where(tap == 1, jnp.sum(d2 * u1, axis=0, keepdims=True),
                                  jnp.where(tap == 2, jnp.sum(d2 * u, axis=0, keepdims=True), 0.0)))

        @pl.when(i == 0)
        def _():
            dw_ref[...] = dwp

        @pl.when(i > 0)
        def _():
            dw_ref[...] += dwp

    def col(c):
        return pl.BlockSpec((seq, D_CONV), lambda b, c=c: (b, c))

    taps = pl.BlockSpec((8, D_CONV), lambda b: (0, 0))
    return pl.pallas_call(
        body, grid=(t // seq,), in_specs=[col(1), col(2), col(3), col(3), taps],
        out_specs=[pl.BlockSpec((seq, 3 * D_CONV), lambda b: (b, 0)), taps],
        out_shape=[jax.ShapeDtypeStruct((t, 3 * D_CONV), BF16), jax.ShapeDtypeStruct((8, D_CONV), F32)],
        name="conv_bwd", compiler_params=_cparams(("arbitrary",), 48 * seq * D_CONV * 4),
    )(rest, rest, rest, dy, cw)


def _adamw(w, g, m, v, name):
    r, c = w.shape
    tr = _pick(r, (512, 352, 256, 128)) if r > 512 else r

    def body(w_ref, g_ref, m_ref, v_ref, d_ref, mo_ref, vo_ref):
        gv = g_ref[...]
        mn = ADAM_B1 * m_ref[...] + (1.0 - ADAM_B1) * gv
        vn = ADAM_B2 * v_ref[...] + (1.0 - ADAM_B2) * (gv * gv)
        m_hat = mn / (1.0 - ADAM_B1 ** ADAM_STEP)
        v_hat = vn / (1.0 - ADAM_B2 ** ADAM_STEP)
        d_ref[...] = -ADAM_LR * (m_hat / (jnp.sqrt(v_hat) + ADAM_EPS) + ADAM_WD * w_ref[...])
        mo_ref[...] = mn
        vo_ref[...] = vn

    blk = pl.BlockSpec((tr, c), lambda i: (i, 0))
    sds = jax.ShapeDtypeStruct((r, c), F32)
    return pl.pallas_call(
        body, grid=(r // tr,), in_specs=[blk] * 4, out_specs=[blk] * 3, out_shape=[sds] * 3, name=name,
        compiler_params=_cparams(("parallel",), 20 * tr * max(c, LANES) * 4),
    )(w, g, m, v)


def _sum_slots(a, name):
    ns, r, c = a.shape
    tr = _pick(r, (368, 256, 184, 136, 128, 88, 8))

    def body(a_ref, o_ref):
        acc = a_ref[0].astype(F32)
        for s in range(1, ns):
            acc = acc + a_ref[s].astype(F32)
        o_ref[...] = acc

    return pl.pallas_call(
        body, grid=(r // tr,), in_specs=[pl.BlockSpec((ns, tr, c), lambda i: (0, i, 0))],
        out_specs=pl.BlockSpec((tr, c), lambda i: (i, 0)), out_shape=jax.ShapeDtypeStruct((r, c), F32), name=name,
        compiler_params=_cparams(("parallel",), 4 * (ns + 2) * tr * c * 4),
    )(a)


def _add_core_half(core, g4, theirs, out_dtype, name):
    ns, _, r, c = g4.shape
    tr = _pick(r, (368, 256, 184, 136, 128, 88, 8))

    def body(core_ref, a_ref, b_ref, o_ref):
        o_ref[...] = (a_ref[0] + b_ref[...]).astype(o_ref.dtype)

    blk = pl.BlockSpec((1, tr, c), lambda s, i, core_ref: (s, i, 0))
    return pl.pallas_call(
        body,
        grid_spec=pltpu.PrefetchScalarGridSpec(
            num_scalar_prefetch=1, grid=(ns, r // tr),
            in_specs=[pl.BlockSpec((1, 1, tr, c), lambda s, i, core_ref: (s, core_ref[0], i, 0)), blk],
            out_specs=blk),
        out_shape=jax.ShapeDtypeStruct((ns, r, c), out_dtype), name=name,
        compiler_params=_cparams(("parallel", "parallel"), 10 * tr * c * 4),
    )(core, g4, theirs)


def _sum_chips(order, own, landed, name):
    ns, r, c = own.shape
    tr = _pick(r, (368, 256, 184, 136, 128, 88, 8))

    def body(order_ref, a_ref, b1_ref, b2_ref, b3_ref, o_ref):
        o_ref[...] = ((a_ref[0].astype(F32) + b1_ref[0].astype(F32)) + b2_ref[0].astype(F32)) + b3_ref[0].astype(F32)

    def slot(k):
        return pl.BlockSpec((1, tr, c), lambda i, order_ref, k=k: (order_ref[k], i, 0))

    return pl.pallas_call(
        body,
        grid_spec=pltpu.PrefetchScalarGridSpec(
            num_scalar_prefetch=1, grid=(r // tr,), in_specs=[slot(0), slot(1), slot(2), slot(3)],
            out_specs=pl.BlockSpec((tr, c), lambda i, order_ref: (i, 0))),
        out_shape=jax.ShapeDtypeStruct((r, c), F32), name=name,
        compiler_params=_cparams(("parallel",), 16 * tr * c * 4),
    )(order, own, landed, landed, landed)


def _mesh_pos():
    return lax.axis_index("x"), lax.axis_index("y"), lax.axis_index("c")


def _all_gather(x, name):
    r, c = x.shape

    def body(x_ref, out_ref, send_sems, recv_sems):
        _gather_start(x_ref, out_ref, send_sems, recv_sems)
        _gather_finish(x_ref, out_ref, send_sems, recv_sems)

    gathered = pl.pallas_call(
        body, out_shape=jax.ShapeDtypeStruct((N_DEV, r, c), x.dtype),
        in_specs=[pl.BlockSpec(memory_space=pl.ANY)], out_specs=pl.BlockSpec(memory_space=pl.ANY),
        scratch_shapes=list(GATHER_SEMS), name=name,
    )(x)
    return _fill_own_slot(gathered, x)


GATHER_SEMS = (pltpu.SemaphoreType.DMA((7,)), pltpu.SemaphoreType.DMA((7,)))


def _fill_own_slot(gathered, x):
    mx, my, mc = _mesh_pos()
    return lax.dynamic_update_slice_in_dim(gathered, x[None], 4 * mx + 2 * my + mc, axis=0)


def _gather_copies(x_ref, out_ref, send_sems, recv_sems):
    mx, my, mc = _mesh_pos()
    me, sibling = (mx, my, mc), (mx, my, 1 - mc)
    chips = [(1 - mx, my), (mx, 1 - my), (1 - mx, 1 - my)]

    def slot(px, py, pc):
        return out_ref.at[4 * px + 2 * py + pc]

    def copy(k, block, to, src=None):
        return pltpu.make_async_remote_copy(
            src_ref=slot(*block) if src is None else src, dst_ref=slot(*block),
            send_sem=send_sems.at[k], recv_sem=recv_sems.at[k],
            device_id=to, device_id_type=pl.DeviceIdType.MESH)

    first = [copy(0, me, sibling, src=x_ref)]
    first += [copy(1 + j, me, (*chip, mc), src=x_ref) for j, chip in enumerate(chips)]
    passed = [copy(4 + j, (*chip, mc), sibling) for j, chip in enumerate(chips)]
    over_ici = [copy(1 + j, (*chip, mc), me) for j, chip in enumerate(chips)]
    over_d2d = [copy(0, sibling, me)] + [copy(4 + j, (*chip, 1 - mc), me) for j, chip in enumerate(chips)]
    return first, passed, over_ici, over_d2d


def _gather_start(x_ref, out_ref, send_sems, recv_sems):
    for cp in _gather_copies(x_ref, out_ref, send_sems, recv_sems)[0]:
        cp.start()


def _gather_finish(x_ref, out_ref, send_sems, recv_sems):
    first, passed, over_ici, over_d2d = _gather_copies(x_ref, out_ref, send_sems, recv_sems)
    for landed, relay in zip(over_ici, passed):
        landed.wait_recv()
        relay.start()
    for landed in over_d2d:
        landed.wait_recv()
    for cp in first + passed:
        cp.wait_send()


def _exchange_sibling(g4, name):
    nchip, _, r, c = g4.shape

    def body(g_ref, theirs_ref, send_sems, recv_sems):
        _sibling_start(g_ref, theirs_ref, send_sems, recv_sems)
        _sibling_finish(g_ref, theirs_ref, send_sems, recv_sems)

    any_spec = pl.BlockSpec(memory_space=pl.ANY)
    return pl.pallas_call(
        body, out_shape=jax.ShapeDtypeStruct((nchip, r, c), g4.dtype), in_specs=[any_spec], out_specs=any_spec,
        scratch_shapes=list(SIBLING_SEMS), name=name,
    )(g4)


SIBLING_SEMS = (pltpu.SemaphoreType.DMA((N_CHIPS,)), pltpu.SemaphoreType.DMA((N_CHIPS,)))


def _sibling_copies(g_ref, theirs_ref, send_sems, recv_sems):
    mx, my, mc = _mesh_pos()
    return [pltpu.make_async_remote_copy(
        src_ref=g_ref.at[chip, 1 - mc], dst_ref=theirs_ref.at[chip],
        send_sem=send_sems.at[chip], recv_sem=recv_sems.at[chip],
        device_id=(mx, my, 1 - mc), device_id_type=pl.DeviceIdType.MESH) for chip in range(N_CHIPS)]


def _sibling_start(g_ref, theirs_ref, send_sems, recv_sems):
    for cp in _sibling_copies(g_ref, theirs_ref, send_sems, recv_sems):
        cp.start()


def _sibling_finish(g_ref, theirs_ref, send_sems, recv_sems):
    copies = _sibling_copies(g_ref, theirs_ref, send_sems, recv_sems)
    for cp in copies:
        cp.wait_recv()
    for cp in copies:
        cp.wait_send()


def _exchange_chips(ts, name):
    def body(t_ref, out_ref, send_sems, recv_sems):
        _chips_start(t_ref, out_ref, send_sems, recv_sems)
        _chips_finish(t_ref, out_ref, send_sems, recv_sems)

    any_spec = pl.BlockSpec(memory_space=pl.ANY)
    return pl.pallas_call(
        body, out_shape=jax.ShapeDtypeStruct(ts.shape, ts.dtype), in_specs=[any_spec], out_specs=any_spec,
        scratch_shapes=list(CHIPS_SEMS), name=name,
    )(ts)


CHIPS_SEMS = (pltpu.SemaphoreType.DMA((N_CHIPS - 1,)), pltpu.SemaphoreType.DMA((N_CHIPS - 1,)))


def _chips_copies(t_ref, out_ref, send_sems, recv_sems):
    mx, my, mc = _mesh_pos()
    my_chip = 2 * mx + my
    copies = []
    for k in range(1, N_CHIPS):
        px = 1 - mx if k & 2 else mx
        py = 1 - my if k & 1 else my
        peer_chip = 2 * px + py

        def rdma(dst_slot, px=px, py=py, peer_chip=peer_chip, k=k):
            return pltpu.make_async_remote_copy(
                src_ref=t_ref.at[peer_chip], dst_ref=out_ref.at[dst_slot],
                send_sem=send_sems.at[k - 1], recv_sem=recv_sems.at[k - 1],
                device_id=(px, py, mc), device_id_type=pl.DeviceIdType.MESH)

        copies.append((rdma(my_chip), rdma(peer_chip)))
    return copies


def _chips_start(t_ref, out_ref, send_sems, recv_sems):
    for send, _ in _chips_copies(t_ref, out_ref, send_sems, recv_sems):
        send.start()


def _chips_finish(t_ref, out_ref, send_sems, recv_sems):
    copies = _chips_copies(t_ref, out_ref, send_sems, recv_sems)
    for _, landed in copies:
        landed.wait_recv()
    for send, _ in copies:
        send.wait_send()


def _add_sibling(g4, theirs):
    core = jnp.reshape(lax.axis_index("c"), (1,)).astype(jnp.int32)
    return _add_core_half(core, g4, theirs, BF16, name="add_sibling_grads")


def _sum_landed(chip_sums, landed):
    mx, my, _ = _mesh_pos()
    order = jnp.stack([2 * mx + my, 2 * (1 - mx) + my, 2 * mx + (1 - my), 2 * (1 - mx) + (1 - my)]).astype(jnp.int32)
    return _sum_chips(order, chip_sums, landed, name="sum_grads")


def _perm_mix_cols(wm):
    f0 = D_QKV
    f1 = f0 + N_HEADS
    pad = jnp.zeros((wm.shape[0], LANES - N_HEADS), wm.dtype)
    return jnp.concatenate([wm[:, :f0], wm[:, f1:], wm[:, f0:f1], pad], axis=1)


def _unperm_mix_rows(gt):
    f0 = D_QKV
    return jnp.concatenate([gt[:f0], gt[f0 + D_REST:f0 + D_REST + N_HEADS], gt[f0:f0 + D_REST]], axis=0)


def _pack_shards(parts, l, dtype):
    w1i, w1o, wmi, wmo, w2i, w2o = parts
    rows = [w1i[l].T, w1o[l], jnp.pad(wmi[l].T, ((0, MIX_ROWS_PAD - MIX_ROWS), (0, 0))), wmo[l], w2i[l].T, w2o[l]]
    return jnp.concatenate(rows, axis=0).astype(dtype)


def _layer_weights(wg):
    offs = {}
    o = 0
    for nm, n in (("f1i", FFN_ROWS), ("f1o", OUT_ROWS), ("mi", MIX_ROWS_PAD), ("mo", MO_ROWS), ("f2i", FFN_ROWS),
                  ("f2o", OUT_ROWS)):
        offs[nm] = (o, n)
        o += n

    def piece(nm, n_used=None):
        o, n = offs[nm]
        return wg[:, o:o + (n if n_used is None else n_used)]

    out = {}
    for tag in ("f1", "f2"):
        wi_t = piece(tag + "i").reshape(2 * D_FF, D_MODEL)
        wo = piece(tag + "o").reshape(D_FF, D_MODEL)
        out[tag] = dict(wi=wi_t.T, wi_t=wi_t, wo=wo, wo_t=wo.T)
    wm = _perm_mix_cols(piece("mi", MIX_ROWS).reshape(D_IN, D_MODEL).T)
    wm_t = wm.T
    wo = piece("mo").reshape(D_MODEL, D_MODEL)
    out["mix"] = dict(w_qkv=wm[:, :D_QKV], w_rest=wm[:, D_QKV:D_QKV + D_REST], w_f=wm[:, D_QKV + D_REST:],
                      wm_t=wm_t, wo=wo, wo_t=wo.T)
    return out


def _layer_grad_rows(gr):
    def halves(gate_t, up_t):
        return jnp.concatenate([gate_t.reshape(N_DEV // 2, FFN_ROWS, D_MODEL),
                                up_t.reshape(N_DEV // 2, FFN_ROWS, D_MODEL)], axis=0)

    gmi = _unperm_mix_rows(gr["mix_in_t"]).reshape(N_DEV, MIX_ROWS, D_MODEL)
    gmi = jnp.pad(gmi, ((0, 0), (0, MIX_ROWS_PAD - MIX_ROWS), (0, 0)))
    return jnp.concatenate(
        [halves(*gr["f1_in_t"]), gr["f1_out"].reshape(N_DEV, OUT_ROWS, D_MODEL), gmi,
         gr["mix_out"].reshape(N_DEV, MO_ROWS, D_MODEL), halves(*gr["f2_in_t"]),
         gr["f2_out"].reshape(N_DEV, OUT_ROWS, D_MODEL)], axis=1)


def _ffn_forward(x, gain, w):
    xn = _rmsnorm_fwd(x, gain, name="ffn_norm")
    h, pg, pu = _ffn_in(xn, w["wi"], name="ffn_in")
    x_new = _mm_nn(h, w["wo"], out_dtype=F32, res=x, alpha=0.5, name="ffn_out")
    return x_new, dict(x=x, xn=xn, h=h, pg=pg, pu=pu)


def _ffn_backward(dxo, dxo_b, gain, w, saved, carry_sibling=None):
    res = _ffn_bwd_mid(dxo_b, w["wo_t"], saved["pg"], saved["pu"], name="ffn_bwd_mid", carry_sibling=carry_sibling)
    dzg, dzu = res[:2]
    theirs = None if carry_sibling is None else res[2]
    g_out = _mm_tn(saved["h"], dxo_b, alpha=0.5, tm=F_HALF, name="ffn_gw_out")
    g_in_t = (_mm_tn(dzg, saved["xn"], tm=F_HALF, name="ffn_gw_in"), _mm_tn(dzu, saved["xn"], tm=F_HALF, name="ffn_gw_in"))
    dxn = _mm_nn2(dzg, dzu, w["wi_t"], name="ffn_dxn")
    dx, dx_b, dg = _rmsnorm_bwd(saved["x"], gain, dxn, dxo, name="ffn_norm_bwd")
    return dx, dx_b, dg, g_in_t, g_out, theirs


def _mixer_forward(x, p, w, nb, seq, ta, next_pack=None):
    xn = _rmsnorm_fwd(x, p["norm"], name="mix_norm")
    qkv = _mm_nn(xn, w["w_qkv"], out_dtype=BF16, name="mix_qkv")
    rest = _mm_nn(xn, w["w_rest"], out_dtype=F32, name="mix_rest")
    fl = _mm_nn(xn, w["w_f"], out_dtype=F32, name="mix_f")
    qa, ka, vm = _fox_prep(fl, p["bf"], qkv, nb, seq)
    if next_pack is None:
        (y_attn, lse), next_gathered = _fox_fwd(qa, ka, vm, nb, seq, ta), None
    else:
        y_attn, lse, next_gathered = _fox_fwd(qa, ka, vm, nb, seq, ta, carry_gather=next_pack)
        next_gathered = _fill_own_slot(next_gathered, next_pack)
    y_pool = _pool_fwd(rest, p["wbd"], p["scale"], seq)
    y_conv = _conv_fwd(rest, p["cw"], seq)
    y = jnp.concatenate([y_attn, y_pool, y_conv], axis=1)
    x_new = _mm_nn(y, w["wo"], out_dtype=F32, res=x, alpha=1.0, name="mix_out")
    return x_new, dict(x=x, xn=xn, qa=qa, ka=ka, vm=vm, rest=rest, fl=fl, lse=lse, y=y), next_gathered


def _mixer_backward(dxo, dxo_b, p, w, sv, nb, seq, ta, pending=None):
    t = dxo.shape[0]
    dy = _mm_nn(dxo_b, w["wo_t"], out_dtype=F32, name="mix_dy")
    g_out = _mm_tn(sv["y"], dxo_b, name="mix_gw_out")
    res = _fox_bwd(sv["qa"], sv["ka"], sv["vm"], sv["y"], dy, sv["lse"], nb, seq, ta, carry_exchange=pending)
    dq, dk, dv, d_rows, d_cols = res[:5]
    landed = None if pending is None else res[5]
    ddh = (d_rows.reshape(nb, N_HEADS, seq) - d_cols.reshape(nb, N_HEADS, seq)).transpose(0, 2, 1)
    ddh = ddh.reshape(t, N_HEADS)
    dfl, dbf = _fox_prep_bwd(jnp.pad(ddh, ((0, 0), (0, LANES - N_HEADS))), sv["fl"], p["bf"], seq)
    dpool, dwbd, dscale = _pool_bwd(sv["rest"], dy, p["wbd"], p["wbd_t"], p["scale"], seq)
    dconv, dcw = _conv_bwd(sv["rest"], dy, p["cw"], seq)
    dproj = jnp.concatenate([dq, dk.astype(BF16), dv.astype(BF16), dpool, dconv, dfl], axis=1)
    g_in_t = _mm_tn(dproj, sv["xn"], tm=D_INP // 3, name="mix_gw_in")
    dxn = _mm_nn(dproj, w["wm_t"], out_dtype=F32, name="mix_dxn")
    dx, dx_b, dg = _rmsnorm_bwd(sv["x"], p["norm"], dxn, dxo, name="mix_norm_bwd")
    return dx, dx_b, dict(norm=dg, bf=dbf, wbd=dwbd, scale=dscale, cw=dcw, mix_in_t=g_in_t, mix_out=g_out), landed


def _block_diag(wp):
    z = jnp.zeros((POOL_GROUP, POOL_GROUP), wp.dtype)
    return jnp.concatenate(
        [jnp.concatenate([wp[g] if g == r else z for g in range(4)], axis=1) for r in range(4)], axis=0)


def _row_pad(a, rows):
    a = a.reshape(-1, a.shape[-1])
    return jnp.pad(a, ((0, rows - a.shape[0]), (0, 0)))


def kernel(x, norm_ffn1, w_ffn1_in, w_ffn1_out, norm_mix, w_mix_in, b_forget, w_pool, pool_scale, conv_w, w_mix_out, norm_ffn2, w_ffn2_in, w_ffn2_out, norm_final, loss_target, m_norm_ffn1, m_w_ffn1_in, m_w_ffn1_out, m_norm_mix, m_w_mix_in, m_b_forget, m_w_pool, m_pool_scale, m_conv_w, m_w_mix_out, m_norm_ffn2, m_w_ffn2_in, m_w_ffn2_out, m_norm_final, v_norm_ffn1, v_w_ffn1_in, v_w_ffn1_out, v_norm_mix, v_w_mix_in, v_b_forget, v_w_pool, v_pool_scale, v_conv_w, v_w_mix_out, v_norm_ffn2, v_w_ffn2_in, v_w_ffn2_out, v_norm_final):
    nb, seq, d = x.shape
    depth = norm_ffn1.shape[0]
    t = nb * seq
    ta = _pick(seq, (ATT_TILE, 128))
    my_id = 4 * lax.axis_index("x") + 2 * lax.axis_index("y") + lax.axis_index("c")
    cshard = conv_w.shape[-1]

    shards = (w_ffn1_in, w_ffn1_out, w_mix_in, w_mix_out, w_ffn2_in, w_ffn2_out)
    wg = _all_gather(_pack_shards(shards, 0, BF16), name="gather_weights")
    cw_g = _all_gather(_row_pad(conv_w.reshape(depth * 3, cshard), 16).reshape(4, LANES), name="gather_conv_taps")
    cw_all = cw_g.reshape(N_DEV, 16, cshard)[:, :depth * 3].reshape(N_DEV, depth, 3, cshard)
    cw_all = cw_all.transpose(1, 2, 0, 3).reshape(depth, 3, D_CONV)

    xs = x.reshape(t, d)
    saved = []
    for l in range(depth):
        w = _layer_weights(wg)
        wbd = _block_diag(w_pool[l])
        p = dict(norm=norm_mix[l][None], bf=jnp.pad(b_forget[l], (0, LANES - N_HEADS))[None],
                 wbd=wbd.astype(BF16), wbd_t=wbd.T.astype(BF16), scale=pool_scale[l][None],
                 cw=_row_pad(cw_all[l], 8))
        xs, s1 = _ffn_forward(xs, norm_ffn1[l][None], w["f1"])
        next_pack = _pack_shards(shards, l + 1, BF16) if l + 1 < depth else None
        xs, sm, wg = _mixer_forward(xs, p, w["mix"], nb, seq, ta, next_pack)
        xs, s2 = _ffn_forward(xs, norm_ffn2[l][None], w["f2"])
        saved.append((w, p, s1, sm, s2))

    dx, dx_b, g_norm_final, loss_part = _final_loss_bwd(xs, norm_final[None], loss_target.reshape(t, d))
    layer_g = [None] * depth
    small = [None] * depth
    rows_g4 = None
    for l in reversed(range(depth)):
        w, p, s1, sm, s2 = saved[l]
        dx, dx_b, dg2, g2_in_t, g2_out, theirs = _ffn_backward(dx, dx_b, norm_ffn2[l][None], w["f2"], s2, rows_g4)
        chip_sums = None if rows_g4 is None else _add_sibling(rows_g4, theirs)
        dx, dx_b, gm, landed = _mixer_backward(dx, dx_b, p, w["mix"], sm, nb, seq, ta, chip_sums)
        if chip_sums is not None:
            layer_g[l + 1] = _sum_landed(chip_sums, landed)
        dx, dx_b, dg1, g1_in_t, g1_out, _ = _ffn_backward(dx, dx_b, norm_ffn1[l][None], w["f1"], s1)
        rows_g4 = _layer_grad_rows(dict(f1_in_t=g1_in_t, f1_out=g1_out, mix_in_t=gm["mix_in_t"],
                                        mix_out=gm["mix_out"], f2_in_t=g2_in_t, f2_out=g2_out)
                                   ).reshape(N_CHIPS, 2, LAYER_ROWS, D_MODEL)
        small[l] = dict(n1=dg1, nm=gm["norm"], n2=dg2, bf=gm["bf"], wbd=gm["wbd"], scale=gm["scale"], cw=gm["cw"])
    grad_x = dx.reshape(nb, seq, d)
    chip_sums = _add_sibling(rows_g4, _exchange_sibling(rows_g4, name="exchange_grads_sibling"))
    layer_g[0] = _sum_landed(chip_sums, _exchange_chips(chip_sums, name="exchange_grads_chips"))

    g_rows = jnp.stack(layer_g)
    o = 0
    pieces = {}
    for nm, n in (("f1i", FFN_ROWS), ("f1o", OUT_ROWS), ("mi", MIX_ROWS_PAD), ("mo", MO_ROWS), ("f2i", FFN_ROWS),
                  ("f2o", OUT_ROWS)):
        pieces[nm] = g_rows[:, o:o + n]
        o += n
    g_sharded = dict(
        w_ffn1_in=pieces["f1i"].transpose(0, 2, 1), w_ffn1_out=pieces["f1o"],
        w_mix_in=pieces["mi"][:, :MIX_ROWS].transpose(0, 2, 1), w_mix_out=pieces["mo"],
        w_ffn2_in=pieces["f2i"].transpose(0, 2, 1), w_ffn2_out=pieces["f2o"])

    def tile8(a):
        return jnp.pad(a, ((0, 8 - a.shape[0]), (0, D_MODEL - a.shape[1])))

    rows = []
    for l in range(depth):
        s = small[l]
        wp_rows = jnp.stack([s["wbd"][POOL_GROUP * g:POOL_GROUP * (g + 1), POOL_GROUP * g:POOL_GROUP * (g + 1)]
                             for g in range(4)]).reshape(16, D_MODEL)
        rows += [tile8(s["n1"]), tile8(s["nm"]), tile8(s["n2"]), tile8(s["bf"]), tile8(s["scale"]), tile8(s["cw"]),
                 wp_rows]
    rows += [tile8(g_norm_final), tile8(loss_part)]
    per_layer = 6 * 8 + 16
    small_sum = _sum_slots(_all_gather(jnp.concatenate(rows, axis=0), name="gather_small_grads"),
                           name="sum_small_grads")
    lay = small_sum[:depth * per_layer].reshape(depth, per_layer, D_MODEL)
    g_small = dict(
        norm_ffn1=lay[:, 0], norm_mix=lay[:, 8], norm_ffn2=lay[:, 16], b_forget=lay[:, 24, :N_HEADS],
        pool_scale=lay[:, 32, :D_POOL],
        conv_w=lax.dynamic_slice_in_dim(lay[:, 40:43, :D_CONV], my_id * cshard, cshard, axis=2),
        w_pool=lay[:, 48:64].reshape(depth, 4, POOL_GROUP, POOL_GROUP),
        norm_final=small_sum[depth * per_layer])
    loss = small_sum[depth * per_layer + 8, 0]

    given = dict(norm_ffn1=(norm_ffn1, m_norm_ffn1, v_norm_ffn1), w_ffn1_in=(w_ffn1_in, m_w_ffn1_in, v_w_ffn1_in),
                 w_ffn1_out=(w_ffn1_out, m_w_ffn1_out, v_w_ffn1_out), norm_mix=(norm_mix, m_norm_mix, v_norm_mix),
                 w_mix_in=(w_mix_in, m_w_mix_in, v_w_mix_in), b_forget=(b_forget, m_b_forget, v_b_forget),
                 w_pool=(w_pool, m_w_pool, v_w_pool), pool_scale=(pool_scale, m_pool_scale, v_pool_scale),
                 conv_w=(conv_w, m_conv_w, v_conv_w), w_mix_out=(w_mix_out, m_w_mix_out, v_w_mix_out),
                 norm_ffn2=(norm_ffn2, m_norm_ffn2, v_norm_ffn2), w_ffn2_in=(w_ffn2_in, m_w_ffn2_in, v_w_ffn2_in),
                 w_ffn2_out=(w_ffn2_out, m_w_ffn2_out, v_w_ffn2_out), norm_final=(norm_final, m_norm_final, v_norm_final))
    names = list(given)
    grads, deltas, new_m, new_v = {}, {}, {}, {}
    for nm in names:
        wv, mv, vv = given[nm]
        gv = (g_sharded[nm] if nm in g_sharded else g_small[nm]).reshape(wv.shape)
        shape2 = (-1, wv.shape[-1]) if wv.ndim > 1 else (1, wv.shape[0])
        dl, mn, vn = _adamw(wv.reshape(shape2), gv.reshape(shape2), mv.reshape(shape2), vv.reshape(shape2),
                            name="adamw_" + nm)
        grads[nm], deltas[nm], new_m[nm], new_v[nm] = gv, dl.reshape(wv.shape), mn.reshape(wv.shape), vn.reshape(wv.shape)
    return (loss, grad_x, *[grads[n] for n in names], *[deltas[n] for n in names],
            *[new_m[n] for n in names], *[new_v[n] for n in names])
```

```python
import functools

import jax
import jax.numpy as jnp
from jax import lax
from jax.experimental import pallas as pl
from jax.experimental.pallas import tpu as pltpu

F32 = jnp.float32
BF16 = jnp.bfloat16

D_MODEL = 1024
D_FF = 2816
HEAD_DIM = 64
N_HEADS = 8
N_PAIRS = N_HEADS // 2
D_ATTN = 512
D_POOL = 256
D_CONV = 256
POOL_WINDOWS = (2, 4, 8, 16)
POOL_GROUP = 64
D_IN = 2568
RMS_EPS = 1e-6
ADAM_LR, ADAM_B1, ADAM_B2, ADAM_EPS, ADAM_WD, ADAM_STEP = 0.001, 0.9, 0.999, 1e-08, 0.01, 10

N_DEV = 8
N_CHIPS = 4
LANES = 128
VMEM_BYTES_V7X = 64 * 1024 * 1024
VMEM_LIMIT_MAX = VMEM_BYTES_V7X - 8 * 1024 * 1024

F_HALF = D_FF // 2
D_QKV = 3 * D_ATTN
D_REST = D_POOL + 3 * D_CONV
D_INP = D_QKV + D_REST + LANES
MIX_ROWS = 321
MIX_ROWS_PAD = 336
FFN_ROWS = 704
OUT_ROWS = 352
MO_ROWS = 128
LAYER_ROWS = 2 * (FFN_ROWS + OUT_ROWS) + MIX_ROWS_PAD + MO_ROWS
NEG_BIG = -1e30
ATT_SCALE = HEAD_DIM ** -0.5
ATT_K = 2 * LANES
ATT_TILE = 256
ATT_PAIRS_FWD = 4
ATT_PAIRS_BWD = 4
MXU_COLS = 256


def _cparams(sem, vmem_bytes):
    limit = int(min(max(vmem_bytes, 16 * 1024 * 1024), VMEM_LIMIT_MAX))
    return pltpu.CompilerParams(dimension_semantics=sem, vmem_limit_bytes=limit)


def _nbytes(shape, dtype):
    n = 1
    for s in shape:
        n *= s
    return n * jnp.dtype(dtype).itemsize


def _pick(n, prefs):
    for p in prefs:
        if n % p == 0:
            return p
    return n


def _rmsnorm_fwd(x, g, name):
    t, d = x.shape
    tm = _pick(t, (512, 256, 128))

    def body(x_ref, g_ref, o_ref):
        xv = x_ref[...]
        r = lax.rsqrt(jnp.mean(xv * xv, axis=-1, keepdims=True) + RMS_EPS)
        o_ref[...] = ((xv * r) * g_ref[...]).astype(o_ref.dtype)

    return pl.pallas_call(
        body, grid=(t // tm,),
        in_specs=[pl.BlockSpec((tm, d), lambda i: (i, 0)), pl.BlockSpec((1, d), lambda i: (0, 0))],
        out_specs=pl.BlockSpec((tm, d), lambda i: (i, 0)),
        out_shape=jax.ShapeDtypeStruct((t, d), BF16), name=name,
        compiler_params=_cparams(("parallel",), 6 * tm * d * 4),
    )(x, g)


def _mm_nn(a, b, *, out_dtype, name, res=None, alpha=1.0, tn=None, next_gain=None):
    m, k = a.shape
    n = b.shape[1]
    tn = n if tn is None else tn
    tm = _pick(m, (512, 256, 128))
    with_res = res is not None
    with_norm = next_gain is not None
    assert not with_norm or tn == n

    def body(*refs):
        refs = list(refs)
        a_ref, b_ref = refs[:2]
        r_ref = refs[2] if with_res else None
        g_ref = refs[2 + with_res] if with_norm else None
        o_ref = refs[2 + with_res + with_norm]
        acc = jnp.dot(a_ref[...], b_ref[...], preferred_element_type=F32)
        if with_res:
            acc = r_ref[...] + alpha * acc
        o_ref[...] = acc.astype(o_ref.dtype)
        if with_norm:
            r = lax.rsqrt(jnp.mean(acc * acc, axis=-1, keepdims=True) + RMS_EPS)
            refs[-1][...] = ((acc * r) * g_ref[...]).astype(BF16)

    in_specs = [pl.BlockSpec((tm, k), lambda j, i: (i, 0)), pl.BlockSpec((k, tn), lambda j, i: (0, j))]
    args = [a, b]
    out_blk = pl.BlockSpec((tm, tn), lambda j, i: (i, j))
    out_specs, out_shape = [out_blk], [jax.ShapeDtypeStruct((m, n), out_dtype)]
    if with_res:
        in_specs.append(out_blk)
        args.append(res)
    if with_norm:
        in_specs.append(pl.BlockSpec((1, n), lambda j, i: (0, 0)))
        args.append(next_gain)
        out_specs.append(out_blk)
        out_shape.append(jax.ShapeDtypeStruct((m, n), BF16))
    vmem = 2 * (_nbytes((tm, k), BF16) + _nbytes((k, tn), BF16) + 4 * _nbytes((tm, tn), F32))
    outs = pl.pallas_call(
        body, grid=(n // tn, m // tm), in_specs=in_specs, out_specs=out_specs, out_shape=out_shape, name=name,
        compiler_params=_cparams(("parallel", "parallel"), vmem),
    )(*args)
    return outs if with_norm else outs[0]


def _mm_tn(a, b, *, name, alpha=1.0, tm=None):
    t, m = a.shape
    n = b.shape[1]
    tm = m if tm is None else tm
    tk = _pick(t, (1024, 512, 256, 128))
    nk = t // tk

    def body(a_ref, b_ref, o_ref):
        kk = pl.program_id(1)
        p = lax.dot_general(a_ref[...], b_ref[...], (((0,), (0,)), ((), ())), preferred_element_type=F32)
        if alpha != 1.0:
            p = alpha * p

        @pl.when(kk == 0)
        def _():
            o_ref[...] = p

        @pl.when(kk > 0)
        def _():
            o_ref[...] += p

    vmem = 2 * (_nbytes((tk, tm), BF16) + _nbytes((tk, n), BF16) + 2 * _nbytes((tm, n), F32))
    return pl.pallas_call(
        body, grid=(m // tm, nk),
        in_specs=[pl.BlockSpec((tk, tm), lambda i, kk: (kk, i)), pl.BlockSpec((tk, n), lambda i, kk: (kk, 0))],
        out_specs=pl.BlockSpec((tm, n), lambda i, kk: (i, 0)),
        out_shape=jax.ShapeDtypeStruct((m, n), F32), name=name,
        compiler_params=_cparams(("parallel", "arbitrary"), vmem),
    )(a, b)


def _sigmoid(v):
    return 1.0 / (1.0 + jnp.exp(-v))


def _col_chunks(n, width):
    return [(c, min(width, n - c)) for c in range(0, n, width)]


def _ffn_in(xn, w_t, name):
    t, d = xn.shape
    tm = _pick(t, (1024, 512, 256, 128))

    def body(x_ref, wg_ref, wu_ref, h_ref, pg_ref, pu_ref):
        xv = x_ref[...]
        for c0, cw in _col_chunks(F_HALF, MXU_COLS):
            cols = slice(c0, c0 + cw)
            g = _nt(xv, wg_ref[cols, :])
            u = _nt(xv, wu_ref[cols, :])
            s = _sigmoid(g)
            silu = g * s
            h_ref[:, cols] = (silu * u).astype(h_ref.dtype)
            pg_ref[:, cols] = (u * (s * (1.0 + g * (1.0 - s)))).astype(pg_ref.dtype)
            pu_ref[:, cols] = silu.astype(pu_ref.dtype)

    vmem = 2 * (_nbytes((tm, d), BF16) + 2 * _nbytes((d, F_HALF), BF16) + 4 * _nbytes((tm, D_FF), F32))
    out_blk = pl.BlockSpec((tm, F_HALF), lambda j, i: (i, j))
    sds = jax.ShapeDtypeStruct((t, D_FF), BF16)
    return pl.pallas_call(
        body, grid=(2, t // tm),
        in_specs=[pl.BlockSpec((tm, d), lambda j, i: (i, 0)), pl.BlockSpec((F_HALF, d), lambda j, i: (j, 0)),
                  pl.BlockSpec((F_HALF, d), lambda j, i: (2 + j, 0))],
        out_specs=[out_blk, out_blk, out_blk], out_shape=[sds, sds, sds], name=name,
        compiler_params=_cparams(("parallel", "parallel"), vmem),
    )(xn, w_t, w_t)


def _ffn_bwd_mid(dxo, w_out, pg, pu, name, carry_sibling=None):
    t, d = dxo.shape
    tm = _pick(t, (1024, 512, 256, 128))
    grid = (2, t // tm)

    def body(d_ref, w_ref, pg_ref, pu_ref, *rest):
        if carry_sibling is None:
            dg_ref, du_ref = rest
        else:
            g4_ref, dg_ref, du_ref, theirs_ref, send_sems, recv_sems = rest
            first_step, last_step = _grid_ends([pl.program_id(a) for a in range(2)], grid)

            @pl.when(first_step)
            def _():
                _sibling_start(g4_ref, theirs_ref, send_sems, recv_sems)

        dv = d_ref[...]
        for c0, cw in _col_chunks(F_HALF, MXU_COLS):
            cols = slice(c0, c0 + cw)
            dh = 0.5 * _nt(dv, w_ref[cols, :])
            dg_ref[:, cols] = (dh * pg_ref[:, cols].astype(F32)).astype(dg_ref.dtype)
            du_ref[:, cols] = (dh * pu_ref[:, cols].astype(F32)).astype(du_ref.dtype)

        if carry_sibling is not None:
            @pl.when(last_step)
            def _():
                _sibling_finish(g4_ref, theirs_ref, send_sems, recv_sems)

    vmem = 2 * (_nbytes((tm, d), BF16) + _nbytes((d, F_HALF), BF16) + 5 * _nbytes((tm, D_FF), F32))
    blk = pl.BlockSpec((tm, F_HALF), lambda j, i: (i, j))
    sds = jax.ShapeDtypeStruct((t, D_FF), BF16)
    in_specs = [pl.BlockSpec((tm, d), lambda j, i: (i, 0)), pl.BlockSpec((F_HALF, d), lambda j, i: (j, 0)), blk, blk]
    if carry_sibling is None:
        return pl.pallas_call(
            body, grid=grid, in_specs=in_specs, out_specs=[blk, blk], out_shape=[sds, sds], name=name,
            compiler_params=_cparams(("parallel", "parallel"), vmem),
        )(dxo, w_out, pg, pu)
    nchip, _, r, c = carry_sibling.shape
    any_spec = pl.BlockSpec(memory_space=pl.ANY)
    return pl.pallas_call(
        body, grid=grid, in_specs=in_specs + [any_spec], out_specs=[blk, blk, any_spec],
        out_shape=[sds, sds, jax.ShapeDtypeStruct((nchip, r, c), carry_sibling.dtype)],
        scratch_shapes=list(SIBLING_SEMS), name=name + "_exchange",
        compiler_params=_cparams(("arbitrary", "arbitrary"), vmem),
    )(dxo, w_out, pg, pu, carry_sibling)


def _dxn_norm_bwd(parts, b, x, g, dxo, name):
    t, d = x.shape
    k = parts[0].shape[1]
    n_parts = len(parts)
    tm = _pick(t, (256, 128))

    def body(*refs):
        a_refs, b_refs = refs[:n_parts], refs[n_parts:2 * n_parts]
        x_ref, g_ref, do_ref, dx_ref, dxb_ref, dg_ref = refs[2 * n_parts:]
        i = pl.program_id(0)
        dn = jnp.dot(a_refs[0][...], b_refs[0][...], preferred_element_type=F32)
        for a_ref, b_ref in zip(a_refs[1:], b_refs[1:]):
            dn = dn + jnp.dot(a_ref[...], b_ref[...], preferred_element_type=F32)
        xv = x_ref[...]
        r = lax.rsqrt(jnp.mean(xv * xv, axis=-1, keepdims=True) + RMS_EPS)
        xh = xv * r
        dgp = jnp.sum(dn * xh, axis=0, keepdims=True)
        dh = dn * g_ref[...]
        dx = do_ref[...] + r * (dh - xh * jnp.mean(dh * xh, axis=-1, keepdims=True))
        dx_ref[...] = dx
        dxb_ref[...] = dx.astype(dxb_ref.dtype)

        @pl.when(i == 0)
        def _():
            dg_ref[...] = dgp

        @pl.when(i > 0)
        def _():
            dg_ref[...] += dgp

    blk = pl.BlockSpec((tm, d), lambda i: (i, 0))
    row = pl.BlockSpec((1, d), lambda i: (0, 0))
    a_specs = [pl.BlockSpec((tm, k), lambda i: (i, 0)) for _ in parts]
    b_specs = [pl.BlockSpec((k, d), lambda i, kk=kk: (kk, 0)) for kk in range(n_parts)]
    vmem = 2 * n_parts * (_nbytes((tm, k), BF16) + _nbytes((k, d), BF16)) + 16 * tm * d * 4
    return pl.pallas_call(
        body, grid=(t // tm,), in_specs=a_specs + b_specs + [blk, row, blk], out_specs=[blk, blk, row],
        out_shape=[jax.ShapeDtypeStruct((t, d), F32), jax.ShapeDtypeStruct((t, d), BF16),
                   jax.ShapeDtypeStruct((1, d), F32)], name=name,
        compiler_params=_cparams(("arbitrary",), vmem),
    )(*parts, *([b] * n_parts), x, g, dxo)


def _final_loss_bwd(x, g, tgt):
    t, d = x.shape
    tm = _pick(t, (512, 256, 128))

    def body(x_ref, g_ref, t_ref, dx_ref, dxb_ref, dg_ref, loss_ref):
        i = pl.program_id(0)
        xv = x_ref[...]
        r = lax.rsqrt(jnp.mean(xv * xv, axis=-1, keepdims=True) + RMS_EPS)
        xh = xv * r
        gv = g_ref[...]
        err = xh * gv - t_ref[...]
        lp = 0.5 * jnp.sum(jnp.mean(err * err, axis=-1, keepdims=True), axis=0, keepdims=True)
        dy = err * (1.0 / d)
        dgp = jnp.sum(dy * xh, axis=0, keepdims=True)
        dh = dy * gv
        dx = r * (dh - xh * jnp.mean(dh * xh, axis=-1, keepdims=True))
        dx_ref[...] = dx
        dxb_ref[...] = dx.astype(dxb_ref.dtype)
        lpb = jnp.broadcast_to(lp, (1, LANES))

        @pl.when(i == 0)
        def _():
            dg_ref[...] = dgp
            loss_ref[...] = lpb

        @pl.when(i > 0)
        def _():
            dg_ref[...] += dgp
            loss_ref[...] += lpb

    blk = pl.BlockSpec((tm, d), lambda i: (i, 0))
    row = pl.BlockSpec((1, d), lambda i: (0, 0))
    return pl.pallas_call(
        body, grid=(t // tm,), in_specs=[blk, row, blk],
        out_specs=[blk, blk, row, pl.BlockSpec((1, LANES), lambda i: (0, 0))],
        out_shape=[jax.ShapeDtypeStruct((t, d), F32), jax.ShapeDtypeStruct((t, d), BF16),
                   jax.ShapeDtypeStruct((1, d), F32), jax.ShapeDtypeStruct((1, LANES), F32)], name="final_loss_bwd",
        compiler_params=_cparams(("arbitrary",), 16 * tm * d * 4),
    )(x, g, tgt)


def _seq_scan(v, seq, reverse):
    row = lax.broadcasted_iota(jnp.int32, v.shape, 0)
    k = 1
    while k < seq:
        if reverse:
            v = v + jnp.where(row < seq - k, pltpu.roll(v, seq - k, 0), 0.0)
        else:
            v = v + jnp.where(row >= k, pltpu.roll(v, k, 0), 0.0)
        k *= 2
    return v


def _log_sigmoid(v):
    return jnp.minimum(v, 0.0) - jnp.log(1.0 + jnp.exp(-jnp.abs(v)))


def _fox_prep(fl, bf, qkv, nb, seq):
    def body(f_ref, b_ref, q_ref, k_ref, v_ref, qa_ref, ka_ref, vm_ref):
        dsum = _seq_scan(_log_sigmoid(f_ref[...] + b_ref[...]), seq, False)
        d1 = dsum.astype(BF16).astype(F32)
        r1 = dsum - d1
        d2 = r1.astype(BF16).astype(F32)
        d3 = (r1 - d2).astype(BF16).astype(F32)
        lane = lax.broadcasted_iota(jnp.int32, (seq, LANES), 1)
        first = lane < HEAD_DIM
        l64 = jnp.where(first, lane, lane - HEAD_DIM)
        for p in range(N_PAIRS):
            def head_cols(a, p=p):
                return jnp.where(first, a[:, 2 * p:2 * p + 1], a[:, 2 * p + 1:2 * p + 2])

            e1, e2, e3 = head_cols(d1), head_cols(d2), head_cols(d3)
            aux_q = jnp.where(l64 == 0, e1, jnp.where(l64 == 1, e2, jnp.where(l64 == 2, e3,
                              jnp.where(l64 < 6, 1.0, 0.0)))).astype(BF16)
            aux_k = jnp.where(l64 < 3, 1.0, jnp.where(l64 == 3, -e1, jnp.where(l64 == 4, -e2,
                              jnp.where(l64 == 5, -e3, 0.0)))).astype(BF16)
            cols = slice(LANES * p, LANES * (p + 1))
            qs = q_ref[:, cols] * ATT_SCALE
            vp = v_ref[:, cols]
            zero = jnp.zeros_like(qs)
            qa_ref[0, p, 0, :, :LANES] = jnp.where(first, qs, zero)
            qa_ref[0, p, 0, :, LANES:] = jnp.where(first, aux_q, zero)
            qa_ref[0, p, 1, :, :LANES] = jnp.where(first, zero, qs)
            qa_ref[0, p, 1, :, LANES:] = jnp.where(first, zero, aux_q)
            ka_ref[0, p, :, :LANES] = k_ref[:, cols]
            ka_ref[0, p, :, LANES:] = aux_k
            vm_ref[0, p, 0] = jnp.where(first, vp, zero)
            vm_ref[0, p, 1] = jnp.where(first, zero, vp)

    def part(c):
        return pl.BlockSpec((seq, D_ATTN), lambda b, c=c: (b, c))

    return pl.pallas_call(
        body, grid=(nb,),
        in_specs=[pl.BlockSpec((seq, LANES), lambda b: (b, 0)), pl.BlockSpec((1, LANES), lambda b: (0, 0)),
                  part(0), part(1), part(2)],
        out_specs=[pl.BlockSpec((1, N_PAIRS, 2, seq, ATT_K), lambda b: (b, 0, 0, 0, 0)),
                   pl.BlockSpec((1, N_PAIRS, seq, ATT_K), lambda b: (b, 0, 0, 0)),
                   pl.BlockSpec((1, N_PAIRS, 2, seq, LANES), lambda b: (b, 0, 0, 0, 0))],
        out_shape=[jax.ShapeDtypeStruct((nb, N_PAIRS, 2, seq, ATT_K), BF16),
                   jax.ShapeDtypeStruct((nb, N_PAIRS, seq, ATT_K), BF16),
                   jax.ShapeDtypeStruct((nb, N_PAIRS, 2, seq, LANES), BF16)],
        name="fox_prep", compiler_params=_cparams(("parallel",), 48 * 1024 * 1024),
    )(fl, bf, qkv, qkv, qkv)


def _fox_prep_bwd(dd, fl, bf, seq):
    t = fl.shape[0]

    def body(d_ref, f_ref, b_ref, o_ref, db_ref):
        i = pl.program_id(0)
        dlog = _seq_scan(d_ref[...], seq, True)
        dfl = dlog * _sigmoid(-(f_ref[...] + b_ref[...]))
        o_ref[...] = dfl.astype(o_ref.dtype)
        dbp = jnp.sum(dfl, axis=0, keepdims=True)

        @pl.when(i == 0)
        def _():
            db_ref[...] = dbp

        @pl.when(i > 0)
        def _():
            db_ref[...] += dbp

    blk = pl.BlockSpec((seq, LANES), lambda b: (b, 0))
    row = pl.BlockSpec((1, LANES), lambda b: (0, 0))
    return pl.pallas_call(
        body, grid=(t // seq,), in_specs=[blk, blk, row], out_specs=[blk, row],
        out_shape=[jax.ShapeDtypeStruct((t, LANES), BF16), jax.ShapeDtypeStruct((1, LANES), F32)], name="fox_prep_bwd",
        compiler_params=_cparams(("arbitrary",), 24 * seq * LANES * 4),
    )(dd, fl, bf)


def _pair_rows(a, ta):
    lane = lax.broadcasted_iota(jnp.int32, (ta, LANES), 1)
    return jnp.where(lane < HEAD_DIM, a[:ta], a[ta:])


def _diag_mask(ta):
    r = lax.broadcasted_iota(jnp.int32, (2 * ta, ta), 0)
    c = lax.broadcasted_iota(jnp.int32, (2 * ta, ta), 1)
    return c <= jnp.where(r >= ta, r - ta, r)


def _nt(a, b):
    return lax.dot_general(a, b, (((1,), (1,)), ((), ())), preferred_element_type=F32)


def _tn(a, b):
    return lax.dot_general(a, b, (((0,), (0,)), ((), ())), preferred_element_type=F32)


def _grid_ends(ids, sizes):
    first = functools.reduce(jnp.logical_and, [i == 0 for i in ids])
    last = functools.reduce(jnp.logical_and, [i == n - 1 for i, n in zip(ids, sizes)])
    return first, last


def _fox_fwd(qa, ka, vm, nb, seq, ta, carry_gather=None):
    nq = seq // ta
    npp = ATT_PAIRS_FWD
    grid = (nb, N_PAIRS // npp, nq)

    def body(q_ref, k_ref, v_ref, *rest):
        if carry_gather is None:
            o_ref, lse_ref = rest
        else:
            x_ref, o_ref, lse_ref, gathered_ref, send_sems, recv_sems = rest
            first_step, last_step = _grid_ends([pl.program_id(a) for a in range(3)], grid)

            @pl.when(first_step)
            def _():
                _gather_start(x_ref, gathered_ref, send_sems, recv_sems)

        i = pl.program_id(2)
        q2s = [q_ref[0, pp].reshape(2 * ta, ATT_K) for pp in range(npp)]

        def step(j, carry, masked):
            rows = pl.ds(pl.multiple_of(j * ta, ta), ta)
            out = []
            for pp in range(npp):
                m, l, acc = carry[pp]
                s = _nt(q2s[pp], k_ref[0, pp, rows, :])
                if masked:
                    s = jnp.where(_diag_mask(ta), s, NEG_BIG)
                m_new = jnp.maximum(m, jnp.max(s, axis=-1, keepdims=True))
                p = jnp.exp(s - m_new)
                corr = jnp.exp(m - m_new)
                l = corr * l + jnp.sum(p, axis=-1, keepdims=True)
                pb = p.astype(BF16)
                pv = (jnp.dot(pb[:ta], v_ref[0, pp, 0, rows, :], preferred_element_type=F32)
                      + jnp.dot(pb[ta:], v_ref[0, pp, 1, rows, :], preferred_element_type=F32))
                out.append((m_new, l, _pair_rows(corr, ta) * acc + pv))
            return tuple(out)

        init = tuple((jnp.full((2 * ta, 1), NEG_BIG, F32), jnp.zeros((2 * ta, 1), F32),
                      jnp.zeros((ta, LANES), F32)) for _ in range(npp))
        carry = lax.fori_loop(0, i, functools.partial(step, masked=False), init)
        for pp, (m, l, acc) in enumerate(step(i, carry, True)):
            o_ref[:, LANES * pp:LANES * (pp + 1)] = (acc * _pair_rows(1.0 / l, ta)).astype(o_ref.dtype)
            lse = m + jnp.log(l)
            lse_ref[0, pp, 0] = lse[:ta]
            lse_ref[0, pp, 1] = lse[ta:]

        if carry_gather is not None:
            @pl.when(last_step)
            def _():
                _gather_finish(x_ref, gathered_ref, send_sems, recv_sems)

    vmem = (2 * npp * (_nbytes((seq, ATT_K), BF16) + 2 * _nbytes((seq, LANES), BF16)) + 24 * npp * ta * ta * 4
            + 8 * 1024 * 1024)
    in_specs = [pl.BlockSpec((1, npp, 2, ta, ATT_K), lambda b, g, i: (b, g, 0, i, 0)),
                pl.BlockSpec((1, npp, seq, ATT_K), lambda b, g, i: (b, g, 0, 0)),
                pl.BlockSpec((1, npp, 2, seq, LANES), lambda b, g, i: (b, g, 0, 0, 0))]
    out_specs = [pl.BlockSpec((ta, LANES * npp), lambda b, g, i: (b * nq + i, g)),
                 pl.BlockSpec((1, npp, 2, ta, 1), lambda b, g, i: (b, g, 0, i, 0))]
    out_shape = [jax.ShapeDtypeStruct((nb * seq, D_ATTN), BF16), jax.ShapeDtypeStruct((nb, N_PAIRS, 2, seq, 1), F32)]
    if carry_gather is None:
        return pl.pallas_call(
            body, grid=grid, in_specs=in_specs, out_specs=out_specs, out_shape=out_shape, name="fox_fwd",
            compiler_params=_cparams(("parallel", "parallel", "parallel"), vmem),
        )(qa, ka, vm)
    any_spec = pl.BlockSpec(memory_space=pl.ANY)
    return pl.pallas_call(
        body, grid=grid, in_specs=in_specs + [any_spec], out_specs=out_specs + [any_spec],
        out_shape=out_shape + [jax.ShapeDtypeStruct((N_DEV,) + carry_gather.shape, carry_gather.dtype)],
        scratch_shapes=list(GATHER_SEMS), name="fox_fwd_gather",
        compiler_params=_cparams(("arbitrary", "arbitrary", "arbitrary"), vmem),
    )(qa, ka, vm, carry_gather)


def _fox_bwd(qa, ka, vm, y, dy, lse, nb, seq, ta, carry_exchange=None):
    nq = seq // ta
    npp = ATT_PAIRS_BWD
    grid = (nb, N_PAIRS // npp, nq)

    def body(q_ref, k_ref, v_ref, o_ref, do_ref, lse_ref, *rest):
        if carry_exchange is None:
            dq_ref, dk_ref, dv_ref, rs_ref, cs_ref = rest
        else:
            t_ref, dq_ref, dk_ref, dv_ref, rs_ref, cs_ref, landed_ref, send_sems, recv_sems = rest
            first_step, last_step = _grid_ends([pl.program_id(a) for a in range(3)], grid)

            @pl.when(first_step)
            def _():
                _chips_start(t_ref, landed_ref, send_sems, recv_sems)

        i = pl.program_id(2)

        @pl.when(i == 0)
        def _():
            dk_ref[...] = jnp.zeros_like(dk_ref)
            dv_ref[...] = jnp.zeros_like(dv_ref)
            cs_ref[...] = jnp.zeros_like(cs_ref)

        first = lax.broadcasted_iota(jnp.int32, (ta, LANES), 1) < HEAD_DIM
        q2s, do2s, deltas, lses = [], [], [], []
        for pp in range(npp):
            cols = slice(LANES * pp, LANES * (pp + 1))
            q2s.append(q_ref[0, pp].reshape(2 * ta, ATT_K))
            do = do_ref[:, cols]
            doo = do * o_ref[:, cols].astype(F32)
            do2s.append(jnp.concatenate([jnp.where(first, do, 0.0), jnp.where(first, 0.0, do)], axis=0).astype(BF16))
            deltas.append(jnp.concatenate([jnp.sum(jnp.where(first, doo, 0.0), axis=-1, keepdims=True),
                                           jnp.sum(jnp.where(first, 0.0, doo), axis=-1, keepdims=True)], axis=0))
            lses.append(jnp.concatenate([lse_ref[0, pp, 0], lse_ref[0, pp, 1]], axis=0))

        def step(j, carry, masked):
            rows = pl.ds(pl.multiple_of(j * ta, ta), ta)
            out = []
            for pp in range(npp):
                dq_acc, rs_acc = carry[pp]
                cols = slice(LANES * pp, LANES * (pp + 1))
                ks = k_ref[0, pp, rows, :]
                s = _nt(q2s[pp], ks)
                if masked:
                    s = jnp.where(_diag_mask(ta), s, NEG_BIG)
                p = jnp.exp(s - lses[pp])
                dp = _nt(do2s[pp], v_ref[0, pp, 0, rows, :] + v_ref[0, pp, 1, rows, :])
                ds32 = p * (dp - deltas[pp])
                ds = ds32.astype(BF16)
                dk_ref[rows, cols] += _tn(ds, q2s[pp][:, :LANES])
                dv_ref[rows, cols] += _tn(p.astype(BF16), do2s[pp])
                cs_ref[0, pp, 0, j] += jnp.sum(ds32[:ta], axis=0, keepdims=True)
                cs_ref[0, pp, 1, j] += jnp.sum(ds32[ta:], axis=0, keepdims=True)
                out.append((dq_acc + jnp.dot(ds, ks[:, :LANES], preferred_element_type=F32),
                            rs_acc + jnp.sum(ds32, axis=-1, keepdims=True)))
            return tuple(out)

        init = tuple((jnp.zeros((2 * ta, LANES), F32), jnp.zeros((2 * ta, 1), F32)) for _ in range(npp))
        carry = lax.fori_loop(0, i, functools.partial(step, masked=False), init)
        for pp, (dq_acc, rs_acc) in enumerate(step(i, carry, True)):
            dq = jnp.where(first, dq_acc[:ta], dq_acc[ta:]) * ATT_SCALE
            dq_ref[:, LANES * pp:LANES * (pp + 1)] = dq.astype(dq_ref.dtype)
            rs_row = jnp.transpose(jnp.broadcast_to(rs_acc, (2 * ta, LANES)))[0:1]
            rs_ref[0, pp, 0, 0] = rs_row[:, :ta]
            rs_ref[0, pp, 1, 0] = rs_row[:, ta:]

        if carry_exchange is not None:
            @pl.when(last_step)
            def _():
                _chips_finish(t_ref, landed_ref, send_sems, recv_sems)

    vmem = (2 * npp * (_nbytes((seq, ATT_K), BF16) + 2 * _nbytes((seq, LANES), BF16) + 2 * _nbytes((seq, LANES), F32))
            + 32 * npp * ta * ta * 4 + 8 * 1024 * 1024)
    qblk = lambda b, g, i: (b * nq + i, g)
    acc_blk = pl.BlockSpec((seq, LANES * npp), lambda b, g, i: (b, g))
    in_specs = [pl.BlockSpec((1, npp, 2, ta, ATT_K), lambda b, g, i: (b, g, 0, i, 0)),
                pl.BlockSpec((1, npp, seq, ATT_K), lambda b, g, i: (b, g, 0, 0)),
                pl.BlockSpec((1, npp, 2, seq, LANES), lambda b, g, i: (b, g, 0, 0, 0)),
                pl.BlockSpec((ta, LANES * npp), qblk), pl.BlockSpec((ta, LANES * npp), qblk),
                pl.BlockSpec((1, npp, 2, ta, 1), lambda b, g, i: (b, g, 0, i, 0))]
    out_specs = [pl.BlockSpec((ta, LANES * npp), qblk), acc_blk, acc_blk,
                 pl.BlockSpec((1, npp, 2, 1, 1, ta), lambda b, g, i: (b, g, 0, i, 0, 0)),
                 pl.BlockSpec((1, npp, 2, nq, 1, ta), lambda b, g, i: (b, g, 0, 0, 0, 0))]
    sums = jax.ShapeDtypeStruct((nb, N_PAIRS, 2, nq, 1, ta), F32)
    out_shape = [jax.ShapeDtypeStruct((nb * seq, D_ATTN), BF16), jax.ShapeDtypeStruct((nb * seq, D_ATTN), F32),
                 jax.ShapeDtypeStruct((nb * seq, D_ATTN), F32), sums, sums]
    if carry_exchange is None:
        return pl.pallas_call(
            body, grid=grid, in_specs=in_specs, out_specs=out_specs, out_shape=out_shape, name="fox_bwd",
            compiler_params=_cparams(("parallel", "parallel", "arbitrary"), vmem),
        )(qa, ka, vm, y, dy, lse)
    any_spec = pl.BlockSpec(memory_space=pl.ANY)
    return pl.pallas_call(
        body, grid=grid, in_specs=in_specs + [any_spec], out_specs=out_specs + [any_spec],
        out_shape=out_shape + [jax.ShapeDtypeStruct(carry_exchange.shape, carry_exchange.dtype)],
        scratch_shapes=list(CHIPS_SEMS), name="fox_bwd_exchange",
        compiler_params=_cparams(("arbitrary", "arbitrary", "arbitrary"), vmem),
    )(qa, ka, vm, y, dy, lse, carry_exchange)


def _shift_down(a, k):
    row = lax.broadcasted_iota(jnp.int32, a.shape, 0)
    return jnp.where(row >= k, pltpu.roll(a, k, 0), 0.0)


def _shift_up(a, k):
    n = a.shape[0]
    row = lax.broadcasted_iota(jnp.int32, a.shape, 0)
    return jnp.where(row < n - k, pltpu.roll(a, n - k, 0), 0.0)


def _by_group(vals, shape):
    lane = lax.broadcasted_iota(jnp.int32, shape, 1)
    out = vals[-1]
    for gi in range(len(vals) - 2, -1, -1):
        out = jnp.where(lane < POOL_GROUP * (gi + 1), vals[gi], out)
    return out


def _pooled(u):
    s2 = u + _shift_down(u, 1)
    s4 = s2 + _shift_down(s2, 2)
    s8 = s4 + _shift_down(s4, 4)
    s16 = s8 + _shift_down(s8, 8)
    win = _by_group([s2, s4, s8, s16], u.shape)
    row = lax.broadcasted_iota(jnp.int32, u.shape, 0)
    wsize = _by_group([jnp.full(u.shape, w, jnp.int32) for w in POOL_WINDOWS], u.shape)
    inv = 1.0 / jnp.minimum(row + 1, wsize).astype(F32)
    return win * inv - u, inv


def _pool_fwd(rest, wbd, scale, seq):
    t = rest.shape[0]

    def body(u_ref, w_ref, s_ref, o_ref):
        pooled, _ = _pooled(u_ref[...])
        pw = jnp.dot(pooled.astype(BF16), w_ref[...], preferred_element_type=F32)
        o_ref[...] = (pw * s_ref[...]).astype(o_ref.dtype)

    blk = pl.BlockSpec((seq, D_POOL), lambda b: (b, 0))
    return pl.pallas_call(
        body, grid=(t // seq,),
        in_specs=[blk, pl.BlockSpec((D_POOL, D_POOL), lambda b: (0, 0)), pl.BlockSpec((1, D_POOL), lambda b: (0, 0))],
        out_specs=blk, out_shape=jax.ShapeDtypeStruct((t, D_POOL), BF16), name="pool_fwd",
        compiler_params=_cparams(("parallel",), 24 * seq * D_POOL * 4),
    )(rest, wbd, scale)


def _pool_bwd(rest, dy, wbd, wbd_t, scale, seq):
    t = rest.shape[0]

    def body(u_ref, dy_ref, w_ref, wt_ref, s_ref, du_ref, dw_ref, dsc_ref):
        i = pl.program_id(0)
        pooled, inv = _pooled(u_ref[...])
        pb = pooled.astype(BF16)
        pw = jnp.dot(pb, w_ref[...], preferred_element_type=F32)
        dyp = dy_ref[...]
        dsp = jnp.sum(dyp * pw, axis=0, keepdims=True)
        dpw = (dyp * s_ref[...]).astype(BF16)
        dwp = _tn(pb, dpw)
        dpooled = jnp.dot(dpw, wt_ref[...], preferred_element_type=F32)
        dwin = dpooled * inv
        t2 = dwin + _shift_up(dwin, 1)
        t4 = t2 + _shift_up(t2, 2)
        t8 = t4 + _shift_up(t4, 4)
        t16 = t8 + _shift_up(t8, 8)
        du_ref[...] = (_by_group([t2, t4, t8, t16], dwin.shape) - dpooled).astype(du_ref.dtype)

        @pl.when(i == 0)
        def _():
            dw_ref[...] = dwp
            dsc_ref[...] = dsp

        @pl.when(i > 0)
        def _():
            dw_ref[...] += dwp
            dsc_ref[...] += dsp

    blk = pl.BlockSpec((seq, D_POOL), lambda b: (b, 0))
    sq = pl.BlockSpec((D_POOL, D_POOL), lambda b: (0, 0))
    row = pl.BlockSpec((1, D_POOL), lambda b: (0, 0))
    return pl.pallas_call(
        body, grid=(t // seq,),
        in_specs=[blk, pl.BlockSpec((seq, D_POOL), lambda b: (b, 2)), sq, sq, row],
        out_specs=[blk, sq, row],
        out_shape=[jax.ShapeDtypeStruct((t, D_POOL), BF16), jax.ShapeDtypeStruct((D_POOL, D_POOL), F32),
                   jax.ShapeDtypeStruct((1, D_POOL), F32)], name="pool_bwd",
        compiler_params=_cparams(("arbitrary",), 40 * seq * D_POOL * 4),
    )(rest, dy, wbd, wbd_t, scale)


def _conv_fwd(rest, cw, seq):
    t = rest.shape[0]

    def body(cb_ref, cc_ref, ch_ref, w_ref, o_ref):
        u = cc_ref[...] * ch_ref[...]
        y = w_ref[0:1, :] * _shift_down(u, 2) + w_ref[1:2, :] * _shift_down(u, 1) + w_ref[2:3, :] * u
        o_ref[...] = (cb_ref[...] * y).astype(o_ref.dtype)

    def col(c):
        return pl.BlockSpec((seq, D_CONV), lambda b, c=c: (b, c))

    return pl.pallas_call(
        body, grid=(t // seq,), in_specs=[col(1), col(2), col(3), pl.BlockSpec((8, D_CONV), lambda b: (0, 0))],
        out_specs=pl.BlockSpec((seq, D_CONV), lambda b: (b, 0)),
        out_shape=jax.ShapeDtypeStruct((t, D_CONV), BF16), name="conv_fwd",
        compiler_params=_cparams(("parallel",), 24 * seq * D_CONV * 4),
    )(rest, rest, rest, cw)


def _conv_bwd(rest, dy, cw, seq):
    t = rest.shape[0]

    def body(cb_ref, cc_ref, ch_ref, dy_ref, w_ref, o_ref, dw_ref):
        i = pl.program_id(0)
        cc = cc_ref[...]
        ch = ch_ref[...]
        u = cc * ch
        u1 = _shift_down(u, 1)
        u2 = _shift_down(u, 2)
        y = w_ref[0:1, :] * u2 + w_ref[1:2, :] * u1 + w_ref[2:3, :] * u
        dyc = dy_ref[...]
        d2 = dyc * cb_ref[...]
        du = w_ref[0:1, :] * _shift_up(d2, 2) + w_ref[1:2, :] * _shift_up(d2, 1) + w_ref[2:3, :] * d2
        o_ref[:, 0:D_CONV] = (dyc * y).astype(o_ref.dtype)
        o_ref[:, D_CONV:2 * D_CONV] = (du * ch).astype(o_ref.dtype)
        o_ref[:, 2 * D_CONV:3 * D_CONV] = (du * cc).astype(o_ref.dtype)
        tap = lax.broadcasted_iota(jnp.int32, (8, D_CONV), 0)
        dwp = jnp.where(tap == 0, jnp.sum(d2 * u2, axis=0, keepdims=True),
                        jnp.where(tap == 1, jnp.sum(d2 * u1, axis=0, keepdims=True),
                                  jnp.where(tap == 2, jnp.sum(d2 * u, axis=0, keepdims=True), 0.0)))

        @pl.when(i == 0)
        def _():
            dw_ref[...] = dwp

        @pl.when(i > 0)
        def _():
            dw_ref[...] += dwp

    def col(c):
        return pl.BlockSpec((seq, D_CONV), lambda b, c=c: (b, c))

    taps = pl.BlockSpec((8, D_CONV), lambda b: (0, 0))
    return pl.pallas_call(
        body, grid=(t // seq,), in_specs=[col(1), col(2), col(3), col(3), taps],
        out_specs=[pl.BlockSpec((seq, 3 * D_CONV), lambda b: (b, 0)), taps],
        out_shape=[jax.ShapeDtypeStruct((t, 3 * D_CONV), BF16), jax.ShapeDtypeStruct((8, D_CONV), F32)],
        name="conv_bwd", compiler_params=_cparams(("arbitrary",), 48 * seq * D_CONV * 4),
    )(rest, rest, rest, dy, cw)


def _adamw(w, g, m, v, name):
    r, c = w.shape
    tr = _pick(r, (512, 352, 256, 128)) if r > 512 else r

    def body(w_ref, g_ref, m_ref, v_ref, d_ref, mo_ref, vo_ref):
        gv = g_ref[...]
        mn = ADAM_B1 * m_ref[...] + (1.0 - ADAM_B1) * gv
        vn = ADAM_B2 * v_ref[...] + (1.0 - ADAM_B2) * (gv * gv)
        m_hat = mn / (1.0 - ADAM_B1 ** ADAM_STEP)
        v_hat = vn / (1.0 - ADAM_B2 ** ADAM_STEP)
        d_ref[...] = -ADAM_LR * (m_hat / (jnp.sqrt(v_hat) + ADAM_EPS) + ADAM_WD * w_ref[...])
        mo_ref[...] = mn
        vo_ref[...] = vn

    blk = pl.BlockSpec((tr, c), lambda i: (i, 0))
    sds = jax.ShapeDtypeStruct((r, c), F32)
    return pl.pallas_call(
        body, grid=(r // tr,), in_specs=[blk] * 4, out_specs=[blk] * 3, out_shape=[sds] * 3, name=name,
        compiler_params=_cparams(("parallel",), 20 * tr * max(c, LANES) * 4),
    )(w, g, m, v)


def _sum_slots(a, name):
    ns, r, c = a.shape
    tr = _pick(r, (368, 256, 184, 136, 128, 88, 8))

    def body(a_ref, o_ref):
        acc = a_ref[0].astype(F32)
        for s in range(1, ns):
            acc = acc + a_ref[s].astype(F32)
        o_ref[...] = acc

    return pl.pallas_call(
        body, grid=(r // tr,), in_specs=[pl.BlockSpec((ns, tr, c), lambda i: (0, i, 0))],
        out_specs=pl.BlockSpec((tr, c), lambda i: (i, 0)), out_shape=jax.ShapeDtypeStruct((r, c), F32), name=name,
        compiler_params=_cparams(("parallel",), 4 * (ns + 2) * tr * c * 4),
    )(a)


def _add_core_half(core, g4, theirs, out_dtype, name):
    ns, _, r, c = g4.shape
    tr = _pick(r, (368, 256, 184, 136, 128, 88, 8))

    def body(core_ref, a_ref, b_ref, o_ref):
        o_ref[...] = (a_ref[0] + b_ref[...]).astype(o_ref.dtype)

    blk = pl.BlockSpec((1, tr, c), lambda s, i, core_ref: (s, i, 0))
    return pl.pallas_call(
        body,
        grid_spec=pltpu.PrefetchScalarGridSpec(
            num_scalar_prefetch=1, grid=(ns, r // tr),
            in_specs=[pl.BlockSpec((1, 1, tr, c), lambda s, i, core_ref: (s, core_ref[0], i, 0)), blk],
            out_specs=blk),
        out_shape=jax.ShapeDtypeStruct((ns, r, c), out_dtype), name=name,
        compiler_params=_cparams(("parallel", "parallel"), 10 * tr * c * 4),
    )(core, g4, theirs)


def _sum_chips(order, own, landed, name):
    ns, r, c = own.shape
    tr = _pick(r, (368, 256, 184, 136, 128, 88, 8))

    def body(order_ref, a_ref, b1_ref, b2_ref, b3_ref, o_ref):
        o_ref[...] = ((a_ref[0].astype(F32) + b1_ref[0].astype(F32)) + b2_ref[0].astype(F32)) + b3_ref[0].astype(F32)

    def slot(k):
        return pl.BlockSpec((1, tr, c), lambda i, order_ref, k=k: (order_ref[k], i, 0))

    return pl.pallas_call(
        body,
        grid_spec=pltpu.PrefetchScalarGridSpec(
            num_scalar_prefetch=1, grid=(r // tr,), in_specs=[slot(0), slot(1), slot(2), slot(3)],
            out_specs=pl.BlockSpec((tr, c), lambda i, order_ref: (i, 0))),
        out_shape=jax.ShapeDtypeStruct((r, c), F32), name=name,
        compiler_params=_cparams(("parallel",), 16 * tr * c * 4),
    )(order, own, landed, landed, landed)


def _mesh_pos():
    return lax.axis_index("x"), lax.axis_index("y"), lax.axis_index("c")


def _all_gather(x, name):
    r, c = x.shape

    def body(x_ref, out_ref, send_sems, recv_sems):
        _gather_start(x_ref, out_ref, send_sems, recv_sems)
        _gather_finish(x_ref, out_ref, send_sems, recv_sems)

    gathered = pl.pallas_call(
        body, out_shape=jax.ShapeDtypeStruct((N_DEV, r, c), x.dtype),
        in_specs=[pl.BlockSpec(memory_space=pl.ANY)], out_specs=pl.BlockSpec(memory_space=pl.ANY),
        scratch_shapes=list(GATHER_SEMS), name=name,
    )(x)
    return _fill_own_slot(gathered, x)


GATHER_SEMS = (pltpu.SemaphoreType.DMA((7,)), pltpu.SemaphoreType.DMA((7,)))


def _fill_own_slot(gathered, x):
    mx, my, mc = _mesh_pos()
    return lax.dynamic_update_slice_in_dim(gathered, x[None], 4 * mx + 2 * my + mc, axis=0)


def _gather_copies(x_ref, out_ref, send_sems, recv_sems):
    mx, my, mc = _mesh_pos()
    me, sibling = (mx, my, mc), (mx, my, 1 - mc)
    chips = [(1 - mx, my), (mx, 1 - my), (1 - mx, 1 - my)]

    def slot(px, py, pc):
        return out_ref.at[4 * px + 2 * py + pc]

    def copy(k, block, to, src=None):
        return pltpu.make_async_remote_copy(
            src_ref=slot(*block) if src is None else src, dst_ref=slot(*block),
            send_sem=send_sems.at[k], recv_sem=recv_sems.at[k],
            device_id=to, device_id_type=pl.DeviceIdType.MESH)

    first = [copy(0, me, sibling, src=x_ref)]
    first += [copy(1 + j, me, (*chip, mc), src=x_ref) for j, chip in enumerate(chips)]
    passed = [copy(4 + j, (*chip, mc), sibling) for j, chip in enumerate(chips)]
    over_ici = [copy(1 + j, (*chip, mc), me) for j, chip in enumerate(chips)]
    over_d2d = [copy(0, sibling, me)] + [copy(4 + j, (*chip, 1 - mc), me) for j, chip in enumerate(chips)]
    return first, passed, over_ici, over_d2d


def _gather_start(x_ref, out_ref, send_sems, recv_sems):
    for cp in _gather_copies(x_ref, out_ref, send_sems, recv_sems)[0]:
        cp.start()


def _gather_finish(x_ref, out_ref, send_sems, recv_sems):
    first, passed, over_ici, over_d2d = _gather_copies(x_ref, out_ref, send_sems, recv_sems)
    for landed, relay in zip(over_ici, passed):
        landed.wait_recv()
        relay.start()
    for landed in over_d2d:
        landed.wait_recv()
    for cp in first + passed:
        cp.wait_send()


def _exchange_sibling(g4, name):
    nchip, _, r, c = g4.shape

    def body(g_ref, theirs_ref, send_sems, recv_sems):
        _sibling_start(g_ref, theirs_ref, send_sems, recv_sems)
        _sibling_finish(g_ref, theirs_ref, send_sems, recv_sems)

    any_spec = pl.BlockSpec(memory_space=pl.ANY)
    return pl.pallas_call(
        body, out_shape=jax.ShapeDtypeStruct((nchip, r, c), g4.dtype), in_specs=[any_spec], out_specs=any_spec,
        scratch_shapes=list(SIBLING_SEMS), name=name,
    )(g4)


SIBLING_SEMS = (pltpu.SemaphoreType.DMA((N_CHIPS,)), pltpu.SemaphoreType.DMA((N_CHIPS,)))


def _sibling_copies(g_ref, theirs_ref, send_sems, recv_sems):
    mx, my, mc = _mesh_pos()
    return [pltpu.make_async_remote_copy(
        src_ref=g_ref.at[chip, 1 - mc], dst_ref=theirs_ref.at[chip],
        send_sem=send_sems.at[chip], recv_sem=recv_sems.at[chip],
        device_id=(mx, my, 1 - mc), device_id_type=pl.DeviceIdType.MESH) for chip in range(N_CHIPS)]


def _sibling_start(g_ref, theirs_ref, send_sems, recv_sems):
    for cp in _sibling_copies(g_ref, theirs_ref, send_sems, recv_sems):
        cp.start()


def _sibling_finish(g_ref, theirs_ref, send_sems, recv_sems):
    copies = _sibling_copies(g_ref, theirs_ref, send_sems, recv_sems)
    for cp in copies:
        cp.wait_recv()
    for cp in copies:
        cp.wait_send()


def _exchange_chips(ts, name):
    def body(t_ref, out_ref, send_sems, recv_sems):
        _chips_start(t_ref, out_ref, send_sems, recv_sems)
        _chips_finish(t_ref, out_ref, send_sems, recv_sems)

    any_spec = pl.BlockSpec(memory_space=pl.ANY)
    return pl.pallas_call(
        body, out_shape=jax.ShapeDtypeStruct(ts.shape, ts.dtype), in_specs=[any_spec], out_specs=any_spec,
        scratch_shapes=list(CHIPS_SEMS), name=name,
    )(ts)


CHIPS_SEMS = (pltpu.SemaphoreType.DMA((N_CHIPS - 1,)), pltpu.SemaphoreType.DMA((N_CHIPS - 1,)))


def _chips_copies(t_ref, out_ref, send_sems, recv_sems):
    mx, my, mc = _mesh_pos()
    my_chip = 2 * mx + my
    copies = []
    for k in range(1, N_CHIPS):
        px = 1 - mx if k & 2 else mx
        py = 1 - my if k & 1 else my
        peer_chip = 2 * px + py

        def rdma(dst_slot, px=px, py=py, peer_chip=peer_chip, k=k):
            return pltpu.make_async_remote_copy(
                src_ref=t_ref.at[peer_chip], dst_ref=out_ref.at[dst_slot],
                send_sem=send_sems.at[k - 1], recv_sem=recv_sems.at[k - 1],
                device_id=(px, py, mc), device_id_type=pl.DeviceIdType.MESH)

        copies.append((rdma(my_chip), rdma(peer_chip)))
    return copies


def _chips_start(t_ref, out_ref, send_sems, recv_sems):
    for send, _ in _chips_copies(t_ref, out_ref, send_sems, recv_sems):
        send.start()


def _chips_finish(t_ref, out_ref, send_sems, recv_sems):
    copies = _chips_copies(t_ref, out_ref, send_sems, recv_sems)
    for _, landed in copies:
        landed.wait_recv()
    for send, _ in copies:
        send.wait_send()


def _add_sibling(g4, theirs):
    core = jnp.reshape(lax.axis_index("c"), (1,)).astype(jnp.int32)
    return _add_core_half(core, g4, theirs, BF16, name="add_sibling_grads")


def _sum_landed(chip_sums, landed):
    mx, my, _ = _mesh_pos()
    order = jnp.stack([2 * mx + my, 2 * (1 - mx) + my, 2 * mx + (1 - my), 2 * (1 - mx) + (1 - my)]).astype(jnp.int32)
    return _sum_chips(order, chip_sums, landed, name="sum_grads")


def _perm_mix_cols(wm):
    f0 = D_QKV
    f1 = f0 + N_HEADS
    pad = jnp.zeros((wm.shape[0], LANES - N_HEADS), wm.dtype)
    return jnp.concatenate([wm[:, :f0], wm[:, f1:], wm[:, f0:f1], pad], axis=1)


def _unperm_mix_rows(gt):
    f0 = D_QKV
    return jnp.concatenate([gt[:f0], gt[f0 + D_REST:f0 + D_REST + N_HEADS], gt[f0:f0 + D_REST]], axis=0)


def _pack_shards(parts, l, dtype):
    w1i, w1o, wmi, wmo, w2i, w2o = parts
    rows = [w1i[l].T, w1o[l], jnp.pad(wmi[l].T, ((0, MIX_ROWS_PAD - MIX_ROWS), (0, 0))), wmo[l], w2i[l].T, w2o[l]]
    return jnp.concatenate(rows, axis=0).astype(dtype)


def _layer_weights(wg):
    offs = {}
    o = 0
    for nm, n in (("f1i", FFN_ROWS), ("f1o", OUT_ROWS), ("mi", MIX_ROWS_PAD), ("mo", MO_ROWS), ("f2i", FFN_ROWS),
                  ("f2o", OUT_ROWS)):
        offs[nm] = (o, n)
        o += n

    def piece(nm, n_used=None):
        o, n = offs[nm]
        return wg[:, o:o + (n if n_used is None else n_used)]

    out = {}
    for tag in ("f1", "f2"):
        wi_t = piece(tag + "i").reshape(2 * D_FF, D_MODEL)
        wo = piece(tag + "o").reshape(D_FF, D_MODEL)
        out[tag] = dict(wi_t=wi_t, wo=wo)
    wm = _perm_mix_cols(piece("mi", MIX_ROWS).reshape(D_IN, D_MODEL).T)
    wm_t = wm.T
    wo = piece("mo").reshape(D_MODEL, D_MODEL)
    out["mix"] = dict(w_qkv=wm[:, :D_QKV], w_rest=wm[:, D_QKV:D_QKV + D_REST], w_f=wm[:, D_QKV + D_REST:],
                      wm_t=wm_t, wo=wo, wo_t=wo.T)
    return out


def _layer_grad_rows(gr):
    def halves(gate_t, up_t):
        return jnp.concatenate([gate_t.reshape(N_DEV // 2, FFN_ROWS, D_MODEL),
                                up_t.reshape(N_DEV // 2, FFN_ROWS, D_MODEL)], axis=0)

    gmi = _unperm_mix_rows(gr["mix_in_t"]).reshape(N_DEV, MIX_ROWS, D_MODEL)
    gmi = jnp.pad(gmi, ((0, 0), (0, MIX_ROWS_PAD - MIX_ROWS), (0, 0)))
    return jnp.concatenate(
        [halves(*gr["f1_in_t"]), gr["f1_out"].reshape(N_DEV, OUT_ROWS, D_MODEL), gmi,
         gr["mix_out"].reshape(N_DEV, MO_ROWS, D_MODEL), halves(*gr["f2_in_t"]),
         gr["f2_out"].reshape(N_DEV, OUT_ROWS, D_MODEL)], axis=1)


def _out_proj(a, w, x, alpha, next_gain, name):
    if next_gain is None:
        return _mm_nn(a, w, out_dtype=F32, res=x, alpha=alpha, name=name), None
    return _mm_nn(a, w, out_dtype=F32, res=x, alpha=alpha, next_gain=next_gain, name=name + "_norm")


def _ffn_forward(x, xn, w, next_gain):
    h, pg, pu = _ffn_in(xn, w["wi_t"], name="ffn_in")
    x_new, xn_next = _out_proj(h, w["wo"], x, 0.5, next_gain, "ffn_out")
    return x_new, xn_next, dict(x=x, xn=xn, h=h, pg=pg, pu=pu)


def _ffn_backward(dxo, dxo_b, gain, w, saved, carry_sibling=None):
    res = _ffn_bwd_mid(dxo_b, w["wo"], saved["pg"], saved["pu"], name="ffn_bwd_mid", carry_sibling=carry_sibling)
    dzg, dzu = res[:2]
    theirs = None if carry_sibling is None else res[2]
    g_out = _mm_tn(saved["h"], dxo_b, alpha=0.5, tm=F_HALF, name="ffn_gw_out")
    g_in_t = (_mm_tn(dzg, saved["xn"], tm=F_HALF, name="ffn_gw_in"), _mm_tn(dzu, saved["xn"], tm=F_HALF, name="ffn_gw_in"))
    dx, dx_b, dg = _dxn_norm_bwd([dzg, dzu], w["wi_t"], saved["x"], gain, dxo, name="ffn_dxn_norm_bwd")
    return dx, dx_b, dg, g_in_t, g_out, theirs


def _mixer_forward(x, xn, p, w, nb, seq, ta, next_gain, next_pack=None):
    qkv = _mm_nn(xn, w["w_qkv"], out_dtype=BF16, name="mix_qkv")
    rest = _mm_nn(xn, w["w_rest"], out_dtype=F32, name="mix_rest")
    fl = _mm_nn(xn, w["w_f"], out_dtype=F32, name="mix_f")
    qa, ka, vm = _fox_prep(fl, p["bf"], qkv, nb, seq)
    if next_pack is None:
        (y_attn, lse), next_gathered = _fox_fwd(qa, ka, vm, nb, seq, ta), None
    else:
        y_attn, lse, next_gathered = _fox_fwd(qa, ka, vm, nb, seq, ta, carry_gather=next_pack)
        next_gathered = _fill_own_slot(next_gathered, next_pack)
    y_pool = _pool_fwd(rest, p["wbd"], p["scale"], seq)
    y_conv = _conv_fwd(rest, p["cw"], seq)
    y = jnp.concatenate([y_attn, y_pool, y_conv], axis=1)
    x_new, xn_next = _out_proj(y, w["wo"], x, 1.0, next_gain, "mix_out")
    return x_new, xn_next, dict(x=x, xn=xn, qa=qa, ka=ka, vm=vm, rest=rest, fl=fl, lse=lse, y=y), next_gathered


def _mixer_backward(dxo, dxo_b, p, w, sv, nb, seq, ta, pending=None):
    t = dxo.shape[0]
    dy = _mm_nn(dxo_b, w["wo_t"], out_dtype=F32, name="mix_dy")
    g_out = _mm_tn(sv["y"], dxo_b, name="mix_gw_out")
    res = _fox_bwd(sv["qa"], sv["ka"], sv["vm"], sv["y"], dy, sv["lse"], nb, seq, ta, carry_exchange=pending)
    dq, dk, dv, d_rows, d_cols = res[:5]
    landed = None if pending is None else res[5]
    ddh = (d_rows.reshape(nb, N_HEADS, seq) - d_cols.reshape(nb, N_HEADS, seq)).transpose(0, 2, 1)
    ddh = ddh.reshape(t, N_HEADS)
    dfl, dbf = _fox_prep_bwd(jnp.pad(ddh, ((0, 0), (0, LANES - N_HEADS))), sv["fl"], p["bf"], seq)
    dpool, dwbd, dscale = _pool_bwd(sv["rest"], dy, p["wbd"], p["wbd_t"], p["scale"], seq)
    dconv, dcw = _conv_bwd(sv["rest"], dy, p["cw"], seq)
    dproj = jnp.concatenate([dq, dk.astype(BF16), dv.astype(BF16), dpool, dconv, dfl], axis=1)
    g_in_t = _mm_tn(dproj, sv["xn"], tm=D_INP // 3, name="mix_gw_in")
    dx, dx_b, dg = _dxn_norm_bwd([dproj], w["wm_t"], sv["x"], p["norm"], dxo, name="mix_dxn_norm_bwd")
    return dx, dx_b, dict(norm=dg, bf=dbf, wbd=dwbd, scale=dscale, cw=dcw, mix_in_t=g_in_t, mix_out=g_out), landed


def _block_diag(wp):
    z = jnp.zeros((POOL_GROUP, POOL_GROUP), wp.dtype)
    return jnp.concatenate(
        [jnp.concatenate([wp[g] if g == r else z for g in range(4)], axis=1) for r in range(4)], axis=0)


def _row_pad(a, rows):
    a = a.reshape(-1, a.shape[-1])
    return jnp.pad(a, ((0, rows - a.shape[0]), (0, 0)))


def kernel(x, norm_ffn1, w_ffn1_in, w_ffn1_out, norm_mix, w_mix_in, b_forget, w_pool, pool_scale, conv_w, w_mix_out, norm_ffn2, w_ffn2_in, w_ffn2_out, norm_final, loss_target, m_norm_ffn1, m_w_ffn1_in, m_w_ffn1_out, m_norm_mix, m_w_mix_in, m_b_forget, m_w_pool, m_pool_scale, m_conv_w, m_w_mix_out, m_norm_ffn2, m_w_ffn2_in, m_w_ffn2_out, m_norm_final, v_norm_ffn1, v_w_ffn1_in, v_w_ffn1_out, v_norm_mix, v_w_mix_in, v_b_forget, v_w_pool, v_pool_scale, v_conv_w, v_w_mix_out, v_norm_ffn2, v_w_ffn2_in, v_w_ffn2_out, v_norm_final):
    nb, seq, d = x.shape
    depth = norm_ffn1.shape[0]
    t = nb * seq
    ta = _pick(seq, (ATT_TILE, 128))
    my_id = 4 * lax.axis_index("x") + 2 * lax.axis_index("y") + lax.axis_index("c")
    cshard = conv_w.shape[-1]

    shards = (w_ffn1_in, w_ffn1_out, w_mix_in, w_mix_out, w_ffn2_in, w_ffn2_out)
    wg = _all_gather(_pack_shards(shards, 0, BF16), name="gather_weights")
    cw_g = _all_gather(_row_pad(conv_w.reshape(depth * 3, cshard), 16).reshape(4, LANES), name="gather_conv_taps")
    cw_all = cw_g.reshape(N_DEV, 16, cshard)[:, :depth * 3].reshape(N_DEV, depth, 3, cshard)
    cw_all = cw_all.transpose(1, 2, 0, 3).reshape(depth, 3, D_CONV)

    xs = x.reshape(t, d)
    xn = _rmsnorm_fwd(xs, norm_ffn1[0][None], name="first_norm")
    saved = []
    for l in range(depth):
        w = _layer_weights(wg)
        wbd = _block_diag(w_pool[l])
        p = dict(norm=norm_mix[l][None], bf=jnp.pad(b_forget[l], (0, LANES - N_HEADS))[None],
                 wbd=wbd.astype(BF16), wbd_t=wbd.T.astype(BF16), scale=pool_scale[l][None],
                 cw=_row_pad(cw_all[l], 8))
        xs, xn, s1 = _ffn_forward(xs, xn, w["f1"], norm_mix[l][None])
        next_pack = _pack_shards(shards, l + 1, BF16) if l + 1 < depth else None
        xs, xn, sm, wg = _mixer_forward(xs, xn, p, w["mix"], nb, seq, ta, norm_ffn2[l][None], next_pack)
        xs, xn, s2 = _ffn_forward(xs, xn, w["f2"], norm_ffn1[l + 1][None] if l + 1 < depth else None)
        saved.append((w, p, s1, sm, s2))

    dx, dx_b, g_norm_final, loss_part = _final_loss_bwd(xs, norm_final[None], loss_target.reshape(t, d))
    layer_g = [None] * depth
    small = [None] * depth
    rows_g4 = None
    for l in reversed(range(depth)):
        w, p, s1, sm, s2 = saved[l]
        dx, dx_b, dg2, g2_in_t, g2_out, theirs = _ffn_backward(dx, dx_b, norm_ffn2[l][None], w["f2"], s2, rows_g4)
        chip_sums = None if rows_g4 is None else _add_sibling(rows_g4, theirs)
        dx, dx_b, gm, landed = _mixer_backward(dx, dx_b, p, w["mix"], sm, nb, seq, ta, chip_sums)
        if chip_sums is not None:
            layer_g[l + 1] = _sum_landed(chip_sums, landed)
        dx, dx_b, dg1, g1_in_t, g1_out, _ = _ffn_backward(dx, dx_b, norm_ffn1[l][None], w["f1"], s1)
        rows_g4 = _layer_grad_rows(dict(f1_in_t=g1_in_t, f1_out=g1_out, mix_in_t=gm["mix_in_t"],
                                        mix_out=gm["mix_out"], f2_in_t=g2_in_t, f2_out=g2_out)
                                   ).reshape(N_CHIPS, 2, LAYER_ROWS, D_MODEL)
        small[l] = dict(n1=dg1, nm=gm["norm"], n2=dg2, bf=gm["bf"], wbd=gm["wbd"], scale=gm["scale"], cw=gm["cw"])
    grad_x = dx.reshape(nb, seq, d)
    chip_sums = _add_sibling(rows_g4, _exchange_sibling(rows_g4, name="exchange_grads_sibling"))
    layer_g[0] = _sum_landed(chip_sums, _exchange_chips(chip_sums, name="exchange_grads_chips"))

    g_rows = jnp.stack(layer_g)
    o = 0
    pieces = {}
    for nm, n in (("f1i", FFN_ROWS), ("f1o", OUT_ROWS), ("mi", MIX_ROWS_PAD), ("mo", MO_ROWS), ("f2i", FFN_ROWS),
                  ("f2o", OUT_ROWS)):
        pieces[nm] = g_rows[:, o:o + n]
        o += n
    g_sharded = dict(
        w_ffn1_in=pieces["f1i"].transpose(0, 2, 1), w_ffn1_out=pieces["f1o"],
        w_mix_in=pieces["mi"][:, :MIX_ROWS].transpose(0, 2, 1), w_mix_out=pieces["mo"],
        w_ffn2_in=pieces["f2i"].transpose(0, 2, 1), w_ffn2_out=pieces["f2o"])

    def tile8(a):
        return jnp.pad(a, ((0, 8 - a.shape[0]), (0, D_MODEL - a.shape[1])))

    rows = []
    for l in range(depth):
        s = small[l]
        wp_rows = jnp.stack([s["wbd"][POOL_GROUP * g:POOL_GROUP * (g + 1), POOL_GROUP * g:POOL_GROUP * (g + 1)]
                             for g in range(4)]).reshape(16, D_MODEL)
        rows += [tile8(s["n1"]), tile8(s["nm"]), tile8(s["n2"]), tile8(s["bf"]), tile8(s["scale"]), tile8(s["cw"]),
                 wp_rows]
    rows += [tile8(g_norm_final), tile8(loss_part)]
    per_layer = 6 * 8 + 16
    small_sum = _sum_slots(_all_gather(jnp.concatenate(rows, axis=0), name="gather_small_grads"),
                           name="sum_small_grads")
    lay = small_sum[:depth * per_layer].reshape(depth, per_layer, D_MODEL)
    g_small = dict(
        norm_ffn1=lay[:, 0], norm_mix=lay[:, 8], norm_ffn2=lay[:, 16], b_forget=lay[:, 24, :N_HEADS],
        pool_scale=lay[:, 32, :D_POOL],
        conv_w=lax.dynamic_slice_in_dim(lay[:, 40:43, :D_CONV], my_id * cshard, cshard, axis=2),
        w_pool=lay[:, 48:64].reshape(depth, 4, POOL_GROUP, POOL_GROUP),
        norm_final=small_sum[depth * per_layer])
    loss = small_sum[depth * per_layer + 8, 0]

    given = dict(norm_ffn1=(norm_ffn1, m_norm_ffn1, v_norm_ffn1), w_ffn1_in=(w_ffn1_in, m_w_ffn1_in, v_w_ffn1_in),
                 w_ffn1_out=(w_ffn1_out, m_w_ffn1_out, v_w_ffn1_out), norm_mix=(norm_mix, m_norm_mix, v_norm_mix),
                 w_mix_in=(w_mix_in, m_w_mix_in, v_w_mix_in), b_forget=(b_forget, m_b_forget, v_b_forget),
                 w_pool=(w_pool, m_w_pool, v_w_pool), pool_scale=(pool_scale, m_pool_scale, v_pool_scale),
                 conv_w=(conv_w, m_conv_w, v_conv_w), w_mix_out=(w_mix_out, m_w_mix_out, v_w_mix_out),
                 norm_ffn2=(norm_ffn2, m_norm_ffn2, v_norm_ffn2), w_ffn2_in=(w_ffn2_in, m_w_ffn2_in, v_w_ffn2_in),
                 w_ffn2_out=(w_ffn2_out, m_w_ffn2_out, v_w_ffn2_out), norm_final=(norm_final, m_norm_final, v_norm_final))
    names = list(given)
    grads, deltas, new_m, new_v = {}, {}, {}, {}
    for nm in names:
        wv, mv, vv = given[nm]
        gv = (g_sharded[nm] if nm in g_sharded else g_small[nm]).reshape(wv.shape)
        shape2 = (-1, wv.shape[-1]) if wv.ndim > 1 else (1, wv.shape[0])
        dl, mn, vn = _adamw(wv.reshape(shape2), gv.reshape(shape2), mv.reshape(shape2), vv.reshape(shape2),
                            name="adamw_" + nm)
        grads[nm], deltas[nm], new_m[nm], new_v[nm] = gv, dl.reshape(wv.shape), mn.reshape(wv.shape), vn.reshape(wv.shape)
    return (loss, grad_x, *[grads[n] for n in names], *[deltas[n] for n in names],
            *[new_m[n] for n in names], *[new_v[n] for n in names])
```

```python
import functools

import jax
import jax.numpy as jnp
from jax import lax
from jax.experimental import pallas as pl
from jax.experimental.pallas import tpu as pltpu

F32 = jnp.float32
BF16 = jnp.bfloat16

D_MODEL = 1024
D_FF = 2816
HEAD_DIM = 64
N_HEADS = 8
N_PAIRS = N_HEADS // 2
D_ATTN = 512
D_POOL = 256
D_CONV = 256
POOL_WINDOWS = (2, 4, 8, 16)
POOL_GROUP = 64
D_IN = 2568
RMS_EPS = 1e-6
ADAM_LR, ADAM_B1, ADAM_B2, ADAM_EPS, ADAM_WD, ADAM_STEP = 0.001, 0.9, 0.999, 1e-08, 0.01, 10

N_DEV = 8
N_CHIPS = 4
LANES = 128
VMEM_BYTES_V7X = 64 * 1024 * 1024
VMEM_LIMIT_MAX = VMEM_BYTES_V7X - 8 * 1024 * 1024

F_HALF = D_FF // 2
D_QKV = 3 * D_ATTN
D_REST = D_POOL + 3 * D_CONV
D_INP = D_QKV + D_REST + LANES
MIX_ROWS = 321
MIX_ROWS_PAD = 336
FFN_ROWS = 704
OUT_ROWS = 352
MO_ROWS = 128
LAYER_ROWS = 2 * (FFN_ROWS + OUT_ROWS) + MIX_ROWS_PAD + MO_ROWS
NEG_BIG = -1e30
ATT_SCALE = HEAD_DIM ** -0.5
ATT_K = 2 * LANES
ATT_TILE = 256
ATT_PAIRS_FWD = 4
ATT_PAIRS_BWD = 4
MXU_COLS = 256


def _cparams(sem, vmem_bytes):
    limit = int(min(max(vmem_bytes, 16 * 1024 * 1024), VMEM_LIMIT_MAX))
    return pltpu.CompilerParams(dimension_semantics=sem, vmem_limit_bytes=limit)


def _nbytes(shape, dtype):
    n = 1
    for s in shape:
        n *= s
    return n * jnp.dtype(dtype).itemsize


def _pick(n, prefs):
    for p in prefs:
        if n % p == 0:
            return p
    return n


def _rmsnorm_fwd(x, g, name):
    t, d = x.shape
    tm = _pick(t, (512, 256, 128))

    def body(x_ref, g_ref, o_ref):
        xv = x_ref[...]
        r = lax.rsqrt(jnp.mean(xv * xv, axis=-1, keepdims=True) + RMS_EPS)
        o_ref[...] = ((xv * r) * g_ref[...]).astype(o_ref.dtype)

    return pl.pallas_call(
        body, grid=(t // tm,),
        in_specs=[pl.BlockSpec((tm, d), lambda i: (i, 0)), pl.BlockSpec((1, d), lambda i: (0, 0))],
        out_specs=pl.BlockSpec((tm, d), lambda i: (i, 0)),
        out_shape=jax.ShapeDtypeStruct((t, d), BF16), name=name,
        compiler_params=_cparams(("parallel",), 6 * tm * d * 4),
    )(x, g)


def _mm_nn(a, b, *, out_dtype, name, res=None, alpha=1.0, tn=None, next_gain=None):
    m, k = a.shape
    n = b.shape[1]
    tn = n if tn is None else tn
    tm = _pick(m, (512, 256, 128))
    with_res = res is not None
    with_norm = next_gain is not None
    assert not with_norm or tn == n

    def body(*refs):
        refs = list(refs)
        a_ref, b_ref = refs[:2]
        r_ref = refs[2] if with_res else None
        g_ref = refs[2 + with_res] if with_norm else None
        o_ref = refs[2 + with_res + with_norm]
        acc = jnp.dot(a_ref[...], b_ref[...], preferred_element_type=F32)
        if with_res:
            acc = r_ref[...] + alpha * acc
        o_ref[...] = acc.astype(o_ref.dtype)
        if with_norm:
            r = lax.rsqrt(jnp.mean(acc * acc, axis=-1, keepdims=True) + RMS_EPS)
            refs[-1][...] = ((acc * r) * g_ref[...]).astype(BF16)

    in_specs = [pl.BlockSpec((tm, k), lambda j, i: (i, 0)), pl.BlockSpec((k, tn), lambda j, i: (0, j))]
    args = [a, b]
    out_blk = pl.BlockSpec((tm, tn), lambda j, i: (i, j))
    out_specs, out_shape = [out_blk], [jax.ShapeDtypeStruct((m, n), out_dtype)]
    if with_res:
        in_specs.append(out_blk)
        args.append(res)
    if with_norm:
        in_specs.append(pl.BlockSpec((1, n), lambda j, i: (0, 0)))
        args.append(next_gain)
        out_specs.append(out_blk)
        out_shape.append(jax.ShapeDtypeStruct((m, n), BF16))
    vmem = 2 * (_nbytes((tm, k), BF16) + _nbytes((k, tn), BF16) + 4 * _nbytes((tm, tn), F32))
    outs = pl.pallas_call(
        body, grid=(n // tn, m // tm), in_specs=in_specs, out_specs=out_specs, out_shape=out_shape, name=name,
        compiler_params=_cparams(("parallel", "parallel"), vmem),
    )(*args)
    return outs if with_norm else outs[0]


def _mm_tn(a, b, *, name, alpha=1.0, tm=None, into=None):
    t, m = a.shape
    n = b.shape[1]
    tm = m if tm is None else tm
    tk = _pick(t, (2048, 1024, 512, 256, 128))
    nk = t // tk

    def body(a_ref, b_ref, *rest):
        o_ref = rest[-1]
        kk = pl.program_id(1)
        p = lax.dot_general(a_ref[...], b_ref[...], (((0,), (0,)), ((), ())), preferred_element_type=F32)
        if alpha != 1.0:
            p = alpha * p
        p = p.reshape(o_ref.shape)

        @pl.when(kk == 0)
        def _():
            o_ref[...] = p

        @pl.when(kk > 0)
        def _():
            o_ref[...] += p

    vmem = 2 * (_nbytes((tk, tm), BF16) + _nbytes((tk, n), BF16) + 2 * _nbytes((tm, n), F32))
    in_specs = [pl.BlockSpec((tk, tm), lambda i, kk: (kk, i)), pl.BlockSpec((tk, n), lambda i, kk: (kk, 0))]
    cp = _cparams(("parallel", "arbitrary"), vmem)
    if into is None:
        return pl.pallas_call(
            body, grid=(m // tm, nk), in_specs=in_specs, out_specs=pl.BlockSpec((tm, n), lambda i, kk: (i, 0)),
            out_shape=jax.ShapeDtypeStruct((m, n), F32), name=name, compiler_params=cp,
        )(a, b)
    buf, buf_shape, blk, index = into
    out_spec = pl.BlockSpec(blk, lambda i, kk: index(i))
    out_shape = jax.ShapeDtypeStruct(buf_shape, F32)
    if buf is None:
        return pl.pallas_call(body, grid=(m // tm, nk), in_specs=in_specs, out_specs=out_spec, out_shape=out_shape,
                              name=name + "_new", compiler_params=cp)(a, b)
    return pl.pallas_call(
        body, grid=(m // tm, nk), in_specs=in_specs + [pl.BlockSpec(memory_space=pl.ANY)], out_specs=out_spec,
        out_shape=out_shape, input_output_aliases={2: 0}, name=name + "_into", compiler_params=cp,
    )(a, b, buf)


def _sigmoid(v):
    return 1.0 / (1.0 + jnp.exp(-v))


def _col_chunks(n, width):
    return [(c, min(width, n - c)) for c in range(0, n, width)]


def _ffn_in(xn, w_t, name):
    t, d = xn.shape
    tm = _pick(t, (1024, 512, 256, 128))

    def body(x_ref, wg_ref, wu_ref, h_ref, pg_ref, pu_ref):
        xv = x_ref[...]
        for c0, cw in _col_chunks(F_HALF, MXU_COLS):
            cols = slice(c0, c0 + cw)
            g = _nt(xv, wg_ref[cols, :])
            u = _nt(xv, wu_ref[cols, :])
            s = _sigmoid(g)
            silu = g * s
            h_ref[:, cols] = (silu * u).astype(h_ref.dtype)
            pg_ref[:, cols] = (u * (s * (1.0 + g * (1.0 - s)))).astype(pg_ref.dtype)
            pu_ref[:, cols] = silu.astype(pu_ref.dtype)

    vmem = 2 * (_nbytes((tm, d), BF16) + 2 * _nbytes((d, F_HALF), BF16) + 4 * _nbytes((tm, D_FF), F32))
    out_blk = pl.BlockSpec((tm, F_HALF), lambda j, i: (i, j))
    sds = jax.ShapeDtypeStruct((t, D_FF), BF16)
    return pl.pallas_call(
        body, grid=(2, t // tm),
        in_specs=[pl.BlockSpec((tm, d), lambda j, i: (i, 0)), pl.BlockSpec((F_HALF, d), lambda j, i: (j, 0)),
                  pl.BlockSpec((F_HALF, d), lambda j, i: (2 + j, 0))],
        out_specs=[out_blk, out_blk, out_blk], out_shape=[sds, sds, sds], name=name,
        compiler_params=_cparams(("parallel", "parallel"), vmem),
    )(xn, w_t, w_t)


def _ffn_bwd_mid(dxo, w_out, pg, pu, name, carry_sibling=None):
    t, d = dxo.shape
    tm = _pick(t, (1024, 512, 256, 128))
    grid = (2, t // tm)

    def body(d_ref, w_ref, pg_ref, pu_ref, *rest):
        if carry_sibling is None:
            dg_ref, du_ref = rest
        else:
            g4_ref, dg_ref, du_ref, theirs_ref, send_sems, recv_sems = rest
            first_step, last_step = _grid_ends([pl.program_id(a) for a in range(2)], grid)

            @pl.when(first_step)
            def _():
                _sibling_start(g4_ref, theirs_ref, send_sems, recv_sems)

        dv = d_ref[...]
        for c0, cw in _col_chunks(F_HALF, MXU_COLS):
            cols = slice(c0, c0 + cw)
            dh = 0.5 * _nt(dv, w_ref[cols, :])
            dg_ref[:, cols] = (dh * pg_ref[:, cols].astype(F32)).astype(dg_ref.dtype)
            du_ref[:, cols] = (dh * pu_ref[:, cols].astype(F32)).astype(du_ref.dtype)

        if carry_sibling is not None:
            @pl.when(last_step)
            def _():
                _sibling_finish(g4_ref, theirs_ref, send_sems, recv_sems)

    vmem = 2 * (_nbytes((tm, d), BF16) + _nbytes((d, F_HALF), BF16) + 5 * _nbytes((tm, D_FF), F32))
    blk = pl.BlockSpec((tm, F_HALF), lambda j, i: (i, j))
    sds = jax.ShapeDtypeStruct((t, D_FF), BF16)
    in_specs = [pl.BlockSpec((tm, d), lambda j, i: (i, 0)), pl.BlockSpec((F_HALF, d), lambda j, i: (j, 0)), blk, blk]
    if carry_sibling is None:
        return pl.pallas_call(
            body, grid=grid, in_specs=in_specs, out_specs=[blk, blk], out_shape=[sds, sds], name=name,
            compiler_params=_cparams(("parallel", "parallel"), vmem),
        )(dxo, w_out, pg, pu)
    nchip, _, r, c = carry_sibling.shape
    any_spec = pl.BlockSpec(memory_space=pl.ANY)
    return pl.pallas_call(
        body, grid=grid, in_specs=in_specs + [any_spec], out_specs=[blk, blk, any_spec],
        out_shape=[sds, sds, jax.ShapeDtypeStruct((nchip, r, c), carry_sibling.dtype)],
        scratch_shapes=list(SIBLING_SEMS), name=name + "_exchange",
        compiler_params=_cparams(("arbitrary", "arbitrary"), vmem),
    )(dxo, w_out, pg, pu, carry_sibling)


def _dxn_norm_bwd(parts, b, x, g, dxo, name):
    t, d = x.shape
    k = parts[0].shape[1]
    n_parts = len(parts)
    tm = _pick(t, (256, 128))

    def body(*refs):
        a_refs, b_refs = refs[:n_parts], refs[n_parts:2 * n_parts]
        x_ref, g_ref, do_ref, dx_ref, dxb_ref, dg_ref = refs[2 * n_parts:]
        i = pl.program_id(0)
        dn = jnp.dot(a_refs[0][...], b_refs[0][...], preferred_element_type=F32)
        for a_ref, b_ref in zip(a_refs[1:], b_refs[1:]):
            dn = dn + jnp.dot(a_ref[...], b_ref[...], preferred_element_type=F32)
        xv = x_ref[...]
        r = lax.rsqrt(jnp.mean(xv * xv, axis=-1, keepdims=True) + RMS_EPS)
        xh = xv * r
        dgp = jnp.sum(dn * xh, axis=0, keepdims=True)
        dh = dn * g_ref[...]
        dx = do_ref[...] + r * (dh - xh * jnp.mean(dh * xh, axis=-1, keepdims=True))
        dx_ref[...] = dx
        dxb_ref[...] = dx.astype(dxb_ref.dtype)

        @pl.when(i == 0)
        def _():
            dg_ref[...] = dgp

        @pl.when(i > 0)
        def _():
            dg_ref[...] += dgp

    blk = pl.BlockSpec((tm, d), lambda i: (i, 0))
    row = pl.BlockSpec((1, d), lambda i: (0, 0))
    a_specs = [pl.BlockSpec((tm, k), lambda i: (i, 0)) for _ in parts]
    b_specs = [pl.BlockSpec((k, d), lambda i, kk=kk: (kk, 0)) for kk in range(n_parts)]
    vmem = 2 * n_parts * (_nbytes((tm, k), BF16) + _nbytes((k, d), BF16)) + 16 * tm * d * 4
    return pl.pallas_call(
        body, grid=(t // tm,), in_specs=a_specs + b_specs + [blk, row, blk], out_specs=[blk, blk, row],
        out_shape=[jax.ShapeDtypeStruct((t, d), F32), jax.ShapeDtypeStruct((t, d), BF16),
                   jax.ShapeDtypeStruct((1, d), F32)], name=name,
        compiler_params=_cparams(("arbitrary",), vmem),
    )(*parts, *([b] * n_parts), x, g, dxo)


def _final_loss_bwd(x, g, tgt):
    t, d = x.shape
    tm = _pick(t, (512, 256, 128))

    def body(x_ref, g_ref, t_ref, dx_ref, dxb_ref, dg_ref, loss_ref):
        i = pl.program_id(0)
        xv = x_ref[...]
        r = lax.rsqrt(jnp.mean(xv * xv, axis=-1, keepdims=True) + RMS_EPS)
        xh = xv * r
        gv = g_ref[...]
        err = xh * gv - t_ref[...]
        lp = 0.5 * jnp.sum(jnp.mean(err * err, axis=-1, keepdims=True), axis=0, keepdims=True)
        dy = err * (1.0 / d)
        dgp = jnp.sum(dy * xh, axis=0, keepdims=True)
        dh = dy * gv
        dx = r * (dh - xh * jnp.mean(dh * xh, axis=-1, keepdims=True))
        dx_ref[...] = dx
        dxb_ref[...] = dx.astype(dxb_ref.dtype)
        lpb = jnp.broadcast_to(lp, (1, LANES))

        @pl.when(i == 0)
        def _():
            dg_ref[...] = dgp
            loss_ref[...] = lpb

        @pl.when(i > 0)
        def _():
            dg_ref[...] += dgp
            loss_ref[...] += lpb

    blk = pl.BlockSpec((tm, d), lambda i: (i, 0))
    row = pl.BlockSpec((1, d), lambda i: (0, 0))
    return pl.pallas_call(
        body, grid=(t // tm,), in_specs=[blk, row, blk],
        out_specs=[blk, blk, row, pl.BlockSpec((1, LANES), lambda i: (0, 0))],
        out_shape=[jax.ShapeDtypeStruct((t, d), F32), jax.ShapeDtypeStruct((t, d), BF16),
                   jax.ShapeDtypeStruct((1, d), F32), jax.ShapeDtypeStruct((1, LANES), F32)], name="final_loss_bwd",
        compiler_params=_cparams(("arbitrary",), 16 * tm * d * 4),
    )(x, g, tgt)


def _seq_scan(v, seq, reverse):
    row = lax.broadcasted_iota(jnp.int32, v.shape, 0)
    k = 1
    while k < seq:
        if reverse:
            v = v + jnp.where(row < seq - k, pltpu.roll(v, seq - k, 0), 0.0)
        else:
            v = v + jnp.where(row >= k, pltpu.roll(v, k, 0), 0.0)
        k *= 2
    return v


def _log_sigmoid(v):
    return jnp.minimum(v, 0.0) - jnp.log(1.0 + jnp.exp(-jnp.abs(v)))


def _fox_prep(fl, bf, qkv, nb, seq):
    def body(f_ref, b_ref, q_ref, k_ref, v_ref, qa_ref, ka_ref, vm_ref):
        dsum = _seq_scan(_log_sigmoid(f_ref[...] + b_ref[...]), seq, False)
        d1 = dsum.astype(BF16).astype(F32)
        r1 = dsum - d1
        d2 = r1.astype(BF16).astype(F32)
        d3 = (r1 - d2).astype(BF16).astype(F32)
        lane = lax.broadcasted_iota(jnp.int32, (seq, LANES), 1)
        first = lane < HEAD_DIM
        l64 = jnp.where(first, lane, lane - HEAD_DIM)
        for p in range(N_PAIRS):
            def head_cols(a, p=p):
                return jnp.where(first, a[:, 2 * p:2 * p + 1], a[:, 2 * p + 1:2 * p + 2])

            e1, e2, e3 = head_cols(d1), head_cols(d2), head_cols(d3)
            aux_q = jnp.where(l64 == 0, e1, jnp.where(l64 == 1, e2, jnp.where(l64 == 2, e3,
                              jnp.where(l64 < 6, 1.0, 0.0)))).astype(BF16)
            aux_k = jnp.where(l64 < 3, 1.0, jnp.where(l64 == 3, -e1, jnp.where(l64 == 4, -e2,
                              jnp.where(l64 == 5, -e3, 0.0)))).astype(BF16)
            cols = slice(LANES * p, LANES * (p + 1))
            qs = q_ref[:, cols] * ATT_SCALE
            vp = v_ref[:, cols]
            zero = jnp.zeros_like(qs)
            qa_ref[0, p, 0, :, :LANES] = jnp.where(first, qs, zero)
            qa_ref[0, p, 0, :, LANES:] = jnp.where(first, aux_q, zero)
            qa_ref[0, p, 1, :, :LANES] = jnp.where(first, zero, qs)
            qa_ref[0, p, 1, :, LANES:] = jnp.where(first, zero, aux_q)
            ka_ref[0, p, :, :LANES] = k_ref[:, cols]
            ka_ref[0, p, :, LANES:] = aux_k
            vm_ref[0, p, 0] = jnp.where(first, vp, zero)
            vm_ref[0, p, 1] = jnp.where(first, zero, vp)

    def part(c):
        return pl.BlockSpec((seq, D_ATTN), lambda b, c=c: (b, c))

    return pl.pallas_call(
        body, grid=(nb,),
        in_specs=[pl.BlockSpec((seq, LANES), lambda b: (b, 0)), pl.BlockSpec((1, LANES), lambda b: (0, 0)),
                  part(0), part(1), part(2)],
        out_specs=[pl.BlockSpec((1, N_PAIRS, 2, seq, ATT_K), lambda b: (b, 0, 0, 0, 0)),
                   pl.BlockSpec((1, N_PAIRS, seq, ATT_K), lambda b: (b, 0, 0, 0)),
                   pl.BlockSpec((1, N_PAIRS, 2, seq, LANES), lambda b: (b, 0, 0, 0, 0))],
        out_shape=[jax.ShapeDtypeStruct((nb, N_PAIRS, 2, seq, ATT_K), BF16),
                   jax.ShapeDtypeStruct((nb, N_PAIRS, seq, ATT_K), BF16),
                   jax.ShapeDtypeStruct((nb, N_PAIRS, 2, seq, LANES), BF16)],
        name="fox_prep", compiler_params=_cparams(("parallel",), 48 * 1024 * 1024),
    )(fl, bf, qkv, qkv, qkv)


def _fox_prep_bwd(dd, fl, bf, seq):
    t = fl.shape[0]

    def body(d_ref, f_ref, b_ref, o_ref, db_ref):
        i = pl.program_id(0)
        dlog = _seq_scan(d_ref[...], seq, True)
        dfl = dlog * _sigmoid(-(f_ref[...] + b_ref[...]))
        o_ref[...] = dfl.astype(o_ref.dtype)
        dbp = jnp.sum(dfl, axis=0, keepdims=True)

        @pl.when(i == 0)
        def _():
            db_ref[...] = dbp

        @pl.when(i > 0)
        def _():
            db_ref[...] += dbp

    blk = pl.BlockSpec((seq, LANES), lambda b: (b, 0))
    row = pl.BlockSpec((1, LANES), lambda b: (0, 0))
    return pl.pallas_call(
        body, grid=(t // seq,), in_specs=[blk, blk, row], out_specs=[blk, row],
        out_shape=[jax.ShapeDtypeStruct((t, LANES), BF16), jax.ShapeDtypeStruct((1, LANES), F32)], name="fox_prep_bwd",
        compiler_params=_cparams(("arbitrary",), 24 * seq * LANES * 4),
    )(dd, fl, bf)


def _pair_rows(a, ta):
    lane = lax.broadcasted_iota(jnp.int32, (ta, LANES), 1)
    return jnp.where(lane < HEAD_DIM, a[:ta], a[ta:])


def _diag_mask(ta):
    r = lax.broadcasted_iota(jnp.int32, (2 * ta, ta), 0)
    c = lax.broadcasted_iota(jnp.int32, (2 * ta, ta), 1)
    return c <= jnp.where(r >= ta, r - ta, r)


def _nt(a, b):
    return lax.dot_general(a, b, (((1,), (1,)), ((), ())), preferred_element_type=F32)


def _tn(a, b):
    return lax.dot_general(a, b, (((0,), (0,)), ((), ())), preferred_element_type=F32)


def _grid_ends(ids, sizes):
    first = functools.reduce(jnp.logical_and, [i == 0 for i in ids])
    last = functools.reduce(jnp.logical_and, [i == n - 1 for i, n in zip(ids, sizes)])
    return first, last


def _fox_fwd(qa, ka, vm, nb, seq, ta, carry_gather=None):
    nq = seq // ta
    npp = ATT_PAIRS_FWD
    grid = (nb, N_PAIRS // npp, nq)

    def body(q_ref, k_ref, v_ref, *rest):
        if carry_gather is None:
            o_ref, lse_ref = rest
        else:
            x_ref, o_ref, lse_ref, gathered_ref, send_sems, recv_sems = rest
            first_step, last_step = _grid_ends([pl.program_id(a) for a in range(3)], grid)

            @pl.when(first_step)
            def _():
                _gather_start(x_ref, gathered_ref, send_sems, recv_sems)

        i = pl.program_id(2)
        q2s = [q_ref[0, pp].reshape(2 * ta, ATT_K) for pp in range(npp)]

        def step(j, carry, masked):
            rows = pl.ds(pl.multiple_of(j * ta, ta), ta)
            out = []
            for pp in range(npp):
                m, l, acc = carry[pp]
                s = _nt(q2s[pp], k_ref[0, pp, rows, :])
                if masked:
                    s = jnp.where(_diag_mask(ta), s, NEG_BIG)
                m_new = jnp.maximum(m, jnp.max(s, axis=-1, keepdims=True))
                p = jnp.exp(s - m_new)
                corr = jnp.exp(m - m_new)
                l = corr * l + jnp.sum(p, axis=-1, keepdims=True)
                pb = p.astype(BF16)
                pv = (jnp.dot(pb[:ta], v_ref[0, pp, 0, rows, :], preferred_element_type=F32)
                      + jnp.dot(pb[ta:], v_ref[0, pp, 1, rows, :], preferred_element_type=F32))
                out.append((m_new, l, _pair_rows(corr, ta) * acc + pv))
            return tuple(out)

        init = tuple((jnp.full((2 * ta, 1), NEG_BIG, F32), jnp.zeros((2 * ta, 1), F32),
                      jnp.zeros((ta, LANES), F32)) for _ in range(npp))
        carry = lax.fori_loop(0, i, functools.partial(step, masked=False), init)
        for pp, (m, l, acc) in enumerate(step(i, carry, True)):
            o_ref[:, LANES * pp:LANES * (pp + 1)] = (acc * _pair_rows(1.0 / l, ta)).astype(o_ref.dtype)
            lse = m + jnp.log(l)
            lse_ref[0, pp, 0] = lse[:ta]
            lse_ref[0, pp, 1] = lse[ta:]

        if carry_gather is not None:
            @pl.when(last_step)
            def _():
                _gather_finish(x_ref, gathered_ref, send_sems, recv_sems)

    vmem = (2 * npp * (_nbytes((seq, ATT_K), BF16) + 2 * _nbytes((seq, LANES), BF16)) + 24 * npp * ta * ta * 4
            + 8 * 1024 * 1024)
    in_specs = [pl.BlockSpec((1, npp, 2, ta, ATT_K), lambda b, g, i: (b, g, 0, i, 0)),
                pl.BlockSpec((1, npp, seq, ATT_K), lambda b, g, i: (b, g, 0, 0)),
                pl.BlockSpec((1, npp, 2, seq, LANES), lambda b, g, i: (b, g, 0, 0, 0))]
    out_specs = [pl.BlockSpec((ta, LANES * npp), lambda b, g, i: (b * nq + i, g)),
                 pl.BlockSpec((1, npp, 2, ta, 1), lambda b, g, i: (b, g, 0, i, 0))]
    out_shape = [jax.ShapeDtypeStruct((nb * seq, D_ATTN), BF16), jax.ShapeDtypeStruct((nb, N_PAIRS, 2, seq, 1), F32)]
    if carry_gather is None:
        return pl.pallas_call(
            body, grid=grid, in_specs=in_specs, out_specs=out_specs, out_shape=out_shape, name="fox_fwd",
            compiler_params=_cparams(("parallel", "parallel", "parallel"), vmem),
        )(qa, ka, vm)
    any_spec = pl.BlockSpec(memory_space=pl.ANY)
    return pl.pallas_call(
        body, grid=grid, in_specs=in_specs + [any_spec], out_specs=out_specs + [any_spec],
        out_shape=out_shape + [jax.ShapeDtypeStruct((N_DEV,) + carry_gather.shape, carry_gather.dtype)],
        scratch_shapes=list(GATHER_SEMS), name="fox_fwd_gather",
        compiler_params=_cparams(("arbitrary", "arbitrary", "arbitrary"), vmem),
    )(qa, ka, vm, carry_gather)


def _fox_bwd(qa, ka, vm, y, dy, lse, nb, seq, ta, carry_exchange=None):
    nq = seq // ta
    npp = ATT_PAIRS_BWD
    grid = (nb, N_PAIRS // npp, nq)

    def body(q_ref, k_ref, v_ref, o_ref, do_ref, lse_ref, *rest):
        if carry_exchange is None:
            dq_ref, dk_ref, dv_ref, rs_ref, cs_ref = rest
        else:
            t_ref, dq_ref, dk_ref, dv_ref, rs_ref, cs_ref, landed_ref, send_sems, recv_sems = rest
            first_step, last_step = _grid_ends([pl.program_id(a) for a in range(3)], grid)

            @pl.when(first_step)
            def _():
                _chips_start(t_ref, landed_ref, send_sems, recv_sems)

        i = pl.program_id(2)

        @pl.when(i == 0)
        def _():
            dk_ref[...] = jnp.zeros_like(dk_ref)
            dv_ref[...] = jnp.zeros_like(dv_ref)
            cs_ref[...] = jnp.zeros_like(cs_ref)

        first = lax.broadcasted_iota(jnp.int32, (ta, LANES), 1) < HEAD_DIM
        q2s, do2s, deltas, lses = [], [], [], []
        for pp in range(npp):
            cols = slice(LANES * pp, LANES * (pp + 1))
            q2s.append(q_ref[0, pp].reshape(2 * ta, ATT_K))
            do = do_ref[:, cols]
            doo = do * o_ref[:, cols].astype(F32)
            do2s.append(jnp.concatenate([jnp.where(first, do, 0.0), jnp.where(first, 0.0, do)], axis=0).astype(BF16))
            deltas.append(jnp.concatenate([jnp.sum(jnp.where(first, doo, 0.0), axis=-1, keepdims=True),
                                           jnp.sum(jnp.where(first, 0.0, doo), axis=-1, keepdims=True)], axis=0))
            lses.append(jnp.concatenate([lse_ref[0, pp, 0], lse_ref[0, pp, 1]], axis=0))

        def step(j, carry, masked):
            rows = pl.ds(pl.multiple_of(j * ta, ta), ta)
            out = []
            for pp in range(npp):
                dq_acc, rs_acc = carry[pp]
                cols = slice(LANES * pp, LANES * (pp + 1))
                ks = k_ref[0, pp, rows, :]
                s = _nt(q2s[pp], ks)
                if masked:
                    s = jnp.where(_diag_mask(ta), s, NEG_BIG)
                p = jnp.exp(s - lses[pp])
                dp = _nt(do2s[pp], v_ref[0, pp, 0, rows, :] + v_ref[0, pp, 1, rows, :])
                ds32 = p * (dp - deltas[pp])
                ds = ds32.astype(BF16)
                dk_ref[rows, cols] += _tn(ds, q2s[pp][:, :LANES])
                dv_ref[rows, cols] += _tn(p.astype(BF16), do2s[pp])
                cs_ref[0, pp, 0, j] += jnp.sum(ds32[:ta], axis=0, keepdims=True)
                cs_ref[0, pp, 1, j] += jnp.sum(ds32[ta:], axis=0, keepdims=True)
                out.append((dq_acc + jnp.dot(ds, ks[:, :LANES], preferred_element_type=F32),
                            rs_acc + jnp.sum(ds32, axis=-1, keepdims=True)))
            return tuple(out)

        init = tuple((jnp.zeros((2 * ta, LANES), F32), jnp.zeros((2 * ta, 1), F32)) for _ in range(npp))
        carry = lax.fori_loop(0, i, functools.partial(step, masked=False), init)
        for pp, (dq_acc, rs_acc) in enumerate(step(i, carry, True)):
            dq = jnp.where(first, dq_acc[:ta], dq_acc[ta:]) * ATT_SCALE
            dq_ref[:, LANES * pp:LANES * (pp + 1)] = dq.astype(dq_ref.dtype)
            rs_row = jnp.transpose(jnp.broadcast_to(rs_acc, (2 * ta, LANES)))[0:1]
            rs_ref[0, pp, 0, 0] = rs_row[:, :ta]
            rs_ref[0, pp, 1, 0] = rs_row[:, ta:]

        if carry_exchange is not None:
            @pl.when(last_step)
            def _():
                _chips_finish(t_ref, landed_ref, send_sems, recv_sems)

    vmem = (2 * npp * (_nbytes((seq, ATT_K), BF16) + 2 * _nbytes((seq, LANES), BF16) + 2 * _nbytes((seq, LANES), F32))
            + 32 * npp * ta * ta * 4 + 8 * 1024 * 1024)
    qblk = lambda b, g, i: (b * nq + i, g)
    acc_blk = pl.BlockSpec((seq, LANES * npp), lambda b, g, i: (b, g))
    in_specs = [pl.BlockSpec((1, npp, 2, ta, ATT_K), lambda b, g, i: (b, g, 0, i, 0)),
                pl.BlockSpec((1, npp, seq, ATT_K), lambda b, g, i: (b, g, 0, 0)),
                pl.BlockSpec((1, npp, 2, seq, LANES), lambda b, g, i: (b, g, 0, 0, 0)),
                pl.BlockSpec((ta, LANES * npp), qblk), pl.BlockSpec((ta, LANES * npp), qblk),
                pl.BlockSpec((1, npp, 2, ta, 1), lambda b, g, i: (b, g, 0, i, 0))]
    out_specs = [pl.BlockSpec((ta, LANES * npp), qblk), acc_blk, acc_blk,
                 pl.BlockSpec((1, npp, 2, 1, 1, ta), lambda b, g, i: (b, g, 0, i, 0, 0)),
                 pl.BlockSpec((1, npp, 2, nq, 1, ta), lambda b, g, i: (b, g, 0, 0, 0, 0))]
    sums = jax.ShapeDtypeStruct((nb, N_PAIRS, 2, nq, 1, ta), F32)
    out_shape = [jax.ShapeDtypeStruct((nb * seq, D_ATTN), BF16), jax.ShapeDtypeStruct((nb * seq, D_ATTN), F32),
                 jax.ShapeDtypeStruct((nb * seq, D_ATTN), F32), sums, sums]
    if carry_exchange is None:
        return pl.pallas_call(
            body, grid=grid, in_specs=in_specs, out_specs=out_specs, out_shape=out_shape, name="fox_bwd",
            compiler_params=_cparams(("parallel", "parallel", "arbitrary"), vmem),
        )(qa, ka, vm, y, dy, lse)
    any_spec = pl.BlockSpec(memory_space=pl.ANY)
    return pl.pallas_call(
        body, grid=grid, in_specs=in_specs + [any_spec], out_specs=out_specs + [any_spec],
        out_shape=out_shape + [jax.ShapeDtypeStruct(carry_exchange.shape, carry_exchange.dtype)],
        scratch_shapes=list(CHIPS_SEMS), name="fox_bwd_exchange",
        compiler_params=_cparams(("arbitrary", "arbitrary", "arbitrary"), vmem),
    )(qa, ka, vm, y, dy, lse, carry_exchange)


def _shift_down(a, k):
    row = lax.broadcasted_iota(jnp.int32, a.shape, 0)
    return jnp.where(row >= k, pltpu.roll(a, k, 0), 0.0)


def _shift_up(a, k):
    n = a.shape[0]
    row = lax.broadcasted_iota(jnp.int32, a.shape, 0)
    return jnp.where(row < n - k, pltpu.roll(a, n - k, 0), 0.0)


def _by_group(vals, shape):
    lane = lax.broadcasted_iota(jnp.int32, shape, 1)
    out = vals[-1]
    for gi in range(len(vals) - 2, -1, -1):
        out = jnp.where(lane < POOL_GROUP * (gi + 1), vals[gi], out)
    return out


def _pooled(u):
    s2 = u + _shift_down(u, 1)
    s4 = s2 + _shift_down(s2, 2)
    s8 = s4 + _shift_down(s4, 4)
    s16 = s8 + _shift_down(s8, 8)
    win = _by_group([s2, s4, s8, s16], u.shape)
    row = lax.broadcasted_iota(jnp.int32, u.shape, 0)
    wsize = _by_group([jnp.full(u.shape, w, jnp.int32) for w in POOL_WINDOWS], u.shape)
    inv = 1.0 / jnp.minimum(row + 1, wsize).astype(F32)
    return win * inv - u, inv


def _pool_fwd(rest, wbd, scale, seq):
    t = rest.shape[0]

    def body(u_ref, w_ref, s_ref, o_ref):
        pooled, _ = _pooled(u_ref[...])
        pw = jnp.dot(pooled.astype(BF16), w_ref[...], preferred_element_type=F32)
        o_ref[...] = (pw * s_ref[...]).astype(o_ref.dtype)

    blk = pl.BlockSpec((seq, D_POOL), lambda b: (b, 0))
    return pl.pallas_call(
        body, grid=(t // seq,),
        in_specs=[blk, pl.BlockSpec((D_POOL, D_POOL), lambda b: (0, 0)), pl.BlockSpec((1, D_POOL), lambda b: (0, 0))],
        out_specs=blk, out_shape=jax.ShapeDtypeStruct((t, D_POOL), BF16), name="pool_fwd",
        compiler_params=_cparams(("parallel",), 24 * seq * D_POOL * 4),
    )(rest, wbd, scale)


def _pool_bwd(rest, dy, wbd, wbd_t, scale, seq):
    t = rest.shape[0]

    def body(u_ref, dy_ref, w_ref, wt_ref, s_ref, du_ref, dw_ref, dsc_ref):
        i = pl.program_id(0)
        pooled, inv = _pooled(u_ref[...])
        pb = pooled.astype(BF16)
        pw = jnp.dot(pb, w_ref[...], preferred_element_type=F32)
        dyp = dy_ref[...]
        dsp = jnp.sum(dyp * pw, axis=0, keepdims=True)
        dpw = (dyp * s_ref[...]).astype(BF16)
        dwp = _tn(pb, dpw)
        dpooled = jnp.dot(dpw, wt_ref[...], preferred_element_type=F32)
        dwin = dpooled * inv
        t2 = dwin + _shift_up(dwin, 1)
        t4 = t2 + _shift_up(t2, 2)
        t8 = t4 + _shift_up(t4, 4)
        t16 = t8 + _shift_up(t8, 8)
        du_ref[...] = (_by_group([t2, t4, t8, t16], dwin.shape) - dpooled).astype(du_ref.dtype)

        @pl.when(i == 0)
        def _():
            dw_ref[...] = dwp
            dsc_ref[...] = dsp

        @pl.when(i > 0)
        def _():
            dw_ref[...] += dwp
            dsc_ref[...] += dsp

    blk = pl.BlockSpec((seq, D_POOL), lambda b: (b, 0))
    sq = pl.BlockSpec((D_POOL, D_POOL), lambda b: (0, 0))
    row = pl.BlockSpec((1, D_POOL), lambda b: (0, 0))
    return pl.pallas_call(
        body, grid=(t // seq,),
        in_specs=[blk, pl.BlockSpec((seq, D_POOL), lambda b: (b, 2)), sq, sq, row],
        out_specs=[blk, sq, row],
        out_shape=[jax.ShapeDtypeStruct((t, D_POOL), BF16), jax.ShapeDtypeStruct((D_POOL, D_POOL), F32),
                   jax.ShapeDtypeStruct((1, D_POOL), F32)], name="pool_bwd",
        compiler_params=_cparams(("arbitrary",), 40 * seq * D_POOL * 4),
    )(rest, dy, wbd, wbd_t, scale)


def _conv_fwd(rest, cw, seq):
    t = rest.shape[0]

    def body(cb_ref, cc_ref, ch_ref, w_ref, o_ref):
        u = cc_ref[...] * ch_ref[...]
        y = w_ref[0:1, :] * _shift_down(u, 2) + w_ref[1:2, :] * _shift_down(u, 1) + w_ref[2:3, :] * u
        o_ref[...] = (cb_ref[...] * y).astype(o_ref.dtype)

    def col(c):
        return pl.BlockSpec((seq, D_CONV), lambda b, c=c: (b, c))

    return pl.pallas_call(
        body, grid=(t // seq,), in_specs=[col(1), col(2), col(3), pl.BlockSpec((8, D_CONV), lambda b: (0, 0))],
        out_specs=pl.BlockSpec((seq, D_CONV), lambda b: (b, 0)),
        out_shape=jax.ShapeDtypeStruct((t, D_CONV), BF16), name="conv_fwd",
        compiler_params=_cparams(("parallel",), 24 * seq * D_CONV * 4),
    )(rest, rest, rest, cw)


def _conv_bwd(rest, dy, cw, seq):
    t = rest.shape[0]

    def body(cb_ref, cc_ref, ch_ref, dy_ref, w_ref, o_ref, dw_ref):
        i = pl.program_id(0)
        cc = cc_ref[...]
        ch = ch_ref[...]
        u = cc * ch
        u1 = _shift_down(u, 1)
        u2 = _shift_down(u, 2)
        y = w_ref[0:1, :] * u2 + w_ref[1:2, :] * u1 + w_ref[2:3, :] * u
        dyc = dy_ref[...]
        d2 = dyc * cb_ref[...]
        du = w_ref[0:1, :] * _shift_up(d2, 2) + w_ref[1:2, :] * _shift_up(d2, 1) + w_ref[2:3, :] * d2
        o_ref[:, 0:D_CONV] = (dyc * y).astype(o_ref.dtype)
        o_ref[:, D_CONV:2 * D_CONV] = (du * ch).astype(o_ref.dtype)
        o_ref[:, 2 * D_CONV:3 * D_CONV] = (du * cc).astype(o_ref.dtype)
        tap = lax.broadcasted_iota(jnp.int32, (8, D_CONV), 0)
        dwp = jnp.where(tap == 0, jnp.sum(d2 * u2, axis=0, keepdims=True),
                        jnp.where(tap == 1, jnp.sum(d2 * u1, axis=0, keepdims=True),
                                  jnp.where(tap == 2, jnp.sum(d2 * u, axis=0, keepdims=True), 0.0)))

        @pl.when(i == 0)
        def _():
            dw_ref[...] = dwp

        @pl.when(i > 0)
        def _():
            dw_ref[...] += dwp

    def col(c):
        return pl.BlockSpec((seq, D_CONV), lambda b, c=c: (b, c))

    taps = pl.BlockSpec((8, D_CONV), lambda b: (0, 0))
    return pl.pallas_call(
        body, grid=(t // seq,), in_specs=[col(1), col(2), col(3), col(3), taps],
        out_specs=[pl.BlockSpec((seq, 3 * D_CONV), lambda b: (b, 0)), taps],
        out_shape=[jax.ShapeDtypeStruct((t, 3 * D_CONV), BF16), jax.ShapeDtypeStruct((8, D_CONV), F32)],
        name="conv_bwd", compiler_params=_cparams(("arbitrary",), 48 * seq * D_CONV * 4),
    )(rest, rest, rest, dy, cw)


def _adamw(w, g, m, v, name):
    r, c = w.shape
    tr = _pick(r, (512, 352, 256, 128)) if r > 512 else r

    def body(w_ref, g_ref, m_ref, v_ref, d_ref, mo_ref, vo_ref):
        gv = g_ref[...]
        mn = ADAM_B1 * m_ref[...] + (1.0 - ADAM_B1) * gv
        vn = ADAM_B2 * v_ref[...] + (1.0 - ADAM_B2) * (gv * gv)
        m_hat = mn / (1.0 - ADAM_B1 ** ADAM_STEP)
        v_hat = vn / (1.0 - ADAM_B2 ** ADAM_STEP)
        d_ref[...] = -ADAM_LR * (m_hat / (jnp.sqrt(v_hat) + ADAM_EPS) + ADAM_WD * w_ref[...])
        mo_ref[...] = mn
        vo_ref[...] = vn

    blk = pl.BlockSpec((tr, c), lambda i: (i, 0))
    sds = jax.ShapeDtypeStruct((r, c), F32)
    return pl.pallas_call(
        body, grid=(r // tr,), in_specs=[blk] * 4, out_specs=[blk] * 3, out_shape=[sds] * 3, name=name,
        compiler_params=_cparams(("parallel",), 20 * tr * max(c, LANES) * 4),
    )(w, g, m, v)


def _sum_slots(a, name):
    ns, r, c = a.shape
    tr = _pick(r, (384, 368, 256, 184, 136, 128, 88, 8))

    def body(a_ref, o_ref):
        acc = a_ref[0].astype(F32)
        for s in range(1, ns):
            acc = acc + a_ref[s].astype(F32)
        o_ref[...] = acc

    return pl.pallas_call(
        body, grid=(r // tr,), in_specs=[pl.BlockSpec((ns, tr, c), lambda i: (0, i, 0))],
        out_specs=pl.BlockSpec((tr, c), lambda i: (i, 0)), out_shape=jax.ShapeDtypeStruct((r, c), F32), name=name,
        compiler_params=_cparams(("parallel",), 4 * (ns + 2) * tr * c * 4),
    )(a)


def _add_core_half(core, g4, theirs, out_dtype, name):
    ns, _, r, c = g4.shape
    tr = _pick(r, (384, 368, 256, 184, 136, 128, 88, 8))

    def body(core_ref, a_ref, b_ref, o_ref):
        o_ref[...] = (a_ref[0] + b_ref[...]).astype(o_ref.dtype)

    blk = pl.BlockSpec((1, tr, c), lambda s, i, core_ref: (s, i, 0))
    return pl.pallas_call(
        body,
        grid_spec=pltpu.PrefetchScalarGridSpec(
            num_scalar_prefetch=1, grid=(ns, r // tr),
            in_specs=[pl.BlockSpec((1, 1, tr, c), lambda s, i, core_ref: (s, core_ref[0], i, 0)), blk],
            out_specs=blk),
        out_shape=jax.ShapeDtypeStruct((ns, r, c), out_dtype), name=name,
        compiler_params=_cparams(("parallel", "parallel"), 10 * tr * c * 4),
    )(core, g4, theirs)


def _sum_chips(order, own, landed, name):
    ns, r, c = own.shape
    tr = _pick(r, (384, 368, 256, 184, 136, 128, 88, 8))

    def body(order_ref, a_ref, b1_ref, b2_ref, b3_ref, o_ref):
        o_ref[...] = ((a_ref[0].astype(F32) + b1_ref[0].astype(F32)) + b2_ref[0].astype(F32)) + b3_ref[0].astype(F32)

    def slot(k):
        return pl.BlockSpec((1, tr, c), lambda i, order_ref, k=k: (order_ref[k], i, 0))

    return pl.pallas_call(
        body,
        grid_spec=pltpu.PrefetchScalarGridSpec(
            num_scalar_prefetch=1, grid=(r // tr,), in_specs=[slot(0), slot(1), slot(2), slot(3)],
            out_specs=pl.BlockSpec((tr, c), lambda i, order_ref: (i, 0))),
        out_shape=jax.ShapeDtypeStruct((r, c), F32), name=name,
        compiler_params=_cparams(("parallel",), 16 * tr * c * 4),
    )(order, own, landed, landed, landed)


def _mesh_pos():
    return lax.axis_index("x"), lax.axis_index("y"), lax.axis_index("c")


def _all_gather(x, name):
    r, c = x.shape

    def body(x_ref, out_ref, send_sems, recv_sems):
        _gather_start(x_ref, out_ref, send_sems, recv_sems)
        _gather_finish(x_ref, out_ref, send_sems, recv_sems)

    gathered = pl.pallas_call(
        body, out_shape=jax.ShapeDtypeStruct((N_DEV, r, c), x.dtype),
        in_specs=[pl.BlockSpec(memory_space=pl.ANY)], out_specs=pl.BlockSpec(memory_space=pl.ANY),
        scratch_shapes=list(GATHER_SEMS), name=name,
    )(x)
    return _fill_own_slot(gathered, x)


GATHER_SEMS = (pltpu.SemaphoreType.DMA((7,)), pltpu.SemaphoreType.DMA((7,)))


def _fill_own_slot(gathered, x):
    mx, my, mc = _mesh_pos()
    return lax.dynamic_update_slice_in_dim(gathered, x[None], 4 * mx + 2 * my + mc, axis=0)


def _gather_copies(x_ref, out_ref, send_sems, recv_sems):
    mx, my, mc = _mesh_pos()
    me, sibling = (mx, my, mc), (mx, my, 1 - mc)
    chips = [(1 - mx, my), (mx, 1 - my), (1 - mx, 1 - my)]

    def slot(px, py, pc):
        return out_ref.at[4 * px + 2 * py + pc]

    def copy(k, block, to, src=None):
        return pltpu.make_async_remote_copy(
            src_ref=slot(*block) if src is None else src, dst_ref=slot(*block),
            send_sem=send_sems.at[k], recv_sem=recv_sems.at[k],
            device_id=to, device_id_type=pl.DeviceIdType.MESH)

    first = [copy(0, me, sibling, src=x_ref)]
    first += [copy(1 + j, me, (*chip, mc), src=x_ref) for j, chip in enumerate(chips)]
    passed = [copy(4 + j, (*chip, mc), sibling) for j, chip in enumerate(chips)]
    over_ici = [copy(1 + j, (*chip, mc), me) for j, chip in enumerate(chips)]
    over_d2d = [copy(0, sibling, me)] + [copy(4 + j, (*chip, 1 - mc), me) for j, chip in enumerate(chips)]
    return first, passed, over_ici, over_d2d


def _gather_start(x_ref, out_ref, send_sems, recv_sems):
    for cp in _gather_copies(x_ref, out_ref, send_sems, recv_sems)[0]:
        cp.start()


def _gather_finish(x_ref, out_ref, send_sems, recv_sems):
    first, passed, over_ici, over_d2d = _gather_copies(x_ref, out_ref, send_sems, recv_sems)
    for landed, relay in zip(over_ici, passed):
        landed.wait_recv()
        relay.start()
    for landed in over_d2d:
        landed.wait_recv()
    for cp in first + passed:
        cp.wait_send()


def _exchange_sibling(g4, name):
    nchip, _, r, c = g4.shape

    def body(g_ref, theirs_ref, send_sems, recv_sems):
        _sibling_start(g_ref, theirs_ref, send_sems, recv_sems)
        _sibling_finish(g_ref, theirs_ref, send_sems, recv_sems)

    any_spec = pl.BlockSpec(memory_space=pl.ANY)
    return pl.pallas_call(
        body, out_shape=jax.ShapeDtypeStruct((nchip, r, c), g4.dtype), in_specs=[any_spec], out_specs=any_spec,
        scratch_shapes=list(SIBLING_SEMS), name=name,
    )(g4)


SIBLING_SEMS = (pltpu.SemaphoreType.DMA((N_CHIPS,)), pltpu.SemaphoreType.DMA((N_CHIPS,)))


def _sibling_copies(g_ref, theirs_ref, send_sems, recv_sems):
    mx, my, mc = _mesh_pos()
    return [pltpu.make_async_remote_copy(
        src_ref=g_ref.at[chip, 1 - mc], dst_ref=theirs_ref.at[chip],
        send_sem=send_sems.at[chip], recv_sem=recv_sems.at[chip],
        device_id=(mx, my, 1 - mc), device_id_type=pl.DeviceIdType.MESH) for chip in range(N_CHIPS)]


def _sibling_start(g_ref, theirs_ref, send_sems, recv_sems):
    for cp in _sibling_copies(g_ref, theirs_ref, send_sems, recv_sems):
        cp.start()


def _sibling_finish(g_ref, theirs_ref, send_sems, recv_sems):
    copies = _sibling_copies(g_ref, theirs_ref, send_sems, recv_sems)
    for cp in copies:
        cp.wait_recv()
    for cp in copies:
        cp.wait_send()


def _exchange_chips(ts, name):
    def body(t_ref, out_ref, send_sems, recv_sems):
        _chips_start(t_ref, out_ref, send_sems, recv_sems)
        _chips_finish(t_ref, out_ref, send_sems, recv_sems)

    any_spec = pl.BlockSpec(memory_space=pl.ANY)
    return pl.pallas_call(
        body, out_shape=jax.ShapeDtypeStruct(ts.shape, ts.dtype), in_specs=[any_spec], out_specs=any_spec,
        scratch_shapes=list(CHIPS_SEMS), name=name,
    )(ts)


CHIPS_SEMS = (pltpu.SemaphoreType.DMA((N_CHIPS - 1,)), pltpu.SemaphoreType.DMA((N_CHIPS - 1,)))


def _chips_copies(t_ref, out_ref, send_sems, recv_sems):
    mx, my, mc = _mesh_pos()
    my_chip = 2 * mx + my
    copies = []
    for k in range(1, N_CHIPS):
        px = 1 - mx if k & 2 else mx
        py = 1 - my if k & 1 else my
        peer_chip = 2 * px + py

        def rdma(dst_slot, px=px, py=py, peer_chip=peer_chip, k=k):
            return pltpu.make_async_remote_copy(
                src_ref=t_ref.at[peer_chip], dst_ref=out_ref.at[dst_slot],
                send_sem=send_sems.at[k - 1], recv_sem=recv_sems.at[k - 1],
                device_id=(px, py, mc), device_id_type=pl.DeviceIdType.MESH)

        copies.append((rdma(my_chip), rdma(peer_chip)))
    return copies


def _chips_start(t_ref, out_ref, send_sems, recv_sems):
    for send, _ in _chips_copies(t_ref, out_ref, send_sems, recv_sems):
        send.start()


def _chips_finish(t_ref, out_ref, send_sems, recv_sems):
    copies = _chips_copies(t_ref, out_ref, send_sems, recv_sems)
    for _, landed in copies:
        landed.wait_recv()
    for send, _ in copies:
        send.wait_send()


def _add_sibling(g4, theirs):
    core = jnp.reshape(lax.axis_index("c"), (1,)).astype(jnp.int32)
    return _add_core_half(core, g4, theirs, BF16, name="add_sibling_grads")


def _sum_landed(chip_sums, landed):
    mx, my, _ = _mesh_pos()
    order = jnp.stack([2 * mx + my, 2 * (1 - mx) + my, 2 * mx + (1 - my), 2 * (1 - mx) + (1 - my)]).astype(jnp.int32)
    return _sum_chips(order, chip_sums, landed, name="sum_grads")


def _perm_mix_cols(wm):
    f0 = D_QKV
    f1 = f0 + N_HEADS
    pad = jnp.zeros((wm.shape[0], LANES - N_HEADS), wm.dtype)
    return jnp.concatenate([wm[:, :f0], wm[:, f1:], wm[:, f0:f1], pad], axis=1)


def _unperm_mix_rows(gt):
    f0 = D_QKV
    return jnp.concatenate([gt[:f0], gt[f0 + D_REST:f0 + D_REST + N_HEADS], gt[f0:f0 + D_REST]], axis=0)


def _pack_shards(parts, l, dtype):
    w1i, w1o, wmi, wmo, w2i, w2o = parts
    rows = [w1i[l].T, w1o[l], jnp.pad(wmi[l].T, ((0, MIX_ROWS_PAD - MIX_ROWS), (0, 0))), wmo[l], w2i[l].T, w2o[l]]
    return jnp.concatenate(rows, axis=0).astype(dtype)


def _layer_weights(wg):
    offs = {}
    o = 0
    for nm, n in (("f1i", FFN_ROWS), ("f1o", OUT_ROWS), ("mi", MIX_ROWS_PAD), ("mo", MO_ROWS), ("f2i", FFN_ROWS),
                  ("f2o", OUT_ROWS)):
        offs[nm] = (o, n)
        o += n

    def piece(nm, n_used=None):
        o, n = offs[nm]
        return wg[:, o:o + (n if n_used is None else n_used)]

    out = {}
    for tag in ("f1", "f2"):
        wi_t = piece(tag + "i").reshape(2 * D_FF, D_MODEL)
        wo = piece(tag + "o").reshape(D_FF, D_MODEL)
        out[tag] = dict(wi_t=wi_t, wo=wo)
    wm = _perm_mix_cols(piece("mi", MIX_ROWS).reshape(D_IN, D_MODEL).T)
    wm_t = wm.T
    wo = piece("mo").reshape(D_MODEL, D_MODEL)
    out["mix"] = dict(w_qkv=wm[:, :D_QKV], w_rest=wm[:, D_QKV:D_QKV + D_REST], w_f=wm[:, D_QKV + D_REST:],
                      wm_t=wm_t, wo=wo, wo_t=wo.T)
    return out


GRAD_AT = dict(f1i=0, f2i=FFN_ROWS, f1o=4 * OUT_ROWS, f2o=5 * OUT_ROWS, mi=6 * OUT_ROWS, mo=20 * MO_ROWS)
GRAD_ROWS = GRAD_AT["mo"] + MO_ROWS
GRAD_SHAPE = (N_CHIPS, 2, GRAD_ROWS, D_MODEL)


def _into_ffn_in(buf, tag, half):
    rb = GRAD_AT[tag] // FFN_ROWS
    return (buf, GRAD_SHAPE, (1, 2, FFN_ROWS, D_MODEL), lambda i: (2 * half + i, 0, rb, 0))


def _into_ffn_out(buf, tag):
    rb = GRAD_AT[tag] // OUT_ROWS
    return (buf, GRAD_SHAPE, (2, 2, OUT_ROWS, D_MODEL), lambda i: (i, 0, rb, 0))


def _into_mix_out(buf):
    rb = GRAD_AT["mo"] // MO_ROWS
    return (buf, GRAD_SHAPE, (N_CHIPS, 2, MO_ROWS, D_MODEL), lambda i: (0, 0, rb, 0))


def _put_mix_in(buf, g_in_t):
    gmi = _unperm_mix_rows(g_in_t).reshape(N_DEV, MIX_ROWS, D_MODEL)
    gmi = jnp.pad(gmi, ((0, 0), (0, OUT_ROWS - MIX_ROWS), (0, 0))).reshape(N_CHIPS, 2, OUT_ROWS, D_MODEL)
    return lax.dynamic_update_slice(buf, gmi, (0, 0, GRAD_AT["mi"], 0))


def _out_proj(a, w, x, alpha, next_gain, name):
    if next_gain is None:
        return _mm_nn(a, w, out_dtype=F32, res=x, alpha=alpha, name=name), None
    return _mm_nn(a, w, out_dtype=F32, res=x, alpha=alpha, next_gain=next_gain, name=name + "_norm")


def _ffn_forward(x, xn, w, next_gain):
    h, pg, pu = _ffn_in(xn, w["wi_t"], name="ffn_in")
    x_new, xn_next = _out_proj(h, w["wo"], x, 0.5, next_gain, "ffn_out")
    return x_new, xn_next, dict(x=x, xn=xn, h=h, pg=pg, pu=pu)


def _ffn_backward(dxo, dxo_b, gain, w, saved, gbuf, tag, carry_sibling=None):
    res = _ffn_bwd_mid(dxo_b, w["wo"], saved["pg"], saved["pu"], name="ffn_bwd_mid", carry_sibling=carry_sibling)
    dzg, dzu = res[:2]
    theirs = None if carry_sibling is None else res[2]
    gbuf = _mm_tn(saved["h"], dxo_b, alpha=0.5, tm=F_HALF, name="ffn_gw_out", into=_into_ffn_out(gbuf, tag + "o"))
    gbuf = _mm_tn(dzg, saved["xn"], tm=F_HALF, name="ffn_gw_in", into=_into_ffn_in(gbuf, tag + "i", 0))
    gbuf = _mm_tn(dzu, saved["xn"], tm=F_HALF, name="ffn_gw_in", into=_into_ffn_in(gbuf, tag + "i", 1))
    dx, dx_b, dg = _dxn_norm_bwd([dzg, dzu], w["wi_t"], saved["x"], gain, dxo, name="ffn_dxn_norm_bwd")
    return dx, dx_b, dg, gbuf, theirs


def _mixer_forward(x, xn, p, w, nb, seq, ta, next_gain, next_pack=None):
    qkv = _mm_nn(xn, w["w_qkv"], out_dtype=BF16, name="mix_qkv")
    rest = _mm_nn(xn, w["w_rest"], out_dtype=F32, name="mix_rest")
    fl = _mm_nn(xn, w["w_f"], out_dtype=F32, name="mix_f")
    qa, ka, vm = _fox_prep(fl, p["bf"], qkv, nb, seq)
    if next_pack is None:
        (y_attn, lse), next_gathered = _fox_fwd(qa, ka, vm, nb, seq, ta), None
    else:
        y_attn, lse, next_gathered = _fox_fwd(qa, ka, vm, nb, seq, ta, carry_gather=next_pack)
        next_gathered = _fill_own_slot(next_gathered, next_pack)
    y_pool = _pool_fwd(rest, p["wbd"], p["scale"], seq)
    y_conv = _conv_fwd(rest, p["cw"], seq)
    y = jnp.concatenate([y_attn, y_pool, y_conv], axis=1)
    x_new, xn_next = _out_proj(y, w["wo"], x, 1.0, next_gain, "mix_out")
    return x_new, xn_next, dict(x=x, xn=xn, qa=qa, ka=ka, vm=vm, rest=rest, fl=fl, lse=lse, y=y), next_gathered


def _mixer_backward(dxo, dxo_b, p, w, sv, nb, seq, ta, gbuf, pending=None):
    t = dxo.shape[0]
    dy = _mm_nn(dxo_b, w["wo_t"], out_dtype=F32, name="mix_dy")
    gbuf = _mm_tn(sv["y"], dxo_b, name="mix_gw_out", into=_into_mix_out(gbuf))
    res = _fox_bwd(sv["qa"], sv["ka"], sv["vm"], sv["y"], dy, sv["lse"], nb, seq, ta, carry_exchange=pending)
    dq, dk, dv, d_rows, d_cols = res[:5]
    landed = None if pending is None else res[5]
    ddh = (d_rows.reshape(nb, N_HEADS, seq) - d_cols.reshape(nb, N_HEADS, seq)).transpose(0, 2, 1)
    ddh = ddh.reshape(t, N_HEADS)
    dfl, dbf = _fox_prep_bwd(jnp.pad(ddh, ((0, 0), (0, LANES - N_HEADS))), sv["fl"], p["bf"], seq)
    dpool, dwbd, dscale = _pool_bwd(sv["rest"], dy, p["wbd"], p["wbd_t"], p["scale"], seq)
    dconv, dcw = _conv_bwd(sv["rest"], dy, p["cw"], seq)
    dproj = jnp.concatenate([dq, dk.astype(BF16), dv.astype(BF16), dpool, dconv, dfl], axis=1)
    gbuf = _put_mix_in(gbuf, _mm_tn(dproj, sv["xn"], tm=D_INP // 3, name="mix_gw_in"))
    dx, dx_b, dg = _dxn_norm_bwd([dproj], w["wm_t"], sv["x"], p["norm"], dxo, name="mix_dxn_norm_bwd")
    return dx, dx_b, dict(norm=dg, bf=dbf, wbd=dwbd, scale=dscale, cw=dcw), gbuf, landed


def _block_diag(wp):
    z = jnp.zeros((POOL_GROUP, POOL_GROUP), wp.dtype)
    return jnp.concatenate(
        [jnp.concatenate([wp[g] if g == r else z for g in range(4)], axis=1) for r in range(4)], axis=0)


def _row_pad(a, rows):
    a = a.reshape(-1, a.shape[-1])
    return jnp.pad(a, ((0, rows - a.shape[0]), (0, 0)))


def kernel(x, norm_ffn1, w_ffn1_in, w_ffn1_out, norm_mix, w_mix_in, b_forget, w_pool, pool_scale, conv_w, w_mix_out, norm_ffn2, w_ffn2_in, w_ffn2_out, norm_final, loss_target, m_norm_ffn1, m_w_ffn1_in, m_w_ffn1_out, m_norm_mix, m_w_mix_in, m_b_forget, m_w_pool, m_pool_scale, m_conv_w, m_w_mix_out, m_norm_ffn2, m_w_ffn2_in, m_w_ffn2_out, m_norm_final, v_norm_ffn1, v_w_ffn1_in, v_w_ffn1_out, v_norm_mix, v_w_mix_in, v_b_forget, v_w_pool, v_pool_scale, v_conv_w, v_w_mix_out, v_norm_ffn2, v_w_ffn2_in, v_w_ffn2_out, v_norm_final):
    nb, seq, d = x.shape
    depth = norm_ffn1.shape[0]
    t = nb * seq
    ta = _pick(seq, (ATT_TILE, 128))
    my_id = 4 * lax.axis_index("x") + 2 * lax.axis_index("y") + lax.axis_index("c")
    cshard = conv_w.shape[-1]

    shards = (w_ffn1_in, w_ffn1_out, w_mix_in, w_mix_out, w_ffn2_in, w_ffn2_out)
    wg = _all_gather(_pack_shards(shards, 0, BF16), name="gather_weights")
    cw_g = _all_gather(_row_pad(conv_w.reshape(depth * 3, cshard), 16).reshape(4, LANES), name="gather_conv_taps")
    cw_all = cw_g.reshape(N_DEV, 16, cshard)[:, :depth * 3].reshape(N_DEV, depth, 3, cshard)
    cw_all = cw_all.transpose(1, 2, 0, 3).reshape(depth, 3, D_CONV)

    xs = x.reshape(t, d)
    xn = _rmsnorm_fwd(xs, norm_ffn1[0][None], name="first_norm")
    saved = []
    for l in range(depth):
        w = _layer_weights(wg)
        wbd = _block_diag(w_pool[l])
        p = dict(norm=norm_mix[l][None], bf=jnp.pad(b_forget[l], (0, LANES - N_HEADS))[None],
                 wbd=wbd.astype(BF16), wbd_t=wbd.T.astype(BF16), scale=pool_scale[l][None],
                 cw=_row_pad(cw_all[l], 8))
        xs, xn, s1 = _ffn_forward(xs, xn, w["f1"], norm_mix[l][None])
        next_pack = _pack_shards(shards, l + 1, BF16) if l + 1 < depth else None
        xs, xn, sm, wg = _mixer_forward(xs, xn, p, w["mix"], nb, seq, ta, norm_ffn2[l][None], next_pack)
        xs, xn, s2 = _ffn_forward(xs, xn, w["f2"], norm_ffn1[l + 1][None] if l + 1 < depth else None)
        saved.append((w, p, s1, sm, s2))

    dx, dx_b, g_norm_final, loss_part = _final_loss_bwd(xs, norm_final[None], loss_target.reshape(t, d))
    layer_g = [None] * depth
    small = [None] * depth
    rows_g4 = None
    for l in reversed(range(depth)):
        w, p, s1, sm, s2 = saved[l]
        dx, dx_b, dg2, gbuf, theirs = _ffn_backward(dx, dx_b, norm_ffn2[l][None], w["f2"], s2, None, "f2", rows_g4)
        chip_sums = None if rows_g4 is None else _add_sibling(rows_g4, theirs)
        dx, dx_b, gm, gbuf, landed = _mixer_backward(dx, dx_b, p, w["mix"], sm, nb, seq, ta, gbuf, chip_sums)
        if chip_sums is not None:
            layer_g[l + 1] = _sum_landed(chip_sums, landed)
        dx, dx_b, dg1, rows_g4, _ = _ffn_backward(dx, dx_b, norm_ffn1[l][None], w["f1"], s1, gbuf, "f1")
        small[l] = dict(n1=dg1, nm=gm["norm"], n2=dg2, bf=gm["bf"], wbd=gm["wbd"], scale=gm["scale"], cw=gm["cw"])
    grad_x = dx.reshape(nb, seq, d)
    chip_sums = _add_sibling(rows_g4, _exchange_sibling(rows_g4, name="exchange_grads_sibling"))
    layer_g[0] = _sum_landed(chip_sums, _exchange_chips(chip_sums, name="exchange_grads_chips"))

    pieces = {}
    for nm, n in (("f1i", FFN_ROWS), ("f1o", OUT_ROWS), ("mi", MIX_ROWS), ("mo", MO_ROWS), ("f2i", FFN_ROWS),
                  ("f2o", OUT_ROWS)):
        pieces[nm] = jnp.stack([g[GRAD_AT[nm]:GRAD_AT[nm] + n] for g in layer_g])
    g_sharded = dict(
        w_ffn1_in=pieces["f1i"].transpose(0, 2, 1), w_ffn1_out=pieces["f1o"],
        w_mix_in=pieces["mi"].transpose(0, 2, 1), w_mix_out=pieces["mo"],
        w_ffn2_in=pieces["f2i"].transpose(0, 2, 1), w_ffn2_out=pieces["f2o"])

    def tile8(a):
        return jnp.pad(a, ((0, 8 - a.shape[0]), (0, D_MODEL - a.shape[1])))

    rows = []
    for l in range(depth):
        s = small[l]
        wp_rows = jnp.stack([s["wbd"][POOL_GROUP * g:POOL_GROUP * (g + 1), POOL_GROUP * g:POOL_GROUP * (g + 1)]
                             for g in range(4)]).reshape(16, D_MODEL)
        rows += [tile8(s["n1"]), tile8(s["nm"]), tile8(s["n2"]), tile8(s["bf"]), tile8(s["scale"]), tile8(s["cw"]),
                 wp_rows]
    rows += [tile8(g_norm_final), tile8(loss_part)]
    per_layer = 6 * 8 + 16
    small_sum = _sum_slots(_all_gather(jnp.concatenate(rows, axis=0), name="gather_small_grads"),
                           name="sum_small_grads")
    lay = small_sum[:depth * per_layer].reshape(depth, per_layer, D_MODEL)
    g_small = dict(
        norm_ffn1=lay[:, 0], norm_mix=lay[:, 8], norm_ffn2=lay[:, 16], b_forget=lay[:, 24, :N_HEADS],
        pool_scale=lay[:, 32, :D_POOL],
        conv_w=lax.dynamic_slice_in_dim(lay[:, 40:43, :D_CONV], my_id * cshard, cshard, axis=2),
        w_pool=lay[:, 48:64].reshape(depth, 4, POOL_GROUP, POOL_GROUP),
        norm_final=small_sum[depth * per_layer])
    loss = small_sum[depth * per_layer + 8, 0]

    given = dict(norm_ffn1=(norm_ffn1, m_norm_ffn1, v_norm_ffn1), w_ffn1_in=(w_ffn1_in, m_w_ffn1_in, v_w_ffn1_in),
                 w_ffn1_out=(w_ffn1_out, m_w_ffn1_out, v_w_ffn1_out), norm_mix=(norm_mix, m_norm_mix, v_norm_mix),
                 w_mix_in=(w_mix_in, m_w_mix_in, v_w_mix_in), b_forget=(b_forget, m_b_forget, v_b_forget),
                 w_pool=(w_pool, m_w_pool, v_w_pool), pool_scale=(pool_scale, m_pool_scale, v_pool_scale),
                 conv_w=(conv_w, m_conv_w, v_conv_w), w_mix_out=(w_mix_out, m_w_mix_out, v_w_mix_out),
                 norm_ffn2=(norm_ffn2, m_norm_ffn2, v_norm_ffn2), w_ffn2_in=(w_ffn2_in, m_w_ffn2_in, v_w_ffn2_in),
                 w_ffn2_out=(w_ffn2_out, m_w_ffn2_out, v_w_ffn2_out), norm_final=(norm_final, m_norm_final, v_norm_final))
    names = list(given)
    grads, deltas, new_m, new_v = {}, {}, {}, {}
    for nm in names:
        wv, mv, vv = given[nm]
        gv = (g_sharded[nm] if nm in g_sharded else g_small[nm]).reshape(wv.shape)
        shape2 = (-1, wv.shape[-1]) if wv.ndim > 1 else (1, wv.shape[0])
        dl, mn, vn = _adamw(wv.reshape(shape2), gv.reshape(shape2), mv.reshape(shape2), vv.reshape(shape2),
                            name="adamw_" + nm)
        grads[nm], deltas[nm], new_m[nm], new_v[nm] = gv, dl.reshape(wv.shape), mn.reshape(wv.shape), vn.reshape(wv.shape)
    return (loss, grad_x, *[grads[n] for n in names], *[deltas[n] for n in names],
            *[new_m[n] for n in names], *[new_v[n] for n in names])
```

```python
import functools

import jax
import jax.numpy as jnp
from jax import lax
from jax.experimental import pallas as pl
from jax.experimental.pallas import tpu as pltpu

F32 = jnp.float32
BF16 = jnp.bfloat16

D_MODEL = 1024
D_FF = 2816
HEAD_DIM = 64
N_HEADS = 8
N_PAIRS = N_HEADS // 2
D_ATTN = 512
D_POOL = 256
D_CONV = 256
POOL_WINDOWS = (2, 4, 8, 16)
POOL_GROUP = 64
D_IN = 2568
RMS_EPS = 1e-6
ADAM_LR, ADAM_B1, ADAM_B2, ADAM_EPS, ADAM_WD, ADAM_STEP = 0.001, 0.9, 0.999, 1e-08, 0.01, 10

N_DEV = 8
N_CHIPS = 4
LANES = 128
VMEM_BYTES_V7X = 64 * 1024 * 1024
VMEM_LIMIT_MAX = VMEM_BYTES_V7X - 8 * 1024 * 1024

F_HALF = D_FF // 2
D_QKV = 3 * D_ATTN
D_REST = D_POOL + 3 * D_CONV
D_INP = D_QKV + D_REST + LANES
MIX_ROWS = 321
MIX_ROWS_PAD = 336
FFN_ROWS = 704
OUT_ROWS = 352
MO_ROWS = 128
LAYER_ROWS = 2 * (FFN_ROWS + OUT_ROWS) + MIX_ROWS_PAD + MO_ROWS
NEG_BIG = -1e30
ATT_SCALE = HEAD_DIM ** -0.5
ATT_K = 2 * LANES
ATT_TILE = 256
ATT_PAIRS_FWD = 4
ATT_PAIRS_BWD = 4
MXU_COLS = 256


def _cparams(sem, vmem_bytes):
    limit = int(min(max(vmem_bytes, 16 * 1024 * 1024), VMEM_LIMIT_MAX))
    return pltpu.CompilerParams(dimension_semantics=sem, vmem_limit_bytes=limit)


def _nbytes(shape, dtype):
    n = 1
    for s in shape:
        n *= s
    return n * jnp.dtype(dtype).itemsize


def _pick(n, prefs):
    for p in prefs:
        if n % p == 0:
            return p
    return n


def _rmsnorm_fwd(x, g, name):
    t, d = x.shape
    tm = _pick(t, (512, 256, 128))

    def body(x_ref, g_ref, o_ref):
        xv = x_ref[...]
        r = lax.rsqrt(jnp.mean(xv * xv, axis=-1, keepdims=True) + RMS_EPS)
        o_ref[...] = ((xv * r) * g_ref[...]).astype(o_ref.dtype)

    return pl.pallas_call(
        body, grid=(t // tm,),
        in_specs=[pl.BlockSpec((tm, d), lambda i: (i, 0)), pl.BlockSpec((1, d), lambda i: (0, 0))],
        out_specs=pl.BlockSpec((tm, d), lambda i: (i, 0)),
        out_shape=jax.ShapeDtypeStruct((t, d), BF16), name=name,
        compiler_params=_cparams(("parallel",), 6 * tm * d * 4),
    )(x, g)


def _mm_nn(a, b, *, out_dtype, name, res=None, alpha=1.0, tn=None, next_gain=None):
    m, k = a.shape
    n = b.shape[1]
    tn = n if tn is None else tn
    tm = _pick(m, (512, 256, 128))
    with_res = res is not None
    with_norm = next_gain is not None
    assert not with_norm or tn == n

    def body(*refs):
        refs = list(refs)
        a_ref, b_ref = refs[:2]
        r_ref = refs[2] if with_res else None
        g_ref = refs[2 + with_res] if with_norm else None
        o_ref = refs[2 + with_res + with_norm]
        acc = jnp.dot(a_ref[...], b_ref[...], preferred_element_type=F32)
        if with_res:
            acc = r_ref[...] + alpha * acc
        o_ref[...] = acc.astype(o_ref.dtype)
        if with_norm:
            r = lax.rsqrt(jnp.mean(acc * acc, axis=-1, keepdims=True) + RMS_EPS)
            refs[-1][...] = ((acc * r) * g_ref[...]).astype(BF16)

    in_specs = [pl.BlockSpec((tm, k), lambda j, i: (i, 0)), pl.BlockSpec((k, tn), lambda j, i: (0, j))]
    args = [a, b]
    out_blk = pl.BlockSpec((tm, tn), lambda j, i: (i, j))
    out_specs, out_shape = [out_blk], [jax.ShapeDtypeStruct((m, n), out_dtype)]
    if with_res:
        in_specs.append(out_blk)
        args.append(res)
    if with_norm:
        in_specs.append(pl.BlockSpec((1, n), lambda j, i: (0, 0)))
        args.append(next_gain)
        out_specs.append(out_blk)
        out_shape.append(jax.ShapeDtypeStruct((m, n), BF16))
    vmem = 2 * (_nbytes((tm, k), BF16) + _nbytes((k, tn), BF16) + 4 * _nbytes((tm, tn), F32))
    outs = pl.pallas_call(
        body, grid=(n // tn, m // tm), in_specs=in_specs, out_specs=out_specs, out_shape=out_shape, name=name,
        compiler_params=_cparams(("parallel", "parallel"), vmem),
    )(*args)
    return outs if with_norm else outs[0]


def _mm_nt(a, b_t, *, out_dtype, name):
    m, k = a.shape
    n = b_t.shape[0]
    tm = _pick(m, (512, 256, 128))

    def body(a_ref, b_ref, o_ref):
        o_ref[...] = _nt(a_ref[...], b_ref[...]).astype(o_ref.dtype)

    vmem = 2 * (_nbytes((tm, k), BF16) + _nbytes((n, k), BF16) + 3 * _nbytes((tm, n), F32))
    return pl.pallas_call(
        body, grid=(m // tm,),
        in_specs=[pl.BlockSpec((tm, k), lambda i: (i, 0)), pl.BlockSpec((n, k), lambda i: (0, 0))],
        out_specs=pl.BlockSpec((tm, n), lambda i: (i, 0)), out_shape=jax.ShapeDtypeStruct((m, n), out_dtype),
        name=name, compiler_params=_cparams(("parallel",), vmem),
    )(a, b_t)


def _mix_proj(xn, wm_t):
    t, d = xn.shape
    tm = _pick(t, (512, 256, 128))

    def body(x_ref, w_ref, qkv_ref, rest_ref, f_ref):
        xv = x_ref[...]
        for c0, cw in _col_chunks(D_QKV, MXU_COLS):
            qkv_ref[:, c0:c0 + cw] = _nt(xv, w_ref[c0:c0 + cw, :]).astype(qkv_ref.dtype)
        for c0, cw in _col_chunks(D_REST, MXU_COLS):
            rest_ref[:, c0:c0 + cw] = _nt(xv, w_ref[D_QKV + c0:D_QKV + c0 + cw, :])
        f_ref[...] = _nt(xv, w_ref[D_QKV + D_REST:, :])

    def rows(n):
        return pl.BlockSpec((tm, n), lambda i: (i, 0))

    vmem = 2 * (_nbytes((tm, d), BF16) + _nbytes((D_INP, d), BF16) + 3 * _nbytes((tm, D_INP), F32))
    return pl.pallas_call(
        body, grid=(t // tm,), in_specs=[rows(d), pl.BlockSpec((D_INP, d), lambda i: (0, 0))],
        out_specs=[rows(D_QKV), rows(D_REST), rows(LANES)],
        out_shape=[jax.ShapeDtypeStruct((t, D_QKV), BF16), jax.ShapeDtypeStruct((t, D_REST), F32),
                   jax.ShapeDtypeStruct((t, LANES), F32)],
        name="mix_proj", compiler_params=_cparams(("parallel",), vmem),
    )(xn, wm_t)


def _mm_tn(a, b, *, name, alpha=1.0, tm=None, into=None):
    t, m = a.shape
    n = b.shape[1]
    tm = m if tm is None else tm
    tk = _pick(t, (2048, 1024, 512, 256, 128))
    nk = t // tk

    def body(a_ref, b_ref, *rest):
        o_ref = rest[-1]
        kk = pl.program_id(1)
        p = lax.dot_general(a_ref[...], b_ref[...], (((0,), (0,)), ((), ())), preferred_element_type=F32)
        if alpha != 1.0:
            p = alpha * p
        p = p.reshape(o_ref.shape)

        @pl.when(kk == 0)
        def _():
            o_ref[...] = p

        @pl.when(kk > 0)
        def _():
            o_ref[...] += p

    vmem = 2 * (_nbytes((tk, tm), BF16) + _nbytes((tk, n), BF16) + 2 * _nbytes((tm, n), F32))
    in_specs = [pl.BlockSpec((tk, tm), lambda i, kk: (kk, i)), pl.BlockSpec((tk, n), lambda i, kk: (kk, 0))]
    cp = _cparams(("parallel", "arbitrary"), vmem)
    if into is None:
        return pl.pallas_call(
            body, grid=(m // tm, nk), in_specs=in_specs, out_specs=pl.BlockSpec((tm, n), lambda i, kk: (i, 0)),
            out_shape=jax.ShapeDtypeStruct((m, n), F32), name=name, compiler_params=cp,
        )(a, b)
    buf, buf_shape, blk, index = into
    out_spec = pl.BlockSpec(blk, lambda i, kk: index(i))
    out_shape = jax.ShapeDtypeStruct(buf_shape, F32)
    if buf is None:
        return pl.pallas_call(body, grid=(m // tm, nk), in_specs=in_specs, out_specs=out_spec, out_shape=out_shape,
                              name=name + "_new", compiler_params=cp)(a, b)
    return pl.pallas_call(
        body, grid=(m // tm, nk), in_specs=in_specs + [pl.BlockSpec(memory_space=pl.ANY)], out_specs=out_spec,
        out_shape=out_shape, input_output_aliases={2: 0}, name=name + "_into", compiler_params=cp,
    )(a, b, buf)


def _sigmoid(v):
    return 1.0 / (1.0 + jnp.exp(-v))


def _col_chunks(n, width):
    return [(c, min(width, n - c)) for c in range(0, n, width)]


def _ffn_in(xn, w_t, name, carry_gather=None):
    t, d = xn.shape
    tm = _pick(t, (1024, 512, 256, 128))
    grid = (2, t // tm)

    def body(x_ref, wg_ref, wu_ref, *rest):
        if carry_gather is None:
            h_ref, pg_ref, pu_ref = rest
        else:
            blk_ref, h_ref, pg_ref, pu_ref, gathered_ref, send_sems, recv_sems = rest
            first_step, last_step = _grid_ends([pl.program_id(a) for a in range(2)], grid)

            @pl.when(first_step)
            def _():
                _gather_start(blk_ref, gathered_ref, send_sems, recv_sems)

        xv = x_ref[...]
        for c0, cw in _col_chunks(F_HALF, MXU_COLS):
            cols = slice(c0, c0 + cw)
            g = _nt(xv, wg_ref[cols, :])
            u = _nt(xv, wu_ref[cols, :])
            s = _sigmoid(g)
            silu = g * s
            h_ref[:, cols] = (silu * u).astype(h_ref.dtype)
            pg_ref[:, cols] = (u * (s * (1.0 + g * (1.0 - s)))).astype(pg_ref.dtype)
            pu_ref[:, cols] = silu.astype(pu_ref.dtype)

        if carry_gather is not None:
            @pl.when(last_step)
            def _():
                _gather_finish(blk_ref, gathered_ref, send_sems, recv_sems)

    vmem = 2 * (_nbytes((tm, d), BF16) + 2 * _nbytes((d, F_HALF), BF16) + 4 * _nbytes((tm, D_FF), F32))
    out_blk = pl.BlockSpec((tm, F_HALF), lambda j, i: (i, j))
    sds = jax.ShapeDtypeStruct((t, D_FF), BF16)
    in_specs = [pl.BlockSpec((tm, d), lambda j, i: (i, 0)), pl.BlockSpec((F_HALF, d), lambda j, i: (j, 0)),
                pl.BlockSpec((F_HALF, d), lambda j, i: (2 + j, 0))]
    if carry_gather is None:
        return pl.pallas_call(
            body, grid=grid, in_specs=in_specs, out_specs=[out_blk, out_blk, out_blk], out_shape=[sds, sds, sds],
            name=name, compiler_params=_cparams(("parallel", "parallel"), vmem),
        )(xn, w_t, w_t)
    any_spec = pl.BlockSpec(memory_space=pl.ANY)
    return pl.pallas_call(
        body, grid=grid, in_specs=in_specs + [any_spec], out_specs=[out_blk, out_blk, out_blk, any_spec],
        out_shape=[sds, sds, sds, jax.ShapeDtypeStruct((N_DEV,) + carry_gather.shape, carry_gather.dtype)],
        scratch_shapes=list(GATHER_SEMS), name=name + "_gather",
        compiler_params=_cparams(("arbitrary", "arbitrary"), vmem),
    )(xn, w_t, w_t, carry_gather)


def _ffn_bwd_mid(dxo, w_out, pg, pu, name, carry_sibling=None):
    t, d = dxo.shape
    tm = _pick(t, (1024, 512, 256, 128))
    grid = (2, t // tm)

    def body(d_ref, w_ref, pg_ref, pu_ref, *rest):
        if carry_sibling is None:
            dg_ref, du_ref = rest
        else:
            g4_ref, dg_ref, du_ref, theirs_ref, send_sems, recv_sems = rest
            first_step, last_step = _grid_ends([pl.program_id(a) for a in range(2)], grid)

            @pl.when(first_step)
            def _():
                _sibling_start(g4_ref, theirs_ref, send_sems, recv_sems)

        dv = d_ref[...]
        for c0, cw in _col_chunks(F_HALF, MXU_COLS):
            cols = slice(c0, c0 + cw)
            dh = 0.5 * _nt(dv, w_ref[cols, :])
            dg_ref[:, cols] = (dh * pg_ref[:, cols].astype(F32)).astype(dg_ref.dtype)
            du_ref[:, cols] = (dh * pu_ref[:, cols].astype(F32)).astype(du_ref.dtype)

        if carry_sibling is not None:
            @pl.when(last_step)
            def _():
                _sibling_finish(g4_ref, theirs_ref, send_sems, recv_sems)

    vmem = 2 * (_nbytes((tm, d), BF16) + _nbytes((d, F_HALF), BF16) + 5 * _nbytes((tm, D_FF), F32))
    blk = pl.BlockSpec((tm, F_HALF), lambda j, i: (i, j))
    sds = jax.ShapeDtypeStruct((t, D_FF), BF16)
    in_specs = [pl.BlockSpec((tm, d), lambda j, i: (i, 0)), pl.BlockSpec((F_HALF, d), lambda j, i: (j, 0)), blk, blk]
    if carry_sibling is None:
        return pl.pallas_call(
            body, grid=grid, in_specs=in_specs, out_specs=[blk, blk], out_shape=[sds, sds], name=name,
            compiler_params=_cparams(("parallel", "parallel"), vmem),
        )(dxo, w_out, pg, pu)
    nchip, _, r, c = carry_sibling.shape
    any_spec = pl.BlockSpec(memory_space=pl.ANY)
    return pl.pallas_call(
        body, grid=grid, in_specs=in_specs + [any_spec], out_specs=[blk, blk, any_spec],
        out_shape=[sds, sds, jax.ShapeDtypeStruct((nchip, r, c), carry_sibling.dtype)],
        scratch_shapes=list(SIBLING_SEMS), name=name + "_exchange",
        compiler_params=_cparams(("arbitrary", "arbitrary"), vmem),
    )(dxo, w_out, pg, pu, carry_sibling)


def _dxn_norm_bwd(parts, b, x, g, dxo, name):
    t, d = x.shape
    k = parts[0].shape[1]
    n_parts = len(parts)
    tm = _pick(t, (256, 128))

    def body(*refs):
        a_refs, b_refs = refs[:n_parts], refs[n_parts:2 * n_parts]
        x_ref, g_ref, do_ref, dx_ref, dxb_ref, dg_ref = refs[2 * n_parts:]
        i = pl.program_id(0)
        dn = jnp.dot(a_refs[0][...], b_refs[0][...], preferred_element_type=F32)
        for a_ref, b_ref in zip(a_refs[1:], b_refs[1:]):
            dn = dn + jnp.dot(a_ref[...], b_ref[...], preferred_element_type=F32)
        xv = x_ref[...]
        r = lax.rsqrt(jnp.mean(xv * xv, axis=-1, keepdims=True) + RMS_EPS)
        xh = xv * r
        dgp = jnp.sum(dn * xh, axis=0, keepdims=True)
        dh = dn * g_ref[...]
        dx = do_ref[...] + r * (dh - xh * jnp.mean(dh * xh, axis=-1, keepdims=True))
        dx_ref[...] = dx
        dxb_ref[...] = dx.astype(dxb_ref.dtype)

        @pl.when(i == 0)
        def _():
            dg_ref[...] = dgp

        @pl.when(i > 0)
        def _():
            dg_ref[...] += dgp

    blk = pl.BlockSpec((tm, d), lambda i: (i, 0))
    row = pl.BlockSpec((1, d), lambda i: (0, 0))
    a_specs = [pl.BlockSpec((tm, k), lambda i: (i, 0)) for _ in parts]
    b_specs = [pl.BlockSpec((k, d), lambda i, kk=kk: (kk, 0)) for kk in range(n_parts)]
    vmem = 2 * n_parts * (_nbytes((tm, k), BF16) + _nbytes((k, d), BF16)) + 16 * tm * d * 4
    return pl.pallas_call(
        body, grid=(t // tm,), in_specs=a_specs + b_specs + [blk, row, blk], out_specs=[blk, blk, row],
        out_shape=[jax.ShapeDtypeStruct((t, d), F32), jax.ShapeDtypeStruct((t, d), BF16),
                   jax.ShapeDtypeStruct((1, d), F32)], name=name,
        compiler_params=_cparams(("arbitrary",), vmem),
    )(*parts, *([b] * n_parts), x, g, dxo)


def _final_loss_bwd(x, g, tgt):
    t, d = x.shape
    tm = _pick(t, (512, 256, 128))

    def body(x_ref, g_ref, t_ref, dx_ref, dxb_ref, dg_ref, loss_ref):
        i = pl.program_id(0)
        xv = x_ref[...]
        r = lax.rsqrt(jnp.mean(xv * xv, axis=-1, keepdims=True) + RMS_EPS)
        xh = xv * r
        gv = g_ref[...]
        err = xh * gv - t_ref[...]
        lp = 0.5 * jnp.sum(jnp.mean(err * err, axis=-1, keepdims=True), axis=0, keepdims=True)
        dy = err * (1.0 / d)
        dgp = jnp.sum(dy * xh, axis=0, keepdims=True)
        dh = dy * gv
        dx = r * (dh - xh * jnp.mean(dh * xh, axis=-1, keepdims=True))
        dx_ref[...] = dx
        dxb_ref[...] = dx.astype(dxb_ref.dtype)
        lpb = jnp.broadcast_to(lp, (1, LANES))

        @pl.when(i == 0)
        def _():
            dg_ref[...] = dgp
            loss_ref[...] = lpb

        @pl.when(i > 0)
        def _():
            dg_ref[...] += dgp
            loss_ref[...] += lpb

    blk = pl.BlockSpec((tm, d), lambda i: (i, 0))
    row = pl.BlockSpec((1, d), lambda i: (0, 0))
    return pl.pallas_call(
        body, grid=(t // tm,), in_specs=[blk, row, blk],
        out_specs=[blk, blk, row, pl.BlockSpec((1, LANES), lambda i: (0, 0))],
        out_shape=[jax.ShapeDtypeStruct((t, d), F32), jax.ShapeDtypeStruct((t, d), BF16),
                   jax.ShapeDtypeStruct((1, d), F32), jax.ShapeDtypeStruct((1, LANES), F32)], name="final_loss_bwd",
        compiler_params=_cparams(("arbitrary",), 16 * tm * d * 4),
    )(x, g, tgt)


def _seq_scan(v, seq, reverse):
    row = lax.broadcasted_iota(jnp.int32, v.shape, 0)
    k = 1
    while k < seq:
        if reverse:
            v = v + jnp.where(row < seq - k, pltpu.roll(v, seq - k, 0), 0.0)
        else:
            v = v + jnp.where(row >= k, pltpu.roll(v, k, 0), 0.0)
        k *= 2
    return v


def _log_sigmoid(v):
    return jnp.minimum(v, 0.0) - jnp.log(1.0 + jnp.exp(-jnp.abs(v)))


def _fox_prep(fl, bf, qkv, nb, seq):
    def body(f_ref, b_ref, q_ref, k_ref, v_ref, qa_ref, ka_ref, vm_ref):
        dsum = _seq_scan(_log_sigmoid(f_ref[...] + b_ref[...]), seq, False)
        d1 = dsum.astype(BF16).astype(F32)
        r1 = dsum - d1
        d2 = r1.astype(BF16).astype(F32)
        d3 = (r1 - d2).astype(BF16).astype(F32)
        lane = lax.broadcasted_iota(jnp.int32, (seq, LANES), 1)
        first = lane < HEAD_DIM
        l64 = jnp.where(first, lane, lane - HEAD_DIM)
        for p in range(N_PAIRS):
            def head_cols(a, p=p):
                return jnp.where(first, a[:, 2 * p:2 * p + 1], a[:, 2 * p + 1:2 * p + 2])

            e1, e2, e3 = head_cols(d1), head_cols(d2), head_cols(d3)
            aux_q = jnp.where(l64 == 0, e1, jnp.where(l64 == 1, e2, jnp.where(l64 == 2, e3,
                              jnp.where(l64 < 6, 1.0, 0.0)))).astype(BF16)
            aux_k = jnp.where(l64 < 3, 1.0, jnp.where(l64 == 3, -e1, jnp.where(l64 == 4, -e2,
                              jnp.where(l64 == 5, -e3, 0.0)))).astype(BF16)
            cols = slice(LANES * p, LANES * (p + 1))
            qs = q_ref[:, cols] * ATT_SCALE
            vp = v_ref[:, cols]
            zero = jnp.zeros_like(qs)
            qa_ref[0, p, 0, :, :LANES] = jnp.where(first, qs, zero)
            qa_ref[0, p, 0, :, LANES:] = jnp.where(first, aux_q, zero)
            qa_ref[0, p, 1, :, :LANES] = jnp.where(first, zero, qs)
            qa_ref[0, p, 1, :, LANES:] = jnp.where(first, zero, aux_q)
            ka_ref[0, p, :, :LANES] = k_ref[:, cols]
            ka_ref[0, p, :, LANES:] = aux_k
            vm_ref[0, p, 0] = jnp.where(first, vp, zero)
            vm_ref[0, p, 1] = jnp.where(first, zero, vp)

    def part(c):
        return pl.BlockSpec((seq, D_ATTN), lambda b, c=c: (b, c))

    return pl.pallas_call(
        body, grid=(nb,),
        in_specs=[pl.BlockSpec((seq, LANES), lambda b: (b, 0)), pl.BlockSpec((1, LANES), lambda b: (0, 0)),
                  part(0), part(1), part(2)],
        out_specs=[pl.BlockSpec((1, N_PAIRS, 2, seq, ATT_K), lambda b: (b, 0, 0, 0, 0)),
                   pl.BlockSpec((1, N_PAIRS, seq, ATT_K), lambda b: (b, 0, 0, 0)),
                   pl.BlockSpec((1, N_PAIRS, 2, seq, LANES), lambda b: (b, 0, 0, 0, 0))],
        out_shape=[jax.ShapeDtypeStruct((nb, N_PAIRS, 2, seq, ATT_K), BF16),
                   jax.ShapeDtypeStruct((nb, N_PAIRS, seq, ATT_K), BF16),
                   jax.ShapeDtypeStruct((nb, N_PAIRS, 2, seq, LANES), BF16)],
        name="fox_prep", compiler_params=_cparams(("parallel",), 48 * 1024 * 1024),
    )(fl, bf, qkv, qkv, qkv)


def _fox_prep_bwd(dd, fl, bf, seq):
    t = fl.shape[0]

    def body(d_ref, f_ref, b_ref, o_ref, db_ref):
        i = pl.program_id(0)
        dlog = _seq_scan(d_ref[...], seq, True)
        dfl = dlog * _sigmoid(-(f_ref[...] + b_ref[...]))
        o_ref[...] = dfl.astype(o_ref.dtype)
        dbp = jnp.sum(dfl, axis=0, keepdims=True)

        @pl.when(i == 0)
        def _():
            db_ref[...] = dbp

        @pl.when(i > 0)
        def _():
            db_ref[...] += dbp

    blk = pl.BlockSpec((seq, LANES), lambda b: (b, 0))
    row = pl.BlockSpec((1, LANES), lambda b: (0, 0))
    return pl.pallas_call(
        body, grid=(t // seq,), in_specs=[blk, blk, row], out_specs=[blk, row],
        out_shape=[jax.ShapeDtypeStruct((t, LANES), BF16), jax.ShapeDtypeStruct((1, LANES), F32)], name="fox_prep_bwd",
        compiler_params=_cparams(("arbitrary",), 24 * seq * LANES * 4),
    )(dd, fl, bf)


def _pair_rows(a, ta):
    lane = lax.broadcasted_iota(jnp.int32, (ta, LANES), 1)
    return jnp.where(lane < HEAD_DIM, a[:ta], a[ta:])


def _diag_mask(ta):
    r = lax.broadcasted_iota(jnp.int32, (2 * ta, ta), 0)
    c = lax.broadcasted_iota(jnp.int32, (2 * ta, ta), 1)
    return c <= jnp.where(r >= ta, r - ta, r)


def _nt(a, b):
    return lax.dot_general(a, b, (((1,), (1,)), ((), ())), preferred_element_type=F32)


def _tn(a, b):
    return lax.dot_general(a, b, (((0,), (0,)), ((), ())), preferred_element_type=F32)


def _grid_ends(ids, sizes):
    first = functools.reduce(jnp.logical_and, [i == 0 for i in ids])
    last = functools.reduce(jnp.logical_and, [i == n - 1 for i, n in zip(ids, sizes)])
    return first, last


def _fox_fwd(qa, ka, vm, nb, seq, ta, carry_gather=None):
    nq = seq // ta
    npp = ATT_PAIRS_FWD
    grid = (nb, N_PAIRS // npp, nq)

    def body(q_ref, k_ref, v_ref, *rest):
        if carry_gather is None:
            o_ref, lse_ref = rest
        else:
            x_ref, o_ref, lse_ref, gathered_ref, send_sems, recv_sems = rest
            first_step, last_step = _grid_ends([pl.program_id(a) for a in range(3)], grid)

            @pl.when(first_step)
            def _():
                _gather_start(x_ref, gathered_ref, send_sems, recv_sems)

        i = pl.program_id(2)
        q2s = [q_ref[0, pp].reshape(2 * ta, ATT_K) for pp in range(npp)]

        def step(j, carry, masked):
            rows = pl.ds(pl.multiple_of(j * ta, ta), ta)
            out = []
            for pp in range(npp):
                m, l, acc = carry[pp]
                s = _nt(q2s[pp], k_ref[0, pp, rows, :])
                if masked:
                    s = jnp.where(_diag_mask(ta), s, NEG_BIG)
                m_new = jnp.maximum(m, jnp.max(s, axis=-1, keepdims=True))
                p = jnp.exp(s - m_new)
                corr = jnp.exp(m - m_new)
                l = corr * l + jnp.sum(p, axis=-1, keepdims=True)
                pb = p.astype(BF16)
                pv = (jnp.dot(pb[:ta], v_ref[0, pp, 0, rows, :], preferred_element_type=F32)
                      + jnp.dot(pb[ta:], v_ref[0, pp, 1, rows, :], preferred_element_type=F32))
                out.append((m_new, l, _pair_rows(corr, ta) * acc + pv))
            return tuple(out)

        init = tuple((jnp.full((2 * ta, 1), NEG_BIG, F32), jnp.zeros((2 * ta, 1), F32),
                      jnp.zeros((ta, LANES), F32)) for _ in range(npp))
        carry = lax.fori_loop(0, i, functools.partial(step, masked=False), init)
        for pp, (m, l, acc) in enumerate(step(i, carry, True)):
            o_ref[:, LANES * pp:LANES * (pp + 1)] = (acc * _pair_rows(1.0 / l, ta)).astype(o_ref.dtype)
            lse = m + jnp.log(l)
            lse_ref[0, pp, 0] = lse[:ta]
            lse_ref[0, pp, 1] = lse[ta:]

        if carry_gather is not None:
            @pl.when(last_step)
            def _():
                _gather_finish(x_ref, gathered_ref, send_sems, recv_sems)

    vmem = (2 * npp * (_nbytes((seq, ATT_K), BF16) + 2 * _nbytes((seq, LANES), BF16)) + 24 * npp * ta * ta * 4
            + 8 * 1024 * 1024)
    in_specs = [pl.BlockSpec((1, npp, 2, ta, ATT_K), lambda b, g, i: (b, g, 0, i, 0)),
                pl.BlockSpec((1, npp, seq, ATT_K), lambda b, g, i: (b, g, 0, 0)),
                pl.BlockSpec((1, npp, 2, seq, LANES), lambda b, g, i: (b, g, 0, 0, 0))]
    out_specs = [pl.BlockSpec((ta, LANES * npp), lambda b, g, i: (b * nq + i, g)),
                 pl.BlockSpec((1, npp, 2, ta, 1), lambda b, g, i: (b, g, 0, i, 0))]
    out_shape = [jax.ShapeDtypeStruct((nb * seq, D_ATTN), BF16), jax.ShapeDtypeStruct((nb, N_PAIRS, 2, seq, 1), F32)]
    if carry_gather is None:
        return pl.pallas_call(
            body, grid=grid, in_specs=in_specs, out_specs=out_specs, out_shape=out_shape, name="fox_fwd",
            compiler_params=_cparams(("parallel", "parallel", "parallel"), vmem),
        )(qa, ka, vm)
    any_spec = pl.BlockSpec(memory_space=pl.ANY)
    return pl.pallas_call(
        body, grid=grid, in_specs=in_specs + [any_spec], out_specs=out_specs + [any_spec],
        out_shape=out_shape + [jax.ShapeDtypeStruct((N_DEV,) + carry_gather.shape, carry_gather.dtype)],
        scratch_shapes=list(GATHER_SEMS), name="fox_fwd_gather",
        compiler_params=_cparams(("arbitrary", "arbitrary", "arbitrary"), vmem),
    )(qa, ka, vm, carry_gather)


def _fox_bwd(qa, ka, vm, y, dy, lse, nb, seq, ta, carry_exchange=None):
    nq = seq // ta
    npp = ATT_PAIRS_BWD
    grid = (nb, N_PAIRS // npp, nq)

    def body(q_ref, k_ref, v_ref, o_ref, do_ref, lse_ref, *rest):
        if carry_exchange is None:
            dq_ref, dk_ref, dv_ref, rs_ref, cs_ref = rest
        else:
            t_ref, dq_ref, dk_ref, dv_ref, rs_ref, cs_ref, landed_ref, send_sems, recv_sems = rest
            first_step, last_step = _grid_ends([pl.program_id(a) for a in range(3)], grid)

            @pl.when(first_step)
            def _():
                _chips_start(t_ref, landed_ref, send_sems, recv_sems)

        i = pl.program_id(2)

        @pl.when(i == 0)
        def _():
            dk_ref[...] = jnp.zeros_like(dk_ref)
            dv_ref[...] = jnp.zeros_like(dv_ref)
            cs_ref[...] = jnp.zeros_like(cs_ref)

        first = lax.broadcasted_iota(jnp.int32, (ta, LANES), 1) < HEAD_DIM
        q2s, do2s, deltas, lses = [], [], [], []
        for pp in range(npp):
            cols = slice(LANES * pp, LANES * (pp + 1))
            q2s.append(q_ref[0, pp].reshape(2 * ta, ATT_K))
            do = do_ref[:, cols]
            doo = do * o_ref[:, cols].astype(F32)
            do2s.append(jnp.concatenate([jnp.where(first, do, 0.0), jnp.where(first, 0.0, do)], axis=0).astype(BF16))
            deltas.append(jnp.concatenate([jnp.sum(jnp.where(first, doo, 0.0), axis=-1, keepdims=True),
                                           jnp.sum(jnp.where(first, 0.0, doo), axis=-1, keepdims=True)], axis=0))
            lses.append(jnp.concatenate([lse_ref[0, pp, 0], lse_ref[0, pp, 1]], axis=0))

        def step(j, carry, masked):
            rows = pl.ds(pl.multiple_of(j * ta, ta), ta)
            out = []
            for pp in range(npp):
                dq_acc, rs_acc = carry[pp]
                cols = slice(LANES * pp, LANES * (pp + 1))
                ks = k_ref[0, pp, rows, :]
                s = _nt(q2s[pp], ks)
                if masked:
                    s = jnp.where(_diag_mask(ta), s, NEG_BIG)
                p = jnp.exp(s - lses[pp])
                dp = _nt(do2s[pp], v_ref[0, pp, 0, rows, :] + v_ref[0, pp, 1, rows, :])
                ds32 = p * (dp - deltas[pp])
                ds = ds32.astype(BF16)
                dk_ref[rows, cols] += _tn(ds, q2s[pp][:, :LANES])
                dv_ref[rows, cols] += _tn(p.astype(BF16), do2s[pp])
                cs_ref[0, pp, 0, j] += jnp.sum(ds32[:ta], axis=0, keepdims=True)
                cs_ref[0, pp, 1, j] += jnp.sum(ds32[ta:], axis=0, keepdims=True)
                out.append((dq_acc + jnp.dot(ds, ks[:, :LANES], preferred_element_type=F32),
                            rs_acc + jnp.sum(ds32, axis=-1, keepdims=True)))
            return tuple(out)

        init = tuple((jnp.zeros((2 * ta, LANES), F32), jnp.zeros((2 * ta, 1), F32)) for _ in range(npp))
        carry = lax.fori_loop(0, i, functools.partial(step, masked=False), init)
        for pp, (dq_acc, rs_acc) in enumerate(step(i, carry, True)):
            dq = jnp.where(first, dq_acc[:ta], dq_acc[ta:]) * ATT_SCALE
            dq_ref[:, LANES * pp:LANES * (pp + 1)] = dq.astype(dq_ref.dtype)
            rs_row = jnp.transpose(jnp.broadcast_to(rs_acc, (2 * ta, LANES)))[0:1]
            rs_ref[0, pp, 0, 0] = rs_row[:, :ta]
            rs_ref[0, pp, 1, 0] = rs_row[:, ta:]

        if carry_exchange is not None:
            @pl.when(last_step)
            def _():
                _chips_finish(t_ref, landed_ref, send_sems, recv_sems)

    vmem = (2 * npp * (_nbytes((seq, ATT_K), BF16) + 2 * _nbytes((seq, LANES), BF16) + 2 * _nbytes((seq, LANES), F32))
            + 32 * npp * ta * ta * 4 + 8 * 1024 * 1024)
    qblk = lambda b, g, i: (b * nq + i, g)
    acc_blk = pl.BlockSpec((seq, LANES * npp), lambda b, g, i: (b, g))
    in_specs = [pl.BlockSpec((1, npp, 2, ta, ATT_K), lambda b, g, i: (b, g, 0, i, 0)),
                pl.BlockSpec((1, npp, seq, ATT_K), lambda b, g, i: (b, g, 0, 0)),
                pl.BlockSpec((1, npp, 2, seq, LANES), lambda b, g, i: (b, g, 0, 0, 0)),
                pl.BlockSpec((ta, LANES * npp), qblk), pl.BlockSpec((ta, LANES * npp), qblk),
                pl.BlockSpec((1, npp, 2, ta, 1), lambda b, g, i: (b, g, 0, i, 0))]
    out_specs = [pl.BlockSpec((ta, LANES * npp), qblk), acc_blk, acc_blk,
                 pl.BlockSpec((1, npp, 2, 1, 1, ta), lambda b, g, i: (b, g, 0, i, 0, 0)),
                 pl.BlockSpec((1, npp, 2, nq, 1, ta), lambda b, g, i: (b, g, 0, 0, 0, 0))]
    sums = jax.ShapeDtypeStruct((nb, N_PAIRS, 2, nq, 1, ta), F32)
    out_shape = [jax.ShapeDtypeStruct((nb * seq, D_ATTN), BF16), jax.ShapeDtypeStruct((nb * seq, D_ATTN), F32),
                 jax.ShapeDtypeStruct((nb * seq, D_ATTN), F32), sums, sums]
    if carry_exchange is None:
        return pl.pallas_call(
            body, grid=grid, in_specs=in_specs, out_specs=out_specs, out_shape=out_shape, name="fox_bwd",
            compiler_params=_cparams(("parallel", "parallel", "arbitrary"), vmem),
        )(qa, ka, vm, y, dy, lse)
    any_spec = pl.BlockSpec(memory_space=pl.ANY)
    return pl.pallas_call(
        body, grid=grid, in_specs=in_specs + [any_spec], out_specs=out_specs + [any_spec],
        out_shape=out_shape + [jax.ShapeDtypeStruct(carry_exchange.shape, carry_exchange.dtype)],
        scratch_shapes=list(CHIPS_SEMS), name="fox_bwd_exchange",
        compiler_params=_cparams(("arbitrary", "arbitrary", "arbitrary"), vmem),
    )(qa, ka, vm, y, dy, lse, carry_exchange)


def _shift_down(a, k):
    row = lax.broadcasted_iota(jnp.int32, a.shape, 0)
    return jnp.where(row >= k, pltpu.roll(a, k, 0), 0.0)


def _shift_up(a, k):
    n = a.shape[0]
    row = lax.broadcasted_iota(jnp.int32, a.shape, 0)
    return jnp.where(row < n - k, pltpu.roll(a, n - k, 0), 0.0)


def _by_group(vals, shape):
    lane = lax.broadcasted_iota(jnp.int32, shape, 1)
    out = vals[-1]
    for gi in range(len(vals) - 2, -1, -1):
        out = jnp.where(lane < POOL_GROUP * (gi + 1), vals[gi], out)
    return out


def _pooled(u):
    s2 = u + _shift_down(u, 1)
    s4 = s2 + _shift_down(s2, 2)
    s8 = s4 + _shift_down(s4, 4)
    s16 = s8 + _shift_down(s8, 8)
    win = _by_group([s2, s4, s8, s16], u.shape)
    row = lax.broadcasted_iota(jnp.int32, u.shape, 0)
    wsize = _by_group([jnp.full(u.shape, w, jnp.int32) for w in POOL_WINDOWS], u.shape)
    inv = 1.0 / jnp.minimum(row + 1, wsize).astype(F32)
    return win * inv - u, inv


def _pool_fwd(rest, wbd, scale, seq):
    t = rest.shape[0]

    def body(u_ref, w_ref, s_ref, o_ref):
        pooled, _ = _pooled(u_ref[...])
        pw = jnp.dot(pooled.astype(BF16), w_ref[...], preferred_element_type=F32)
        o_ref[...] = (pw * s_ref[...]).astype(o_ref.dtype)

    blk = pl.BlockSpec((seq, D_POOL), lambda b: (b, 0))
    return pl.pallas_call(
        body, grid=(t // seq,),
        in_specs=[blk, pl.BlockSpec((D_POOL, D_POOL), lambda b: (0, 0)), pl.BlockSpec((1, D_POOL), lambda b: (0, 0))],
        out_specs=blk, out_shape=jax.ShapeDtypeStruct((t, D_POOL), BF16), name="pool_fwd",
        compiler_params=_cparams(("parallel",), 24 * seq * D_POOL * 4),
    )(rest, wbd, scale)


def _pool_bwd(rest, dy, wbd, wbd_t, scale, seq):
    t = rest.shape[0]

    def body(u_ref, dy_ref, w_ref, wt_ref, s_ref, du_ref, dw_ref, dsc_ref):
        i = pl.program_id(0)
        pooled, inv = _pooled(u_ref[...])
        pb = pooled.astype(BF16)
        pw = jnp.dot(pb, w_ref[...], preferred_element_type=F32)
        dyp = dy_ref[...]
        dsp = jnp.sum(dyp * pw, axis=0, keepdims=True)
        dpw = (dyp * s_ref[...]).astype(BF16)
        dwp = _tn(pb, dpw)
        dpooled = jnp.dot(dpw, wt_ref[...], preferred_element_type=F32)
        dwin = dpooled * inv
        t2 = dwin + _shift_up(dwin, 1)
        t4 = t2 + _shift_up(t2, 2)
        t8 = t4 + _shift_up(t4, 4)
        t16 = t8 + _shift_up(t8, 8)
        du_ref[...] = (_by_group([t2, t4, t8, t16], dwin.shape) - dpooled).astype(du_ref.dtype)

        @pl.when(i == 0)
        def _():
            dw_ref[...] = dwp
            dsc_ref[...] = dsp

        @pl.when(i > 0)
        def _():
            dw_ref[...] += dwp
            dsc_ref[...] += dsp

    blk = pl.BlockSpec((seq, D_POOL), lambda b: (b, 0))
    sq = pl.BlockSpec((D_POOL, D_POOL), lambda b: (0, 0))
    row = pl.BlockSpec((1, D_POOL), lambda b: (0, 0))
    return pl.pallas_call(
        body, grid=(t // seq,),
        in_specs=[blk, pl.BlockSpec((seq, D_POOL), lambda b: (b, 2)), sq, sq, row],
        out_specs=[blk, sq, row],
        out_shape=[jax.ShapeDtypeStruct((t, D_POOL), BF16), jax.ShapeDtypeStruct((D_POOL, D_POOL), F32),
                   jax.ShapeDtypeStruct((1, D_POOL), F32)], name="pool_bwd",
        compiler_params=_cparams(("arbitrary",), 40 * seq * D_POOL * 4),
    )(rest, dy, wbd, wbd_t, scale)


def _conv_fwd(rest, cw, seq):
    t = rest.shape[0]

    def body(cb_ref, cc_ref, ch_ref, w_ref, o_ref):
        u = cc_ref[...] * ch_ref[...]
        y = w_ref[0:1, :] * _shift_down(u, 2) + w_ref[1:2, :] * _shift_down(u, 1) + w_ref[2:3, :] * u
        o_ref[...] = (cb_ref[...] * y).astype(o_ref.dtype)

    def col(c):
        return pl.BlockSpec((seq, D_CONV), lambda b, c=c: (b, c))

    return pl.pallas_call(
        body, grid=(t // seq,), in_specs=[col(1), col(2), col(3), pl.BlockSpec((8, D_CONV), lambda b: (0, 0))],
        out_specs=pl.BlockSpec((seq, D_CONV), lambda b: (b, 0)),
        out_shape=jax.ShapeDtypeStruct((t, D_CONV), BF16), name="conv_fwd",
        compiler_params=_cparams(("parallel",), 24 * seq * D_CONV * 4),
    )(rest, rest, rest, cw)


def _conv_bwd(rest, dy, cw, seq):
    t = rest.shape[0]

    def body(cb_ref, cc_ref, ch_ref, dy_ref, w_ref, o_ref, dw_ref):
        i = pl.program_id(0)
        cc = cc_ref[...]
        ch = ch_ref[...]
        u = cc * ch
        u1 = _shift_down(u, 1)
        u2 = _shift_down(u, 2)
        y = w_ref[0:1, :] * u2 + w_ref[1:2, :] * u1 + w_ref[2:3, :] * u
        dyc = dy_ref[...]
        d2 = dyc * cb_ref[...]
        du = w_ref[0:1, :] * _shift_up(d2, 2) + w_ref[1:2, :] * _shift_up(d2, 1) + w_ref[2:3, :] * d2
        o_ref[:, 0:D_CONV] = (dyc * y).astype(o_ref.dtype)
        o_ref[:, D_CONV:2 * D_CONV] = (du * ch).astype(o_ref.dtype)
        o_ref[:, 2 * D_CONV:3 * D_CONV] = (du * cc).astype(o_ref.dtype)
        tap = lax.broadcasted_iota(jnp.int32, (8, D_CONV), 0)
        dwp = jnp.where(tap == 0, jnp.sum(d2 * u2, axis=0, keepdims=True),
                        jnp.where(tap == 1, jnp.sum(d2 * u1, axis=0, keepdims=True),
                                  jnp.where(tap == 2, jnp.sum(d2 * u, axis=0, keepdims=True), 0.0)))

        @pl.when(i == 0)
        def _():
            dw_ref[...] = dwp

        @pl.when(i > 0)
        def _():
            dw_ref[...] += dwp

    def col(c):
        return pl.BlockSpec((seq, D_CONV), lambda b, c=c: (b, c))

    taps = pl.BlockSpec((8, D_CONV), lambda b: (0, 0))
    return pl.pallas_call(
        body, grid=(t // seq,), in_specs=[col(1), col(2), col(3), col(3), taps],
        out_specs=[pl.BlockSpec((seq, 3 * D_CONV), lambda b: (b, 0)), taps],
        out_shape=[jax.ShapeDtypeStruct((t, 3 * D_CONV), BF16), jax.ShapeDtypeStruct((8, D_CONV), F32)],
        name="conv_bwd", compiler_params=_cparams(("arbitrary",), 48 * seq * D_CONV * 4),
    )(rest, rest, rest, dy, cw)


def _adamw(w, g, m, v, name):
    r, c = w.shape
    tr = _pick(r, (512, 352, 256, 128)) if r > 512 else r

    def body(w_ref, g_ref, m_ref, v_ref, d_ref, mo_ref, vo_ref):
        gv = g_ref[...]
        mn = ADAM_B1 * m_ref[...] + (1.0 - ADAM_B1) * gv
        vn = ADAM_B2 * v_ref[...] + (1.0 - ADAM_B2) * (gv * gv)
        m_hat = mn / (1.0 - ADAM_B1 ** ADAM_STEP)
        v_hat = vn / (1.0 - ADAM_B2 ** ADAM_STEP)
        d_ref[...] = -ADAM_LR * (m_hat / (jnp.sqrt(v_hat) + ADAM_EPS) + ADAM_WD * w_ref[...])
        mo_ref[...] = mn
        vo_ref[...] = vn

    blk = pl.BlockSpec((tr, c), lambda i: (i, 0))
    sds = jax.ShapeDtypeStruct((r, c), F32)
    return pl.pallas_call(
        body, grid=(r // tr,), in_specs=[blk] * 4, out_specs=[blk] * 3, out_shape=[sds] * 3, name=name,
        compiler_params=_cparams(("parallel",), 20 * tr * max(c, LANES) * 4),
    )(w, g, m, v)


def _sum_slots(a, name):
    ns, r, c = a.shape
    tr = _pick(r, (384, 368, 256, 184, 136, 128, 88, 8))

    def body(a_ref, o_ref):
        acc = a_ref[0].astype(F32)
        for s in range(1, ns):
            acc = acc + a_ref[s].astype(F32)
        o_ref[...] = acc

    return pl.pallas_call(
        body, grid=(r // tr,), in_specs=[pl.BlockSpec((ns, tr, c), lambda i: (0, i, 0))],
        out_specs=pl.BlockSpec((tr, c), lambda i: (i, 0)), out_shape=jax.ShapeDtypeStruct((r, c), F32), name=name,
        compiler_params=_cparams(("parallel",), 4 * (ns + 2) * tr * c * 4),
    )(a)


def _add_core_half(core, g4, theirs, out_dtype, name):
    ns, _, r, c = g4.shape
    tr = _pick(r, (384, 368, 256, 184, 136, 128, 88, 8))

    def body(core_ref, a_ref, b_ref, o_ref):
        o_ref[...] = (a_ref[0] + b_ref[...]).astype(o_ref.dtype)

    blk = pl.BlockSpec((1, tr, c), lambda s, i, core_ref: (s, i, 0))
    return pl.pallas_call(
        body,
        grid_spec=pltpu.PrefetchScalarGridSpec(
            num_scalar_prefetch=1, grid=(ns, r // tr),
            in_specs=[pl.BlockSpec((1, 1, tr, c), lambda s, i, core_ref: (s, core_ref[0], i, 0)), blk],
            out_specs=blk),
        out_shape=jax.ShapeDtypeStruct((ns, r, c), out_dtype), name=name,
        compiler_params=_cparams(("parallel", "parallel"), 10 * tr * c * 4),
    )(core, g4, theirs)


def _sum_chips(order, own, landed, name):
    ns, r, c = own.shape
    tr = _pick(r, (384, 368, 256, 184, 136, 128, 88, 8))

    def body(order_ref, a_ref, b1_ref, b2_ref, b3_ref, o_ref):
        o_ref[...] = ((a_ref[0].astype(F32) + b1_ref[0].astype(F32)) + b2_ref[0].astype(F32)) + b3_ref[0].astype(F32)

    def slot(k):
        return pl.BlockSpec((1, tr, c), lambda i, order_ref, k=k: (order_ref[k], i, 0))

    return pl.pallas_call(
        body,
        grid_spec=pltpu.PrefetchScalarGridSpec(
            num_scalar_prefetch=1, grid=(r // tr,), in_specs=[slot(0), slot(1), slot(2), slot(3)],
            out_specs=pl.BlockSpec((tr, c), lambda i, order_ref: (i, 0))),
        out_shape=jax.ShapeDtypeStruct((r, c), F32), name=name,
        compiler_params=_cparams(("parallel",), 16 * tr * c * 4),
    )(order, own, landed, landed, landed)


def _mesh_pos():
    return lax.axis_index("x"), lax.axis_index("y"), lax.axis_index("c")


def _all_gather(x, name):
    r, c = x.shape

    def body(x_ref, out_ref, send_sems, recv_sems):
        _gather_start(x_ref, out_ref, send_sems, recv_sems)
        _gather_finish(x_ref, out_ref, send_sems, recv_sems)

    gathered = pl.pallas_call(
        body, out_shape=jax.ShapeDtypeStruct((N_DEV, r, c), x.dtype),
        in_specs=[pl.BlockSpec(memory_space=pl.ANY)], out_specs=pl.BlockSpec(memory_space=pl.ANY),
        scratch_shapes=list(GATHER_SEMS), name=name,
    )(x)
    return _fill_own_slot(gathered, x)


GATHER_SEMS = (pltpu.SemaphoreType.DMA((7,)), pltpu.SemaphoreType.DMA((7,)))


def _fill_own_slot(gathered, x):
    mx, my, mc = _mesh_pos()
    return lax.dynamic_update_slice_in_dim(gathered, x[None], 4 * mx + 2 * my + mc, axis=0)


def _gather_copies(x_ref, out_ref, send_sems, recv_sems):
    mx, my, mc = _mesh_pos()
    me, sibling = (mx, my, mc), (mx, my, 1 - mc)
    chips = [(1 - mx, my), (mx, 1 - my), (1 - mx, 1 - my)]

    def slot(px, py, pc):
        return out_ref.at[4 * px + 2 * py + pc]

    def copy(k, block, to, src=None):
        return pltpu.make_async_remote_copy(
            src_ref=slot(*block) if src is None else src, dst_ref=slot(*block),
            send_sem=send_sems.at[k], recv_sem=recv_sems.at[k],
            device_id=to, device_id_type=pl.DeviceIdType.MESH)

    first = [copy(0, me, sibling, src=x_ref)]
    first += [copy(1 + j, me, (*chip, mc), src=x_ref) for j, chip in enumerate(chips)]
    passed = [copy(4 + j, (*chip, mc), sibling) for j, chip in enumerate(chips)]
    over_ici = [copy(1 + j, (*chip, mc), me) for j, chip in enumerate(chips)]
    over_d2d = [copy(0, sibling, me)] + [copy(4 + j, (*chip, 1 - mc), me) for j, chip in enumerate(chips)]
    return first, passed, over_ici, over_d2d


def _gather_start(x_ref, out_ref, send_sems, recv_sems):
    for cp in _gather_copies(x_ref, out_ref, send_sems, recv_sems)[0]:
        cp.start()


def _gather_finish(x_ref, out_ref, send_sems, recv_sems):
    first, passed, over_ici, over_d2d = _gather_copies(x_ref, out_ref, send_sems, recv_sems)
    for landed, relay in zip(over_ici, passed):
        landed.wait_recv()
        relay.start()
    for landed in over_d2d:
        landed.wait_recv()
    for cp in first + passed:
        cp.wait_send()


def _exchange_sibling(g4, name):
    nchip, _, r, c = g4.shape

    def body(g_ref, theirs_ref, send_sems, recv_sems):
        _sibling_start(g_ref, theirs_ref, send_sems, recv_sems)
        _sibling_finish(g_ref, theirs_ref, send_sems, recv_sems)

    any_spec = pl.BlockSpec(memory_space=pl.ANY)
    return pl.pallas_call(
        body, out_shape=jax.ShapeDtypeStruct((nchip, r, c), g4.dtype), in_specs=[any_spec], out_specs=any_spec,
        scratch_shapes=list(SIBLING_SEMS), name=name,
    )(g4)


SIBLING_SEMS = (pltpu.SemaphoreType.DMA((N_CHIPS,)), pltpu.SemaphoreType.DMA((N_CHIPS,)))


def _sibling_copies(g_ref, theirs_ref, send_sems, recv_sems):
    mx, my, mc = _mesh_pos()
    return [pltpu.make_async_remote_copy(
        src_ref=g_ref.at[chip, 1 - mc], dst_ref=theirs_ref.at[chip],
        send_sem=send_sems.at[chip], recv_sem=recv_sems.at[chip],
        device_id=(mx, my, 1 - mc), device_id_type=pl.DeviceIdType.MESH) for chip in range(N_CHIPS)]


def _sibling_start(g_ref, theirs_ref, send_sems, recv_sems):
    for cp in _sibling_copies(g_ref, theirs_ref, send_sems, recv_sems):
        cp.start()


def _sibling_finish(g_ref, theirs_ref, send_sems, recv_sems):
    copies = _sibling_copies(g_ref, theirs_ref, send_sems, recv_sems)
    for cp in copies:
        cp.wait_recv()
    for cp in copies:
        cp.wait_send()


def _exchange_chips(ts, name):
    def body(t_ref, out_ref, send_sems, recv_sems):
        _chips_start(t_ref, out_ref, send_sems, recv_sems)
        _chips_finish(t_ref, out_ref, send_sems, recv_sems)

    any_spec = pl.BlockSpec(memory_space=pl.ANY)
    return pl.pallas_call(
        body, out_shape=jax.ShapeDtypeStruct(ts.shape, ts.dtype), in_specs=[any_spec], out_specs=any_spec,
        scratch_shapes=list(CHIPS_SEMS), name=name,
    )(ts)


CHIPS_SEMS = (pltpu.SemaphoreType.DMA((N_CHIPS - 1,)), pltpu.SemaphoreType.DMA((N_CHIPS - 1,)))


def _chips_copies(t_ref, out_ref, send_sems, recv_sems):
    mx, my, mc = _mesh_pos()
    my_chip = 2 * mx + my
    copies = []
    for k in range(1, N_CHIPS):
        px = 1 - mx if k & 2 else mx
        py = 1 - my if k & 1 else my
        peer_chip = 2 * px + py

        def rdma(dst_slot, px=px, py=py, peer_chip=peer_chip, k=k):
            return pltpu.make_async_remote_copy(
                src_ref=t_ref.at[peer_chip], dst_ref=out_ref.at[dst_slot],
                send_sem=send_sems.at[k - 1], recv_sem=recv_sems.at[k - 1],
                device_id=(px, py, mc), device_id_type=pl.DeviceIdType.MESH)

        copies.append((rdma(my_chip), rdma(peer_chip)))
    return copies


def _chips_start(t_ref, out_ref, send_sems, recv_sems):
    for send, _ in _chips_copies(t_ref, out_ref, send_sems, recv_sems):
        send.start()


def _chips_finish(t_ref, out_ref, send_sems, recv_sems):
    copies = _chips_copies(t_ref, out_ref, send_sems, recv_sems)
    for _, landed in copies:
        landed.wait_recv()
    for send, _ in copies:
        send.wait_send()


def _add_sibling(g4, theirs):
    core = jnp.reshape(lax.axis_index("c"), (1,)).astype(jnp.int32)
    return _add_core_half(core, g4, theirs, BF16, name="add_sibling_grads")


def _sum_landed(chip_sums, landed):
    mx, my, _ = _mesh_pos()
    order = jnp.stack([2 * mx + my, 2 * (1 - mx) + my, 2 * mx + (1 - my), 2 * (1 - mx) + (1 - my)]).astype(jnp.int32)
    return _sum_chips(order, chip_sums, landed, name="sum_grads")


def _perm_mix_rows(wt):
    f0 = D_QKV
    f1 = f0 + N_HEADS
    return jnp.concatenate([wt[:f0], wt[f1:], jnp.pad(wt[f0:f1], ((0, LANES - N_HEADS), (0, 0)))], axis=0)


def _unperm_mix_rows(gt):
    f0 = D_QKV
    return jnp.concatenate([gt[:f0], gt[f0 + D_REST:f0 + D_REST + N_HEADS], gt[f0:f0 + D_REST]], axis=0)


def _pack_shards(parts, l, dtype):
    w1i, w1o, wmi, wmo, w2i, w2o = parts
    rows = [w1i[l].T, w1o[l], jnp.pad(wmi[l].T, ((0, MIX_ROWS_PAD - MIX_ROWS), (0, 0))), wmo[l], w2i[l].T, w2o[l]]
    return jnp.concatenate(rows, axis=0).astype(dtype)


PACK_HEAD = FFN_ROWS + OUT_ROWS


def _ffn_weights(wg, o):
    return dict(wi_t=wg[:, o:o + FFN_ROWS].reshape(2 * D_FF, D_MODEL),
                wo=wg[:, o + FFN_ROWS:o + FFN_ROWS + OUT_ROWS].reshape(D_FF, D_MODEL))


def _tail_weights(wg):
    mix = dict(wm_t=_perm_mix_rows(wg[:, :MIX_ROWS].reshape(D_IN, D_MODEL)),
               wo=wg[:, MIX_ROWS_PAD:MIX_ROWS_PAD + MO_ROWS].reshape(D_MODEL, D_MODEL))
    return mix, _ffn_weights(wg, MIX_ROWS_PAD + MO_ROWS)


GRAD_AT = dict(f1i=0, f2i=FFN_ROWS, f1o=4 * OUT_ROWS, f2o=5 * OUT_ROWS, mi=6 * OUT_ROWS, mo=20 * MO_ROWS)
GRAD_ROWS = GRAD_AT["mo"] + MO_ROWS
GRAD_SHAPE = (N_CHIPS, 2, GRAD_ROWS, D_MODEL)


def _into_ffn_in(buf, tag, half):
    rb = GRAD_AT[tag] // FFN_ROWS
    return (buf, GRAD_SHAPE, (1, 2, FFN_ROWS, D_MODEL), lambda i: (2 * half + i, 0, rb, 0))


def _into_ffn_out(buf, tag):
    rb = GRAD_AT[tag] // OUT_ROWS
    return (buf, GRAD_SHAPE, (2, 2, OUT_ROWS, D_MODEL), lambda i: (i, 0, rb, 0))


def _into_mix_out(buf):
    rb = GRAD_AT["mo"] // MO_ROWS
    return (buf, GRAD_SHAPE, (N_CHIPS, 2, MO_ROWS, D_MODEL), lambda i: (0, 0, rb, 0))


def _put_mix_in(buf, g_in_t):
    gmi = _unperm_mix_rows(g_in_t).reshape(N_DEV, MIX_ROWS, D_MODEL)
    gmi = jnp.pad(gmi, ((0, 0), (0, OUT_ROWS - MIX_ROWS), (0, 0))).reshape(N_CHIPS, 2, OUT_ROWS, D_MODEL)
    return lax.dynamic_update_slice(buf, gmi, (0, 0, GRAD_AT["mi"], 0))


def _out_proj(a, w, x, alpha, next_gain, name):
    if next_gain is None:
        return _mm_nn(a, w, out_dtype=F32, res=x, alpha=alpha, name=name), None
    return _mm_nn(a, w, out_dtype=F32, res=x, alpha=alpha, next_gain=next_gain, name=name + "_norm")


def _ffn_forward(x, xn, w, next_gain, carry_gather=None):
    res = _ffn_in(xn, w["wi_t"], name="ffn_in", carry_gather=carry_gather)
    h, pg, pu = res[:3]
    x_new, xn_next = _out_proj(h, w["wo"], x, 0.5, next_gain, "ffn_out")
    out = (x_new, xn_next, dict(x=x, xn=xn, h=h, pg=pg, pu=pu))
    return out if carry_gather is None else out + (_fill_own_slot(res[3], carry_gather),)


def _ffn_backward(dxo, dxo_b, gain, w, saved, gbuf, tag, carry_sibling=None):
    res = _ffn_bwd_mid(dxo_b, w["wo"], saved["pg"], saved["pu"], name="ffn_bwd_mid", carry_sibling=carry_sibling)
    dzg, dzu = res[:2]
    theirs = None if carry_sibling is None else res[2]
    gbuf = _mm_tn(saved["h"], dxo_b, alpha=0.5, tm=F_HALF, name="ffn_gw_out", into=_into_ffn_out(gbuf, tag + "o"))
    gbuf = _mm_tn(dzg, saved["xn"], tm=F_HALF, name="ffn_gw_in", into=_into_ffn_in(gbuf, tag + "i", 0))
    gbuf = _mm_tn(dzu, saved["xn"], tm=F_HALF, name="ffn_gw_in", into=_into_ffn_in(gbuf, tag + "i", 1))
    dx, dx_b, dg = _dxn_norm_bwd([dzg, dzu], w["wi_t"], saved["x"], gain, dxo, name="ffn_dxn_norm_bwd")
    return dx, dx_b, dg, gbuf, theirs


def _mixer_forward(x, xn, p, w, nb, seq, ta, next_gain, next_pack=None):
    qkv, rest, fl = _mix_proj(xn, w["wm_t"])
    qa, ka, vm = _fox_prep(fl, p["bf"], qkv, nb, seq)
    if next_pack is None:
        (y_attn, lse), next_gathered = _fox_fwd(qa, ka, vm, nb, seq, ta), None
    else:
        y_attn, lse, next_gathered = _fox_fwd(qa, ka, vm, nb, seq, ta, carry_gather=next_pack)
        next_gathered = _fill_own_slot(next_gathered, next_pack)
    y_pool = _pool_fwd(rest, p["wbd"], p["scale"], seq)
    y_conv = _conv_fwd(rest, p["cw"], seq)
    y = jnp.concatenate([y_attn, y_pool, y_conv], axis=1)
    x_new, xn_next = _out_proj(y, w["wo"], x, 1.0, next_gain, "mix_out")
    return x_new, xn_next, dict(x=x, xn=xn, qa=qa, ka=ka, vm=vm, rest=rest, fl=fl, lse=lse, y=y), next_gathered


def _mixer_backward(dxo, dxo_b, p, w, sv, nb, seq, ta, gbuf, pending=None):
    t = dxo.shape[0]
    dy = _mm_nt(dxo_b, w["wo"], out_dtype=F32, name="mix_dy")
    gbuf = _mm_tn(sv["y"], dxo_b, name="mix_gw_out", into=_into_mix_out(gbuf))
    res = _fox_bwd(sv["qa"], sv["ka"], sv["vm"], sv["y"], dy, sv["lse"], nb, seq, ta, carry_exchange=pending)
    dq, dk, dv, d_rows, d_cols = res[:5]
    landed = None if pending is None else res[5]
    ddh = (d_rows.reshape(nb, N_HEADS, seq) - d_cols.reshape(nb, N_HEADS, seq)).transpose(0, 2, 1)
    ddh = ddh.reshape(t, N_HEADS)
    dfl, dbf = _fox_prep_bwd(jnp.pad(ddh, ((0, 0), (0, LANES - N_HEADS))), sv["fl"], p["bf"], seq)
    dpool, dwbd, dscale = _pool_bwd(sv["rest"], dy, p["wbd"], p["wbd_t"], p["scale"], seq)
    dconv, dcw = _conv_bwd(sv["rest"], dy, p["cw"], seq)
    dproj = jnp.concatenate([dq, dk.astype(BF16), dv.astype(BF16), dpool, dconv, dfl], axis=1)
    gbuf = _put_mix_in(gbuf, _mm_tn(dproj, sv["xn"], tm=D_INP // 3, name="mix_gw_in"))
    dx, dx_b, dg = _dxn_norm_bwd([dproj], w["wm_t"], sv["x"], p["norm"], dxo, name="mix_dxn_norm_bwd")
    return dx, dx_b, dict(norm=dg, bf=dbf, wbd=dwbd, scale=dscale, cw=dcw), gbuf, landed


def _block_diag(wp):
    z = jnp.zeros((POOL_GROUP, POOL_GROUP), wp.dtype)
    return jnp.concatenate(
        [jnp.concatenate([wp[g] if g == r else z for g in range(4)], axis=1) for r in range(4)], axis=0)


def _row_pad(a, rows):
    a = a.reshape(-1, a.shape[-1])
    return jnp.pad(a, ((0, rows - a.shape[0]), (0, 0)))


def kernel(x, norm_ffn1, w_ffn1_in, w_ffn1_out, norm_mix, w_mix_in, b_forget, w_pool, pool_scale, conv_w, w_mix_out, norm_ffn2, w_ffn2_in, w_ffn2_out, norm_final, loss_target, m_norm_ffn1, m_w_ffn1_in, m_w_ffn1_out, m_norm_mix, m_w_mix_in, m_b_forget, m_w_pool, m_pool_scale, m_conv_w, m_w_mix_out, m_norm_ffn2, m_w_ffn2_in, m_w_ffn2_out, m_norm_final, v_norm_ffn1, v_w_ffn1_in, v_w_ffn1_out, v_norm_mix, v_w_mix_in, v_b_forget, v_w_pool, v_pool_scale, v_conv_w, v_w_mix_out, v_norm_ffn2, v_w_ffn2_in, v_w_ffn2_out, v_norm_final):
    nb, seq, d = x.shape
    depth = norm_ffn1.shape[0]
    t = nb * seq
    ta = _pick(seq, (ATT_TILE, 128))
    my_id = 4 * lax.axis_index("x") + 2 * lax.axis_index("y") + lax.axis_index("c")
    cshard = conv_w.shape[-1]

    shards = (w_ffn1_in, w_ffn1_out, w_mix_in, w_mix_out, w_ffn2_in, w_ffn2_out)
    pack0 = _pack_shards(shards, 0, BF16)
    wg_head = _all_gather(pack0[:PACK_HEAD], name="gather_weights")
    cw_g = _all_gather(_row_pad(conv_w.reshape(depth * 3, cshard), 16).reshape(4, LANES), name="gather_conv_taps")
    cw_all = cw_g.reshape(N_DEV, 16, cshard)[:, :depth * 3].reshape(N_DEV, depth, 3, cshard)
    cw_all = cw_all.transpose(1, 2, 0, 3).reshape(depth, 3, D_CONV)

    xs = x.reshape(t, d)
    xn = _rmsnorm_fwd(xs, norm_ffn1[0][None], name="first_norm")
    saved = []
    for l in range(depth):
        wbd = _block_diag(w_pool[l])
        p = dict(norm=norm_mix[l][None], bf=jnp.pad(b_forget[l], (0, LANES - N_HEADS))[None],
                 wbd=wbd.astype(BF16), wbd_t=wbd.T.astype(BF16), scale=pool_scale[l][None],
                 cw=_row_pad(cw_all[l], 8))
        w = dict(f1=_ffn_weights(wg_head, 0))
        if l == 0:
            xs, xn, s1, wg_tail = _ffn_forward(xs, xn, w["f1"], norm_mix[l][None], carry_gather=pack0[PACK_HEAD:])
        else:
            xs, xn, s1 = _ffn_forward(xs, xn, w["f1"], norm_mix[l][None])
        w["mix"], w["f2"] = _tail_weights(wg_tail)
        next_pack = _pack_shards(shards, l + 1, BF16) if l + 1 < depth else None
        xs, xn, sm, wg = _mixer_forward(xs, xn, p, w["mix"], nb, seq, ta, norm_ffn2[l][None], next_pack)
        if wg is not None:
            wg_head, wg_tail = wg[:, :PACK_HEAD], wg[:, PACK_HEAD:]
        xs, xn, s2 = _ffn_forward(xs, xn, w["f2"], norm_ffn1[l + 1][None] if l + 1 < depth else None)
        saved.append((w, p, s1, sm, s2))

    dx, dx_b, g_norm_final, loss_part = _final_loss_bwd(xs, norm_final[None], loss_target.reshape(t, d))
    layer_g = [None] * depth
    small = [None] * depth
    rows_g4 = None
    for l in reversed(range(depth)):
        w, p, s1, sm, s2 = saved[l]
        dx, dx_b, dg2, gbuf, theirs = _ffn_backward(dx, dx_b, norm_ffn2[l][None], w["f2"], s2, None, "f2", rows_g4)
        chip_sums = None if rows_g4 is None else _add_sibling(rows_g4, theirs)
        dx, dx_b, gm, gbuf, landed = _mixer_backward(dx, dx_b, p, w["mix"], sm, nb, seq, ta, gbuf, chip_sums)
        if chip_sums is not None:
            layer_g[l + 1] = _sum_landed(chip_sums, landed)
        dx, dx_b, dg1, rows_g4, _ = _ffn_backward(dx, dx_b, norm_ffn1[l][None], w["f1"], s1, gbuf, "f1")
        small[l] = dict(n1=dg1, nm=gm["norm"], n2=dg2, bf=gm["bf"], wbd=gm["wbd"], scale=gm["scale"], cw=gm["cw"])
    grad_x = dx.reshape(nb, seq, d)
    chip_sums = _add_sibling(rows_g4, _exchange_sibling(rows_g4, name="exchange_grads_sibling"))
    layer_g[0] = _sum_landed(chip_sums, _exchange_chips(chip_sums, name="exchange_grads_chips"))

    pieces = {}
    for nm, n in (("f1i", FFN_ROWS), ("f1o", OUT_ROWS), ("mi", MIX_ROWS), ("mo", MO_ROWS), ("f2i", FFN_ROWS),
                  ("f2o", OUT_ROWS)):
        pieces[nm] = jnp.stack([g[GRAD_AT[nm]:GRAD_AT[nm] + n] for g in layer_g])
    g_sharded = dict(
        w_ffn1_in=pieces["f1i"].transpose(0, 2, 1), w_ffn1_out=pieces["f1o"],
        w_mix_in=pieces["mi"].transpose(0, 2, 1), w_mix_out=pieces["mo"],
        w_ffn2_in=pieces["f2i"].transpose(0, 2, 1), w_ffn2_out=pieces["f2o"])

    def tile8(a):
        return jnp.pad(a, ((0, 8 - a.shape[0]), (0, D_MODEL - a.shape[1])))

    rows = []
    for l in range(depth):
        s = small[l]
        wp_rows = jnp.stack([s["wbd"][POOL_GROUP * g:POOL_GROUP * (g + 1), POOL_GROUP * g:POOL_GROUP * (g + 1)]
                             for g in range(4)]).reshape(16, D_MODEL)
        rows += [tile8(s["n1"]), tile8(s["nm"]), tile8(s["n2"]), tile8(s["bf"]), tile8(s["scale"]), tile8(s["cw"]),
                 wp_rows]
    rows += [tile8(g_norm_final), tile8(loss_part)]
    per_layer = 6 * 8 + 16
    small_sum = _sum_slots(_all_gather(jnp.concatenate(rows, axis=0), name="gather_small_grads"),
                           name="sum_small_grads")
    lay = small_sum[:depth * per_layer].reshape(depth, per_layer, D_MODEL)
    g_small = dict(
        norm_ffn1=lay[:, 0], norm_mix=lay[:, 8], norm_ffn2=lay[:, 16], b_forget=lay[:, 24, :N_HEADS],
        pool_scale=lay[:, 32, :D_POOL],
        conv_w=lax.dynamic_slice_in_dim(lay[:, 40:43, :D_CONV], my_id * cshard, cshard, axis=2),
        w_pool=lay[:, 48:64].reshape(depth, 4, POOL_GROUP, POOL_GROUP),
        norm_final=small_sum[depth * per_layer])
    loss = small_sum[depth * per_layer + 8, 0]

    given = dict(norm_ffn1=(norm_ffn1, m_norm_ffn1, v_norm_ffn1), w_ffn1_in=(w_ffn1_in, m_w_ffn1_in, v_w_ffn1_in),
                 w_ffn1_out=(w_ffn1_out, m_w_ffn1_out, v_w_ffn1_out), norm_mix=(norm_mix, m_norm_mix, v_norm_mix),
                 w_mix_in=(w_mix_in, m_w_mix_in, v_w_mix_in), b_forget=(b_forget, m_b_forget, v_b_forget),
                 w_pool=(w_pool, m_w_pool, v_w_pool), pool_scale=(pool_scale, m_pool_scale, v_pool_scale),
                 conv_w=(conv_w, m_conv_w, v_conv_w), w_mix_out=(w_mix_out, m_w_mix_out, v_w_mix_out),
                 norm_ffn2=(norm_ffn2, m_norm_ffn2, v_norm_ffn2), w_ffn2_in=(w_ffn2_in, m_w_ffn2_in, v_w_ffn2_in),
                 w_ffn2_out=(w_ffn2_out, m_w_ffn2_out, v_w_ffn2_out), norm_final=(norm_final, m_norm_final, v_norm_final))
    names = list(given)
    grads, deltas, new_m, new_v = {}, {}, {}, {}
    for nm in names:
        wv, mv, vv = given[nm]
        gv = (g_sharded[nm] if nm in g_sharded else g_small[nm]).reshape(wv.shape)
        shape2 = (-1, wv.shape[-1]) if wv.ndim > 1 else (1, wv.shape[0])
        dl, mn, vn = _adamw(wv.reshape(shape2), gv.reshape(shape2), mv.reshape(shape2), vv.reshape(shape2),
                            name="adamw_" + nm)
        grads[nm], deltas[nm], new_m[nm], new_v[nm] = gv, dl.reshape(wv.shape), mn.reshape(wv.shape), vn.reshape(wv.shape)
    return (loss, grad_x, *[grads[n] for n in names], *[deltas[n] for n in names],
            *[new_m[n] for n in names], *[new_v[n] for n in names])
```

```python
import functools

import jax
import jax.numpy as jnp
from jax import lax
from jax.experimental import pallas as pl
from jax.experimental.pallas import tpu as pltpu

F32 = jnp.float32
BF16 = jnp.bfloat16

D_MODEL = 1024
D_FF = 2816
HEAD_DIM = 64
N_HEADS = 8
N_PAIRS = N_HEADS // 2
D_ATTN = 512
D_POOL = 256
D_CONV = 256
POOL_WINDOWS = (2, 4, 8, 16)
POOL_GROUP = 64
D_IN = 2568
RMS_EPS = 1e-6
ADAM_LR, ADAM_B1, ADAM_B2, ADAM_EPS, ADAM_WD, ADAM_STEP = 0.001, 0.9, 0.999, 1e-08, 0.01, 10

N_DEV = 8
N_CHIPS = 4
LANES = 128
VMEM_BYTES_V7X = 64 * 1024 * 1024
VMEM_LIMIT_MAX = VMEM_BYTES_V7X - 8 * 1024 * 1024

F_HALF = D_FF // 2
D_QKV = 3 * D_ATTN
D_REST = D_POOL + 3 * D_CONV
D_INP = D_QKV + D_REST + LANES
MIX_ROWS = 321
MIX_ROWS_PAD = 336
FFN_ROWS = 704
OUT_ROWS = 352
MO_ROWS = 128
LAYER_ROWS = 2 * (FFN_ROWS + OUT_ROWS) + MIX_ROWS_PAD + MO_ROWS
NEG_BIG = -1e30
ATT_SCALE = HEAD_DIM ** -0.5
ATT_K = 2 * LANES
ATT_TILE = 256
ATT_PAIRS_FWD = 4
ATT_PAIRS_BWD = 4
MXU_COLS = 256


def _cparams(sem, vmem_bytes):
    limit = int(min(max(vmem_bytes, 16 * 1024 * 1024), VMEM_LIMIT_MAX))
    return pltpu.CompilerParams(dimension_semantics=sem, vmem_limit_bytes=limit)


def _nbytes(shape, dtype):
    n = 1
    for s in shape:
        n *= s
    return n * jnp.dtype(dtype).itemsize


def _pick(n, prefs):
    for p in prefs:
        if n % p == 0:
            return p
    return n


def _rmsnorm_fwd(x, g, name):
    t, d = x.shape
    tm = _pick(t, (512, 256, 128))

    def body(x_ref, g_ref, o_ref):
        xv = x_ref[...]
        r = lax.rsqrt(jnp.mean(xv * xv, axis=-1, keepdims=True) + RMS_EPS)
        o_ref[...] = ((xv * r) * g_ref[...]).astype(o_ref.dtype)

    return pl.pallas_call(
        body, grid=(t // tm,),
        in_specs=[pl.BlockSpec((tm, d), lambda i: (i, 0)), pl.BlockSpec((1, d), lambda i: (0, 0))],
        out_specs=pl.BlockSpec((tm, d), lambda i: (i, 0)),
        out_shape=jax.ShapeDtypeStruct((t, d), BF16), name=name,
        compiler_params=_cparams(("parallel",), 6 * tm * d * 4),
    )(x, g)


def _mm_nn(a, b, *, out_dtype, name, res=None, alpha=1.0, tn=None, next_gain=None):
    m, k = a.shape
    n = b.shape[1]
    tn = n if tn is None else tn
    tm = _pick(m, (512, 256, 128))
    with_res = res is not None
    with_norm = next_gain is not None
    assert not with_norm or tn == n

    def body(*refs):
        refs = list(refs)
        a_ref, b_ref = refs[:2]
        r_ref = refs[2] if with_res else None
        g_ref = refs[2 + with_res] if with_norm else None
        o_ref = refs[2 + with_res + with_norm]
        acc = jnp.dot(a_ref[...], b_ref[...], preferred_element_type=F32)
        if with_res:
            acc = r_ref[...] + alpha * acc
        o_ref[...] = acc.astype(o_ref.dtype)
        if with_norm:
            r = lax.rsqrt(jnp.mean(acc * acc, axis=-1, keepdims=True) + RMS_EPS)
            refs[-1][...] = ((acc * r) * g_ref[...]).astype(BF16)

    in_specs = [pl.BlockSpec((tm, k), lambda j, i: (i, 0)), pl.BlockSpec((k, tn), lambda j, i: (0, j))]
    args = [a, b]
    out_blk = pl.BlockSpec((tm, tn), lambda j, i: (i, j))
    out_specs, out_shape = [out_blk], [jax.ShapeDtypeStruct((m, n), out_dtype)]
    if with_res:
        in_specs.append(out_blk)
        args.append(res)
    if with_norm:
        in_specs.append(pl.BlockSpec((1, n), lambda j, i: (0, 0)))
        args.append(next_gain)
        out_specs.append(out_blk)
        out_shape.append(jax.ShapeDtypeStruct((m, n), BF16))
    vmem = 2 * (_nbytes((tm, k), BF16) + _nbytes((k, tn), BF16) + 4 * _nbytes((tm, tn), F32))
    outs = pl.pallas_call(
        body, grid=(n // tn, m // tm), in_specs=in_specs, out_specs=out_specs, out_shape=out_shape, name=name,
        compiler_params=_cparams(("parallel", "parallel"), vmem),
    )(*args)
    return outs if with_norm else outs[0]


def _mm_nt(a, b_t, *, out_dtype, name):
    m, k = a.shape
    n = b_t.shape[0]
    tm = _pick(m, (512, 256, 128))

    def body(a_ref, b_ref, o_ref):
        o_ref[...] = _nt(a_ref[...], b_ref[...]).astype(o_ref.dtype)

    vmem = 2 * (_nbytes((tm, k), BF16) + _nbytes((n, k), BF16) + 3 * _nbytes((tm, n), F32))
    return pl.pallas_call(
        body, grid=(m // tm,),
        in_specs=[pl.BlockSpec((tm, k), lambda i: (i, 0)), pl.BlockSpec((n, k), lambda i: (0, 0))],
        out_specs=pl.BlockSpec((tm, n), lambda i: (i, 0)), out_shape=jax.ShapeDtypeStruct((m, n), out_dtype),
        name=name, compiler_params=_cparams(("parallel",), vmem),
    )(a, b_t)


def _mix_proj(xn, wm_t):
    t, d = xn.shape
    tm = _pick(t, (512, 256, 128))

    def body(x_ref, w_ref, qkv_ref, rest_ref, f_ref):
        xv = x_ref[...]
        for c0, cw in _col_chunks(D_QKV, MXU_COLS):
            qkv_ref[:, c0:c0 + cw] = _nt(xv, w_ref[c0:c0 + cw, :]).astype(qkv_ref.dtype)
        for c0, cw in _col_chunks(D_REST, MXU_COLS):
            rest_ref[:, c0:c0 + cw] = _nt(xv, w_ref[D_QKV + c0:D_QKV + c0 + cw, :])
        f_ref[...] = _nt(xv, w_ref[D_QKV + D_REST:, :])

    def rows(n):
        return pl.BlockSpec((tm, n), lambda i: (i, 0))

    vmem = 2 * (_nbytes((tm, d), BF16) + _nbytes((D_INP, d), BF16) + 3 * _nbytes((tm, D_INP), F32))
    return pl.pallas_call(
        body, grid=(t // tm,), in_specs=[rows(d), pl.BlockSpec((D_INP, d), lambda i: (0, 0))],
        out_specs=[rows(D_QKV), rows(D_REST), rows(LANES)],
        out_shape=[jax.ShapeDtypeStruct((t, D_QKV), BF16), jax.ShapeDtypeStruct((t, D_REST), F32),
                   jax.ShapeDtypeStruct((t, LANES), F32)],
        name="mix_proj", compiler_params=_cparams(("parallel",), vmem),
    )(xn, wm_t)


def _mm_tn(a, b, *, name, alpha=1.0, tm=None, into=None):
    t, m = a.shape
    n = b.shape[1]
    tm = m if tm is None else tm
    tk = _pick(t, (2048, 1024, 512, 256, 128))
    nk = t // tk

    def body(a_ref, b_ref, *rest):
        o_ref = rest[-1]
        kk = pl.program_id(1)
        p = lax.dot_general(a_ref[...], b_ref[...], (((0,), (0,)), ((), ())), preferred_element_type=F32)
        if alpha != 1.0:
            p = alpha * p
        p = p.reshape(o_ref.shape)

        @pl.when(kk == 0)
        def _():
            o_ref[...] = p

        @pl.when(kk > 0)
        def _():
            o_ref[...] += p

    vmem = 2 * (_nbytes((tk, tm), BF16) + _nbytes((tk, n), BF16) + 2 * _nbytes((tm, n), F32))
    in_specs = [pl.BlockSpec((tk, tm), lambda i, kk: (kk, i)), pl.BlockSpec((tk, n), lambda i, kk: (kk, 0))]
    cp = _cparams(("parallel", "arbitrary"), vmem)
    if into is None:
        return pl.pallas_call(
            body, grid=(m // tm, nk), in_specs=in_specs, out_specs=pl.BlockSpec((tm, n), lambda i, kk: (i, 0)),
            out_shape=jax.ShapeDtypeStruct((m, n), F32), name=name, compiler_params=cp,
        )(a, b)
    buf, buf_shape, blk, index = into
    out_spec = pl.BlockSpec(blk, lambda i, kk: index(i))
    out_shape = jax.ShapeDtypeStruct(buf_shape, F32)
    if buf is None:
        return pl.pallas_call(body, grid=(m // tm, nk), in_specs=in_specs, out_specs=out_spec, out_shape=out_shape,
                              name=name + "_new", compiler_params=cp)(a, b)
    return pl.pallas_call(
        body, grid=(m // tm, nk), in_specs=in_specs + [pl.BlockSpec(memory_space=pl.ANY)], out_specs=out_spec,
        out_shape=out_shape, input_output_aliases={2: 0}, name=name + "_into", compiler_params=cp,
    )(a, b, buf)


def _sigmoid(v):
    return 1.0 / (1.0 + jnp.exp(-v))


def _col_chunks(n, width):
    return [(c, min(width, n - c)) for c in range(0, n, width)]


def _ffn_in(xn, w_t, name, carry_gather=None):
    t, d = xn.shape
    tm = _pick(t, (1024, 512, 256, 128))
    grid = (2, t // tm)

    def body(x_ref, wg_ref, wu_ref, *rest):
        if carry_gather is None:
            h_ref, pg_ref, pu_ref = rest
        else:
            blk_ref, h_ref, pg_ref, pu_ref, gathered_ref, send_sems, recv_sems = rest
            first_step, last_step = _grid_ends([pl.program_id(a) for a in range(2)], grid)

            @pl.when(first_step)
            def _():
                _gather_start(blk_ref, gathered_ref, send_sems, recv_sems)

        xv = x_ref[...]
        for c0, cw in _col_chunks(F_HALF, MXU_COLS):
            cols = slice(c0, c0 + cw)
            g = _nt(xv, wg_ref[cols, :])
            u = _nt(xv, wu_ref[cols, :])
            s = _sigmoid(g)
            silu = g * s
            h_ref[:, cols] = (silu * u).astype(h_ref.dtype)
            pg_ref[:, cols] = (u * (s * (1.0 + g * (1.0 - s)))).astype(pg_ref.dtype)
            pu_ref[:, cols] = silu.astype(pu_ref.dtype)

        if carry_gather is not None:
            @pl.when(last_step)
            def _():
                _gather_finish(blk_ref, gathered_ref, send_sems, recv_sems)

    vmem = 2 * (_nbytes((tm, d), BF16) + 2 * _nbytes((d, F_HALF), BF16) + 4 * _nbytes((tm, D_FF), F32))
    out_blk = pl.BlockSpec((tm, F_HALF), lambda j, i: (i, j))
    sds = jax.ShapeDtypeStruct((t, D_FF), BF16)
    in_specs = [pl.BlockSpec((tm, d), lambda j, i: (i, 0)), pl.BlockSpec((F_HALF, d), lambda j, i: (j, 0)),
                pl.BlockSpec((F_HALF, d), lambda j, i: (2 + j, 0))]
    if carry_gather is None:
        return pl.pallas_call(
            body, grid=grid, in_specs=in_specs, out_specs=[out_blk, out_blk, out_blk], out_shape=[sds, sds, sds],
            name=name, compiler_params=_cparams(("parallel", "parallel"), vmem),
        )(xn, w_t, w_t)
    any_spec = pl.BlockSpec(memory_space=pl.ANY)
    return pl.pallas_call(
        body, grid=grid, in_specs=in_specs + [any_spec], out_specs=[out_blk, out_blk, out_blk, any_spec],
        out_shape=[sds, sds, sds, jax.ShapeDtypeStruct((N_DEV,) + carry_gather.shape, carry_gather.dtype)],
        scratch_shapes=list(GATHER_SEMS), name=name + "_gather",
        compiler_params=_cparams(("arbitrary", "arbitrary"), vmem),
    )(xn, w_t, w_t, carry_gather)


def _ffn_bwd_mid(dxo, w_out, pg, pu, name, carry_sibling=None):
    t, d = dxo.shape
    tm = _pick(t, (1024, 512, 256, 128))
    grid = (2, t // tm)

    def body(d_ref, w_ref, pg_ref, pu_ref, *rest):
        if carry_sibling is None:
            dg_ref, du_ref = rest
        else:
            g4_ref, dg_ref, du_ref, theirs_ref, send_sems, recv_sems = rest
            first_step, last_step = _grid_ends([pl.program_id(a) for a in range(2)], grid)

            @pl.when(first_step)
            def _():
                _sibling_start(g4_ref, theirs_ref, send_sems, recv_sems)

        dv = d_ref[...]
        for c0, cw in _col_chunks(F_HALF, MXU_COLS):
            cols = slice(c0, c0 + cw)
            dh = 0.5 * _nt(dv, w_ref[cols, :])
            dg_ref[:, cols] = (dh * pg_ref[:, cols].astype(F32)).astype(dg_ref.dtype)
            du_ref[:, cols] = (dh * pu_ref[:, cols].astype(F32)).astype(du_ref.dtype)

        if carry_sibling is not None:
            @pl.when(last_step)
            def _():
                _sibling_finish(g4_ref, theirs_ref, send_sems, recv_sems)

    vmem = 2 * (_nbytes((tm, d), BF16) + _nbytes((d, F_HALF), BF16) + 5 * _nbytes((tm, D_FF), F32))
    blk = pl.BlockSpec((tm, F_HALF), lambda j, i: (i, j))
    sds = jax.ShapeDtypeStruct((t, D_FF), BF16)
    in_specs = [pl.BlockSpec((tm, d), lambda j, i: (i, 0)), pl.BlockSpec((F_HALF, d), lambda j, i: (j, 0)), blk, blk]
    if carry_sibling is None:
        return pl.pallas_call(
            body, grid=grid, in_specs=in_specs, out_specs=[blk, blk], out_shape=[sds, sds], name=name,
            compiler_params=_cparams(("parallel", "parallel"), vmem),
        )(dxo, w_out, pg, pu)
    nchip, _, r, c = carry_sibling.shape
    any_spec = pl.BlockSpec(memory_space=pl.ANY)
    return pl.pallas_call(
        body, grid=grid, in_specs=in_specs + [any_spec], out_specs=[blk, blk, any_spec],
        out_shape=[sds, sds, jax.ShapeDtypeStruct((nchip, r, c), carry_sibling.dtype)],
        scratch_shapes=list(SIBLING_SEMS), name=name + "_exchange",
        compiler_params=_cparams(("arbitrary", "arbitrary"), vmem),
    )(dxo, w_out, pg, pu, carry_sibling)


def _dxn_norm_bwd(parts, b, x, g, dxo, name):
    t, d = x.shape
    k = parts[0].shape[1]
    n_parts = len(parts)
    tm = _pick(t, (256, 128))

    def body(*refs):
        a_refs, b_refs = refs[:n_parts], refs[n_parts:2 * n_parts]
        x_ref, g_ref, do_ref, dx_ref, dxb_ref, dg_ref = refs[2 * n_parts:]
        i = pl.program_id(0)
        dn = jnp.dot(a_refs[0][...], b_refs[0][...], preferred_element_type=F32)
        for a_ref, b_ref in zip(a_refs[1:], b_refs[1:]):
            dn = dn + jnp.dot(a_ref[...], b_ref[...], preferred_element_type=F32)
        xv = x_ref[...]
        r = lax.rsqrt(jnp.mean(xv * xv, axis=-1, keepdims=True) + RMS_EPS)
        xh = xv * r
        dgp = jnp.sum(dn * xh, axis=0, keepdims=True)
        dh = dn * g_ref[...]
        dx = do_ref[...] + r * (dh - xh * jnp.mean(dh * xh, axis=-1, keepdims=True))
        dx_ref[...] = dx
        dxb_ref[...] = dx.astype(dxb_ref.dtype)

        @pl.when(i == 0)
        def _():
            dg_ref[...] = dgp

        @pl.when(i > 0)
        def _():
            dg_ref[...] += dgp

    blk = pl.BlockSpec((tm, d), lambda i: (i, 0))
    row = pl.BlockSpec((1, d), lambda i: (0, 0))
    a_specs = [pl.BlockSpec((tm, k), lambda i: (i, 0)) for _ in parts]
    b_specs = [pl.BlockSpec((k, d), lambda i, kk=kk: (kk, 0)) for kk in range(n_parts)]
    vmem = 2 * n_parts * (_nbytes((tm, k), BF16) + _nbytes((k, d), BF16)) + 16 * tm * d * 4
    return pl.pallas_call(
        body, grid=(t // tm,), in_specs=a_specs + b_specs + [blk, row, blk], out_specs=[blk, blk, row],
        out_shape=[jax.ShapeDtypeStruct((t, d), F32), jax.ShapeDtypeStruct((t, d), BF16),
                   jax.ShapeDtypeStruct((1, d), F32)], name=name,
        compiler_params=_cparams(("arbitrary",), vmem),
    )(*parts, *([b] * n_parts), x, g, dxo)


def _final_loss_bwd(x, g, tgt):
    t, d = x.shape
    tm = _pick(t, (512, 256, 128))

    def body(x_ref, g_ref, t_ref, dx_ref, dxb_ref, dg_ref, loss_ref):
        i = pl.program_id(0)
        xv = x_ref[...]
        r = lax.rsqrt(jnp.mean(xv * xv, axis=-1, keepdims=True) + RMS_EPS)
        xh = xv * r
        gv = g_ref[...]
        err = xh * gv - t_ref[...]
        lp = 0.5 * jnp.sum(jnp.mean(err * err, axis=-1, keepdims=True), axis=0, keepdims=True)
        dy = err * (1.0 / d)
        dgp = jnp.sum(dy * xh, axis=0, keepdims=True)
        dh = dy * gv
        dx = r * (dh - xh * jnp.mean(dh * xh, axis=-1, keepdims=True))
        dx_ref[...] = dx
        dxb_ref[...] = dx.astype(dxb_ref.dtype)
        lpb = jnp.broadcast_to(lp, (1, LANES))

        @pl.when(i == 0)
        def _():
            dg_ref[...] = dgp
            loss_ref[...] = lpb

        @pl.when(i > 0)
        def _():
            dg_ref[...] += dgp
            loss_ref[...] += lpb

    blk = pl.BlockSpec((tm, d), lambda i: (i, 0))
    row = pl.BlockSpec((1, d), lambda i: (0, 0))
    return pl.pallas_call(
        body, grid=(t // tm,), in_specs=[blk, row, blk],
        out_specs=[blk, blk, row, pl.BlockSpec((1, LANES), lambda i: (0, 0))],
        out_shape=[jax.ShapeDtypeStruct((t, d), F32), jax.ShapeDtypeStruct((t, d), BF16),
                   jax.ShapeDtypeStruct((1, d), F32), jax.ShapeDtypeStruct((1, LANES), F32)], name="final_loss_bwd",
        compiler_params=_cparams(("arbitrary",), 16 * tm * d * 4),
    )(x, g, tgt)


def _seq_scan(v, seq, reverse):
    row = lax.broadcasted_iota(jnp.int32, v.shape, 0)
    k = 1
    while k < seq:
        if reverse:
            v = v + jnp.where(row < seq - k, pltpu.roll(v, seq - k, 0), 0.0)
        else:
            v = v + jnp.where(row >= k, pltpu.roll(v, k, 0), 0.0)
        k *= 2
    return v


def _log_sigmoid(v):
    return jnp.minimum(v, 0.0) - jnp.log(1.0 + jnp.exp(-jnp.abs(v)))


def _fox_prep(fl, bf, qkv, nb, seq):
    def body(f_ref, b_ref, q_ref, k_ref, v_ref, qa_ref, ka_ref, vm_ref):
        dsum = _seq_scan(_log_sigmoid(f_ref[...] + b_ref[...]), seq, False)
        d1 = dsum.astype(BF16).astype(F32)
        r1 = dsum - d1
        d2 = r1.astype(BF16).astype(F32)
        d3 = (r1 - d2).astype(BF16).astype(F32)
        lane = lax.broadcasted_iota(jnp.int32, (seq, LANES), 1)
        first = lane < HEAD_DIM
        l64 = jnp.where(first, lane, lane - HEAD_DIM)
        for p in range(N_PAIRS):
            def head_cols(a, p=p):
                return jnp.where(first, a[:, 2 * p:2 * p + 1], a[:, 2 * p + 1:2 * p + 2])

            e1, e2, e3 = head_cols(d1), head_cols(d2), head_cols(d3)
            aux_q = jnp.where(l64 == 0, e1, jnp.where(l64 == 1, e2, jnp.where(l64 == 2, e3,
                              jnp.where(l64 < 6, 1.0, 0.0)))).astype(BF16)
            aux_k = jnp.where(l64 < 3, 1.0, jnp.where(l64 == 3, -e1, jnp.where(l64 == 4, -e2,
                              jnp.where(l64 == 5, -e3, 0.0)))).astype(BF16)
            cols = slice(LANES * p, LANES * (p + 1))
            qs = q_ref[:, cols] * ATT_SCALE
            vp = v_ref[:, cols]
            zero = jnp.zeros_like(qs)
            qa_ref[0, p, 0, :, :LANES] = jnp.where(first, qs, zero)
            qa_ref[0, p, 0, :, LANES:] = jnp.where(first, aux_q, zero)
            qa_ref[0, p, 1, :, :LANES] = jnp.where(first, zero, qs)
            qa_ref[0, p, 1, :, LANES:] = jnp.where(first, zero, aux_q)
            ka_ref[0, p, :, :LANES] = k_ref[:, cols]
            ka_ref[0, p, :, LANES:] = aux_k
            vm_ref[0, p, 0] = jnp.where(first, vp, zero)
            vm_ref[0, p, 1] = jnp.where(first, zero, vp)

    def part(c):
        return pl.BlockSpec((seq, D_ATTN), lambda b, c=c: (b, c))

    return pl.pallas_call(
        body, grid=(nb,),
        in_specs=[pl.BlockSpec((seq, LANES), lambda b: (b, 0)), pl.BlockSpec((1, LANES), lambda b: (0, 0)),
                  part(0), part(1), part(2)],
        out_specs=[pl.BlockSpec((1, N_PAIRS, 2, seq, ATT_K), lambda b: (b, 0, 0, 0, 0)),
                   pl.BlockSpec((1, N_PAIRS, seq, ATT_K), lambda b: (b, 0, 0, 0)),
                   pl.BlockSpec((1, N_PAIRS, 2, seq, LANES), lambda b: (b, 0, 0, 0, 0))],
        out_shape=[jax.ShapeDtypeStruct((nb, N_PAIRS, 2, seq, ATT_K), BF16),
                   jax.ShapeDtypeStruct((nb, N_PAIRS, seq, ATT_K), BF16),
                   jax.ShapeDtypeStruct((nb, N_PAIRS, 2, seq, LANES), BF16)],
        name="fox_prep", compiler_params=_cparams(("parallel",), 48 * 1024 * 1024),
    )(fl, bf, qkv, qkv, qkv)


def _fox_prep_bwd(dd, fl, bf, seq):
    t = fl.shape[0]

    def body(d_ref, f_ref, b_ref, o_ref, db_ref):
        i = pl.program_id(0)
        dlog = _seq_scan(d_ref[...], seq, True)
        dfl = dlog * _sigmoid(-(f_ref[...] + b_ref[...]))
        o_ref[...] = dfl.astype(o_ref.dtype)
        dbp = jnp.sum(dfl, axis=0, keepdims=True)

        @pl.when(i == 0)
        def _():
            db_ref[...] = dbp

        @pl.when(i > 0)
        def _():
            db_ref[...] += dbp

    blk = pl.BlockSpec((seq, LANES), lambda b: (b, 0))
    row = pl.BlockSpec((1, LANES), lambda b: (0, 0))
    return pl.pallas_call(
        body, grid=(t // seq,), in_specs=[blk, blk, row], out_specs=[blk, row],
        out_shape=[jax.ShapeDtypeStruct((t, LANES), BF16), jax.ShapeDtypeStruct((1, LANES), F32)], name="fox_prep_bwd",
        compiler_params=_cparams(("arbitrary",), 24 * seq * LANES * 4),
    )(dd, fl, bf)


def _pair_rows(a, ta):
    lane = lax.broadcasted_iota(jnp.int32, (ta, LANES), 1)
    return jnp.where(lane < HEAD_DIM, a[:ta], a[ta:])


def _diag_mask(ta):
    r = lax.broadcasted_iota(jnp.int32, (2 * ta, ta), 0)
    c = lax.broadcasted_iota(jnp.int32, (2 * ta, ta), 1)
    return c <= jnp.where(r >= ta, r - ta, r)


def _nt(a, b):
    return lax.dot_general(a, b, (((1,), (1,)), ((), ())), preferred_element_type=F32)


def _tn(a, b):
    return lax.dot_general(a, b, (((0,), (0,)), ((), ())), preferred_element_type=F32)


def _grid_ends(ids, sizes):
    first = functools.reduce(jnp.logical_and, [i == 0 for i in ids])
    last = functools.reduce(jnp.logical_and, [i == n - 1 for i, n in zip(ids, sizes)])
    return first, last


def _fox_fwd(qa, ka, vm, nb, seq, ta, carry_gather=None):
    nq = seq // ta
    npp = ATT_PAIRS_FWD
    grid = (nb, N_PAIRS // npp, nq)

    def body(q_ref, k_ref, v_ref, *rest):
        if carry_gather is None:
            o_ref, lse_ref = rest
        else:
            x_ref, o_ref, lse_ref, gathered_ref, send_sems, recv_sems = rest
            first_step, last_step = _grid_ends([pl.program_id(a) for a in range(3)], grid)

            @pl.when(first_step)
            def _():
                _gather_start(x_ref, gathered_ref, send_sems, recv_sems)

        i = pl.program_id(2)
        q2s = [q_ref[0, pp].reshape(2 * ta, ATT_K) for pp in range(npp)]

        def step(j, carry, masked):
            rows = pl.ds(pl.multiple_of(j * ta, ta), ta)
            out = []
            for pp in range(npp):
                m, l, acc = carry[pp]
                s = _nt(q2s[pp], k_ref[0, pp, rows, :])
                if masked:
                    s = jnp.where(_diag_mask(ta), s, NEG_BIG)
                m_new = jnp.maximum(m, jnp.max(s, axis=-1, keepdims=True))
                p = jnp.exp(s - m_new)
                corr = jnp.exp(m - m_new)
                l = corr * l + jnp.sum(p, axis=-1, keepdims=True)
                pb = p.astype(BF16)
                pv = (jnp.dot(pb[:ta], v_ref[0, pp, 0, rows, :], preferred_element_type=F32)
                      + jnp.dot(pb[ta:], v_ref[0, pp, 1, rows, :], preferred_element_type=F32))
                out.append((m_new, l, _pair_rows(corr, ta) * acc + pv))
            return tuple(out)

        init = tuple((jnp.full((2 * ta, 1), NEG_BIG, F32), jnp.zeros((2 * ta, 1), F32),
                      jnp.zeros((ta, LANES), F32)) for _ in range(npp))
        carry = lax.fori_loop(0, i, functools.partial(step, masked=False), init)
        for pp, (m, l, acc) in enumerate(step(i, carry, True)):
            o_ref[:, LANES * pp:LANES * (pp + 1)] = (acc * _pair_rows(1.0 / l, ta)).astype(o_ref.dtype)
            lse = m + jnp.log(l)
            lse_ref[0, pp, 0] = lse[:ta]
            lse_ref[0, pp, 1] = lse[ta:]

        if carry_gather is not None:
            @pl.when(last_step)
            def _():
                _gather_finish(x_ref, gathered_ref, send_sems, recv_sems)

    vmem = (2 * npp * (_nbytes((seq, ATT_K), BF16) + 2 * _nbytes((seq, LANES), BF16)) + 24 * npp * ta * ta * 4
            + 8 * 1024 * 1024)
    in_specs = [pl.BlockSpec((1, npp, 2, ta, ATT_K), lambda b, g, i: (b, g, 0, i, 0)),
                pl.BlockSpec((1, npp, seq, ATT_K), lambda b, g, i: (b, g, 0, 0)),
                pl.BlockSpec((1, npp, 2, seq, LANES), lambda b, g, i: (b, g, 0, 0, 0))]
    out_specs = [pl.BlockSpec((ta, LANES * npp), lambda b, g, i: (b * nq + i, g)),
                 pl.BlockSpec((1, npp, 2, ta, 1), lambda b, g, i: (b, g, 0, i, 0))]
    out_shape = [jax.ShapeDtypeStruct((nb * seq, D_ATTN), BF16), jax.ShapeDtypeStruct((nb, N_PAIRS, 2, seq, 1), F32)]
    if carry_gather is None:
        return pl.pallas_call(
            body, grid=grid, in_specs=in_specs, out_specs=out_specs, out_shape=out_shape, name="fox_fwd",
            compiler_params=_cparams(("parallel", "parallel", "parallel"), vmem),
        )(qa, ka, vm)
    any_spec = pl.BlockSpec(memory_space=pl.ANY)
    return pl.pallas_call(
        body, grid=grid, in_specs=in_specs + [any_spec], out_specs=out_specs + [any_spec],
        out_shape=out_shape + [jax.ShapeDtypeStruct((N_DEV,) + carry_gather.shape, carry_gather.dtype)],
        scratch_shapes=list(GATHER_SEMS), name="fox_fwd_gather",
        compiler_params=_cparams(("arbitrary", "arbitrary", "arbitrary"), vmem),
    )(qa, ka, vm, carry_gather)


def _fox_bwd(qa, ka, vm, y, dy, lse, nb, seq, ta, carry_exchange=None):
    nq = seq // ta
    npp = ATT_PAIRS_BWD
    grid = (nb, N_PAIRS // npp, nq)

    def body(q_ref, k_ref, v_ref, o_ref, do_ref, lse_ref, *rest):
        if carry_exchange is None:
            dq_ref, dk_ref, dv_ref, rs_ref, cs_ref = rest
        else:
            t_ref, dq_ref, dk_ref, dv_ref, rs_ref, cs_ref, landed_ref, send_sems, recv_sems = rest
            first_step, last_step = _grid_ends([pl.program_id(a) for a in range(3)], grid)

            @pl.when(first_step)
            def _():
                _chips_start(t_ref, landed_ref, send_sems, recv_sems)

        i = pl.program_id(2)

        @pl.when(i == 0)
        def _():
            dk_ref[...] = jnp.zeros_like(dk_ref)
            dv_ref[...] = jnp.zeros_like(dv_ref)
            cs_ref[...] = jnp.zeros_like(cs_ref)

        first = lax.broadcasted_iota(jnp.int32, (ta, LANES), 1) < HEAD_DIM
        q2s, do2s, deltas, lses = [], [], [], []
        for pp in range(npp):
            cols = slice(LANES * pp, LANES * (pp + 1))
            q2s.append(q_ref[0, pp].reshape(2 * ta, ATT_K))
            do = do_ref[:, cols]
            doo = do * o_ref[:, cols].astype(F32)
            do2s.append(jnp.concatenate([jnp.where(first, do, 0.0), jnp.where(first, 0.0, do)], axis=0).astype(BF16))
            deltas.append(jnp.concatenate([jnp.sum(jnp.where(first, doo, 0.0), axis=-1, keepdims=True),
                                           jnp.sum(jnp.where(first, 0.0, doo), axis=-1, keepdims=True)], axis=0))
            lses.append(jnp.concatenate([lse_ref[0, pp, 0], lse_ref[0, pp, 1]], axis=0))

        def step(j, carry, masked):
            rows = pl.ds(pl.multiple_of(j * ta, ta), ta)
            out = []
            for pp in range(npp):
                dq_acc, rs_acc = carry[pp]
                cols = slice(LANES * pp, LANES * (pp + 1))
                ks = k_ref[0, pp, rows, :]
                s = _nt(q2s[pp], ks)
                if masked:
                    s = jnp.where(_diag_mask(ta), s, NEG_BIG)
                p = jnp.exp(s - lses[pp])
                dp = _nt(do2s[pp], v_ref[0, pp, 0, rows, :] + v_ref[0, pp, 1, rows, :])
                ds32 = p * (dp - deltas[pp])
                ds = ds32.astype(BF16)
                dk_ref[rows, cols] += _tn(ds, q2s[pp][:, :LANES])
                dv_ref[rows, cols] += _tn(p.astype(BF16), do2s[pp])
                cs_ref[0, pp, 0, j] += jnp.sum(ds32[:ta], axis=0, keepdims=True)
                cs_ref[0, pp, 1, j] += jnp.sum(ds32[ta:], axis=0, keepdims=True)
                out.append((dq_acc + jnp.dot(ds, ks[:, :LANES], preferred_element_type=F32),
                            rs_acc + jnp.sum(ds32, axis=-1, keepdims=True)))
            return tuple(out)

        init = tuple((jnp.zeros((2 * ta, LANES), F32), jnp.zeros((2 * ta, 1), F32)) for _ in range(npp))
        carry = lax.fori_loop(0, i, functools.partial(step, masked=False), init)
        for pp, (dq_acc, rs_acc) in enumerate(step(i, carry, True)):
            dq = jnp.where(first, dq_acc[:ta], dq_acc[ta:]) * ATT_SCALE
            dq_ref[:, LANES * pp:LANES * (pp + 1)] = dq.astype(dq_ref.dtype)
            rs_row = jnp.transpose(jnp.broadcast_to(rs_acc, (2 * ta, LANES)))[0:1]
            rs_ref[0, pp, 0, 0] = rs_row[:, :ta]
            rs_ref[0, pp, 1, 0] = rs_row[:, ta:]

        if carry_exchange is not None:
            @pl.when(last_step)
            def _():
                _chips_finish(t_ref, landed_ref, send_sems, recv_sems)

    vmem = (2 * npp * (_nbytes((seq, ATT_K), BF16) + 2 * _nbytes((seq, LANES), BF16) + 2 * _nbytes((seq, LANES), F32))
            + 32 * npp * ta * ta * 4 + 8 * 1024 * 1024)
    qblk = lambda b, g, i: (b * nq + i, g)
    acc_blk = pl.BlockSpec((seq, LANES * npp), lambda b, g, i: (b, g))
    in_specs = [pl.BlockSpec((1, npp, 2, ta, ATT_K), lambda b, g, i: (b, g, 0, i, 0)),
                pl.BlockSpec((1, npp, seq, ATT_K), lambda b, g, i: (b, g, 0, 0)),
                pl.BlockSpec((1, npp, 2, seq, LANES), lambda b, g, i: (b, g, 0, 0, 0)),
                pl.BlockSpec((ta, LANES * npp), qblk), pl.BlockSpec((ta, LANES * npp), qblk),
                pl.BlockSpec((1, npp, 2, ta, 1), lambda b, g, i: (b, g, 0, i, 0))]
    out_specs = [pl.BlockSpec((ta, LANES * npp), qblk), acc_blk, acc_blk,
                 pl.BlockSpec((1, npp, 2, 1, 1, ta), lambda b, g, i: (b, g, 0, i, 0, 0)),
                 pl.BlockSpec((1, npp, 2, nq, 1, ta), lambda b, g, i: (b, g, 0, 0, 0, 0))]
    sums = jax.ShapeDtypeStruct((nb, N_PAIRS, 2, nq, 1, ta), F32)
    out_shape = [jax.ShapeDtypeStruct((nb * seq, D_ATTN), BF16), jax.ShapeDtypeStruct((nb * seq, D_ATTN), F32),
                 jax.ShapeDtypeStruct((nb * seq, D_ATTN), F32), sums, sums]
    if carry_exchange is None:
        return pl.pallas_call(
            body, grid=grid, in_specs=in_specs, out_specs=out_specs, out_shape=out_shape, name="fox_bwd",
            compiler_params=_cparams(("parallel", "parallel", "arbitrary"), vmem),
        )(qa, ka, vm, y, dy, lse)
    any_spec = pl.BlockSpec(memory_space=pl.ANY)
    return pl.pallas_call(
        body, grid=grid, in_specs=in_specs + [any_spec], out_specs=out_specs + [any_spec],
        out_shape=out_shape + [jax.ShapeDtypeStruct(carry_exchange.shape, carry_exchange.dtype)],
        scratch_shapes=list(CHIPS_SEMS), name="fox_bwd_exchange",
        compiler_params=_cparams(("arbitrary", "arbitrary", "arbitrary"), vmem),
    )(qa, ka, vm, y, dy, lse, carry_exchange)


def _shift_down(a, k):
    row = lax.broadcasted_iota(jnp.int32, a.shape, 0)
    return jnp.where(row >= k, pltpu.roll(a, k, 0), 0.0)


def _shift_up(a, k):
    n = a.shape[0]
    row = lax.broadcasted_iota(jnp.int32, a.shape, 0)
    return jnp.where(row < n - k, pltpu.roll(a, n - k, 0), 0.0)


def _by_group(vals, shape):
    lane = lax.broadcasted_iota(jnp.int32, shape, 1)
    out = vals[-1]
    for gi in range(len(vals) - 2, -1, -1):
        out = jnp.where(lane < POOL_GROUP * (gi + 1), vals[gi], out)
    return out


def _pooled(u):
    s2 = u + _shift_down(u, 1)
    s4 = s2 + _shift_down(s2, 2)
    s8 = s4 + _shift_down(s4, 4)
    s16 = s8 + _shift_down(s8, 8)
    win = _by_group([s2, s4, s8, s16], u.shape)
    row = lax.broadcasted_iota(jnp.int32, u.shape, 0)
    wsize = _by_group([jnp.full(u.shape, w, jnp.int32) for w in POOL_WINDOWS], u.shape)
    inv = 1.0 / jnp.minimum(row + 1, wsize).astype(F32)
    return win * inv - u, inv


def _pool_fwd(rest, wbd, scale, seq):
    t = rest.shape[0]

    def body(u_ref, w_ref, s_ref, o_ref):
        pooled, _ = _pooled(u_ref[...])
        pw = jnp.dot(pooled.astype(BF16), w_ref[...], preferred_element_type=F32)
        o_ref[...] = (pw * s_ref[...]).astype(o_ref.dtype)

    blk = pl.BlockSpec((seq, D_POOL), lambda b: (b, 0))
    return pl.pallas_call(
        body, grid=(t // seq,),
        in_specs=[blk, pl.BlockSpec((D_POOL, D_POOL), lambda b: (0, 0)), pl.BlockSpec((1, D_POOL), lambda b: (0, 0))],
        out_specs=blk, out_shape=jax.ShapeDtypeStruct((t, D_POOL), BF16), name="pool_fwd",
        compiler_params=_cparams(("parallel",), 24 * seq * D_POOL * 4),
    )(rest, wbd, scale)


def _pool_bwd(rest, dy, wbd, wbd_t, scale, seq):
    t = rest.shape[0]

    def body(u_ref, dy_ref, w_ref, wt_ref, s_ref, du_ref, dw_ref, dsc_ref):
        i = pl.program_id(0)
        pooled, inv = _pooled(u_ref[...])
        pb = pooled.astype(BF16)
        pw = jnp.dot(pb, w_ref[...], preferred_element_type=F32)
        dyp = dy_ref[...]
        dsp = jnp.sum(dyp * pw, axis=0, keepdims=True)
        dpw = (dyp * s_ref[...]).astype(BF16)
        dwp = _tn(pb, dpw)
        dpooled = jnp.dot(dpw, wt_ref[...], preferred_element_type=F32)
        dwin = dpooled * inv
        t2 = dwin + _shift_up(dwin, 1)
        t4 = t2 + _shift_up(t2, 2)
        t8 = t4 + _shift_up(t4, 4)
        t16 = t8 + _shift_up(t8, 8)
        du_ref[...] = (_by_group([t2, t4, t8, t16], dwin.shape) - dpooled).astype(du_ref.dtype)

        @pl.when(i == 0)
        def _():
            dw_ref[...] = dwp
            dsc_ref[...] = dsp

        @pl.when(i > 0)
        def _():
            dw_ref[...] += dwp
            dsc_ref[...] += dsp

    blk = pl.BlockSpec((seq, D_POOL), lambda b: (b, 0))
    sq = pl.BlockSpec((D_POOL, D_POOL), lambda b: (0, 0))
    row = pl.BlockSpec((1, D_POOL), lambda b: (0, 0))
    return pl.pallas_call(
        body, grid=(t // seq,),
        in_specs=[blk, pl.BlockSpec((seq, D_POOL), lambda b: (b, 2)), sq, sq, row],
        out_specs=[blk, sq, row],
        out_shape=[jax.ShapeDtypeStruct((t, D_POOL), BF16), jax.ShapeDtypeStruct((D_POOL, D_POOL), F32),
                   jax.ShapeDtypeStruct((1, D_POOL), F32)], name="pool_bwd",
        compiler_params=_cparams(("arbitrary",), 40 * seq * D_POOL * 4),
    )(rest, dy, wbd, wbd_t, scale)


def _conv_fwd(rest, cw, seq):
    t = rest.shape[0]

    def body(cb_ref, cc_ref, ch_ref, w_ref, o_ref):
        u = cc_ref[...] * ch_ref[...]
        y = w_ref[0:1, :] * _shift_down(u, 2) + w_ref[1:2, :] * _shift_down(u, 1) + w_ref[2:3, :] * u
        o_ref[...] = (cb_ref[...] * y).astype(o_ref.dtype)

    def col(c):
        return pl.BlockSpec((seq, D_CONV), lambda b, c=c: (b, c))

    return pl.pallas_call(
        body, grid=(t // seq,), in_specs=[col(1), col(2), col(3), pl.BlockSpec((8, D_CONV), lambda b: (0, 0))],
        out_specs=pl.BlockSpec((seq, D_CONV), lambda b: (b, 0)),
        out_shape=jax.ShapeDtypeStruct((t, D_CONV), BF16), name="conv_fwd",
        compiler_params=_cparams(("parallel",), 24 * seq * D_CONV * 4),
    )(rest, rest, rest, cw)


def _conv_bwd(rest, dy, cw, seq):
    t = rest.shape[0]

    def body(cb_ref, cc_ref, ch_ref, dy_ref, w_ref, o_ref, dw_ref):
        i = pl.program_id(0)
        cc = cc_ref[...]
        ch = ch_ref[...]
        u = cc * ch
        u1 = _shift_down(u, 1)
        u2 = _shift_down(u, 2)
        y = w_ref[0:1, :] * u2 + w_ref[1:2, :] * u1 + w_ref[2:3, :] * u
        dyc = dy_ref[...]
        d2 = dyc * cb_ref[...]
        du = w_ref[0:1, :] * _shift_up(d2, 2) + w_ref[1:2, :] * _shift_up(d2, 1) + w_ref[2:3, :] * d2
        o_ref[:, 0:D_CONV] = (dyc * y).astype(o_ref.dtype)
        o_ref[:, D_CONV:2 * D_CONV] = (du * ch).astype(o_ref.dtype)
        o_ref[:, 2 * D_CONV:3 * D_CONV] = (du * cc).astype(o_ref.dtype)
        tap = lax.broadcasted_iota(jnp.int32, (8, D_CONV), 0)
        dwp = jnp.where(tap == 0, jnp.sum(d2 * u2, axis=0, keepdims=True),
                        jnp.where(tap == 1, jnp.sum(d2 * u1, axis=0, keepdims=True),
                                  jnp.where(tap == 2, jnp.sum(d2 * u, axis=0, keepdims=True), 0.0)))

        @pl.when(i == 0)
        def _():
            dw_ref[...] = dwp

        @pl.when(i > 0)
        def _():
            dw_ref[...] += dwp

    def col(c):
        return pl.BlockSpec((seq, D_CONV), lambda b, c=c: (b, c))

    taps = pl.BlockSpec((8, D_CONV), lambda b: (0, 0))
    return pl.pallas_call(
        body, grid=(t // seq,), in_specs=[col(1), col(2), col(3), col(3), taps],
        out_specs=[pl.BlockSpec((seq, 3 * D_CONV), lambda b: (b, 0)), taps],
        out_shape=[jax.ShapeDtypeStruct((t, 3 * D_CONV), BF16), jax.ShapeDtypeStruct((8, D_CONV), F32)],
        name="conv_bwd", compiler_params=_cparams(("arbitrary",), 48 * seq * D_CONV * 4),
    )(rest, rest, rest, dy, cw)


def _adamw(w, g, m, v, name):
    n, r, c = w.shape
    tr = _pick(r, (512, 352, 256, 128)) if r > 512 else r

    def body(w_ref, g_ref, m_ref, v_ref, d_ref, mo_ref, vo_ref):
        gv = g_ref[...]
        mn = ADAM_B1 * m_ref[...] + (1.0 - ADAM_B1) * gv
        vn = ADAM_B2 * v_ref[...] + (1.0 - ADAM_B2) * (gv * gv)
        m_hat = mn / (1.0 - ADAM_B1 ** ADAM_STEP)
        v_hat = vn / (1.0 - ADAM_B2 ** ADAM_STEP)
        d_ref[...] = -ADAM_LR * (m_hat / (jnp.sqrt(v_hat) + ADAM_EPS) + ADAM_WD * w_ref[...])
        mo_ref[...] = mn
        vo_ref[...] = vn

    blk = pl.BlockSpec((1, tr, c), lambda a, i: (a, i, 0))
    sds = jax.ShapeDtypeStruct((n, r, c), F32)
    return pl.pallas_call(
        body, grid=(n, r // tr), in_specs=[blk] * 4, out_specs=[blk] * 3, out_shape=[sds] * 3, name=name,
        compiler_params=_cparams(("parallel", "parallel"), 20 * tr * max(c, LANES) * 4),
    )(w, g, m, v)


def _sum_slots(a, name):
    ns, r, c = a.shape
    tr = _pick(r, (384, 368, 256, 184, 136, 128, 88, 8))

    def body(a_ref, o_ref):
        acc = a_ref[0].astype(F32)
        for s in range(1, ns):
            acc = acc + a_ref[s].astype(F32)
        o_ref[...] = acc

    return pl.pallas_call(
        body, grid=(r // tr,), in_specs=[pl.BlockSpec((ns, tr, c), lambda i: (0, i, 0))],
        out_specs=pl.BlockSpec((tr, c), lambda i: (i, 0)), out_shape=jax.ShapeDtypeStruct((r, c), F32), name=name,
        compiler_params=_cparams(("parallel",), 4 * (ns + 2) * tr * c * 4),
    )(a)


def _add_core_half(core, g4, theirs, out_dtype, name):
    ns, _, r, c = g4.shape
    tr = _pick(r, (384, 368, 256, 184, 136, 128, 88, 8))

    def body(core_ref, a_ref, b_ref, o_ref):
        o_ref[...] = (a_ref[0] + b_ref[...]).astype(o_ref.dtype)

    blk = pl.BlockSpec((1, tr, c), lambda s, i, core_ref: (s, i, 0))
    return pl.pallas_call(
        body,
        grid_spec=pltpu.PrefetchScalarGridSpec(
            num_scalar_prefetch=1, grid=(ns, r // tr),
            in_specs=[pl.BlockSpec((1, 1, tr, c), lambda s, i, core_ref: (s, core_ref[0], i, 0)), blk],
            out_specs=blk),
        out_shape=jax.ShapeDtypeStruct((ns, r, c), out_dtype), name=name,
        compiler_params=_cparams(("parallel", "parallel"), 10 * tr * c * 4),
    )(core, g4, theirs)


def _sum_chips(order, own, landed, name):
    ns, r, c = own.shape
    tr = _pick(r, (384, 368, 256, 184, 136, 128, 88, 8))

    def body(order_ref, a_ref, b1_ref, b2_ref, b3_ref, o_ref):
        o_ref[...] = ((a_ref[0].astype(F32) + b1_ref[0].astype(F32)) + b2_ref[0].astype(F32)) + b3_ref[0].astype(F32)

    def slot(k):
        return pl.BlockSpec((1, tr, c), lambda i, order_ref, k=k: (order_ref[k], i, 0))

    return pl.pallas_call(
        body,
        grid_spec=pltpu.PrefetchScalarGridSpec(
            num_scalar_prefetch=1, grid=(r // tr,), in_specs=[slot(0), slot(1), slot(2), slot(3)],
            out_specs=pl.BlockSpec((tr, c), lambda i, order_ref: (i, 0))),
        out_shape=jax.ShapeDtypeStruct((r, c), F32), name=name,
        compiler_params=_cparams(("parallel",), 16 * tr * c * 4),
    )(order, own, landed, landed, landed)


def _mesh_pos():
    return lax.axis_index("x"), lax.axis_index("y"), lax.axis_index("c")


def _comm_call(name, gathers=(), chips=None):
    payloads = list(gathers) + ([] if chips is None else [chips])
    n = len(payloads)

    def body(*refs):
        ins, outs, sems = refs[:n], refs[n:2 * n], refs[2 * n:]
        jobs = [(_gather_start, _gather_finish)] * len(gathers) + ([] if chips is None else [(_chips_start, _chips_finish)])
        for k, (start, _) in enumerate(jobs):
            start(ins[k], outs[k], sems[2 * k], sems[2 * k + 1])
        for k, (_, finish) in enumerate(jobs):
            finish(ins[k], outs[k], sems[2 * k], sems[2 * k + 1])

    any_spec = pl.BlockSpec(memory_space=pl.ANY)
    out_shape = [jax.ShapeDtypeStruct((N_DEV,) + x.shape, x.dtype) for x in gathers]
    sems = list(GATHER_SEMS) * len(gathers)
    if chips is not None:
        out_shape.append(jax.ShapeDtypeStruct(chips.shape, chips.dtype))
        sems += list(CHIPS_SEMS)
    outs = pl.pallas_call(body, out_shape=out_shape, in_specs=[any_spec] * n, out_specs=[any_spec] * n,
                          scratch_shapes=sems, name=name)(*payloads)
    return [_fill_own_slot(o, x) for o, x in zip(outs, gathers)] + ([] if chips is None else [outs[-1]])


GATHER_SEMS = (pltpu.SemaphoreType.DMA((7,)), pltpu.SemaphoreType.DMA((7,)))


def _fill_own_slot(gathered, x):
    mx, my, mc = _mesh_pos()
    return lax.dynamic_update_slice_in_dim(gathered, x[None], 4 * mx + 2 * my + mc, axis=0)


def _gather_copies(x_ref, out_ref, send_sems, recv_sems):
    mx, my, mc = _mesh_pos()
    me, sibling = (mx, my, mc), (mx, my, 1 - mc)
    chips = [(1 - mx, my), (mx, 1 - my), (1 - mx, 1 - my)]

    def slot(px, py, pc):
        return out_ref.at[4 * px + 2 * py + pc]

    def copy(k, block, to, src=None):
        return pltpu.make_async_remote_copy(
            src_ref=slot(*block) if src is None else src, dst_ref=slot(*block),
            send_sem=send_sems.at[k], recv_sem=recv_sems.at[k],
            device_id=to, device_id_type=pl.DeviceIdType.MESH)

    first = [copy(0, me, sibling, src=x_ref)]
    first += [copy(1 + j, me, (*chip, mc), src=x_ref) for j, chip in enumerate(chips)]
    passed = [copy(4 + j, (*chip, mc), sibling) for j, chip in enumerate(chips)]
    over_ici = [copy(1 + j, (*chip, mc), me) for j, chip in enumerate(chips)]
    over_d2d = [copy(0, sibling, me)] + [copy(4 + j, (*chip, 1 - mc), me) for j, chip in enumerate(chips)]
    return first, passed, over_ici, over_d2d


def _gather_start(x_ref, out_ref, send_sems, recv_sems):
    for cp in _gather_copies(x_ref, out_ref, send_sems, recv_sems)[0]:
        cp.start()


def _gather_finish(x_ref, out_ref, send_sems, recv_sems):
    first, passed, over_ici, over_d2d = _gather_copies(x_ref, out_ref, send_sems, recv_sems)
    for landed, relay in zip(over_ici, passed):
        landed.wait_recv()
        relay.start()
    for landed in over_d2d:
        landed.wait_recv()
    for cp in first + passed:
        cp.wait_send()


def _exchange_sibling(g4, name):
    nchip, _, r, c = g4.shape

    def body(g_ref, theirs_ref, send_sems, recv_sems):
        _sibling_start(g_ref, theirs_ref, send_sems, recv_sems)
        _sibling_finish(g_ref, theirs_ref, send_sems, recv_sems)

    any_spec = pl.BlockSpec(memory_space=pl.ANY)
    return pl.pallas_call(
        body, out_shape=jax.ShapeDtypeStruct((nchip, r, c), g4.dtype), in_specs=[any_spec], out_specs=any_spec,
        scratch_shapes=list(SIBLING_SEMS), name=name,
    )(g4)


SIBLING_SEMS = (pltpu.SemaphoreType.DMA((N_CHIPS,)), pltpu.SemaphoreType.DMA((N_CHIPS,)))


def _sibling_copies(g_ref, theirs_ref, send_sems, recv_sems):
    mx, my, mc = _mesh_pos()
    return [pltpu.make_async_remote_copy(
        src_ref=g_ref.at[chip, 1 - mc], dst_ref=theirs_ref.at[chip],
        send_sem=send_sems.at[chip], recv_sem=recv_sems.at[chip],
        device_id=(mx, my, 1 - mc), device_id_type=pl.DeviceIdType.MESH) for chip in range(N_CHIPS)]


def _sibling_start(g_ref, theirs_ref, send_sems, recv_sems):
    for cp in _sibling_copies(g_ref, theirs_ref, send_sems, recv_sems):
        cp.start()


def _sibling_finish(g_ref, theirs_ref, send_sems, recv_sems):
    copies = _sibling_copies(g_ref, theirs_ref, send_sems, recv_sems)
    for cp in copies:
        cp.wait_recv()
    for cp in copies:
        cp.wait_send()


CHIPS_SEMS = (pltpu.SemaphoreType.DMA((N_CHIPS - 1,)), pltpu.SemaphoreType.DMA((N_CHIPS - 1,)))


def _chips_copies(t_ref, out_ref, send_sems, recv_sems):
    mx, my, mc = _mesh_pos()
    my_chip = 2 * mx + my
    copies = []
    for k in range(1, N_CHIPS):
        px = 1 - mx if k & 2 else mx
        py = 1 - my if k & 1 else my
        peer_chip = 2 * px + py

        def rdma(dst_slot, px=px, py=py, peer_chip=peer_chip, k=k):
            return pltpu.make_async_remote_copy(
                src_ref=t_ref.at[peer_chip], dst_ref=out_ref.at[dst_slot],
                send_sem=send_sems.at[k - 1], recv_sem=recv_sems.at[k - 1],
                device_id=(px, py, mc), device_id_type=pl.DeviceIdType.MESH)

        copies.append((rdma(my_chip), rdma(peer_chip)))
    return copies


def _chips_start(t_ref, out_ref, send_sems, recv_sems):
    for send, _ in _chips_copies(t_ref, out_ref, send_sems, recv_sems):
        send.start()


def _chips_finish(t_ref, out_ref, send_sems, recv_sems):
    copies = _chips_copies(t_ref, out_ref, send_sems, recv_sems)
    for _, landed in copies:
        landed.wait_recv()
    for send, _ in copies:
        send.wait_send()


def _add_sibling(g4, theirs):
    core = jnp.reshape(lax.axis_index("c"), (1,)).astype(jnp.int32)
    return _add_core_half(core, g4, theirs, BF16, name="add_sibling_grads")


def _sum_landed(chip_sums, landed):
    mx, my, _ = _mesh_pos()
    order = jnp.stack([2 * mx + my, 2 * (1 - mx) + my, 2 * mx + (1 - my), 2 * (1 - mx) + (1 - my)]).astype(jnp.int32)
    return _sum_chips(order, chip_sums, landed, name="sum_grads")


def _perm_mix_rows(wt):
    f0 = D_QKV
    f1 = f0 + N_HEADS
    return jnp.concatenate([wt[:f0], wt[f1:], jnp.pad(wt[f0:f1], ((0, LANES - N_HEADS), (0, 0)))], axis=0)


def _unperm_mix_rows(gt):
    f0 = D_QKV
    return jnp.concatenate([gt[:f0], gt[f0 + D_REST:f0 + D_REST + N_HEADS], gt[f0:f0 + D_REST]], axis=0)


def _pack_shards(parts, l, dtype):
    w1i, w1o, wmi, wmo, w2i, w2o = parts
    rows = [w1i[l].T, w1o[l], jnp.pad(wmi[l].T, ((0, MIX_ROWS_PAD - MIX_ROWS), (0, 0))), wmo[l], w2i[l].T, w2o[l]]
    return jnp.concatenate(rows, axis=0).astype(dtype)


PACK_HEAD = FFN_ROWS + OUT_ROWS


def _ffn_weights(wg, o):
    return dict(wi_t=wg[:, o:o + FFN_ROWS].reshape(2 * D_FF, D_MODEL),
                wo=wg[:, o + FFN_ROWS:o + FFN_ROWS + OUT_ROWS].reshape(D_FF, D_MODEL))


def _tail_weights(wg):
    mix = dict(wm_t=_perm_mix_rows(wg[:, :MIX_ROWS].reshape(D_IN, D_MODEL)),
               wo=wg[:, MIX_ROWS_PAD:MIX_ROWS_PAD + MO_ROWS].reshape(D_MODEL, D_MODEL))
    return mix, _ffn_weights(wg, MIX_ROWS_PAD + MO_ROWS)


GRAD_AT = dict(f1i=0, f2i=FFN_ROWS, f1o=4 * OUT_ROWS, f2o=5 * OUT_ROWS, mi=6 * OUT_ROWS, mo=20 * MO_ROWS)
GRAD_ROWS = GRAD_AT["mo"] + MO_ROWS
GRAD_SHAPE = (N_CHIPS, 2, GRAD_ROWS, D_MODEL)


def _into_ffn_in(buf, tag, half):
    rb = GRAD_AT[tag] // FFN_ROWS
    return (buf, GRAD_SHAPE, (1, 2, FFN_ROWS, D_MODEL), lambda i: (2 * half + i, 0, rb, 0))


def _into_ffn_out(buf, tag):
    rb = GRAD_AT[tag] // OUT_ROWS
    return (buf, GRAD_SHAPE, (2, 2, OUT_ROWS, D_MODEL), lambda i: (i, 0, rb, 0))


def _into_mix_out(buf):
    rb = GRAD_AT["mo"] // MO_ROWS
    return (buf, GRAD_SHAPE, (N_CHIPS, 2, MO_ROWS, D_MODEL), lambda i: (0, 0, rb, 0))


def _put_mix_in(buf, g_in_t):
    gmi = _unperm_mix_rows(g_in_t).reshape(N_DEV, MIX_ROWS, D_MODEL)
    gmi = jnp.pad(gmi, ((0, 0), (0, OUT_ROWS - MIX_ROWS), (0, 0))).reshape(N_CHIPS, 2, OUT_ROWS, D_MODEL)
    return lax.dynamic_update_slice(buf, gmi, (0, 0, GRAD_AT["mi"], 0))


def _out_proj(a, w, x, alpha, next_gain, name):
    if next_gain is None:
        return _mm_nn(a, w, out_dtype=F32, res=x, alpha=alpha, name=name), None
    return _mm_nn(a, w, out_dtype=F32, res=x, alpha=alpha, next_gain=next_gain, name=name + "_norm")


def _ffn_forward(x, xn, w, next_gain, carry_gather=None):
    res = _ffn_in(xn, w["wi_t"], name="ffn_in", carry_gather=carry_gather)
    h, pg, pu = res[:3]
    x_new, xn_next = _out_proj(h, w["wo"], x, 0.5, next_gain, "ffn_out")
    out = (x_new, xn_next, dict(x=x, xn=xn, h=h, pg=pg, pu=pu))
    return out if carry_gather is None else out + (_fill_own_slot(res[3], carry_gather),)


def _ffn_backward(dxo, dxo_b, gain, w, saved, gbuf, tag, carry_sibling=None):
    res = _ffn_bwd_mid(dxo_b, w["wo"], saved["pg"], saved["pu"], name="ffn_bwd_mid", carry_sibling=carry_sibling)
    dzg, dzu = res[:2]
    theirs = None if carry_sibling is None else res[2]
    gbuf = _mm_tn(saved["h"], dxo_b, alpha=0.5, tm=F_HALF, name="ffn_gw_out", into=_into_ffn_out(gbuf, tag + "o"))
    gbuf = _mm_tn(dzg, saved["xn"], tm=F_HALF, name="ffn_gw_in", into=_into_ffn_in(gbuf, tag + "i", 0))
    gbuf = _mm_tn(dzu, saved["xn"], tm=F_HALF, name="ffn_gw_in", into=_into_ffn_in(gbuf, tag + "i", 1))
    dx, dx_b, dg = _dxn_norm_bwd([dzg, dzu], w["wi_t"], saved["x"], gain, dxo, name="ffn_dxn_norm_bwd")
    return dx, dx_b, dg, gbuf, theirs


def _mixer_forward(x, xn, p, w, nb, seq, ta, next_gain, next_pack=None):
    qkv, rest, fl = _mix_proj(xn, w["wm_t"])
    qa, ka, vm = _fox_prep(fl, p["bf"], qkv, nb, seq)
    if next_pack is None:
        (y_attn, lse), next_gathered = _fox_fwd(qa, ka, vm, nb, seq, ta), None
    else:
        y_attn, lse, next_gathered = _fox_fwd(qa, ka, vm, nb, seq, ta, carry_gather=next_pack)
        next_gathered = _fill_own_slot(next_gathered, next_pack)
    y_pool = _pool_fwd(rest, p["wbd"], p["scale"], seq)
    y_conv = _conv_fwd(rest, p["cw"], seq)
    y = jnp.concatenate([y_attn, y_pool, y_conv], axis=1)
    x_new, xn_next = _out_proj(y, w["wo"], x, 1.0, next_gain, "mix_out")
    return x_new, xn_next, dict(x=x, xn=xn, qa=qa, ka=ka, vm=vm, rest=rest, fl=fl, lse=lse, y=y), next_gathered


def _mixer_backward(dxo, dxo_b, p, w, sv, nb, seq, ta, gbuf, pending=None):
    t = dxo.shape[0]
    dy = _mm_nt(dxo_b, w["wo"], out_dtype=F32, name="mix_dy")
    gbuf = _mm_tn(sv["y"], dxo_b, name="mix_gw_out", into=_into_mix_out(gbuf))
    res = _fox_bwd(sv["qa"], sv["ka"], sv["vm"], sv["y"], dy, sv["lse"], nb, seq, ta, carry_exchange=pending)
    dq, dk, dv, d_rows, d_cols = res[:5]
    landed = None if pending is None else res[5]
    ddh = (d_rows.reshape(nb, N_HEADS, seq) - d_cols.reshape(nb, N_HEADS, seq)).transpose(0, 2, 1)
    ddh = ddh.reshape(t, N_HEADS)
    dfl, dbf = _fox_prep_bwd(jnp.pad(ddh, ((0, 0), (0, LANES - N_HEADS))), sv["fl"], p["bf"], seq)
    dpool, dwbd, dscale = _pool_bwd(sv["rest"], dy, p["wbd"], p["wbd_t"], p["scale"], seq)
    dconv, dcw = _conv_bwd(sv["rest"], dy, p["cw"], seq)
    dproj = jnp.concatenate([dq, dk.astype(BF16), dv.astype(BF16), dpool, dconv, dfl], axis=1)
    gbuf = _put_mix_in(gbuf, _mm_tn(dproj, sv["xn"], tm=D_INP // 3, name="mix_gw_in"))
    dx, dx_b, dg = _dxn_norm_bwd([dproj], w["wm_t"], sv["x"], p["norm"], dxo, name="mix_dxn_norm_bwd")
    return dx, dx_b, dict(norm=dg, bf=dbf, wbd=dwbd, scale=dscale, cw=dcw), gbuf, landed


def _block_diag(wp):
    z = jnp.zeros((POOL_GROUP, POOL_GROUP), wp.dtype)
    return jnp.concatenate(
        [jnp.concatenate([wp[g] if g == r else z for g in range(4)], axis=1) for r in range(4)], axis=0)


def _row_pad(a, rows):
    a = a.reshape(-1, a.shape[-1])
    return jnp.pad(a, ((0, rows - a.shape[0]), (0, 0)))


def kernel(x, norm_ffn1, w_ffn1_in, w_ffn1_out, norm_mix, w_mix_in, b_forget, w_pool, pool_scale, conv_w, w_mix_out, norm_ffn2, w_ffn2_in, w_ffn2_out, norm_final, loss_target, m_norm_ffn1, m_w_ffn1_in, m_w_ffn1_out, m_norm_mix, m_w_mix_in, m_b_forget, m_w_pool, m_pool_scale, m_conv_w, m_w_mix_out, m_norm_ffn2, m_w_ffn2_in, m_w_ffn2_out, m_norm_final, v_norm_ffn1, v_w_ffn1_in, v_w_ffn1_out, v_norm_mix, v_w_mix_in, v_b_forget, v_w_pool, v_pool_scale, v_conv_w, v_w_mix_out, v_norm_ffn2, v_w_ffn2_in, v_w_ffn2_out, v_norm_final):
    nb, seq, d = x.shape
    depth = norm_ffn1.shape[0]
    t = nb * seq
    ta = _pick(seq, (ATT_TILE, 128))
    my_id = 4 * lax.axis_index("x") + 2 * lax.axis_index("y") + lax.axis_index("c")
    cshard = conv_w.shape[-1]

    shards = (w_ffn1_in, w_ffn1_out, w_mix_in, w_mix_out, w_ffn2_in, w_ffn2_out)
    pack0 = _pack_shards(shards, 0, BF16)
    wg_head, cw_g = _comm_call("gather_weights_and_taps", gathers=[
        pack0[:PACK_HEAD], _row_pad(conv_w.reshape(depth * 3, cshard), 16).reshape(4, LANES)])
    cw_all = cw_g.reshape(N_DEV, 16, cshard)[:, :depth * 3].reshape(N_DEV, depth, 3, cshard)
    cw_all = cw_all.transpose(1, 2, 0, 3).reshape(depth, 3, D_CONV)

    xs = x.reshape(t, d)
    xn = _rmsnorm_fwd(xs, norm_ffn1[0][None], name="first_norm")
    saved = []
    for l in range(depth):
        wbd = _block_diag(w_pool[l])
        p = dict(norm=norm_mix[l][None], bf=jnp.pad(b_forget[l], (0, LANES - N_HEADS))[None],
                 wbd=wbd.astype(BF16), wbd_t=wbd.T.astype(BF16), scale=pool_scale[l][None],
                 cw=_row_pad(cw_all[l], 8))
        w = dict(f1=_ffn_weights(wg_head, 0))
        if l == 0:
            xs, xn, s1, wg_tail = _ffn_forward(xs, xn, w["f1"], norm_mix[l][None], carry_gather=pack0[PACK_HEAD:])
        else:
            xs, xn, s1 = _ffn_forward(xs, xn, w["f1"], norm_mix[l][None])
        w["mix"], w["f2"] = _tail_weights(wg_tail)
        next_pack = _pack_shards(shards, l + 1, BF16) if l + 1 < depth else None
        xs, xn, sm, wg = _mixer_forward(xs, xn, p, w["mix"], nb, seq, ta, norm_ffn2[l][None], next_pack)
        if wg is not None:
            wg_head, wg_tail = wg[:, :PACK_HEAD], wg[:, PACK_HEAD:]
        xs, xn, s2 = _ffn_forward(xs, xn, w["f2"], norm_ffn1[l + 1][None] if l + 1 < depth else None)
        saved.append((w, p, s1, sm, s2))

    dx, dx_b, g_norm_final, loss_part = _final_loss_bwd(xs, norm_final[None], loss_target.reshape(t, d))
    layer_g = [None] * depth
    small = [None] * depth
    rows_g4 = None
    for l in reversed(range(depth)):
        w, p, s1, sm, s2 = saved[l]
        dx, dx_b, dg2, gbuf, theirs = _ffn_backward(dx, dx_b, norm_ffn2[l][None], w["f2"], s2, None, "f2", rows_g4)
        chip_sums = None if rows_g4 is None else _add_sibling(rows_g4, theirs)
        dx, dx_b, gm, gbuf, landed = _mixer_backward(dx, dx_b, p, w["mix"], sm, nb, seq, ta, gbuf, chip_sums)
        if chip_sums is not None:
            layer_g[l + 1] = _sum_landed(chip_sums, landed)
        dx, dx_b, dg1, rows_g4, _ = _ffn_backward(dx, dx_b, norm_ffn1[l][None], w["f1"], s1, gbuf, "f1")
        small[l] = dict(n1=dg1, nm=gm["norm"], n2=dg2, bf=gm["bf"], wbd=gm["wbd"], scale=gm["scale"], cw=gm["cw"])
    grad_x = dx.reshape(nb, seq, d)
    chip_sums = _add_sibling(rows_g4, _exchange_sibling(rows_g4, name="exchange_grads_sibling"))

    def tile8(a):
        return jnp.pad(a, ((0, 8 - a.shape[0]), (0, D_MODEL - a.shape[1])))

    rows = []
    for l in range(depth):
        s = small[l]
        wp_rows = jnp.stack([s["wbd"][POOL_GROUP * g:POOL_GROUP * (g + 1), POOL_GROUP * g:POOL_GROUP * (g + 1)]
                             for g in range(4)]).reshape(16, D_MODEL)
        rows += [tile8(s["n1"]), tile8(s["nm"]), tile8(s["n2"]), tile8(s["bf"]), tile8(s["scale"]), tile8(s["cw"]),
                 wp_rows]
    rows += [tile8(g_norm_final), tile8(loss_part)]
    small_gathered, landed = _comm_call("exchange_grads_chips_gather_small", gathers=[jnp.concatenate(rows, axis=0)],
                                        chips=chip_sums)
    layer_g[0] = _sum_landed(chip_sums, landed)

    pieces = {}
    for nm, n in (("f1i", FFN_ROWS), ("f1o", OUT_ROWS), ("mi", MIX_ROWS), ("mo", MO_ROWS), ("f2i", FFN_ROWS),
                  ("f2o", OUT_ROWS)):
        pieces[nm] = jnp.stack([g[GRAD_AT[nm]:GRAD_AT[nm] + n] for g in layer_g])
    g_sharded = dict(
        w_ffn1_in=pieces["f1i"].transpose(0, 2, 1), w_ffn1_out=pieces["f1o"],
        w_mix_in=pieces["mi"].transpose(0, 2, 1), w_mix_out=pieces["mo"],
        w_ffn2_in=pieces["f2i"].transpose(0, 2, 1), w_ffn2_out=pieces["f2o"])

    per_layer = 6 * 8 + 16
    small_sum = _sum_slots(small_gathered, name="sum_small_grads")
    lay = small_sum[:depth * per_layer].reshape(depth, per_layer, D_MODEL)
    g_small = dict(
        norm_ffn1=lay[:, 0], norm_mix=lay[:, 8], norm_ffn2=lay[:, 16], b_forget=lay[:, 24, :N_HEADS],
        pool_scale=lay[:, 32, :D_POOL],
        conv_w=lax.dynamic_slice_in_dim(lay[:, 40:43, :D_CONV], my_id * cshard, cshard, axis=2),
        w_pool=lay[:, 48:64].reshape(depth, 4, POOL_GROUP, POOL_GROUP),
        norm_final=small_sum[depth * per_layer])
    loss = small_sum[depth * per_layer + 8, 0]

    given = dict(norm_ffn1=(norm_ffn1, m_norm_ffn1, v_norm_ffn1), w_ffn1_in=(w_ffn1_in, m_w_ffn1_in, v_w_ffn1_in),
                 w_ffn1_out=(w_ffn1_out, m_w_ffn1_out, v_w_ffn1_out), norm_mix=(norm_mix, m_norm_mix, v_norm_mix),
                 w_mix_in=(w_mix_in, m_w_mix_in, v_w_mix_in), b_forget=(b_forget, m_b_forget, v_b_forget),
                 w_pool=(w_pool, m_w_pool, v_w_pool), pool_scale=(pool_scale, m_pool_scale, v_pool_scale),
                 conv_w=(conv_w, m_conv_w, v_conv_w), w_mix_out=(w_mix_out, m_w_mix_out, v_w_mix_out),
                 norm_ffn2=(norm_ffn2, m_norm_ffn2, v_norm_ffn2), w_ffn2_in=(w_ffn2_in, m_w_ffn2_in, v_w_ffn2_in),
                 w_ffn2_out=(w_ffn2_out, m_w_ffn2_out, v_w_ffn2_out), norm_final=(norm_final, m_norm_final, v_norm_final))
    names = list(given)
    grads, deltas, new_m, new_v = {}, {}, {}, {}
    for nm in names:
        wv, mv, vv = given[nm]
        gv = (g_sharded[nm] if nm in g_sharded else g_small[nm]).reshape(wv.shape)
        shape3 = (-1,) + wv.shape[-2:] if wv.ndim > 2 else (1, -1, wv.shape[-1])
        dl, mn, vn = _adamw(wv.reshape(shape3), gv.reshape(shape3), mv.reshape(shape3), vv.reshape(shape3),
                            name="adamw_" + nm)
        grads[nm], deltas[nm], new_m[nm], new_v[nm] = gv, dl.reshape(wv.shape), mn.reshape(wv.shape), vn.reshape(wv.shape)
    return (loss, grad_x, *[grads[n] for n in names], *[deltas[n] for n in names],
            *[new_m[n] for n in names], *[new_v[n] for n in names])
```

```python
import functools

import jax
import jax.numpy as jnp
from jax import lax
from jax.experimental import pallas as pl
from jax.experimental.pallas import tpu as pltpu

F32 = jnp.float32
BF16 = jnp.bfloat16

D_MODEL = 1024
D_FF = 2816
HEAD_DIM = 64
N_HEADS = 8
N_PAIRS = N_HEADS // 2
D_ATTN = 512
D_POOL = 256
D_CONV = 256
POOL_WINDOWS = (2, 4, 8, 16)
POOL_GROUP = 64
D_IN = 2568
RMS_EPS = 1e-6
ADAM_LR, ADAM_B1, ADAM_B2, ADAM_EPS, ADAM_WD, ADAM_STEP = 0.001, 0.9, 0.999, 1e-08, 0.01, 10

N_DEV = 8
N_CHIPS = 4
LANES = 128
VMEM_BYTES_V7X = 64 * 1024 * 1024
VMEM_LIMIT_MAX = VMEM_BYTES_V7X - 8 * 1024 * 1024

F_HALF = D_FF // 2
D_QKV = 3 * D_ATTN
D_REST = D_POOL + 3 * D_CONV
D_INP = D_QKV + D_REST + LANES
MIX_ROWS = 321
MIX_ROWS_PAD = 336
FFN_ROWS = 704
OUT_ROWS = 352
MO_ROWS = 128
LAYER_ROWS = 2 * (FFN_ROWS + OUT_ROWS) + MIX_ROWS_PAD + MO_ROWS
NEG_BIG = -1e30
ATT_SCALE = HEAD_DIM ** -0.5
ATT_K = 2 * LANES
ATT_TILE = 256
ATT_PAIRS_FWD = 4
ATT_PAIRS_BWD = 4
MXU_COLS = 256


def _cparams(sem, vmem_bytes):
    limit = int(min(max(vmem_bytes, 16 * 1024 * 1024), VMEM_LIMIT_MAX))
    return pltpu.CompilerParams(dimension_semantics=sem, vmem_limit_bytes=limit)


def _nbytes(shape, dtype):
    n = 1
    for s in shape:
        n *= s
    return n * jnp.dtype(dtype).itemsize


def _pick(n, prefs):
    for p in prefs:
        if n % p == 0:
            return p
    return n


def _rmsnorm_fwd(x, g, name):
    t, d = x.shape
    tm = _pick(t, (512, 256, 128))

    def body(x_ref, g_ref, o_ref):
        xv = x_ref[...]
        r = lax.rsqrt(jnp.mean(xv * xv, axis=-1, keepdims=True) + RMS_EPS)
        o_ref[...] = ((xv * r) * g_ref[...]).astype(o_ref.dtype)

    return pl.pallas_call(
        body, grid=(t // tm,),
        in_specs=[pl.BlockSpec((tm, d), lambda i: (i, 0)), pl.BlockSpec((1, d), lambda i: (0, 0))],
        out_specs=pl.BlockSpec((tm, d), lambda i: (i, 0)),
        out_shape=jax.ShapeDtypeStruct((t, d), BF16), name=name,
        compiler_params=_cparams(("parallel",), 6 * tm * d * 4),
    )(x, g)


def _mm_nn(a, b, *, out_dtype, name, res=None, alpha=1.0, tn=None, next_gain=None):
    m, k = a.shape
    n = b.shape[1]
    tn = n if tn is None else tn
    tm = _pick(m, (512, 256, 128))
    with_res = res is not None
    with_norm = next_gain is not None
    assert not with_norm or tn == n

    def body(*refs):
        refs = list(refs)
        a_ref, b_ref = refs[:2]
        r_ref = refs[2] if with_res else None
        g_ref = refs[2 + with_res] if with_norm else None
        o_ref = refs[2 + with_res + with_norm]
        acc = jnp.dot(a_ref[...], b_ref[...], preferred_element_type=F32)
        if with_res:
            acc = r_ref[...] + alpha * acc
        o_ref[...] = acc.astype(o_ref.dtype)
        if with_norm:
            r = lax.rsqrt(jnp.mean(acc * acc, axis=-1, keepdims=True) + RMS_EPS)
            refs[-1][...] = ((acc * r) * g_ref[...]).astype(BF16)

    in_specs = [pl.BlockSpec((tm, k), lambda j, i: (i, 0)), pl.BlockSpec((k, tn), lambda j, i: (0, j))]
    args = [a, b]
    out_blk = pl.BlockSpec((tm, tn), lambda j, i: (i, j))
    out_specs, out_shape = [out_blk], [jax.ShapeDtypeStruct((m, n), out_dtype)]
    if with_res:
        in_specs.append(out_blk)
        args.append(res)
    if with_norm:
        in_specs.append(pl.BlockSpec((1, n), lambda j, i: (0, 0)))
        args.append(next_gain)
        out_specs.append(out_blk)
        out_shape.append(jax.ShapeDtypeStruct((m, n), BF16))
    vmem = 2 * (_nbytes((tm, k), BF16) + _nbytes((k, tn), BF16) + 4 * _nbytes((tm, tn), F32))
    outs = pl.pallas_call(
        body, grid=(n // tn, m // tm), in_specs=in_specs, out_specs=out_specs, out_shape=out_shape, name=name,
        compiler_params=_cparams(("parallel", "parallel"), vmem),
    )(*args)
    return outs if with_norm else outs[0]


def _mm_nt(a, b_t, *, out_dtype, name):
    m, k = a.shape
    n = b_t.shape[0]
    tm = _pick(m, (512, 256, 128))

    def body(a_ref, b_ref, o_ref):
        o_ref[...] = _nt(a_ref[...], b_ref[...]).astype(o_ref.dtype)

    vmem = 2 * (_nbytes((tm, k), BF16) + _nbytes((n, k), BF16) + 3 * _nbytes((tm, n), F32))
    return pl.pallas_call(
        body, grid=(m // tm,),
        in_specs=[pl.BlockSpec((tm, k), lambda i: (i, 0)), pl.BlockSpec((n, k), lambda i: (0, 0))],
        out_specs=pl.BlockSpec((tm, n), lambda i: (i, 0)), out_shape=jax.ShapeDtypeStruct((m, n), out_dtype),
        name=name, compiler_params=_cparams(("parallel",), vmem),
    )(a, b_t)


def _mix_proj(xn, wm_t):
    t, d = xn.shape
    tm = _pick(t, (512, 256, 128))

    def body(x_ref, w_ref, qkv_ref, rest_ref, f_ref):
        xv = x_ref[...]
        for c0, cw in _col_chunks(D_QKV, MXU_COLS):
            qkv_ref[:, c0:c0 + cw] = _nt(xv, w_ref[c0:c0 + cw, :]).astype(qkv_ref.dtype)
        for c0, cw in _col_chunks(D_REST, MXU_COLS):
            rest_ref[:, c0:c0 + cw] = _nt(xv, w_ref[D_QKV + c0:D_QKV + c0 + cw, :])
        f_ref[...] = _nt(xv, w_ref[D_QKV + D_REST:, :])

    def rows(n):
        return pl.BlockSpec((tm, n), lambda i: (i, 0))

    vmem = 2 * (_nbytes((tm, d), BF16) + _nbytes((D_INP, d), BF16) + 3 * _nbytes((tm, D_INP), F32))
    return pl.pallas_call(
        body, grid=(t // tm,), in_specs=[rows(d), pl.BlockSpec((D_INP, d), lambda i: (0, 0))],
        out_specs=[rows(D_QKV), rows(D_REST), rows(LANES)],
        out_shape=[jax.ShapeDtypeStruct((t, D_QKV), BF16), jax.ShapeDtypeStruct((t, D_REST), F32),
                   jax.ShapeDtypeStruct((t, LANES), F32)],
        name="mix_proj", compiler_params=_cparams(("parallel",), vmem),
    )(xn, wm_t)


def _mm_tn(a, b, *, name, alpha=1.0, tm=None, into=None):
    t, m = a.shape
    n = b.shape[1]
    tm = m if tm is None else tm
    tk = _pick(t, (2048, 1024, 512, 256, 128))
    nk = t // tk

    def body(a_ref, b_ref, *rest):
        o_ref = rest[-1]
        kk = pl.program_id(1)
        p = lax.dot_general(a_ref[...], b_ref[...], (((0,), (0,)), ((), ())), preferred_element_type=F32)
        if alpha != 1.0:
            p = alpha * p
        p = p.reshape(o_ref.shape)

        @pl.when(kk == 0)
        def _():
            o_ref[...] = p

        @pl.when(kk > 0)
        def _():
            o_ref[...] += p

    vmem = 2 * (_nbytes((tk, tm), BF16) + _nbytes((tk, n), BF16) + 2 * _nbytes((tm, n), F32))
    in_specs = [pl.BlockSpec((tk, tm), lambda i, kk: (kk, i)), pl.BlockSpec((tk, n), lambda i, kk: (kk, 0))]
    cp = _cparams(("parallel", "arbitrary"), vmem)
    if into is None:
        return pl.pallas_call(
            body, grid=(m // tm, nk), in_specs=in_specs, out_specs=pl.BlockSpec((tm, n), lambda i, kk: (i, 0)),
            out_shape=jax.ShapeDtypeStruct((m, n), F32), name=name, compiler_params=cp,
        )(a, b)
    buf, buf_shape, blk, index = into
    out_spec = pl.BlockSpec(blk, lambda i, kk: index(i))
    out_shape = jax.ShapeDtypeStruct(buf_shape, F32)
    if buf is None:
        return pl.pallas_call(body, grid=(m // tm, nk), in_specs=in_specs, out_specs=out_spec, out_shape=out_shape,
                              name=name + "_new", compiler_params=cp)(a, b)
    return pl.pallas_call(
        body, grid=(m // tm, nk), in_specs=in_specs + [pl.BlockSpec(memory_space=pl.ANY)], out_specs=out_spec,
        out_shape=out_shape, input_output_aliases={2: 0}, name=name + "_into", compiler_params=cp,
    )(a, b, buf)


def _sigmoid(v):
    return 1.0 / (1.0 + jnp.exp(-v))


def _col_chunks(n, width):
    return [(c, min(width, n - c)) for c in range(0, n, width)]


def _ffn_in(xn, w_t, name, carry_gather=None):
    t, d = xn.shape
    tm = _pick(t, (1024, 512, 256, 128))
    grid = (2, t // tm)

    def body(x_ref, wg_ref, wu_ref, *rest):
        if carry_gather is None:
            h_ref, pg_ref, pu_ref = rest
        else:
            blk_ref, h_ref, pg_ref, pu_ref, gathered_ref, send_sems, recv_sems = rest
            first_step, last_step = _grid_ends([pl.program_id(a) for a in range(2)], grid)

            @pl.when(first_step)
            def _():
                _gather_start(blk_ref, gathered_ref, send_sems, recv_sems)

        xv = x_ref[...]
        for c0, cw in _col_chunks(F_HALF, MXU_COLS):
            cols = slice(c0, c0 + cw)
            g = _nt(xv, wg_ref[cols, :])
            u = _nt(xv, wu_ref[cols, :])
            s = _sigmoid(g)
            silu = g * s
            h_ref[:, cols] = (silu * u).astype(h_ref.dtype)
            pg_ref[:, cols] = (u * (s * (1.0 + g * (1.0 - s)))).astype(pg_ref.dtype)
            pu_ref[:, cols] = silu.astype(pu_ref.dtype)

        if carry_gather is not None:
            @pl.when(last_step)
            def _():
                _gather_finish(blk_ref, gathered_ref, send_sems, recv_sems)

    vmem = 2 * (_nbytes((tm, d), BF16) + 2 * _nbytes((d, F_HALF), BF16) + 4 * _nbytes((tm, D_FF), F32))
    out_blk = pl.BlockSpec((tm, F_HALF), lambda j, i: (i, j))
    sds = jax.ShapeDtypeStruct((t, D_FF), BF16)
    in_specs = [pl.BlockSpec((tm, d), lambda j, i: (i, 0)), pl.BlockSpec((F_HALF, d), lambda j, i: (j, 0)),
                pl.BlockSpec((F_HALF, d), lambda j, i: (2 + j, 0))]
    if carry_gather is None:
        return pl.pallas_call(
            body, grid=grid, in_specs=in_specs, out_specs=[out_blk, out_blk, out_blk], out_shape=[sds, sds, sds],
            name=name, compiler_params=_cparams(("parallel", "parallel"), vmem),
        )(xn, w_t, w_t)
    any_spec = pl.BlockSpec(memory_space=pl.ANY)
    return pl.pallas_call(
        body, grid=grid, in_specs=in_specs + [any_spec], out_specs=[out_blk, out_blk, out_blk, any_spec],
        out_shape=[sds, sds, sds, jax.ShapeDtypeStruct((N_DEV,) + carry_gather.shape, carry_gather.dtype)],
        scratch_shapes=list(GATHER_SEMS), name=name + "_gather",
        compiler_params=_cparams(("arbitrary", "arbitrary"), vmem),
    )(xn, w_t, w_t, carry_gather)


def _ffn_bwd_mid(dxo, w_out, pg, pu, name):
    t, d = dxo.shape
    tm = _pick(t, (1024, 512, 256, 128))

    def body(d_ref, w_ref, pg_ref, pu_ref, dg_ref, du_ref):
        dv = d_ref[...]
        for c0, cw in _col_chunks(F_HALF, MXU_COLS):
            cols = slice(c0, c0 + cw)
            dh = 0.5 * _nt(dv, w_ref[cols, :])
            dg_ref[:, cols] = (dh * pg_ref[:, cols].astype(F32)).astype(dg_ref.dtype)
            du_ref[:, cols] = (dh * pu_ref[:, cols].astype(F32)).astype(du_ref.dtype)

    vmem = 2 * (_nbytes((tm, d), BF16) + _nbytes((d, F_HALF), BF16) + 5 * _nbytes((tm, D_FF), F32))
    blk = pl.BlockSpec((tm, F_HALF), lambda j, i: (i, j))
    sds = jax.ShapeDtypeStruct((t, D_FF), BF16)
    return pl.pallas_call(
        body, grid=(2, t // tm),
        in_specs=[pl.BlockSpec((tm, d), lambda j, i: (i, 0)), pl.BlockSpec((F_HALF, d), lambda j, i: (j, 0)), blk, blk],
        out_specs=[blk, blk], out_shape=[sds, sds], name=name,
        compiler_params=_cparams(("parallel", "parallel"), vmem),
    )(dxo, w_out, pg, pu)


def _dxn_norm_bwd(parts, b, x, g, dxo, name, carry_sibling=None):
    t, d = x.shape
    k = parts[0].shape[1]
    n_parts = len(parts)
    tm = _pick(t, (256, 128))
    grid = (t // tm,)

    def body(*refs):
        a_refs, b_refs = refs[:n_parts], refs[n_parts:2 * n_parts]
        if carry_sibling is None:
            x_ref, g_ref, do_ref, dx_ref, dxb_ref, dg_ref = refs[2 * n_parts:]
        else:
            x_ref, g_ref, do_ref, g4_ref, dx_ref, dxb_ref, dg_ref, theirs_ref, send_sems, recv_sems = refs[2 * n_parts:]
            first_step, last_step = _grid_ends([pl.program_id(0)], grid)

            @pl.when(first_step)
            def _():
                _sibling_start(g4_ref, theirs_ref, send_sems, recv_sems)

        i = pl.program_id(0)
        dn = jnp.dot(a_refs[0][...], b_refs[0][...], preferred_element_type=F32)
        for a_ref, b_ref in zip(a_refs[1:], b_refs[1:]):
            dn = dn + jnp.dot(a_ref[...], b_ref[...], preferred_element_type=F32)
        xv = x_ref[...]
        r = lax.rsqrt(jnp.mean(xv * xv, axis=-1, keepdims=True) + RMS_EPS)
        xh = xv * r
        dgp = jnp.sum(dn * xh, axis=0, keepdims=True)
        dh = dn * g_ref[...]
        dx = do_ref[...] + r * (dh - xh * jnp.mean(dh * xh, axis=-1, keepdims=True))
        dx_ref[...] = dx
        dxb_ref[...] = dx.astype(dxb_ref.dtype)

        @pl.when(i == 0)
        def _():
            dg_ref[...] = dgp

        @pl.when(i > 0)
        def _():
            dg_ref[...] += dgp

        if carry_sibling is not None:
            @pl.when(last_step)
            def _():
                _sibling_finish(g4_ref, theirs_ref, send_sems, recv_sems)

    blk = pl.BlockSpec((tm, d), lambda i: (i, 0))
    row = pl.BlockSpec((1, d), lambda i: (0, 0))
    a_specs = [pl.BlockSpec((tm, k), lambda i: (i, 0)) for _ in parts]
    b_specs = [pl.BlockSpec((k, d), lambda i, kk=kk: (kk, 0)) for kk in range(n_parts)]
    vmem = 2 * n_parts * (_nbytes((tm, k), BF16) + _nbytes((k, d), BF16)) + 16 * tm * d * 4
    in_specs = a_specs + b_specs + [blk, row, blk]
    out_shape = [jax.ShapeDtypeStruct((t, d), F32), jax.ShapeDtypeStruct((t, d), BF16), jax.ShapeDtypeStruct((1, d), F32)]
    args = (*parts, *([b] * n_parts), x, g, dxo)
    if carry_sibling is None:
        return pl.pallas_call(body, grid=grid, in_specs=in_specs, out_specs=[blk, blk, row], out_shape=out_shape,
                              name=name, compiler_params=_cparams(("arbitrary",), vmem))(*args)
    nchip, _, r, c = carry_sibling.shape
    any_spec = pl.BlockSpec(memory_space=pl.ANY)
    return pl.pallas_call(
        body, grid=grid, in_specs=in_specs + [any_spec], out_specs=[blk, blk, row, any_spec],
        out_shape=out_shape + [jax.ShapeDtypeStruct((nchip, r, c), carry_sibling.dtype)],
        scratch_shapes=list(SIBLING_SEMS), name=name + "_exchange", compiler_params=_cparams(("arbitrary",), vmem),
    )(*args, carry_sibling)


def _final_loss_bwd(x, g, tgt):
    t, d = x.shape
    tm = _pick(t, (512, 256, 128))

    def body(x_ref, g_ref, t_ref, dx_ref, dxb_ref, dg_ref, loss_ref):
        i = pl.program_id(0)
        xv = x_ref[...]
        r = lax.rsqrt(jnp.mean(xv * xv, axis=-1, keepdims=True) + RMS_EPS)
        xh = xv * r
        gv = g_ref[...]
        err = xh * gv - t_ref[...]
        lp = 0.5 * jnp.sum(jnp.mean(err * err, axis=-1, keepdims=True), axis=0, keepdims=True)
        dy = err * (1.0 / d)
        dgp = jnp.sum(dy * xh, axis=0, keepdims=True)
        dh = dy * gv
        dx = r * (dh - xh * jnp.mean(dh * xh, axis=-1, keepdims=True))
        dx_ref[...] = dx
        dxb_ref[...] = dx.astype(dxb_ref.dtype)
        lpb = jnp.broadcast_to(lp, (1, LANES))

        @pl.when(i == 0)
        def _():
            dg_ref[...] = dgp
            loss_ref[...] = lpb

        @pl.when(i > 0)
        def _():
            dg_ref[...] += dgp
            loss_ref[...] += lpb

    blk = pl.BlockSpec((tm, d), lambda i: (i, 0))
    row = pl.BlockSpec((1, d), lambda i: (0, 0))
    return pl.pallas_call(
        body, grid=(t // tm,), in_specs=[blk, row, blk],
        out_specs=[blk, blk, row, pl.BlockSpec((1, LANES), lambda i: (0, 0))],
        out_shape=[jax.ShapeDtypeStruct((t, d), F32), jax.ShapeDtypeStruct((t, d), BF16),
                   jax.ShapeDtypeStruct((1, d), F32), jax.ShapeDtypeStruct((1, LANES), F32)], name="final_loss_bwd",
        compiler_params=_cparams(("arbitrary",), 16 * tm * d * 4),
    )(x, g, tgt)


def _seq_scan(v, seq, reverse):
    row = lax.broadcasted_iota(jnp.int32, v.shape, 0)
    k = 1
    while k < seq:
        if reverse:
            v = v + jnp.where(row < seq - k, pltpu.roll(v, seq - k, 0), 0.0)
        else:
            v = v + jnp.where(row >= k, pltpu.roll(v, k, 0), 0.0)
        k *= 2
    return v


def _log_sigmoid(v):
    return jnp.minimum(v, 0.0) - jnp.log(1.0 + jnp.exp(-jnp.abs(v)))


def _fox_prep(fl, bf, qkv, nb, seq):
    def body(f_ref, b_ref, q_ref, k_ref, v_ref, qa_ref, ka_ref, vm_ref):
        dsum = _seq_scan(_log_sigmoid(f_ref[...] + b_ref[...]), seq, False)
        d1 = dsum.astype(BF16).astype(F32)
        r1 = dsum - d1
        d2 = r1.astype(BF16).astype(F32)
        d3 = (r1 - d2).astype(BF16).astype(F32)
        lane = lax.broadcasted_iota(jnp.int32, (seq, LANES), 1)
        first = lane < HEAD_DIM
        l64 = jnp.where(first, lane, lane - HEAD_DIM)
        for p in range(N_PAIRS):
            def head_cols(a, p=p):
                return jnp.where(first, a[:, 2 * p:2 * p + 1], a[:, 2 * p + 1:2 * p + 2])

            e1, e2, e3 = head_cols(d1), head_cols(d2), head_cols(d3)
            aux_q = jnp.where(l64 == 0, e1, jnp.where(l64 == 1, e2, jnp.where(l64 == 2, e3,
                              jnp.where(l64 < 6, 1.0, 0.0)))).astype(BF16)
            aux_k = jnp.where(l64 < 3, 1.0, jnp.where(l64 == 3, -e1, jnp.where(l64 == 4, -e2,
                              jnp.where(l64 == 5, -e3, 0.0)))).astype(BF16)
            cols = slice(LANES * p, LANES * (p + 1))
            qs = q_ref[:, cols] * ATT_SCALE
            vp = v_ref[:, cols]
            zero = jnp.zeros_like(qs)
            qa_ref[0, p, 0, :, :LANES] = jnp.where(first, qs, zero)
            qa_ref[0, p, 0, :, LANES:] = jnp.where(first, aux_q, zero)
            qa_ref[0, p, 1, :, :LANES] = jnp.where(first, zero, qs)
            qa_ref[0, p, 1, :, LANES:] = jnp.where(first, zero, aux_q)
            ka_ref[0, p, :, :LANES] = k_ref[:, cols]
            ka_ref[0, p, :, LANES:] = aux_k
            vm_ref[0, p, 0] = jnp.where(first, vp, zero)
            vm_ref[0, p, 1] = jnp.where(first, zero, vp)

    def part(c):
        return pl.BlockSpec((seq, D_ATTN), lambda b, c=c: (b, c))

    return pl.pallas_call(
        body, grid=(nb,),
        in_specs=[pl.BlockSpec((seq, LANES), lambda b: (b, 0)), pl.BlockSpec((1, LANES), lambda b: (0, 0)),
                  part(0), part(1), part(2)],
        out_specs=[pl.BlockSpec((1, N_PAIRS, 2, seq, ATT_K), lambda b: (b, 0, 0, 0, 0)),
                   pl.BlockSpec((1, N_PAIRS, seq, ATT_K), lambda b: (b, 0, 0, 0)),
                   pl.BlockSpec((1, N_PAIRS, 2, seq, LANES), lambda b: (b, 0, 0, 0, 0))],
        out_shape=[jax.ShapeDtypeStruct((nb, N_PAIRS, 2, seq, ATT_K), BF16),
                   jax.ShapeDtypeStruct((nb, N_PAIRS, seq, ATT_K), BF16),
                   jax.ShapeDtypeStruct((nb, N_PAIRS, 2, seq, LANES), BF16)],
        name="fox_prep", compiler_params=_cparams(("parallel",), 48 * 1024 * 1024),
    )(fl, bf, qkv, qkv, qkv)


def _fox_prep_bwd(dd, fl, bf, seq):
    t = fl.shape[0]

    def body(d_ref, f_ref, b_ref, o_ref, db_ref):
        i = pl.program_id(0)
        dlog = _seq_scan(d_ref[...], seq, True)
        dfl = dlog * _sigmoid(-(f_ref[...] + b_ref[...]))
        o_ref[...] = dfl.astype(o_ref.dtype)
        dbp = jnp.sum(dfl, axis=0, keepdims=True)

        @pl.when(i == 0)
        def _():
            db_ref[...] = dbp

        @pl.when(i > 0)
        def _():
            db_ref[...] += dbp

    blk = pl.BlockSpec((seq, LANES), lambda b: (b, 0))
    row = pl.BlockSpec((1, LANES), lambda b: (0, 0))
    return pl.pallas_call(
        body, grid=(t // seq,), in_specs=[blk, blk, row], out_specs=[blk, row],
        out_shape=[jax.ShapeDtypeStruct((t, LANES), BF16), jax.ShapeDtypeStruct((1, LANES), F32)], name="fox_prep_bwd",
        compiler_params=_cparams(("arbitrary",), 24 * seq * LANES * 4),
    )(dd, fl, bf)


def _pair_rows(a, ta):
    lane = lax.broadcasted_iota(jnp.int32, (ta, LANES), 1)
    return jnp.where(lane < HEAD_DIM, a[:ta], a[ta:])


def _diag_mask(ta):
    r = lax.broadcasted_iota(jnp.int32, (2 * ta, ta), 0)
    c = lax.broadcasted_iota(jnp.int32, (2 * ta, ta), 1)
    return c <= jnp.where(r >= ta, r - ta, r)


def _nt(a, b):
    return lax.dot_general(a, b, (((1,), (1,)), ((), ())), preferred_element_type=F32)


def _tn(a, b):
    return lax.dot_general(a, b, (((0,), (0,)), ((), ())), preferred_element_type=F32)


def _grid_ends(ids, sizes):
    first = functools.reduce(jnp.logical_and, [i == 0 for i in ids])
    last = functools.reduce(jnp.logical_and, [i == n - 1 for i, n in zip(ids, sizes)])
    return first, last


def _fox_fwd(qa, ka, vm, nb, seq, ta, carry_gather=None):
    nq = seq // ta
    npp = ATT_PAIRS_FWD
    grid = (nb, N_PAIRS // npp, nq)

    def body(q_ref, k_ref, v_ref, *rest):
        if carry_gather is None:
            o_ref, lse_ref = rest
        else:
            x_ref, o_ref, lse_ref, gathered_ref, send_sems, recv_sems = rest
            first_step, last_step = _grid_ends([pl.program_id(a) for a in range(3)], grid)

            @pl.when(first_step)
            def _():
                _gather_start(x_ref, gathered_ref, send_sems, recv_sems)

        i = pl.program_id(2)
        q2s = [q_ref[0, pp].reshape(2 * ta, ATT_K) for pp in range(npp)]

        def step(j, carry, masked):
            rows = pl.ds(pl.multiple_of(j * ta, ta), ta)
            out = []
            for pp in range(npp):
                m, l, acc = carry[pp]
                s = _nt(q2s[pp], k_ref[0, pp, rows, :])
                if masked:
                    s = jnp.where(_diag_mask(ta), s, NEG_BIG)
                m_new = jnp.maximum(m, jnp.max(s, axis=-1, keepdims=True))
                p = jnp.exp(s - m_new)
                corr = jnp.exp(m - m_new)
                l = corr * l + jnp.sum(p, axis=-1, keepdims=True)
                pb = p.astype(BF16)
                pv = (jnp.dot(pb[:ta], v_ref[0, pp, 0, rows, :], preferred_element_type=F32)
                      + jnp.dot(pb[ta:], v_ref[0, pp, 1, rows, :], preferred_element_type=F32))
                out.append((m_new, l, _pair_rows(corr, ta) * acc + pv))
            return tuple(out)

        init = tuple((jnp.full((2 * ta, 1), NEG_BIG, F32), jnp.zeros((2 * ta, 1), F32),
                      jnp.zeros((ta, LANES), F32)) for _ in range(npp))
        carry = lax.fori_loop(0, i, functools.partial(step, masked=False), init)
        for pp, (m, l, acc) in enumerate(step(i, carry, True)):
            o_ref[:, LANES * pp:LANES * (pp + 1)] = (acc * _pair_rows(1.0 / l, ta)).astype(o_ref.dtype)
            lse = m + jnp.log(l)
            lse_ref[0, pp, 0] = lse[:ta]
            lse_ref[0, pp, 1] = lse[ta:]

        if carry_gather is not None:
            @pl.when(last_step)
            def _():
                _gather_finish(x_ref, gathered_ref, send_sems, recv_sems)

    vmem = (2 * npp * (_nbytes((seq, ATT_K), BF16) + 2 * _nbytes((seq, LANES), BF16)) + 24 * npp * ta * ta * 4
            + 8 * 1024 * 1024)
    in_specs = [pl.BlockSpec((1, npp, 2, ta, ATT_K), lambda b, g, i: (b, g, 0, i, 0)),
                pl.BlockSpec((1, npp, seq, ATT_K), lambda b, g, i: (b, g, 0, 0)),
                pl.BlockSpec((1, npp, 2, seq, LANES), lambda b, g, i: (b, g, 0, 0, 0))]
    out_specs = [pl.BlockSpec((ta, LANES * npp), lambda b, g, i: (b * nq + i, g)),
                 pl.BlockSpec((1, npp, 2, ta, 1), lambda b, g, i: (b, g, 0, i, 0))]
    out_shape = [jax.ShapeDtypeStruct((nb * seq, D_ATTN), BF16), jax.ShapeDtypeStruct((nb, N_PAIRS, 2, seq, 1), F32)]
    if carry_gather is None:
        return pl.pallas_call(
            body, grid=grid, in_specs=in_specs, out_specs=out_specs, out_shape=out_shape, name="fox_fwd",
            compiler_params=_cparams(("parallel", "parallel", "parallel"), vmem),
        )(qa, ka, vm)
    any_spec = pl.BlockSpec(memory_space=pl.ANY)
    return pl.pallas_call(
        body, grid=grid, in_specs=in_specs + [any_spec], out_specs=out_specs + [any_spec],
        out_shape=out_shape + [jax.ShapeDtypeStruct((N_DEV,) + carry_gather.shape, carry_gather.dtype)],
        scratch_shapes=list(GATHER_SEMS), name="fox_fwd_gather",
        compiler_params=_cparams(("arbitrary", "arbitrary", "arbitrary"), vmem),
    )(qa, ka, vm, carry_gather)


def _fox_bwd(qa, ka, vm, y, dy, lse, nb, seq, ta, carry_exchange=None):
    nq = seq // ta
    npp = ATT_PAIRS_BWD
    grid = (nb, N_PAIRS // npp, nq)

    def body(q_ref, k_ref, v_ref, o_ref, do_ref, lse_ref, *rest):
        if carry_exchange is None:
            dq_ref, dk_ref, dv_ref, rs_ref, cs_ref = rest
        else:
            t_ref, dq_ref, dk_ref, dv_ref, rs_ref, cs_ref, landed_ref, send_sems, recv_sems = rest
            first_step, last_step = _grid_ends([pl.program_id(a) for a in range(3)], grid)

            @pl.when(first_step)
            def _():
                _chips_start(t_ref, landed_ref, send_sems, recv_sems)

        i = pl.program_id(2)

        @pl.when(i == 0)
        def _():
            dk_ref[...] = jnp.zeros_like(dk_ref)
            dv_ref[...] = jnp.zeros_like(dv_ref)
            cs_ref[...] = jnp.zeros_like(cs_ref)

        first = lax.broadcasted_iota(jnp.int32, (ta, LANES), 1) < HEAD_DIM
        q2s, do2s, deltas, lses = [], [], [], []
        for pp in range(npp):
            cols = slice(LANES * pp, LANES * (pp + 1))
            q2s.append(q_ref[0, pp].reshape(2 * ta, ATT_K))
            do = do_ref[:, cols]
            doo = do * o_ref[:, cols].astype(F32)
            do2s.append(jnp.concatenate([jnp.where(first, do, 0.0), jnp.where(first, 0.0, do)], axis=0).astype(BF16))
            deltas.append(jnp.concatenate([jnp.sum(jnp.where(first, doo, 0.0), axis=-1, keepdims=True),
                                           jnp.sum(jnp.where(first, 0.0, doo), axis=-1, keepdims=True)], axis=0))
            lses.append(jnp.concatenate([lse_ref[0, pp, 0], lse_ref[0, pp, 1]], axis=0))

        def step(j, carry, masked):
            rows = pl.ds(pl.multiple_of(j * ta, ta), ta)
            out = []
            for pp in range(npp):
                dq_acc, rs_acc = carry[pp]
                cols = slice(LANES * pp, LANES * (pp + 1))
                ks = k_ref[0, pp, rows, :]
                s = _nt(q2s[pp], ks)
                if masked:
                    s = jnp.where(_diag_mask(ta), s, NEG_BIG)
                p = jnp.exp(s - lses[pp])
                dp = _nt(do2s[pp], v_ref[0, pp, 0, rows, :] + v_ref[0, pp, 1, rows, :])
                ds32 = p * (dp - deltas[pp])
                ds = ds32.astype(BF16)
                dk_ref[rows, cols] += _tn(ds, q2s[pp][:, :LANES])
                dv_ref[rows, cols] += _tn(p.astype(BF16), do2s[pp])
                cs_ref[0, pp, 0, j] += jnp.sum(ds32[:ta], axis=0, keepdims=True)
                cs_ref[0, pp, 1, j] += jnp.sum(ds32[ta:], axis=0, keepdims=True)
                out.append((dq_acc + jnp.dot(ds, ks[:, :LANES], preferred_element_type=F32),
                            rs_acc + jnp.sum(ds32, axis=-1, keepdims=True)))
            return tuple(out)

        init = tuple((jnp.zeros((2 * ta, LANES), F32), jnp.zeros((2 * ta, 1), F32)) for _ in range(npp))
        carry = lax.fori_loop(0, i, functools.partial(step, masked=False), init)
        for pp, (dq_acc, rs_acc) in enumerate(step(i, carry, True)):
            dq = jnp.where(first, dq_acc[:ta], dq_acc[ta:]) * ATT_SCALE
            dq_ref[:, LANES * pp:LANES * (pp + 1)] = dq.astype(dq_ref.dtype)
            rs_row = jnp.transpose(jnp.broadcast_to(rs_acc, (2 * ta, LANES)))[0:1]
            rs_ref[0, pp, 0, 0] = rs_row[:, :ta]
            rs_ref[0, pp, 1, 0] = rs_row[:, ta:]

        if carry_exchange is not None:
            @pl.when(last_step)
            def _():
                _chips_finish(t_ref, landed_ref, send_sems, recv_sems)

    vmem = (2 * npp * (_nbytes((seq, ATT_K), BF16) + 2 * _nbytes((seq, LANES), BF16) + 2 * _nbytes((seq, LANES), F32))
            + 32 * npp * ta * ta * 4 + 8 * 1024 * 1024)
    qblk = lambda b, g, i: (b * nq + i, g)
    acc_blk = pl.BlockSpec((seq, LANES * npp), lambda b, g, i: (b, g))
    in_specs = [pl.BlockSpec((1, npp, 2, ta, ATT_K), lambda b, g, i: (b, g, 0, i, 0)),
                pl.BlockSpec((1, npp, seq, ATT_K), lambda b, g, i: (b, g, 0, 0)),
                pl.BlockSpec((1, npp, 2, seq, LANES), lambda b, g, i: (b, g, 0, 0, 0)),
                pl.BlockSpec((ta, LANES * npp), qblk), pl.BlockSpec((ta, LANES * npp), qblk),
                pl.BlockSpec((1, npp, 2, ta, 1), lambda b, g, i: (b, g, 0, i, 0))]
    out_specs = [pl.BlockSpec((ta, LANES * npp), qblk), acc_blk, acc_blk,
                 pl.BlockSpec((1, npp, 2, 1, 1, ta), lambda b, g, i: (b, g, 0, i, 0, 0)),
                 pl.BlockSpec((1, npp, 2, nq, 1, ta), lambda b, g, i: (b, g, 0, 0, 0, 0))]
    sums = jax.ShapeDtypeStruct((nb, N_PAIRS, 2, nq, 1, ta), F32)
    out_shape = [jax.ShapeDtypeStruct((nb * seq, D_ATTN), BF16), jax.ShapeDtypeStruct((nb * seq, D_ATTN), F32),
                 jax.ShapeDtypeStruct((nb * seq, D_ATTN), F32), sums, sums]
    if carry_exchange is None:
        return pl.pallas_call(
            body, grid=grid, in_specs=in_specs, out_specs=out_specs, out_shape=out_shape, name="fox_bwd",
            compiler_params=_cparams(("parallel", "parallel", "arbitrary"), vmem),
        )(qa, ka, vm, y, dy, lse)
    any_spec = pl.BlockSpec(memory_space=pl.ANY)
    return pl.pallas_call(
        body, grid=grid, in_specs=in_specs + [any_spec], out_specs=out_specs + [any_spec],
        out_shape=out_shape + [jax.ShapeDtypeStruct(carry_exchange.shape, carry_exchange.dtype)],
        scratch_shapes=list(CHIPS_SEMS), name="fox_bwd_exchange",
        compiler_params=_cparams(("arbitrary", "arbitrary", "arbitrary"), vmem),
    )(qa, ka, vm, y, dy, lse, carry_exchange)


def _shift_down(a, k):
    row = lax.broadcasted_iota(jnp.int32, a.shape, 0)
    return jnp.where(row >= k, pltpu.roll(a, k, 0), 0.0)


def _shift_up(a, k):
    n = a.shape[0]
    row = lax.broadcasted_iota(jnp.int32, a.shape, 0)
    return jnp.where(row < n - k, pltpu.roll(a, n - k, 0), 0.0)


def _by_group(vals, shape):
    lane = lax.broadcasted_iota(jnp.int32, shape, 1)
    out = vals[-1]
    for gi in range(len(vals) - 2, -1, -1):
        out = jnp.where(lane < POOL_GROUP * (gi + 1), vals[gi], out)
    return out


def _pooled(u):
    s2 = u + _shift_down(u, 1)
    s4 = s2 + _shift_down(s2, 2)
    s8 = s4 + _shift_down(s4, 4)
    s16 = s8 + _shift_down(s8, 8)
    win = _by_group([s2, s4, s8, s16], u.shape)
    row = lax.broadcasted_iota(jnp.int32, u.shape, 0)
    wsize = _by_group([jnp.full(u.shape, w, jnp.int32) for w in POOL_WINDOWS], u.shape)
    inv = 1.0 / jnp.minimum(row + 1, wsize).astype(F32)
    return win * inv - u, inv


def _pool_fwd(rest, wbd, scale, seq):
    t = rest.shape[0]

    def body(u_ref, w_ref, s_ref, o_ref):
        pooled, _ = _pooled(u_ref[...])
        pw = jnp.dot(pooled.astype(BF16), w_ref[...], preferred_element_type=F32)
        o_ref[...] = (pw * s_ref[...]).astype(o_ref.dtype)

    blk = pl.BlockSpec((seq, D_POOL), lambda b: (b, 0))
    return pl.pallas_call(
        body, grid=(t // seq,),
        in_specs=[blk, pl.BlockSpec((D_POOL, D_POOL), lambda b: (0, 0)), pl.BlockSpec((1, D_POOL), lambda b: (0, 0))],
        out_specs=blk, out_shape=jax.ShapeDtypeStruct((t, D_POOL), BF16), name="pool_fwd",
        compiler_params=_cparams(("parallel",), 24 * seq * D_POOL * 4),
    )(rest, wbd, scale)


def _pool_bwd(rest, dy, wbd, wbd_t, scale, seq):
    t = rest.shape[0]

    def body(u_ref, dy_ref, w_ref, wt_ref, s_ref, du_ref, dw_ref, dsc_ref):
        i = pl.program_id(0)
        pooled, inv = _pooled(u_ref[...])
        pb = pooled.astype(BF16)
        pw = jnp.dot(pb, w_ref[...], preferred_element_type=F32)
        dyp = dy_ref[...]
        dsp = jnp.sum(dyp * pw, axis=0, keepdims=True)
        dpw = (dyp * s_ref[...]).astype(BF16)
        dwp = _tn(pb, dpw)
        dpooled = jnp.dot(dpw, wt_ref[...], preferred_element_type=F32)
        dwin = dpooled * inv
        t2 = dwin + _shift_up(dwin, 1)
        t4 = t2 + _shift_up(t2, 2)
        t8 = t4 + _shift_up(t4, 4)
        t16 = t8 + _shift_up(t8, 8)
        du_ref[...] = (_by_group([t2, t4, t8, t16], dwin.shape) - dpooled).astype(du_ref.dtype)

        @pl.when(i == 0)
        def _():
            dw_ref[...] = dwp
            dsc_ref[...] = dsp

        @pl.when(i > 0)
        def _():
            dw_ref[...] += dwp
            dsc_ref[...] += dsp

    blk = pl.BlockSpec((seq, D_POOL), lambda b: (b, 0))
    sq = pl.BlockSpec((D_POOL, D_POOL), lambda b: (0, 0))
    row = pl.BlockSpec((1, D_POOL), lambda b: (0, 0))
    return pl.pallas_call(
        body, grid=(t // seq,),
        in_specs=[blk, pl.BlockSpec((seq, D_POOL), lambda b: (b, 2)), sq, sq, row],
        out_specs=[blk, sq, row],
        out_shape=[jax.ShapeDtypeStruct((t, D_POOL), BF16), jax.ShapeDtypeStruct((D_POOL, D_POOL), F32),
                   jax.ShapeDtypeStruct((1, D_POOL), F32)], name="pool_bwd",
        compiler_params=_cparams(("arbitrary",), 40 * seq * D_POOL * 4),
    )(rest, dy, wbd, wbd_t, scale)


def _conv_fwd(rest, cw, seq):
    t = rest.shape[0]

    def body(cb_ref, cc_ref, ch_ref, w_ref, o_ref):
        u = cc_ref[...] * ch_ref[...]
        y = w_ref[0:1, :] * _shift_down(u, 2) + w_ref[1:2, :] * _shift_down(u, 1) + w_ref[2:3, :] * u
        o_ref[...] = (cb_ref[...] * y).astype(o_ref.dtype)

    def col(c):
        return pl.BlockSpec((seq, D_CONV), lambda b, c=c: (b, c))

    return pl.pallas_call(
        body, grid=(t // seq,), in_specs=[col(1), col(2), col(3), pl.BlockSpec((8, D_CONV), lambda b: (0, 0))],
        out_specs=pl.BlockSpec((seq, D_CONV), lambda b: (b, 0)),
        out_shape=jax.ShapeDtypeStruct((t, D_CONV), BF16), name="conv_fwd",
        compiler_params=_cparams(("parallel",), 24 * seq * D_CONV * 4),
    )(rest, rest, rest, cw)


def _conv_bwd(rest, dy, cw, seq):
    t = rest.shape[0]

    def body(cb_ref, cc_ref, ch_ref, dy_ref, w_ref, o_ref, dw_ref):
        i = pl.program_id(0)
        cc = cc_ref[...]
        ch = ch_ref[...]
        u = cc * ch
        u1 = _shift_down(u, 1)
        u2 = _shift_down(u, 2)
        y = w_ref[0:1, :] * u2 + w_ref[1:2, :] * u1 + w_ref[2:3, :] * u
        dyc = dy_ref[...]
        d2 = dyc * cb_ref[...]
        du = w_ref[0:1, :] * _shift_up(d2, 2) + w_ref[1:2, :] * _shift_up(d2, 1) + w_ref[2:3, :] * d2
        o_ref[:, 0:D_CONV] = (dyc * y).astype(o_ref.dtype)
        o_ref[:, D_CONV:2 * D_CONV] = (du * ch).astype(o_ref.dtype)
        o_ref[:, 2 * D_CONV:3 * D_CONV] = (du * cc).astype(o_ref.dtype)
        tap = lax.broadcasted_iota(jnp.int32, (8, D_CONV), 0)
        dwp = jnp.where(tap == 0, jnp.sum(d2 * u2, axis=0, keepdims=True),
                        jnp.where(tap == 1, jnp.sum(d2 * u1, axis=0, keepdims=True),
                                  jnp.where(tap == 2, jnp.sum(d2 * u, axis=0, keepdims=True), 0.0)))

        @pl.when(i == 0)
        def _():
            dw_ref[...] = dwp

        @pl.when(i > 0)
        def _():
            dw_ref[...] += dwp

    def col(c):
        return pl.BlockSpec((seq, D_CONV), lambda b, c=c: (b, c))

    taps = pl.BlockSpec((8, D_CONV), lambda b: (0, 0))
    return pl.pallas_call(
        body, grid=(t // seq,), in_specs=[col(1), col(2), col(3), col(3), taps],
        out_specs=[pl.BlockSpec((seq, 3 * D_CONV), lambda b: (b, 0)), taps],
        out_shape=[jax.ShapeDtypeStruct((t, 3 * D_CONV), BF16), jax.ShapeDtypeStruct((8, D_CONV), F32)],
        name="conv_bwd", compiler_params=_cparams(("arbitrary",), 48 * seq * D_CONV * 4),
    )(rest, rest, rest, dy, cw)


def _adamw(w, g, m, v, name):
    r, c = w.shape
    tr = _pick(r, (512, 352, 256, 128)) if r > 512 else r

    def body(w_ref, g_ref, m_ref, v_ref, d_ref, mo_ref, vo_ref):
        gv = g_ref[...]
        mn = ADAM_B1 * m_ref[...] + (1.0 - ADAM_B1) * gv
        vn = ADAM_B2 * v_ref[...] + (1.0 - ADAM_B2) * (gv * gv)
        m_hat = mn / (1.0 - ADAM_B1 ** ADAM_STEP)
        v_hat = vn / (1.0 - ADAM_B2 ** ADAM_STEP)
        d_ref[...] = -ADAM_LR * (m_hat / (jnp.sqrt(v_hat) + ADAM_EPS) + ADAM_WD * w_ref[...])
        mo_ref[...] = mn
        vo_ref[...] = vn

    blk = pl.BlockSpec((tr, c), lambda i: (i, 0))
    sds = jax.ShapeDtypeStruct((r, c), F32)
    return pl.pallas_call(
        body, grid=(r // tr,), in_specs=[blk] * 4, out_specs=[blk] * 3, out_shape=[sds] * 3, name=name,
        compiler_params=_cparams(("parallel",), 20 * tr * max(c, LANES) * 4),
    )(w, g, m, v)


def _sum_slots(a, name):
    ns, r, c = a.shape
    tr = _pick(r, (384, 368, 256, 184, 136, 128, 88, 8))

    def body(a_ref, o_ref):
        acc = a_ref[0].astype(F32)
        for s in range(1, ns):
            acc = acc + a_ref[s].astype(F32)
        o_ref[...] = acc

    return pl.pallas_call(
        body, grid=(r // tr,), in_specs=[pl.BlockSpec((ns, tr, c), lambda i: (0, i, 0))],
        out_specs=pl.BlockSpec((tr, c), lambda i: (i, 0)), out_shape=jax.ShapeDtypeStruct((r, c), F32), name=name,
        compiler_params=_cparams(("parallel",), 4 * (ns + 2) * tr * c * 4),
    )(a)


def _add_core_half(core, g4, theirs, out_dtype, name):
    ns, _, r, c = g4.shape
    tr = _pick(r, (384, 368, 256, 184, 136, 128, 88, 8))

    def body(core_ref, a_ref, b_ref, o_ref):
        o_ref[...] = (a_ref[0] + b_ref[...]).astype(o_ref.dtype)

    blk = pl.BlockSpec((1, tr, c), lambda s, i, core_ref: (s, i, 0))
    return pl.pallas_call(
        body,
        grid_spec=pltpu.PrefetchScalarGridSpec(
            num_scalar_prefetch=1, grid=(ns, r // tr),
            in_specs=[pl.BlockSpec((1, 1, tr, c), lambda s, i, core_ref: (s, core_ref[0], i, 0)), blk],
            out_specs=blk),
        out_shape=jax.ShapeDtypeStruct((ns, r, c), out_dtype), name=name,
        compiler_params=_cparams(("parallel", "parallel"), 10 * tr * c * 4),
    )(core, g4, theirs)


def _sum_chips(order, own, landed, name):
    ns, r, c = own.shape
    tr = _pick(r, (384, 368, 256, 184, 136, 128, 88, 8))

    def body(order_ref, a_ref, b1_ref, b2_ref, b3_ref, o_ref):
        o_ref[...] = ((a_ref[0].astype(F32) + b1_ref[0].astype(F32)) + b2_ref[0].astype(F32)) + b3_ref[0].astype(F32)

    def slot(k):
        return pl.BlockSpec((1, tr, c), lambda i, order_ref, k=k: (order_ref[k], i, 0))

    return pl.pallas_call(
        body,
        grid_spec=pltpu.PrefetchScalarGridSpec(
            num_scalar_prefetch=1, grid=(r // tr,), in_specs=[slot(0), slot(1), slot(2), slot(3)],
            out_specs=pl.BlockSpec((tr, c), lambda i, order_ref: (i, 0))),
        out_shape=jax.ShapeDtypeStruct((r, c), F32), name=name,
        compiler_params=_cparams(("parallel",), 16 * tr * c * 4),
    )(order, own, landed, landed, landed)


def _mesh_pos():
    return lax.axis_index("x"), lax.axis_index("y"), lax.axis_index("c")


def _comm_call(name, gathers=(), chips=None):
    payloads = list(gathers) + ([] if chips is None else [chips])
    n = len(payloads)

    def body(*refs):
        ins, outs, sems = refs[:n], refs[n:2 * n], refs[2 * n:]
        jobs = [(_gather_start, _gather_finish)] * len(gathers) + ([] if chips is None else [(_chips_start, _chips_finish)])
        for k, (start, _) in enumerate(jobs):
            start(ins[k], outs[k], sems[2 * k], sems[2 * k + 1])
        for k, (_, finish) in enumerate(jobs):
            finish(ins[k], outs[k], sems[2 * k], sems[2 * k + 1])

    any_spec = pl.BlockSpec(memory_space=pl.ANY)
    out_shape = [jax.ShapeDtypeStruct((N_DEV,) + x.shape, x.dtype) for x in gathers]
    sems = list(GATHER_SEMS) * len(gathers)
    if chips is not None:
        out_shape.append(jax.ShapeDtypeStruct(chips.shape, chips.dtype))
        sems += list(CHIPS_SEMS)
    outs = pl.pallas_call(body, out_shape=out_shape, in_specs=[any_spec] * n, out_specs=[any_spec] * n,
                          scratch_shapes=sems, name=name)(*payloads)
    return [_fill_own_slot(o, x) for o, x in zip(outs, gathers)] + ([] if chips is None else [outs[-1]])


GATHER_SEMS = (pltpu.SemaphoreType.DMA((7,)), pltpu.SemaphoreType.DMA((7,)))


def _fill_own_slot(gathered, x):
    mx, my, mc = _mesh_pos()
    return lax.dynamic_update_slice_in_dim(gathered, x[None], 4 * mx + 2 * my + mc, axis=0)


def _gather_copies(x_ref, out_ref, send_sems, recv_sems):
    mx, my, mc = _mesh_pos()
    me, sibling = (mx, my, mc), (mx, my, 1 - mc)
    chips = [(1 - mx, my), (mx, 1 - my), (1 - mx, 1 - my)]

    def slot(px, py, pc):
        return out_ref.at[4 * px + 2 * py + pc]

    def copy(k, block, to, src=None):
        return pltpu.make_async_remote_copy(
            src_ref=slot(*block) if src is None else src, dst_ref=slot(*block),
            send_sem=send_sems.at[k], recv_sem=recv_sems.at[k],
            device_id=to, device_id_type=pl.DeviceIdType.MESH)

    first = [copy(0, me, sibling, src=x_ref)]
    first += [copy(1 + j, me, (*chip, mc), src=x_ref) for j, chip in enumerate(chips)]
    passed = [copy(4 + j, (*chip, mc), sibling) for j, chip in enumerate(chips)]
    over_ici = [copy(1 + j, (*chip, mc), me) for j, chip in enumerate(chips)]
    over_d2d = [copy(0, sibling, me)] + [copy(4 + j, (*chip, 1 - mc), me) for j, chip in enumerate(chips)]
    return first, passed, over_ici, over_d2d


def _gather_start(x_ref, out_ref, send_sems, recv_sems):
    for cp in _gather_copies(x_ref, out_ref, send_sems, recv_sems)[0]:
        cp.start()


def _gather_finish(x_ref, out_ref, send_sems, recv_sems):
    first, passed, over_ici, over_d2d = _gather_copies(x_ref, out_ref, send_sems, recv_sems)
    for landed, relay in zip(over_ici, passed):
        landed.wait_recv()
        relay.start()
    for landed in over_d2d:
        landed.wait_recv()
    for cp in first + passed:
        cp.wait_send()


SIBLING_SEMS = (pltpu.SemaphoreType.DMA((N_CHIPS,)), pltpu.SemaphoreType.DMA((N_CHIPS,)))


def _sibling_copies(g_ref, theirs_ref, send_sems, recv_sems):
    mx, my, mc = _mesh_pos()
    return [pltpu.make_async_remote_copy(
        src_ref=g_ref.at[chip, 1 - mc], dst_ref=theirs_ref.at[chip],
        send_sem=send_sems.at[chip], recv_sem=recv_sems.at[chip],
        device_id=(mx, my, 1 - mc), device_id_type=pl.DeviceIdType.MESH) for chip in range(N_CHIPS)]


def _sibling_start(g_ref, theirs_ref, send_sems, recv_sems):
    for cp in _sibling_copies(g_ref, theirs_ref, send_sems, recv_sems):
        cp.start()


def _sibling_finish(g_ref, theirs_ref, send_sems, recv_sems):
    copies = _sibling_copies(g_ref, theirs_ref, send_sems, recv_sems)
    for cp in copies:
        cp.wait_recv()
    for cp in copies:
        cp.wait_send()


CHIPS_SEMS = (pltpu.SemaphoreType.DMA((N_CHIPS - 1,)), pltpu.SemaphoreType.DMA((N_CHIPS - 1,)))


def _chips_copies(t_ref, out_ref, send_sems, recv_sems):
    mx, my, mc = _mesh_pos()
    my_chip = 2 * mx + my
    copies = []
    for k in range(1, N_CHIPS):
        px = 1 - mx if k & 2 else mx
        py = 1 - my if k & 1 else my
        peer_chip = 2 * px + py

        def rdma(dst_slot, px=px, py=py, peer_chip=peer_chip, k=k):
            return pltpu.make_async_remote_copy(
                src_ref=t_ref.at[peer_chip], dst_ref=out_ref.at[dst_slot],
                send_sem=send_sems.at[k - 1], recv_sem=recv_sems.at[k - 1],
                device_id=(px, py, mc), device_id_type=pl.DeviceIdType.MESH)

        copies.append((rdma(my_chip), rdma(peer_chip)))
    return copies


def _chips_start(t_ref, out_ref, send_sems, recv_sems):
    for send, _ in _chips_copies(t_ref, out_ref, send_sems, recv_sems):
        send.start()


def _chips_finish(t_ref, out_ref, send_sems, recv_sems):
    copies = _chips_copies(t_ref, out_ref, send_sems, recv_sems)
    for _, landed in copies:
        landed.wait_recv()
    for send, _ in copies:
        send.wait_send()


def _add_sibling(g4, theirs):
    core = jnp.reshape(lax.axis_index("c"), (1,)).astype(jnp.int32)
    return _add_core_half(core, g4, theirs, BF16, name="add_sibling_grads")


def _sum_landed(chip_sums, landed):
    mx, my, _ = _mesh_pos()
    order = jnp.stack([2 * mx + my, 2 * (1 - mx) + my, 2 * mx + (1 - my), 2 * (1 - mx) + (1 - my)]).astype(jnp.int32)
    return _sum_chips(order, chip_sums, landed, name="sum_grads")


def _perm_mix_rows(wt):
    f0 = D_QKV
    f1 = f0 + N_HEADS
    return jnp.concatenate([wt[:f0], wt[f1:], jnp.pad(wt[f0:f1], ((0, LANES - N_HEADS), (0, 0)))], axis=0)


def _unperm_mix_rows(gt):
    f0 = D_QKV
    return jnp.concatenate([gt[:f0], gt[f0 + D_REST:f0 + D_REST + N_HEADS], gt[f0:f0 + D_REST]], axis=0)


def _pack_shards(parts, l, dtype):
    w1i, w1o, wmi, wmo, w2i, w2o = parts
    rows = [w1i[l].T, w1o[l], jnp.pad(wmi[l].T, ((0, MIX_ROWS_PAD - MIX_ROWS), (0, 0))), wmo[l], w2i[l].T, w2o[l]]
    return jnp.concatenate(rows, axis=0).astype(dtype)


PACK_HEAD = FFN_ROWS + OUT_ROWS


def _ffn_weights(wg, o):
    return dict(wi_t=wg[:, o:o + FFN_ROWS].reshape(2 * D_FF, D_MODEL),
                wo=wg[:, o + FFN_ROWS:o + FFN_ROWS + OUT_ROWS].reshape(D_FF, D_MODEL))


def _tail_weights(wg):
    mix = dict(wm_t=_perm_mix_rows(wg[:, :MIX_ROWS].reshape(D_IN, D_MODEL)),
               wo=wg[:, MIX_ROWS_PAD:MIX_ROWS_PAD + MO_ROWS].reshape(D_MODEL, D_MODEL))
    return mix, _ffn_weights(wg, MIX_ROWS_PAD + MO_ROWS)


GRAD_AT = dict(f1i=0, f2i=FFN_ROWS, f1o=4 * OUT_ROWS, f2o=5 * OUT_ROWS, mi=6 * OUT_ROWS, mo=20 * MO_ROWS)
GRAD_ROWS = GRAD_AT["mo"] + MO_ROWS
GRAD_SHAPE = (N_CHIPS, 2, GRAD_ROWS, D_MODEL)


def _into_ffn_in(buf, tag, half):
    rb = GRAD_AT[tag] // FFN_ROWS
    return (buf, GRAD_SHAPE, (1, 2, FFN_ROWS, D_MODEL), lambda i: (2 * half + i, 0, rb, 0))


def _into_ffn_out(buf, tag):
    rb = GRAD_AT[tag] // OUT_ROWS
    return (buf, GRAD_SHAPE, (2, 2, OUT_ROWS, D_MODEL), lambda i: (i, 0, rb, 0))


def _into_mix_out(buf):
    rb = GRAD_AT["mo"] // MO_ROWS
    return (buf, GRAD_SHAPE, (N_CHIPS, 2, MO_ROWS, D_MODEL), lambda i: (0, 0, rb, 0))


def _put_mix_in(buf, g_in_t):
    gmi = _unperm_mix_rows(g_in_t).reshape(N_DEV, MIX_ROWS, D_MODEL)
    gmi = jnp.pad(gmi, ((0, 0), (0, OUT_ROWS - MIX_ROWS), (0, 0))).reshape(N_CHIPS, 2, OUT_ROWS, D_MODEL)
    return lax.dynamic_update_slice(buf, gmi, (0, 0, GRAD_AT["mi"], 0))


def _out_proj(a, w, x, alpha, next_gain, name):
    if next_gain is None:
        return _mm_nn(a, w, out_dtype=F32, res=x, alpha=alpha, name=name), None
    return _mm_nn(a, w, out_dtype=F32, res=x, alpha=alpha, next_gain=next_gain, name=name + "_norm")


def _ffn_forward(x, xn, w, next_gain, carry_gather=None):
    res = _ffn_in(xn, w["wi_t"], name="ffn_in", carry_gather=carry_gather)
    h, pg, pu = res[:3]
    x_new, xn_next = _out_proj(h, w["wo"], x, 0.5, next_gain, "ffn_out")
    out = (x_new, xn_next, dict(x=x, xn=xn, h=h, pg=pg, pu=pu))
    return out if carry_gather is None else out + (_fill_own_slot(res[3], carry_gather),)


def _ffn_backward(dxo, dxo_b, gain, w, saved, gbuf, tag, exchange=False):
    dzg, dzu = _ffn_bwd_mid(dxo_b, w["wo"], saved["pg"], saved["pu"], name="ffn_bwd_mid")
    gbuf = _mm_tn(saved["h"], dxo_b, alpha=0.5, tm=F_HALF, name="ffn_gw_out", into=_into_ffn_out(gbuf, tag + "o"))
    gbuf = _mm_tn(dzg, saved["xn"], tm=F_HALF, name="ffn_gw_in", into=_into_ffn_in(gbuf, tag + "i", 0))
    gbuf = _mm_tn(dzu, saved["xn"], tm=F_HALF, name="ffn_gw_in", into=_into_ffn_in(gbuf, tag + "i", 1))
    res = _dxn_norm_bwd([dzg, dzu], w["wi_t"], saved["x"], gain, dxo, name="ffn_dxn_norm_bwd",
                        carry_sibling=gbuf if exchange else None)
    return res[0], res[1], res[2], gbuf, (res[3] if exchange else None)


def _mixer_forward(x, xn, p, w, nb, seq, ta, next_gain, next_pack=None):
    qkv, rest, fl = _mix_proj(xn, w["wm_t"])
    qa, ka, vm = _fox_prep(fl, p["bf"], qkv, nb, seq)
    if next_pack is None:
        (y_attn, lse), next_gathered = _fox_fwd(qa, ka, vm, nb, seq, ta), None
    else:
        y_attn, lse, next_gathered = _fox_fwd(qa, ka, vm, nb, seq, ta, carry_gather=next_pack)
        next_gathered = _fill_own_slot(next_gathered, next_pack)
    y_pool = _pool_fwd(rest, p["wbd"], p["scale"], seq)
    y_conv = _conv_fwd(rest, p["cw"], seq)
    y = jnp.concatenate([y_attn, y_pool, y_conv], axis=1)
    x_new, xn_next = _out_proj(y, w["wo"], x, 1.0, next_gain, "mix_out")
    return x_new, xn_next, dict(x=x, xn=xn, qa=qa, ka=ka, vm=vm, rest=rest, fl=fl, lse=lse, y=y), next_gathered


def _mixer_backward(dxo, dxo_b, p, w, sv, nb, seq, ta, gbuf, pending=None):
    t = dxo.shape[0]
    dy = _mm_nt(dxo_b, w["wo"], out_dtype=F32, name="mix_dy")
    gbuf = _mm_tn(sv["y"], dxo_b, name="mix_gw_out", into=_into_mix_out(gbuf))
    res = _fox_bwd(sv["qa"], sv["ka"], sv["vm"], sv["y"], dy, sv["lse"], nb, seq, ta, carry_exchange=pending)
    dq, dk, dv, d_rows, d_cols = res[:5]
    landed = None if pending is None else res[5]
    ddh = (d_rows.reshape(nb, N_HEADS, seq) - d_cols.reshape(nb, N_HEADS, seq)).transpose(0, 2, 1)
    ddh = ddh.reshape(t, N_HEADS)
    dfl, dbf = _fox_prep_bwd(jnp.pad(ddh, ((0, 0), (0, LANES - N_HEADS))), sv["fl"], p["bf"], seq)
    dpool, dwbd, dscale = _pool_bwd(sv["rest"], dy, p["wbd"], p["wbd_t"], p["scale"], seq)
    dconv, dcw = _conv_bwd(sv["rest"], dy, p["cw"], seq)
    dproj = jnp.concatenate([dq, dk.astype(BF16), dv.astype(BF16), dpool, dconv, dfl], axis=1)
    gbuf = _put_mix_in(gbuf, _mm_tn(dproj, sv["xn"], tm=D_INP // 3, name="mix_gw_in"))
    dx, dx_b, dg = _dxn_norm_bwd([dproj], w["wm_t"], sv["x"], p["norm"], dxo, name="mix_dxn_norm_bwd")
    return dx, dx_b, dict(norm=dg, bf=dbf, wbd=dwbd, scale=dscale, cw=dcw), gbuf, landed


def _block_diag(wp):
    z = jnp.zeros((POOL_GROUP, POOL_GROUP), wp.dtype)
    return jnp.concatenate(
        [jnp.concatenate([wp[g] if g == r else z for g in range(4)], axis=1) for r in range(4)], axis=0)


def _row_pad(a, rows):
    a = a.reshape(-1, a.shape[-1])
    return jnp.pad(a, ((0, rows - a.shape[0]), (0, 0)))


def kernel(x, norm_ffn1, w_ffn1_in, w_ffn1_out, norm_mix, w_mix_in, b_forget, w_pool, pool_scale, conv_w, w_mix_out, norm_ffn2, w_ffn2_in, w_ffn2_out, norm_final, loss_target, m_norm_ffn1, m_w_ffn1_in, m_w_ffn1_out, m_norm_mix, m_w_mix_in, m_b_forget, m_w_pool, m_pool_scale, m_conv_w, m_w_mix_out, m_norm_ffn2, m_w_ffn2_in, m_w_ffn2_out, m_norm_final, v_norm_ffn1, v_w_ffn1_in, v_w_ffn1_out, v_norm_mix, v_w_mix_in, v_b_forget, v_w_pool, v_pool_scale, v_conv_w, v_w_mix_out, v_norm_ffn2, v_w_ffn2_in, v_w_ffn2_out, v_norm_final):
    nb, seq, d = x.shape
    depth = norm_ffn1.shape[0]
    t = nb * seq
    ta = _pick(seq, (ATT_TILE, 128))
    my_id = 4 * lax.axis_index("x") + 2 * lax.axis_index("y") + lax.axis_index("c")
    cshard = conv_w.shape[-1]

    shards = (w_ffn1_in, w_ffn1_out, w_mix_in, w_mix_out, w_ffn2_in, w_ffn2_out)
    pack0 = _pack_shards(shards, 0, BF16)
    wg_head, cw_g = _comm_call("gather_weights_and_taps", gathers=[
        pack0[:PACK_HEAD], _row_pad(conv_w.reshape(depth * 3, cshard), 16).reshape(4, LANES)])
    cw_all = cw_g.reshape(N_DEV, 16, cshard)[:, :depth * 3].reshape(N_DEV, depth, 3, cshard)
    cw_all = cw_all.transpose(1, 2, 0, 3).reshape(depth, 3, D_CONV)

    xs = x.reshape(t, d)
    xn = _rmsnorm_fwd(xs, norm_ffn1[0][None], name="first_norm")
    saved = []
    for l in range(depth):
        wbd = _block_diag(w_pool[l])
        p = dict(norm=norm_mix[l][None], bf=jnp.pad(b_forget[l], (0, LANES - N_HEADS))[None],
                 wbd=wbd.astype(BF16), wbd_t=wbd.T.astype(BF16), scale=pool_scale[l][None],
                 cw=_row_pad(cw_all[l], 8))
        w = dict(f1=_ffn_weights(wg_head, 0))
        if l == 0:
            xs, xn, s1, wg_tail = _ffn_forward(xs, xn, w["f1"], norm_mix[l][None], carry_gather=pack0[PACK_HEAD:])
        else:
            xs, xn, s1 = _ffn_forward(xs, xn, w["f1"], norm_mix[l][None])
        w["mix"], w["f2"] = _tail_weights(wg_tail)
        next_pack = _pack_shards(shards, l + 1, BF16) if l + 1 < depth else None
        xs, xn, sm, wg = _mixer_forward(xs, xn, p, w["mix"], nb, seq, ta, norm_ffn2[l][None], next_pack)
        if wg is not None:
            wg_head, wg_tail = wg[:, :PACK_HEAD], wg[:, PACK_HEAD:]
        xs, xn, s2 = _ffn_forward(xs, xn, w["f2"], norm_ffn1[l + 1][None] if l + 1 < depth else None)
        saved.append((w, p, s1, sm, s2))

    dx, dx_b, g_norm_final, loss_part = _final_loss_bwd(xs, norm_final[None], loss_target.reshape(t, d))
    layer_g = [None] * depth
    small = [None] * depth
    chip_sums = None
    for l in reversed(range(depth)):
        w, p, s1, sm, s2 = saved[l]
        dx, dx_b, dg2, gbuf, _ = _ffn_backward(dx, dx_b, norm_ffn2[l][None], w["f2"], s2, None, "f2")
        dx, dx_b, gm, gbuf, landed = _mixer_backward(dx, dx_b, p, w["mix"], sm, nb, seq, ta, gbuf, chip_sums)
        if chip_sums is not None:
            layer_g[l + 1] = _sum_landed(chip_sums, landed)
        dx, dx_b, dg1, gbuf, theirs = _ffn_backward(dx, dx_b, norm_ffn1[l][None], w["f1"], s1, gbuf, "f1", exchange=True)
        chip_sums = _add_sibling(gbuf, theirs)
        small[l] = dict(n1=dg1, nm=gm["norm"], n2=dg2, bf=gm["bf"], wbd=gm["wbd"], scale=gm["scale"], cw=gm["cw"])
    grad_x = dx.reshape(nb, seq, d)

    def tile8(a):
        return jnp.pad(a, ((0, 8 - a.shape[0]), (0, D_MODEL - a.shape[1])))

    rows = []
    for l in range(depth):
        s = small[l]
        wp_rows = jnp.stack([s["wbd"][POOL_GROUP * g:POOL_GROUP * (g + 1), POOL_GROUP * g:POOL_GROUP * (g + 1)]
                             for g in range(4)]).reshape(16, D_MODEL)
        rows += [tile8(s["n1"]), tile8(s["nm"]), tile8(s["n2"]), tile8(s["bf"]), tile8(s["scale"]), tile8(s["cw"]),
                 wp_rows]
    rows += [tile8(g_norm_final), tile8(loss_part)]
    small_gathered, landed = _comm_call("exchange_grads_chips_gather_small", gathers=[jnp.concatenate(rows, axis=0)],
                                        chips=chip_sums)
    layer_g[0] = _sum_landed(chip_sums, landed)

    pieces = {}
    for nm, n in (("f1i", FFN_ROWS), ("f1o", OUT_ROWS), ("mi", MIX_ROWS), ("mo", MO_ROWS), ("f2i", FFN_ROWS),
                  ("f2o", OUT_ROWS)):
        pieces[nm] = jnp.stack([g[GRAD_AT[nm]:GRAD_AT[nm] + n] for g in layer_g])
    g_sharded = dict(
        w_ffn1_in=pieces["f1i"].transpose(0, 2, 1), w_ffn1_out=pieces["f1o"],
        w_mix_in=pieces["mi"].transpose(0, 2, 1), w_mix_out=pieces["mo"],
        w_ffn2_in=pieces["f2i"].transpose(0, 2, 1), w_ffn2_out=pieces["f2o"])

    per_layer = 6 * 8 + 16
    small_sum = _sum_slots(small_gathered, name="sum_small_grads")
    lay = small_sum[:depth * per_layer].reshape(depth, per_layer, D_MODEL)
    g_small = dict(
        norm_ffn1=lay[:, 0], norm_mix=lay[:, 8], norm_ffn2=lay[:, 16], b_forget=lay[:, 24, :N_HEADS],
        pool_scale=lay[:, 32, :D_POOL],
        conv_w=lax.dynamic_slice_in_dim(lay[:, 40:43, :D_CONV], my_id * cshard, cshard, axis=2),
        w_pool=lay[:, 48:64].reshape(depth, 4, POOL_GROUP, POOL_GROUP),
        norm_final=small_sum[depth * per_layer])
    loss = small_sum[depth * per_layer + 8, 0]

    given = dict(norm_ffn1=(norm_ffn1, m_norm_ffn1, v_norm_ffn1), w_ffn1_in=(w_ffn1_in, m_w_ffn1_in, v_w_ffn1_in),
                 w_ffn1_out=(w_ffn1_out, m_w_ffn1_out, v_w_ffn1_out), norm_mix=(norm_mix, m_norm_mix, v_norm_mix),
                 w_mix_in=(w_mix_in, m_w_mix_in, v_w_mix_in), b_forget=(b_forget, m_b_forget, v_b_forget),
                 w_pool=(w_pool, m_w_pool, v_w_pool), pool_scale=(pool_scale, m_pool_scale, v_pool_scale),
                 conv_w=(conv_w, m_conv_w, v_conv_w), w_mix_out=(w_mix_out, m_w_mix_out, v_w_mix_out),
                 norm_ffn2=(norm_ffn2, m_norm_ffn2, v_norm_ffn2), w_ffn2_in=(w_ffn2_in, m_w_ffn2_in, v_w_ffn2_in),
                 w_ffn2_out=(w_ffn2_out, m_w_ffn2_out, v_w_ffn2_out), norm_final=(norm_final, m_norm_final, v_norm_final))
    names = list(given)
    grads, deltas, new_m, new_v = {}, {}, {}, {}
    for nm in names:
        wv, mv, vv = given[nm]
        gv = (g_sharded[nm] if nm in g_sharded else g_small[nm]).reshape(wv.shape)
        shape2 = (-1, wv.shape[-1]) if wv.ndim > 1 else (1, wv.shape[0])
        dl, mn, vn = _adamw(wv.reshape(shape2), gv.reshape(shape2), mv.reshape(shape2), vv.reshape(shape2),
                            name="adamw_" + nm)
        grads[nm], deltas[nm], new_m[nm], new_v[nm] = gv, dl.reshape(wv.shape), mn.reshape(wv.shape), vn.reshape(wv.shape)
    return (loss, grad_x, *[grads[n] for n in names], *[deltas[n] for n in names],
            *[new_m[n] for n in names], *[new_v[n] for n in names])
```

```python
import functools

import jax
import jax.numpy as jnp
from jax import lax
from jax.experimental import pallas as pl
from jax.experimental.pallas import tpu as pltpu

F32 = jnp.float32
BF16 = jnp.bfloat16

D_MODEL = 1024
D_FF = 2816
HEAD_DIM = 64
N_HEADS = 8
N_PAIRS = N_HEADS // 2
D_ATTN = 512
D_POOL = 256
D_CONV = 256
POOL_WINDOWS = (2, 4, 8, 16)
POOL_GROUP = 64
D_IN = 2568
RMS_EPS = 1e-6
ADAM_LR, ADAM_B1, ADAM_B2, ADAM_EPS, ADAM_WD, ADAM_STEP = 0.001, 0.9, 0.999, 1e-08, 0.01, 10

N_DEV = 8
N_CHIPS = 4
LANES = 128
VMEM_BYTES_V7X = 64 * 1024 * 1024
VMEM_LIMIT_MAX = VMEM_BYTES_V7X - 8 * 1024 * 1024

F_HALF = D_FF // 2
D_QKV = 3 * D_ATTN
D_REST = D_POOL + 3 * D_CONV
D_INP = D_QKV + D_REST + LANES
MIX_ROWS = 321
MIX_ROWS_PAD = 336
FFN_ROWS = 704
OUT_ROWS = 352
MO_ROWS = 128
LAYER_ROWS = 2 * (FFN_ROWS + OUT_ROWS) + MIX_ROWS_PAD + MO_ROWS
NEG_BIG = -1e30
ATT_SCALE = HEAD_DIM ** -0.5
ATT_K = 2 * LANES
ATT_TILE = 256
ATT_PAIRS_FWD = 4
ATT_PAIRS_BWD = 4
MXU_COLS = 256


def _cparams(sem, vmem_bytes):
    limit = int(min(max(vmem_bytes, 16 * 1024 * 1024), VMEM_LIMIT_MAX))
    return pltpu.CompilerParams(dimension_semantics=sem, vmem_limit_bytes=limit)


def _nbytes(shape, dtype):
    n = 1
    for s in shape:
        n *= s
    return n * jnp.dtype(dtype).itemsize


def _pick(n, prefs):
    for p in prefs:
        if n % p == 0:
            return p
    return n


def _rmsnorm_fwd(x, g, name):
    t, d = x.shape
    tm = _pick(t, (512, 256, 128))

    def body(x_ref, g_ref, o_ref):
        xv = x_ref[...]
        r = lax.rsqrt(jnp.mean(xv * xv, axis=-1, keepdims=True) + RMS_EPS)
        o_ref[...] = ((xv * r) * g_ref[...]).astype(o_ref.dtype)

    return pl.pallas_call(
        body, grid=(t // tm,),
        in_specs=[pl.BlockSpec((tm, d), lambda i: (i, 0)), pl.BlockSpec((1, d), lambda i: (0, 0))],
        out_specs=pl.BlockSpec((tm, d), lambda i: (i, 0)),
        out_shape=jax.ShapeDtypeStruct((t, d), BF16), name=name,
        compiler_params=_cparams(("parallel",), 6 * tm * d * 4),
    )(x, g)


def _mm_nn(a, b, *, out_dtype, name, res=None, alpha=1.0, tn=None, next_gain=None):
    m, k = a.shape
    n = b.shape[1]
    tn = n if tn is None else tn
    tm = _pick(m, (512, 256, 128))
    with_res = res is not None
    with_norm = next_gain is not None
    assert not with_norm or tn == n

    def body(*refs):
        refs = list(refs)
        a_ref, b_ref = refs[:2]
        r_ref = refs[2] if with_res else None
        g_ref = refs[2 + with_res] if with_norm else None
        o_ref = refs[2 + with_res + with_norm]
        acc = jnp.dot(a_ref[...], b_ref[...], preferred_element_type=F32)
        if with_res:
            acc = r_ref[...] + alpha * acc
        o_ref[...] = acc.astype(o_ref.dtype)
        if with_norm:
            r = lax.rsqrt(jnp.mean(acc * acc, axis=-1, keepdims=True) + RMS_EPS)
            refs[-1][...] = ((acc * r) * g_ref[...]).astype(BF16)

    in_specs = [pl.BlockSpec((tm, k), lambda j, i: (i, 0)), pl.BlockSpec((k, tn), lambda j, i: (0, j))]
    args = [a, b]
    out_blk = pl.BlockSpec((tm, tn), lambda j, i: (i, j))
    out_specs, out_shape = [out_blk], [jax.ShapeDtypeStruct((m, n), out_dtype)]
    if with_res:
        in_specs.append(out_blk)
        args.append(res)
    if with_norm:
        in_specs.append(pl.BlockSpec((1, n), lambda j, i: (0, 0)))
        args.append(next_gain)
        out_specs.append(out_blk)
        out_shape.append(pltpu.HBM((m, n), BF16))
    vmem = 2 * (_nbytes((tm, k), BF16) + _nbytes((k, tn), BF16) + 4 * _nbytes((tm, tn), F32))
    outs = pl.pallas_call(
        body, grid=(n // tn, m // tm), in_specs=in_specs, out_specs=out_specs, out_shape=out_shape, name=name,
        compiler_params=_cparams(("parallel", "parallel"), vmem),
    )(*args)
    return outs if with_norm else outs[0]


def _mm_nt(a, b_t, *, out_dtype, name):
    m, k = a.shape
    n = b_t.shape[0]
    tm = _pick(m, (512, 256, 128))

    def body(a_ref, b_ref, o_ref):
        o_ref[...] = _nt(a_ref[...], b_ref[...]).astype(o_ref.dtype)

    vmem = 2 * (_nbytes((tm, k), BF16) + _nbytes((n, k), BF16) + 3 * _nbytes((tm, n), F32))
    return pl.pallas_call(
        body, grid=(m // tm,),
        in_specs=[pl.BlockSpec((tm, k), lambda i: (i, 0)), pl.BlockSpec((n, k), lambda i: (0, 0))],
        out_specs=pl.BlockSpec((tm, n), lambda i: (i, 0)), out_shape=jax.ShapeDtypeStruct((m, n), out_dtype),
        name=name, compiler_params=_cparams(("parallel",), vmem),
    )(a, b_t)


def _mix_proj(xn, wm_t):
    t, d = xn.shape
    tm = _pick(t, (512, 256, 128))

    def body(x_ref, w_ref, qkv_ref, rest_ref, f_ref):
        xv = x_ref[...]
        for c0, cw in _col_chunks(D_QKV, MXU_COLS):
            qkv_ref[:, c0:c0 + cw] = _nt(xv, w_ref[c0:c0 + cw, :]).astype(qkv_ref.dtype)
        for c0, cw in _col_chunks(D_REST, MXU_COLS):
            rest_ref[:, c0:c0 + cw] = _nt(xv, w_ref[D_QKV + c0:D_QKV + c0 + cw, :])
        f_ref[...] = _nt(xv, w_ref[D_QKV + D_REST:, :])

    def rows(n):
        return pl.BlockSpec((tm, n), lambda i: (i, 0))

    vmem = 2 * (_nbytes((tm, d), BF16) + _nbytes((D_INP, d), BF16) + 3 * _nbytes((tm, D_INP), F32))
    return pl.pallas_call(
        body, grid=(t // tm,), in_specs=[rows(d), pl.BlockSpec((D_INP, d), lambda i: (0, 0))],
        out_specs=[rows(D_QKV), rows(D_REST), rows(LANES)],
        out_shape=[jax.ShapeDtypeStruct((t, D_QKV), BF16), jax.ShapeDtypeStruct((t, D_REST), F32),
                   jax.ShapeDtypeStruct((t, LANES), F32)],
        name="mix_proj", compiler_params=_cparams(("parallel",), vmem),
    )(xn, wm_t)


def _mm_tn(a, b, *, name, alpha=1.0, tm=None, into=None):
    t, m = a.shape
    n = b.shape[1]
    tm = m if tm is None else tm
    tk = _pick(t, (2048, 1024, 512, 256, 128))
    nk = t // tk

    def body(a_ref, b_ref, *rest):
        o_ref = rest[-1]
        kk = pl.program_id(1)
        p = lax.dot_general(a_ref[...], b_ref[...], (((0,), (0,)), ((), ())), preferred_element_type=F32)
        if alpha != 1.0:
            p = alpha * p
        p = p.reshape(o_ref.shape)

        @pl.when(kk == 0)
        def _():
            o_ref[...] = p

        @pl.when(kk > 0)
        def _():
            o_ref[...] += p

    vmem = 2 * (_nbytes((tk, tm), BF16) + _nbytes((tk, n), BF16) + 2 * _nbytes((tm, n), F32))
    in_specs = [pl.BlockSpec((tk, tm), lambda i, kk: (kk, i)), pl.BlockSpec((tk, n), lambda i, kk: (kk, 0))]
    cp = _cparams(("parallel", "arbitrary"), vmem)
    if into is None:
        return pl.pallas_call(
            body, grid=(m // tm, nk), in_specs=in_specs, out_specs=pl.BlockSpec((tm, n), lambda i, kk: (i, 0)),
            out_shape=jax.ShapeDtypeStruct((m, n), F32), name=name, compiler_params=cp,
        )(a, b)
    buf, buf_shape, blk, index = into
    out_spec = pl.BlockSpec(blk, lambda i, kk: index(i))
    out_shape = jax.ShapeDtypeStruct(buf_shape, F32)
    if buf is None:
        return pl.pallas_call(body, grid=(m // tm, nk), in_specs=in_specs, out_specs=out_spec, out_shape=out_shape,
                              name=name + "_new", compiler_params=cp)(a, b)
    return pl.pallas_call(
        body, grid=(m // tm, nk), in_specs=in_specs + [pl.BlockSpec(memory_space=pl.ANY)], out_specs=out_spec,
        out_shape=out_shape, input_output_aliases={2: 0}, name=name + "_into", compiler_params=cp,
    )(a, b, buf)


def _sigmoid(v):
    return 1.0 / (1.0 + jnp.exp(-v))


def _col_chunks(n, width):
    return [(c, min(width, n - c)) for c in range(0, n, width)]


def _ffn_in(xn, w_t, name, carry_gather=None):
    t, d = xn.shape
    tm = _pick(t, (1024, 512, 256, 128))
    grid = (2, t // tm)

    def body(x_ref, wg_ref, wu_ref, *rest):
        if carry_gather is None:
            h_ref, pg_ref, pu_ref = rest
        else:
            blk_ref, h_ref, pg_ref, pu_ref, gathered_ref, send_sems, recv_sems = rest
            first_step, last_step = _grid_ends([pl.program_id(a) for a in range(2)], grid)

            @pl.when(first_step)
            def _():
                _gather_start(blk_ref, gathered_ref, send_sems, recv_sems)

        xv = x_ref[...]
        for c0, cw in _col_chunks(F_HALF, MXU_COLS):
            cols = slice(c0, c0 + cw)
            g = _nt(xv, wg_ref[cols, :])
            u = _nt(xv, wu_ref[cols, :])
            s = _sigmoid(g)
            silu = g * s
            h_ref[:, cols] = (silu * u).astype(h_ref.dtype)
            pg_ref[:, cols] = (u * (s * (1.0 + g * (1.0 - s)))).astype(pg_ref.dtype)
            pu_ref[:, cols] = silu.astype(pu_ref.dtype)

        if carry_gather is not None:
            @pl.when(last_step)
            def _():
                _gather_finish(blk_ref, gathered_ref, send_sems, recv_sems)

    vmem = 2 * (_nbytes((tm, d), BF16) + 2 * _nbytes((d, F_HALF), BF16) + 4 * _nbytes((tm, D_FF), F32))
    out_blk = pl.BlockSpec((tm, F_HALF), lambda j, i: (i, j))
    sds = jax.ShapeDtypeStruct((t, D_FF), BF16)
    in_specs = [pl.BlockSpec((tm, d), lambda j, i: (i, 0)), pl.BlockSpec((F_HALF, d), lambda j, i: (j, 0)),
                pl.BlockSpec((F_HALF, d), lambda j, i: (2 + j, 0))]
    if carry_gather is None:
        return pl.pallas_call(
            body, grid=grid, in_specs=in_specs, out_specs=[out_blk, out_blk, out_blk], out_shape=[sds, sds, sds],
            name=name, compiler_params=_cparams(("parallel", "parallel"), vmem),
        )(xn, w_t, w_t)
    any_spec = pl.BlockSpec(memory_space=pl.ANY)
    return pl.pallas_call(
        body, grid=grid, in_specs=in_specs + [any_spec], out_specs=[out_blk, out_blk, out_blk, any_spec],
        out_shape=[sds, sds, sds, jax.ShapeDtypeStruct((N_DEV,) + carry_gather.shape, carry_gather.dtype)],
        scratch_shapes=list(GATHER_SEMS), name=name + "_gather",
        compiler_params=_cparams(("arbitrary", "arbitrary"), vmem),
    )(xn, w_t, w_t, carry_gather)


def _ffn_bwd_mid(dxo, w_out, pg, pu, name):
    t, d = dxo.shape
    tm = _pick(t, (1024, 512, 256, 128))

    def body(d_ref, w_ref, pg_ref, pu_ref, dg_ref, du_ref):
        dv = d_ref[...]
        for c0, cw in _col_chunks(F_HALF, MXU_COLS):
            cols = slice(c0, c0 + cw)
            dh = 0.5 * _nt(dv, w_ref[cols, :])
            dg_ref[:, cols] = (dh * pg_ref[:, cols].astype(F32)).astype(dg_ref.dtype)
            du_ref[:, cols] = (dh * pu_ref[:, cols].astype(F32)).astype(du_ref.dtype)

    vmem = 2 * (_nbytes((tm, d), BF16) + _nbytes((d, F_HALF), BF16) + 5 * _nbytes((tm, D_FF), F32))
    blk = pl.BlockSpec((tm, F_HALF), lambda j, i: (i, j))
    sds = jax.ShapeDtypeStruct((t, D_FF), BF16)
    return pl.pallas_call(
        body, grid=(2, t // tm),
        in_specs=[pl.BlockSpec((tm, d), lambda j, i: (i, 0)), pl.BlockSpec((F_HALF, d), lambda j, i: (j, 0)), blk, blk],
        out_specs=[blk, blk], out_shape=[sds, sds], name=name,
        compiler_params=_cparams(("parallel", "parallel"), vmem),
    )(dxo, w_out, pg, pu)


def _dxn_norm_bwd(parts, b, x, g, dxo, name, carry_sibling=None):
    t, d = x.shape
    k = parts[0].shape[1]
    n_parts = len(parts)
    tm = _pick(t, (256, 128))
    grid = (t // tm,)

    def body(*refs):
        a_refs, b_refs = refs[:n_parts], refs[n_parts:2 * n_parts]
        if carry_sibling is None:
            x_ref, g_ref, do_ref, dx_ref, dxb_ref, dg_ref = refs[2 * n_parts:]
        else:
            x_ref, g_ref, do_ref, g4_ref, dx_ref, dxb_ref, dg_ref, theirs_ref, send_sems, recv_sems = refs[2 * n_parts:]
            first_step, last_step = _grid_ends([pl.program_id(0)], grid)

            @pl.when(first_step)
            def _():
                _sibling_start(g4_ref, theirs_ref, send_sems, recv_sems)

        i = pl.program_id(0)
        dn = jnp.dot(a_refs[0][...], b_refs[0][...], preferred_element_type=F32)
        for a_ref, b_ref in zip(a_refs[1:], b_refs[1:]):
            dn = dn + jnp.dot(a_ref[...], b_ref[...], preferred_element_type=F32)
        xv = x_ref[...]
        r = lax.rsqrt(jnp.mean(xv * xv, axis=-1, keepdims=True) + RMS_EPS)
        xh = xv * r
        dgp = jnp.sum(dn * xh, axis=0, keepdims=True)
        dh = dn * g_ref[...]
        dx = do_ref[...] + r * (dh - xh * jnp.mean(dh * xh, axis=-1, keepdims=True))
        dx_ref[...] = dx
        dxb_ref[...] = dx.astype(dxb_ref.dtype)

        @pl.when(i == 0)
        def _():
            dg_ref[...] = dgp

        @pl.when(i > 0)
        def _():
            dg_ref[...] += dgp

        if carry_sibling is not None:
            @pl.when(last_step)
            def _():
                _sibling_finish(g4_ref, theirs_ref, send_sems, recv_sems)

    blk = pl.BlockSpec((tm, d), lambda i: (i, 0))
    row = pl.BlockSpec((1, d), lambda i: (0, 0))
    a_specs = [pl.BlockSpec((tm, k), lambda i: (i, 0)) for _ in parts]
    b_specs = [pl.BlockSpec((k, d), lambda i, kk=kk: (kk, 0)) for kk in range(n_parts)]
    vmem = 2 * n_parts * (_nbytes((tm, k), BF16) + _nbytes((k, d), BF16)) + 16 * tm * d * 4
    in_specs = a_specs + b_specs + [blk, row, blk]
    out_shape = [pltpu.HBM((t, d), F32), pltpu.HBM((t, d), BF16), jax.ShapeDtypeStruct((1, d), F32)]
    args = (*parts, *([b] * n_parts), x, g, dxo)
    if carry_sibling is None:
        return pl.pallas_call(body, grid=grid, in_specs=in_specs, out_specs=[blk, blk, row], out_shape=out_shape,
                              name=name, compiler_params=_cparams(("arbitrary",), vmem))(*args)
    nchip, _, r, c = carry_sibling.shape
    any_spec = pl.BlockSpec(memory_space=pl.ANY)
    return pl.pallas_call(
        body, grid=grid, in_specs=in_specs + [any_spec], out_specs=[blk, blk, row, any_spec],
        out_shape=out_shape + [jax.ShapeDtypeStruct((nchip, r, c), carry_sibling.dtype)],
        scratch_shapes=list(SIBLING_SEMS), name=name + "_exchange", compiler_params=_cparams(("arbitrary",), vmem),
    )(*args, carry_sibling)


def _final_loss_bwd(x, g, tgt):
    t, d = x.shape
    tm = _pick(t, (512, 256, 128))

    def body(x_ref, g_ref, t_ref, dx_ref, dxb_ref, dg_ref, loss_ref):
        i = pl.program_id(0)
        xv = x_ref[...]
        r = lax.rsqrt(jnp.mean(xv * xv, axis=-1, keepdims=True) + RMS_EPS)
        xh = xv * r
        gv = g_ref[...]
        err = xh * gv - t_ref[...]
        lp = 0.5 * jnp.sum(jnp.mean(err * err, axis=-1, keepdims=True), axis=0, keepdims=True)
        dy = err * (1.0 / d)
        dgp = jnp.sum(dy * xh, axis=0, keepdims=True)
        dh = dy * gv
        dx = r * (dh - xh * jnp.mean(dh * xh, axis=-1, keepdims=True))
        dx_ref[...] = dx
        dxb_ref[...] = dx.astype(dxb_ref.dtype)
        lpb = jnp.broadcast_to(lp, (1, LANES))

        @pl.when(i == 0)
        def _():
            dg_ref[...] = dgp
            loss_ref[...] = lpb

        @pl.when(i > 0)
        def _():
            dg_ref[...] += dgp
            loss_ref[...] += lpb

    blk = pl.BlockSpec((tm, d), lambda i: (i, 0))
    row = pl.BlockSpec((1, d), lambda i: (0, 0))
    return pl.pallas_call(
        body, grid=(t // tm,), in_specs=[blk, row, blk],
        out_specs=[blk, blk, row, pl.BlockSpec((1, LANES), lambda i: (0, 0))],
        out_shape=[jax.ShapeDtypeStruct((t, d), F32), jax.ShapeDtypeStruct((t, d), BF16),
                   jax.ShapeDtypeStruct((1, d), F32), jax.ShapeDtypeStruct((1, LANES), F32)], name="final_loss_bwd",
        compiler_params=_cparams(("arbitrary",), 16 * tm * d * 4),
    )(x, g, tgt)


def _seq_scan(v, seq, reverse):
    row = lax.broadcasted_iota(jnp.int32, v.shape, 0)
    k = 1
    while k < seq:
        if reverse:
            v = v + jnp.where(row < seq - k, pltpu.roll(v, seq - k, 0), 0.0)
        else:
            v = v + jnp.where(row >= k, pltpu.roll(v, k, 0), 0.0)
        k *= 2
    return v


def _log_sigmoid(v):
    return jnp.minimum(v, 0.0) - jnp.log(1.0 + jnp.exp(-jnp.abs(v)))


def _fox_prep(fl, bf, qkv, nb, seq):
    def body(f_ref, b_ref, q_ref, k_ref, v_ref, qa_ref, ka_ref, vm_ref):
        dsum = _seq_scan(_log_sigmoid(f_ref[...] + b_ref[...]), seq, False)
        d1 = dsum.astype(BF16).astype(F32)
        r1 = dsum - d1
        d2 = r1.astype(BF16).astype(F32)
        d3 = (r1 - d2).astype(BF16).astype(F32)
        lane = lax.broadcasted_iota(jnp.int32, (seq, LANES), 1)
        first = lane < HEAD_DIM
        l64 = jnp.where(first, lane, lane - HEAD_DIM)
        for p in range(N_PAIRS):
            def head_cols(a, p=p):
                return jnp.where(first, a[:, 2 * p:2 * p + 1], a[:, 2 * p + 1:2 * p + 2])

            e1, e2, e3 = head_cols(d1), head_cols(d2), head_cols(d3)
            aux_q = jnp.where(l64 == 0, e1, jnp.where(l64 == 1, e2, jnp.where(l64 == 2, e3,
                              jnp.where(l64 < 6, 1.0, 0.0)))).astype(BF16)
            aux_k = jnp.where(l64 < 3, 1.0, jnp.where(l64 == 3, -e1, jnp.where(l64 == 4, -e2,
                              jnp.where(l64 == 5, -e3, 0.0)))).astype(BF16)
            cols = slice(LANES * p, LANES * (p + 1))
            qs = q_ref[:, cols] * ATT_SCALE
            vp = v_ref[:, cols]
            zero = jnp.zeros_like(qs)
            qa_ref[0, p, 0, :, :LANES] = jnp.where(first, qs, zero)
            qa_ref[0, p, 0, :, LANES:] = jnp.where(first, aux_q, zero)
            qa_ref[0, p, 1, :, :LANES] = jnp.where(first, zero, qs)
            qa_ref[0, p, 1, :, LANES:] = jnp.where(first, zero, aux_q)
            ka_ref[0, p, :, :LANES] = k_ref[:, cols]
            ka_ref[0, p, :, LANES:] = aux_k
            vm_ref[0, p, 0] = jnp.where(first, vp, zero)
            vm_ref[0, p, 1] = jnp.where(first, zero, vp)

    def part(c):
        return pl.BlockSpec((seq, D_ATTN), lambda b, c=c: (b, c))

    return pl.pallas_call(
        body, grid=(nb,),
        in_specs=[pl.BlockSpec((seq, LANES), lambda b: (b, 0)), pl.BlockSpec((1, LANES), lambda b: (0, 0)),
                  part(0), part(1), part(2)],
        out_specs=[pl.BlockSpec((1, N_PAIRS, 2, seq, ATT_K), lambda b: (b, 0, 0, 0, 0)),
                   pl.BlockSpec((1, N_PAIRS, seq, ATT_K), lambda b: (b, 0, 0, 0)),
                   pl.BlockSpec((1, N_PAIRS, 2, seq, LANES), lambda b: (b, 0, 0, 0, 0))],
        out_shape=[jax.ShapeDtypeStruct((nb, N_PAIRS, 2, seq, ATT_K), BF16),
                   jax.ShapeDtypeStruct((nb, N_PAIRS, seq, ATT_K), BF16),
                   jax.ShapeDtypeStruct((nb, N_PAIRS, 2, seq, LANES), BF16)],
        name="fox_prep", compiler_params=_cparams(("parallel",), 48 * 1024 * 1024),
    )(fl, bf, qkv, qkv, qkv)


def _fox_prep_bwd(dd, fl, bf, seq):
    t = fl.shape[0]

    def body(d_ref, f_ref, b_ref, o_ref, db_ref):
        i = pl.program_id(0)
        dlog = _seq_scan(d_ref[...], seq, True)
        dfl = dlog * _sigmoid(-(f_ref[...] + b_ref[...]))
        o_ref[...] = dfl.astype(o_ref.dtype)
        dbp = jnp.sum(dfl, axis=0, keepdims=True)

        @pl.when(i == 0)
        def _():
            db_ref[...] = dbp

        @pl.when(i > 0)
        def _():
            db_ref[...] += dbp

    blk = pl.BlockSpec((seq, LANES), lambda b: (b, 0))
    row = pl.BlockSpec((1, LANES), lambda b: (0, 0))
    return pl.pallas_call(
        body, grid=(t // seq,), in_specs=[blk, blk, row], out_specs=[blk, row],
        out_shape=[jax.ShapeDtypeStruct((t, LANES), BF16), jax.ShapeDtypeStruct((1, LANES), F32)], name="fox_prep_bwd",
        compiler_params=_cparams(("arbitrary",), 24 * seq * LANES * 4),
    )(dd, fl, bf)


def _pair_rows(a, ta):
    lane = lax.broadcasted_iota(jnp.int32, (ta, LANES), 1)
    return jnp.where(lane < HEAD_DIM, a[:ta], a[ta:])


def _diag_mask(ta):
    r = lax.broadcasted_iota(jnp.int32, (2 * ta, ta), 0)
    c = lax.broadcasted_iota(jnp.int32, (2 * ta, ta), 1)
    return c <= jnp.where(r >= ta, r - ta, r)


def _nt(a, b):
    return lax.dot_general(a, b, (((1,), (1,)), ((), ())), preferred_element_type=F32)


def _tn(a, b):
    return lax.dot_general(a, b, (((0,), (0,)), ((), ())), preferred_element_type=F32)


def _grid_ends(ids, sizes):
    first = functools.reduce(jnp.logical_and, [i == 0 for i in ids])
    last = functools.reduce(jnp.logical_and, [i == n - 1 for i, n in zip(ids, sizes)])
    return first, last


def _fox_fwd(qa, ka, vm, nb, seq, ta, carry_gather=None):
    nq = seq // ta
    npp = ATT_PAIRS_FWD
    grid = (nb, N_PAIRS // npp, nq)

    def body(q_ref, k_ref, v_ref, *rest):
        if carry_gather is None:
            o_ref, lse_ref = rest
        else:
            x_ref, o_ref, lse_ref, gathered_ref, send_sems, recv_sems = rest
            first_step, last_step = _grid_ends([pl.program_id(a) for a in range(3)], grid)

            @pl.when(first_step)
            def _():
                _gather_start(x_ref, gathered_ref, send_sems, recv_sems)

        i = pl.program_id(2)
        q2s = [q_ref[0, pp].reshape(2 * ta, ATT_K) for pp in range(npp)]

        def step(j, carry, masked):
            rows = pl.ds(pl.multiple_of(j * ta, ta), ta)
            out = []
            for pp in range(npp):
                m, l, acc = carry[pp]
                s = _nt(q2s[pp], k_ref[0, pp, rows, :])
                if masked:
                    s = jnp.where(_diag_mask(ta), s, NEG_BIG)
                m_new = jnp.maximum(m, jnp.max(s, axis=-1, keepdims=True))
                p = jnp.exp(s - m_new)
                corr = jnp.exp(m - m_new)
                l = corr * l + jnp.sum(p, axis=-1, keepdims=True)
                pb = p.astype(BF16)
                pv = (jnp.dot(pb[:ta], v_ref[0, pp, 0, rows, :], preferred_element_type=F32)
                      + jnp.dot(pb[ta:], v_ref[0, pp, 1, rows, :], preferred_element_type=F32))
                out.append((m_new, l, _pair_rows(corr, ta) * acc + pv))
            return tuple(out)

        init = tuple((jnp.full((2 * ta, 1), NEG_BIG, F32), jnp.zeros((2 * ta, 1), F32),
                      jnp.zeros((ta, LANES), F32)) for _ in range(npp))
        carry = lax.fori_loop(0, i, functools.partial(step, masked=False), init)
        for pp, (m, l, acc) in enumerate(step(i, carry, True)):
            o_ref[:, LANES * pp:LANES * (pp + 1)] = (acc * _pair_rows(1.0 / l, ta)).astype(o_ref.dtype)
            lse = m + jnp.log(l)
            lse_ref[0, pp, 0] = lse[:ta]
            lse_ref[0, pp, 1] = lse[ta:]

        if carry_gather is not None:
            @pl.when(last_step)
            def _():
                _gather_finish(x_ref, gathered_ref, send_sems, recv_sems)

    vmem = (2 * npp * (_nbytes((seq, ATT_K), BF16) + 2 * _nbytes((seq, LANES), BF16)) + 24 * npp * ta * ta * 4
            + 8 * 1024 * 1024)
    in_specs = [pl.BlockSpec((1, npp, 2, ta, ATT_K), lambda b, g, i: (b, g, 0, i, 0)),
                pl.BlockSpec((1, npp, seq, ATT_K), lambda b, g, i: (b, g, 0, 0)),
                pl.BlockSpec((1, npp, 2, seq, LANES), lambda b, g, i: (b, g, 0, 0, 0))]
    out_specs = [pl.BlockSpec((ta, LANES * npp), lambda b, g, i: (b * nq + i, g)),
                 pl.BlockSpec((1, npp, 2, ta, 1), lambda b, g, i: (b, g, 0, i, 0))]
    out_shape = [jax.ShapeDtypeStruct((nb * seq, D_ATTN), BF16), jax.ShapeDtypeStruct((nb, N_PAIRS, 2, seq, 1), F32)]
    if carry_gather is None:
        return pl.pallas_call(
            body, grid=grid, in_specs=in_specs, out_specs=out_specs, out_shape=out_shape, name="fox_fwd",
            compiler_params=_cparams(("parallel", "parallel", "parallel"), vmem),
        )(qa, ka, vm)
    any_spec = pl.BlockSpec(memory_space=pl.ANY)
    return pl.pallas_call(
        body, grid=grid, in_specs=in_specs + [any_spec], out_specs=out_specs + [any_spec],
        out_shape=out_shape + [jax.ShapeDtypeStruct((N_DEV,) + carry_gather.shape, carry_gather.dtype)],
        scratch_shapes=list(GATHER_SEMS), name="fox_fwd_gather",
        compiler_params=_cparams(("arbitrary", "arbitrary", "arbitrary"), vmem),
    )(qa, ka, vm, carry_gather)


def _fox_bwd(qa, ka, vm, y, dy, lse, nb, seq, ta, carry_exchange=None):
    nq = seq // ta
    npp = ATT_PAIRS_BWD
    grid = (nb, N_PAIRS // npp, nq)

    def body(q_ref, k_ref, v_ref, o_ref, do_ref, lse_ref, *rest):
        if carry_exchange is None:
            dq_ref, dk_ref, dv_ref, rs_ref, cs_ref = rest
        else:
            t_ref, dq_ref, dk_ref, dv_ref, rs_ref, cs_ref, landed_ref, send_sems, recv_sems = rest
            first_step, last_step = _grid_ends([pl.program_id(a) for a in range(3)], grid)

            @pl.when(first_step)
            def _():
                _chips_start(t_ref, landed_ref, send_sems, recv_sems)

        i = pl.program_id(2)

        @pl.when(i == 0)
        def _():
            dk_ref[...] = jnp.zeros_like(dk_ref)
            dv_ref[...] = jnp.zeros_like(dv_ref)
            cs_ref[...] = jnp.zeros_like(cs_ref)

        first = lax.broadcasted_iota(jnp.int32, (ta, LANES), 1) < HEAD_DIM
        q2s, do2s, deltas, lses = [], [], [], []
        for pp in range(npp):
            cols = slice(LANES * pp, LANES * (pp + 1))
            q2s.append(q_ref[0, pp].reshape(2 * ta, ATT_K))
            do = do_ref[:, cols]
            doo = do * o_ref[:, cols].astype(F32)
            do2s.append(jnp.concatenate([jnp.where(first, do, 0.0), jnp.where(first, 0.0, do)], axis=0).astype(BF16))
            deltas.append(jnp.concatenate([jnp.sum(jnp.where(first, doo, 0.0), axis=-1, keepdims=True),
                                           jnp.sum(jnp.where(first, 0.0, doo), axis=-1, keepdims=True)], axis=0))
            lses.append(jnp.concatenate([lse_ref[0, pp, 0], lse_ref[0, pp, 1]], axis=0))

        def step(j, carry, masked):
            rows = pl.ds(pl.multiple_of(j * ta, ta), ta)
            out = []
            for pp in range(npp):
                dq_acc, rs_acc = carry[pp]
                cols = slice(LANES * pp, LANES * (pp + 1))
                ks = k_ref[0, pp, rows, :]
                s = _nt(q2s[pp], ks)
                if masked:
                    s = jnp.where(_diag_mask(ta), s, NEG_BIG)
                p = jnp.exp(s - lses[pp])
                dp = _nt(do2s[pp], v_ref[0, pp, 0, rows, :] + v_ref[0, pp, 1, rows, :])
                ds32 = p * (dp - deltas[pp])
                ds = ds32.astype(BF16)
                dk_ref[rows, cols] += _tn(ds, q2s[pp][:, :LANES])
                dv_ref[rows, cols] += _tn(p.astype(BF16), do2s[pp])
                cs_ref[0, pp, 0, j] += jnp.sum(ds32[:ta], axis=0, keepdims=True)
                cs_ref[0, pp, 1, j] += jnp.sum(ds32[ta:], axis=0, keepdims=True)
                out.append((dq_acc + jnp.dot(ds, ks[:, :LANES], preferred_element_type=F32),
                            rs_acc + jnp.sum(ds32, axis=-1, keepdims=True)))
            return tuple(out)

        init = tuple((jnp.zeros((2 * ta, LANES), F32), jnp.zeros((2 * ta, 1), F32)) for _ in range(npp))
        carry = lax.fori_loop(0, i, functools.partial(step, masked=False), init)
        for pp, (dq_acc, rs_acc) in enumerate(step(i, carry, True)):
            dq = jnp.where(first, dq_acc[:ta], dq_acc[ta:]) * ATT_SCALE
            dq_ref[:, LANES * pp:LANES * (pp + 1)] = dq.astype(dq_ref.dtype)
            rs_row = jnp.transpose(jnp.broadcast_to(rs_acc, (2 * ta, LANES)))[0:1]
            rs_ref[0, pp, 0, 0] = rs_row[:, :ta]
            rs_ref[0, pp, 1, 0] = rs_row[:, ta:]

        if carry_exchange is not None:
            @pl.when(last_step)
            def _():
                _chips_finish(t_ref, landed_ref, send_sems, recv_sems)

    vmem = (2 * npp * (_nbytes((seq, ATT_K), BF16) + 2 * _nbytes((seq, LANES), BF16) + 2 * _nbytes((seq, LANES), F32))
            + 32 * npp * ta * ta * 4 + 8 * 1024 * 1024)
    qblk = lambda b, g, i: (b * nq + i, g)
    acc_blk = pl.BlockSpec((seq, LANES * npp), lambda b, g, i: (b, g))
    in_specs = [pl.BlockSpec((1, npp, 2, ta, ATT_K), lambda b, g, i: (b, g, 0, i, 0)),
                pl.BlockSpec((1, npp, seq, ATT_K), lambda b, g, i: (b, g, 0, 0)),
                pl.BlockSpec((1, npp, 2, seq, LANES), lambda b, g, i: (b, g, 0, 0, 0)),
                pl.BlockSpec((ta, LANES * npp), qblk), pl.BlockSpec((ta, LANES * npp), qblk),
                pl.BlockSpec((1, npp, 2, ta, 1), lambda b, g, i: (b, g, 0, i, 0))]
    out_specs = [pl.BlockSpec((ta, LANES * npp), qblk), acc_blk, acc_blk,
                 pl.BlockSpec((1, npp, 2, 1, 1, ta), lambda b, g, i: (b, g, 0, i, 0, 0)),
                 pl.BlockSpec((1, npp, 2, nq, 1, ta), lambda b, g, i: (b, g, 0, 0, 0, 0))]
    sums = jax.ShapeDtypeStruct((nb, N_PAIRS, 2, nq, 1, ta), F32)
    out_shape = [jax.ShapeDtypeStruct((nb * seq, D_ATTN), BF16), jax.ShapeDtypeStruct((nb * seq, D_ATTN), F32),
                 jax.ShapeDtypeStruct((nb * seq, D_ATTN), F32), sums, sums]
    if carry_exchange is None:
        return pl.pallas_call(
            body, grid=grid, in_specs=in_specs, out_specs=out_specs, out_shape=out_shape, name="fox_bwd",
            compiler_params=_cparams(("parallel", "parallel", "arbitrary"), vmem),
        )(qa, ka, vm, y, dy, lse)
    any_spec = pl.BlockSpec(memory_space=pl.ANY)
    return pl.pallas_call(
        body, grid=grid, in_specs=in_specs + [any_spec], out_specs=out_specs + [any_spec],
        out_shape=out_shape + [jax.ShapeDtypeStruct(carry_exchange.shape, carry_exchange.dtype)],
        scratch_shapes=list(CHIPS_SEMS), name="fox_bwd_exchange",
        compiler_params=_cparams(("arbitrary", "arbitrary", "arbitrary"), vmem),
    )(qa, ka, vm, y, dy, lse, carry_exchange)


def _shift_down(a, k):
    row = lax.broadcasted_iota(jnp.int32, a.shape, 0)
    return jnp.where(row >= k, pltpu.roll(a, k, 0), 0.0)


def _shift_up(a, k):
    n = a.shape[0]
    row = lax.broadcasted_iota(jnp.int32, a.shape, 0)
    return jnp.where(row < n - k, pltpu.roll(a, n - k, 0), 0.0)


def _by_group(vals, shape):
    lane = lax.broadcasted_iota(jnp.int32, shape, 1)
    out = vals[-1]
    for gi in range(len(vals) - 2, -1, -1):
        out = jnp.where(lane < POOL_GROUP * (gi + 1), vals[gi], out)
    return out


def _pooled(u):
    s2 = u + _shift_down(u, 1)
    s4 = s2 + _shift_down(s2, 2)
    s8 = s4 + _shift_down(s4, 4)
    s16 = s8 + _shift_down(s8, 8)
    win = _by_group([s2, s4, s8, s16], u.shape)
    row = lax.broadcasted_iota(jnp.int32, u.shape, 0)
    wsize = _by_group([jnp.full(u.shape, w, jnp.int32) for w in POOL_WINDOWS], u.shape)
    inv = 1.0 / jnp.minimum(row + 1, wsize).astype(F32)
    return win * inv - u, inv


def _pool_fwd(rest, wbd, scale, seq):
    t = rest.shape[0]

    def body(u_ref, w_ref, s_ref, o_ref):
        pooled, _ = _pooled(u_ref[...])
        pw = jnp.dot(pooled.astype(BF16), w_ref[...], preferred_element_type=F32)
        o_ref[...] = (pw * s_ref[...]).astype(o_ref.dtype)

    blk = pl.BlockSpec((seq, D_POOL), lambda b: (b, 0))
    return pl.pallas_call(
        body, grid=(t // seq,),
        in_specs=[blk, pl.BlockSpec((D_POOL, D_POOL), lambda b: (0, 0)), pl.BlockSpec((1, D_POOL), lambda b: (0, 0))],
        out_specs=blk, out_shape=jax.ShapeDtypeStruct((t, D_POOL), BF16), name="pool_fwd",
        compiler_params=_cparams(("parallel",), 24 * seq * D_POOL * 4),
    )(rest, wbd, scale)


def _pool_bwd(rest, dy, wbd, wbd_t, scale, seq):
    t = rest.shape[0]

    def body(u_ref, dy_ref, w_ref, wt_ref, s_ref, du_ref, dw_ref, dsc_ref):
        i = pl.program_id(0)
        pooled, inv = _pooled(u_ref[...])
        pb = pooled.astype(BF16)
        pw = jnp.dot(pb, w_ref[...], preferred_element_type=F32)
        dyp = dy_ref[...]
        dsp = jnp.sum(dyp * pw, axis=0, keepdims=True)
        dpw = (dyp * s_ref[...]).astype(BF16)
        dwp = _tn(pb, dpw)
        dpooled = jnp.dot(dpw, wt_ref[...], preferred_element_type=F32)
        dwin = dpooled * inv
        t2 = dwin + _shift_up(dwin, 1)
        t4 = t2 + _shift_up(t2, 2)
        t8 = t4 + _shift_up(t4, 4)
        t16 = t8 + _shift_up(t8, 8)
        du_ref[...] = (_by_group([t2, t4, t8, t16], dwin.shape) - dpooled).astype(du_ref.dtype)

        @pl.when(i == 0)
        def _():
            dw_ref[...] = dwp
            dsc_ref[...] = dsp

        @pl.when(i > 0)
        def _():
            dw_ref[...] += dwp
            dsc_ref[...] += dsp

    blk = pl.BlockSpec((seq, D_POOL), lambda b: (b, 0))
    sq = pl.BlockSpec((D_POOL, D_POOL), lambda b: (0, 0))
    row = pl.BlockSpec((1, D_POOL), lambda b: (0, 0))
    return pl.pallas_call(
        body, grid=(t // seq,),
        in_specs=[blk, pl.BlockSpec((seq, D_POOL), lambda b: (b, 2)), sq, sq, row],
        out_specs=[blk, sq, row],
        out_shape=[jax.ShapeDtypeStruct((t, D_POOL), BF16), jax.ShapeDtypeStruct((D_POOL, D_POOL), F32),
                   jax.ShapeDtypeStruct((1, D_POOL), F32)], name="pool_bwd",
        compiler_params=_cparams(("arbitrary",), 40 * seq * D_POOL * 4),
    )(rest, dy, wbd, wbd_t, scale)


def _conv_fwd(rest, cw, seq):
    t = rest.shape[0]

    def body(cb_ref, cc_ref, ch_ref, w_ref, o_ref):
        u = cc_ref[...] * ch_ref[...]
        y = w_ref[0:1, :] * _shift_down(u, 2) + w_ref[1:2, :] * _shift_down(u, 1) + w_ref[2:3, :] * u
        o_ref[...] = (cb_ref[...] * y).astype(o_ref.dtype)

    def col(c):
        return pl.BlockSpec((seq, D_CONV), lambda b, c=c: (b, c))

    return pl.pallas_call(
        body, grid=(t // seq,), in_specs=[col(1), col(2), col(3), pl.BlockSpec((8, D_CONV), lambda b: (0, 0))],
        out_specs=pl.BlockSpec((seq, D_CONV), lambda b: (b, 0)),
        out_shape=jax.ShapeDtypeStruct((t, D_CONV), BF16), name="conv_fwd",
        compiler_params=_cparams(("parallel",), 24 * seq * D_CONV * 4),
    )(rest, rest, rest, cw)


def _conv_bwd(rest, dy, cw, seq):
    t = rest.shape[0]

    def body(cb_ref, cc_ref, ch_ref, dy_ref, w_ref, o_ref, dw_ref):
        i = pl.program_id(0)
        cc = cc_ref[...]
        ch = ch_ref[...]
        u = cc * ch
        u1 = _shift_down(u, 1)
        u2 = _shift_down(u, 2)
        y = w_ref[0:1, :] * u2 + w_ref[1:2, :] * u1 + w_ref[2:3, :] * u
        dyc = dy_ref[...]
        d2 = dyc * cb_ref[...]
        du = w_ref[0:1, :] * _shift_up(d2, 2) + w_ref[1:2, :] * _shift_up(d2, 1) + w_ref[2:3, :] * d2
        o_ref[:, 0:D_CONV] = (dyc * y).astype(o_ref.dtype)
        o_ref[:, D_CONV:2 * D_CONV] = (du * ch).astype(o_ref.dtype)
        o_ref[:, 2 * D_CONV:3 * D_CONV] = (du * cc).astype(o_ref.dtype)
        tap = lax.broadcasted_iota(jnp.int32, (8, D_CONV), 0)
        dwp = jnp.where(tap == 0, jnp.sum(d2 * u2, axis=0, keepdims=True),
                        jnp.where(tap == 1, jnp.sum(d2 * u1, axis=0, keepdims=True),
                                  jnp.where(tap == 2, jnp.sum(d2 * u, axis=0, keepdims=True), 0.0)))

        @pl.when(i == 0)
        def _():
            dw_ref[...] = dwp

        @pl.when(i > 0)
        def _():
            dw_ref[...] += dwp

    def col(c):
        return pl.BlockSpec((seq, D_CONV), lambda b, c=c: (b, c))

    taps = pl.BlockSpec((8, D_CONV), lambda b: (0, 0))
    return pl.pallas_call(
        body, grid=(t // seq,), in_specs=[col(1), col(2), col(3), col(3), taps],
        out_specs=[pl.BlockSpec((seq, 3 * D_CONV), lambda b: (b, 0)), taps],
        out_shape=[jax.ShapeDtypeStruct((t, 3 * D_CONV), BF16), jax.ShapeDtypeStruct((8, D_CONV), F32)],
        name="conv_bwd", compiler_params=_cparams(("arbitrary",), 48 * seq * D_CONV * 4),
    )(rest, rest, rest, dy, cw)


def _adamw(w, g, m, v, name):
    r, c = w.shape
    tr = _pick(r, (512, 352, 256, 128)) if r > 512 else r

    def body(w_ref, g_ref, m_ref, v_ref, d_ref, mo_ref, vo_ref):
        gv = g_ref[...]
        mn = ADAM_B1 * m_ref[...] + (1.0 - ADAM_B1) * gv
        vn = ADAM_B2 * v_ref[...] + (1.0 - ADAM_B2) * (gv * gv)
        m_hat = mn / (1.0 - ADAM_B1 ** ADAM_STEP)
        v_hat = vn / (1.0 - ADAM_B2 ** ADAM_STEP)
        d_ref[...] = -ADAM_LR * (m_hat / (jnp.sqrt(v_hat) + ADAM_EPS) + ADAM_WD * w_ref[...])
        mo_ref[...] = mn
        vo_ref[...] = vn

    blk = pl.BlockSpec((tr, c), lambda i: (i, 0))
    sds = jax.ShapeDtypeStruct((r, c), F32)
    return pl.pallas_call(
        body, grid=(r // tr,), in_specs=[blk] * 4, out_specs=[blk] * 3, out_shape=[sds] * 3, name=name,
        compiler_params=_cparams(("parallel",), 20 * tr * max(c, LANES) * 4),
    )(w, g, m, v)


def _sum_slots(a, name):
    ns, r, c = a.shape
    tr = _pick(r, (384, 368, 256, 184, 136, 128, 88, 8))

    def body(a_ref, o_ref):
        acc = a_ref[0].astype(F32)
        for s in range(1, ns):
            acc = acc + a_ref[s].astype(F32)
        o_ref[...] = acc

    return pl.pallas_call(
        body, grid=(r // tr,), in_specs=[pl.BlockSpec((ns, tr, c), lambda i: (0, i, 0))],
        out_specs=pl.BlockSpec((tr, c), lambda i: (i, 0)), out_shape=jax.ShapeDtypeStruct((r, c), F32), name=name,
        compiler_params=_cparams(("parallel",), 4 * (ns + 2) * tr * c * 4),
    )(a)


def _add_core_half(core, g4, theirs, out_dtype, name):
    ns, _, r, c = g4.shape
    tr = _pick(r, (384, 368, 256, 184, 136, 128, 88, 8))

    def body(core_ref, a_ref, b_ref, o_ref):
        o_ref[...] = (a_ref[0] + b_ref[...]).astype(o_ref.dtype)

    blk = pl.BlockSpec((1, tr, c), lambda s, i, core_ref: (s, i, 0))
    return pl.pallas_call(
        body,
        grid_spec=pltpu.PrefetchScalarGridSpec(
            num_scalar_prefetch=1, grid=(ns, r // tr),
            in_specs=[pl.BlockSpec((1, 1, tr, c), lambda s, i, core_ref: (s, core_ref[0], i, 0)), blk],
            out_specs=blk),
        out_shape=jax.ShapeDtypeStruct((ns, r, c), out_dtype), name=name,
        compiler_params=_cparams(("parallel", "parallel"), 10 * tr * c * 4),
    )(core, g4, theirs)


def _sum_chips(order, own, landed, name):
    ns, r, c = own.shape
    tr = _pick(r, (384, 368, 256, 184, 136, 128, 88, 8))

    def body(order_ref, a_ref, b1_ref, b2_ref, b3_ref, o_ref):
        o_ref[...] = ((a_ref[0].astype(F32) + b1_ref[0].astype(F32)) + b2_ref[0].astype(F32)) + b3_ref[0].astype(F32)

    def slot(k):
        return pl.BlockSpec((1, tr, c), lambda i, order_ref, k=k: (order_ref[k], i, 0))

    return pl.pallas_call(
        body,
        grid_spec=pltpu.PrefetchScalarGridSpec(
            num_scalar_prefetch=1, grid=(r // tr,), in_specs=[slot(0), slot(1), slot(2), slot(3)],
            out_specs=pl.BlockSpec((tr, c), lambda i, order_ref: (i, 0))),
        out_shape=jax.ShapeDtypeStruct((r, c), F32), name=name,
        compiler_params=_cparams(("parallel",), 16 * tr * c * 4),
    )(order, own, landed, landed, landed)


def _mesh_pos():
    return lax.axis_index("x"), lax.axis_index("y"), lax.axis_index("c")


def _comm_call(name, gathers=(), chips=None):
    payloads = list(gathers) + ([] if chips is None else [chips])
    n = len(payloads)

    def body(*refs):
        ins, outs, sems = refs[:n], refs[n:2 * n], refs[2 * n:]
        jobs = [(_gather_start, _gather_finish)] * len(gathers) + ([] if chips is None else [(_chips_start, _chips_finish)])
        for k, (start, _) in enumerate(jobs):
            start(ins[k], outs[k], sems[2 * k], sems[2 * k + 1])
        for k, (_, finish) in enumerate(jobs):
            finish(ins[k], outs[k], sems[2 * k], sems[2 * k + 1])

    any_spec = pl.BlockSpec(memory_space=pl.ANY)
    out_shape = [jax.ShapeDtypeStruct((N_DEV,) + x.shape, x.dtype) for x in gathers]
    sems = list(GATHER_SEMS) * len(gathers)
    if chips is not None:
        out_shape.append(jax.ShapeDtypeStruct(chips.shape, chips.dtype))
        sems += list(CHIPS_SEMS)
    outs = pl.pallas_call(body, out_shape=out_shape, in_specs=[any_spec] * n, out_specs=[any_spec] * n,
                          scratch_shapes=sems, name=name)(*payloads)
    return [_fill_own_slot(o, x) for o, x in zip(outs, gathers)] + ([] if chips is None else [outs[-1]])


GATHER_SEMS = (pltpu.SemaphoreType.DMA((7,)), pltpu.SemaphoreType.DMA((7,)))


def _fill_own_slot(gathered, x):
    mx, my, mc = _mesh_pos()
    return lax.dynamic_update_slice_in_dim(gathered, x[None], 4 * mx + 2 * my + mc, axis=0)


def _gather_copies(x_ref, out_ref, send_sems, recv_sems):
    mx, my, mc = _mesh_pos()
    me, sibling = (mx, my, mc), (mx, my, 1 - mc)
    chips = [(1 - mx, my), (mx, 1 - my), (1 - mx, 1 - my)]

    def slot(px, py, pc):
        return out_ref.at[4 * px + 2 * py + pc]

    def copy(k, block, to, src=None):
        return pltpu.make_async_remote_copy(
            src_ref=slot(*block) if src is None else src, dst_ref=slot(*block),
            send_sem=send_sems.at[k], recv_sem=recv_sems.at[k],
            device_id=to, device_id_type=pl.DeviceIdType.MESH)

    first = [copy(0, me, sibling, src=x_ref)]
    first += [copy(1 + j, me, (*chip, mc), src=x_ref) for j, chip in enumerate(chips)]
    passed = [copy(4 + j, (*chip, mc), sibling) for j, chip in enumerate(chips)]
    over_ici = [copy(1 + j, (*chip, mc), me) for j, chip in enumerate(chips)]
    over_d2d = [copy(0, sibling, me)] + [copy(4 + j, (*chip, 1 - mc), me) for j, chip in enumerate(chips)]
    return first, passed, over_ici, over_d2d


def _gather_start(x_ref, out_ref, send_sems, recv_sems):
    for cp in _gather_copies(x_ref, out_ref, send_sems, recv_sems)[0]:
        cp.start()


def _gather_finish(x_ref, out_ref, send_sems, recv_sems):
    first, passed, over_ici, over_d2d = _gather_copies(x_ref, out_ref, send_sems, recv_sems)
    for landed, relay in zip(over_ici, passed):
        landed.wait_recv()
        relay.start()
    for landed in over_d2d:
        landed.wait_recv()
    for cp in first + passed:
        cp.wait_send()


SIBLING_SEMS = (pltpu.SemaphoreType.DMA((N_CHIPS,)), pltpu.SemaphoreType.DMA((N_CHIPS,)))


def _sibling_copies(g_ref, theirs_ref, send_sems, recv_sems):
    mx, my, mc = _mesh_pos()
    return [pltpu.make_async_remote_copy(
        src_ref=g_ref.at[chip, 1 - mc], dst_ref=theirs_ref.at[chip],
        send_sem=send_sems.at[chip], recv_sem=recv_sems.at[chip],
        device_id=(mx, my, 1 - mc), device_id_type=pl.DeviceIdType.MESH) for chip in range(N_CHIPS)]


def _sibling_start(g_ref, theirs_ref, send_sems, recv_sems):
    for cp in _sibling_copies(g_ref, theirs_ref, send_sems, recv_sems):
        cp.start()


def _sibling_finish(g_ref, theirs_ref, send_sems, recv_sems):
    copies = _sibling_copies(g_ref, theirs_ref, send_sems, recv_sems)
    for cp in copies:
        cp.wait_recv()
    for cp in copies:
        cp.wait_send()


CHIPS_SEMS = (pltpu.SemaphoreType.DMA((N_CHIPS - 1,)), pltpu.SemaphoreType.DMA((N_CHIPS - 1,)))


def _chips_copies(t_ref, out_ref, send_sems, recv_sems):
    mx, my, mc = _mesh_pos()
    my_chip = 2 * mx + my
    copies = []
    for k in range(1, N_CHIPS):
        px = 1 - mx if k & 2 else mx
        py = 1 - my if k & 1 else my
        peer_chip = 2 * px + py

        def rdma(dst_slot, px=px, py=py, peer_chip=peer_chip, k=k):
            return pltpu.make_async_remote_copy(
                src_ref=t_ref.at[peer_chip], dst_ref=out_ref.at[dst_slot],
                send_sem=send_sems.at[k - 1], recv_sem=recv_sems.at[k - 1],
                device_id=(px, py, mc), device_id_type=pl.DeviceIdType.MESH)

        copies.append((rdma(my_chip), rdma(peer_chip)))
    return copies


def _chips_start(t_ref, out_ref, send_sems, recv_sems):
    for send, _ in _chips_copies(t_ref, out_ref, send_sems, recv_sems):
        send.start()


def _chips_finish(t_ref, out_ref, send_sems, recv_sems):
    copies = _chips_copies(t_ref, out_ref, send_sems, recv_sems)
    for _, landed in copies:
        landed.wait_recv()
    for send, _ in copies:
        send.wait_send()


def _add_sibling(g4, theirs):
    core = jnp.reshape(lax.axis_index("c"), (1,)).astype(jnp.int32)
    return _add_core_half(core, g4, theirs, BF16, name="add_sibling_grads")


def _sum_landed(chip_sums, landed):
    mx, my, _ = _mesh_pos()
    order = jnp.stack([2 * mx + my, 2 * (1 - mx) + my, 2 * mx + (1 - my), 2 * (1 - mx) + (1 - my)]).astype(jnp.int32)
    return _sum_chips(order, chip_sums, landed, name="sum_grads")


def _perm_mix_rows(wt):
    f0 = D_QKV
    f1 = f0 + N_HEADS
    return jnp.concatenate([wt[:f0], wt[f1:], jnp.pad(wt[f0:f1], ((0, LANES - N_HEADS), (0, 0)))], axis=0)


def _unperm_mix_rows(gt):
    f0 = D_QKV
    return jnp.concatenate([gt[:f0], gt[f0 + D_REST:f0 + D_REST + N_HEADS], gt[f0:f0 + D_REST]], axis=0)


def _pack_shards(parts, l, dtype):
    w1i, w1o, wmi, wmo, w2i, w2o = parts
    rows = [w1i[l].T, w1o[l], jnp.pad(wmi[l].T, ((0, MIX_ROWS_PAD - MIX_ROWS), (0, 0))), wmo[l], w2i[l].T, w2o[l]]
    return jnp.concatenate(rows, axis=0).astype(dtype)


PACK_HEAD = FFN_ROWS + OUT_ROWS


def _ffn_weights(wg, o):
    return dict(wi_t=wg[:, o:o + FFN_ROWS].reshape(2 * D_FF, D_MODEL),
                wo=wg[:, o + FFN_ROWS:o + FFN_ROWS + OUT_ROWS].reshape(D_FF, D_MODEL))


def _tail_weights(wg):
    mix = dict(wm_t=_perm_mix_rows(wg[:, :MIX_ROWS].reshape(D_IN, D_MODEL)),
               wo=wg[:, MIX_ROWS_PAD:MIX_ROWS_PAD + MO_ROWS].reshape(D_MODEL, D_MODEL))
    return mix, _ffn_weights(wg, MIX_ROWS_PAD + MO_ROWS)


GRAD_AT = dict(f1i=0, f2i=FFN_ROWS, f1o=4 * OUT_ROWS, f2o=5 * OUT_ROWS, mi=6 * OUT_ROWS, mo=20 * MO_ROWS)
GRAD_ROWS = GRAD_AT["mo"] + MO_ROWS
GRAD_SHAPE = (N_CHIPS, 2, GRAD_ROWS, D_MODEL)


def _into_ffn_in(buf, tag, half):
    rb = GRAD_AT[tag] // FFN_ROWS
    return (buf, GRAD_SHAPE, (1, 2, FFN_ROWS, D_MODEL), lambda i: (2 * half + i, 0, rb, 0))


def _into_ffn_out(buf, tag):
    rb = GRAD_AT[tag] // OUT_ROWS
    return (buf, GRAD_SHAPE, (2, 2, OUT_ROWS, D_MODEL), lambda i: (i, 0, rb, 0))


def _into_mix_out(buf):
    rb = GRAD_AT["mo"] // MO_ROWS
    return (buf, GRAD_SHAPE, (N_CHIPS, 2, MO_ROWS, D_MODEL), lambda i: (0, 0, rb, 0))


def _put_mix_in(buf, g_in_t):
    gmi = _unperm_mix_rows(g_in_t).reshape(N_DEV, MIX_ROWS, D_MODEL)
    gmi = jnp.pad(gmi, ((0, 0), (0, OUT_ROWS - MIX_ROWS), (0, 0))).reshape(N_CHIPS, 2, OUT_ROWS, D_MODEL)
    return lax.dynamic_update_slice(buf, gmi, (0, 0, GRAD_AT["mi"], 0))


def _out_proj(a, w, x, alpha, next_gain, name):
    if next_gain is None:
        return _mm_nn(a, w, out_dtype=F32, res=x, alpha=alpha, name=name), None
    return _mm_nn(a, w, out_dtype=F32, res=x, alpha=alpha, next_gain=next_gain, name=name + "_norm")


def _ffn_forward(x, xn, w, next_gain, carry_gather=None):
    res = _ffn_in(xn, w["wi_t"], name="ffn_in", carry_gather=carry_gather)
    h, pg, pu = res[:3]
    x_new, xn_next = _out_proj(h, w["wo"], x, 0.5, next_gain, "ffn_out")
    out = (x_new, xn_next, dict(x=x, xn=xn, h=h, pg=pg, pu=pu))
    return out if carry_gather is None else out + (_fill_own_slot(res[3], carry_gather),)


def _ffn_backward(dxo, dxo_b, gain, w, saved, gbuf, tag, exchange=False):
    dzg, dzu = _ffn_bwd_mid(dxo_b, w["wo"], saved["pg"], saved["pu"], name="ffn_bwd_mid")
    gbuf = _mm_tn(saved["h"], dxo_b, alpha=0.5, tm=F_HALF, name="ffn_gw_out", into=_into_ffn_out(gbuf, tag + "o"))
    gbuf = _mm_tn(dzg, saved["xn"], tm=F_HALF, name="ffn_gw_in", into=_into_ffn_in(gbuf, tag + "i", 0))
    gbuf = _mm_tn(dzu, saved["xn"], tm=F_HALF, name="ffn_gw_in", into=_into_ffn_in(gbuf, tag + "i", 1))
    res = _dxn_norm_bwd([dzg, dzu], w["wi_t"], saved["x"], gain, dxo, name="ffn_dxn_norm_bwd",
                        carry_sibling=gbuf if exchange else None)
    return res[0], res[1], res[2], gbuf, (res[3] if exchange else None)


def _mixer_forward(x, xn, p, w, nb, seq, ta, next_gain, next_pack=None):
    qkv, rest, fl = _mix_proj(xn, w["wm_t"])
    qa, ka, vm = _fox_prep(fl, p["bf"], qkv, nb, seq)
    if next_pack is None:
        (y_attn, lse), next_gathered = _fox_fwd(qa, ka, vm, nb, seq, ta), None
    else:
        y_attn, lse, next_gathered = _fox_fwd(qa, ka, vm, nb, seq, ta, carry_gather=next_pack)
        next_gathered = _fill_own_slot(next_gathered, next_pack)
    y_pool = _pool_fwd(rest, p["wbd"], p["scale"], seq)
    y_conv = _conv_fwd(rest, p["cw"], seq)
    y = jnp.concatenate([y_attn, y_pool, y_conv], axis=1)
    x_new, xn_next = _out_proj(y, w["wo"], x, 1.0, next_gain, "mix_out")
    return x_new, xn_next, dict(x=x, xn=xn, qa=qa, ka=ka, vm=vm, rest=rest, fl=fl, lse=lse, y=y), next_gathered


def _mixer_backward(dxo, dxo_b, p, w, sv, nb, seq, ta, gbuf, pending=None):
    t = dxo.shape[0]
    dy = _mm_nt(dxo_b, w["wo"], out_dtype=F32, name="mix_dy")
    gbuf = _mm_tn(sv["y"], dxo_b, name="mix_gw_out", into=_into_mix_out(gbuf))
    res = _fox_bwd(sv["qa"], sv["ka"], sv["vm"], sv["y"], dy, sv["lse"], nb, seq, ta, carry_exchange=pending)
    dq, dk, dv, d_rows, d_cols = res[:5]
    landed = None if pending is None else res[5]
    ddh = (d_rows.reshape(nb, N_HEADS, seq) - d_cols.reshape(nb, N_HEADS, seq)).transpose(0, 2, 1)
    ddh = ddh.reshape(t, N_HEADS)
    dfl, dbf = _fox_prep_bwd(jnp.pad(ddh, ((0, 0), (0, LANES - N_HEADS))), sv["fl"], p["bf"], seq)
    dpool, dwbd, dscale = _pool_bwd(sv["rest"], dy, p["wbd"], p["wbd_t"], p["scale"], seq)
    dconv, dcw = _conv_bwd(sv["rest"], dy, p["cw"], seq)
    dproj = jnp.concatenate([dq, dk.astype(BF16), dv.astype(BF16), dpool, dconv, dfl], axis=1)
    gbuf = _put_mix_in(gbuf, _mm_tn(dproj, sv["xn"], tm=D_INP // 3, name="mix_gw_in"))
    dx, dx_b, dg = _dxn_norm_bwd([dproj], w["wm_t"], sv["x"], p["norm"], dxo, name="mix_dxn_norm_bwd")
    return dx, dx_b, dict(norm=dg, bf=dbf, wbd=dwbd, scale=dscale, cw=dcw), gbuf, landed


def _block_diag(wp):
    z = jnp.zeros((POOL_GROUP, POOL_GROUP), wp.dtype)
    return jnp.concatenate(
        [jnp.concatenate([wp[g] if g == r else z for g in range(4)], axis=1) for r in range(4)], axis=0)


def _row_pad(a, rows):
    a = a.reshape(-1, a.shape[-1])
    return jnp.pad(a, ((0, rows - a.shape[0]), (0, 0)))


def kernel(x, norm_ffn1, w_ffn1_in, w_ffn1_out, norm_mix, w_mix_in, b_forget, w_pool, pool_scale, conv_w, w_mix_out, norm_ffn2, w_ffn2_in, w_ffn2_out, norm_final, loss_target, m_norm_ffn1, m_w_ffn1_in, m_w_ffn1_out, m_norm_mix, m_w_mix_in, m_b_forget, m_w_pool, m_pool_scale, m_conv_w, m_w_mix_out, m_norm_ffn2, m_w_ffn2_in, m_w_ffn2_out, m_norm_final, v_norm_ffn1, v_w_ffn1_in, v_w_ffn1_out, v_norm_mix, v_w_mix_in, v_b_forget, v_w_pool, v_pool_scale, v_conv_w, v_w_mix_out, v_norm_ffn2, v_w_ffn2_in, v_w_ffn2_out, v_norm_final):
    nb, seq, d = x.shape
    depth = norm_ffn1.shape[0]
    t = nb * seq
    ta = _pick(seq, (ATT_TILE, 128))
    my_id = 4 * lax.axis_index("x") + 2 * lax.axis_index("y") + lax.axis_index("c")
    cshard = conv_w.shape[-1]

    shards = (w_ffn1_in, w_ffn1_out, w_mix_in, w_mix_out, w_ffn2_in, w_ffn2_out)
    pack0 = _pack_shards(shards, 0, BF16)
    wg_head, cw_g = _comm_call("gather_weights_and_taps", gathers=[
        pack0[:PACK_HEAD], _row_pad(conv_w.reshape(depth * 3, cshard), 16).reshape(4, LANES)])
    cw_all = cw_g.reshape(N_DEV, 16, cshard)[:, :depth * 3].reshape(N_DEV, depth, 3, cshard)
    cw_all = cw_all.transpose(1, 2, 0, 3).reshape(depth, 3, D_CONV)

    xs = x.reshape(t, d)
    xn = _rmsnorm_fwd(xs, norm_ffn1[0][None], name="first_norm")
    saved = []
    for l in range(depth):
        wbd = _block_diag(w_pool[l])
        p = dict(norm=norm_mix[l][None], bf=jnp.pad(b_forget[l], (0, LANES - N_HEADS))[None],
                 wbd=wbd.astype(BF16), wbd_t=wbd.T.astype(BF16), scale=pool_scale[l][None],
                 cw=_row_pad(cw_all[l], 8))
        w = dict(f1=_ffn_weights(wg_head, 0))
        if l == 0:
            xs, xn, s1, wg_tail = _ffn_forward(xs, xn, w["f1"], norm_mix[l][None], carry_gather=pack0[PACK_HEAD:])
        else:
            xs, xn, s1 = _ffn_forward(xs, xn, w["f1"], norm_mix[l][None])
        w["mix"], w["f2"] = _tail_weights(wg_tail)
        next_pack = _pack_shards(shards, l + 1, BF16) if l + 1 < depth else None
        xs, xn, sm, wg = _mixer_forward(xs, xn, p, w["mix"], nb, seq, ta, norm_ffn2[l][None], next_pack)
        if wg is not None:
            wg_head, wg_tail = wg[:, :PACK_HEAD], wg[:, PACK_HEAD:]
        xs, xn, s2 = _ffn_forward(xs, xn, w["f2"], norm_ffn1[l + 1][None] if l + 1 < depth else None)
        saved.append((w, p, s1, sm, s2))

    dx, dx_b, g_norm_final, loss_part = _final_loss_bwd(xs, norm_final[None], loss_target.reshape(t, d))
    layer_g = [None] * depth
    small = [None] * depth
    chip_sums = None
    for l in reversed(range(depth)):
        w, p, s1, sm, s2 = saved[l]
        dx, dx_b, dg2, gbuf, _ = _ffn_backward(dx, dx_b, norm_ffn2[l][None], w["f2"], s2, None, "f2")
        dx, dx_b, gm, gbuf, landed = _mixer_backward(dx, dx_b, p, w["mix"], sm, nb, seq, ta, gbuf, chip_sums)
        if chip_sums is not None:
            layer_g[l + 1] = _sum_landed(chip_sums, landed)
        dx, dx_b, dg1, gbuf, theirs = _ffn_backward(dx, dx_b, norm_ffn1[l][None], w["f1"], s1, gbuf, "f1", exchange=True)
        chip_sums = _add_sibling(gbuf, theirs)
        small[l] = dict(n1=dg1, nm=gm["norm"], n2=dg2, bf=gm["bf"], wbd=gm["wbd"], scale=gm["scale"], cw=gm["cw"])
    grad_x = dx.reshape(nb, seq, d)

    def tile8(a):
        return jnp.pad(a, ((0, 8 - a.shape[0]), (0, D_MODEL - a.shape[1])))

    rows = []
    for l in range(depth):
        s = small[l]
        wp_rows = jnp.stack([s["wbd"][POOL_GROUP * g:POOL_GROUP * (g + 1), POOL_GROUP * g:POOL_GROUP * (g + 1)]
                             for g in range(4)]).reshape(16, D_MODEL)
        rows += [tile8(s["n1"]), tile8(s["nm"]), tile8(s["n2"]), tile8(s["bf"]), tile8(s["scale"]), tile8(s["cw"]),
                 wp_rows]
    rows += [tile8(g_norm_final), tile8(loss_part)]
    small_gathered, landed = _comm_call("exchange_grads_chips_gather_small", gathers=[jnp.concatenate(rows, axis=0)],
                                        chips=chip_sums)
    layer_g[0] = _sum_landed(chip_sums, landed)

    pieces = {}
    for nm, n in (("f1i", FFN_ROWS), ("f1o", OUT_ROWS), ("mi", MIX_ROWS), ("mo", MO_ROWS), ("f2i", FFN_ROWS),
                  ("f2o", OUT_ROWS)):
        pieces[nm] = jnp.stack([g[GRAD_AT[nm]:GRAD_AT[nm] + n] for g in layer_g])
    g_sharded = dict(
        w_ffn1_in=pieces["f1i"].transpose(0, 2, 1), w_ffn1_out=pieces["f1o"],
        w_mix_in=pieces["mi"].transpose(0, 2, 1), w_mix_out=pieces["mo"],
        w_ffn2_in=pieces["f2i"].transpose(0, 2, 1), w_ffn2_out=pieces["f2o"])

    per_layer = 6 * 8 + 16
    small_sum = _sum_slots(small_gathered, name="sum_small_grads")
    lay = small_sum[:depth * per_layer].reshape(depth, per_layer, D_MODEL)
    g_small = dict(
        norm_ffn1=lay[:, 0], norm_mix=lay[:, 8], norm_ffn2=lay[:, 16], b_forget=lay[:, 24, :N_HEADS],
        pool_scale=lay[:, 32, :D_POOL],
        conv_w=lax.dynamic_slice_in_dim(lay[:, 40:43, :D_CONV], my_id * cshard, cshard, axis=2),
        w_pool=lay[:, 48:64].reshape(depth, 4, POOL_GROUP, POOL_GROUP),
        norm_final=small_sum[depth * per_layer])
    loss = small_sum[depth * per_layer + 8, 0]

    given = dict(norm_ffn1=(norm_ffn1, m_norm_ffn1, v_norm_ffn1), w_ffn1_in=(w_ffn1_in, m_w_ffn1_in, v_w_ffn1_in),
                 w_ffn1_out=(w_ffn1_out, m_w_ffn1_out, v_w_ffn1_out), norm_mix=(norm_mix, m_norm_mix, v_norm_mix),
                 w_mix_in=(w_mix_in, m_w_mix_in, v_w_mix_in), b_forget=(b_forget, m_b_forget, v_b_forget),
                 w_pool=(w_pool, m_w_pool, v_w_pool), pool_scale=(pool_scale, m_pool_scale, v_pool_scale),
                 conv_w=(conv_w, m_conv_w, v_conv_w), w_mix_out=(w_mix_out, m_w_mix_out, v_w_mix_out),
                 norm_ffn2=(norm_ffn2, m_norm_ffn2, v_norm_ffn2), w_ffn2_in=(w_ffn2_in, m_w_ffn2_in, v_w_ffn2_in),
                 w_ffn2_out=(w_ffn2_out, m_w_ffn2_out, v_w_ffn2_out), norm_final=(norm_final, m_norm_final, v_norm_final))
    names = list(given)
    grads, deltas, new_m, new_v = {}, {}, {}, {}
    for nm in names:
        wv, mv, vv = given[nm]
        gv = (g_sharded[nm] if nm in g_sharded else g_small[nm]).reshape(wv.shape)
        shape2 = (-1, wv.shape[-1]) if wv.ndim > 1 else (1, wv.shape[0])
        dl, mn, vn = _adamw(wv.reshape(shape2), gv.reshape(shape2), mv.reshape(shape2), vv.reshape(shape2),
                            name="adamw_" + nm)
        grads[nm], deltas[nm], new_m[nm], new_v[nm] = gv, dl.reshape(wv.shape), mn.reshape(wv.shape), vn.reshape(wv.shape)
    return (loss, grad_x, *[grads[n] for n in names], *[deltas[n] for n in names],
            *[new_m[n] for n in names], *[new_v[n] for n in names])
```

```python
import functools

import jax
import jax.numpy as jnp
from jax import lax
from jax.experimental import pallas as pl
from jax.experimental.pallas import tpu as pltpu

F32 = jnp.float32
BF16 = jnp.bfloat16

D_MODEL = 1024
D_FF = 2816
HEAD_DIM = 64
N_HEADS = 8
N_PAIRS = N_HEADS // 2
D_ATTN = 512
D_POOL = 256
D_CONV = 256
POOL_WINDOWS = (2, 4, 8, 16)
POOL_GROUP = 64
D_IN = 2568
RMS_EPS = 1e-6
ADAM_LR, ADAM_B1, ADAM_B2, ADAM_EPS, ADAM_WD, ADAM_STEP = 0.001, 0.9, 0.999, 1e-08, 0.01, 10

N_DEV = 8
N_CHIPS = 4
LANES = 128
VMEM_BYTES_V7X = 64 * 1024 * 1024
VMEM_LIMIT_MAX = VMEM_BYTES_V7X - 8 * 1024 * 1024

F_HALF = D_FF // 2
D_QKV = 3 * D_ATTN
D_REST = D_POOL + 3 * D_CONV
D_INP = D_QKV + D_REST + LANES
MIX_ROWS = 321
MIX_ROWS_PAD = 336
FFN_ROWS = 704
OUT_ROWS = 352
MO_ROWS = 128
LAYER_ROWS = 2 * (FFN_ROWS + OUT_ROWS) + MIX_ROWS_PAD + MO_ROWS
NEG_BIG = -1e30
ATT_SCALE = HEAD_DIM ** -0.5
ATT_K = 2 * LANES
ATT_TILE = 256
ATT_PAIRS_FWD = 4
ATT_PAIRS_BWD = 4
MXU_COLS = 256


def _cparams(sem, vmem_bytes):
    limit = int(min(max(vmem_bytes, 16 * 1024 * 1024), VMEM_LIMIT_MAX))
    return pltpu.CompilerParams(dimension_semantics=sem, vmem_limit_bytes=limit)


def _nbytes(shape, dtype):
    n = 1
    for s in shape:
        n *= s
    return n * jnp.dtype(dtype).itemsize


def _pick(n, prefs):
    for p in prefs:
        if n % p == 0:
            return p
    return n


def _rmsnorm_fwd(x, g, name):
    t, d = x.shape
    tm = _pick(t, (512, 256, 128))

    def body(x_ref, g_ref, o_ref):
        xv = x_ref[...]
        r = lax.rsqrt(jnp.mean(xv * xv, axis=-1, keepdims=True) + RMS_EPS)
        o_ref[...] = ((xv * r) * g_ref[...]).astype(o_ref.dtype)

    return pl.pallas_call(
        body, grid=(t // tm,),
        in_specs=[pl.BlockSpec((tm, d), lambda i: (i, 0)), pl.BlockSpec((1, d), lambda i: (0, 0))],
        out_specs=pl.BlockSpec((tm, d), lambda i: (i, 0)),
        out_shape=jax.ShapeDtypeStruct((t, d), BF16), name=name,
        compiler_params=_cparams(("parallel",), 6 * tm * d * 4),
    )(x, g)


def _mm_nn(a, b, *, out_dtype, name, res=None, alpha=1.0, tn=None, next_gain=None):
    m, k = a.shape
    n = b.shape[1]
    tn = n if tn is None else tn
    tm = _pick(m, (512, 256, 128))
    with_res = res is not None
    with_norm = next_gain is not None
    assert not with_norm or tn == n

    def body(*refs):
        refs = list(refs)
        a_ref, b_ref = refs[:2]
        r_ref = refs[2] if with_res else None
        g_ref = refs[2 + with_res] if with_norm else None
        o_ref = refs[2 + with_res + with_norm]
        acc = jnp.dot(a_ref[...], b_ref[...], preferred_element_type=F32)
        if with_res:
            acc = r_ref[...] + alpha * acc
        o_ref[...] = acc.astype(o_ref.dtype)
        if with_norm:
            r = lax.rsqrt(jnp.mean(acc * acc, axis=-1, keepdims=True) + RMS_EPS)
            refs[-1][...] = ((acc * r) * g_ref[...]).astype(BF16)

    in_specs = [pl.BlockSpec((tm, k), lambda j, i: (i, 0)), pl.BlockSpec((k, tn), lambda j, i: (0, j))]
    args = [a, b]
    out_blk = pl.BlockSpec((tm, tn), lambda j, i: (i, j))
    out_specs, out_shape = [out_blk], [jax.ShapeDtypeStruct((m, n), out_dtype)]
    if with_res:
        in_specs.append(out_blk)
        args.append(res)
    if with_norm:
        in_specs.append(pl.BlockSpec((1, n), lambda j, i: (0, 0)))
        args.append(next_gain)
        out_specs.append(out_blk)
        out_shape.append(jax.ShapeDtypeStruct((m, n), BF16))
    vmem = 2 * (_nbytes((tm, k), BF16) + _nbytes((k, tn), BF16) + 4 * _nbytes((tm, tn), F32))
    outs = pl.pallas_call(
        body, grid=(n // tn, m // tm), in_specs=in_specs, out_specs=out_specs, out_shape=out_shape, name=name,
        compiler_params=_cparams(("parallel", "parallel"), vmem),
    )(*args)
    return outs if with_norm else outs[0]


def _mm_nt(a, b_t, *, out_dtype, name):
    m, k = a.shape
    n = b_t.shape[0]
    tm = _pick(m, (512, 256, 128))

    def body(a_ref, b_ref, o_ref):
        o_ref[...] = _nt(a_ref[...], b_ref[...]).astype(o_ref.dtype)

    vmem = 2 * (_nbytes((tm, k), BF16) + _nbytes((n, k), BF16) + 3 * _nbytes((tm, n), F32))
    return pl.pallas_call(
        body, grid=(m // tm,),
        in_specs=[pl.BlockSpec((tm, k), lambda i: (i, 0)), pl.BlockSpec((n, k), lambda i: (0, 0))],
        out_specs=pl.BlockSpec((tm, n), lambda i: (i, 0)), out_shape=jax.ShapeDtypeStruct((m, n), out_dtype),
        name=name, compiler_params=_cparams(("parallel",), vmem),
    )(a, b_t)


def _mix_proj(xn, wm_t):
    t, d = xn.shape
    tm = _pick(t, (512, 256, 128))

    def body(x_ref, w_ref, qkv_ref, rest_ref, f_ref):
        xv = x_ref[...]
        for c0, cw in _col_chunks(D_QKV, MXU_COLS):
            qkv_ref[:, c0:c0 + cw] = _nt(xv, w_ref[c0:c0 + cw, :]).astype(qkv_ref.dtype)
        for c0, cw in _col_chunks(D_REST, MXU_COLS):
            rest_ref[:, c0:c0 + cw] = _nt(xv, w_ref[D_QKV + c0:D_QKV + c0 + cw, :])
        f_ref[...] = _nt(xv, w_ref[D_QKV + D_REST:, :])

    def rows(n):
        return pl.BlockSpec((tm, n), lambda i: (i, 0))

    vmem = 2 * (_nbytes((tm, d), BF16) + _nbytes((D_INP, d), BF16) + 3 * _nbytes((tm, D_INP), F32))
    return pl.pallas_call(
        body, grid=(t // tm,), in_specs=[rows(d), pl.BlockSpec((D_INP, d), lambda i: (0, 0))],
        out_specs=[rows(D_QKV), rows(D_REST), rows(LANES)],
        out_shape=[jax.ShapeDtypeStruct((t, D_QKV), BF16), jax.ShapeDtypeStruct((t, D_REST), F32),
                   jax.ShapeDtypeStruct((t, LANES), F32)],
        name="mix_proj", compiler_params=_cparams(("parallel",), vmem),
    )(xn, wm_t)


def _mm_tn(a, b, *, name, alpha=1.0, tm=None, into=None):
    t, m = a.shape
    n = b.shape[1]
    tm = m if tm is None else tm
    tk = _pick(t, (2048, 1024, 512, 256, 128))
    nk = t // tk

    def body(a_ref, b_ref, *rest):
        o_ref = rest[-1]
        kk = pl.program_id(1)
        p = lax.dot_general(a_ref[...], b_ref[...], (((0,), (0,)), ((), ())), preferred_element_type=F32)
        if alpha != 1.0:
            p = alpha * p
        p = p.reshape(o_ref.shape)

        @pl.when(kk == 0)
        def _():
            o_ref[...] = p

        @pl.when(kk > 0)
        def _():
            o_ref[...] += p

    vmem = 2 * (_nbytes((tk, tm), BF16) + _nbytes((tk, n), BF16) + 2 * _nbytes((tm, n), F32))
    in_specs = [pl.BlockSpec((tk, tm), lambda i, kk: (kk, i)), pl.BlockSpec((tk, n), lambda i, kk: (kk, 0))]
    cp = _cparams(("parallel", "arbitrary"), vmem)
    if into is None:
        return pl.pallas_call(
            body, grid=(m // tm, nk), in_specs=in_specs, out_specs=pl.BlockSpec((tm, n), lambda i, kk: (i, 0)),
            out_shape=jax.ShapeDtypeStruct((m, n), F32), name=name, compiler_params=cp,
        )(a, b)
    buf, buf_shape, blk, index = into
    out_spec = pl.BlockSpec(blk, lambda i, kk: index(i))
    out_shape = jax.ShapeDtypeStruct(buf_shape, F32)
    if buf is None:
        return pl.pallas_call(body, grid=(m // tm, nk), in_specs=in_specs, out_specs=out_spec, out_shape=out_shape,
                              name=name + "_new", compiler_params=cp)(a, b)
    return pl.pallas_call(
        body, grid=(m // tm, nk), in_specs=in_specs + [pl.BlockSpec(memory_space=pl.ANY)], out_specs=out_spec,
        out_shape=out_shape, input_output_aliases={2: 0}, name=name + "_into", compiler_params=cp,
    )(a, b, buf)


def _sigmoid(v):
    return 1.0 / (1.0 + jnp.exp(-v))


def _col_chunks(n, width):
    return [(c, min(width, n - c)) for c in range(0, n, width)]


def _ffn_in(xn, w_t, name, carry_gather=None):
    t, d = xn.shape
    tm = _pick(t, (1024, 512, 256, 128))
    grid = (2, t // tm)

    def body(x_ref, wg_ref, wu_ref, *rest):
        if carry_gather is None:
            h_ref, pg_ref, pu_ref = rest
        else:
            blk_ref, h_ref, pg_ref, pu_ref, gathered_ref, send_sems, recv_sems = rest
            first_step, last_step = _grid_ends([pl.program_id(a) for a in range(2)], grid)

            @pl.when(first_step)
            def _():
                _gather_start(blk_ref, gathered_ref, send_sems, recv_sems)

        xv = x_ref[...]
        for c0, cw in _col_chunks(F_HALF, MXU_COLS):
            cols = slice(c0, c0 + cw)
            g = _nt(xv, wg_ref[cols, :])
            u = _nt(xv, wu_ref[cols, :])
            s = _sigmoid(g)
            silu = g * s
            h_ref[:, cols] = (silu * u).astype(h_ref.dtype)
            pg_ref[:, cols] = (u * (s * (1.0 + g * (1.0 - s)))).astype(pg_ref.dtype)
            pu_ref[:, cols] = silu.astype(pu_ref.dtype)

        if carry_gather is not None:
            @pl.when(last_step)
            def _():
                _gather_finish(blk_ref, gathered_ref, send_sems, recv_sems)

    vmem = 2 * (_nbytes((tm, d), BF16) + 2 * _nbytes((d, F_HALF), BF16) + 4 * _nbytes((tm, D_FF), F32))
    out_blk = pl.BlockSpec((tm, F_HALF), lambda j, i: (i, j))
    sds = jax.ShapeDtypeStruct((t, D_FF), BF16)
    in_specs = [pl.BlockSpec((tm, d), lambda j, i: (i, 0)), pl.BlockSpec((F_HALF, d), lambda j, i: (j, 0)),
                pl.BlockSpec((F_HALF, d), lambda j, i: (2 + j, 0))]
    if carry_gather is None:
        return pl.pallas_call(
            body, grid=grid, in_specs=in_specs, out_specs=[out_blk, out_blk, out_blk], out_shape=[sds, sds, sds],
            name=name, compiler_params=_cparams(("parallel", "parallel"), vmem),
        )(xn, w_t, w_t)
    any_spec = pl.BlockSpec(memory_space=pl.ANY)
    return pl.pallas_call(
        body, grid=grid, in_specs=in_specs + [any_spec], out_specs=[out_blk, out_blk, out_blk, any_spec],
        out_shape=[sds, sds, sds, jax.ShapeDtypeStruct((N_DEV,) + carry_gather.shape, carry_gather.dtype)],
        scratch_shapes=list(GATHER_SEMS), name=name + "_gather",
        compiler_params=_cparams(("arbitrary", "arbitrary"), vmem),
    )(xn, w_t, w_t, carry_gather)


def _ffn_bwd_mid(dxo, w_out, pg, pu, name):
    t, d = dxo.shape
    tm = _pick(t, (1024, 512, 256, 128))

    def body(d_ref, w_ref, pg_ref, pu_ref, dg_ref, du_ref):
        dv = d_ref[...]
        for c0, cw in _col_chunks(F_HALF, MXU_COLS):
            cols = slice(c0, c0 + cw)
            dh = 0.5 * _nt(dv, w_ref[cols, :])
            dg_ref[:, cols] = (dh * pg_ref[:, cols].astype(F32)).astype(dg_ref.dtype)
            du_ref[:, cols] = (dh * pu_ref[:, cols].astype(F32)).astype(du_ref.dtype)

    vmem = 2 * (_nbytes((tm, d), BF16) + _nbytes((d, F_HALF), BF16) + 5 * _nbytes((tm, D_FF), F32))
    blk = pl.BlockSpec((tm, F_HALF), lambda j, i: (i, j))
    sds = jax.ShapeDtypeStruct((t, D_FF), BF16)
    return pl.pallas_call(
        body, grid=(2, t // tm),
        in_specs=[pl.BlockSpec((tm, d), lambda j, i: (i, 0)), pl.BlockSpec((F_HALF, d), lambda j, i: (j, 0)), blk, blk],
        out_specs=[blk, blk], out_shape=[sds, sds], name=name,
        compiler_params=_cparams(("parallel", "parallel"), vmem),
    )(dxo, w_out, pg, pu)


def _dxn_norm_bwd(parts, b, x, g, dxo, name, carry_sibling=None):
    t, d = x.shape
    k = parts[0].shape[1]
    n_parts = len(parts)
    tm = _pick(t, (256, 128))
    grid = (t // tm,)

    def body(*refs):
        a_refs, b_refs = refs[:n_parts], refs[n_parts:2 * n_parts]
        if carry_sibling is None:
            x_ref, g_ref, do_ref, dx_ref, dxb_ref, dg_ref = refs[2 * n_parts:]
        else:
            x_ref, g_ref, do_ref, g4_ref, dx_ref, dxb_ref, dg_ref, theirs_ref, send_sems, recv_sems = refs[2 * n_parts:]
            first_step, last_step = _grid_ends([pl.program_id(0)], grid)

            @pl.when(first_step)
            def _():
                _sibling_start(g4_ref, theirs_ref, send_sems, recv_sems)

        i = pl.program_id(0)
        dn = jnp.dot(a_refs[0][...], b_refs[0][...], preferred_element_type=F32)
        for a_ref, b_ref in zip(a_refs[1:], b_refs[1:]):
            dn = dn + jnp.dot(a_ref[...], b_ref[...], preferred_element_type=F32)
        xv = x_ref[...]
        r = lax.rsqrt(jnp.mean(xv * xv, axis=-1, keepdims=True) + RMS_EPS)
        xh = xv * r
        dgp = jnp.sum(dn * xh, axis=0, keepdims=True)
        dh = dn * g_ref[...]
        dx = do_ref[...] + r * (dh - xh * jnp.mean(dh * xh, axis=-1, keepdims=True))
        dx_ref[...] = dx
        dxb_ref[...] = dx.astype(dxb_ref.dtype)

        @pl.when(i == 0)
        def _():
            dg_ref[...] = dgp

        @pl.when(i > 0)
        def _():
            dg_ref[...] += dgp

        if carry_sibling is not None:
            @pl.when(last_step)
            def _():
                _sibling_finish(g4_ref, theirs_ref, send_sems, recv_sems)

    blk = pl.BlockSpec((tm, d), lambda i: (i, 0))
    row = pl.BlockSpec((1, d), lambda i: (0, 0))
    a_specs = [pl.BlockSpec((tm, k), lambda i: (i, 0)) for _ in parts]
    b_specs = [pl.BlockSpec((k, d), lambda i, kk=kk: (kk, 0)) for kk in range(n_parts)]
    vmem = 2 * n_parts * (_nbytes((tm, k), BF16) + _nbytes((k, d), BF16)) + 16 * tm * d * 4
    in_specs = a_specs + b_specs + [blk, row, blk]
    out_shape = [jax.ShapeDtypeStruct((t, d), F32), jax.ShapeDtypeStruct((t, d), BF16), jax.ShapeDtypeStruct((1, d), F32)]
    args = (*parts, *([b] * n_parts), x, g, dxo)
    if carry_sibling is None:
        return pl.pallas_call(body, grid=grid, in_specs=in_specs, out_specs=[blk, blk, row], out_shape=out_shape,
                              name=name, compiler_params=_cparams(("arbitrary",), vmem))(*args)
    nchip, _, r, c = carry_sibling.shape
    any_spec = pl.BlockSpec(memory_space=pl.ANY)
    return pl.pallas_call(
        body, grid=grid, in_specs=in_specs + [any_spec], out_specs=[blk, blk, row, any_spec],
        out_shape=out_shape + [jax.ShapeDtypeStruct((nchip, r, c), carry_sibling.dtype)],
        scratch_shapes=list(SIBLING_SEMS), name=name + "_exchange", compiler_params=_cparams(("arbitrary",), vmem),
    )(*args, carry_sibling)


def _final_loss_bwd(x, g, tgt):
    t, d = x.shape
    tm = _pick(t, (512, 256, 128))

    def body(x_ref, g_ref, t_ref, dx_ref, dxb_ref, dg_ref, loss_ref):
        i = pl.program_id(0)
        xv = x_ref[...]
        r = lax.rsqrt(jnp.mean(xv * xv, axis=-1, keepdims=True) + RMS_EPS)
        xh = xv * r
        gv = g_ref[...]
        err = xh * gv - t_ref[...]
        lp = 0.5 * jnp.sum(jnp.mean(err * err, axis=-1, keepdims=True), axis=0, keepdims=True)
        dy = err * (1.0 / d)
        dgp = jnp.sum(dy * xh, axis=0, keepdims=True)
        dh = dy * gv
        dx = r * (dh - xh * jnp.mean(dh * xh, axis=-1, keepdims=True))
        dx_ref[...] = dx
        dxb_ref[...] = dx.astype(dxb_ref.dtype)
        lpb = jnp.broadcast_to(lp, (1, LANES))

        @pl.when(i == 0)
        def _():
            dg_ref[...] = dgp
            loss_ref[...] = lpb

        @pl.when(i > 0)
        def _():
            dg_ref[...] += dgp
            loss_ref[...] += lpb

    blk = pl.BlockSpec((tm, d), lambda i: (i, 0))
    row = pl.BlockSpec((1, d), lambda i: (0, 0))
    return pl.pallas_call(
        body, grid=(t // tm,), in_specs=[blk, row, blk],
        out_specs=[blk, blk, row, pl.BlockSpec((1, LANES), lambda i: (0, 0))],
        out_shape=[jax.ShapeDtypeStruct((t, d), F32), jax.ShapeDtypeStruct((t, d), BF16),
                   jax.ShapeDtypeStruct((1, d), F32), jax.ShapeDtypeStruct((1, LANES), F32)], name="final_loss_bwd",
        compiler_params=_cparams(("arbitrary",), 16 * tm * d * 4),
    )(x, g, tgt)


def _seq_scan(v, seq, reverse):
    row = lax.broadcasted_iota(jnp.int32, v.shape, 0)
    k = 1
    while k < seq:
        if reverse:
            v = v + jnp.where(row < seq - k, pltpu.roll(v, seq - k, 0), 0.0)
        else:
            v = v + jnp.where(row >= k, pltpu.roll(v, k, 0), 0.0)
        k *= 2
    return v


def _log_sigmoid(v):
    return jnp.minimum(v, 0.0) - jnp.log(1.0 + jnp.exp(-jnp.abs(v)))


def _fox_prep(fl, bf, qkv, nb, seq):
    def body(f_ref, b_ref, q_ref, k_ref, v_ref, qa_ref, ka_ref, vm_ref):
        dsum = _seq_scan(_log_sigmoid(f_ref[...] + b_ref[...]), seq, False)
        d1 = dsum.astype(BF16).astype(F32)
        r1 = dsum - d1
        d2 = r1.astype(BF16).astype(F32)
        d3 = (r1 - d2).astype(BF16).astype(F32)
        lane = lax.broadcasted_iota(jnp.int32, (seq, LANES), 1)
        first = lane < HEAD_DIM
        l64 = jnp.where(first, lane, lane - HEAD_DIM)
        for p in range(N_PAIRS):
            def head_cols(a, p=p):
                return jnp.where(first, a[:, 2 * p:2 * p + 1], a[:, 2 * p + 1:2 * p + 2])

            e1, e2, e3 = head_cols(d1), head_cols(d2), head_cols(d3)
            aux_q = jnp.where(l64 == 0, e1, jnp.where(l64 == 1, e2, jnp.where(l64 == 2, e3,
                              jnp.where(l64 < 6, 1.0, 0.0)))).astype(BF16)
            aux_k = jnp.where(l64 < 3, 1.0, jnp.where(l64 == 3, -e1, jnp.where(l64 == 4, -e2,
                              jnp.where(l64 == 5, -e3, 0.0)))).astype(BF16)
            cols = slice(LANES * p, LANES * (p + 1))
            qs = q_ref[:, cols] * ATT_SCALE
            vp = v_ref[:, cols]
            zero = jnp.zeros_like(qs)
            qa_ref[0, p, :, :LANES] = qs
            qa_ref[0, p, :, LANES:] = aux_q
            ka_ref[0, p, :, :LANES] = k_ref[:, cols]
            ka_ref[0, p, :, LANES:] = aux_k
            vm_ref[0, p, 0] = jnp.where(first, vp, zero)
            vm_ref[0, p, 1] = jnp.where(first, zero, vp)

    def part(c):
        return pl.BlockSpec((seq, D_ATTN), lambda b, c=c: (b, c))

    return pl.pallas_call(
        body, grid=(nb,),
        in_specs=[pl.BlockSpec((seq, LANES), lambda b: (b, 0)), pl.BlockSpec((1, LANES), lambda b: (0, 0)),
                  part(0), part(1), part(2)],
        out_specs=[pl.BlockSpec((1, N_PAIRS, seq, ATT_K), lambda b: (b, 0, 0, 0)),
                   pl.BlockSpec((1, N_PAIRS, seq, ATT_K), lambda b: (b, 0, 0, 0)),
                   pl.BlockSpec((1, N_PAIRS, 2, seq, LANES), lambda b: (b, 0, 0, 0, 0))],
        out_shape=[jax.ShapeDtypeStruct((nb, N_PAIRS, seq, ATT_K), BF16),
                   jax.ShapeDtypeStruct((nb, N_PAIRS, seq, ATT_K), BF16),
                   jax.ShapeDtypeStruct((nb, N_PAIRS, 2, seq, LANES), BF16)],
        name="fox_prep", compiler_params=_cparams(("parallel",), 48 * 1024 * 1024),
    )(fl, bf, qkv, qkv, qkv)


def _fox_prep_bwd(dd, fl, bf, seq):
    t = fl.shape[0]

    def body(d_ref, f_ref, b_ref, o_ref, db_ref):
        i = pl.program_id(0)
        dlog = _seq_scan(d_ref[...], seq, True)
        dfl = dlog * _sigmoid(-(f_ref[...] + b_ref[...]))
        o_ref[...] = dfl.astype(o_ref.dtype)
        dbp = jnp.sum(dfl, axis=0, keepdims=True)

        @pl.when(i == 0)
        def _():
            db_ref[...] = dbp

        @pl.when(i > 0)
        def _():
            db_ref[...] += dbp

    blk = pl.BlockSpec((seq, LANES), lambda b: (b, 0))
    row = pl.BlockSpec((1, LANES), lambda b: (0, 0))
    return pl.pallas_call(
        body, grid=(t // seq,), in_specs=[blk, blk, row], out_specs=[blk, row],
        out_shape=[jax.ShapeDtypeStruct((t, LANES), BF16), jax.ShapeDtypeStruct((1, LANES), F32)], name="fox_prep_bwd",
        compiler_params=_cparams(("arbitrary",), 24 * seq * LANES * 4),
    )(dd, fl, bf)


def _stack_heads(qc):
    lane = lax.broadcasted_iota(jnp.int32, qc.shape, 1)
    first = (lane & HEAD_DIM) == 0
    zero = jnp.zeros_like(qc)
    return jnp.concatenate([jnp.where(first, qc, zero), jnp.where(first, zero, qc)], axis=0)


def _pair_rows(a, ta):
    lane = lax.broadcasted_iota(jnp.int32, (ta, LANES), 1)
    return jnp.where(lane < HEAD_DIM, a[:ta], a[ta:])


def _diag_mask(ta):
    r = lax.broadcasted_iota(jnp.int32, (2 * ta, ta), 0)
    c = lax.broadcasted_iota(jnp.int32, (2 * ta, ta), 1)
    return c <= jnp.where(r >= ta, r - ta, r)


def _nt(a, b):
    return lax.dot_general(a, b, (((1,), (1,)), ((), ())), preferred_element_type=F32)


def _tn(a, b):
    return lax.dot_general(a, b, (((0,), (0,)), ((), ())), preferred_element_type=F32)


def _grid_ends(ids, sizes):
    first = functools.reduce(jnp.logical_and, [i == 0 for i in ids])
    last = functools.reduce(jnp.logical_and, [i == n - 1 for i, n in zip(ids, sizes)])
    return first, last


def _fox_fwd(qa, ka, vm, nb, seq, ta, carry_gather=None):
    nq = seq // ta
    npp = ATT_PAIRS_FWD
    grid = (nb, N_PAIRS // npp, nq)

    def body(q_ref, k_ref, v_ref, *rest):
        if carry_gather is None:
            o_ref, lse_ref = rest
        else:
            x_ref, o_ref, lse_ref, gathered_ref, send_sems, recv_sems = rest
            first_step, last_step = _grid_ends([pl.program_id(a) for a in range(3)], grid)

            @pl.when(first_step)
            def _():
                _gather_start(x_ref, gathered_ref, send_sems, recv_sems)

        i = pl.program_id(2)
        q2s = [_stack_heads(q_ref[0, pp]) for pp in range(npp)]

        def step(j, carry, masked):
            rows = pl.ds(pl.multiple_of(j * ta, ta), ta)
            out = []
            for pp in range(npp):
                m, l, acc = carry[pp]
                s = _nt(q2s[pp], k_ref[0, pp, rows, :])
                if masked:
                    s = jnp.where(_diag_mask(ta), s, NEG_BIG)
                m_new = jnp.maximum(m, jnp.max(s, axis=-1, keepdims=True))
                p = jnp.exp(s - m_new)
                corr = jnp.exp(m - m_new)
                l = corr * l + jnp.sum(p, axis=-1, keepdims=True)
                pb = p.astype(BF16)
                pv = (jnp.dot(pb[:ta], v_ref[0, pp, 0, rows, :], preferred_element_type=F32)
                      + jnp.dot(pb[ta:], v_ref[0, pp, 1, rows, :], preferred_element_type=F32))
                out.append((m_new, l, _pair_rows(corr, ta) * acc + pv))
            return tuple(out)

        init = tuple((jnp.full((2 * ta, 1), NEG_BIG, F32), jnp.zeros((2 * ta, 1), F32),
                      jnp.zeros((ta, LANES), F32)) for _ in range(npp))
        carry = lax.fori_loop(0, i, functools.partial(step, masked=False), init)
        for pp, (m, l, acc) in enumerate(step(i, carry, True)):
            o_ref[:, LANES * pp:LANES * (pp + 1)] = (acc * _pair_rows(1.0 / l, ta)).astype(o_ref.dtype)
            lse = m + jnp.log(l)
            lse_ref[0, pp, 0] = lse[:ta]
            lse_ref[0, pp, 1] = lse[ta:]

        if carry_gather is not None:
            @pl.when(last_step)
            def _():
                _gather_finish(x_ref, gathered_ref, send_sems, recv_sems)

    vmem = (2 * npp * (_nbytes((seq, ATT_K), BF16) + 2 * _nbytes((seq, LANES), BF16)) + 24 * npp * ta * ta * 4
            + 8 * 1024 * 1024)
    in_specs = [pl.BlockSpec((1, npp, ta, ATT_K), lambda b, g, i: (b, g, i, 0)),
                pl.BlockSpec((1, npp, seq, ATT_K), lambda b, g, i: (b, g, 0, 0)),
                pl.BlockSpec((1, npp, 2, seq, LANES), lambda b, g, i: (b, g, 0, 0, 0))]
    out_specs = [pl.BlockSpec((ta, LANES * npp), lambda b, g, i: (b * nq + i, g)),
                 pl.BlockSpec((1, npp, 2, ta, 1), lambda b, g, i: (b, g, 0, i, 0))]
    out_shape = [jax.ShapeDtypeStruct((nb * seq, D_ATTN), BF16), jax.ShapeDtypeStruct((nb, N_PAIRS, 2, seq, 1), F32)]
    if carry_gather is None:
        return pl.pallas_call(
            body, grid=grid, in_specs=in_specs, out_specs=out_specs, out_shape=out_shape, name="fox_fwd",
            compiler_params=_cparams(("parallel", "parallel", "parallel"), vmem),
        )(qa, ka, vm)
    any_spec = pl.BlockSpec(memory_space=pl.ANY)
    return pl.pallas_call(
        body, grid=grid, in_specs=in_specs + [any_spec], out_specs=out_specs + [any_spec],
        out_shape=out_shape + [jax.ShapeDtypeStruct((N_DEV,) + carry_gather.shape, carry_gather.dtype)],
        scratch_shapes=list(GATHER_SEMS), name="fox_fwd_gather",
        compiler_params=_cparams(("arbitrary", "arbitrary", "arbitrary"), vmem),
    )(qa, ka, vm, carry_gather)


def _fox_bwd(qa, ka, vm, y, dy, lse, nb, seq, ta, carry_exchange=None):
    nq = seq // ta
    npp = ATT_PAIRS_BWD
    grid = (nb, N_PAIRS // npp, nq)

    def body(q_ref, k_ref, v_ref, o_ref, do_ref, lse_ref, *rest):
        if carry_exchange is None:
            dq_ref, dk_ref, dv_ref, rs_ref, cs_ref = rest
        else:
            t_ref, dq_ref, dk_ref, dv_ref, rs_ref, cs_ref, landed_ref, send_sems, recv_sems = rest
            first_step, last_step = _grid_ends([pl.program_id(a) for a in range(3)], grid)

            @pl.when(first_step)
            def _():
                _chips_start(t_ref, landed_ref, send_sems, recv_sems)

        i = pl.program_id(2)

        @pl.when(i == 0)
        def _():
            dk_ref[...] = jnp.zeros_like(dk_ref)
            dv_ref[...] = jnp.zeros_like(dv_ref)
            cs_ref[...] = jnp.zeros_like(cs_ref)

        first = lax.broadcasted_iota(jnp.int32, (ta, LANES), 1) < HEAD_DIM
        q2s, do2s, deltas, lses = [], [], [], []
        for pp in range(npp):
            cols = slice(LANES * pp, LANES * (pp + 1))
            q2s.append(_stack_heads(q_ref[0, pp]))
            do = do_ref[:, cols]
            doo = do * o_ref[:, cols].astype(F32)
            do2s.append(jnp.concatenate([jnp.where(first, do, 0.0), jnp.where(first, 0.0, do)], axis=0).astype(BF16))
            deltas.append(jnp.concatenate([jnp.sum(jnp.where(first, doo, 0.0), axis=-1, keepdims=True),
                                           jnp.sum(jnp.where(first, 0.0, doo), axis=-1, keepdims=True)], axis=0))
            lses.append(jnp.concatenate([lse_ref[0, pp, 0], lse_ref[0, pp, 1]], axis=0))

        def step(j, carry, masked):
            rows = pl.ds(pl.multiple_of(j * ta, ta), ta)
            out = []
            for pp in range(npp):
                dq_acc, rs_acc = carry[pp]
                cols = slice(LANES * pp, LANES * (pp + 1))
                ks = k_ref[0, pp, rows, :]
                s = _nt(q2s[pp], ks)
                if masked:
                    s = jnp.where(_diag_mask(ta), s, NEG_BIG)
                p = jnp.exp(s - lses[pp])
                dp = _nt(do2s[pp], v_ref[0, pp, 0, rows, :] + v_ref[0, pp, 1, rows, :])
                ds32 = p * (dp - deltas[pp])
                ds = ds32.astype(BF16)
                dk_ref[rows, cols] += _tn(ds, q2s[pp][:, :LANES])
                dv_ref[rows, cols] += _tn(p.astype(BF16), do2s[pp])
                cs_ref[0, pp, 0, j] += jnp.sum(ds32[:ta], axis=0, keepdims=True)
                cs_ref[0, pp, 1, j] += jnp.sum(ds32[ta:], axis=0, keepdims=True)
                out.append((dq_acc + jnp.dot(ds, ks[:, :LANES], preferred_element_type=F32),
                            rs_acc + jnp.sum(ds32, axis=-1, keepdims=True)))
            return tuple(out)

        init = tuple((jnp.zeros((2 * ta, LANES), F32), jnp.zeros((2 * ta, 1), F32)) for _ in range(npp))
        carry = lax.fori_loop(0, i, functools.partial(step, masked=False), init)
        for pp, (dq_acc, rs_acc) in enumerate(step(i, carry, True)):
            dq = jnp.where(first, dq_acc[:ta], dq_acc[ta:]) * ATT_SCALE
            dq_ref[:, LANES * pp:LANES * (pp + 1)] = dq.astype(dq_ref.dtype)
            rs_row = jnp.transpose(jnp.broadcast_to(rs_acc, (2 * ta, LANES)))[0:1]
            rs_ref[0, pp, 0, 0] = rs_row[:, :ta]
            rs_ref[0, pp, 1, 0] = rs_row[:, ta:]

        if carry_exchange is not None:
            @pl.when(last_step)
            def _():
                _chips_finish(t_ref, landed_ref, send_sems, recv_sems)

    vmem = (2 * npp * (_nbytes((seq, ATT_K), BF16) + 2 * _nbytes((seq, LANES), BF16) + 2 * _nbytes((seq, LANES), F32))
            + 32 * npp * ta * ta * 4 + 8 * 1024 * 1024)
    qblk = lambda b, g, i: (b * nq + i, g)
    acc_blk = pl.BlockSpec((seq, LANES * npp), lambda b, g, i: (b, g))
    in_specs = [pl.BlockSpec((1, npp, ta, ATT_K), lambda b, g, i: (b, g, i, 0)),
                pl.BlockSpec((1, npp, seq, ATT_K), lambda b, g, i: (b, g, 0, 0)),
                pl.BlockSpec((1, npp, 2, seq, LANES), lambda b, g, i: (b, g, 0, 0, 0)),
                pl.BlockSpec((ta, LANES * npp), qblk), pl.BlockSpec((ta, LANES * npp), qblk),
                pl.BlockSpec((1, npp, 2, ta, 1), lambda b, g, i: (b, g, 0, i, 0))]
    out_specs = [pl.BlockSpec((ta, LANES * npp), qblk), acc_blk, acc_blk,
                 pl.BlockSpec((1, npp, 2, 1, 1, ta), lambda b, g, i: (b, g, 0, i, 0, 0)),
                 pl.BlockSpec((1, npp, 2, nq, 1, ta), lambda b, g, i: (b, g, 0, 0, 0, 0))]
    sums = jax.ShapeDtypeStruct((nb, N_PAIRS, 2, nq, 1, ta), F32)
    out_shape = [jax.ShapeDtypeStruct((nb * seq, D_ATTN), BF16), jax.ShapeDtypeStruct((nb * seq, D_ATTN), F32),
                 jax.ShapeDtypeStruct((nb * seq, D_ATTN), F32), sums, sums]
    if carry_exchange is None:
        return pl.pallas_call(
            body, grid=grid, in_specs=in_specs, out_specs=out_specs, out_shape=out_shape, name="fox_bwd",
            compiler_params=_cparams(("parallel", "parallel", "arbitrary"), vmem),
        )(qa, ka, vm, y, dy, lse)
    any_spec = pl.BlockSpec(memory_space=pl.ANY)
    return pl.pallas_call(
        body, grid=grid, in_specs=in_specs + [any_spec], out_specs=out_specs + [any_spec],
        out_shape=out_shape + [jax.ShapeDtypeStruct(carry_exchange.shape, carry_exchange.dtype)],
        scratch_shapes=list(CHIPS_SEMS), name="fox_bwd_exchange",
        compiler_params=_cparams(("arbitrary", "arbitrary", "arbitrary"), vmem),
    )(qa, ka, vm, y, dy, lse, carry_exchange)


def _shift_down(a, k):
    row = lax.broadcasted_iota(jnp.int32, a.shape, 0)
    return jnp.where(row >= k, pltpu.roll(a, k, 0), 0.0)


def _shift_up(a, k):
    n = a.shape[0]
    row = lax.broadcasted_iota(jnp.int32, a.shape, 0)
    return jnp.where(row < n - k, pltpu.roll(a, n - k, 0), 0.0)


def _by_group(vals, shape):
    lane = lax.broadcasted_iota(jnp.int32, shape, 1)
    out = vals[-1]
    for gi in range(len(vals) - 2, -1, -1):
        out = jnp.where(lane < POOL_GROUP * (gi + 1), vals[gi], out)
    return out


def _pooled(u):
    s2 = u + _shift_down(u, 1)
    s4 = s2 + _shift_down(s2, 2)
    s8 = s4 + _shift_down(s4, 4)
    s16 = s8 + _shift_down(s8, 8)
    win = _by_group([s2, s4, s8, s16], u.shape)
    row = lax.broadcasted_iota(jnp.int32, u.shape, 0)
    wsize = _by_group([jnp.full(u.shape, w, jnp.int32) for w in POOL_WINDOWS], u.shape)
    inv = 1.0 / jnp.minimum(row + 1, wsize).astype(F32)
    return win * inv - u, inv


def _pool_fwd(rest, wbd, scale, seq):
    t = rest.shape[0]

    def body(u_ref, w_ref, s_ref, o_ref):
        pooled, _ = _pooled(u_ref[...])
        pw = jnp.dot(pooled.astype(BF16), w_ref[...], preferred_element_type=F32)
        o_ref[...] = (pw * s_ref[...]).astype(o_ref.dtype)

    blk = pl.BlockSpec((seq, D_POOL), lambda b: (b, 0))
    return pl.pallas_call(
        body, grid=(t // seq,),
        in_specs=[blk, pl.BlockSpec((D_POOL, D_POOL), lambda b: (0, 0)), pl.BlockSpec((1, D_POOL), lambda b: (0, 0))],
        out_specs=blk, out_shape=jax.ShapeDtypeStruct((t, D_POOL), BF16), name="pool_fwd",
        compiler_params=_cparams(("parallel",), 24 * seq * D_POOL * 4),
    )(rest, wbd, scale)


def _pool_bwd(rest, dy, wbd, wbd_t, scale, seq):
    t = rest.shape[0]

    def body(u_ref, dy_ref, w_ref, wt_ref, s_ref, du_ref, dw_ref, dsc_ref):
        i = pl.program_id(0)
        pooled, inv = _pooled(u_ref[...])
        pb = pooled.astype(BF16)
        pw = jnp.dot(pb, w_ref[...], preferred_element_type=F32)
        dyp = dy_ref[...]
        dsp = jnp.sum(dyp * pw, axis=0, keepdims=True)
        dpw = (dyp * s_ref[...]).astype(BF16)
        dwp = _tn(pb, dpw)
        dpooled = jnp.dot(dpw, wt_ref[...], preferred_element_type=F32)
        dwin = dpooled * inv
        t2 = dwin + _shift_up(dwin, 1)
        t4 = t2 + _shift_up(t2, 2)
        t8 = t4 + _shift_up(t4, 4)
        t16 = t8 + _shift_up(t8, 8)
        du_ref[...] = (_by_group([t2, t4, t8, t16], dwin.shape) - dpooled).astype(du_ref.dtype)

        @pl.when(i == 0)
        def _():
            dw_ref[...] = dwp
            dsc_ref[...] = dsp

        @pl.when(i > 0)
        def _():
            dw_ref[...] += dwp
            dsc_ref[...] += dsp

    blk = pl.BlockSpec((seq, D_POOL), lambda b: (b, 0))
    sq = pl.BlockSpec((D_POOL, D_POOL), lambda b: (0, 0))
    row = pl.BlockSpec((1, D_POOL), lambda b: (0, 0))
    return pl.pallas_call(
        body, grid=(t // seq,),
        in_specs=[blk, pl.BlockSpec((seq, D_POOL), lambda b: (b, 2)), sq, sq, row],
        out_specs=[blk, sq, row],
        out_shape=[jax.ShapeDtypeStruct((t, D_POOL), BF16), jax.ShapeDtypeStruct((D_POOL, D_POOL), F32),
                   jax.ShapeDtypeStruct((1, D_POOL), F32)], name="pool_bwd",
        compiler_params=_cparams(("arbitrary",), 40 * seq * D_POOL * 4),
    )(rest, dy, wbd, wbd_t, scale)


def _conv_fwd(rest, cw, seq):
    t = rest.shape[0]

    def body(cb_ref, cc_ref, ch_ref, w_ref, o_ref):
        u = cc_ref[...] * ch_ref[...]
        y = w_ref[0:1, :] * _shift_down(u, 2) + w_ref[1:2, :] * _shift_down(u, 1) + w_ref[2:3, :] * u
        o_ref[...] = (cb_ref[...] * y).astype(o_ref.dtype)

    def col(c):
        return pl.BlockSpec((seq, D_CONV), lambda b, c=c: (b, c))

    return pl.pallas_call(
        body, grid=(t // seq,), in_specs=[col(1), col(2), col(3), pl.BlockSpec((8, D_CONV), lambda b: (0, 0))],
        out_specs=pl.BlockSpec((seq, D_CONV), lambda b: (b, 0)),
        out_shape=jax.ShapeDtypeStruct((t, D_CONV), BF16), name="conv_fwd",
        compiler_params=_cparams(("parallel",), 24 * seq * D_CONV * 4),
    )(rest, rest, rest, cw)


def _conv_bwd(rest, dy, cw, seq):
    t = rest.shape[0]

    def body(cb_ref, cc_ref, ch_ref, dy_ref, w_ref, o_ref, dw_ref):
        i = pl.program_id(0)
        cc = cc_ref[...]
        ch = ch_ref[...]
        u = cc * ch
        u1 = _shift_down(u, 1)
        u2 = _shift_down(u, 2)
        y = w_ref[0:1, :] * u2 + w_ref[1:2, :] * u1 + w_ref[2:3, :] * u
        dyc = dy_ref[...]
        d2 = dyc * cb_ref[...]
        du = w_ref[0:1, :] * _shift_up(d2, 2) + w_ref[1:2, :] * _shift_up(d2, 1) + w_ref[2:3, :] * d2
        o_ref[:, 0:D_CONV] = (dyc * y).astype(o_ref.dtype)
        o_ref[:, D_CONV:2 * D_CONV] = (du * ch).astype(o_ref.dtype)
        o_ref[:, 2 * D_CONV:3 * D_CONV] = (du * cc).astype(o_ref.dtype)
        tap = lax.broadcasted_iota(jnp.int32, (8, D_CONV), 0)
        dwp = jnp.where(tap == 0, jnp.sum(d2 * u2, axis=0, keepdims=True),
                        jnp.where(tap == 1, jnp.sum(d2 * u1, axis=0, keepdims=True),
                                  jnp.where(tap == 2, jnp.sum(d2 * u, axis=0, keepdims=True), 0.0)))

        @pl.when(i == 0)
        def _():
            dw_ref[...] = dwp

        @pl.when(i > 0)
        def _():
            dw_ref[...] += dwp

    def col(c):
        return pl.BlockSpec((seq, D_CONV), lambda b, c=c: (b, c))

    taps = pl.BlockSpec((8, D_CONV), lambda b: (0, 0))
    return pl.pallas_call(
        body, grid=(t // seq,), in_specs=[col(1), col(2), col(3), col(3), taps],
        out_specs=[pl.BlockSpec((seq, 3 * D_CONV), lambda b: (b, 0)), taps],
        out_shape=[jax.ShapeDtypeStruct((t, 3 * D_CONV), BF16), jax.ShapeDtypeStruct((8, D_CONV), F32)],
        name="conv_bwd", compiler_params=_cparams(("arbitrary",), 48 * seq * D_CONV * 4),
    )(rest, rest, rest, dy, cw)


def _adamw(w, g, m, v, name):
    r, c = w.shape
    tr = _pick(r, (512, 352, 256, 128)) if r > 512 else r

    def body(w_ref, g_ref, m_ref, v_ref, d_ref, mo_ref, vo_ref):
        gv = g_ref[...]
        mn = ADAM_B1 * m_ref[...] + (1.0 - ADAM_B1) * gv
        vn = ADAM_B2 * v_ref[...] + (1.0 - ADAM_B2) * (gv * gv)
        m_hat = mn / (1.0 - ADAM_B1 ** ADAM_STEP)
        v_hat = vn / (1.0 - ADAM_B2 ** ADAM_STEP)
        d_ref[...] = -ADAM_LR * (m_hat / (jnp.sqrt(v_hat) + ADAM_EPS) + ADAM_WD * w_ref[...])
        mo_ref[...] = mn
        vo_ref[...] = vn

    blk = pl.BlockSpec((tr, c), lambda i: (i, 0))
    sds = jax.ShapeDtypeStruct((r, c), F32)
    return pl.pallas_call(
        body, grid=(r // tr,), in_specs=[blk] * 4, out_specs=[blk] * 3, out_shape=[sds] * 3, name=name,
        compiler_params=_cparams(("parallel",), 20 * tr * max(c, LANES) * 4),
    )(w, g, m, v)


def _sum_slots(a, name):
    ns, r, c = a.shape
    tr = _pick(r, (384, 368, 256, 184, 136, 128, 88, 8))

    def body(a_ref, o_ref):
        acc = a_ref[0].astype(F32)
        for s in range(1, ns):
            acc = acc + a_ref[s].astype(F32)
        o_ref[...] = acc

    return pl.pallas_call(
        body, grid=(r // tr,), in_specs=[pl.BlockSpec((ns, tr, c), lambda i: (0, i, 0))],
        out_specs=pl.BlockSpec((tr, c), lambda i: (i, 0)), out_shape=jax.ShapeDtypeStruct((r, c), F32), name=name,
        compiler_params=_cparams(("parallel",), 4 * (ns + 2) * tr * c * 4),
    )(a)


def _add_core_half(core, g4, theirs, out_dtype, name):
    ns, _, r, c = g4.shape
    tr = _pick(r, (384, 368, 256, 184, 136, 128, 88, 8))

    def body(core_ref, a_ref, b_ref, o_ref):
        o_ref[...] = (a_ref[0] + b_ref[...]).astype(o_ref.dtype)

    blk = pl.BlockSpec((1, tr, c), lambda s, i, core_ref: (s, i, 0))
    return pl.pallas_call(
        body,
        grid_spec=pltpu.PrefetchScalarGridSpec(
            num_scalar_prefetch=1, grid=(ns, r // tr),
            in_specs=[pl.BlockSpec((1, 1, tr, c), lambda s, i, core_ref: (s, core_ref[0], i, 0)), blk],
            out_specs=blk),
        out_shape=jax.ShapeDtypeStruct((ns, r, c), out_dtype), name=name,
        compiler_params=_cparams(("parallel", "parallel"), 10 * tr * c * 4),
    )(core, g4, theirs)


def _sum_chips(order, own, landed, name):
    ns, r, c = own.shape
    tr = _pick(r, (384, 368, 256, 184, 136, 128, 88, 8))

    def body(order_ref, a_ref, b1_ref, b2_ref, b3_ref, o_ref):
        o_ref[...] = ((a_ref[0].astype(F32) + b1_ref[0].astype(F32)) + b2_ref[0].astype(F32)) + b3_ref[0].astype(F32)

    def slot(k):
        return pl.BlockSpec((1, tr, c), lambda i, order_ref, k=k: (order_ref[k], i, 0))

    return pl.pallas_call(
        body,
        grid_spec=pltpu.PrefetchScalarGridSpec(
            num_scalar_prefetch=1, grid=(r // tr,), in_specs=[slot(0), slot(1), slot(2), slot(3)],
            out_specs=pl.BlockSpec((tr, c), lambda i, order_ref: (i, 0))),
        out_shape=jax.ShapeDtypeStruct((r, c), F32), name=name,
        compiler_params=_cparams(("parallel",), 16 * tr * c * 4),
    )(order, own, landed, landed, landed)


def _mesh_pos():
    return lax.axis_index("x"), lax.axis_index("y"), lax.axis_index("c")


def _comm_call(name, gathers=(), chips=None):
    payloads = list(gathers) + ([] if chips is None else [chips])
    n = len(payloads)

    def body(*refs):
        ins, outs, sems = refs[:n], refs[n:2 * n], refs[2 * n:]
        jobs = [(_gather_start, _gather_finish)] * len(gathers) + ([] if chips is None else [(_chips_start, _chips_finish)])
        for k, (start, _) in enumerate(jobs):
            start(ins[k], outs[k], sems[2 * k], sems[2 * k + 1])
        for k, (_, finish) in enumerate(jobs):
            finish(ins[k], outs[k], sems[2 * k], sems[2 * k + 1])

    any_spec = pl.BlockSpec(memory_space=pl.ANY)
    out_shape = [jax.ShapeDtypeStruct((N_DEV,) + x.shape, x.dtype) for x in gathers]
    sems = list(GATHER_SEMS) * len(gathers)
    if chips is not None:
        out_shape.append(jax.ShapeDtypeStruct(chips.shape, chips.dtype))
        sems += list(CHIPS_SEMS)
    outs = pl.pallas_call(body, out_shape=out_shape, in_specs=[any_spec] * n, out_specs=[any_spec] * n,
                          scratch_shapes=sems, name=name)(*payloads)
    return [_fill_own_slot(o, x) for o, x in zip(outs, gathers)] + ([] if chips is None else [outs[-1]])


GATHER_SEMS = (pltpu.SemaphoreType.DMA((7,)), pltpu.SemaphoreType.DMA((7,)))


def _fill_own_slot(gathered, x):
    mx, my, mc = _mesh_pos()
    return lax.dynamic_update_slice_in_dim(gathered, x[None], 4 * mx + 2 * my + mc, axis=0)


def _gather_copies(x_ref, out_ref, send_sems, recv_sems):
    mx, my, mc = _mesh_pos()
    me, sibling = (mx, my, mc), (mx, my, 1 - mc)
    chips = [(1 - mx, my), (mx, 1 - my), (1 - mx, 1 - my)]

    def slot(px, py, pc):
        return out_ref.at[4 * px + 2 * py + pc]

    def copy(k, block, to, src=None):
        return pltpu.make_async_remote_copy(
            src_ref=slot(*block) if src is None else src, dst_ref=slot(*block),
            send_sem=send_sems.at[k], recv_sem=recv_sems.at[k],
            device_id=to, device_id_type=pl.DeviceIdType.MESH)

    first = [copy(0, me, sibling, src=x_ref)]
    first += [copy(1 + j, me, (*chip, mc), src=x_ref) for j, chip in enumerate(chips)]
    passed = [copy(4 + j, (*chip, mc), sibling) for j, chip in enumerate(chips)]
    over_ici = [copy(1 + j, (*chip, mc), me) for j, chip in enumerate(chips)]
    over_d2d = [copy(0, sibling, me)] + [copy(4 + j, (*chip, 1 - mc), me) for j, chip in enumerate(chips)]
    return first, passed, over_ici, over_d2d


def _gather_start(x_ref, out_ref, send_sems, recv_sems):
    for cp in _gather_copies(x_ref, out_ref, send_sems, recv_sems)[0]:
        cp.start()


def _gather_finish(x_ref, out_ref, send_sems, recv_sems):
    first, passed, over_ici, over_d2d = _gather_copies(x_ref, out_ref, send_sems, recv_sems)
    for landed, relay in zip(over_ici, passed):
        landed.wait_recv()
        relay.start()
    for landed in over_d2d:
        landed.wait_recv()
    for cp in first + passed:
        cp.wait_send()


SIBLING_SEMS = (pltpu.SemaphoreType.DMA((N_CHIPS,)), pltpu.SemaphoreType.DMA((N_CHIPS,)))


def _sibling_copies(g_ref, theirs_ref, send_sems, recv_sems):
    mx, my, mc = _mesh_pos()
    return [pltpu.make_async_remote_copy(
        src_ref=g_ref.at[chip, 1 - mc], dst_ref=theirs_ref.at[chip],
        send_sem=send_sems.at[chip], recv_sem=recv_sems.at[chip],
        device_id=(mx, my, 1 - mc), device_id_type=pl.DeviceIdType.MESH) for chip in range(N_CHIPS)]


def _sibling_start(g_ref, theirs_ref, send_sems, recv_sems):
    for cp in _sibling_copies(g_ref, theirs_ref, send_sems, recv_sems):
        cp.start()


def _sibling_finish(g_ref, theirs_ref, send_sems, recv_sems):
    copies = _sibling_copies(g_ref, theirs_ref, send_sems, recv_sems)
    for cp in copies:
        cp.wait_recv()
    for cp in copies:
        cp.wait_send()


CHIPS_SEMS = (pltpu.SemaphoreType.DMA((N_CHIPS - 1,)), pltpu.SemaphoreType.DMA((N_CHIPS - 1,)))


def _chips_copies(t_ref, out_ref, send_sems, recv_sems):
    mx, my, mc = _mesh_pos()
    my_chip = 2 * mx + my
    copies = []
    for k in range(1, N_CHIPS):
        px = 1 - mx if k & 2 else mx
        py = 1 - my if k & 1 else my
        peer_chip = 2 * px + py

        def rdma(dst_slot, px=px, py=py, peer_chip=peer_chip, k=k):
            return pltpu.make_async_remote_copy(
                src_ref=t_ref.at[peer_chip], dst_ref=out_ref.at[dst_slot],
                send_sem=send_sems.at[k - 1], recv_sem=recv_sems.at[k - 1],
                device_id=(px, py, mc), device_id_type=pl.DeviceIdType.MESH)

        copies.append((rdma(my_chip), rdma(peer_chip)))
    return copies


def _chips_start(t_ref, out_ref, send_sems, recv_sems):
    for send, _ in _chips_copies(t_ref, out_ref, send_sems, recv_sems):
        send.start()


def _chips_finish(t_ref, out_ref, send_sems, recv_sems):
    copies = _chips_copies(t_ref, out_ref, send_sems, recv_sems)
    for _, landed in copies:
        landed.wait_recv()
    for send, _ in copies:
        send.wait_send()


def _add_sibling(g4, theirs):
    core = jnp.reshape(lax.axis_index("c"), (1,)).astype(jnp.int32)
    return _add_core_half(core, g4, theirs, BF16, name="add_sibling_grads")


def _sum_landed(chip_sums, landed):
    mx, my, _ = _mesh_pos()
    order = jnp.stack([2 * mx + my, 2 * (1 - mx) + my, 2 * mx + (1 - my), 2 * (1 - mx) + (1 - my)]).astype(jnp.int32)
    return _sum_chips(order, chip_sums, landed, name="sum_grads")


def _perm_mix_rows(wt):
    f0 = D_QKV
    f1 = f0 + N_HEADS
    return jnp.concatenate([wt[:f0], wt[f1:], jnp.pad(wt[f0:f1], ((0, LANES - N_HEADS), (0, 0)))], axis=0)


def _unperm_mix_rows(gt):
    f0 = D_QKV
    return jnp.concatenate([gt[:f0], gt[f0 + D_REST:f0 + D_REST + N_HEADS], gt[f0:f0 + D_REST]], axis=0)


def _pack_shards(parts, l, dtype):
    w1i, w1o, wmi, wmo, w2i, w2o = parts
    rows = [w1i[l].T, w1o[l], jnp.pad(wmi[l].T, ((0, MIX_ROWS_PAD - MIX_ROWS), (0, 0))), wmo[l], w2i[l].T, w2o[l]]
    return jnp.concatenate(rows, axis=0).astype(dtype)


PACK_HEAD = FFN_ROWS + OUT_ROWS


def _ffn_weights(wg, o):
    return dict(wi_t=wg[:, o:o + FFN_ROWS].reshape(2 * D_FF, D_MODEL),
                wo=wg[:, o + FFN_ROWS:o + FFN_ROWS + OUT_ROWS].reshape(D_FF, D_MODEL))


def _tail_weights(wg):
    mix = dict(wm_t=_perm_mix_rows(wg[:, :MIX_ROWS].reshape(D_IN, D_MODEL)),
               wo=wg[:, MIX_ROWS_PAD:MIX_ROWS_PAD + MO_ROWS].reshape(D_MODEL, D_MODEL))
    return mix, _ffn_weights(wg, MIX_ROWS_PAD + MO_ROWS)


GRAD_AT = dict(f1i=0, f2i=FFN_ROWS, f1o=4 * OUT_ROWS, f2o=5 * OUT_ROWS, mi=6 * OUT_ROWS, mo=20 * MO_ROWS)
GRAD_ROWS = GRAD_AT["mo"] + MO_ROWS
GRAD_SHAPE = (N_CHIPS, 2, GRAD_ROWS, D_MODEL)


def _into_ffn_in(buf, tag, half):
    rb = GRAD_AT[tag] // FFN_ROWS
    return (buf, GRAD_SHAPE, (1, 2, FFN_ROWS, D_MODEL), lambda i: (2 * half + i, 0, rb, 0))


def _into_ffn_out(buf, tag):
    rb = GRAD_AT[tag] // OUT_ROWS
    return (buf, GRAD_SHAPE, (2, 2, OUT_ROWS, D_MODEL), lambda i: (i, 0, rb, 0))


def _into_mix_out(buf):
    rb = GRAD_AT["mo"] // MO_ROWS
    return (buf, GRAD_SHAPE, (N_CHIPS, 2, MO_ROWS, D_MODEL), lambda i: (0, 0, rb, 0))


def _put_mix_in(buf, g_in_t):
    gmi = _unperm_mix_rows(g_in_t).reshape(N_DEV, MIX_ROWS, D_MODEL)
    gmi = jnp.pad(gmi, ((0, 0), (0, OUT_ROWS - MIX_ROWS), (0, 0))).reshape(N_CHIPS, 2, OUT_ROWS, D_MODEL)
    return lax.dynamic_update_slice(buf, gmi, (0, 0, GRAD_AT["mi"], 0))


def _out_proj(a, w, x, alpha, next_gain, name):
    if next_gain is None:
        return _mm_nn(a, w, out_dtype=F32, res=x, alpha=alpha, name=name), None
    return _mm_nn(a, w, out_dtype=F32, res=x, alpha=alpha, next_gain=next_gain, name=name + "_norm")


def _ffn_forward(x, xn, w, next_gain, carry_gather=None):
    res = _ffn_in(xn, w["wi_t"], name="ffn_in", carry_gather=carry_gather)
    h, pg, pu = res[:3]
    x_new, xn_next = _out_proj(h, w["wo"], x, 0.5, next_gain, "ffn_out")
    out = (x_new, xn_next, dict(x=x, xn=xn, h=h, pg=pg, pu=pu))
    return out if carry_gather is None else out + (_fill_own_slot(res[3], carry_gather),)


def _ffn_backward(dxo, dxo_b, gain, w, saved, gbuf, tag, exchange=False):
    dzg, dzu = _ffn_bwd_mid(dxo_b, w["wo"], saved["pg"], saved["pu"], name="ffn_bwd_mid")
    gbuf = _mm_tn(saved["h"], dxo_b, alpha=0.5, tm=F_HALF, name="ffn_gw_out", into=_into_ffn_out(gbuf, tag + "o"))
    gbuf = _mm_tn(dzg, saved["xn"], tm=F_HALF, name="ffn_gw_in", into=_into_ffn_in(gbuf, tag + "i", 0))
    gbuf = _mm_tn(dzu, saved["xn"], tm=F_HALF, name="ffn_gw_in", into=_into_ffn_in(gbuf, tag + "i", 1))
    res = _dxn_norm_bwd([dzg, dzu], w["wi_t"], saved["x"], gain, dxo, name="ffn_dxn_norm_bwd",
                        carry_sibling=gbuf if exchange else None)
    return res[0], res[1], res[2], gbuf, (res[3] if exchange else None)


def _mixer_forward(x, xn, p, w, nb, seq, ta, next_gain, next_pack=None):
    qkv, rest, fl = _mix_proj(xn, w["wm_t"])
    qa, ka, vm = _fox_prep(fl, p["bf"], qkv, nb, seq)
    if next_pack is None:
        (y_attn, lse), next_gathered = _fox_fwd(qa, ka, vm, nb, seq, ta), None
    else:
        y_attn, lse, next_gathered = _fox_fwd(qa, ka, vm, nb, seq, ta, carry_gather=next_pack)
        next_gathered = _fill_own_slot(next_gathered, next_pack)
    y_pool = _pool_fwd(rest, p["wbd"], p["scale"], seq)
    y_conv = _conv_fwd(rest, p["cw"], seq)
    y = jnp.concatenate([y_attn, y_pool, y_conv], axis=1)
    x_new, xn_next = _out_proj(y, w["wo"], x, 1.0, next_gain, "mix_out")
    return x_new, xn_next, dict(x=x, xn=xn, qa=qa, ka=ka, vm=vm, rest=rest, fl=fl, lse=lse, y=y), next_gathered


def _mixer_backward(dxo, dxo_b, p, w, sv, nb, seq, ta, gbuf, pending=None):
    t = dxo.shape[0]
    dy = _mm_nt(dxo_b, w["wo"], out_dtype=F32, name="mix_dy")
    gbuf = _mm_tn(sv["y"], dxo_b, name="mix_gw_out", into=_into_mix_out(gbuf))
    res = _fox_bwd(sv["qa"], sv["ka"], sv["vm"], sv["y"], dy, sv["lse"], nb, seq, ta, carry_exchange=pending)
    dq, dk, dv, d_rows, d_cols = res[:5]
    landed = None if pending is None else res[5]
    ddh = (d_rows.reshape(nb, N_HEADS, seq) - d_cols.reshape(nb, N_HEADS, seq)).transpose(0, 2, 1)
    ddh = ddh.reshape(t, N_HEADS)
    dfl, dbf = _fox_prep_bwd(jnp.pad(ddh, ((0, 0), (0, LANES - N_HEADS))), sv["fl"], p["bf"], seq)
    dpool, dwbd, dscale = _pool_bwd(sv["rest"], dy, p["wbd"], p["wbd_t"], p["scale"], seq)
    dconv, dcw = _conv_bwd(sv["rest"], dy, p["cw"], seq)
    dproj = jnp.concatenate([dq, dk.astype(BF16), dv.astype(BF16), dpool, dconv, dfl], axis=1)
    gbuf = _put_mix_in(gbuf, _mm_tn(dproj, sv["xn"], tm=D_INP // 3, name="mix_gw_in"))
    dx, dx_b, dg = _dxn_norm_bwd([dproj], w["wm_t"], sv["x"], p["norm"], dxo, name="mix_dxn_norm_bwd")
    return dx, dx_b, dict(norm=dg, bf=dbf, wbd=dwbd, scale=dscale, cw=dcw), gbuf, landed


def _block_diag(wp):
    z = jnp.zeros((POOL_GROUP, POOL_GROUP), wp.dtype)
    return jnp.concatenate(
        [jnp.concatenate([wp[g] if g == r else z for g in range(4)], axis=1) for r in range(4)], axis=0)


def _row_pad(a, rows):
    a = a.reshape(-1, a.shape[-1])
    return jnp.pad(a, ((0, rows - a.shape[0]), (0, 0)))


def kernel(x, norm_ffn1, w_ffn1_in, w_ffn1_out, norm_mix, w_mix_in, b_forget, w_pool, pool_scale, conv_w, w_mix_out, norm_ffn2, w_ffn2_in, w_ffn2_out, norm_final, loss_target, m_norm_ffn1, m_w_ffn1_in, m_w_ffn1_out, m_norm_mix, m_w_mix_in, m_b_forget, m_w_pool, m_pool_scale, m_conv_w, m_w_mix_out, m_norm_ffn2, m_w_ffn2_in, m_w_ffn2_out, m_norm_final, v_norm_ffn1, v_w_ffn1_in, v_w_ffn1_out, v_norm_mix, v_w_mix_in, v_b_forget, v_w_pool, v_pool_scale, v_conv_w, v_w_mix_out, v_norm_ffn2, v_w_ffn2_in, v_w_ffn2_out, v_norm_final):
    nb, seq, d = x.shape
    depth = norm_ffn1.shape[0]
    t = nb * seq
    ta = _pick(seq, (ATT_TILE, 128))
    my_id = 4 * lax.axis_index("x") + 2 * lax.axis_index("y") + lax.axis_index("c")
    cshard = conv_w.shape[-1]

    shards = (w_ffn1_in, w_ffn1_out, w_mix_in, w_mix_out, w_ffn2_in, w_ffn2_out)
    pack0 = _pack_shards(shards, 0, BF16)
    wg_head, cw_g = _comm_call("gather_weights_and_taps", gathers=[
        pack0[:PACK_HEAD], _row_pad(conv_w.reshape(depth * 3, cshard), 16).reshape(4, LANES)])
    cw_all = cw_g.reshape(N_DEV, 16, cshard)[:, :depth * 3].reshape(N_DEV, depth, 3, cshard)
    cw_all = cw_all.transpose(1, 2, 0, 3).reshape(depth, 3, D_CONV)

    xs = x.reshape(t, d)
    xn = _rmsnorm_fwd(xs, norm_ffn1[0][None], name="first_norm")
    saved = []
    for l in range(depth):
        wbd = _block_diag(w_pool[l])
        p = dict(norm=norm_mix[l][None], bf=jnp.pad(b_forget[l], (0, LANES - N_HEADS))[None],
                 wbd=wbd.astype(BF16), wbd_t=wbd.T.astype(BF16), scale=pool_scale[l][None],
                 cw=_row_pad(cw_all[l], 8))
        w = dict(f1=_ffn_weights(wg_head, 0))
        if l == 0:
            xs, xn, s1, wg_tail = _ffn_forward(xs, xn, w["f1"], norm_mix[l][None], carry_gather=pack0[PACK_HEAD:])
        else:
            xs, xn, s1 = _ffn_forward(xs, xn, w["f1"], norm_mix[l][None])
        w["mix"], w["f2"] = _tail_weights(wg_tail)
        next_pack = _pack_shards(shards, l + 1, BF16) if l + 1 < depth else None
        xs, xn, sm, wg = _mixer_forward(xs, xn, p, w["mix"], nb, seq, ta, norm_ffn2[l][None], next_pack)
        if wg is not None:
            wg_head, wg_tail = wg[:, :PACK_HEAD], wg[:, PACK_HEAD:]
        xs, xn, s2 = _ffn_forward(xs, xn, w["f2"], norm_ffn1[l + 1][None] if l + 1 < depth else None)
        saved.append((w, p, s1, sm, s2))

    dx, dx_b, g_norm_final, loss_part = _final_loss_bwd(xs, norm_final[None], loss_target.reshape(t, d))
    layer_g = [None] * depth
    small = [None] * depth
    chip_sums = None
    for l in reversed(range(depth)):
        w, p, s1, sm, s2 = saved[l]
        dx, dx_b, dg2, gbuf, _ = _ffn_backward(dx, dx_b, norm_ffn2[l][None], w["f2"], s2, None, "f2")
        dx, dx_b, gm, gbuf, landed = _mixer_backward(dx, dx_b, p, w["mix"], sm, nb, seq, ta, gbuf, chip_sums)
        if chip_sums is not None:
            layer_g[l + 1] = _sum_landed(chip_sums, landed)
        dx, dx_b, dg1, gbuf, theirs = _ffn_backward(dx, dx_b, norm_ffn1[l][None], w["f1"], s1, gbuf, "f1", exchange=True)
        chip_sums = _add_sibling(gbuf, theirs)
        small[l] = dict(n1=dg1, nm=gm["norm"], n2=dg2, bf=gm["bf"], wbd=gm["wbd"], scale=gm["scale"], cw=gm["cw"])
    grad_x = dx.reshape(nb, seq, d)

    def tile8(a):
        return jnp.pad(a, ((0, 8 - a.shape[0]), (0, D_MODEL - a.shape[1])))

    rows = []
    for l in range(depth):
        s = small[l]
        wp_rows = jnp.stack([s["wbd"][POOL_GROUP * g:POOL_GROUP * (g + 1), POOL_GROUP * g:POOL_GROUP * (g + 1)]
                             for g in range(4)]).reshape(16, D_MODEL)
        rows += [tile8(s["n1"]), tile8(s["nm"]), tile8(s["n2"]), tile8(s["bf"]), tile8(s["scale"]), tile8(s["cw"]),
                 wp_rows]
    rows += [tile8(g_norm_final), tile8(loss_part)]
    small_gathered, landed = _comm_call("exchange_grads_chips_gather_small", gathers=[jnp.concatenate(rows, axis=0)],
                                        chips=chip_sums)
    layer_g[0] = _sum_landed(chip_sums, landed)

    pieces = {}
    for nm, n in (("f1i", FFN_ROWS), ("f1o", OUT_ROWS), ("mi", MIX_ROWS), ("mo", MO_ROWS), ("f2i", FFN_ROWS),
                  ("f2o", OUT_ROWS)):
        pieces[nm] = jnp.stack([g[GRAD_AT[nm]:GRAD_AT[nm] + n] for g in layer_g])
    g_sharded = dict(
        w_ffn1_in=pieces["f1i"].transpose(0, 2, 1), w_ffn1_out=pieces["f1o"],
        w_mix_in=pieces["mi"].transpose(0, 2, 1), w_mix_out=pieces["mo"],
        w_ffn2_in=pieces["f2i"].transpose(0, 2, 1), w_ffn2_out=pieces["f2o"])

    per_layer = 6 * 8 + 16
    small_sum = _sum_slots(small_gathered, name="sum_small_grads")
    lay = small_sum[:depth * per_layer].reshape(depth, per_layer, D_MODEL)
    g_small = dict(
        norm_ffn1=lay[:, 0], norm_mix=lay[:, 8], norm_ffn2=lay[:, 16], b_forget=lay[:, 24, :N_HEADS],
        pool_scale=lay[:, 32, :D_POOL],
        conv_w=lax.dynamic_slice_in_dim(lay[:, 40:43, :D_CONV], my_id * cshard, cshard, axis=2),
        w_pool=lay[:, 48:64].reshape(depth, 4, POOL_GROUP, POOL_GROUP),
        norm_final=small_sum[depth * per_layer])
    loss = small_sum[depth * per_layer + 8, 0]

    given = dict(norm_ffn1=(norm_ffn1, m_norm_ffn1, v_norm_ffn1), w_ffn1_in=(w_ffn1_in, m_w_ffn1_in, v_w_ffn1_in),
                 w_ffn1_out=(w_ffn1_out, m_w_ffn1_out, v_w_ffn1_out), norm_mix=(norm_mix, m_norm_mix, v_norm_mix),
                 w_mix_in=(w_mix_in, m_w_mix_in, v_w_mix_in), b_forget=(b_forget, m_b_forget, v_b_forget),
                 w_pool=(w_pool, m_w_pool, v_w_pool), pool_scale=(pool_scale, m_pool_scale, v_pool_scale),
                 conv_w=(conv_w, m_conv_w, v_conv_w), w_mix_out=(w_mix_out, m_w_mix_out, v_w_mix_out),
                 norm_ffn2=(norm_ffn2, m_norm_ffn2, v_norm_ffn2), w_ffn2_in=(w_ffn2_in, m_w_ffn2_in, v_w_ffn2_in),
                 w_ffn2_out=(w_ffn2_out, m_w_ffn2_out, v_w_ffn2_out), norm_final=(norm_final, m_norm_final, v_norm_final))
    names = list(given)
    grads, deltas, new_m, new_v = {}, {}, {}, {}
    for nm in names:
        wv, mv, vv = given[nm]
        gv = (g_sharded[nm] if nm in g_sharded else g_small[nm]).reshape(wv.shape)
        shape2 = (-1, wv.shape[-1]) if wv.ndim > 1 else (1, wv.shape[0])
        dl, mn, vn = _adamw(wv.reshape(shape2), gv.reshape(shape2), mv.reshape(shape2), vv.reshape(shape2),
                            name="adamw_" + nm)
        grads[nm], deltas[nm], new_m[nm], new_v[nm] = gv, dl.reshape(wv.shape), mn.reshape(wv.shape), vn.reshape(wv.shape)
    return (loss, grad_x, *[grads[n] for n in names], *[deltas[n] for n in names],
            *[new_m[n] for n in names], *[new_v[n] for n in names])
```

```python
import functools

import jax
import jax.numpy as jnp
from jax import lax
from jax.experimental import pallas as pl
from jax.experimental.pallas import tpu as pltpu

F32 = jnp.float32
BF16 = jnp.bfloat16

D_MODEL = 1024
D_FF = 2816
HEAD_DIM = 64
N_HEADS = 8
N_PAIRS = N_HEADS // 2
D_ATTN = 512
D_POOL = 256
D_CONV = 256
POOL_WINDOWS = (2, 4, 8, 16)
POOL_GROUP = 64
D_IN = 2568
RMS_EPS = 1e-6
ADAM_LR, ADAM_B1, ADAM_B2, ADAM_EPS, ADAM_WD, ADAM_STEP = 0.001, 0.9, 0.999, 1e-08, 0.01, 10

N_DEV = 8
N_CHIPS = 4
LANES = 128
VMEM_BYTES_V7X = 64 * 1024 * 1024
VMEM_LIMIT_MAX = VMEM_BYTES_V7X - 8 * 1024 * 1024

F_HALF = D_FF // 2
D_QKV = 3 * D_ATTN
D_REST = D_POOL + 3 * D_CONV
D_INP = D_QKV + D_REST + LANES
MIX_ROWS = 321
MIX_ROWS_PAD = 336
FFN_ROWS = 704
OUT_ROWS = 352
MO_ROWS = 128
LAYER_ROWS = 2 * (FFN_ROWS + OUT_ROWS) + MIX_ROWS_PAD + MO_ROWS
NEG_BIG = -1e30
ATT_SCALE = HEAD_DIM ** -0.5
ATT_K = 2 * LANES
ATT_TILE = 256
ATT_PAIRS_FWD = 4
ATT_PAIRS_BWD = 4
MXU_COLS = 256


def _cparams(sem, vmem_bytes):
    limit = int(min(max(vmem_bytes, 16 * 1024 * 1024), VMEM_LIMIT_MAX))
    return pltpu.CompilerParams(dimension_semantics=sem, vmem_limit_bytes=limit)


def _nbytes(shape, dtype):
    n = 1
    for s in shape:
        n *= s
    return n * jnp.dtype(dtype).itemsize


def _pick(n, prefs):
    for p in prefs:
        if n % p == 0:
            return p
    return n


def _rmsnorm_fwd(x, g, name):
    t, d = x.shape
    tm = _pick(t, (512, 256, 128))

    def body(x_ref, g_ref, o_ref):
        xv = x_ref[...]
        r = lax.rsqrt(jnp.mean(xv * xv, axis=-1, keepdims=True) + RMS_EPS)
        o_ref[...] = ((xv * r) * g_ref[...]).astype(o_ref.dtype)

    return pl.pallas_call(
        body, grid=(t // tm,),
        in_specs=[pl.BlockSpec((tm, d), lambda i: (i, 0)), pl.BlockSpec((1, d), lambda i: (0, 0))],
        out_specs=pl.BlockSpec((tm, d), lambda i: (i, 0)),
        out_shape=jax.ShapeDtypeStruct((t, d), BF16), name=name,
        compiler_params=_cparams(("parallel",), 6 * tm * d * 4),
    )(x, g)


def _mm_nn(a, b, *, out_dtype, name, res=None, alpha=1.0, tn=None, next_gain=None):
    m, k = a.shape
    n = b.shape[1]
    tn = n if tn is None else tn
    tm = _pick(m, (512, 256, 128))
    with_res = res is not None
    with_norm = next_gain is not None
    assert not with_norm or tn == n

    def body(*refs):
        refs = list(refs)
        a_ref, b_ref = refs[:2]
        r_ref = refs[2] if with_res else None
        g_ref = refs[2 + with_res] if with_norm else None
        o_ref = refs[2 + with_res + with_norm]
        acc = jnp.dot(a_ref[...], b_ref[...], preferred_element_type=F32)
        if with_res:
            acc = r_ref[...] + alpha * acc
        o_ref[...] = acc.astype(o_ref.dtype)
        if with_norm:
            r = lax.rsqrt(jnp.mean(acc * acc, axis=-1, keepdims=True) + RMS_EPS)
            refs[-1][...] = ((acc * r) * g_ref[...]).astype(BF16)

    in_specs = [pl.BlockSpec((tm, k), lambda j, i: (i, 0)), pl.BlockSpec((k, tn), lambda j, i: (0, j))]
    args = [a, b]
    out_blk = pl.BlockSpec((tm, tn), lambda j, i: (i, j))
    out_specs, out_shape = [out_blk], [jax.ShapeDtypeStruct((m, n), out_dtype)]
    if with_res:
        in_specs.append(out_blk)
        args.append(res)
    if with_norm:
        in_specs.append(pl.BlockSpec((1, n), lambda j, i: (0, 0)))
        args.append(next_gain)
        out_specs.append(out_blk)
        out_shape.append(jax.ShapeDtypeStruct((m, n), BF16))
    vmem = 2 * (_nbytes((tm, k), BF16) + _nbytes((k, tn), BF16) + 4 * _nbytes((tm, tn), F32))
    outs = pl.pallas_call(
        body, grid=(n // tn, m // tm), in_specs=in_specs, out_specs=out_specs, out_shape=out_shape, name=name,
        compiler_params=_cparams(("parallel", "parallel"), vmem),
    )(*args)
    return outs if with_norm else outs[0]


def _mm_nt(a, b_t, *, out_dtype, name):
    m, k = a.shape
    n = b_t.shape[0]
    tm = _pick(m, (512, 256, 128))

    def body(a_ref, b_ref, o_ref):
        o_ref[...] = _nt(a_ref[...], b_ref[...]).astype(o_ref.dtype)

    vmem = 2 * (_nbytes((tm, k), BF16) + _nbytes((n, k), BF16) + 3 * _nbytes((tm, n), F32))
    return pl.pallas_call(
        body, grid=(m // tm,),
        in_specs=[pl.BlockSpec((tm, k), lambda i: (i, 0)), pl.BlockSpec((n, k), lambda i: (0, 0))],
        out_specs=pl.BlockSpec((tm, n), lambda i: (i, 0)), out_shape=jax.ShapeDtypeStruct((m, n), out_dtype),
        name=name, compiler_params=_cparams(("parallel",), vmem),
    )(a, b_t)


def _mix_proj(xn, wm_t):
    t, d = xn.shape
    tm = _pick(t, (512, 256, 128))

    def body(x_ref, w_ref, qkv_ref, rest_ref, f_ref):
        xv = x_ref[...]
        for c0, cw in _col_chunks(D_QKV, MXU_COLS):
            qkv_ref[:, c0:c0 + cw] = _nt(xv, w_ref[c0:c0 + cw, :]).astype(qkv_ref.dtype)
        for c0, cw in _col_chunks(D_REST, MXU_COLS):
            rest_ref[:, c0:c0 + cw] = _nt(xv, w_ref[D_QKV + c0:D_QKV + c0 + cw, :])
        f_ref[...] = _nt(xv, w_ref[D_QKV + D_REST:, :])

    def rows(n):
        return pl.BlockSpec((tm, n), lambda i: (i, 0))

    vmem = 2 * (_nbytes((tm, d), BF16) + _nbytes((D_INP, d), BF16) + 3 * _nbytes((tm, D_INP), F32))
    return pl.pallas_call(
        body, grid=(t // tm,), in_specs=[rows(d), pl.BlockSpec((D_INP, d), lambda i: (0, 0))],
        out_specs=[rows(D_QKV), rows(D_REST), rows(LANES)],
        out_shape=[jax.ShapeDtypeStruct((t, D_QKV), BF16), jax.ShapeDtypeStruct((t, D_REST), F32),
                   jax.ShapeDtypeStruct((t, LANES), F32)],
        name="mix_proj", compiler_params=_cparams(("parallel",), vmem),
    )(xn, wm_t)


def _mm_tn(a, b, *, name, alpha=1.0, tm=None, into=None):
    t, m = a.shape
    n = b.shape[1]
    tm = m if tm is None else tm
    tk = _pick(t, (2048, 1024, 512, 256, 128))
    nk = t // tk

    def body(a_ref, b_ref, *rest):
        o_ref = rest[-1]
        kk = pl.program_id(1)
        p = lax.dot_general(a_ref[...], b_ref[...], (((0,), (0,)), ((), ())), preferred_element_type=F32)
        if alpha != 1.0:
            p = alpha * p
        p = p.reshape(o_ref.shape)

        @pl.when(kk == 0)
        def _():
            o_ref[...] = p

        @pl.when(kk > 0)
        def _():
            o_ref[...] += p

    vmem = 2 * (_nbytes((tk, tm), BF16) + _nbytes((tk, n), BF16) + 2 * _nbytes((tm, n), F32))
    in_specs = [pl.BlockSpec((tk, tm), lambda i, kk: (kk, i)), pl.BlockSpec((tk, n), lambda i, kk: (kk, 0))]
    cp = _cparams(("parallel", "arbitrary"), vmem)
    if into is None:
        return pl.pallas_call(
            body, grid=(m // tm, nk), in_specs=in_specs, out_specs=pl.BlockSpec((tm, n), lambda i, kk: (i, 0)),
            out_shape=jax.ShapeDtypeStruct((m, n), F32), name=name, compiler_params=cp,
        )(a, b)
    buf, buf_shape, blk, index = into
    out_spec = pl.BlockSpec(blk, lambda i, kk: index(i))
    out_shape = jax.ShapeDtypeStruct(buf_shape, F32)
    if buf is None:
        return pl.pallas_call(body, grid=(m // tm, nk), in_specs=in_specs, out_specs=out_spec, out_shape=out_shape,
                              name=name + "_new", compiler_params=cp)(a, b)
    return pl.pallas_call(
        body, grid=(m // tm, nk), in_specs=in_specs + [pl.BlockSpec(memory_space=pl.ANY)], out_specs=out_spec,
        out_shape=out_shape, input_output_aliases={2: 0}, name=name + "_into", compiler_params=cp,
    )(a, b, buf)


def _sigmoid(v):
    return 1.0 / (1.0 + jnp.exp(-v))


def _col_chunks(n, width):
    return [(c, min(width, n - c)) for c in range(0, n, width)]


def _ffn_in(xn, w_t, name, carry_gather=None):
    t, d = xn.shape
    tm = _pick(t, (1024, 512, 256, 128))
    grid = (2, t // tm)

    def body(x_ref, wg_ref, wu_ref, *rest):
        if carry_gather is None:
            h_ref, pg_ref, pu_ref = rest
        else:
            blk_ref, h_ref, pg_ref, pu_ref, gathered_ref, send_sems, recv_sems = rest
            first_step, last_step = _grid_ends([pl.program_id(a) for a in range(2)], grid)

            @pl.when(first_step)
            def _():
                _gather_start(blk_ref, gathered_ref, send_sems, recv_sems)

        xv = x_ref[...]
        for c0, cw in _col_chunks(F_HALF, MXU_COLS):
            cols = slice(c0, c0 + cw)
            g = _nt(xv, wg_ref[cols, :])
            u = _nt(xv, wu_ref[cols, :])
            s = _sigmoid(g)
            silu = g * s
            h_ref[:, cols] = (silu * u).astype(h_ref.dtype)
            pg_ref[:, cols] = (u * (s * (1.0 + g * (1.0 - s)))).astype(pg_ref.dtype)
            pu_ref[:, cols] = silu.astype(pu_ref.dtype)

        if carry_gather is not None:
            @pl.when(last_step)
            def _():
                _gather_finish(blk_ref, gathered_ref, send_sems, recv_sems)

    vmem = 2 * (_nbytes((tm, d), BF16) + 2 * _nbytes((d, F_HALF), BF16) + 4 * _nbytes((tm, D_FF), F32))
    out_blk = pl.BlockSpec((tm, F_HALF), lambda j, i: (i, j))
    sds = jax.ShapeDtypeStruct((t, D_FF), BF16)
    in_specs = [pl.BlockSpec((tm, d), lambda j, i: (i, 0)), pl.BlockSpec((F_HALF, d), lambda j, i: (j, 0)),
                pl.BlockSpec((F_HALF, d), lambda j, i: (2 + j, 0))]
    if carry_gather is None:
        return pl.pallas_call(
            body, grid=grid, in_specs=in_specs, out_specs=[out_blk, out_blk, out_blk], out_shape=[sds, sds, sds],
            name=name, compiler_params=_cparams(("parallel", "parallel"), vmem),
        )(xn, w_t, w_t)
    any_spec = pl.BlockSpec(memory_space=pl.ANY)
    return pl.pallas_call(
        body, grid=grid, in_specs=in_specs + [any_spec], out_specs=[out_blk, out_blk, out_blk, any_spec],
        out_shape=[sds, sds, sds, jax.ShapeDtypeStruct((N_DEV,) + carry_gather.shape, carry_gather.dtype)],
        scratch_shapes=list(GATHER_SEMS), name=name + "_gather",
        compiler_params=_cparams(("arbitrary", "arbitrary"), vmem),
    )(xn, w_t, w_t, carry_gather)


def _ffn_bwd_mid(dxo, w_out, pg, pu, name):
    t, d = dxo.shape
    tm = _pick(t, (1024, 512, 256, 128))

    def body(d_ref, w_ref, pg_ref, pu_ref, dg_ref, du_ref):
        dv = d_ref[...]
        for c0, cw in _col_chunks(F_HALF, MXU_COLS):
            cols = slice(c0, c0 + cw)
            dh = 0.5 * _nt(dv, w_ref[cols, :])
            dg_ref[:, cols] = (dh * pg_ref[:, cols].astype(F32)).astype(dg_ref.dtype)
            du_ref[:, cols] = (dh * pu_ref[:, cols].astype(F32)).astype(du_ref.dtype)

    vmem = 2 * (_nbytes((tm, d), BF16) + _nbytes((d, F_HALF), BF16) + 5 * _nbytes((tm, D_FF), F32))
    blk = pl.BlockSpec((tm, F_HALF), lambda j, i: (i, j))
    sds = jax.ShapeDtypeStruct((t, D_FF), BF16)
    return pl.pallas_call(
        body, grid=(2, t // tm),
        in_specs=[pl.BlockSpec((tm, d), lambda j, i: (i, 0)), pl.BlockSpec((F_HALF, d), lambda j, i: (j, 0)), blk, blk],
        out_specs=[blk, blk], out_shape=[sds, sds], name=name,
        compiler_params=_cparams(("parallel", "parallel"), vmem),
    )(dxo, w_out, pg, pu)


def _dxn_norm_bwd(parts, b, x, g, dxo, name, carry_sibling=None):
    t, d = x.shape
    k = parts[0].shape[1]
    n_parts = len(parts)
    tm = _pick(t, (256, 128))
    grid = (t // tm,)

    def body(*refs):
        a_refs, b_refs = refs[:n_parts], refs[n_parts:2 * n_parts]
        if carry_sibling is None:
            x_ref, g_ref, do_ref, dx_ref, dxb_ref, dg_ref = refs[2 * n_parts:]
        else:
            x_ref, g_ref, do_ref, g4_ref, dx_ref, dxb_ref, dg_ref, theirs_ref, send_sems, recv_sems = refs[2 * n_parts:]
            first_step, last_step = _grid_ends([pl.program_id(0)], grid)

            @pl.when(first_step)
            def _():
                _sibling_start(g4_ref, theirs_ref, send_sems, recv_sems)

        i = pl.program_id(0)
        dn = jnp.dot(a_refs[0][...], b_refs[0][...], preferred_element_type=F32)
        for a_ref, b_ref in zip(a_refs[1:], b_refs[1:]):
            dn = dn + jnp.dot(a_ref[...], b_ref[...], preferred_element_type=F32)
        xv = x_ref[...]
        r = lax.rsqrt(jnp.mean(xv * xv, axis=-1, keepdims=True) + RMS_EPS)
        xh = xv * r
        dgp = jnp.sum(dn * xh, axis=0, keepdims=True)
        dh = dn * g_ref[...]
        dx = do_ref[...] + r * (dh - xh * jnp.mean(dh * xh, axis=-1, keepdims=True))
        dx_ref[...] = dx
        dxb_ref[...] = dx.astype(dxb_ref.dtype)

        @pl.when(i == 0)
        def _():
            dg_ref[...] = dgp

        @pl.when(i > 0)
        def _():
            dg_ref[...] += dgp

        if carry_sibling is not None:
            @pl.when(last_step)
            def _():
                _sibling_finish(g4_ref, theirs_ref, send_sems, recv_sems)

    blk = pl.BlockSpec((tm, d), lambda i: (i, 0))
    row = pl.BlockSpec((1, d), lambda i: (0, 0))
    a_specs = [pl.BlockSpec((tm, k), lambda i: (i, 0)) for _ in parts]
    b_specs = [pl.BlockSpec((k, d), lambda i, kk=kk: (kk, 0)) for kk in range(n_parts)]
    vmem = 2 * n_parts * (_nbytes((tm, k), BF16) + _nbytes((k, d), BF16)) + 16 * tm * d * 4
    in_specs = a_specs + b_specs + [blk, row, blk]
    out_shape = [jax.ShapeDtypeStruct((t, d), F32), jax.ShapeDtypeStruct((t, d), BF16), jax.ShapeDtypeStruct((1, d), F32)]
    args = (*parts, *([b] * n_parts), x, g, dxo)
    if carry_sibling is None:
        return pl.pallas_call(body, grid=grid, in_specs=in_specs, out_specs=[blk, blk, row], out_shape=out_shape,
                              name=name, compiler_params=_cparams(("arbitrary",), vmem))(*args)
    nchip, _, r, c = carry_sibling.shape
    any_spec = pl.BlockSpec(memory_space=pl.ANY)
    return pl.pallas_call(
        body, grid=grid, in_specs=in_specs + [any_spec], out_specs=[blk, blk, row, any_spec],
        out_shape=out_shape + [jax.ShapeDtypeStruct((nchip, r, c), carry_sibling.dtype)],
        scratch_shapes=list(SIBLING_SEMS), name=name + "_exchange", compiler_params=_cparams(("arbitrary",), vmem),
    )(*args, carry_sibling)


def _final_loss_bwd(x, g, tgt):
    t, d = x.shape
    tm = _pick(t, (512, 256, 128))

    def body(x_ref, g_ref, t_ref, dx_ref, dxb_ref, dg_ref, loss_ref):
        i = pl.program_id(0)
        xv = x_ref[...]
        r = lax.rsqrt(jnp.mean(xv * xv, axis=-1, keepdims=True) + RMS_EPS)
        xh = xv * r
        gv = g_ref[...]
        err = xh * gv - t_ref[...]
        lp = 0.5 * jnp.sum(jnp.mean(err * err, axis=-1, keepdims=True), axis=0, keepdims=True)
        dy = err * (1.0 / d)
        dgp = jnp.sum(dy * xh, axis=0, keepdims=True)
        dh = dy * gv
        dx = r * (dh - xh * jnp.mean(dh * xh, axis=-1, keepdims=True))
        dx_ref[...] = dx
        dxb_ref[...] = dx.astype(dxb_ref.dtype)
        lpb = jnp.broadcast_to(lp, (1, LANES))

        @pl.when(i == 0)
        def _():
            dg_ref[...] = dgp
            loss_ref[...] = lpb

        @pl.when(i > 0)
        def _():
            dg_ref[...] += dgp
            loss_ref[...] += lpb

    blk = pl.BlockSpec((tm, d), lambda i: (i, 0))
    row = pl.BlockSpec((1, d), lambda i: (0, 0))
    return pl.pallas_call(
        body, grid=(t // tm,), in_specs=[blk, row, blk],
        out_specs=[blk, blk, row, pl.BlockSpec((1, LANES), lambda i: (0, 0))],
        out_shape=[jax.ShapeDtypeStruct((t, d), F32), jax.ShapeDtypeStruct((t, d), BF16),
                   jax.ShapeDtypeStruct((1, d), F32), jax.ShapeDtypeStruct((1, LANES), F32)], name="final_loss_bwd",
        compiler_params=_cparams(("arbitrary",), 16 * tm * d * 4),
    )(x, g, tgt)


def _seq_scan(v, seq, reverse):
    row = lax.broadcasted_iota(jnp.int32, v.shape, 0)
    k = 1
    while k < seq:
        if reverse:
            v = v + jnp.where(row < seq - k, pltpu.roll(v, seq - k, 0), 0.0)
        else:
            v = v + jnp.where(row >= k, pltpu.roll(v, k, 0), 0.0)
        k *= 2
    return v


def _log_sigmoid(v):
    return jnp.minimum(v, 0.0) - jnp.log(1.0 + jnp.exp(-jnp.abs(v)))


def _fox_prep(fl, bf, qkv, nb, seq):
    def body(f_ref, b_ref, q_ref, k_ref, v_ref, qa_ref, ka_ref, vm_ref):
        dsum = _seq_scan(_log_sigmoid(f_ref[...] + b_ref[...]), seq, False)
        d1 = dsum.astype(BF16).astype(F32)
        r1 = dsum - d1
        d2 = r1.astype(BF16).astype(F32)
        d3 = (r1 - d2).astype(BF16).astype(F32)
        lane = lax.broadcasted_iota(jnp.int32, (seq, LANES), 1)
        first = lane < HEAD_DIM
        l64 = jnp.where(first, lane, lane - HEAD_DIM)
        for p in range(N_PAIRS):
            def head_cols(a, p=p):
                return jnp.where(first, a[:, 2 * p:2 * p + 1], a[:, 2 * p + 1:2 * p + 2])

            e1, e2, e3 = head_cols(d1), head_cols(d2), head_cols(d3)
            aux_q = jnp.where(l64 == 0, e1, jnp.where(l64 == 1, e2, jnp.where(l64 == 2, e3,
                              jnp.where(l64 < 6, 1.0, 0.0)))).astype(BF16)
            aux_k = jnp.where(l64 < 3, 1.0, jnp.where(l64 == 3, -e1, jnp.where(l64 == 4, -e2,
                              jnp.where(l64 == 5, -e3, 0.0)))).astype(BF16)
            cols = slice(LANES * p, LANES * (p + 1))
            qs = q_ref[:, cols] * ATT_SCALE
            vp = v_ref[:, cols]
            zero = jnp.zeros_like(qs)
            qa_ref[0, p, :, :LANES] = qs
            qa_ref[0, p, :, LANES:] = aux_q
            ka_ref[0, p, :, :LANES] = k_ref[:, cols]
            ka_ref[0, p, :, LANES:] = aux_k
            vm_ref[0, p, 0] = jnp.where(first, vp, zero)
            vm_ref[0, p, 1] = jnp.where(first, zero, vp)

    def part(c):
        return pl.BlockSpec((seq, D_ATTN), lambda b, c=c: (b, c))

    return pl.pallas_call(
        body, grid=(nb,),
        in_specs=[pl.BlockSpec((seq, LANES), lambda b: (b, 0)), pl.BlockSpec((1, LANES), lambda b: (0, 0)),
                  part(0), part(1), part(2)],
        out_specs=[pl.BlockSpec((1, N_PAIRS, seq, ATT_K), lambda b: (b, 0, 0, 0)),
                   pl.BlockSpec((1, N_PAIRS, seq, ATT_K), lambda b: (b, 0, 0, 0)),
                   pl.BlockSpec((1, N_PAIRS, 2, seq, LANES), lambda b: (b, 0, 0, 0, 0))],
        out_shape=[jax.ShapeDtypeStruct((nb, N_PAIRS, seq, ATT_K), BF16),
                   jax.ShapeDtypeStruct((nb, N_PAIRS, seq, ATT_K), BF16),
                   jax.ShapeDtypeStruct((nb, N_PAIRS, 2, seq, LANES), BF16)],
        name="fox_prep", compiler_params=_cparams(("parallel",), 48 * 1024 * 1024),
    )(fl, bf, qkv, qkv, qkv)


def _fox_prep_bwd(dd, fl, bf, seq):
    t = fl.shape[0]

    def body(d_ref, f_ref, b_ref, o_ref, db_ref):
        i = pl.program_id(0)
        dlog = _seq_scan(d_ref[...], seq, True)
        dfl = dlog * _sigmoid(-(f_ref[...] + b_ref[...]))
        o_ref[...] = dfl.astype(o_ref.dtype)
        dbp = jnp.sum(dfl, axis=0, keepdims=True)

        @pl.when(i == 0)
        def _():
            db_ref[...] = dbp

        @pl.when(i > 0)
        def _():
            db_ref[...] += dbp

    blk = pl.BlockSpec((seq, LANES), lambda b: (b, 0))
    row = pl.BlockSpec((1, LANES), lambda b: (0, 0))
    return pl.pallas_call(
        body, grid=(t // seq,), in_specs=[blk, blk, row], out_specs=[blk, row],
        out_shape=[jax.ShapeDtypeStruct((t, LANES), BF16), jax.ShapeDtypeStruct((1, LANES), F32)], name="fox_prep_bwd",
        compiler_params=_cparams(("arbitrary",), 24 * seq * LANES * 4),
    )(dd, fl, bf)


def _stack_heads(qc):
    lane = lax.broadcasted_iota(jnp.int32, qc.shape, 1)
    first = (lane & HEAD_DIM) == 0
    zero = jnp.zeros_like(qc)
    return jnp.concatenate([jnp.where(first, qc, zero), jnp.where(first, zero, qc)], axis=0)


def _pair_rows(a, ta):
    lane = lax.broadcasted_iota(jnp.int32, (ta, LANES), 1)
    return jnp.where(lane < HEAD_DIM, a[:ta], a[ta:])


def _diag_mask(ta):
    r = lax.broadcasted_iota(jnp.int32, (2 * ta, ta), 0)
    c = lax.broadcasted_iota(jnp.int32, (2 * ta, ta), 1)
    return c <= jnp.where(r >= ta, r - ta, r)


def _nt(a, b):
    return lax.dot_general(a, b, (((1,), (1,)), ((), ())), preferred_element_type=F32)


def _tn(a, b):
    return lax.dot_general(a, b, (((0,), (0,)), ((), ())), preferred_element_type=F32)


def _grid_ends(ids, sizes):
    first = functools.reduce(jnp.logical_and, [i == 0 for i in ids])
    last = functools.reduce(jnp.logical_and, [i == n - 1 for i, n in zip(ids, sizes)])
    return first, last


def _grid_step_is(ids, sizes, step):
    linear = ids[0]
    for i, n in zip(ids[1:], sizes[1:]):
        linear = linear * n + i
    return linear == step


GATHER_RELAY_LEAD = 3


def _fox_fwd(qa, ka, vm, nb, seq, ta, carry_gather=None):
    nq = seq // ta
    npp = ATT_PAIRS_FWD
    grid = (nb, N_PAIRS // npp, nq)

    def body(q_ref, k_ref, v_ref, *rest):
        if carry_gather is None:
            o_ref, lse_ref = rest
        else:
            x_ref, o_ref, lse_ref, gathered_ref, send_sems, recv_sems = rest
            first_step, last_step = _grid_ends([pl.program_id(a) for a in range(3)], grid)

            @pl.when(first_step)
            def _():
                _gather_start(x_ref, gathered_ref, send_sems, recv_sems)

        i = pl.program_id(2)
        q2s = [_stack_heads(q_ref[0, pp]) for pp in range(npp)]

        def step(j, carry, masked):
            rows = pl.ds(pl.multiple_of(j * ta, ta), ta)
            out = []
            for pp in range(npp):
                m, l, acc = carry[pp]
                s = _nt(q2s[pp], k_ref[0, pp, rows, :])
                if masked:
                    s = jnp.where(_diag_mask(ta), s, NEG_BIG)
                m_new = jnp.maximum(m, jnp.max(s, axis=-1, keepdims=True))
                p = jnp.exp(s - m_new)
                corr = jnp.exp(m - m_new)
                l = corr * l + jnp.sum(p, axis=-1, keepdims=True)
                pb = p.astype(BF16)
                pv = (jnp.dot(pb[:ta], v_ref[0, pp, 0, rows, :], preferred_element_type=F32)
                      + jnp.dot(pb[ta:], v_ref[0, pp, 1, rows, :], preferred_element_type=F32))
                out.append((m_new, l, _pair_rows(corr, ta) * acc + pv))
            return tuple(out)

        init = tuple((jnp.full((2 * ta, 1), NEG_BIG, F32), jnp.zeros((2 * ta, 1), F32),
                      jnp.zeros((ta, LANES), F32)) for _ in range(npp))
        carry = lax.fori_loop(0, i, functools.partial(step, masked=False), init)
        for pp, (m, l, acc) in enumerate(step(i, carry, True)):
            o_ref[:, LANES * pp:LANES * (pp + 1)] = (acc * _pair_rows(1.0 / l, ta)).astype(o_ref.dtype)
            lse = m + jnp.log(l)
            lse_ref[0, pp, 0] = lse[:ta]
            lse_ref[0, pp, 1] = lse[ta:]

        if carry_gather is not None:
            n_steps = grid[0] * grid[1] * grid[2]
            relay_at = n_steps - 1 - min(GATHER_RELAY_LEAD, n_steps - 1)

            @pl.when(_grid_step_is([pl.program_id(a) for a in range(3)], grid, relay_at))
            def _():
                _gather_relay(x_ref, gathered_ref, send_sems, recv_sems)

            @pl.when(last_step)
            def _():
                _gather_drain(x_ref, gathered_ref, send_sems, recv_sems)

    vmem = (2 * npp * (_nbytes((seq, ATT_K), BF16) + 2 * _nbytes((seq, LANES), BF16)) + 24 * npp * ta * ta * 4
            + 8 * 1024 * 1024)
    in_specs = [pl.BlockSpec((1, npp, ta, ATT_K), lambda b, g, i: (b, g, i, 0)),
                pl.BlockSpec((1, npp, seq, ATT_K), lambda b, g, i: (b, g, 0, 0)),
                pl.BlockSpec((1, npp, 2, seq, LANES), lambda b, g, i: (b, g, 0, 0, 0))]
    out_specs = [pl.BlockSpec((ta, LANES * npp), lambda b, g, i: (b * nq + i, g)),
                 pl.BlockSpec((1, npp, 2, ta, 1), lambda b, g, i: (b, g, 0, i, 0))]
    out_shape = [jax.ShapeDtypeStruct((nb * seq, D_ATTN), BF16), jax.ShapeDtypeStruct((nb, N_PAIRS, 2, seq, 1), F32)]
    if carry_gather is None:
        return pl.pallas_call(
            body, grid=grid, in_specs=in_specs, out_specs=out_specs, out_shape=out_shape, name="fox_fwd",
            compiler_params=_cparams(("parallel", "parallel", "parallel"), vmem),
        )(qa, ka, vm)
    any_spec = pl.BlockSpec(memory_space=pl.ANY)
    return pl.pallas_call(
        body, grid=grid, in_specs=in_specs + [any_spec], out_specs=out_specs + [any_spec],
        out_shape=out_shape + [jax.ShapeDtypeStruct((N_DEV,) + carry_gather.shape, carry_gather.dtype)],
        scratch_shapes=list(GATHER_SEMS), name="fox_fwd_gather",
        compiler_params=_cparams(("arbitrary", "arbitrary", "arbitrary"), vmem),
    )(qa, ka, vm, carry_gather)


def _fox_bwd(qa, ka, vm, y, dy, lse, nb, seq, ta, carry_exchange=None):
    nq = seq // ta
    npp = ATT_PAIRS_BWD
    grid = (nb, N_PAIRS // npp, nq)

    def body(q_ref, k_ref, v_ref, o_ref, do_ref, lse_ref, *rest):
        if carry_exchange is None:
            dq_ref, dk_ref, dv_ref, rs_ref, cs_ref = rest
        else:
            t_ref, dq_ref, dk_ref, dv_ref, rs_ref, cs_ref, landed_ref, send_sems, recv_sems = rest
            first_step, last_step = _grid_ends([pl.program_id(a) for a in range(3)], grid)

            @pl.when(first_step)
            def _():
                _chips_start(t_ref, landed_ref, send_sems, recv_sems)

        i = pl.program_id(2)

        @pl.when(i == 0)
        def _():
            dk_ref[...] = jnp.zeros_like(dk_ref)
            dv_ref[...] = jnp.zeros_like(dv_ref)
            cs_ref[...] = jnp.zeros_like(cs_ref)

        first = lax.broadcasted_iota(jnp.int32, (ta, LANES), 1) < HEAD_DIM
        q2s, do2s, deltas, lses = [], [], [], []
        for pp in range(npp):
            cols = slice(LANES * pp, LANES * (pp + 1))
            q2s.append(_stack_heads(q_ref[0, pp]))
            do = do_ref[:, cols]
            doo = do * o_ref[:, cols].astype(F32)
            do2s.append(jnp.concatenate([jnp.where(first, do, 0.0), jnp.where(first, 0.0, do)], axis=0).astype(BF16))
            deltas.append(jnp.concatenate([jnp.sum(jnp.where(first, doo, 0.0), axis=-1, keepdims=True),
                                           jnp.sum(jnp.where(first, 0.0, doo), axis=-1, keepdims=True)], axis=0))
            lses.append(jnp.concatenate([lse_ref[0, pp, 0], lse_ref[0, pp, 1]], axis=0))

        def step(j, carry, masked):
            rows = pl.ds(pl.multiple_of(j * ta, ta), ta)
            out = []
            for pp in range(npp):
                dq_acc, rs_acc = carry[pp]
                cols = slice(LANES * pp, LANES * (pp + 1))
                ks = k_ref[0, pp, rows, :]
                s = _nt(q2s[pp], ks)
                if masked:
                    s = jnp.where(_diag_mask(ta), s, NEG_BIG)
                p = jnp.exp(s - lses[pp])
                dp = _nt(do2s[pp], v_ref[0, pp, 0, rows, :] + v_ref[0, pp, 1, rows, :])
                ds32 = p * (dp - deltas[pp])
                ds = ds32.astype(BF16)
                dk_ref[rows, cols] += _tn(ds, q2s[pp][:, :LANES])
                dv_ref[rows, cols] += _tn(p.astype(BF16), do2s[pp])
                cs_ref[0, pp, 0, j] += jnp.sum(ds32[:ta], axis=0, keepdims=True)
                cs_ref[0, pp, 1, j] += jnp.sum(ds32[ta:], axis=0, keepdims=True)
                out.append((dq_acc + jnp.dot(ds, ks[:, :LANES], preferred_element_type=F32),
                            rs_acc + jnp.sum(ds32, axis=-1, keepdims=True)))
            return tuple(out)

        init = tuple((jnp.zeros((2 * ta, LANES), F32), jnp.zeros((2 * ta, 1), F32)) for _ in range(npp))
        carry = lax.fori_loop(0, i, functools.partial(step, masked=False), init)
        for pp, (dq_acc, rs_acc) in enumerate(step(i, carry, True)):
            dq = jnp.where(first, dq_acc[:ta], dq_acc[ta:]) * ATT_SCALE
            dq_ref[:, LANES * pp:LANES * (pp + 1)] = dq.astype(dq_ref.dtype)
            rs_row = jnp.transpose(jnp.broadcast_to(rs_acc, (2 * ta, LANES)))[0:1]
            rs_ref[0, pp, 0, 0] = rs_row[:, :ta]
            rs_ref[0, pp, 1, 0] = rs_row[:, ta:]

        if carry_exchange is not None:
            @pl.when(last_step)
            def _():
                _chips_finish(t_ref, landed_ref, send_sems, recv_sems)

    vmem = (2 * npp * (_nbytes((seq, ATT_K), BF16) + 2 * _nbytes((seq, LANES), BF16) + 2 * _nbytes((seq, LANES), F32))
            + 32 * npp * ta * ta * 4 + 8 * 1024 * 1024)
    qblk = lambda b, g, i: (b * nq + i, g)
    acc_blk = pl.BlockSpec((seq, LANES * npp), lambda b, g, i: (b, g))
    in_specs = [pl.BlockSpec((1, npp, ta, ATT_K), lambda b, g, i: (b, g, i, 0)),
                pl.BlockSpec((1, npp, seq, ATT_K), lambda b, g, i: (b, g, 0, 0)),
                pl.BlockSpec((1, npp, 2, seq, LANES), lambda b, g, i: (b, g, 0, 0, 0)),
                pl.BlockSpec((ta, LANES * npp), qblk), pl.BlockSpec((ta, LANES * npp), qblk),
                pl.BlockSpec((1, npp, 2, ta, 1), lambda b, g, i: (b, g, 0, i, 0))]
    out_specs = [pl.BlockSpec((ta, LANES * npp), qblk), acc_blk, acc_blk,
                 pl.BlockSpec((1, npp, 2, 1, 1, ta), lambda b, g, i: (b, g, 0, i, 0, 0)),
                 pl.BlockSpec((1, npp, 2, nq, 1, ta), lambda b, g, i: (b, g, 0, 0, 0, 0))]
    sums = jax.ShapeDtypeStruct((nb, N_PAIRS, 2, nq, 1, ta), F32)
    out_shape = [jax.ShapeDtypeStruct((nb * seq, D_ATTN), BF16), jax.ShapeDtypeStruct((nb * seq, D_ATTN), F32),
                 jax.ShapeDtypeStruct((nb * seq, D_ATTN), F32), sums, sums]
    if carry_exchange is None:
        return pl.pallas_call(
            body, grid=grid, in_specs=in_specs, out_specs=out_specs, out_shape=out_shape, name="fox_bwd",
            compiler_params=_cparams(("parallel", "parallel", "arbitrary"), vmem),
        )(qa, ka, vm, y, dy, lse)
    any_spec = pl.BlockSpec(memory_space=pl.ANY)
    return pl.pallas_call(
        body, grid=grid, in_specs=in_specs + [any_spec], out_specs=out_specs + [any_spec],
        out_shape=out_shape + [jax.ShapeDtypeStruct(carry_exchange.shape, carry_exchange.dtype)],
        scratch_shapes=list(CHIPS_SEMS), name="fox_bwd_exchange",
        compiler_params=_cparams(("arbitrary", "arbitrary", "arbitrary"), vmem),
    )(qa, ka, vm, y, dy, lse, carry_exchange)


def _shift_down(a, k):
    row = lax.broadcasted_iota(jnp.int32, a.shape, 0)
    return jnp.where(row >= k, pltpu.roll(a, k, 0), 0.0)


def _shift_up(a, k):
    n = a.shape[0]
    row = lax.broadcasted_iota(jnp.int32, a.shape, 0)
    return jnp.where(row < n - k, pltpu.roll(a, n - k, 0), 0.0)


def _by_group(vals, shape):
    lane = lax.broadcasted_iota(jnp.int32, shape, 1)
    out = vals[-1]
    for gi in range(len(vals) - 2, -1, -1):
        out = jnp.where(lane < POOL_GROUP * (gi + 1), vals[gi], out)
    return out


def _pooled(u):
    s2 = u + _shift_down(u, 1)
    s4 = s2 + _shift_down(s2, 2)
    s8 = s4 + _shift_down(s4, 4)
    s16 = s8 + _shift_down(s8, 8)
    win = _by_group([s2, s4, s8, s16], u.shape)
    row = lax.broadcasted_iota(jnp.int32, u.shape, 0)
    wsize = _by_group([jnp.full(u.shape, w, jnp.int32) for w in POOL_WINDOWS], u.shape)
    inv = 1.0 / jnp.minimum(row + 1, wsize).astype(F32)
    return win * inv - u, inv


def _pool_fwd(rest, wbd, scale, seq):
    t = rest.shape[0]

    def body(u_ref, w_ref, s_ref, o_ref):
        pooled, _ = _pooled(u_ref[...])
        pw = jnp.dot(pooled.astype(BF16), w_ref[...], preferred_element_type=F32)
        o_ref[...] = (pw * s_ref[...]).astype(o_ref.dtype)

    blk = pl.BlockSpec((seq, D_POOL), lambda b: (b, 0))
    return pl.pallas_call(
        body, grid=(t // seq,),
        in_specs=[blk, pl.BlockSpec((D_POOL, D_POOL), lambda b: (0, 0)), pl.BlockSpec((1, D_POOL), lambda b: (0, 0))],
        out_specs=blk, out_shape=jax.ShapeDtypeStruct((t, D_POOL), BF16), name="pool_fwd",
        compiler_params=_cparams(("parallel",), 24 * seq * D_POOL * 4),
    )(rest, wbd, scale)


def _pool_bwd(rest, dy, wbd, wbd_t, scale, seq):
    t = rest.shape[0]

    def body(u_ref, dy_ref, w_ref, wt_ref, s_ref, du_ref, dw_ref, dsc_ref):
        i = pl.program_id(0)
        pooled, inv = _pooled(u_ref[...])
        pb = pooled.astype(BF16)
        pw = jnp.dot(pb, w_ref[...], preferred_element_type=F32)
        dyp = dy_ref[...]
        dsp = jnp.sum(dyp * pw, axis=0, keepdims=True)
        dpw = (dyp * s_ref[...]).astype(BF16)
        dwp = _tn(pb, dpw)
        dpooled = jnp.dot(dpw, wt_ref[...], preferred_element_type=F32)
        dwin = dpooled * inv
        t2 = dwin + _shift_up(dwin, 1)
        t4 = t2 + _shift_up(t2, 2)
        t8 = t4 + _shift_up(t4, 4)
        t16 = t8 + _shift_up(t8, 8)
        du_ref[...] = (_by_group([t2, t4, t8, t16], dwin.shape) - dpooled).astype(du_ref.dtype)

        @pl.when(i == 0)
        def _():
            dw_ref[...] = dwp
            dsc_ref[...] = dsp

        @pl.when(i > 0)
        def _():
            dw_ref[...] += dwp
            dsc_ref[...] += dsp

    blk = pl.BlockSpec((seq, D_POOL), lambda b: (b, 0))
    sq = pl.BlockSpec((D_POOL, D_POOL), lambda b: (0, 0))
    row = pl.BlockSpec((1, D_POOL), lambda b: (0, 0))
    return pl.pallas_call(
        body, grid=(t // seq,),
        in_specs=[blk, pl.BlockSpec((seq, D_POOL), lambda b: (b, 2)), sq, sq, row],
        out_specs=[blk, sq, row],
        out_shape=[jax.ShapeDtypeStruct((t, D_POOL), BF16), jax.ShapeDtypeStruct((D_POOL, D_POOL), F32),
                   jax.ShapeDtypeStruct((1, D_POOL), F32)], name="pool_bwd",
        compiler_params=_cparams(("arbitrary",), 40 * seq * D_POOL * 4),
    )(rest, dy, wbd, wbd_t, scale)


def _conv_fwd(rest, cw, seq):
    t = rest.shape[0]

    def body(cb_ref, cc_ref, ch_ref, w_ref, o_ref):
        u = cc_ref[...] * ch_ref[...]
        y = w_ref[0:1, :] * _shift_down(u, 2) + w_ref[1:2, :] * _shift_down(u, 1) + w_ref[2:3, :] * u
        o_ref[...] = (cb_ref[...] * y).astype(o_ref.dtype)

    def col(c):
        return pl.BlockSpec((seq, D_CONV), lambda b, c=c: (b, c))

    return pl.pallas_call(
        body, grid=(t // seq,), in_specs=[col(1), col(2), col(3), pl.BlockSpec((8, D_CONV), lambda b: (0, 0))],
        out_specs=pl.BlockSpec((seq, D_CONV), lambda b: (b, 0)),
        out_shape=jax.ShapeDtypeStruct((t, D_CONV), BF16), name="conv_fwd",
        compiler_params=_cparams(("parallel",), 24 * seq * D_CONV * 4),
    )(rest, rest, rest, cw)


def _conv_bwd(rest, dy, cw, seq):
    t = rest.shape[0]

    def body(cb_ref, cc_ref, ch_ref, dy_ref, w_ref, o_ref, dw_ref):
        i = pl.program_id(0)
        cc = cc_ref[...]
        ch = ch_ref[...]
        u = cc * ch
        u1 = _shift_down(u, 1)
        u2 = _shift_down(u, 2)
        y = w_ref[0:1, :] * u2 + w_ref[1:2, :] * u1 + w_ref[2:3, :] * u
        dyc = dy_ref[...]
        d2 = dyc * cb_ref[...]
        du = w_ref[0:1, :] * _shift_up(d2, 2) + w_ref[1:2, :] * _shift_up(d2, 1) + w_ref[2:3, :] * d2
        o_ref[:, 0:D_CONV] = (dyc * y).astype(o_ref.dtype)
        o_ref[:, D_CONV:2 * D_CONV] = (du * ch).astype(o_ref.dtype)
        o_ref[:, 2 * D_CONV:3 * D_CONV] = (du * cc).astype(o_ref.dtype)
        tap = lax.broadcasted_iota(jnp.int32, (8, D_CONV), 0)
        dwp = jnp.where(tap == 0, jnp.sum(d2 * u2, axis=0, keepdims=True),
                        jnp.where(tap == 1, jnp.sum(d2 * u1, axis=0, keepdims=True),
                                  jnp.where(tap == 2, jnp.sum(d2 * u, axis=0, keepdims=True), 0.0)))

        @pl.when(i == 0)
        def _():
            dw_ref[...] = dwp

        @pl.when(i > 0)
        def _():
            dw_ref[...] += dwp

    def col(c):
        return pl.BlockSpec((seq, D_CONV), lambda b, c=c: (b, c))

    taps = pl.BlockSpec((8, D_CONV), lambda b: (0, 0))
    return pl.pallas_call(
        body, grid=(t // seq,), in_specs=[col(1), col(2), col(3), col(3), taps],
        out_specs=[pl.BlockSpec((seq, 3 * D_CONV), lambda b: (b, 0)), taps],
        out_shape=[jax.ShapeDtypeStruct((t, 3 * D_CONV), BF16), jax.ShapeDtypeStruct((8, D_CONV), F32)],
        name="conv_bwd", compiler_params=_cparams(("arbitrary",), 48 * seq * D_CONV * 4),
    )(rest, rest, rest, dy, cw)


def _adamw(w, g, m, v, name):
    r, c = w.shape
    tr = _pick(r, (512, 352, 256, 128)) if r > 512 else r

    def body(w_ref, g_ref, m_ref, v_ref, d_ref, mo_ref, vo_ref):
        gv = g_ref[...]
        mn = ADAM_B1 * m_ref[...] + (1.0 - ADAM_B1) * gv
        vn = ADAM_B2 * v_ref[...] + (1.0 - ADAM_B2) * (gv * gv)
        m_hat = mn / (1.0 - ADAM_B1 ** ADAM_STEP)
        v_hat = vn / (1.0 - ADAM_B2 ** ADAM_STEP)
        d_ref[...] = -ADAM_LR * (m_hat / (jnp.sqrt(v_hat) + ADAM_EPS) + ADAM_WD * w_ref[...])
        mo_ref[...] = mn
        vo_ref[...] = vn

    blk = pl.BlockSpec((tr, c), lambda i: (i, 0))
    sds = jax.ShapeDtypeStruct((r, c), F32)
    return pl.pallas_call(
        body, grid=(r // tr,), in_specs=[blk] * 4, out_specs=[blk] * 3, out_shape=[sds] * 3, name=name,
        compiler_params=_cparams(("parallel",), 20 * tr * max(c, LANES) * 4),
    )(w, g, m, v)


def _sum_slots(a, name):
    ns, r, c = a.shape
    tr = _pick(r, (384, 368, 256, 184, 136, 128, 88, 8))

    def body(a_ref, o_ref):
        acc = a_ref[0].astype(F32)
        for s in range(1, ns):
            acc = acc + a_ref[s].astype(F32)
        o_ref[...] = acc

    return pl.pallas_call(
        body, grid=(r // tr,), in_specs=[pl.BlockSpec((ns, tr, c), lambda i: (0, i, 0))],
        out_specs=pl.BlockSpec((tr, c), lambda i: (i, 0)), out_shape=jax.ShapeDtypeStruct((r, c), F32), name=name,
        compiler_params=_cparams(("parallel",), 4 * (ns + 2) * tr * c * 4),
    )(a)


def _add_core_half(core, g4, theirs, out_dtype, name):
    ns, _, r, c = g4.shape
    tr = _pick(r, (384, 368, 256, 184, 136, 128, 88, 8))

    def body(core_ref, a_ref, b_ref, o_ref):
        o_ref[...] = (a_ref[0] + b_ref[...]).astype(o_ref.dtype)

    blk = pl.BlockSpec((1, tr, c), lambda s, i, core_ref: (s, i, 0))
    return pl.pallas_call(
        body,
        grid_spec=pltpu.PrefetchScalarGridSpec(
            num_scalar_prefetch=1, grid=(ns, r // tr),
            in_specs=[pl.BlockSpec((1, 1, tr, c), lambda s, i, core_ref: (s, core_ref[0], i, 0)), blk],
            out_specs=blk),
        out_shape=jax.ShapeDtypeStruct((ns, r, c), out_dtype), name=name,
        compiler_params=_cparams(("parallel", "parallel"), 10 * tr * c * 4),
    )(core, g4, theirs)


def _sum_chips(order, own, landed, name):
    ns, r, c = own.shape
    tr = _pick(r, (384, 368, 256, 184, 136, 128, 88, 8))

    def body(order_ref, a_ref, b1_ref, b2_ref, b3_ref, o_ref):
        o_ref[...] = ((a_ref[0].astype(F32) + b1_ref[0].astype(F32)) + b2_ref[0].astype(F32)) + b3_ref[0].astype(F32)

    def slot(k):
        return pl.BlockSpec((1, tr, c), lambda i, order_ref, k=k: (order_ref[k], i, 0))

    return pl.pallas_call(
        body,
        grid_spec=pltpu.PrefetchScalarGridSpec(
            num_scalar_prefetch=1, grid=(r // tr,), in_specs=[slot(0), slot(1), slot(2), slot(3)],
            out_specs=pl.BlockSpec((tr, c), lambda i, order_ref: (i, 0))),
        out_shape=jax.ShapeDtypeStruct((r, c), F32), name=name,
        compiler_params=_cparams(("parallel",), 16 * tr * c * 4),
    )(order, own, landed, landed, landed)


def _mesh_pos():
    return lax.axis_index("x"), lax.axis_index("y"), lax.axis_index("c")


def _comm_call(name, gathers=(), chips=None):
    payloads = list(gathers) + ([] if chips is None else [chips])
    n = len(payloads)

    def body(*refs):
        ins, outs, sems = refs[:n], refs[n:2 * n], refs[2 * n:]
        jobs = [(_gather_start, _gather_finish)] * len(gathers) + ([] if chips is None else [(_chips_start, _chips_finish)])
        for k, (start, _) in enumerate(jobs):
            start(ins[k], outs[k], sems[2 * k], sems[2 * k + 1])
        for k, (_, finish) in enumerate(jobs):
            finish(ins[k], outs[k], sems[2 * k], sems[2 * k + 1])

    any_spec = pl.BlockSpec(memory_space=pl.ANY)
    out_shape = [jax.ShapeDtypeStruct((N_DEV,) + x.shape, x.dtype) for x in gathers]
    sems = list(GATHER_SEMS) * len(gathers)
    if chips is not None:
        out_shape.append(jax.ShapeDtypeStruct(chips.shape, chips.dtype))
        sems += list(CHIPS_SEMS)
    outs = pl.pallas_call(body, out_shape=out_shape, in_specs=[any_spec] * n, out_specs=[any_spec] * n,
                          scratch_shapes=sems, name=name)(*payloads)
    return [_fill_own_slot(o, x) for o, x in zip(outs, gathers)] + ([] if chips is None else [outs[-1]])


GATHER_SEMS = (pltpu.SemaphoreType.DMA((7,)), pltpu.SemaphoreType.DMA((7,)))


def _fill_own_slot(gathered, x):
    mx, my, mc = _mesh_pos()
    return lax.dynamic_update_slice_in_dim(gathered, x[None], 4 * mx + 2 * my + mc, axis=0)


def _gather_copies(x_ref, out_ref, send_sems, recv_sems):
    mx, my, mc = _mesh_pos()
    me, sibling = (mx, my, mc), (mx, my, 1 - mc)
    chips = [(1 - mx, my), (mx, 1 - my), (1 - mx, 1 - my)]

    def slot(px, py, pc):
        return out_ref.at[4 * px + 2 * py + pc]

    def copy(k, block, to, src=None):
        return pltpu.make_async_remote_copy(
            src_ref=slot(*block) if src is None else src, dst_ref=slot(*block),
            send_sem=send_sems.at[k], recv_sem=recv_sems.at[k],
            device_id=to, device_id_type=pl.DeviceIdType.MESH)

    first = [copy(0, me, sibling, src=x_ref)]
    first += [copy(1 + j, me, (*chip, mc), src=x_ref) for j, chip in enumerate(chips)]
    passed = [copy(4 + j, (*chip, mc), sibling) for j, chip in enumerate(chips)]
    over_ici = [copy(1 + j, (*chip, mc), me) for j, chip in enumerate(chips)]
    over_d2d = [copy(0, sibling, me)] + [copy(4 + j, (*chip, 1 - mc), me) for j, chip in enumerate(chips)]
    return first, passed, over_ici, over_d2d


def _gather_start(x_ref, out_ref, send_sems, recv_sems):
    for cp in _gather_copies(x_ref, out_ref, send_sems, recv_sems)[0]:
        cp.start()


def _gather_relay(x_ref, out_ref, send_sems, recv_sems):
    _, passed, over_ici, _ = _gather_copies(x_ref, out_ref, send_sems, recv_sems)
    for landed, relay in zip(over_ici, passed):
        landed.wait_recv()
        relay.start()


def _gather_drain(x_ref, out_ref, send_sems, recv_sems):
    first, passed, _, over_d2d = _gather_copies(x_ref, out_ref, send_sems, recv_sems)
    for landed in over_d2d:
        landed.wait_recv()
    for cp in first + passed:
        cp.wait_send()


def _gather_finish(x_ref, out_ref, send_sems, recv_sems):
    _gather_relay(x_ref, out_ref, send_sems, recv_sems)
    _gather_drain(x_ref, out_ref, send_sems, recv_sems)


SIBLING_SEMS = (pltpu.SemaphoreType.DMA((N_CHIPS,)), pltpu.SemaphoreType.DMA((N_CHIPS,)))


def _sibling_copies(g_ref, theirs_ref, send_sems, recv_sems):
    mx, my, mc = _mesh_pos()
    return [pltpu.make_async_remote_copy(
        src_ref=g_ref.at[chip, 1 - mc], dst_ref=theirs_ref.at[chip],
        send_sem=send_sems.at[chip], recv_sem=recv_sems.at[chip],
        device_id=(mx, my, 1 - mc), device_id_type=pl.DeviceIdType.MESH) for chip in range(N_CHIPS)]


def _sibling_start(g_ref, theirs_ref, send_sems, recv_sems):
    for cp in _sibling_copies(g_ref, theirs_ref, send_sems, recv_sems):
        cp.start()


def _sibling_finish(g_ref, theirs_ref, send_sems, recv_sems):
    copies = _sibling_copies(g_ref, theirs_ref, send_sems, recv_sems)
    for cp in copies:
        cp.wait_recv()
    for cp in copies:
        cp.wait_send()


CHIPS_SEMS = (pltpu.SemaphoreType.DMA((N_CHIPS - 1,)), pltpu.SemaphoreType.DMA((N_CHIPS - 1,)))


def _chips_copies(t_ref, out_ref, send_sems, recv_sems):
    mx, my, mc = _mesh_pos()
    my_chip = 2 * mx + my
    copies = []
    for k in range(1, N_CHIPS):
        px = 1 - mx if k & 2 else mx
        py = 1 - my if k & 1 else my
        peer_chip = 2 * px + py

        def rdma(dst_slot, px=px, py=py, peer_chip=peer_chip, k=k):
            return pltpu.make_async_remote_copy(
                src_ref=t_ref.at[peer_chip], dst_ref=out_ref.at[dst_slot],
                send_sem=send_sems.at[k - 1], recv_sem=recv_sems.at[k - 1],
                device_id=(px, py, mc), device_id_type=pl.DeviceIdType.MESH)

        copies.append((rdma(my_chip), rdma(peer_chip)))
    return copies


def _chips_start(t_ref, out_ref, send_sems, recv_sems):
    for send, _ in _chips_copies(t_ref, out_ref, send_sems, recv_sems):
        send.start()


def _chips_finish(t_ref, out_ref, send_sems, recv_sems):
    copies = _chips_copies(t_ref, out_ref, send_sems, recv_sems)
    for _, landed in copies:
        landed.wait_recv()
    for send, _ in copies:
        send.wait_send()


def _add_sibling(g4, theirs):
    core = jnp.reshape(lax.axis_index("c"), (1,)).astype(jnp.int32)
    return _add_core_half(core, g4, theirs, BF16, name="add_sibling_grads")


def _sum_landed(chip_sums, landed):
    mx, my, _ = _mesh_pos()
    order = jnp.stack([2 * mx + my, 2 * (1 - mx) + my, 2 * mx + (1 - my), 2 * (1 - mx) + (1 - my)]).astype(jnp.int32)
    return _sum_chips(order, chip_sums, landed, name="sum_grads")


def _perm_mix_rows(wt):
    f0 = D_QKV
    f1 = f0 + N_HEADS
    return jnp.concatenate([wt[:f0], wt[f1:], jnp.pad(wt[f0:f1], ((0, LANES - N_HEADS), (0, 0)))], axis=0)


def _unperm_mix_rows(gt):
    f0 = D_QKV
    return jnp.concatenate([gt[:f0], gt[f0 + D_REST:f0 + D_REST + N_HEADS], gt[f0:f0 + D_REST]], axis=0)


def _pack_shards(parts, l, dtype):
    w1i, w1o, wmi, wmo, w2i, w2o = parts
    rows = [w1i[l].T, w1o[l], jnp.pad(wmi[l].T, ((0, MIX_ROWS_PAD - MIX_ROWS), (0, 0))), wmo[l], w2i[l].T, w2o[l]]
    return jnp.concatenate(rows, axis=0).astype(dtype)


PACK_HEAD = FFN_ROWS + OUT_ROWS


def _ffn_weights(wg, o):
    return dict(wi_t=wg[:, o:o + FFN_ROWS].reshape(2 * D_FF, D_MODEL),
                wo=wg[:, o + FFN_ROWS:o + FFN_ROWS + OUT_ROWS].reshape(D_FF, D_MODEL))


def _tail_weights(wg):
    mix = dict(wm_t=_perm_mix_rows(wg[:, :MIX_ROWS].reshape(D_IN, D_MODEL)),
               wo=wg[:, MIX_ROWS_PAD:MIX_ROWS_PAD + MO_ROWS].reshape(D_MODEL, D_MODEL))
    return mix, _ffn_weights(wg, MIX_ROWS_PAD + MO_ROWS)


GRAD_AT = dict(f1i=0, f2i=FFN_ROWS, f1o=4 * OUT_ROWS, f2o=5 * OUT_ROWS, mi=6 * OUT_ROWS, mo=20 * MO_ROWS)
GRAD_ROWS = GRAD_AT["mo"] + MO_ROWS
GRAD_SHAPE = (N_CHIPS, 2, GRAD_ROWS, D_MODEL)


def _into_ffn_in(buf, tag, half):
    rb = GRAD_AT[tag] // FFN_ROWS
    return (buf, GRAD_SHAPE, (1, 2, FFN_ROWS, D_MODEL), lambda i: (2 * half + i, 0, rb, 0))


def _into_ffn_out(buf, tag):
    rb = GRAD_AT[tag] // OUT_ROWS
    return (buf, GRAD_SHAPE, (2, 2, OUT_ROWS, D_MODEL), lambda i: (i, 0, rb, 0))


def _into_mix_out(buf):
    rb = GRAD_AT["mo"] // MO_ROWS
    return (buf, GRAD_SHAPE, (N_CHIPS, 2, MO_ROWS, D_MODEL), lambda i: (0, 0, rb, 0))


def _put_mix_in(buf, g_in_t):
    gmi = _unperm_mix_rows(g_in_t).reshape(N_DEV, MIX_ROWS, D_MODEL)
    gmi = jnp.pad(gmi, ((0, 0), (0, OUT_ROWS - MIX_ROWS), (0, 0))).reshape(N_CHIPS, 2, OUT_ROWS, D_MODEL)
    return lax.dynamic_update_slice(buf, gmi, (0, 0, GRAD_AT["mi"], 0))


def _out_proj(a, w, x, alpha, next_gain, name):
    if next_gain is None:
        return _mm_nn(a, w, out_dtype=F32, res=x, alpha=alpha, name=name), None
    return _mm_nn(a, w, out_dtype=F32, res=x, alpha=alpha, next_gain=next_gain, name=name + "_norm")


def _ffn_forward(x, xn, w, next_gain, carry_gather=None):
    res = _ffn_in(xn, w["wi_t"], name="ffn_in", carry_gather=carry_gather)
    h, pg, pu = res[:3]
    x_new, xn_next = _out_proj(h, w["wo"], x, 0.5, next_gain, "ffn_out")
    out = (x_new, xn_next, dict(x=x, xn=xn, h=h, pg=pg, pu=pu))
    return out if carry_gather is None else out + (_fill_own_slot(res[3], carry_gather),)


def _ffn_backward(dxo, dxo_b, gain, w, saved, gbuf, tag, exchange=False):
    dzg, dzu = _ffn_bwd_mid(dxo_b, w["wo"], saved["pg"], saved["pu"], name="ffn_bwd_mid")
    gbuf = _mm_tn(saved["h"], dxo_b, alpha=0.5, tm=F_HALF, name="ffn_gw_out", into=_into_ffn_out(gbuf, tag + "o"))
    gbuf = _mm_tn(dzg, saved["xn"], tm=F_HALF, name="ffn_gw_in", into=_into_ffn_in(gbuf, tag + "i", 0))
    gbuf = _mm_tn(dzu, saved["xn"], tm=F_HALF, name="ffn_gw_in", into=_into_ffn_in(gbuf, tag + "i", 1))
    res = _dxn_norm_bwd([dzg, dzu], w["wi_t"], saved["x"], gain, dxo, name="ffn_dxn_norm_bwd",
                        carry_sibling=gbuf if exchange else None)
    return res[0], res[1], res[2], gbuf, (res[3] if exchange else None)


def _mixer_forward(x, xn, p, w, nb, seq, ta, next_gain, next_pack=None):
    qkv, rest, fl = _mix_proj(xn, w["wm_t"])
    qa, ka, vm = _fox_prep(fl, p["bf"], qkv, nb, seq)
    if next_pack is None:
        (y_attn, lse), next_gathered = _fox_fwd(qa, ka, vm, nb, seq, ta), None
    else:
        y_attn, lse, next_gathered = _fox_fwd(qa, ka, vm, nb, seq, ta, carry_gather=next_pack)
        next_gathered = _fill_own_slot(next_gathered, next_pack)
    y_pool = _pool_fwd(rest, p["wbd"], p["scale"], seq)
    y_conv = _conv_fwd(rest, p["cw"], seq)
    y = jnp.concatenate([y_attn, y_pool, y_conv], axis=1)
    x_new, xn_next = _out_proj(y, w["wo"], x, 1.0, next_gain, "mix_out")
    return x_new, xn_next, dict(x=x, xn=xn, qa=qa, ka=ka, vm=vm, rest=rest, fl=fl, lse=lse, y=y), next_gathered


def _mixer_backward(dxo, dxo_b, p, w, sv, nb, seq, ta, gbuf, pending=None):
    t = dxo.shape[0]
    dy = _mm_nt(dxo_b, w["wo"], out_dtype=F32, name="mix_dy")
    gbuf = _mm_tn(sv["y"], dxo_b, name="mix_gw_out", into=_into_mix_out(gbuf))
    res = _fox_bwd(sv["qa"], sv["ka"], sv["vm"], sv["y"], dy, sv["lse"], nb, seq, ta, carry_exchange=pending)
    dq, dk, dv, d_rows, d_cols = res[:5]
    landed = None if pending is None else res[5]
    ddh = (d_rows.reshape(nb, N_HEADS, seq) - d_cols.reshape(nb, N_HEADS, seq)).transpose(0, 2, 1)
    ddh = ddh.reshape(t, N_HEADS)
    dfl, dbf = _fox_prep_bwd(jnp.pad(ddh, ((0, 0), (0, LANES - N_HEADS))), sv["fl"], p["bf"], seq)
    dpool, dwbd, dscale = _pool_bwd(sv["rest"], dy, p["wbd"], p["wbd_t"], p["scale"], seq)
    dconv, dcw = _conv_bwd(sv["rest"], dy, p["cw"], seq)
    dproj = jnp.concatenate([dq, dk.astype(BF16), dv.astype(BF16), dpool, dconv, dfl], axis=1)
    gbuf = _put_mix_in(gbuf, _mm_tn(dproj, sv["xn"], tm=D_INP // 3, name="mix_gw_in"))
    dx, dx_b, dg = _dxn_norm_bwd([dproj], w["wm_t"], sv["x"], p["norm"], dxo, name="mix_dxn_norm_bwd")
    return dx, dx_b, dict(norm=dg, bf=dbf, wbd=dwbd, scale=dscale, cw=dcw), gbuf, landed


def _block_diag(wp):
    z = jnp.zeros((POOL_GROUP, POOL_GROUP), wp.dtype)
    return jnp.concatenate(
        [jnp.concatenate([wp[g] if g == r else z for g in range(4)], axis=1) for r in range(4)], axis=0)


def _row_pad(a, rows):
    a = a.reshape(-1, a.shape[-1])
    return jnp.pad(a, ((0, rows - a.shape[0]), (0, 0)))


def kernel(x, norm_ffn1, w_ffn1_in, w_ffn1_out, norm_mix, w_mix_in, b_forget, w_pool, pool_scale, conv_w, w_mix_out, norm_ffn2, w_ffn2_in, w_ffn2_out, norm_final, loss_target, m_norm_ffn1, m_w_ffn1_in, m_w_ffn1_out, m_norm_mix, m_w_mix_in, m_b_forget, m_w_pool, m_pool_scale, m_conv_w, m_w_mix_out, m_norm_ffn2, m_w_ffn2_in, m_w_ffn2_out, m_norm_final, v_norm_ffn1, v_w_ffn1_in, v_w_ffn1_out, v_norm_mix, v_w_mix_in, v_b_forget, v_w_pool, v_pool_scale, v_conv_w, v_w_mix_out, v_norm_ffn2, v_w_ffn2_in, v_w_ffn2_out, v_norm_final):
    nb, seq, d = x.shape
    depth = norm_ffn1.shape[0]
    t = nb * seq
    ta = _pick(seq, (ATT_TILE, 128))
    my_id = 4 * lax.axis_index("x") + 2 * lax.axis_index("y") + lax.axis_index("c")
    cshard = conv_w.shape[-1]

    shards = (w_ffn1_in, w_ffn1_out, w_mix_in, w_mix_out, w_ffn2_in, w_ffn2_out)
    pack0 = _pack_shards(shards, 0, BF16)
    wg_head, cw_g = _comm_call("gather_weights_and_taps", gathers=[
        pack0[:PACK_HEAD], _row_pad(conv_w.reshape(depth * 3, cshard), 16).reshape(4, LANES)])
    cw_all = cw_g.reshape(N_DEV, 16, cshard)[:, :depth * 3].reshape(N_DEV, depth, 3, cshard)
    cw_all = cw_all.transpose(1, 2, 0, 3).reshape(depth, 3, D_CONV)

    xs = x.reshape(t, d)
    xn = _rmsnorm_fwd(xs, norm_ffn1[0][None], name="first_norm")
    saved = []
    for l in range(depth):
        wbd = _block_diag(w_pool[l])
        p = dict(norm=norm_mix[l][None], bf=jnp.pad(b_forget[l], (0, LANES - N_HEADS))[None],
                 wbd=wbd.astype(BF16), wbd_t=wbd.T.astype(BF16), scale=pool_scale[l][None],
                 cw=_row_pad(cw_all[l], 8))
        w = dict(f1=_ffn_weights(wg_head, 0))
        if l == 0:
            xs, xn, s1, wg_tail = _ffn_forward(xs, xn, w["f1"], norm_mix[l][None], carry_gather=pack0[PACK_HEAD:])
        else:
            xs, xn, s1 = _ffn_forward(xs, xn, w["f1"], norm_mix[l][None])
        w["mix"], w["f2"] = _tail_weights(wg_tail)
        next_pack = _pack_shards(shards, l + 1, BF16) if l + 1 < depth else None
        xs, xn, sm, wg = _mixer_forward(xs, xn, p, w["mix"], nb, seq, ta, norm_ffn2[l][None], next_pack)
        if wg is not None:
            wg_head, wg_tail = wg[:, :PACK_HEAD], wg[:, PACK_HEAD:]
        xs, xn, s2 = _ffn_forward(xs, xn, w["f2"], norm_ffn1[l + 1][None] if l + 1 < depth else None)
        saved.append((w, p, s1, sm, s2))

    dx, dx_b, g_norm_final, loss_part = _final_loss_bwd(xs, norm_final[None], loss_target.reshape(t, d))
    layer_g = [None] * depth
    small = [None] * depth
    chip_sums = None
    for l in reversed(range(depth)):
        w, p, s1, sm, s2 = saved[l]
        dx, dx_b, dg2, gbuf, _ = _ffn_backward(dx, dx_b, norm_ffn2[l][None], w["f2"], s2, None, "f2")
        dx, dx_b, gm, gbuf, landed = _mixer_backward(dx, dx_b, p, w["mix"], sm, nb, seq, ta, gbuf, chip_sums)
        if chip_sums is not None:
            layer_g[l + 1] = _sum_landed(chip_sums, landed)
        dx, dx_b, dg1, gbuf, theirs = _ffn_backward(dx, dx_b, norm_ffn1[l][None], w["f1"], s1, gbuf, "f1", exchange=True)
        chip_sums = _add_sibling(gbuf, theirs)
        small[l] = dict(n1=dg1, nm=gm["norm"], n2=dg2, bf=gm["bf"], wbd=gm["wbd"], scale=gm["scale"], cw=gm["cw"])
    grad_x = dx.reshape(nb, seq, d)

    def tile8(a):
        return jnp.pad(a, ((0, 8 - a.shape[0]), (0, D_MODEL - a.shape[1])))

    rows = []
    for l in range(depth):
        s = small[l]
        wp_rows = jnp.stack([s["wbd"][POOL_GROUP * g:POOL_GROUP * (g + 1), POOL_GROUP * g:POOL_GROUP * (g + 1)]
                             for g in range(4)]).reshape(16, D_MODEL)
        rows += [tile8(s["n1"]), tile8(s["nm"]), tile8(s["n2"]), tile8(s["bf"]), tile8(s["scale"]), tile8(s["cw"]),
                 wp_rows]
    rows += [tile8(g_norm_final), tile8(loss_part)]
    small_gathered, landed = _comm_call("exchange_grads_chips_gather_small", gathers=[jnp.concatenate(rows, axis=0)],
                                        chips=chip_sums)
    layer_g[0] = _sum_landed(chip_sums, landed)

    pieces = {}
    for nm, n in (("f1i", FFN_ROWS), ("f1o", OUT_ROWS), ("mi", MIX_ROWS), ("mo", MO_ROWS), ("f2i", FFN_ROWS),
                  ("f2o", OUT_ROWS)):
        pieces[nm] = jnp.stack([g[GRAD_AT[nm]:GRAD_AT[nm] + n] for g in layer_g])
    g_sharded = dict(
        w_ffn1_in=pieces["f1i"].transpose(0, 2, 1), w_ffn1_out=pieces["f1o"],
        w_mix_in=pieces["mi"].transpose(0, 2, 1), w_mix_out=pieces["mo"],
        w_ffn2_in=pieces["f2i"].transpose(0, 2, 1), w_ffn2_out=pieces["f2o"])

    per_layer = 6 * 8 + 16
    small_sum = _sum_slots(small_gathered, name="sum_small_grads")
    lay = small_sum[:depth * per_layer].reshape(depth, per_layer, D_MODEL)
    g_small = dict(
        norm_ffn1=lay[:, 0], norm_mix=lay[:, 8], norm_ffn2=lay[:, 16], b_forget=lay[:, 24, :N_HEADS],
        pool_scale=lay[:, 32, :D_POOL],
        conv_w=lax.dynamic_slice_in_dim(lay[:, 40:43, :D_CONV], my_id * cshard, cshard, axis=2),
        w_pool=lay[:, 48:64].reshape(depth, 4, POOL_GROUP, POOL_GROUP),
        norm_final=small_sum[depth * per_layer])
    loss = small_sum[depth * per_layer + 8, 0]

    given = dict(norm_ffn1=(norm_ffn1, m_norm_ffn1, v_norm_ffn1), w_ffn1_in=(w_ffn1_in, m_w_ffn1_in, v_w_ffn1_in),
                 w_ffn1_out=(w_ffn1_out, m_w_ffn1_out, v_w_ffn1_out), norm_mix=(norm_mix, m_norm_mix, v_norm_mix),
                 w_mix_in=(w_mix_in, m_w_mix_in, v_w_mix_in), b_forget=(b_forget, m_b_forget, v_b_forget),
                 w_pool=(w_pool, m_w_pool, v_w_pool), pool_scale=(pool_scale, m_pool_scale, v_pool_scale),
                 conv_w=(conv_w, m_conv_w, v_conv_w), w_mix_out=(w_mix_out, m_w_mix_out, v_w_mix_out),
                 norm_ffn2=(norm_ffn2, m_norm_ffn2, v_norm_ffn2), w_ffn2_in=(w_ffn2_in, m_w_ffn2_in, v_w_ffn2_in),
                 w_ffn2_out=(w_ffn2_out, m_w_ffn2_out, v_w_ffn2_out), norm_final=(norm_final, m_norm_final, v_norm_final))
    names = list(given)
    grads, deltas, new_m, new_v = {}, {}, {}, {}
    for nm in names:
        wv, mv, vv = given[nm]
        gv = (g_sharded[nm] if nm in g_sharded else g_small[nm]).reshape(wv.shape)
        shape2 = (-1, wv.shape[-1]) if wv.ndim > 1 else (1, wv.shape[0])
        dl, mn, vn = _adamw(wv.reshape(shape2), gv.reshape(shape2), mv.reshape(shape2), vv.reshape(shape2),
                            name="adamw_" + nm)
        grads[nm], deltas[nm], new_m[nm], new_v[nm] = gv, dl.reshape(wv.shape), mn.reshape(wv.shape), vn.reshape(wv.shape)
    return (loss, grad_x, *[grads[n] for n in names], *[deltas[n] for n in names],
            *[new_m[n] for n in names], *[new_v[n] for n in names])
```

```python
import functools

import jax
import jax.numpy as jnp
from jax import lax
from jax.experimental import pallas as pl
from jax.experimental.pallas import tpu as pltpu

F32 = jnp.float32
BF16 = jnp.bfloat16

D_MODEL = 1024
D_FF = 2816
HEAD_DIM = 64
N_HEADS = 8
N_PAIRS = N_HEADS // 2
D_ATTN = 512
D_POOL = 256
D_CONV = 256
POOL_WINDOWS = (2, 4, 8, 16)
POOL_GROUP = 64
D_IN = 2568
RMS_EPS = 1e-6
ADAM_LR, ADAM_B1, ADAM_B2, ADAM_EPS, ADAM_WD, ADAM_STEP = 0.001, 0.9, 0.999, 1e-08, 0.01, 10

N_DEV = 8
N_CHIPS = 4
LANES = 128
VMEM_BYTES_V7X = 64 * 1024 * 1024
VMEM_LIMIT_MAX = VMEM_BYTES_V7X - 8 * 1024 * 1024

F_HALF = D_FF // 2
D_QKV = 3 * D_ATTN
D_REST = D_POOL + 3 * D_CONV
D_INP = D_QKV + D_REST + LANES
MIX_ROWS = 321
MIX_ROWS_PAD = 336
FFN_ROWS = 704
OUT_ROWS = 352
MO_ROWS = 128
LAYER_ROWS = 2 * (FFN_ROWS + OUT_ROWS) + MIX_ROWS_PAD + MO_ROWS
NEG_BIG = -1e30
ATT_SCALE = HEAD_DIM ** -0.5
ATT_K = 2 * LANES
ATT_TILE = 256
ATT_PAIRS_FWD = 4
ATT_PAIRS_BWD = 4
MXU_COLS = 256


def _cparams(sem, vmem_bytes):
    limit = int(min(max(vmem_bytes, 16 * 1024 * 1024), VMEM_LIMIT_MAX))
    return pltpu.CompilerParams(dimension_semantics=sem, vmem_limit_bytes=limit)


def _nbytes(shape, dtype):
    n = 1
    for s in shape:
        n *= s
    return n * jnp.dtype(dtype).itemsize


def _pick(n, prefs):
    for p in prefs:
        if n % p == 0:
            return p
    return n


def _rmsnorm_fwd(x, g, name):
    t, d = x.shape
    tm = _pick(t, (512, 256, 128))

    def body(x_ref, g_ref, o_ref):
        xv = x_ref[...]
        r = lax.rsqrt(jnp.mean(xv * xv, axis=-1, keepdims=True) + RMS_EPS)
        o_ref[...] = ((xv * r) * g_ref[...]).astype(o_ref.dtype)

    return pl.pallas_call(
        body, grid=(t // tm,),
        in_specs=[pl.BlockSpec((tm, d), lambda i: (i, 0)), pl.BlockSpec((1, d), lambda i: (0, 0))],
        out_specs=pl.BlockSpec((tm, d), lambda i: (i, 0)),
        out_shape=jax.ShapeDtypeStruct((t, d), BF16), name=name,
        compiler_params=_cparams(("parallel",), 6 * tm * d * 4),
    )(x, g)


def _mm_nn(a, b, *, out_dtype, name, res=None, alpha=1.0, tn=None, next_gain=None):
    m, k = a.shape
    n = b.shape[1]
    tn = n if tn is None else tn
    tm = _pick(m, (512, 256, 128))
    with_res = res is not None
    with_norm = next_gain is not None
    assert not with_norm or tn == n

    def body(*refs):
        refs = list(refs)
        a_ref, b_ref = refs[:2]
        r_ref = refs[2] if with_res else None
        g_ref = refs[2 + with_res] if with_norm else None
        o_ref = refs[2 + with_res + with_norm]
        acc = jnp.dot(a_ref[...], b_ref[...], preferred_element_type=F32)
        if with_res:
            acc = r_ref[...] + alpha * acc
        o_ref[...] = acc.astype(o_ref.dtype)
        if with_norm:
            r = lax.rsqrt(jnp.mean(acc * acc, axis=-1, keepdims=True) + RMS_EPS)
            refs[-1][...] = ((acc * r) * g_ref[...]).astype(BF16)

    in_specs = [pl.BlockSpec((tm, k), lambda j, i: (i, 0)), pl.BlockSpec((k, tn), lambda j, i: (0, j))]
    args = [a, b]
    out_blk = pl.BlockSpec((tm, tn), lambda j, i: (i, j))
    out_specs, out_shape = [out_blk], [jax.ShapeDtypeStruct((m, n), out_dtype)]
    if with_res:
        in_specs.append(out_blk)
        args.append(res)
    if with_norm:
        in_specs.append(pl.BlockSpec((1, n), lambda j, i: (0, 0)))
        args.append(next_gain)
        out_specs.append(out_blk)
        out_shape.append(jax.ShapeDtypeStruct((m, n), BF16))
    vmem = 2 * (_nbytes((tm, k), BF16) + _nbytes((k, tn), BF16) + 4 * _nbytes((tm, tn), F32))
    outs = pl.pallas_call(
        body, grid=(n // tn, m // tm), in_specs=in_specs, out_specs=out_specs, out_shape=out_shape, name=name,
        compiler_params=_cparams(("parallel", "parallel"), vmem),
    )(*args)
    return outs if with_norm else outs[0]


def _mm_nt(a, b_t, *, out_dtype, name):
    m, k = a.shape
    n = b_t.shape[0]
    tm = _pick(m, (512, 256, 128))

    def body(a_ref, b_ref, o_ref):
        o_ref[...] = _nt(a_ref[...], b_ref[...]).astype(o_ref.dtype)

    vmem = 2 * (_nbytes((tm, k), BF16) + _nbytes((n, k), BF16) + 3 * _nbytes((tm, n), F32))
    return pl.pallas_call(
        body, grid=(m // tm,),
        in_specs=[pl.BlockSpec((tm, k), lambda i: (i, 0)), pl.BlockSpec((n, k), lambda i: (0, 0))],
        out_specs=pl.BlockSpec((tm, n), lambda i: (i, 0)), out_shape=jax.ShapeDtypeStruct((m, n), out_dtype),
        name=name, compiler_params=_cparams(("parallel",), vmem),
    )(a, b_t)


def _mix_proj(xn, wm_t):
    t, d = xn.shape
    tm = _pick(t, (512, 256, 128))

    def body(x_ref, w_ref, qkv_ref, rest_ref, f_ref):
        xv = x_ref[...]
        for c0, cw in _col_chunks(D_QKV, MXU_COLS):
            qkv_ref[:, c0:c0 + cw] = _nt(xv, w_ref[c0:c0 + cw, :]).astype(qkv_ref.dtype)
        for c0, cw in _col_chunks(D_REST, MXU_COLS):
            rest_ref[:, c0:c0 + cw] = _nt(xv, w_ref[D_QKV + c0:D_QKV + c0 + cw, :])
        f_ref[...] = _nt(xv, w_ref[D_QKV + D_REST:, :])

    def rows(n):
        return pl.BlockSpec((tm, n), lambda i: (i, 0))

    vmem = 2 * (_nbytes((tm, d), BF16) + _nbytes((D_INP, d), BF16) + 3 * _nbytes((tm, D_INP), F32))
    return pl.pallas_call(
        body, grid=(t // tm,), in_specs=[rows(d), pl.BlockSpec((D_INP, d), lambda i: (0, 0))],
        out_specs=[rows(D_QKV), rows(D_REST), rows(LANES)],
        out_shape=[jax.ShapeDtypeStruct((t, D_QKV), BF16), jax.ShapeDtypeStruct((t, D_REST), F32),
                   jax.ShapeDtypeStruct((t, LANES), F32)],
        name="mix_proj", compiler_params=_cparams(("parallel",), vmem),
    )(xn, wm_t)


def _mm_tn(a, b, *, name, alpha=1.0, tm=None, into=None):
    t, m = a.shape
    n = b.shape[1]
    tm = m if tm is None else tm
    tk = _pick(t, (2048, 1024, 512, 256, 128))
    nk = t // tk

    def body(a_ref, b_ref, *rest):
        o_ref = rest[-1]
        kk = pl.program_id(1)
        p = lax.dot_general(a_ref[...], b_ref[...], (((0,), (0,)), ((), ())), preferred_element_type=F32)
        if alpha != 1.0:
            p = alpha * p
        p = p.reshape(o_ref.shape)

        @pl.when(kk == 0)
        def _():
            o_ref[...] = p

        @pl.when(kk > 0)
        def _():
            o_ref[...] += p

    vmem = 2 * (_nbytes((tk, tm), BF16) + _nbytes((tk, n), BF16) + 2 * _nbytes((tm, n), F32))
    in_specs = [pl.BlockSpec((tk, tm), lambda i, kk: (kk, i)), pl.BlockSpec((tk, n), lambda i, kk: (kk, 0))]
    cp = _cparams(("parallel", "arbitrary"), vmem)
    if into is None:
        return pl.pallas_call(
            body, grid=(m // tm, nk), in_specs=in_specs, out_specs=pl.BlockSpec((tm, n), lambda i, kk: (i, 0)),
            out_shape=jax.ShapeDtypeStruct((m, n), F32), name=name, compiler_params=cp,
        )(a, b)
    buf, buf_shape, blk, index = into
    out_spec = pl.BlockSpec(blk, lambda i, kk: index(i))
    out_shape = jax.ShapeDtypeStruct(buf_shape, F32)
    if buf is None:
        return pl.pallas_call(body, grid=(m // tm, nk), in_specs=in_specs, out_specs=out_spec, out_shape=out_shape,
                              name=name + "_new", compiler_params=cp)(a, b)
    return pl.pallas_call(
        body, grid=(m // tm, nk), in_specs=in_specs + [pl.BlockSpec(memory_space=pl.ANY)], out_specs=out_spec,
        out_shape=out_shape, input_output_aliases={2: 0}, name=name + "_into", compiler_params=cp,
    )(a, b, buf)


def _sigmoid(v):
    return 1.0 / (1.0 + jnp.exp(-v))


def _col_chunks(n, width):
    return [(c, min(width, n - c)) for c in range(0, n, width)]


def _ffn_in(xn, w_t, name, carry_gather=None):
    t, d = xn.shape
    tm = _pick(t, (1024, 512, 256, 128))
    grid = (2, t // tm)

    def body(x_ref, wg_ref, wu_ref, *rest):
        if carry_gather is None:
            h_ref, pg_ref, pu_ref = rest
        else:
            blk_ref, h_ref, pg_ref, pu_ref, gathered_ref, send_sems, recv_sems = rest
            first_step, last_step = _grid_ends([pl.program_id(a) for a in range(2)], grid)

            @pl.when(first_step)
            def _():
                _gather_start(blk_ref, gathered_ref, send_sems, recv_sems)

        xv = x_ref[...]
        for c0, cw in _col_chunks(F_HALF, MXU_COLS):
            cols = slice(c0, c0 + cw)
            g = _nt(xv, wg_ref[cols, :])
            u = _nt(xv, wu_ref[cols, :])
            s = _sigmoid(g)
            silu = g * s
            h_ref[:, cols] = (silu * u).astype(h_ref.dtype)
            pg_ref[:, cols] = (u * (s * (1.0 + g * (1.0 - s)))).astype(pg_ref.dtype)
            pu_ref[:, cols] = silu.astype(pu_ref.dtype)

        if carry_gather is not None:
            @pl.when(last_step)
            def _():
                _gather_finish(blk_ref, gathered_ref, send_sems, recv_sems)

    vmem = 2 * (_nbytes((tm, d), BF16) + 2 * _nbytes((d, F_HALF), BF16) + 4 * _nbytes((tm, D_FF), F32))
    out_blk = pl.BlockSpec((tm, F_HALF), lambda j, i: (i, j))
    sds = jax.ShapeDtypeStruct((t, D_FF), BF16)
    in_specs = [pl.BlockSpec((tm, d), lambda j, i: (i, 0)), pl.BlockSpec((F_HALF, d), lambda j, i: (j, 0)),
                pl.BlockSpec((F_HALF, d), lambda j, i: (2 + j, 0))]
    if carry_gather is None:
        return pl.pallas_call(
            body, grid=grid, in_specs=in_specs, out_specs=[out_blk, out_blk, out_blk], out_shape=[sds, sds, sds],
            name=name, compiler_params=_cparams(("parallel", "parallel"), vmem),
        )(xn, w_t, w_t)
    any_spec = pl.BlockSpec(memory_space=pl.ANY)
    return pl.pallas_call(
        body, grid=grid, in_specs=in_specs + [any_spec], out_specs=[out_blk, out_blk, out_blk, any_spec],
        out_shape=[sds, sds, sds, jax.ShapeDtypeStruct((N_DEV,) + carry_gather.shape, carry_gather.dtype)],
        scratch_shapes=list(GATHER_SEMS), name=name + "_gather",
        compiler_params=_cparams(("arbitrary", "arbitrary"), vmem),
    )(xn, w_t, w_t, carry_gather)


def _ffn_bwd_mid(dxo, w_out, pg, pu, name):
    t, d = dxo.shape
    tm = _pick(t, (1024, 512, 256, 128))

    def body(d_ref, w_ref, pg_ref, pu_ref, dg_ref, du_ref):
        dv = d_ref[...]
        for c0, cw in _col_chunks(F_HALF, MXU_COLS):
            cols = slice(c0, c0 + cw)
            dh = 0.5 * _nt(dv, w_ref[cols, :])
            dg_ref[:, cols] = (dh * pg_ref[:, cols].astype(F32)).astype(dg_ref.dtype)
            du_ref[:, cols] = (dh * pu_ref[:, cols].astype(F32)).astype(du_ref.dtype)

    vmem = 2 * (_nbytes((tm, d), BF16) + _nbytes((d, F_HALF), BF16) + 5 * _nbytes((tm, D_FF), F32))
    blk = pl.BlockSpec((tm, F_HALF), lambda j, i: (i, j))
    sds = jax.ShapeDtypeStruct((t, D_FF), BF16)
    return pl.pallas_call(
        body, grid=(2, t // tm),
        in_specs=[pl.BlockSpec((tm, d), lambda j, i: (i, 0)), pl.BlockSpec((F_HALF, d), lambda j, i: (j, 0)), blk, blk],
        out_specs=[blk, blk], out_shape=[sds, sds], name=name,
        compiler_params=_cparams(("parallel", "parallel"), vmem),
    )(dxo, w_out, pg, pu)


def _dxn_norm_bwd(parts, b, x, g, dxo, name, carry_sibling=None):
    t, d = x.shape
    k = parts[0].shape[1]
    n_parts = len(parts)
    tm = _pick(t, (256, 128))
    grid = (t // tm,)

    def body(*refs):
        a_refs, b_refs = refs[:n_parts], refs[n_parts:2 * n_parts]
        if carry_sibling is None:
            x_ref, g_ref, do_ref, dx_ref, dxb_ref, dg_ref = refs[2 * n_parts:]
        else:
            x_ref, g_ref, do_ref, g4_ref, dx_ref, dxb_ref, dg_ref, theirs_ref, send_sems, recv_sems = refs[2 * n_parts:]
            first_step, last_step = _grid_ends([pl.program_id(0)], grid)

            @pl.when(first_step)
            def _():
                _sibling_start(g4_ref, theirs_ref, send_sems, recv_sems)

        i = pl.program_id(0)
        dn = jnp.dot(a_refs[0][...], b_refs[0][...], preferred_element_type=F32)
        for a_ref, b_ref in zip(a_refs[1:], b_refs[1:]):
            dn = dn + jnp.dot(a_ref[...], b_ref[...], preferred_element_type=F32)
        xv = x_ref[...]
        r = lax.rsqrt(jnp.mean(xv * xv, axis=-1, keepdims=True) + RMS_EPS)
        xh = xv * r
        dgp = jnp.sum(dn * xh, axis=0, keepdims=True)
        dh = dn * g_ref[...]
        dx = do_ref[...] + r * (dh - xh * jnp.mean(dh * xh, axis=-1, keepdims=True))
        dx_ref[...] = dx
        dxb_ref[...] = dx.astype(dxb_ref.dtype)

        @pl.when(i == 0)
        def _():
            dg_ref[...] = dgp

        @pl.when(i > 0)
        def _():
            dg_ref[...] += dgp

        if carry_sibling is not None:
            @pl.when(last_step)
            def _():
                _sibling_finish(g4_ref, theirs_ref, send_sems, recv_sems)

    blk = pl.BlockSpec((tm, d), lambda i: (i, 0))
    row = pl.BlockSpec((1, d), lambda i: (0, 0))
    a_specs = [pl.BlockSpec((tm, k), lambda i: (i, 0)) for _ in parts]
    b_specs = [pl.BlockSpec((k, d), lambda i, kk=kk: (kk, 0)) for kk in range(n_parts)]
    vmem = 2 * n_parts * (_nbytes((tm, k), BF16) + _nbytes((k, d), BF16)) + 16 * tm * d * 4
    in_specs = a_specs + b_specs + [blk, row, blk]
    out_shape = [jax.ShapeDtypeStruct((t, d), F32), jax.ShapeDtypeStruct((t, d), BF16), jax.ShapeDtypeStruct((1, d), F32)]
    args = (*parts, *([b] * n_parts), x, g, dxo)
    if carry_sibling is None:
        return pl.pallas_call(body, grid=grid, in_specs=in_specs, out_specs=[blk, blk, row], out_shape=out_shape,
                              name=name, compiler_params=_cparams(("arbitrary",), vmem))(*args)
    nchip, _, r, c = carry_sibling.shape
    any_spec = pl.BlockSpec(memory_space=pl.ANY)
    return pl.pallas_call(
        body, grid=grid, in_specs=in_specs + [any_spec], out_specs=[blk, blk, row, any_spec],
        out_shape=out_shape + [jax.ShapeDtypeStruct((nchip, r, c), carry_sibling.dtype)],
        scratch_shapes=list(SIBLING_SEMS), name=name + "_exchange", compiler_params=_cparams(("arbitrary",), vmem),
    )(*args, carry_sibling)


def _final_loss_bwd(x, g, tgt):
    t, d = x.shape
    tm = _pick(t, (512, 256, 128))

    def body(x_ref, g_ref, t_ref, dx_ref, dxb_ref, dg_ref, loss_ref):
        i = pl.program_id(0)
        xv = x_ref[...]
        r = lax.rsqrt(jnp.mean(xv * xv, axis=-1, keepdims=True) + RMS_EPS)
        xh = xv * r
        gv = g_ref[...]
        err = xh * gv - t_ref[...]
        lp = 0.5 * jnp.sum(jnp.mean(err * err, axis=-1, keepdims=True), axis=0, keepdims=True)
        dy = err * (1.0 / d)
        dgp = jnp.sum(dy * xh, axis=0, keepdims=True)
        dh = dy * gv
        dx = r * (dh - xh * jnp.mean(dh * xh, axis=-1, keepdims=True))
        dx_ref[...] = dx
        dxb_ref[...] = dx.astype(dxb_ref.dtype)
        lpb = jnp.broadcast_to(lp, (1, LANES))

        @pl.when(i == 0)
        def _():
            dg_ref[...] = dgp
            loss_ref[...] = lpb

        @pl.when(i > 0)
        def _():
            dg_ref[...] += dgp
            loss_ref[...] += lpb

    blk = pl.BlockSpec((tm, d), lambda i: (i, 0))
    row = pl.BlockSpec((1, d), lambda i: (0, 0))
    return pl.pallas_call(
        body, grid=(t // tm,), in_specs=[blk, row, blk],
        out_specs=[blk, blk, row, pl.BlockSpec((1, LANES), lambda i: (0, 0))],
        out_shape=[jax.ShapeDtypeStruct((t, d), F32), jax.ShapeDtypeStruct((t, d), BF16),
                   jax.ShapeDtypeStruct((1, d), F32), jax.ShapeDtypeStruct((1, LANES), F32)], name="final_loss_bwd",
        compiler_params=_cparams(("arbitrary",), 16 * tm * d * 4),
    )(x, g, tgt)


def _seq_scan(v, seq, reverse):
    row = lax.broadcasted_iota(jnp.int32, v.shape, 0)
    k = 1
    while k < seq:
        if reverse:
            v = v + jnp.where(row < seq - k, pltpu.roll(v, seq - k, 0), 0.0)
        else:
            v = v + jnp.where(row >= k, pltpu.roll(v, k, 0), 0.0)
        k *= 2
    return v


def _log_sigmoid(v):
    return jnp.minimum(v, 0.0) - jnp.log(1.0 + jnp.exp(-jnp.abs(v)))


def _fox_prep(fl, bf, qkv, nb, seq):
    def body(f_ref, b_ref, q_ref, k_ref, v_ref, qa_ref, ka_ref, vm_ref):
        dsum = _seq_scan(_log_sigmoid(f_ref[...] + b_ref[...]), seq, False)
        d1 = dsum.astype(BF16).astype(F32)
        r1 = dsum - d1
        d2 = r1.astype(BF16).astype(F32)
        d3 = (r1 - d2).astype(BF16).astype(F32)
        lane = lax.broadcasted_iota(jnp.int32, (seq, LANES), 1)
        first = lane < HEAD_DIM
        l64 = jnp.where(first, lane, lane - HEAD_DIM)
        for p in range(N_PAIRS):
            def head_cols(a, p=p):
                return jnp.where(first, a[:, 2 * p:2 * p + 1], a[:, 2 * p + 1:2 * p + 2])

            e1, e2, e3 = head_cols(d1), head_cols(d2), head_cols(d3)
            aux_q = jnp.where(l64 == 0, e1, jnp.where(l64 == 1, e2, jnp.where(l64 == 2, e3,
                              jnp.where(l64 < 6, 1.0, 0.0)))).astype(BF16)
            aux_k = jnp.where(l64 < 3, 1.0, jnp.where(l64 == 3, -e1, jnp.where(l64 == 4, -e2,
                              jnp.where(l64 == 5, -e3, 0.0)))).astype(BF16)
            cols = slice(LANES * p, LANES * (p + 1))
            qs = q_ref[:, cols] * ATT_SCALE
            vp = v_ref[:, cols]
            zero = jnp.zeros_like(qs)
            qa_ref[0, p, :, :LANES] = qs
            qa_ref[0, p, :, LANES:] = aux_q
            ka_ref[0, p, :, :LANES] = k_ref[:, cols]
            ka_ref[0, p, :, LANES:] = aux_k
            vm_ref[0, p, 0] = jnp.where(first, vp, zero)
            vm_ref[0, p, 1] = jnp.where(first, zero, vp)

    def part(c):
        return pl.BlockSpec((seq, D_ATTN), lambda b, c=c: (b, c))

    return pl.pallas_call(
        body, grid=(nb,),
        in_specs=[pl.BlockSpec((seq, LANES), lambda b: (b, 0)), pl.BlockSpec((1, LANES), lambda b: (0, 0)),
                  part(0), part(1), part(2)],
        out_specs=[pl.BlockSpec((1, N_PAIRS, seq, ATT_K), lambda b: (b, 0, 0, 0)),
                   pl.BlockSpec((1, N_PAIRS, seq, ATT_K), lambda b: (b, 0, 0, 0)),
                   pl.BlockSpec((1, N_PAIRS, 2, seq, LANES), lambda b: (b, 0, 0, 0, 0))],
        out_shape=[jax.ShapeDtypeStruct((nb, N_PAIRS, seq, ATT_K), BF16),
                   jax.ShapeDtypeStruct((nb, N_PAIRS, seq, ATT_K), BF16),
                   jax.ShapeDtypeStruct((nb, N_PAIRS, 2, seq, LANES), BF16)],
        name="fox_prep", compiler_params=_cparams(("parallel",), 48 * 1024 * 1024),
    )(fl, bf, qkv, qkv, qkv)


def _fox_prep_bwd(dd, fl, bf, seq):
    t = fl.shape[0]

    def body(d_ref, f_ref, b_ref, o_ref, db_ref):
        i = pl.program_id(0)
        dlog = _seq_scan(d_ref[...], seq, True)
        dfl = dlog * _sigmoid(-(f_ref[...] + b_ref[...]))
        o_ref[...] = dfl.astype(o_ref.dtype)
        dbp = jnp.sum(dfl, axis=0, keepdims=True)

        @pl.when(i == 0)
        def _():
            db_ref[...] = dbp

        @pl.when(i > 0)
        def _():
            db_ref[...] += dbp

    blk = pl.BlockSpec((seq, LANES), lambda b: (b, 0))
    row = pl.BlockSpec((1, LANES), lambda b: (0, 0))
    return pl.pallas_call(
        body, grid=(t // seq,), in_specs=[blk, blk, row], out_specs=[blk, row],
        out_shape=[jax.ShapeDtypeStruct((t, LANES), BF16), jax.ShapeDtypeStruct((1, LANES), F32)], name="fox_prep_bwd",
        compiler_params=_cparams(("arbitrary",), 24 * seq * LANES * 4),
    )(dd, fl, bf)


def _stack_heads(qc):
    lane = lax.broadcasted_iota(jnp.int32, qc.shape, 1)
    first = (lane & HEAD_DIM) == 0
    zero = jnp.zeros_like(qc)
    return jnp.concatenate([jnp.where(first, qc, zero), jnp.where(first, zero, qc)], axis=0)


def _pair_rows(a, ta):
    lane = lax.broadcasted_iota(jnp.int32, (ta, LANES), 1)
    return jnp.where(lane < HEAD_DIM, a[:ta], a[ta:])


def _diag_mask(ta):
    r = lax.broadcasted_iota(jnp.int32, (2 * ta, ta), 0)
    c = lax.broadcasted_iota(jnp.int32, (2 * ta, ta), 1)
    return c <= jnp.where(r >= ta, r - ta, r)


def _nt(a, b):
    return lax.dot_general(a, b, (((1,), (1,)), ((), ())), preferred_element_type=F32)


def _tn(a, b):
    return lax.dot_general(a, b, (((0,), (0,)), ((), ())), preferred_element_type=F32)


def _grid_ends(ids, sizes):
    first = functools.reduce(jnp.logical_and, [i == 0 for i in ids])
    last = functools.reduce(jnp.logical_and, [i == n - 1 for i, n in zip(ids, sizes)])
    return first, last


def _grid_step_is(ids, sizes, step):
    linear = ids[0]
    for i, n in zip(ids[1:], sizes[1:]):
        linear = linear * n + i
    return linear == step


GATHER_RELAY_LEAD = 3


def _fox_fwd(qa, ka, vm, nb, seq, ta, carry_gather=None):
    nq = seq // ta
    npp = ATT_PAIRS_FWD
    grid = (nb, N_PAIRS // npp, nq)

    def body(q_ref, k_ref, v_ref, *rest):
        if carry_gather is None:
            o_ref, lse_ref = rest
        else:
            x_ref, o_ref, lse_ref, gathered_ref, send_sems, recv_sems = rest
            first_step, last_step = _grid_ends([pl.program_id(a) for a in range(3)], grid)

            @pl.when(first_step)
            def _():
                _gather_start(x_ref, gathered_ref, send_sems, recv_sems)

        i = pl.program_id(2)
        q2s = [_stack_heads(q_ref[0, pp]) for pp in range(npp)]

        def step(j, carry, masked):
            rows = pl.ds(pl.multiple_of(j * ta, ta), ta)
            out = []
            for pp in range(npp):
                m, l, acc = carry[pp]
                s = _nt(q2s[pp], k_ref[0, pp, rows, :])
                if masked:
                    s = jnp.where(_diag_mask(ta), s, NEG_BIG)
                m_new, l_new, corr, pv = [], [], [], None
                for h in range(2):
                    s_h = s[h * ta:(h + 1) * ta]
                    m_new.append(jnp.maximum(m[h], jnp.max(s_h, axis=-1, keepdims=True)))
                    p_h = jnp.exp(s_h - m_new[h])
                    corr.append(jnp.exp(m[h] - m_new[h]))
                    l_new.append(corr[h] * l[h] + jnp.sum(p_h, axis=-1, keepdims=True))
                    pv_h = jnp.dot(p_h.astype(BF16), v_ref[0, pp, h, rows, :], preferred_element_type=F32)
                    pv = pv_h if pv is None else pv + pv_h
                first = lax.broadcasted_iota(jnp.int32, (ta, LANES), 1) < HEAD_DIM
                out.append((tuple(m_new), tuple(l_new), jnp.where(first, corr[0], corr[1]) * acc + pv))
            return tuple(out)

        def per_head(value):
            return (jnp.full((ta, 1), value, F32), jnp.full((ta, 1), value, F32))

        init = tuple((per_head(NEG_BIG), per_head(0.0), jnp.zeros((ta, LANES), F32)) for _ in range(npp))
        carry = lax.fori_loop(0, i, functools.partial(step, masked=False), init)
        first = lax.broadcasted_iota(jnp.int32, (ta, LANES), 1) < HEAD_DIM
        for pp, (m, l, acc) in enumerate(step(i, carry, True)):
            o_ref[:, LANES * pp:LANES * (pp + 1)] = (acc * jnp.where(first, 1.0 / l[0], 1.0 / l[1])).astype(o_ref.dtype)
            lse_ref[0, pp, 0] = m[0] + jnp.log(l[0])
            lse_ref[0, pp, 1] = m[1] + jnp.log(l[1])

        if carry_gather is not None:
            n_steps = grid[0] * grid[1] * grid[2]
            relay_at = n_steps - 1 - min(GATHER_RELAY_LEAD, n_steps - 1)

            @pl.when(_grid_step_is([pl.program_id(a) for a in range(3)], grid, relay_at))
            def _():
                _gather_relay(x_ref, gathered_ref, send_sems, recv_sems)

            @pl.when(last_step)
            def _():
                _gather_drain(x_ref, gathered_ref, send_sems, recv_sems)

    vmem = (2 * npp * (_nbytes((seq, ATT_K), BF16) + 2 * _nbytes((seq, LANES), BF16)) + 24 * npp * ta * ta * 4
            + 8 * 1024 * 1024)
    in_specs = [pl.BlockSpec((1, npp, ta, ATT_K), lambda b, g, i: (b, g, i, 0)),
                pl.BlockSpec((1, npp, seq, ATT_K), lambda b, g, i: (b, g, 0, 0)),
                pl.BlockSpec((1, npp, 2, seq, LANES), lambda b, g, i: (b, g, 0, 0, 0))]
    out_specs = [pl.BlockSpec((ta, LANES * npp), lambda b, g, i: (b * nq + i, g)),
                 pl.BlockSpec((1, npp, 2, ta, 1), lambda b, g, i: (b, g, 0, i, 0))]
    out_shape = [jax.ShapeDtypeStruct((nb * seq, D_ATTN), BF16), jax.ShapeDtypeStruct((nb, N_PAIRS, 2, seq, 1), F32)]
    if carry_gather is None:
        return pl.pallas_call(
            body, grid=grid, in_specs=in_specs, out_specs=out_specs, out_shape=out_shape, name="fox_fwd",
            compiler_params=_cparams(("parallel", "parallel", "parallel"), vmem),
        )(qa, ka, vm)
    any_spec = pl.BlockSpec(memory_space=pl.ANY)
    return pl.pallas_call(
        body, grid=grid, in_specs=in_specs + [any_spec], out_specs=out_specs + [any_spec],
        out_shape=out_shape + [jax.ShapeDtypeStruct((N_DEV,) + carry_gather.shape, carry_gather.dtype)],
        scratch_shapes=list(GATHER_SEMS), name="fox_fwd_gather",
        compiler_params=_cparams(("arbitrary", "arbitrary", "arbitrary"), vmem),
    )(qa, ka, vm, carry_gather)


def _fox_bwd(qa, ka, vm, y, dy, lse, nb, seq, ta, carry_exchange=None):
    nq = seq // ta
    npp = ATT_PAIRS_BWD
    grid = (nb, N_PAIRS // npp, nq)

    def body(q_ref, k_ref, v_ref, o_ref, do_ref, lse_ref, *rest):
        if carry_exchange is None:
            dq_ref, dk_ref, dv_ref, rs_ref, cs_ref = rest
        else:
            t_ref, dq_ref, dk_ref, dv_ref, rs_ref, cs_ref, landed_ref, send_sems, recv_sems = rest
            first_step, last_step = _grid_ends([pl.program_id(a) for a in range(3)], grid)

            @pl.when(first_step)
            def _():
                _chips_start(t_ref, landed_ref, send_sems, recv_sems)

        i = pl.program_id(2)

        @pl.when(i == 0)
        def _():
            dk_ref[...] = jnp.zeros_like(dk_ref)
            dv_ref[...] = jnp.zeros_like(dv_ref)
            cs_ref[...] = jnp.zeros_like(cs_ref)

        first = lax.broadcasted_iota(jnp.int32, (ta, LANES), 1) < HEAD_DIM
        q2s, do2s, deltas, lses = [], [], [], []
        for pp in range(npp):
            cols = slice(LANES * pp, LANES * (pp + 1))
            q2s.append(_stack_heads(q_ref[0, pp]))
            do = do_ref[:, cols]
            doo = do * o_ref[:, cols].astype(F32)
            do2s.append(jnp.concatenate([jnp.where(first, do, 0.0), jnp.where(first, 0.0, do)], axis=0).astype(BF16))
            deltas.append(jnp.concatenate([jnp.sum(jnp.where(first, doo, 0.0), axis=-1, keepdims=True),
                                           jnp.sum(jnp.where(first, 0.0, doo), axis=-1, keepdims=True)], axis=0))
            lses.append(jnp.concatenate([lse_ref[0, pp, 0], lse_ref[0, pp, 1]], axis=0))

        def step(j, carry, masked):
            rows = pl.ds(pl.multiple_of(j * ta, ta), ta)
            out = []
            for pp in range(npp):
                dq_acc, rs_acc = carry[pp]
                cols = slice(LANES * pp, LANES * (pp + 1))
                ks = k_ref[0, pp, rows, :]
                s = _nt(q2s[pp], ks)
                if masked:
                    s = jnp.where(_diag_mask(ta), s, NEG_BIG)
                p = jnp.exp(s - lses[pp])
                dp = _nt(do2s[pp], v_ref[0, pp, 0, rows, :] + v_ref[0, pp, 1, rows, :])
                ds32 = p * (dp - deltas[pp])
                ds = ds32.astype(BF16)
                dk_ref[rows, cols] += _tn(ds, q2s[pp][:, :LANES])
                dv_ref[rows, cols] += _tn(p.astype(BF16), do2s[pp])
                cs_ref[0, pp, 0, j] += jnp.sum(ds32[:ta], axis=0, keepdims=True)
                cs_ref[0, pp, 1, j] += jnp.sum(ds32[ta:], axis=0, keepdims=True)
                out.append((dq_acc + jnp.dot(ds, ks[:, :LANES], preferred_element_type=F32),
                            rs_acc + jnp.sum(ds32, axis=-1, keepdims=True)))
            return tuple(out)

        init = tuple((jnp.zeros((2 * ta, LANES), F32), jnp.zeros((2 * ta, 1), F32)) for _ in range(npp))
        carry = lax.fori_loop(0, i, functools.partial(step, masked=False), init)
        for pp, (dq_acc, rs_acc) in enumerate(step(i, carry, True)):
            dq = jnp.where(first, dq_acc[:ta], dq_acc[ta:]) * ATT_SCALE
            dq_ref[:, LANES * pp:LANES * (pp + 1)] = dq.astype(dq_ref.dtype)
            rs_row = jnp.transpose(jnp.broadcast_to(rs_acc, (2 * ta, LANES)))[0:1]
            rs_ref[0, pp, 0, 0] = rs_row[:, :ta]
            rs_ref[0, pp, 1, 0] = rs_row[:, ta:]

        if carry_exchange is not None:
            @pl.when(last_step)
            def _():
                _chips_finish(t_ref, landed_ref, send_sems, recv_sems)

    vmem = (2 * npp * (_nbytes((seq, ATT_K), BF16) + 2 * _nbytes((seq, LANES), BF16) + 2 * _nbytes((seq, LANES), F32))
            + 32 * npp * ta * ta * 4 + 8 * 1024 * 1024)
    qblk = lambda b, g, i: (b * nq + i, g)
    acc_blk = pl.BlockSpec((seq, LANES * npp), lambda b, g, i: (b, g))
    in_specs = [pl.BlockSpec((1, npp, ta, ATT_K), lambda b, g, i: (b, g, i, 0)),
                pl.BlockSpec((1, npp, seq, ATT_K), lambda b, g, i: (b, g, 0, 0)),
                pl.BlockSpec((1, npp, 2, seq, LANES), lambda b, g, i: (b, g, 0, 0, 0)),
                pl.BlockSpec((ta, LANES * npp), qblk), pl.BlockSpec((ta, LANES * npp), qblk),
                pl.BlockSpec((1, npp, 2, ta, 1), lambda b, g, i: (b, g, 0, i, 0))]
    out_specs = [pl.BlockSpec((ta, LANES * npp), qblk), acc_blk, acc_blk,
                 pl.BlockSpec((1, npp, 2, 1, 1, ta), lambda b, g, i: (b, g, 0, i, 0, 0)),
                 pl.BlockSpec((1, npp, 2, nq, 1, ta), lambda b, g, i: (b, g, 0, 0, 0, 0))]
    sums = jax.ShapeDtypeStruct((nb, N_PAIRS, 2, nq, 1, ta), F32)
    out_shape = [jax.ShapeDtypeStruct((nb * seq, D_ATTN), BF16), jax.ShapeDtypeStruct((nb * seq, D_ATTN), F32),
                 jax.ShapeDtypeStruct((nb * seq, D_ATTN), F32), sums, sums]
    if carry_exchange is None:
        return pl.pallas_call(
            body, grid=grid, in_specs=in_specs, out_specs=out_specs, out_shape=out_shape, name="fox_bwd",
            compiler_params=_cparams(("parallel", "parallel", "arbitrary"), vmem),
        )(qa, ka, vm, y, dy, lse)
    any_spec = pl.BlockSpec(memory_space=pl.ANY)
    return pl.pallas_call(
        body, grid=grid, in_specs=in_specs + [any_spec], out_specs=out_specs + [any_spec],
        out_shape=out_shape + [jax.ShapeDtypeStruct(carry_exchange.shape, carry_exchange.dtype)],
        scratch_shapes=list(CHIPS_SEMS), name="fox_bwd_exchange",
        compiler_params=_cparams(("arbitrary", "arbitrary", "arbitrary"), vmem),
    )(qa, ka, vm, y, dy, lse, carry_exchange)


def _shift_down(a, k):
    row = lax.broadcasted_iota(jnp.int32, a.shape, 0)
    return jnp.where(row >= k, pltpu.roll(a, k, 0), 0.0)


def _shift_up(a, k):
    n = a.shape[0]
    row = lax.broadcasted_iota(jnp.int32, a.shape, 0)
    return jnp.where(row < n - k, pltpu.roll(a, n - k, 0), 0.0)


def _by_group(vals, shape):
    lane = lax.broadcasted_iota(jnp.int32, shape, 1)
    out = vals[-1]
    for gi in range(len(vals) - 2, -1, -1):
        out = jnp.where(lane < POOL_GROUP * (gi + 1), vals[gi], out)
    return out


def _pooled(u):
    s2 = u + _shift_down(u, 1)
    s4 = s2 + _shift_down(s2, 2)
    s8 = s4 + _shift_down(s4, 4)
    s16 = s8 + _shift_down(s8, 8)
    win = _by_group([s2, s4, s8, s16], u.shape)
    row = lax.broadcasted_iota(jnp.int32, u.shape, 0)
    wsize = _by_group([jnp.full(u.shape, w, jnp.int32) for w in POOL_WINDOWS], u.shape)
    inv = 1.0 / jnp.minimum(row + 1, wsize).astype(F32)
    return win * inv - u, inv


def _pool_fwd(rest, wbd, scale, seq):
    t = rest.shape[0]

    def body(u_ref, w_ref, s_ref, o_ref):
        pooled, _ = _pooled(u_ref[...])
        pw = jnp.dot(pooled.astype(BF16), w_ref[...], preferred_element_type=F32)
        o_ref[...] = (pw * s_ref[...]).astype(o_ref.dtype)

    blk = pl.BlockSpec((seq, D_POOL), lambda b: (b, 0))
    return pl.pallas_call(
        body, grid=(t // seq,),
        in_specs=[blk, pl.BlockSpec((D_POOL, D_POOL), lambda b: (0, 0)), pl.BlockSpec((1, D_POOL), lambda b: (0, 0))],
        out_specs=blk, out_shape=jax.ShapeDtypeStruct((t, D_POOL), BF16), name="pool_fwd",
        compiler_params=_cparams(("parallel",), 24 * seq * D_POOL * 4),
    )(rest, wbd, scale)


def _pool_bwd(rest, dy, wbd, wbd_t, scale, seq):
    t = rest.shape[0]

    def body(u_ref, dy_ref, w_ref, wt_ref, s_ref, du_ref, dw_ref, dsc_ref):
        i = pl.program_id(0)
        pooled, inv = _pooled(u_ref[...])
        pb = pooled.astype(BF16)
        pw = jnp.dot(pb, w_ref[...], preferred_element_type=F32)
        dyp = dy_ref[...]
        dsp = jnp.sum(dyp * pw, axis=0, keepdims=True)
        dpw = (dyp * s_ref[...]).astype(BF16)
        dwp = _tn(pb, dpw)
        dpooled = jnp.dot(dpw, wt_ref[...], preferred_element_type=F32)
        dwin = dpooled * inv
        t2 = dwin + _shift_up(dwin, 1)
        t4 = t2 + _shift_up(t2, 2)
        t8 = t4 + _shift_up(t4, 4)
        t16 = t8 + _shift_up(t8, 8)
        du_ref[...] = (_by_group([t2, t4, t8, t16], dwin.shape) - dpooled).astype(du_ref.dtype)

        @pl.when(i == 0)
        def _():
            dw_ref[...] = dwp
            dsc_ref[...] = dsp

        @pl.when(i > 0)
        def _():
            dw_ref[...] += dwp
            dsc_ref[...] += dsp

    blk = pl.BlockSpec((seq, D_POOL), lambda b: (b, 0))
    sq = pl.BlockSpec((D_POOL, D_POOL), lambda b: (0, 0))
    row = pl.BlockSpec((1, D_POOL), lambda b: (0, 0))
    return pl.pallas_call(
        body, grid=(t // seq,),
        in_specs=[blk, pl.BlockSpec((seq, D_POOL), lambda b: (b, 2)), sq, sq, row],
        out_specs=[blk, sq, row],
        out_shape=[jax.ShapeDtypeStruct((t, D_POOL), BF16), jax.ShapeDtypeStruct((D_POOL, D_POOL), F32),
                   jax.ShapeDtypeStruct((1, D_POOL), F32)], name="pool_bwd",
        compiler_params=_cparams(("arbitrary",), 40 * seq * D_POOL * 4),
    )(rest, dy, wbd, wbd_t, scale)


def _conv_fwd(rest, cw, seq):
    t = rest.shape[0]

    def body(cb_ref, cc_ref, ch_ref, w_ref, o_ref):
        u = cc_ref[...] * ch_ref[...]
        y = w_ref[0:1, :] * _shift_down(u, 2) + w_ref[1:2, :] * _shift_down(u, 1) + w_ref[2:3, :] * u
        o_ref[...] = (cb_ref[...] * y).astype(o_ref.dtype)

    def col(c):
        return pl.BlockSpec((seq, D_CONV), lambda b, c=c: (b, c))

    return pl.pallas_call(
        body, grid=(t // seq,), in_specs=[col(1), col(2), col(3), pl.BlockSpec((8, D_CONV), lambda b: (0, 0))],
        out_specs=pl.BlockSpec((seq, D_CONV), lambda b: (b, 0)),
        out_shape=jax.ShapeDtypeStruct((t, D_CONV), BF16), name="conv_fwd",
        compiler_params=_cparams(("parallel",), 24 * seq * D_CONV * 4),
    )(rest, rest, rest, cw)


def _conv_bwd(rest, dy, cw, seq):
    t = rest.shape[0]

    def body(cb_ref, cc_ref, ch_ref, dy_ref, w_ref, o_ref, dw_ref):
        i = pl.program_id(0)
        cc = cc_ref[...]
        ch = ch_ref[...]
        u = cc * ch
        u1 = _shift_down(u, 1)
        u2 = _shift_down(u, 2)
        y = w_ref[0:1, :] * u2 + w_ref[1:2, :] * u1 + w_ref[2:3, :] * u
        dyc = dy_ref[...]
        d2 = dyc * cb_ref[...]
        du = w_ref[0:1, :] * _shift_up(d2, 2) + w_ref[1:2, :] * _shift_up(d2, 1) + w_ref[2:3, :] * d2
        o_ref[:, 0:D_CONV] = (dyc * y).astype(o_ref.dtype)
        o_ref[:, D_CONV:2 * D_CONV] = (du * ch).astype(o_ref.dtype)
        o_ref[:, 2 * D_CONV:3 * D_CONV] = (du * cc).astype(o_ref.dtype)
        tap = lax.broadcasted_iota(jnp.int32, (8, D_CONV), 0)
        dwp = jnp.where(tap == 0, jnp.sum(d2 * u2, axis=0, keepdims=True),
                        jnp.where(tap == 1, jnp.sum(d2 * u1, axis=0, keepdims=True),
                                  jnp.where(tap == 2, jnp.sum(d2 * u, axis=0, keepdims=True), 0.0)))

        @pl.when(i == 0)
        def _():
            dw_ref[...] = dwp

        @pl.when(i > 0)
        def _():
            dw_ref[...] += dwp

    def col(c):
        return pl.BlockSpec((seq, D_CONV), lambda b, c=c: (b, c))

    taps = pl.BlockSpec((8, D_CONV), lambda b: (0, 0))
    return pl.pallas_call(
        body, grid=(t // seq,), in_specs=[col(1), col(2), col(3), col(3), taps],
        out_specs=[pl.BlockSpec((seq, 3 * D_CONV), lambda b: (b, 0)), taps],
        out_shape=[jax.ShapeDtypeStruct((t, 3 * D_CONV), BF16), jax.ShapeDtypeStruct((8, D_CONV), F32)],
        name="conv_bwd", compiler_params=_cparams(("arbitrary",), 48 * seq * D_CONV * 4),
    )(rest, rest, rest, dy, cw)


def _adamw(w, g, m, v, name):
    r, c = w.shape
    tr = _pick(r, (512, 352, 256, 128)) if r > 512 else r

    def body(w_ref, g_ref, m_ref, v_ref, d_ref, mo_ref, vo_ref):
        gv = g_ref[...]
        mn = ADAM_B1 * m_ref[...] + (1.0 - ADAM_B1) * gv
        vn = ADAM_B2 * v_ref[...] + (1.0 - ADAM_B2) * (gv * gv)
        m_hat = mn / (1.0 - ADAM_B1 ** ADAM_STEP)
        v_hat = vn / (1.0 - ADAM_B2 ** ADAM_STEP)
        d_ref[...] = -ADAM_LR * (m_hat / (jnp.sqrt(v_hat) + ADAM_EPS) + ADAM_WD * w_ref[...])
        mo_ref[...] = mn
        vo_ref[...] = vn

    blk = pl.BlockSpec((tr, c), lambda i: (i, 0))
    sds = jax.ShapeDtypeStruct((r, c), F32)
    return pl.pallas_call(
        body, grid=(r // tr,), in_specs=[blk] * 4, out_specs=[blk] * 3, out_shape=[sds] * 3, name=name,
        compiler_params=_cparams(("parallel",), 20 * tr * max(c, LANES) * 4),
    )(w, g, m, v)


def _sum_slots(a, name):
    ns, r, c = a.shape
    tr = _pick(r, (384, 368, 256, 184, 136, 128, 88, 8))

    def body(a_ref, o_ref):
        acc = a_ref[0].astype(F32)
        for s in range(1, ns):
            acc = acc + a_ref[s].astype(F32)
        o_ref[...] = acc

    return pl.pallas_call(
        body, grid=(r // tr,), in_specs=[pl.BlockSpec((ns, tr, c), lambda i: (0, i, 0))],
        out_specs=pl.BlockSpec((tr, c), lambda i: (i, 0)), out_shape=jax.ShapeDtypeStruct((r, c), F32), name=name,
        compiler_params=_cparams(("parallel",), 4 * (ns + 2) * tr * c * 4),
    )(a)


def _add_core_half(core, g4, theirs, out_dtype, name):
    ns, _, r, c = g4.shape
    tr = _pick(r, (384, 368, 256, 184, 136, 128, 88, 8))

    def body(core_ref, a_ref, b_ref, o_ref):
        o_ref[...] = (a_ref[0] + b_ref[...]).astype(o_ref.dtype)

    blk = pl.BlockSpec((1, tr, c), lambda s, i, core_ref: (s, i, 0))
    return pl.pallas_call(
        body,
        grid_spec=pltpu.PrefetchScalarGridSpec(
            num_scalar_prefetch=1, grid=(ns, r // tr),
            in_specs=[pl.BlockSpec((1, 1, tr, c), lambda s, i, core_ref: (s, core_ref[0], i, 0)), blk],
            out_specs=blk),
        out_shape=jax.ShapeDtypeStruct((ns, r, c), out_dtype), name=name,
        compiler_params=_cparams(("parallel", "parallel"), 10 * tr * c * 4),
    )(core, g4, theirs)


def _sum_chips(order, own, landed, name):
    ns, r, c = own.shape
    tr = _pick(r, (384, 368, 256, 184, 136, 128, 88, 8))

    def body(order_ref, a_ref, b1_ref, b2_ref, b3_ref, o_ref):
        o_ref[...] = ((a_ref[0].astype(F32) + b1_ref[0].astype(F32)) + b2_ref[0].astype(F32)) + b3_ref[0].astype(F32)

    def slot(k):
        return pl.BlockSpec((1, tr, c), lambda i, order_ref, k=k: (order_ref[k], i, 0))

    return pl.pallas_call(
        body,
        grid_spec=pltpu.PrefetchScalarGridSpec(
            num_scalar_prefetch=1, grid=(r // tr,), in_specs=[slot(0), slot(1), slot(2), slot(3)],
            out_specs=pl.BlockSpec((tr, c), lambda i, order_ref: (i, 0))),
        out_shape=jax.ShapeDtypeStruct((r, c), F32), name=name,
        compiler_params=_cparams(("parallel",), 16 * tr * c * 4),
    )(order, own, landed, landed, landed)


def _mesh_pos():
    return lax.axis_index("x"), lax.axis_index("y"), lax.axis_index("c")


def _comm_call(name, gathers=(), chips=None):
    payloads = list(gathers) + ([] if chips is None else [chips])
    n = len(payloads)

    def body(*refs):
        ins, outs, sems = refs[:n], refs[n:2 * n], refs[2 * n:]
        jobs = [(_gather_start, _gather_finish)] * len(gathers) + ([] if chips is None else [(_chips_start, _chips_finish)])
        for k, (start, _) in enumerate(jobs):
            start(ins[k], outs[k], sems[2 * k], sems[2 * k + 1])
        for k, (_, finish) in enumerate(jobs):
            finish(ins[k], outs[k], sems[2 * k], sems[2 * k + 1])

    any_spec = pl.BlockSpec(memory_space=pl.ANY)
    out_shape = [jax.ShapeDtypeStruct((N_DEV,) + x.shape, x.dtype) for x in gathers]
    sems = list(GATHER_SEMS) * len(gathers)
    if chips is not None:
        out_shape.append(jax.ShapeDtypeStruct(chips.shape, chips.dtype))
        sems += list(CHIPS_SEMS)
    outs = pl.pallas_call(body, out_shape=out_shape, in_specs=[any_spec] * n, out_specs=[any_spec] * n,
                          scratch_shapes=sems, name=name)(*payloads)
    return [_fill_own_slot(o, x) for o, x in zip(outs, gathers)] + ([] if chips is None else [outs[-1]])


GATHER_SEMS = (pltpu.SemaphoreType.DMA((7,)), pltpu.SemaphoreType.DMA((7,)))


def _fill_own_slot(gathered, x):
    mx, my, mc = _mesh_pos()
    return lax.dynamic_update_slice_in_dim(gathered, x[None], 4 * mx + 2 * my + mc, axis=0)


def _gather_copies(x_ref, out_ref, send_sems, recv_sems):
    mx, my, mc = _mesh_pos()
    me, sibling = (mx, my, mc), (mx, my, 1 - mc)
    chips = [(1 - mx, my), (mx, 1 - my), (1 - mx, 1 - my)]

    def slot(px, py, pc):
        return out_ref.at[4 * px + 2 * py + pc]

    def copy(k, block, to, src=None):
        return pltpu.make_async_remote_copy(
            src_ref=slot(*block) if src is None else src, dst_ref=slot(*block),
            send_sem=send_sems.at[k], recv_sem=recv_sems.at[k],
            device_id=to, device_id_type=pl.DeviceIdType.MESH)

    first = [copy(0, me, sibling, src=x_ref)]
    first += [copy(1 + j, me, (*chip, mc), src=x_ref) for j, chip in enumerate(chips)]
    passed = [copy(4 + j, (*chip, mc), sibling) for j, chip in enumerate(chips)]
    over_ici = [copy(1 + j, (*chip, mc), me) for j, chip in enumerate(chips)]
    over_d2d = [copy(0, sibling, me)] + [copy(4 + j, (*chip, 1 - mc), me) for j, chip in enumerate(chips)]
    return first, passed, over_ici, over_d2d


def _gather_start(x_ref, out_ref, send_sems, recv_sems):
    for cp in _gather_copies(x_ref, out_ref, send_sems, recv_sems)[0]:
        cp.start()


def _gather_relay(x_ref, out_ref, send_sems, recv_sems):
    _, passed, over_ici, _ = _gather_copies(x_ref, out_ref, send_sems, recv_sems)
    for landed, relay in zip(over_ici, passed):
        landed.wait_recv()
        relay.start()


def _gather_drain(x_ref, out_ref, send_sems, recv_sems):
    first, passed, _, over_d2d = _gather_copies(x_ref, out_ref, send_sems, recv_sems)
    for landed in over_d2d:
        landed.wait_recv()
    for cp in first + passed:
        cp.wait_send()


def _gather_finish(x_ref, out_ref, send_sems, recv_sems):
    _gather_relay(x_ref, out_ref, send_sems, recv_sems)
    _gather_drain(x_ref, out_ref, send_sems, recv_sems)


SIBLING_SEMS = (pltpu.SemaphoreType.DMA((N_CHIPS,)), pltpu.SemaphoreType.DMA((N_CHIPS,)))


def _sibling_copies(g_ref, theirs_ref, send_sems, recv_sems):
    mx, my, mc = _mesh_pos()
    return [pltpu.make_async_remote_copy(
        src_ref=g_ref.at[chip, 1 - mc], dst_ref=theirs_ref.at[chip],
        send_sem=send_sems.at[chip], recv_sem=recv_sems.at[chip],
        device_id=(mx, my, 1 - mc), device_id_type=pl.DeviceIdType.MESH) for chip in range(N_CHIPS)]


def _sibling_start(g_ref, theirs_ref, send_sems, recv_sems):
    for cp in _sibling_copies(g_ref, theirs_ref, send_sems, recv_sems):
        cp.start()


def _sibling_finish(g_ref, theirs_ref, send_sems, recv_sems):
    copies = _sibling_copies(g_ref, theirs_ref, send_sems, recv_sems)
    for cp in copies:
        cp.wait_recv()
    for cp in copies:
        cp.wait_send()


CHIPS_SEMS = (pltpu.SemaphoreType.DMA((N_CHIPS - 1,)), pltpu.SemaphoreType.DMA((N_CHIPS - 1,)))


def _chips_copies(t_ref, out_ref, send_sems, recv_sems):
    mx, my, mc = _mesh_pos()
    my_chip = 2 * mx + my
    copies = []
    for k in range(1, N_CHIPS):
        px = 1 - mx if k & 2 else mx
        py = 1 - my if k & 1 else my
        peer_chip = 2 * px + py

        def rdma(dst_slot, px=px, py=py, peer_chip=peer_chip, k=k):
            return pltpu.make_async_remote_copy(
                src_ref=t_ref.at[peer_chip], dst_ref=out_ref.at[dst_slot],
                send_sem=send_sems.at[k - 1], recv_sem=recv_sems.at[k - 1],
                device_id=(px, py, mc), device_id_type=pl.DeviceIdType.MESH)

        copies.append((rdma(my_chip), rdma(peer_chip)))
    return copies


def _chips_start(t_ref, out_ref, send_sems, recv_sems):
    for send, _ in _chips_copies(t_ref, out_ref, send_sems, recv_sems):
        send.start()


def _chips_finish(t_ref, out_ref, send_sems, recv_sems):
    copies = _chips_copies(t_ref, out_ref, send_sems, recv_sems)
    for _, landed in copies:
        landed.wait_recv()
    for send, _ in copies:
        send.wait_send()


def _add_sibling(g4, theirs):
    core = jnp.reshape(lax.axis_index("c"), (1,)).astype(jnp.int32)
    return _add_core_half(core, g4, theirs, BF16, name="add_sibling_grads")


def _sum_landed(chip_sums, landed):
    mx, my, _ = _mesh_pos()
    order = jnp.stack([2 * mx + my, 2 * (1 - mx) + my, 2 * mx + (1 - my), 2 * (1 - mx) + (1 - my)]).astype(jnp.int32)
    return _sum_chips(order, chip_sums, landed, name="sum_grads")


def _perm_mix_rows(wt):
    f0 = D_QKV
    f1 = f0 + N_HEADS
    return jnp.concatenate([wt[:f0], wt[f1:], jnp.pad(wt[f0:f1], ((0, LANES - N_HEADS), (0, 0)))], axis=0)


def _unperm_mix_rows(gt):
    f0 = D_QKV
    return jnp.concatenate([gt[:f0], gt[f0 + D_REST:f0 + D_REST + N_HEADS], gt[f0:f0 + D_REST]], axis=0)


def _pack_shards(parts, l, dtype):
    w1i, w1o, wmi, wmo, w2i, w2o = parts
    rows = [w1i[l].T, w1o[l], jnp.pad(wmi[l].T, ((0, MIX_ROWS_PAD - MIX_ROWS), (0, 0))), wmo[l], w2i[l].T, w2o[l]]
    return jnp.concatenate(rows, axis=0).astype(dtype)


PACK_HEAD = FFN_ROWS + OUT_ROWS


def _ffn_weights(wg, o):
    return dict(wi_t=wg[:, o:o + FFN_ROWS].reshape(2 * D_FF, D_MODEL),
                wo=wg[:, o + FFN_ROWS:o + FFN_ROWS + OUT_ROWS].reshape(D_FF, D_MODEL))


def _tail_weights(wg):
    mix = dict(wm_t=_perm_mix_rows(wg[:, :MIX_ROWS].reshape(D_IN, D_MODEL)),
               wo=wg[:, MIX_ROWS_PAD:MIX_ROWS_PAD + MO_ROWS].reshape(D_MODEL, D_MODEL))
    return mix, _ffn_weights(wg, MIX_ROWS_PAD + MO_ROWS)


GRAD_AT = dict(f1i=0, f2i=FFN_ROWS, f1o=4 * OUT_ROWS, f2o=5 * OUT_ROWS, mi=6 * OUT_ROWS, mo=20 * MO_ROWS)
GRAD_ROWS = GRAD_AT["mo"] + MO_ROWS
GRAD_SHAPE = (N_CHIPS, 2, GRAD_ROWS, D_MODEL)


def _into_ffn_in(buf, tag, half):
    rb = GRAD_AT[tag] // FFN_ROWS
    return (buf, GRAD_SHAPE, (1, 2, FFN_ROWS, D_MODEL), lambda i: (2 * half + i, 0, rb, 0))


def _into_ffn_out(buf, tag):
    rb = GRAD_AT[tag] // OUT_ROWS
    return (buf, GRAD_SHAPE, (2, 2, OUT_ROWS, D_MODEL), lambda i: (i, 0, rb, 0))


def _into_mix_out(buf):
    rb = GRAD_AT["mo"] // MO_ROWS
    return (buf, GRAD_SHAPE, (N_CHIPS, 2, MO_ROWS, D_MODEL), lambda i: (0, 0, rb, 0))


def _put_mix_in(buf, g_in_t):
    gmi = _unperm_mix_rows(g_in_t).reshape(N_DEV, MIX_ROWS, D_MODEL)
    gmi = jnp.pad(gmi, ((0, 0), (0, OUT_ROWS - MIX_ROWS), (0, 0))).reshape(N_CHIPS, 2, OUT_ROWS, D_MODEL)
    return lax.dynamic_update_slice(buf, gmi, (0, 0, GRAD_AT["mi"], 0))


def _out_proj(a, w, x, alpha, next_gain, name):
    if next_gain is None:
        return _mm_nn(a, w, out_dtype=F32, res=x, alpha=alpha, name=name), None
    return _mm_nn(a, w, out_dtype=F32, res=x, alpha=alpha, next_gain=next_gain, name=name + "_norm")


def _ffn_forward(x, xn, w, next_gain, carry_gather=None):
    res = _ffn_in(xn, w["wi_t"], name="ffn_in", carry_gather=carry_gather)
    h, pg, pu = res[:3]
    x_new, xn_next = _out_proj(h, w["wo"], x, 0.5, next_gain, "ffn_out")
    out = (x_new, xn_next, dict(x=x, xn=xn, h=h, pg=pg, pu=pu))
    return out if carry_gather is None else out + (_fill_own_slot(res[3], carry_gather),)


def _ffn_backward(dxo, dxo_b, gain, w, saved, gbuf, tag, exchange=False):
    dzg, dzu = _ffn_bwd_mid(dxo_b, w["wo"], saved["pg"], saved["pu"], name="ffn_bwd_mid")
    gbuf = _mm_tn(saved["h"], dxo_b, alpha=0.5, tm=F_HALF, name="ffn_gw_out", into=_into_ffn_out(gbuf, tag + "o"))
    gbuf = _mm_tn(dzg, saved["xn"], tm=F_HALF, name="ffn_gw_in", into=_into_ffn_in(gbuf, tag + "i", 0))
    gbuf = _mm_tn(dzu, saved["xn"], tm=F_HALF, name="ffn_gw_in", into=_into_ffn_in(gbuf, tag + "i", 1))
    res = _dxn_norm_bwd([dzg, dzu], w["wi_t"], saved["x"], gain, dxo, name="ffn_dxn_norm_bwd",
                        carry_sibling=gbuf if exchange else None)
    return res[0], res[1], res[2], gbuf, (res[3] if exchange else None)


def _mixer_forward(x, xn, p, w, nb, seq, ta, next_gain, next_pack=None):
    qkv, rest, fl = _mix_proj(xn, w["wm_t"])
    qa, ka, vm = _fox_prep(fl, p["bf"], qkv, nb, seq)
    if next_pack is None:
        (y_attn, lse), next_gathered = _fox_fwd(qa, ka, vm, nb, seq, ta), None
    else:
        y_attn, lse, next_gathered = _fox_fwd(qa, ka, vm, nb, seq, ta, carry_gather=next_pack)
        next_gathered = _fill_own_slot(next_gathered, next_pack)
    y_pool = _pool_fwd(rest, p["wbd"], p["scale"], seq)
    y_conv = _conv_fwd(rest, p["cw"], seq)
    y = jnp.concatenate([y_attn, y_pool, y_conv], axis=1)
    x_new, xn_next = _out_proj(y, w["wo"], x, 1.0, next_gain, "mix_out")
    return x_new, xn_next, dict(x=x, xn=xn, qa=qa, ka=ka, vm=vm, rest=rest, fl=fl, lse=lse, y=y), next_gathered


def _mixer_backward(dxo, dxo_b, p, w, sv, nb, seq, ta, gbuf, pending=None):
    t = dxo.shape[0]
    dy = _mm_nt(dxo_b, w["wo"], out_dtype=F32, name="mix_dy")
    gbuf = _mm_tn(sv["y"], dxo_b, name="mix_gw_out", into=_into_mix_out(gbuf))
    res = _fox_bwd(sv["qa"], sv["ka"], sv["vm"], sv["y"], dy, sv["lse"], nb, seq, ta, carry_exchange=pending)
    dq, dk, dv, d_rows, d_cols = res[:5]
    landed = None if pending is None else res[5]
    ddh = (d_rows.reshape(nb, N_HEADS, seq) - d_cols.reshape(nb, N_HEADS, seq)).transpose(0, 2, 1)
    ddh = ddh.reshape(t, N_HEADS)
    dfl, dbf = _fox_prep_bwd(jnp.pad(ddh, ((0, 0), (0, LANES - N_HEADS))), sv["fl"], p["bf"], seq)
    dpool, dwbd, dscale = _pool_bwd(sv["rest"], dy, p["wbd"], p["wbd_t"], p["scale"], seq)
    dconv, dcw = _conv_bwd(sv["rest"], dy, p["cw"], seq)
    dproj = jnp.concatenate([dq, dk.astype(BF16), dv.astype(BF16), dpool, dconv, dfl], axis=1)
    gbuf = _put_mix_in(gbuf, _mm_tn(dproj, sv["xn"], tm=D_INP // 3, name="mix_gw_in"))
    dx, dx_b, dg = _dxn_norm_bwd([dproj], w["wm_t"], sv["x"], p["norm"], dxo, name="mix_dxn_norm_bwd")
    return dx, dx_b, dict(norm=dg, bf=dbf, wbd=dwbd, scale=dscale, cw=dcw), gbuf, landed


def _block_diag(wp):
    z = jnp.zeros((POOL_GROUP, POOL_GROUP), wp.dtype)
    return jnp.concatenate(
        [jnp.concatenate([wp[g] if g == r else z for g in range(4)], axis=1) for r in range(4)], axis=0)


def _row_pad(a, rows):
    a = a.reshape(-1, a.shape[-1])
    return jnp.pad(a, ((0, rows - a.shape[0]), (0, 0)))


def kernel(x, norm_ffn1, w_ffn1_in, w_ffn1_out, norm_mix, w_mix_in, b_forget, w_pool, pool_scale, conv_w, w_mix_out, norm_ffn2, w_ffn2_in, w_ffn2_out, norm_final, loss_target, m_norm_ffn1, m_w_ffn1_in, m_w_ffn1_out, m_norm_mix, m_w_mix_in, m_b_forget, m_w_pool, m_pool_scale, m_conv_w, m_w_mix_out, m_norm_ffn2, m_w_ffn2_in, m_w_ffn2_out, m_norm_final, v_norm_ffn1, v_w_ffn1_in, v_w_ffn1_out, v_norm_mix, v_w_mix_in, v_b_forget, v_w_pool, v_pool_scale, v_conv_w, v_w_mix_out, v_norm_ffn2, v_w_ffn2_in, v_w_ffn2_out, v_norm_final):
    nb, seq, d = x.shape
    depth = norm_ffn1.shape[0]
    t = nb * seq
    ta = _pick(seq, (ATT_TILE, 128))
    my_id = 4 * lax.axis_index("x") + 2 * lax.axis_index("y") + lax.axis_index("c")
    cshard = conv_w.shape[-1]

    shards = (w_ffn1_in, w_ffn1_out, w_mix_in, w_mix_out, w_ffn2_in, w_ffn2_out)
    pack0 = _pack_shards(shards, 0, BF16)
    wg_head, cw_g = _comm_call("gather_weights_and_taps", gathers=[
        pack0[:PACK_HEAD], _row_pad(conv_w.reshape(depth * 3, cshard), 16).reshape(4, LANES)])
    cw_all = cw_g.reshape(N_DEV, 16, cshard)[:, :depth * 3].reshape(N_DEV, depth, 3, cshard)
    cw_all = cw_all.transpose(1, 2, 0, 3).reshape(depth, 3, D_CONV)

    xs = x.reshape(t, d)
    xn = _rmsnorm_fwd(xs, norm_ffn1[0][None], name="first_norm")
    saved = []
    for l in range(depth):
        wbd = _block_diag(w_pool[l])
        p = dict(norm=norm_mix[l][None], bf=jnp.pad(b_forget[l], (0, LANES - N_HEADS))[None],
                 wbd=wbd.astype(BF16), wbd_t=wbd.T.astype(BF16), scale=pool_scale[l][None],
                 cw=_row_pad(cw_all[l], 8))
        w = dict(f1=_ffn_weights(wg_head, 0))
        if l == 0:
            xs, xn, s1, wg_tail = _ffn_forward(xs, xn, w["f1"], norm_mix[l][None], carry_gather=pack0[PACK_HEAD:])
        else:
            xs, xn, s1 = _ffn_forward(xs, xn, w["f1"], norm_mix[l][None])
        w["mix"], w["f2"] = _tail_weights(wg_tail)
        next_pack = _pack_shards(shards, l + 1, BF16) if l + 1 < depth else None
        xs, xn, sm, wg = _mixer_forward(xs, xn, p, w["mix"], nb, seq, ta, norm_ffn2[l][None], next_pack)
        if wg is not None:
            wg_head, wg_tail = wg[:, :PACK_HEAD], wg[:, PACK_HEAD:]
        xs, xn, s2 = _ffn_forward(xs, xn, w["f2"], norm_ffn1[l + 1][None] if l + 1 < depth else None)
        saved.append((w, p, s1, sm, s2))

    dx, dx_b, g_norm_final, loss_part = _final_loss_bwd(xs, norm_final[None], loss_target.reshape(t, d))
    layer_g = [None] * depth
    small = [None] * depth
    chip_sums = None
    for l in reversed(range(depth)):
        w, p, s1, sm, s2 = saved[l]
        dx, dx_b, dg2, gbuf, _ = _ffn_backward(dx, dx_b, norm_ffn2[l][None], w["f2"], s2, None, "f2")
        dx, dx_b, gm, gbuf, landed = _mixer_backward(dx, dx_b, p, w["mix"], sm, nb, seq, ta, gbuf, chip_sums)
        if chip_sums is not None:
            layer_g[l + 1] = _sum_landed(chip_sums, landed)
        dx, dx_b, dg1, gbuf, theirs = _ffn_backward(dx, dx_b, norm_ffn1[l][None], w["f1"], s1, gbuf, "f1", exchange=True)
        chip_sums = _add_sibling(gbuf, theirs)
        small[l] = dict(n1=dg1, nm=gm["norm"], n2=dg2, bf=gm["bf"], wbd=gm["wbd"], scale=gm["scale"], cw=gm["cw"])
    grad_x = dx.reshape(nb, seq, d)

    def tile8(a):
        return jnp.pad(a, ((0, 8 - a.shape[0]), (0, D_MODEL - a.shape[1])))

    rows = []
    for l in range(depth):
        s = small[l]
        wp_rows = jnp.stack([s["wbd"][POOL_GROUP * g:POOL_GROUP * (g + 1), POOL_GROUP * g:POOL_GROUP * (g + 1)]
                             for g in range(4)]).reshape(16, D_MODEL)
        rows += [tile8(s["n1"]), tile8(s["nm"]), tile8(s["n2"]), tile8(s["bf"]), tile8(s["scale"]), tile8(s["cw"]),
                 wp_rows]
    rows += [tile8(g_norm_final), tile8(loss_part)]
    small_gathered, landed = _comm_call("exchange_grads_chips_gather_small", gathers=[jnp.concatenate(rows, axis=0)],
                                        chips=chip_sums)
    layer_g[0] = _sum_landed(chip_sums, landed)

    pieces = {}
    for nm, n in (("f1i", FFN_ROWS), ("f1o", OUT_ROWS), ("mi", MIX_ROWS), ("mo", MO_ROWS), ("f2i", FFN_ROWS),
                  ("f2o", OUT_ROWS)):
        pieces[nm] = jnp.stack([g[GRAD_AT[nm]:GRAD_AT[nm] + n] for g in layer_g])
    g_sharded = dict(
        w_ffn1_in=pieces["f1i"].transpose(0, 2, 1), w_ffn1_out=pieces["f1o"],
        w_mix_in=pieces["mi"].transpose(0, 2, 1), w_mix_out=pieces["mo"],
        w_ffn2_in=pieces["f2i"].transpose(0, 2, 1), w_ffn2_out=pieces["f2o"])

    per_layer = 6 * 8 + 16
    small_sum = _sum_slots(small_gathered, name="sum_small_grads")
    lay = small_sum[:depth * per_layer].reshape(depth, per_layer, D_MODEL)
    g_small = dict(
        norm_ffn1=lay[:, 0], norm_mix=lay[:, 8], norm_ffn2=lay[:, 16], b_forget=lay[:, 24, :N_HEADS],
        pool_scale=lay[:, 32, :D_POOL],
        conv_w=lax.dynamic_slice_in_dim(lay[:, 40:43, :D_CONV], my_id * cshard, cshard, axis=2),
        w_pool=lay[:, 48:64].reshape(depth, 4, POOL_GROUP, POOL_GROUP),
        norm_final=small_sum[depth * per_layer])
    loss = small_sum[depth * per_layer + 8, 0]

    given = dict(norm_ffn1=(norm_ffn1, m_norm_ffn1, v_norm_ffn1), w_ffn1_in=(w_ffn1_in, m_w_ffn1_in, v_w_ffn1_in),
                 w_ffn1_out=(w_ffn1_out, m_w_ffn1_out, v_w_ffn1_out), norm_mix=(norm_mix, m_norm_mix, v_norm_mix),
                 w_mix_in=(w_mix_in, m_w_mix_in, v_w_mix_in), b_forget=(b_forget, m_b_forget, v_b_forget),
                 w_pool=(w_pool, m_w_pool, v_w_pool), pool_scale=(pool_scale, m_pool_scale, v_pool_scale),
                 conv_w=(conv_w, m_conv_w, v_conv_w), w_mix_out=(w_mix_out, m_w_mix_out, v_w_mix_out),
                 norm_ffn2=(norm_ffn2, m_norm_ffn2, v_norm_ffn2), w_ffn2_in=(w_ffn2_in, m_w_ffn2_in, v_w_ffn2_in),
                 w_ffn2_out=(w_ffn2_out, m_w_ffn2_out, v_w_ffn2_out), norm_final=(norm_final, m_norm_final, v_norm_final))
    names = list(given)
    grads, deltas, new_m, new_v = {}, {}, {}, {}
    for nm in names:
        wv, mv, vv = given[nm]
        gv = (g_sharded[nm] if nm in g_sharded else g_small[nm]).reshape(wv.shape)
        shape2 = (-1, wv.shape[-1]) if wv.ndim > 1 else (1, wv.shape[0])
        dl, mn, vn = _adamw(wv.reshape(shape2), gv.reshape(shape2), mv.reshape(shape2), vv.reshape(shape2),
                            name="adamw_" + nm)
        grads[nm], deltas[nm], new_m[nm], new_v[nm] = gv, dl.reshape(wv.shape), mn.reshape(wv.shape), vn.reshape(wv.shape)
    return (loss, grad_x, *[grads[n] for n in names], *[deltas[n] for n in names],
            *[new_m[n] for n in names], *[new_v[n] for n in names])
```
